```python
import math
import jax, jax.numpy as jnp
from jax import lax
import numpy as np

D_MODEL = 1024
BATCH = 16
SEQ = 256
DEPTH = 4
DEC_BATCH = 2
DEC_SEQ = 4096
PAST_LEN = 256

GRID_W = 64
N_EVEN = (DEPTH + 1) // 2
N_ODD = DEPTH // 2
NORM_EPS = 1e-6
NEG_BIG = -1e30
HEAD_DIM = 64
N_Q_HEADS = D_MODEL // 128
N_KV_HEADS = N_Q_HEADS // 4
GQA_GROUP = N_Q_HEADS // N_KV_HEADS
WINDOW = 128
ATTN_BLOCK = 128
ROPE_THETA = 10000.0
SSD_HEADS = D_MODEL // 128
SSD_HEAD_DIM = 64
SSD_D_INNER = SSD_HEADS * SSD_HEAD_DIM
SSD_GROUPS = 2
D_STATE = 64
CONV_K = 5
SSD_CHUNK = 64
SSD_CONV_DIM = SSD_D_INNER + 2 * SSD_GROUPS * D_STATE
ATTN_Q_DIM = N_Q_HEADS * HEAD_DIM
ATTN_KV_DIM = N_KV_HEADS * HEAD_DIM
EVEN_SPLITS = (ATTN_Q_DIM,
               ATTN_Q_DIM + ATTN_KV_DIM,
               ATTN_Q_DIM + 2 * ATTN_KV_DIM,
               ATTN_Q_DIM + 2 * ATTN_KV_DIM + SSD_D_INNER,
               ATTN_Q_DIM + 2 * ATTN_KV_DIM + SSD_D_INNER + SSD_CONV_DIM)
EVEN_IN = EVEN_SPLITS[-1] + 2 * SSD_HEADS
EVEN_MIX = ATTN_Q_DIM + SSD_D_INNER
HGRN_HEADS = D_MODEL // 128
HGRN_DK = 128
HGRN_DV = D_MODEL // HGRN_HEADS
HGRN_KW = HGRN_HEADS * HGRN_DK
HGRN_VW = HGRN_HEADS * HGRN_DV
HGRN_SPLITS = (HGRN_KW, 2 * HGRN_KW, 3 * HGRN_KW, 3 * HGRN_KW + HGRN_VW)
HGRN_IN = 3 * HGRN_KW + 2 * HGRN_VW
HGRN_CHUNK = 32
PEER_HEADS = 8
PEER_NKEYS = 128
PEER_EXPERTS = PEER_NKEYS * PEER_NKEYS
PEER_TOPK = 16
PEER_DKEY = 128
PEER_BLOCK = 128

kernel_name = 'hybrid_diffusion_swa_ssd_hgrn2_peer_step'

F32 = jnp.float32


def rms_norm(x, w):
    xf = x.astype(F32)
    y = xf * lax.rsqrt(jnp.mean(xf * xf, axis=-1, keepdims=True) + NORM_EPS)
    return (y * w.astype(F32)).astype(x.dtype)


def modulation(cond, w_ada, b_ada):
    m = jax.nn.silu(cond) @ w_ada + b_ada
    return jnp.split(m[:, None, :], 6, axis=-1)


def depthwise_conv(x, w, b):
    c = x.shape[-1]
    pad = (CONV_K - 1) // 2
    y = lax.conv_general_dilated(x, w[:, None, :].astype(x.dtype), window_strides=(1,),
                                 padding=((pad, pad),), dimension_numbers=('NWC', 'WIO', 'NWC'),
                                 feature_group_count=c)
    return y + b.astype(x.dtype)


def axial_rope(x):
    t = x.shape[1]
    n_rows = t // GRID_W
    rows = jnp.repeat(jnp.arange(n_rows, dtype=F32), GRID_W)
    cols = jnp.tile(jnp.arange(GRID_W, dtype=F32), n_rows)
    axis_dim = HEAD_DIM // 2
    inv_freq = ROPE_THETA ** (-jnp.arange(0, axis_dim, 2, dtype=F32) / axis_dim)

    def rotate(xa, pos):
        ang = pos[:, None] * inv_freq[None, :]
        cos = jnp.cos(ang)[None, :, None, :]
        sin = jnp.sin(ang)[None, :, None, :]
        x1, x2 = jnp.split(xa.astype(F32), 2, axis=-1)
        return jnp.concatenate([x1 * cos - x2 * sin, x2 * cos + x1 * sin], axis=-1)

    out = jnp.concatenate([rotate(x[..., :axis_dim], rows), rotate(x[..., axis_dim:], cols)], axis=-1)
    return out.astype(x.dtype)


def chunked_gated_scan(q, k, v, log_g, s0, chunk):
    b, t, h, _ = q.shape
    n = t // chunk
    out_dtype = v.dtype

    def to_chunks(a):
        return a.astype(F32).reshape(b, n, chunk, h, a.shape[-1]).transpose(1, 0, 3, 2, 4)

    qc, kc, vc, gc = (to_chunks(a) for a in (q, k, v, log_g))
    causal = jnp.tril(jnp.ones((chunk, chunk), dtype=bool))
    scalar = log_g.shape[-1] == 1

    def step(s, xs):
        qi, ki, vi, gi = xs
        cum = jnp.cumsum(gi, axis=2)
        last = cum[:, :, -1:, :]
        if scalar:
            seg = cum[..., 0][:, :, :, None] - cum[..., 0][:, :, None, :]
            decay = jnp.exp(jnp.where(causal, seg, NEG_BIG))
            scores = jnp.einsum('bhtk,bhsk->bhts', qi, ki) * decay
        else:
            seg = cum[:, :, :, None, :] - cum[:, :, None, :, :]
            decay = jnp.exp(jnp.where(causal[..., None], seg, NEG_BIG))
            scores = jnp.einsum('bhtk,bhsk,bhtsk->bhts', qi, ki, decay)
        o = (jnp.einsum('bhts,bhsv->bhtv', scores, vi)
             + jnp.einsum('bhtk,bhkv->bhtv', qi * jnp.exp(cum), s))
        k_tail = ki * jnp.exp(last - cum)
        s_new = jnp.exp(last)[:, :, 0, :, None] * s + jnp.einsum('bhsk,bhsv->bhkv', k_tail, vi)
        return s_new, o

    s_fin, oc = lax.scan(step, s0.astype(F32), (qc, kc, vc, gc))
    o = oc.transpose(1, 0, 3, 2, 4).reshape(b, t, h, -1)
    return o.astype(out_dtype), s_fin


def reverse_gated_scan(q, k, v, log_g, s0, chunk):
    o, s = chunked_gated_scan(*(jnp.flip(a, axis=1) for a in (q, k, v, log_g)), s0, chunk)
    return jnp.flip(o, axis=1), s


def context_attention(q, k, v, sink):
    b, l = q.shape[:2]
    qg = q.reshape(b, l, N_KV_HEADS, GQA_GROUP, HEAD_DIM)
    s = jnp.einsum('bqkgd,bckd->bkgqc', qg, k).astype(F32) * HEAD_DIM ** -0.5
    s_sink = jnp.broadcast_to(sink.astype(F32).reshape(1, N_KV_HEADS, GQA_GROUP, 1, 1),
                              (b, N_KV_HEADS, GQA_GROUP, l, 1))
    p = jax.nn.softmax(jnp.concatenate([s_sink, s], axis=-1), axis=-1)[..., 1:]
    o = jnp.einsum('bkgqc,bckd->bqkgd', p.astype(v.dtype), v)
    return o.reshape(b, l, N_Q_HEADS * HEAD_DIM)


def latent_attention(q, k, v, k_ctx, v_ctx, sink):
    b, t = q.shape[:2]
    nb = t // ATTN_BLOCK
    lc = k_ctx.shape[1]
    qb = q.reshape(b, nb, ATTN_BLOCK, N_KV_HEADS, GQA_GROUP, HEAD_DIM)

    def band(a):
        ap = jnp.pad(a, ((0, 0), (ATTN_BLOCK, ATTN_BLOCK), (0, 0), (0, 0)))
        ap = ap.reshape(b, nb + 2, ATTN_BLOCK, N_KV_HEADS, HEAD_DIM)
        return jnp.concatenate([ap[:, :-2], ap[:, 1:-1], ap[:, 2:]], axis=2)

    kb, vb = band(k), band(v)
    blk = jnp.arange(nb)[:, None, None] * ATTN_BLOCK
    qpos = blk + jnp.arange(ATTN_BLOCK)[None, :, None]
    kpos = blk - ATTN_BLOCK + jnp.arange(3 * ATTN_BLOCK)[None, None, :]
    valid = (jnp.abs(qpos - kpos) <= WINDOW) & (kpos >= 0) & (kpos < t)
    scale = HEAD_DIM ** -0.5
    s_loc = jnp.einsum('bnqkgd,bnrkd->bnkgqr', qb, kb).astype(F32) * scale
    s_loc = jnp.where(valid[None, :, None, None], s_loc, NEG_BIG)
    s_ctx = jnp.einsum('bnqkgd,bckd->bnkgqc', qb, k_ctx).astype(F32) * scale
    s_sink = jnp.broadcast_to(sink.astype(F32).reshape(1, 1, N_KV_HEADS, GQA_GROUP, 1, 1),
                              (b, nb, N_KV_HEADS, GQA_GROUP, ATTN_BLOCK, 1))
    p = jax.nn.softmax(jnp.concatenate([s_sink, s_ctx, s_loc], axis=-1), axis=-1)
    p_ctx = p[..., 1:1 + lc].astype(v.dtype)
    p_loc = p[..., 1 + lc:].astype(v.dtype)
    o = (jnp.einsum('bnkgqc,bckd->bnqkgd', p_ctx, v_ctx.astype(v.dtype))
         + jnp.einsum('bnkgqr,bnrkd->bnqkgd', p_loc, vb))
    return o.reshape(b, t, N_Q_HEADS * HEAD_DIM)


def even_mixer(h, P, j, ctx):
    b, t, _ = h.shape
    q, k, v, z, xbc, dt = jnp.split(h @ P['e_w_in'][j], EVEN_SPLITS, axis=-1)
    q = rms_norm(q.reshape(b, t, N_Q_HEADS, HEAD_DIM), P['e_q_norm'][j])
    k = rms_norm(k.reshape(b, t, N_KV_HEADS, HEAD_DIM), P['e_k_norm'][j])
    v = v.reshape(b, t, N_KV_HEADS, HEAD_DIM)
    xbc = jax.nn.silu(depthwise_conv(xbc, P['e_conv_w'][j], P['e_conv_b'][j]))
    xs, bm, cm = jnp.split(xbc, (SSD_D_INNER, SSD_D_INNER + SSD_GROUPS * D_STATE), axis=-1)
    xs = xs.reshape(b, t, SSD_HEADS, SSD_HEAD_DIM)
    rep = SSD_HEADS // SSD_GROUPS
    bm = jnp.repeat(bm.reshape(b, t, SSD_GROUPS, D_STATE), rep, axis=2)
    cm = jnp.repeat(cm.reshape(b, t, SSD_GROUPS, D_STATE), rep, axis=2)
    dt = jax.nn.softplus(dt.astype(F32).reshape(b, t, 2, SSD_HEADS) + P['e_dt_bias'][j].astype(F32))
    log_a = dt * -jnp.exp(P['e_a_log'][j].astype(F32))
    if ctx is None:
        s0 = jnp.zeros((b, 2, SSD_HEADS, D_STATE, SSD_HEAD_DIM), F32)
    else:
        s0 = ctx[2]
    y_f, s_f = chunked_gated_scan(cm, bm * dt[:, :, 0, :, None], xs, log_a[:, :, 0, :, None], s0[:, 0], SSD_CHUNK)
    y_b, s_b = reverse_gated_scan(cm, bm * dt[:, :, 1, :, None], xs, log_a[:, :, 1, :, None], s0[:, 1], SSD_CHUNK)
    y = y_f + y_b + P['e_d_skip'][j][:, None].astype(xs.dtype) * xs
    y = rms_norm(y.reshape(b, t, SSD_D_INNER) * jax.nn.silu(z), P['e_ssd_norm'][j])
    if ctx is None:
        o_attn = context_attention(q, k, v, P['e_sink'][j])
        new = (k, v, jnp.stack([s_f, s_b], axis=1))
    else:
        o_attn = latent_attention(axial_rope(q), axial_rope(k), v, ctx[0], ctx[1], P['e_sink'][j])
        new = None
    mix = jnp.concatenate([o_attn, y.astype(o_attn.dtype)], axis=-1)
    return mix @ P['e_w_out'][j], new


def hgrn_lower_bounds(lb_param):
    sm = jax.nn.softmax(lb_param.astype(F32), axis=0)
    return jnp.cumsum(sm, axis=0) - sm[:1]


def odd_mixer(h, P, j, lb, state0):
    b, t, _ = h.shape
    q, f_fw, f_bw, i, g = jnp.split(h @ P['o_w_in'][j], HGRN_SPLITS, axis=-1)
    q = (jax.nn.silu(q) * HGRN_DK ** -0.5).reshape(b, t, HGRN_HEADS, HGRN_DK)
    i = i.reshape(b, t, HGRN_HEADS, HGRN_DV)

    def gate(f, lb_d):
        f = f.astype(F32).reshape(b, t, HGRN_HEADS, HGRN_DK)
        lb_d = lb_d.reshape(HGRN_HEADS, HGRN_DK)
        forget = lb_d + (1.0 - lb_d) * jax.nn.sigmoid(f)
        return (1.0 - lb_d) * jax.nn.sigmoid(-f), jnp.log(forget)

    k_fw, lf_fw = gate(f_fw, lb[j, 0])
    k_bw, lf_bw = gate(f_bw, lb[j, 1])
    if state0 is None:
        state0 = jnp.zeros((b, 2, HGRN_HEADS, HGRN_DK, HGRN_DV), F32)
    o_fw, s_fw = chunked_gated_scan(q, k_fw, i, lf_fw, state0[:, 0], HGRN_CHUNK)
    o_bw, s_bw = reverse_gated_scan(q, k_bw, i, lf_bw, state0[:, 1], HGRN_CHUNK)
    g = g.reshape(b, t, HGRN_HEADS, HGRN_DV)
    o = rms_norm(o_fw + o_bw, P['o_g_norm'][j]) * jax.nn.silu(g)
    return o.reshape(b, t, HGRN_VW) @ P['o_w_out'][j], jnp.stack([s_fw, s_bw], axis=1)


def peer(h, P, l):
    b, t, d = h.shape
    m = b * t
    x = h.reshape(m, d)
    q = (x @ P['p_w_q'][l]).reshape(m, PEER_HEADS, 2, PEER_DKEY)
    s = jnp.einsum('mhpd,hpnd->mhpn', q, P['p_sub_keys'][l]).astype(F32)
    top_s, top_i = lax.top_k(s, PEER_TOPK)
    n_cand = PEER_TOPK * PEER_TOPK
    cand_s = (top_s[:, :, 0, :, None] + top_s[:, :, 1, None, :]).reshape(m, PEER_HEADS, n_cand)
    cand_e = (top_i[:, :, 0, :, None] * PEER_NKEYS + top_i[:, :, 1, None, :]).reshape(m, PEER_HEADS, n_cand)
    best_s, best_pos = lax.top_k(cand_s, PEER_TOPK)
    expert = jnp.take_along_axis(cand_e, best_pos, axis=-1)
    gate = jax.nn.softmax(best_s, axis=-1)
    u, v = P['p_u'][l], P['p_v'][l]

    def block(args):
        xb, eb, gb = args
        act = jax.nn.gelu(jnp.einsum('pd,phkd->phk', xb, u[eb]).astype(F32)) * gb
        return jnp.einsum('phk,phkd->pd', act.astype(xb.dtype), v[eb])

    nblk = m // PEER_BLOCK
    out = lax.map(block, (x.reshape(nblk, PEER_BLOCK, d),
                          expert.reshape(nblk, PEER_BLOCK, PEER_HEADS, PEER_TOPK),
                          gate.reshape(nblk, PEER_BLOCK, PEER_HEADS, PEER_TOPK)))
    return out.reshape(b, t, d)


def run_trunk(x, cond, P, cache):
    lb = hgrn_lower_bounds(P['o_lb'])
    ks, vs, ssd_states, hgrn_states = [], [], [], []
    for l in range(DEPTH):
        j = l // 2
        sh1, sc1, g1, sh2, sc2, g2 = modulation(cond, P['w_ada'][l], P['b_ada'][l])
        h = rms_norm(x, P['norm_mix'][l]) * (1 + sc1) + sh1
        if l % 2 == 0:
            ctx = None if cache is None else (cache[0][:, j], cache[1][:, j], cache[2][:, j])
            out, new = even_mixer(h, P, j, ctx)
            if cache is None:
                ks.append(new[0])
                vs.append(new[1])
                ssd_states.append(new[2])
        else:
            s0 = None if cache is None else cache[3][:, j]
            out, s_new = odd_mixer(h, P, j, lb, s0)
            if cache is None:
                hgrn_states.append(s_new)
        x = x + g1 * out
        h = rms_norm(x, P['norm_ffn'][l]) * (1 + sc2) + sh2
        x = x + g2 * peer(h, P, l)
    if cache is not None:
        return x, None
    return x, (jnp.stack(ks, axis=1), jnp.stack(vs, axis=1),
               jnp.stack(ssd_states, axis=1), jnp.stack(hgrn_states, axis=1))


def setup_inputs(seed: int = 0) -> dict:
    key = jax.random.key(seed)
    keys = iter(jax.random.split(key, 40))

    def nrm(shape, scale):
        return jax.random.normal(next(keys), shape, F32) * scale

    def gain(shape):
        return 1.0 + nrm(shape, 0.02)

    dt0 = jnp.exp(jax.random.uniform(next(keys), (N_EVEN, 2, SSD_HEADS), F32,
                                     math.log(1e-3), math.log(1e-1)))
    a_init = jax.random.uniform(next(keys), (N_EVEN, 2, SSD_HEADS), F32, 1.0, 16.0)
    return {
        'x_prompt': nrm((BATCH, SEQ, D_MODEL), 1.0),
        'x_sample': nrm((DEC_BATCH, DEC_SEQ, D_MODEL), 1.0),
        'cache_k': nrm((DEC_BATCH, N_EVEN, PAST_LEN, N_KV_HEADS, HEAD_DIM), 1.0),
        'cache_v': nrm((DEC_BATCH, N_EVEN, PAST_LEN, N_KV_HEADS, HEAD_DIM), 1.0),
        'state_ssd': nrm((DEC_BATCH, N_EVEN, 2, SSD_HEADS, D_STATE, SSD_HEAD_DIM), 0.5),
        'state_hgrn': nrm((DEC_BATCH, N_ODD, 2, HGRN_HEADS, HGRN_DK, HGRN_DV), 0.5),
        'c': nrm((DEC_BATCH, D_MODEL), 1.0),
        'c_ctx': nrm((D_MODEL,), 1.0),
        'w_ada': nrm((DEPTH, D_MODEL, 6 * D_MODEL), 0.5 * D_MODEL ** -0.5),
        'b_ada': nrm((DEPTH, 6 * D_MODEL), 0.02),
        'norm_mix': gain((DEPTH, D_MODEL)),
        'norm_ffn': gain((DEPTH, D_MODEL)),
        'e_w_in': nrm((N_EVEN, D_MODEL, EVEN_IN), D_MODEL ** -0.5),
        'e_q_norm': gain((N_EVEN, HEAD_DIM)),
        'e_k_norm': gain((N_EVEN, HEAD_DIM)),
        'e_sink': nrm((N_EVEN, N_Q_HEADS), 0.5),
        'e_conv_w': nrm((N_EVEN, CONV_K, SSD_CONV_DIM), CONV_K ** -0.5),
        'e_conv_b': nrm((N_EVEN, SSD_CONV_DIM), 0.02),
        'e_dt_bias': dt0 + jnp.log(-jnp.expm1(-dt0)),
        'e_a_log': jnp.log(a_init),
        'e_d_skip': gain((N_EVEN, SSD_HEADS)),
        'e_ssd_norm': gain((N_EVEN, SSD_D_INNER)),
        'e_w_out': nrm((N_EVEN, EVEN_MIX, D_MODEL), EVEN_MIX ** -0.5),
        'o_w_in': nrm((N_ODD, D_MODEL, HGRN_IN), D_MODEL ** -0.5),
        'o_lb': nrm((N_ODD, 2, HGRN_KW), 1.0),
        'o_g_norm': gain((N_ODD, HGRN_DV)),
        'o_w_out': nrm((N_ODD, HGRN_VW, D_MODEL), HGRN_VW ** -0.5),
        'p_w_q': nrm((DEPTH, D_MODEL, PEER_HEADS * 2 * PEER_DKEY), D_MODEL ** -0.5),
        'p_sub_keys': nrm((DEPTH, PEER_HEADS, 2, PEER_NKEYS, PEER_DKEY), PEER_DKEY ** -0.5),
        'p_u': nrm((DEPTH, PEER_EXPERTS, D_MODEL), D_MODEL ** -0.5),
        'p_v': nrm((DEPTH, PEER_EXPERTS, D_MODEL), PEER_HEADS ** -0.5),
    }


def reference(x_prompt, x_sample, cache_k, cache_v, state_ssd, state_hgrn, c, c_ctx,
              w_ada, b_ada, norm_mix, norm_ffn,
              e_w_in, e_q_norm, e_k_norm, e_sink, e_conv_w, e_conv_b, e_dt_bias, e_a_log,
              e_d_skip, e_ssd_norm, e_w_out,
              o_w_in, o_lb, o_g_norm, o_w_out,
              p_w_q, p_sub_keys, p_u, p_v):
    P = {
        'w_ada': w_ada, 'b_ada': b_ada, 'norm_mix': norm_mix, 'norm_ffn': norm_ffn,
        'e_w_in': e_w_in, 'e_q_norm': e_q_norm, 'e_k_norm': e_k_norm, 'e_sink': e_sink,
        'e_conv_w': e_conv_w, 'e_conv_b': e_conv_b, 'e_dt_bias': e_dt_bias, 'e_a_log': e_a_log,
        'e_d_skip': e_d_skip, 'e_ssd_norm': e_ssd_norm, 'e_w_out': e_w_out,
        'o_w_in': o_w_in, 'o_lb': o_lb, 'o_g_norm': o_g_norm, 'o_w_out': o_w_out,
        'p_w_q': p_w_q, 'p_sub_keys': p_sub_keys, 'p_u': p_u, 'p_v': p_v,
    }
    y_prompt, new_state = run_trunk(x_prompt, c_ctx[None, :], P, None)
    new_cache_k, new_cache_v, new_state_ssd, new_state_hgrn = new_state
    y_sample, _ = run_trunk(x_sample, c, P, (cache_k, cache_v, state_ssd, state_hgrn))
    return (y_prompt, y_sample, new_cache_k, new_cache_v, new_state_ssd, new_state_hgrn)
```

```python
import functools
import math

import jax
import jax.numpy as jnp
from jax import lax
from jax.experimental import pallas as pl
from jax.experimental.pallas import tpu as pltpu

F32 = jnp.float32
BF16 = jnp.bfloat16
I32 = jnp.int32
HIGHEST = lax.Precision.HIGHEST

NORM_EPS = 1e-6
NEG_BIG = -1e30
LANES = 128
SUBLANES = 8
VMEM_LIMIT = 48 * 1024 * 1024

GRID_W = 64
HEAD_DIM = 64
N_Q_HEADS = 8
N_KV_HEADS = 2
GQA_GROUP = 4
WINDOW = 128
ROPE_THETA = 10000.0
SSD_HEADS = 8
SSD_HEAD_DIM = 64
SSD_GROUPS = 2
D_STATE = 64
CONV_K = 5
HGRN_HEADS = 8
HGRN_DK = 128
HGRN_CHUNK = 32
PEER_HEADS = 8
PEER_NKEYS = 128
PEER_TOPK = 16
PEER_DKEY = 128
PEER_PAIRS = PEER_HEADS * PEER_TOPK

ROW_TILE = 256
PEER_BLOCK = 128


def _cparams(*sem):
    return pltpu.CompilerParams(dimension_semantics=sem, vmem_limit_bytes=VMEM_LIMIT)


def _norm_mod(x, nw, scale, shift):
    ms = jnp.mean(x * x, axis=-1, keepdims=True)
    return (x * lax.rsqrt(ms + NORM_EPS)) * nw * (1.0 + scale) + shift


def _mod_kernel(c_ref, w_ref, b_ref, o_ref):
    c = c_ref[...]
    s = c * jax.nn.sigmoid(c)
    o_ref[0] = jnp.dot(s, w_ref[0], precision=HIGHEST, preferred_element_type=F32) + b_ref[0]


def _modulation(cond_rows, w_ada, b_ada):
    depth, d, n = w_ada.shape
    rows = cond_rows.shape[0]
    return pl.pallas_call(
        _mod_kernel,
        grid=(depth, n // d),
        in_specs=[pl.BlockSpec((rows, d), lambda l, j: (0, 0)),
                  pl.BlockSpec((1, d, d), lambda l, j: (l, 0, j)),
                  pl.BlockSpec((1, 1, d), lambda l, j: (l, 0, j))],
        out_specs=pl.BlockSpec((1, rows, d), lambda l, j: (l, 0, j)),
        out_shape=jax.ShapeDtypeStruct((depth, rows, n), F32),
        compiler_params=_cparams("arbitrary", "arbitrary"),
        name="modulation",
    )(cond_rows, w_ada, b_ada.reshape(depth, 1, n))


def _mod_spec(mod_row0, seq, tile, d6):
    row0, per_batch = mod_row0
    return pl.BlockSpec((1, 1, d6), lambda i: (row0 + per_batch * ((i * tile) // seq), 0, 0))


def _inproj_kernel(x_ref, m_ref, nw_ref, w_ref, *o_refs, splits, d):
    m = m_ref[0]
    h = _norm_mod(x_ref[...], nw_ref[...], m[:, d:2 * d], m[:, 0:d]).astype(BF16)
    for o_ref, (a, b) in zip(o_refs, splits):
        o_ref[...] = jnp.dot(h, w_ref[:, a:b], preferred_element_type=F32)


def _inproj(x, mods, mod_row0, seq, nw, w_bf16, splits, name):
    m, d = x.shape
    n = w_bf16.shape[1]
    tile = min(ROW_TILE, seq)
    return pl.pallas_call(
        functools.partial(_inproj_kernel, splits=splits, d=d),
        grid=(m // tile,),
        in_specs=[pl.BlockSpec((tile, d), lambda i: (i, 0)),
                  _mod_spec(mod_row0, seq, tile, mods.shape[-1]),
                  pl.BlockSpec((1, d), lambda i: (0, 0)),
                  pl.BlockSpec((d, n), lambda i: (0, 0))],
        out_specs=[pl.BlockSpec((tile, b - a), lambda i: (i, 0)) for a, b in splits],
        out_shape=[jax.ShapeDtypeStruct((m, b - a), F32) for a, b in splits],
        compiler_params=_cparams("arbitrary"),
        name=name,
    )(x, mods, nw.reshape(1, d), w_bf16)


def _outproj_kernel(mix_ref, x_ref, m_ref, w_ref, o_ref, *, d):
    y = jnp.dot(mix_ref[...].astype(BF16), w_ref[...], preferred_element_type=F32)
    o_ref[...] = x_ref[...] + m_ref[0][:, 2 * d:3 * d] * y


def _outproj(mix, x, mods, mod_row0, seq, w_bf16, name):
    m, d = x.shape
    k = mix.shape[1]
    tile = min(ROW_TILE, seq)
    return pl.pallas_call(
        functools.partial(_outproj_kernel, d=d),
        grid=(m // tile,),
        in_specs=[pl.BlockSpec((tile, k), lambda i: (i, 0)),
                  pl.BlockSpec((tile, d), lambda i: (i, 0)),
                  _mod_spec(mod_row0, seq, tile, mods.shape[-1]),
                  pl.BlockSpec((k, d), lambda i: (0, 0))],
        out_specs=pl.BlockSpec((tile, d), lambda i: (i, 0)),
        out_shape=jax.ShapeDtypeStruct((m, d), F32),
        compiler_params=_cparams("arbitrary"),
        name=name,
    )(mix, x, mods, w_bf16)


def _topk_over_rows(s, k, payload=None):
    n = s.shape[0]
    iota = lax.broadcasted_iota(I32, s.shape, 0)
    vals, idxs, pays = [], [], []
    for _ in range(k):
        m = jnp.max(s, axis=0, keepdims=True)
        i = jnp.min(jnp.where(s == m, iota, n), axis=0, keepdims=True)
        hit = iota == i
        vals.append(m)
        idxs.append(i)
        if payload is not None:
            pays.append(jnp.max(jnp.where(hit, payload, -1), axis=0, keepdims=True))
        s = jnp.where(hit, -jnp.inf, s)
    out = (jnp.concatenate(vals, axis=0), jnp.concatenate(idxs, axis=0))
    if payload is not None:
        out += (jnp.concatenate(pays, axis=0),)
    return out


def _peer_route_kernel(x_ref, m_ref, nw_ref, wq_ref, keys_ref, h_ref, e_ref, g_ref, *, d):
    m = m_ref[0]
    h = _norm_mod(x_ref[...], nw_ref[...], m[:, 4 * d:5 * d], m[:, 3 * d:4 * d])
    h_ref[...] = h
    hb = h.astype(BF16)
    nt = (((1,), (1,)), ((), ()))
    for head in range(PEER_HEADS):
        tops = []
        for half in range(2):
            c0 = (head * 2 + half) * PEER_DKEY
            q = jnp.dot(hb, wq_ref[:, c0:c0 + PEER_DKEY], preferred_element_type=F32)
            s = lax.dot_general(keys_ref[head * 2 + half], q.astype(BF16), nt,
                                preferred_element_type=F32)
            tops.append(_topk_over_rows(s, PEER_TOPK))
        (s0, i0), (s1, i1) = tops
        cand_s = jnp.concatenate([s0[a:a + 1] + s1 for a in range(PEER_TOPK)], axis=0)
        cand_e = jnp.concatenate([i0[a:a + 1] * PEER_NKEYS + i1 for a in range(PEER_TOPK)], axis=0)
        best_s, _, best_e = _topk_over_rows(cand_s, PEER_TOPK, payload=cand_e)
        p = jnp.exp(best_s - best_s[0:1])
        r0 = head * PEER_TOPK
        e_ref[r0:r0 + PEER_TOPK, :] = best_e
        g_ref[r0:r0 + PEER_TOPK, :] = p / jnp.sum(p, axis=0, keepdims=True)


def _peer_route(x, mods, mod_row0, seq, nw, wq_bf16, keys_bf16):
    m, d = x.shape
    tile = min(ROW_TILE, seq)
    nq = wq_bf16.shape[1]
    return pl.pallas_call(
        functools.partial(_peer_route_kernel, d=d),
        grid=(m // tile,),
        in_specs=[pl.BlockSpec((tile, d), lambda i: (i, 0)),
                  _mod_spec(mod_row0, seq, tile, mods.shape[-1]),
                  pl.BlockSpec((1, d), lambda i: (0, 0)),
                  pl.BlockSpec((d, nq), lambda i: (0, 0)),
                  pl.BlockSpec(keys_bf16.shape, lambda i: (0, 0, 0))],
        out_specs=[pl.BlockSpec((tile, d), lambda i: (i, 0)),
                   pl.BlockSpec((PEER_PAIRS, tile), lambda i: (0, i)),
                   pl.BlockSpec((PEER_PAIRS, tile), lambda i: (0, i))],
        out_shape=[jax.ShapeDtypeStruct((m, d), F32),
                   jax.ShapeDtypeStruct((PEER_PAIRS, m), I32),
                   jax.ShapeDtypeStruct((PEER_PAIRS, m), F32)],
        compiler_params=_cparams("arbitrary"),
        name="peer_route",
    )(x, mods, nw.reshape(1, d), wq_bf16, keys_bf16)


def _peer_gather_kernel(idx_hbm, h_ref, g_ref, x_ref, m_ref, u_hbm, v_hbm, o_ref,
                        idx_smem, ubuf, vbuf, sem_idx, sem_u, sem_v, *, layer, d):
    blk = pl.program_id(0)
    n_groups = PEER_BLOCK // SUBLANES

    cp = pltpu.make_async_copy(idx_hbm.at[blk], idx_smem, sem_idx)
    cp.start()
    cp.wait()

    def row_copies(tok, pair, slot):
        e = idx_smem[tok, pair]
        return (pltpu.make_async_copy(u_hbm.at[layer, pl.ds(e, 1)], ubuf.at[slot, pl.ds(pair, 1)],
                                      sem_u.at[slot]),
                pltpu.make_async_copy(v_hbm.at[layer, pl.ds(e, 1)], vbuf.at[slot, pl.ds(pair, 1)],
                                      sem_v.at[slot]))

    def issue(tok, slot):
        def body(pair, carry):
            cu, cv = row_copies(tok, pair, slot)
            cu.start()
            cv.start()
            return carry
        lax.fori_loop(0, PEER_PAIRS, body, 0, unroll=8)

    def wait(slot):
        pltpu.make_async_copy(u_hbm.at[layer, pl.ds(0, PEER_PAIRS)], ubuf.at[slot], sem_u.at[slot]).wait()
        pltpu.make_async_copy(v_hbm.at[layer, pl.ds(0, PEER_PAIRS)], vbuf.at[slot], sem_v.at[slot]).wait()

    gate2 = m_ref[0][:, 5 * d:6 * d]
    lane = lax.broadcasted_iota(I32, (PEER_PAIRS, PEER_BLOCK), 1)

    issue(0, 0)

    def group(grp, carry):
        base = pl.multiple_of(grp * SUBLANES, SUBLANES)
        h8 = h_ref[pl.ds(base, SUBLANES), :]
        rows = []
        for r in range(SUBLANES):
            tok = base + r
            slot = r % 2
            if r < SUBLANES - 1:
                issue(tok + 1, 1 - slot)
            else:
                @pl.when(grp < n_groups - 1)
                def _():
                    issue(tok + 1, 1 - slot)
            wait(slot)
            act = jnp.sum(ubuf[slot] * h8[r:r + 1, :], axis=1, keepdims=True)
            gate = jnp.sum(jnp.where(lane == tok, g_ref[...], 0.0), axis=1, keepdims=True)
            w = jax.nn.gelu(act) * gate
            rows.append(jnp.sum(vbuf[slot] * w, axis=0, keepdims=True))
        out8 = jnp.concatenate(rows, axis=0)
        o_ref[pl.ds(base, SUBLANES), :] = x_ref[pl.ds(base, SUBLANES), :] + gate2 * out8
        return carry

    lax.fori_loop(0, n_groups, group, 0)


def _peer_gather(idx, h, gates, x, mods, mod_row0, seq, p_u, p_v, layer):
    m, d = x.shape
    nblk = m // PEER_BLOCK
    idx3 = idx.T.reshape(nblk, PEER_BLOCK, PEER_PAIRS)
    return pl.pallas_call(
        functools.partial(_peer_gather_kernel, layer=layer, d=d),
        grid=(nblk,),
        in_specs=[pl.BlockSpec(memory_space=pl.ANY),
                  pl.BlockSpec((PEER_BLOCK, d), lambda i: (i, 0)),
                  pl.BlockSpec((PEER_PAIRS, PEER_BLOCK), lambda i: (0, i)),
                  pl.BlockSpec((PEER_BLOCK, d), lambda i: (i, 0)),
                  _mod_spec(mod_row0, seq, PEER_BLOCK, mods.shape[-1]),
                  pl.BlockSpec(memory_space=pl.ANY),
                  pl.BlockSpec(memory_space=pl.ANY)],
        out_specs=pl.BlockSpec((PEER_BLOCK, d), lambda i: (i, 0)),
        out_shape=jax.ShapeDtypeStruct((m, d), F32),
        scratch_shapes=[pltpu.SMEM((PEER_BLOCK, PEER_PAIRS), I32),
                        pltpu.VMEM((2, PEER_PAIRS, d), F32),
                        pltpu.VMEM((2, PEER_PAIRS, d), F32),
                        pltpu.SemaphoreType.DMA,
                        pltpu.SemaphoreType.DMA((2,)),
                        pltpu.SemaphoreType.DMA((2,))],
        compiler_params=_cparams("arbitrary"),
        name="peer_gather",
    )(idx3, h, gates, x, mods, p_u, p_v)


def _peer(x, mods, mod_row0, seq, layer, nw, wq_bf16, keys_bf16, p_u, p_v):
    h, idx, gates = _peer_route(x, mods, mod_row0, seq, nw, wq_bf16, keys_bf16)
    return _peer_gather(idx, h, gates, x, mods, mod_row0, seq, p_u, p_v, layer)


def _head_mean_square(x):
    n = x.shape[1]
    r = lax.broadcasted_iota(I32, (n, n), 0) // HEAD_DIM
    c = lax.broadcasted_iota(I32, (n, n), 1) // HEAD_DIM
    seg = jnp.where(r == c, 1.0 / HEAD_DIM, 0.0).astype(F32)
    return jnp.dot(x * x, seg, precision=HIGHEST, preferred_element_type=F32)


def _swap_rot_halves(x):
    n = x.shape[1]
    quarter = HEAD_DIM // 4
    lane = lax.broadcasted_iota(I32, x.shape, 1)
    lo = (lane % (2 * quarter)) < quarter
    return jnp.where(lo, pltpu.roll(x, n - quarter, axis=1), pltpu.roll(x, quarter, axis=1))


def _qkprep_kernel(q_ref, k_ref, qw_ref, kw_ref, *rest, rope):
    if rope:
        cos_ref, sin_ref, qo_ref, ko_ref = rest
    else:
        qo_ref, ko_ref = rest
    q = q_ref[...]
    k = k_ref[...]
    q = q * lax.rsqrt(_head_mean_square(q) + NORM_EPS) * qw_ref[...]
    k = k * lax.rsqrt(_head_mean_square(k) + NORM_EPS) * kw_ref[...]
    if rope:
        cos = cos_ref[...]
        sin = sin_ref[...]
        cq = jnp.concatenate([cos] * (q.shape[1] // LANES), axis=1)
        sq = jnp.concatenate([sin] * (q.shape[1] // LANES), axis=1)
        q = q * cq + _swap_rot_halves(q) * sq
        k = k * cos + _swap_rot_halves(k) * sin
    qo_ref[...] = q
    ko_ref[...] = k


def _rope_tables(seq):
    axis_dim = HEAD_DIM // 2
    inv_freq = ROPE_THETA ** (-jnp.arange(0, axis_dim, 2, dtype=F32) / axis_dim)
    t = jnp.arange(seq)
    pos = jnp.stack([(t // GRID_W).astype(F32), (t % GRID_W).astype(F32)], axis=1)
    lane = jnp.arange(LANES)
    dd = lane % HEAD_DIM
    ang = pos[:, dd // axis_dim] * inv_freq[dd % (axis_dim // 2)][None, :]
    sign = jnp.where((dd % axis_dim) < axis_dim // 2, -1.0, 1.0).astype(F32)
    return jnp.cos(ang), jnp.sin(ang) * sign[None, :]


def _qkprep(q, k, qw, kw, seq, rope):
    m, nq = q.shape
    nk = k.shape[1]
    tile = min(ROW_TILE, seq)
    qw_row = jnp.tile(qw, nq // HEAD_DIM).reshape(1, nq)
    kw_row = jnp.tile(kw, nk // HEAD_DIM).reshape(1, nk)
    in_specs = [pl.BlockSpec((tile, nq), lambda i: (i, 0)),
                pl.BlockSpec((tile, nk), lambda i: (i, 0)),
                pl.BlockSpec((1, nq), lambda i: (0, 0)),
                pl.BlockSpec((1, nk), lambda i: (0, 0))]
    args = [q, k, qw_row, kw_row]
    if rope:
        cos, sin = _rope_tables(seq)
        per_seq = seq // tile
        in_specs += [pl.BlockSpec((tile, LANES), lambda i: (i % per_seq, 0)),
                     pl.BlockSpec((tile, LANES), lambda i: (i % per_seq, 0))]
        args += [cos, sin]
    return pl.pallas_call(
        functools.partial(_qkprep_kernel, rope=rope),
        grid=(m // tile,),
        in_specs=in_specs,
        out_specs=[pl.BlockSpec((tile, nq), lambda i: (i, 0)),
                   pl.BlockSpec((tile, nk), lambda i: (i, 0))],
        out_shape=[jax.ShapeDtypeStruct((m, nq), F32), jax.ShapeDtypeStruct((m, nk), F32)],
        compiler_params=_cparams("arbitrary"),
        name="qk_prep",
    )(*args)


def _dup_halves(x):
    lane = lax.broadcasted_iota(I32, x.shape, 1)
    sw = pltpu.roll(x, HEAD_DIM, axis=1)
    lo = lane < HEAD_DIM
    return jnp.where(lo, x, sw), jnp.where(lo, sw, x)


def _attend(q, k_all, v_all, sink_ref, mask):
    scale = HEAD_DIM ** -0.5
    nt = (((1,), (1,)), ((), ()))
    kk = [a.astype(BF16) for a in _dup_halves(k_all)]
    vv = [a.astype(BF16) for a in _dup_halves(v_all)]
    lane = lax.broadcasted_iota(I32, (q.shape[0], LANES), 1)
    lo = lane < HEAD_DIM
    tiles = []
    for t in range(q.shape[1] // LANES):
        qt = q[:, t * LANES:(t + 1) * LANES]
        g = (2 * t) // GQA_GROUP
        halves = []
        for hh in range(2):
            head = 2 * t + hh
            qm = jnp.where(lo if hh == 0 else ~lo, qt, 0.0).astype(BF16)
            s = lax.dot_general(qm, kk[g], nt, preferred_element_type=F32) * scale
            if mask is not None:
                s = jnp.where(mask, s, NEG_BIG)
            sink = sink_ref[head]
            mx = jnp.maximum(jnp.max(s, axis=1, keepdims=True), sink)
            p = jnp.exp(s - mx)
            den = jnp.sum(p, axis=1, keepdims=True) + jnp.exp(sink - mx)
            p = (p / den).astype(BF16)
            halves.append(jnp.dot(p, vv[g], preferred_element_type=F32))
        tiles.append(jnp.where(lo, halves[0], halves[1]))
    return jnp.concatenate(tiles, axis=1)


def _ctx_attn_kernel(sink_ref, q_ref, k_ref, v_ref, o_ref):
    o_ref[...] = _attend(q_ref[...], k_ref[...], v_ref[...], sink_ref, None)


def _ctx_attention(q, k, v, sink, seq):
    m, nq = q.shape
    nk = k.shape[1]
    return pl.pallas_call(
        _ctx_attn_kernel,
        grid=(m // seq,),
        in_specs=[pl.BlockSpec(memory_space=pltpu.SMEM),
                  pl.BlockSpec((seq, nq), lambda b: (b, 0)),
                  pl.BlockSpec((seq, nk), lambda b: (b, 0)),
                  pl.BlockSpec((seq, nk), lambda b: (b, 0))],
        out_specs=pl.BlockSpec((seq, nq), lambda b: (b, 0)),
        out_shape=jax.ShapeDtypeStruct((m, nq), F32),
        compiler_params=_cparams("arbitrary"),
        name="ctx_attention",
    )(sink, q, k, v)


def _lat_attn_kernel(sink_ref, q_ref, kc_ref, vc_ref, kp_ref, k0_ref, kn_ref, vp_ref, v0_ref, vn_ref,
                     o_ref, *, seq):
    qb = pl.program_id(1)
    blk = q_ref.shape[0]
    n_ctx = kc_ref.shape[1]
    k_all = jnp.concatenate([kc_ref[0], kp_ref[...], k0_ref[...], kn_ref[...]], axis=0)
    v_all = jnp.concatenate([vc_ref[0], vp_ref[...], v0_ref[...], vn_ref[...]], axis=0)
    tk = k_all.shape[0]
    qpos = qb * blk + lax.broadcasted_iota(I32, (blk, tk), 0)
    col = lax.broadcasted_iota(I32, (blk, tk), 1)
    kpos = (qb - 1) * blk + col - n_ctx
    local_ok = (jnp.abs(qpos - kpos) <= WINDOW) & (kpos >= 0) & (kpos < seq)
    mask = (col < n_ctx) | local_ok
    o_ref[...] = _attend(q_ref[...], k_all, v_all, sink_ref, mask)


def _lat_attention(q, k, v, k_ctx, v_ctx, sink, seq):
    m, nq = q.shape
    nk = k.shape[1]
    blk = WINDOW
    nb = seq // blk
    n_ctx = k_ctx.shape[1]
    last = m // blk - 1

    def kv_spec(shift):
        return pl.BlockSpec((blk, nk), lambda b, i: (jnp.clip(b * nb + i + shift, 0, last), 0))

    ctx_spec = pl.BlockSpec((1, n_ctx, nk), lambda b, i: (b, 0, 0))
    return pl.pallas_call(
        functools.partial(_lat_attn_kernel, seq=seq),
        grid=(m // seq, nb),
        in_specs=[pl.BlockSpec(memory_space=pltpu.SMEM),
                  pl.BlockSpec((blk, nq), lambda b, i: (b * nb + i, 0)),
                  ctx_spec, ctx_spec,
                  kv_spec(-1), kv_spec(0), kv_spec(1),
                  kv_spec(-1), kv_spec(0), kv_spec(1)],
        out_specs=pl.BlockSpec((blk, nq), lambda b, i: (b * nb + i, 0)),
        out_shape=jax.ShapeDtypeStruct((m, nq), F32),
        compiler_params=_cparams("arbitrary", "arbitrary"),
        name="lat_attention",
    )(sink, q, k_ctx, v_ctx, k, k, k, v, v, v)

SSD_BLOCK = 256
SSD_PAIRS = SSD_HEADS // 2
SSD_INNER = SSD_HEADS * SSD_HEAD_DIM
HALO = SUBLANES


def _softplus(x):
    return jnp.maximum(x, 0.0) + jnp.log1p(jnp.exp(-jnp.abs(x)))


def _silu(x):
    return x * jax.nn.sigmoid(x)


def _ssd_decays(dt_raw, bias, a_log):
    n = dt_raw.shape[0]
    dt = _softplus(dt_raw + bias)
    log_a = dt * (-jnp.exp(a_log))
    r = lax.broadcasted_iota(I32, (n, n), 0)
    c = lax.broadcasted_iota(I32, (n, n), 1)
    lower = jnp.where(c <= r, 1.0, 0.0).astype(F32)
    upper = jnp.where(r <= c, 1.0, 0.0).astype(F32)
    cum_col = jnp.dot(lower, log_a, precision=HIGHEST, preferred_element_type=F32)
    dt_row = dt.T
    la_row = log_a.T
    cum_row = jnp.dot(la_row, upper, precision=HIGHEST, preferred_element_type=F32)
    return dt, log_a, cum_col, dt_row, la_row, cum_row


def _ssd_scan_chunk(xs, bmat, cmat, w_of, q_scale_of, k_scale_of, carry_of, s_ref):
    nt = (((1,), (1,)), ((), ()))
    n = xs.shape[0]
    lane = lax.broadcasted_iota(I32, (n, LANES), 1)
    lo = lane < SSD_HEAD_DIM
    lane_s = lax.broadcasted_iota(I32, (D_STATE, LANES), 1)
    lo_s = lane_s < SSD_HEAD_DIM
    b_t = bmat.T
    cb16 = cmat.astype(BF16)
    ys = []
    for pair in range(SSD_PAIRS):
        g = (2 * pair) // (SSD_HEADS // SSD_GROUPS)
        in_g = (lane // D_STATE) == g
        cg = jnp.where(in_g, cmat, 0.0)
        cb = lax.dot_general(cg.astype(BF16), bmat.astype(BF16), nt, preferred_element_type=F32)
        x_pair = xs[:, pair * LANES:(pair + 1) * LANES]
        x16 = x_pair.astype(BF16)
        s_old = s_ref[pair]
        s2 = jnp.concatenate([s_old, s_old], axis=0).astype(BF16)
        bg_t = b_t[g * D_STATE:(g + 1) * D_STATE, :]
        y_h, s_h = [], []
        for hh in range(2):
            h = 2 * pair + hh
            w = (cb * w_of(h)).astype(BF16)
            y = jnp.dot(w, x16, preferred_element_type=F32)
            cq = (cg * q_scale_of(h)).astype(BF16)
            y = y + jnp.dot(cq, s2, preferred_element_type=F32)
            y_h.append(y)
            kt = (bg_t * k_scale_of(h)).astype(BF16)
            s_h.append(carry_of(h) * s_old + jnp.dot(kt, x16, preferred_element_type=F32))
        ys.append(jnp.where(lo, y_h[0], y_h[1]))
        s_ref[pair] = jnp.where(lo_s, s_h[0], s_h[1])
    return jnp.concatenate(ys, axis=1)


def _ssd_fwd_kernel(x_ref, xp_ref, xn_ref, dt_ref, s0_ref, cw_ref, cb_ref, bias_ref, alog_ref,
                    y_ref, xc_ref, sfin_ref, s_ref):
    c = pl.program_id(1)
    nc = pl.num_programs(1)
    n = x_ref.shape[0]

    @pl.when(c == 0)
    def _():
        s_ref[...] = s0_ref[0]

    prev = jnp.where(c > 0, xp_ref[...], 0.0)
    nxt = jnp.where(c < nc - 1, xn_ref[...], 0.0)
    xe = jnp.concatenate([prev, x_ref[...], nxt], axis=0)
    pad = (CONV_K - 1) // 2
    acc = cb_ref[...] + cw_ref[0:1, :] * xe[HALO - pad:HALO - pad + n, :]
    for k in range(1, CONV_K):
        acc = acc + cw_ref[k:k + 1, :] * xe[HALO - pad + k:HALO - pad + k + n, :]
    xc = _silu(acc)
    xc_ref[...] = xc
    xs = xc[:, :SSD_INNER]
    bmat = xc[:, SSD_INNER:SSD_INNER + LANES]
    cmat = xc[:, SSD_INNER + LANES:SSD_INNER + 2 * LANES]

    dt, log_a, cum_col, dt_row, la_row, cum_row = _ssd_decays(dt_ref[...], bias_ref[...], alog_ref[...])
    r = lax.broadcasted_iota(I32, (n, n), 0)
    cc = lax.broadcasted_iota(I32, (n, n), 1)
    causal = cc <= r
    last_col = cum_col[n - 1:n, :]

    def w_of(h):
        seg = cum_col[:, h:h + 1] - cum_row[h:h + 1, :]
        return jnp.exp(jnp.where(causal, seg, NEG_BIG)) * dt_row[h:h + 1, :]

    def q_scale_of(h):
        return jnp.exp(cum_col[:, h:h + 1])

    def k_scale_of(h):
        return dt_row[h:h + 1, :] * jnp.exp(cum_row[h:h + 1, n - 1:n] - cum_row[h:h + 1, :])

    def carry_of(h):
        return jnp.exp(last_col[:, h:h + 1])

    y_ref[...] = _ssd_scan_chunk(xs, bmat, cmat, w_of, q_scale_of, k_scale_of, carry_of, s_ref)

    @pl.when(c == nc - 1)
    def _():
        sfin_ref[0] = s_ref[...]


def _ssd_bwd_kernel(xc_ref, dt_ref, yf_ref, z_ref, s0_ref, bias_ref, alog_ref, dskip_ref, nw_ref,
                    y_ref, sfin_ref, s_ref):
    c = pl.program_id(1)
    nc = pl.num_programs(1)
    n = xc_ref.shape[0]

    @pl.when(c == 0)
    def _():
        s_ref[...] = s0_ref[0]

    xc = xc_ref[...]
    xs = xc[:, :SSD_INNER]
    bmat = xc[:, SSD_INNER:SSD_INNER + LANES]
    cmat = xc[:, SSD_INNER + LANES:SSD_INNER + 2 * LANES]
    dt, log_a, cum_col, dt_row, la_row, cum_row = _ssd_decays(dt_ref[...], bias_ref[...], alog_ref[...])
    ex_col = cum_col - log_a
    ex_row = cum_row - la_row
    r = lax.broadcasted_iota(I32, (n, n), 0)
    cc = lax.broadcasted_iota(I32, (n, n), 1)
    anti = cc >= r
    tot_col = cum_col[n - 1:n, :]
    off = SSD_HEADS

    def w_of(h):
        j = off + h
        seg = ex_row[j:j + 1, :] - ex_col[:, j:j + 1]
        return jnp.exp(jnp.where(anti, seg, NEG_BIG)) * dt_row[j:j + 1, :]

    def q_scale_of(h):
        j = off + h
        return jnp.exp(tot_col[:, j:j + 1] - ex_col[:, j:j + 1])

    def k_scale_of(h):
        j = off + h
        return dt_row[j:j + 1, :] * jnp.exp(ex_row[j:j + 1, :])

    def carry_of(h):
        j = off + h
        return jnp.exp(tot_col[:, j:j + 1])

    y_b = _ssd_scan_chunk(xs, bmat, cmat, w_of, q_scale_of, k_scale_of, carry_of, s_ref)
    y = yf_ref[...] + y_b + dskip_ref[...] * xs
    y = y * _silu(z_ref[...])
    ms = jnp.mean(y * y, axis=-1, keepdims=True)
    y_ref[...] = y * lax.rsqrt(ms + NORM_EPS) * nw_ref[...]

    @pl.when(c == nc - 1)
    def _():
        sfin_ref[0] = s_ref[...]


def _pair_states(s):
    b, h, n, p = s.shape
    return s.reshape(b, h // 2, 2, n, p).transpose(0, 1, 3, 2, 4).reshape(b, h // 2, n, 2 * p)


def _unpair_states(s):
    b, hp, n, p2 = s.shape
    return s.reshape(b, hp, n, 2, p2 // 2).transpose(0, 1, 3, 2, 4).reshape(b, hp * 2, n, p2 // 2)


def _ssd(xbc, dt, z, s0_f, s0_b, conv_w, conv_b, dt_bias, a_log, d_skip, ssd_norm, seq):
    m, nx = xbc.shape
    nb = m // seq
    blk = min(SSD_BLOCK, seq)
    nc = seq // blk
    hb = blk // HALO
    n_halo = m // HALO
    pad16 = lambda a: jnp.pad(a.reshape(1, -1), ((0, 0), (0, LANES - a.size)))
    bias = pad16(dt_bias)
    alog = pad16(a_log)
    state_spec = pl.BlockSpec((1, SSD_PAIRS, D_STATE, LANES), lambda b, c: (b, 0, 0, 0))
    state_shape = jax.ShapeDtypeStruct((nb, SSD_PAIRS, D_STATE, LANES), F32)
    row = lambda width: pl.BlockSpec((1, width), lambda b, c: (0, 0))

    def fwd_rows(width):
        return pl.BlockSpec((blk, width), lambda b, c: (b * nc + c, 0))

    def bwd_rows(width):
        return pl.BlockSpec((blk, width), lambda b, c: (b * nc + nc - 1 - c, 0))

    y_f, xc, s_f = pl.pallas_call(
        _ssd_fwd_kernel,
        grid=(nb, nc),
        in_specs=[fwd_rows(nx),
                  pl.BlockSpec((HALO, nx), lambda b, c: (jnp.maximum((b * nc + c) * hb - 1, 0), 0)),
                  pl.BlockSpec((HALO, nx), lambda b, c: (jnp.minimum((b * nc + c + 1) * hb, n_halo - 1), 0)),
                  fwd_rows(LANES), state_spec,
                  pl.BlockSpec((CONV_K, nx), lambda b, c: (0, 0)), row(nx), row(LANES), row(LANES)],
        out_specs=[fwd_rows(SSD_INNER), fwd_rows(nx), state_spec],
        out_shape=[jax.ShapeDtypeStruct((m, SSD_INNER), F32), jax.ShapeDtypeStruct((m, nx), F32), state_shape],
        scratch_shapes=[pltpu.VMEM((SSD_PAIRS, D_STATE, LANES), F32)],
        compiler_params=_cparams("arbitrary", "arbitrary"),
        name="ssd_forward",
    )(xbc, xbc, xbc, dt, _pair_states(s0_f), conv_w, conv_b.reshape(1, nx), bias, alog)

    dskip = jnp.repeat(d_skip, SSD_HEAD_DIM).reshape(1, SSD_INNER)
    y, s_b = pl.pallas_call(
        _ssd_bwd_kernel,
        grid=(nb, nc),
        in_specs=[bwd_rows(nx), bwd_rows(LANES), bwd_rows(SSD_INNER), bwd_rows(SSD_INNER), state_spec,
                  row(LANES), row(LANES), row(SSD_INNER), row(SSD_INNER)],
        out_specs=[bwd_rows(SSD_INNER), state_spec],
        out_shape=[jax.ShapeDtypeStruct((m, SSD_INNER), F32), state_shape],
        scratch_shapes=[pltpu.VMEM((SSD_PAIRS, D_STATE, LANES), F32)],
        compiler_params=_cparams("arbitrary", "arbitrary"),
        name="ssd_backward",
    )(xc, dt, y_f, z, _pair_states(s0_b), bias, alog, dskip, ssd_norm.reshape(1, SSD_INNER))
    return y, _unpair_states(s_f), _unpair_states(s_b)

def _hgrn_kernel(q_ref, ff_ref, fb_ref, i_ref, g_ref, lb_ref, s0_ref, nw_ref, o_ref, sfin_ref,
                 s_ref, *, layer):
    t_len = q_ref.shape[0]
    n = HGRN_CHUNK
    n_chunks = t_len // n
    tn = (((0,), (0,)), ((), ()))
    nt = (((1,), (1,)), ((), ()))

    lbp = lb_ref[...]
    e = jnp.exp(lbp - jnp.max(lbp, axis=0, keepdims=True))
    sm = e / jnp.sum(e, axis=0, keepdims=True)
    lb = sm[0] * 0.0
    for j in range(1, layer + 1):
        lb = lb + sm[j]

    r = lax.broadcasted_iota(I32, (n, n), 0)
    c = lax.broadcasted_iota(I32, (n, n), 1)
    lower = jnp.where(c <= r, 1.0, 0.0).astype(F32)
    srow = lax.broadcasted_iota(I32, (n, HGRN_DK), 0)
    qscale = HGRN_DK ** -0.5

    def chunk(row0, f_ref, lb_d, reverse):
        q = _silu(q_ref[pl.ds(row0, n), :]) * qscale
        f = f_ref[pl.ds(row0, n), :]
        v = i_ref[pl.ds(row0, n), :]
        k = (1.0 - lb_d) * jax.nn.sigmoid(-f)
        lf = jnp.log(lb_d + (1.0 - lb_d) * jax.nn.sigmoid(f))
        cum = jnp.dot(lower, lf, precision=HIGHEST, preferred_element_type=F32)
        tot = cum[n - 1:n, :]
        if reverse:
            cum = cum - lf
        rows = []
        for t in range(n):
            if reverse:
                seg = jnp.where(srow >= t, cum - cum[t:t + 1, :], NEG_BIG)
            else:
                seg = jnp.where(srow <= t, cum[t:t + 1, :] - cum, NEG_BIG)
            a = q[t:t + 1, :] * k * jnp.exp(seg)
            sc = jnp.sum(a, axis=1, keepdims=True)
            rows.append(jnp.sum(sc * v, axis=0, keepdims=True))
        o = jnp.concatenate(rows, axis=0)
        s_old = s_ref[...]
        if reverse:
            q_in = q * jnp.exp(tot - cum)
            k_out = k * jnp.exp(cum)
        else:
            q_in = q * jnp.exp(cum)
            k_out = k * jnp.exp(tot - cum)
        o = o + lax.dot_general(q_in.astype(BF16), s_old.astype(BF16), nt, preferred_element_type=F32)
        s_ref[...] = jnp.exp(tot) * s_old + lax.dot_general(
            v.astype(BF16), k_out.astype(BF16), tn, preferred_element_type=F32)
        return o

    s_ref[...] = s0_ref[0, 0, 0].T

    def fwd_body(ci, carry):
        row0 = pl.multiple_of(ci * n, n)
        o_ref[pl.ds(row0, n), :] = chunk(row0, ff_ref, lb[0:1, :], False)
        return carry

    lax.fori_loop(0, n_chunks, fwd_body, 0)
    sfin_ref[0, 0, 0] = s_ref[...].T

    s_ref[...] = s0_ref[0, 1, 0].T
    nw = nw_ref[...]

    def bwd_body(ci, carry):
        row0 = pl.multiple_of((n_chunks - 1 - ci) * n, n)
        o = o_ref[pl.ds(row0, n), :] + chunk(row0, fb_ref, lb[1:2, :], True)
        ms = jnp.mean(o * o, axis=-1, keepdims=True)
        o = o * lax.rsqrt(ms + NORM_EPS) * nw
        o_ref[pl.ds(row0, n), :] = o * _silu(g_ref[pl.ds(row0, n), :])
        return carry

    lax.fori_loop(0, n_chunks, bwd_body, 0)
    sfin_ref[0, 1, 0] = s_ref[...].T


def _hgrn(q, f_fw, f_bw, iv, g, o_lb, state0, g_norm, seq, layer):
    m, width = q.shape
    nb = m // seq
    dv = width // HGRN_HEADS
    col = pl.BlockSpec((seq, dv), lambda b, h: (b, h))
    state_spec = pl.BlockSpec((1, 2, 1, HGRN_DK, dv), lambda b, h: (b, 0, h, 0, 0))
    return pl.pallas_call(
        functools.partial(_hgrn_kernel, layer=layer),
        grid=(nb, HGRN_HEADS),
        in_specs=[col, col, col, col, col,
                  pl.BlockSpec((o_lb.shape[0], 2, HGRN_DK), lambda b, h: (0, 0, h)),
                  state_spec,
                  pl.BlockSpec((1, dv), lambda b, h: (0, 0))],
        out_specs=[col, state_spec],
        out_shape=[jax.ShapeDtypeStruct((m, width), F32),
                   jax.ShapeDtypeStruct((nb, 2, HGRN_HEADS, HGRN_DK, dv), F32)],
        scratch_shapes=[pltpu.VMEM((dv, HGRN_DK), F32)],
        compiler_params=_cparams("arbitrary", "arbitrary"),
        name="hgrn2",
    )(q, f_fw, f_bw, iv, g, o_lb, state0, g_norm.reshape(1, dv))

EVEN_SPLITS = ((0, 512), (512, 640), (640, 768), (768, 1280), (1280, 2048), (2048, 2176))
HGRN_SPLITS = tuple((i * 1024, (i + 1) * 1024) for i in range(5))


def _even_weight(w):
    main = EVEN_SPLITS[-1][0]
    return jnp.pad(w, ((0, 0), (0, LANES - (w.shape[1] - main)))).astype(BF16)


def _run_trunk(x3, mods, mod_row0, P, cache):
    nb, seq, d = x3.shape
    x = x3.reshape(nb * seq, d)
    depth = P['norm_mix'].shape[0]
    ks, vs, ssd_states, hgrn_states = [], [], [], []
    for l in range(depth):
        j = l // 2
        row0 = (l * SUBLANES + mod_row0, 0 if cache is None else 1)
        if l % 2 == 0:
            q, k, v, z, xbc, dt = _inproj(x, mods, row0, seq, P['norm_mix'][l], P['e_w_in'][j],
                                          EVEN_SPLITS, "even_in_proj")
            q, k = _qkprep(q, k, P['e_q_norm'][j], P['e_k_norm'][j], seq, rope=cache is not None)
            if cache is None:
                s0_f = jnp.zeros((nb, SSD_HEADS, D_STATE, SSD_HEAD_DIM), F32)
                s0_b = s0_f
                o_attn = _ctx_attention(q, k, v, P['e_sink'][j], seq)
            else:
                s0_f, s0_b = cache[2][:, j, 0], cache[2][:, j, 1]
                n_ctx = cache[0].shape[2]
                o_attn = _lat_attention(q, k, v, cache[0][:, j].reshape(nb, n_ctx, -1),
                                        cache[1][:, j].reshape(nb, n_ctx, -1), P['e_sink'][j], seq)
            y, s_f, s_b = _ssd(xbc, dt, z, s0_f, s0_b, P['e_conv_w'][j], P['e_conv_b'][j],
                               P['e_dt_bias'][j], P['e_a_log'][j], P['e_d_skip'][j], P['e_ssd_norm'][j], seq)
            if cache is None:
                ks.append(k.reshape(nb, seq, N_KV_HEADS, HEAD_DIM))
                vs.append(v.reshape(nb, seq, N_KV_HEADS, HEAD_DIM))
                ssd_states.append(jnp.stack([s_f, s_b], axis=1))
            mix = jnp.concatenate([o_attn, y], axis=1)
            x = _outproj(mix, x, mods, row0, seq, P['e_w_out'][j], "even_out_proj")
        else:
            q, f_fw, f_bw, iv, g = _inproj(x, mods, row0, seq, P['norm_mix'][l], P['o_w_in'][j],
                                           HGRN_SPLITS, "odd_in_proj")
            if cache is None:
                s0 = jnp.zeros((nb, 2, HGRN_HEADS, HGRN_DK, d // HGRN_HEADS), F32)
            else:
                s0 = cache[3][:, j]
            o, s_new = _hgrn(q, f_fw, f_bw, iv, g, P['o_lb'], s0, P['o_g_norm'][j], seq, j)
            if cache is None:
                hgrn_states.append(s_new)
            x = _outproj(o, x, mods, row0, seq, P['o_w_out'][j], "odd_out_proj")
        x = _peer(x, mods, row0, seq, l, P['norm_ffn'][l], P['p_w_q'][l], P['p_sub_keys'][l],
                  P['p_u'], P['p_v'])
    y = x.reshape(nb, seq, d)
    if cache is not None:
        return y, None
    return y, (jnp.stack(ks, axis=1), jnp.stack(vs, axis=1),
               jnp.stack(ssd_states, axis=1), jnp.stack(hgrn_states, axis=1))


def kernel(x_prompt, x_sample, cache_k, cache_v, state_ssd, state_hgrn, c, c_ctx, w_ada, b_ada, norm_mix, norm_ffn, e_w_in, e_q_norm, e_k_norm, e_sink, e_conv_w, e_conv_b, e_dt_bias, e_a_log, e_d_skip, e_ssd_norm, e_w_out, o_w_in, o_lb, o_g_norm, o_w_out, p_w_q, p_sub_keys, p_u, p_v):
    depth, d, d6 = w_ada.shape
    b_lat = x_sample.shape[0]
    cond_rows = jnp.concatenate([c_ctx[None, :], c, jnp.zeros((SUBLANES - 1 - b_lat, d), F32)], axis=0)
    mods = _modulation(cond_rows, w_ada, b_ada).reshape(depth * SUBLANES, 1, d6)
    P = {
        'norm_mix': norm_mix, 'norm_ffn': norm_ffn,
        'e_w_in': jnp.stack([_even_weight(w) for w in e_w_in]), 'e_q_norm': e_q_norm, 'e_k_norm': e_k_norm,
        'e_sink': e_sink, 'e_conv_w': e_conv_w, 'e_conv_b': e_conv_b, 'e_dt_bias': e_dt_bias,
        'e_a_log': e_a_log, 'e_d_skip': e_d_skip, 'e_ssd_norm': e_ssd_norm,
        'e_w_out': e_w_out.astype(BF16),
        'o_w_in': o_w_in.astype(BF16), 'o_lb': o_lb, 'o_g_norm': o_g_norm, 'o_w_out': o_w_out.astype(BF16),
        'p_w_q': p_w_q.astype(BF16),
        'p_sub_keys': p_sub_keys.astype(BF16).reshape(depth, PEER_HEADS * 2, PEER_NKEYS, PEER_DKEY),
        'p_u': p_u, 'p_v': p_v,
    }
    y_prompt, new_state = _run_trunk(x_prompt, mods, 0, P, None)
    y_sample, _ = _run_trunk(x_sample, mods, 1, P, (cache_k, cache_v, state_ssd, state_hgrn))
    return (y_prompt, y_sample) + new_state
```

```python
import functools
import math

import jax
import jax.numpy as jnp
from jax import lax
from jax.experimental import pallas as pl
from jax.experimental.pallas import tpu as pltpu

F32 = jnp.float32
BF16 = jnp.bfloat16
I32 = jnp.int32
HIGHEST = lax.Precision.HIGHEST

NORM_EPS = 1e-6
NEG_BIG = -1e30
LANES = 128
SUBLANES = 8
VMEM_LIMIT = 48 * 1024 * 1024

GRID_W = 64
HEAD_DIM = 64
N_Q_HEADS = 8
N_KV_HEADS = 2
GQA_GROUP = 4
WINDOW = 128
ROPE_THETA = 10000.0
SSD_HEADS = 8
SSD_HEAD_DIM = 64
SSD_GROUPS = 2
D_STATE = 64
CONV_K = 5
HGRN_HEADS = 8
HGRN_DK = 128
HGRN_CHUNK = 32
PEER_HEADS = 8
PEER_NKEYS = 128
PEER_TOPK = 16
PEER_DKEY = 128
PEER_PAIRS = PEER_HEADS * PEER_TOPK

ROW_TILE = 256
PEER_BLOCK = 128


def _cparams(*sem):
    return pltpu.CompilerParams(dimension_semantics=sem, vmem_limit_bytes=VMEM_LIMIT)


def _norm_mod(x, nw, scale, shift):
    ms = jnp.mean(x * x, axis=-1, keepdims=True)
    return (x * lax.rsqrt(ms + NORM_EPS)) * nw * (1.0 + scale) + shift


def _mod_kernel(c_ref, w_ref, b_ref, o_ref):
    c = c_ref[...]
    s = c * jax.nn.sigmoid(c)
    o_ref[0] = jnp.dot(s, w_ref[0], precision=HIGHEST, preferred_element_type=F32) + b_ref[0]


def _modulation(cond_rows, w_ada, b_ada):
    depth, d, n = w_ada.shape
    rows = cond_rows.shape[0]
    return pl.pallas_call(
        _mod_kernel,
        grid=(depth, n // d),
        in_specs=[pl.BlockSpec((rows, d), lambda l, j: (0, 0)),
                  pl.BlockSpec((1, d, d), lambda l, j: (l, 0, j)),
                  pl.BlockSpec((1, 1, d), lambda l, j: (l, 0, j))],
        out_specs=pl.BlockSpec((1, rows, d), lambda l, j: (l, 0, j)),
        out_shape=jax.ShapeDtypeStruct((depth, rows, n), F32),
        compiler_params=_cparams("arbitrary", "arbitrary"),
        name="modulation",
    )(cond_rows, w_ada, b_ada.reshape(depth, 1, n))


def _mod_spec(mod_row0, seq, tile, d6):
    row0, per_batch = mod_row0
    return pl.BlockSpec((1, 1, d6), lambda i: (row0 + per_batch * ((i * tile) // seq), 0, 0))


def _inproj_kernel(x_ref, m_ref, nw_ref, w_ref, *o_refs, splits, d):
    m = m_ref[0]
    h = _norm_mod(x_ref[...], nw_ref[...], m[:, d:2 * d], m[:, 0:d]).astype(BF16)
    for o_ref, (a, b) in zip(o_refs, splits):
        o_ref[...] = jnp.dot(h, w_ref[:, a:b], preferred_element_type=F32)


def _inproj(x, mods, mod_row0, seq, nw, w_bf16, splits, name):
    m, d = x.shape
    n = w_bf16.shape[1]
    tile = min(ROW_TILE, seq)
    return pl.pallas_call(
        functools.partial(_inproj_kernel, splits=splits, d=d),
        grid=(m // tile,),
        in_specs=[pl.BlockSpec((tile, d), lambda i: (i, 0)),
                  _mod_spec(mod_row0, seq, tile, mods.shape[-1]),
                  pl.BlockSpec((1, d), lambda i: (0, 0)),
                  pl.BlockSpec((d, n), lambda i: (0, 0))],
        out_specs=[pl.BlockSpec((tile, b - a), lambda i: (i, 0)) for a, b in splits],
        out_shape=[jax.ShapeDtypeStruct((m, b - a), F32) for a, b in splits],
        compiler_params=_cparams("arbitrary"),
        name=name,
    )(x, mods, nw.reshape(1, d), w_bf16)


def _outproj_kernel(mix_ref, x_ref, m_ref, w_ref, o_ref, *, d):
    y = jnp.dot(mix_ref[...].astype(BF16), w_ref[...], preferred_element_type=F32)
    o_ref[...] = x_ref[...] + m_ref[0][:, 2 * d:3 * d] * y


def _outproj(mix, x, mods, mod_row0, seq, w_bf16, name):
    m, d = x.shape
    k = mix.shape[1]
    tile = min(ROW_TILE, seq)
    return pl.pallas_call(
        functools.partial(_outproj_kernel, d=d),
        grid=(m // tile,),
        in_specs=[pl.BlockSpec((tile, k), lambda i: (i, 0)),
                  pl.BlockSpec((tile, d), lambda i: (i, 0)),
                  _mod_spec(mod_row0, seq, tile, mods.shape[-1]),
                  pl.BlockSpec((k, d), lambda i: (0, 0))],
        out_specs=pl.BlockSpec((tile, d), lambda i: (i, 0)),
        out_shape=jax.ShapeDtypeStruct((m, d), F32),
        compiler_params=_cparams("arbitrary"),
        name=name,
    )(mix, x, mods, w_bf16)


def _topk_over_rows(s, k, payload=None):
    n = s.shape[0]
    iota = lax.broadcasted_iota(I32, s.shape, 0)
    vals, idxs, pays = [], [], []
    for _ in range(k):
        m = jnp.max(s, axis=0, keepdims=True)
        i = jnp.min(jnp.where(s == m, iota, n), axis=0, keepdims=True)
        hit = iota == i
        vals.append(m)
        idxs.append(i)
        if payload is not None:
            pays.append(jnp.max(jnp.where(hit, payload, -1), axis=0, keepdims=True))
        s = jnp.where(hit, -jnp.inf, s)
    out = (jnp.concatenate(vals, axis=0), jnp.concatenate(idxs, axis=0))
    if payload is not None:
        out += (jnp.concatenate(pays, axis=0),)
    return out


def _peer_route_kernel(x_ref, m_ref, nw_ref, wq_ref, keys_ref, h_ref, e_ref, g_ref, *, d):
    m = m_ref[0]
    h = _norm_mod(x_ref[...], nw_ref[...], m[:, 4 * d:5 * d], m[:, 3 * d:4 * d])
    h_ref[...] = h
    hb = h.astype(BF16)
    nt = (((1,), (1,)), ((), ()))
    for head in range(PEER_HEADS):
        tops = []
        for half in range(2):
            c0 = (head * 2 + half) * PEER_DKEY
            q = jnp.dot(hb, wq_ref[:, c0:c0 + PEER_DKEY], preferred_element_type=F32)
            s = lax.dot_general(keys_ref[head * 2 + half], q.astype(BF16), nt,
                                preferred_element_type=F32)
            tops.append(_topk_over_rows(s, PEER_TOPK))
        (s0, i0), (s1, i1) = tops
        cand_s = jnp.concatenate([s0[a:a + 1] + s1 for a in range(PEER_TOPK)], axis=0)
        cand_e = jnp.concatenate([i0[a:a + 1] * PEER_NKEYS + i1 for a in range(PEER_TOPK)], axis=0)
        best_s, _, best_e = _topk_over_rows(cand_s, PEER_TOPK, payload=cand_e)
        p = jnp.exp(best_s - best_s[0:1])
        r0 = head * PEER_TOPK
        e_ref[r0:r0 + PEER_TOPK, :] = best_e
        g_ref[r0:r0 + PEER_TOPK, :] = p / jnp.sum(p, axis=0, keepdims=True)


def _peer_route(x, mods, mod_row0, seq, nw, wq_bf16, keys_bf16):
    m, d = x.shape
    tile = min(ROW_TILE, seq)
    nq = wq_bf16.shape[1]
    return pl.pallas_call(
        functools.partial(_peer_route_kernel, d=d),
        grid=(m // tile,),
        in_specs=[pl.BlockSpec((tile, d), lambda i: (i, 0)),
                  _mod_spec(mod_row0, seq, tile, mods.shape[-1]),
                  pl.BlockSpec((1, d), lambda i: (0, 0)),
                  pl.BlockSpec((d, nq), lambda i: (0, 0)),
                  pl.BlockSpec(keys_bf16.shape, lambda i: (0, 0, 0))],
        out_specs=[pl.BlockSpec((tile, d), lambda i: (i, 0)),
                   pl.BlockSpec((PEER_PAIRS, tile), lambda i: (0, i)),
                   pl.BlockSpec((PEER_PAIRS, tile), lambda i: (0, i))],
        out_shape=[jax.ShapeDtypeStruct((m, d), F32),
                   jax.ShapeDtypeStruct((PEER_PAIRS, m), I32),
                   jax.ShapeDtypeStruct((PEER_PAIRS, m), F32)],
        compiler_params=_cparams("arbitrary"),
        name="peer_route",
    )(x, mods, nw.reshape(1, d), wq_bf16, keys_bf16)


def _peer_gather_kernel(idx_hbm, h_ref, g_ref, x_ref, m_ref, uv_hbm, o_ref,
                        idx_smem, buf, sem_idx, sem, *, layer, d):
    blk = pl.program_id(0)
    n_groups = PEER_BLOCK // SUBLANES
    chunks = d // LANES
    span = 2 * chunks

    cp = pltpu.make_async_copy(idx_hbm.at[blk], idx_smem, sem_idx)
    cp.start()
    cp.wait()

    def issue(tok, slot):
        def body(pair, carry):
            row = pl.multiple_of(idx_smem[tok, pair], span)
            pltpu.make_async_copy(uv_hbm.at[layer, pl.ds(row, span)],
                                  buf.at[slot, :, pl.ds(pair, 1), :],
                                  sem.at[slot]).start()
            return carry
        lax.fori_loop(0, PEER_PAIRS, body, 0, unroll=8)

    def wait(slot):
        pltpu.make_async_copy(buf.at[1 - slot], buf.at[slot], sem.at[slot]).wait()

    def rows_of(slot, first):
        return jnp.concatenate([buf[slot, first + c] for c in range(chunks)], axis=1)

    gate2 = m_ref[0][:, 5 * d:6 * d]
    lane = lax.broadcasted_iota(I32, (PEER_PAIRS, PEER_BLOCK), 1)

    issue(0, 0)

    def group(grp, carry):
        base = pl.multiple_of(grp * SUBLANES, SUBLANES)
        h8 = h_ref[pl.ds(base, SUBLANES), :]
        rows = []
        for r in range(SUBLANES):
            tok = base + r
            slot = r % 2
            if r < SUBLANES - 1:
                issue(tok + 1, 1 - slot)
            else:
                @pl.when(grp < n_groups - 1)
                def _():
                    issue(tok + 1, 1 - slot)
            wait(slot)
            act = jnp.sum(rows_of(slot, 0) * h8[r:r + 1, :], axis=1, keepdims=True)
            gate = jnp.sum(jnp.where(lane == tok, g_ref[...], 0.0), axis=1, keepdims=True)
            w = jax.nn.gelu(act) * gate
            rows.append(jnp.sum(rows_of(slot, chunks) * w, axis=0, keepdims=True))
        out8 = jnp.concatenate(rows, axis=0)
        o_ref[pl.ds(base, SUBLANES), :] = x_ref[pl.ds(base, SUBLANES), :] + gate2 * out8
        return carry

    lax.fori_loop(0, n_groups, group, 0)


def _peer_gather(idx, h, gates, x, mods, mod_row0, seq, p_uv, layer):
    m, d = x.shape
    nblk = m // PEER_BLOCK
    span = 2 * d // LANES
    idx3 = (span * idx).T.reshape(nblk, PEER_BLOCK, PEER_PAIRS)
    return pl.pallas_call(
        functools.partial(_peer_gather_kernel, layer=layer, d=d),
        grid=(nblk,),
        in_specs=[pl.BlockSpec(memory_space=pl.ANY),
                  pl.BlockSpec((PEER_BLOCK, d), lambda i: (i, 0)),
                  pl.BlockSpec((PEER_PAIRS, PEER_BLOCK), lambda i: (0, i)),
                  pl.BlockSpec((PEER_BLOCK, d), lambda i: (i, 0)),
                  _mod_spec(mod_row0, seq, PEER_BLOCK, mods.shape[-1]),
                  pl.BlockSpec(memory_space=pl.ANY)],
        out_specs=pl.BlockSpec((PEER_BLOCK, d), lambda i: (i, 0)),
        out_shape=jax.ShapeDtypeStruct((m, d), F32),
        scratch_shapes=[pltpu.SMEM((PEER_BLOCK, PEER_PAIRS), I32),
                        pltpu.VMEM((2, span, PEER_PAIRS, LANES), F32),
                        pltpu.SemaphoreType.DMA,
                        pltpu.SemaphoreType.DMA((2,))],
        compiler_params=_cparams("arbitrary"),
        name="peer_gather",
    )(idx3, h, gates, x, mods, p_uv)


def _peer(x, mods, mod_row0, seq, layer, nw, wq_bf16, keys_bf16, p_uv):
    h, idx, gates = _peer_route(x, mods, mod_row0, seq, nw, wq_bf16, keys_bf16)
    return _peer_gather(idx, h, gates, x, mods, mod_row0, seq, p_uv, layer)


def _head_mean_square(x):
    n = x.shape[1]
    r = lax.broadcasted_iota(I32, (n, n), 0) // HEAD_DIM
    c = lax.broadcasted_iota(I32, (n, n), 1) // HEAD_DIM
    seg = jnp.where(r == c, 1.0 / HEAD_DIM, 0.0).astype(F32)
    return jnp.dot(x * x, seg, precision=HIGHEST, preferred_element_type=F32)


def _swap_rot_halves(x):
    n = x.shape[1]
    quarter = HEAD_DIM // 4
    lane = lax.broadcasted_iota(I32, x.shape, 1)
    lo = (lane % (2 * quarter)) < quarter
    return jnp.where(lo, pltpu.roll(x, n - quarter, axis=1), pltpu.roll(x, quarter, axis=1))


def _qkprep_kernel(q_ref, k_ref, qw_ref, kw_ref, *rest, rope):
    if rope:
        cos_ref, sin_ref, qo_ref, ko_ref = rest
    else:
        qo_ref, ko_ref = rest
    q = q_ref[...]
    k = k_ref[...]
    q = q * lax.rsqrt(_head_mean_square(q) + NORM_EPS) * qw_ref[...]
    k = k * lax.rsqrt(_head_mean_square(k) + NORM_EPS) * kw_ref[...]
    if rope:
        cos = cos_ref[...]
        sin = sin_ref[...]
        cq = jnp.concatenate([cos] * (q.shape[1] // LANES), axis=1)
        sq = jnp.concatenate([sin] * (q.shape[1] // LANES), axis=1)
        q = q * cq + _swap_rot_halves(q) * sq
        k = k * cos + _swap_rot_halves(k) * sin
    qo_ref[...] = q
    ko_ref[...] = k


def _rope_tables(seq):
    axis_dim = HEAD_DIM // 2
    inv_freq = ROPE_THETA ** (-jnp.arange(0, axis_dim, 2, dtype=F32) / axis_dim)
    t = jnp.arange(seq)
    pos = jnp.stack([(t // GRID_W).astype(F32), (t % GRID_W).astype(F32)], axis=1)
    lane = jnp.arange(LANES)
    dd = lane % HEAD_DIM
    ang = pos[:, dd // axis_dim] * inv_freq[dd % (axis_dim // 2)][None, :]
    sign = jnp.where((dd % axis_dim) < axis_dim // 2, -1.0, 1.0).astype(F32)
    return jnp.cos(ang), jnp.sin(ang) * sign[None, :]


def _qkprep(q, k, qw, kw, seq, rope):
    m, nq = q.shape
    nk = k.shape[1]
    tile = min(ROW_TILE, seq)
    qw_row = jnp.tile(qw, nq // HEAD_DIM).reshape(1, nq)
    kw_row = jnp.tile(kw, nk // HEAD_DIM).reshape(1, nk)
    in_specs = [pl.BlockSpec((tile, nq), lambda i: (i, 0)),
                pl.BlockSpec((tile, nk), lambda i: (i, 0)),
                pl.BlockSpec((1, nq), lambda i: (0, 0)),
                pl.BlockSpec((1, nk), lambda i: (0, 0))]
    args = [q, k, qw_row, kw_row]
    if rope:
        cos, sin = _rope_tables(seq)
        per_seq = seq // tile
        in_specs += [pl.BlockSpec((tile, LANES), lambda i: (i % per_seq, 0)),
                     pl.BlockSpec((tile, LANES), lambda i: (i % per_seq, 0))]
        args += [cos, sin]
    return pl.pallas_call(
        functools.partial(_qkprep_kernel, rope=rope),
        grid=(m // tile,),
        in_specs=in_specs,
        out_specs=[pl.BlockSpec((tile, nq), lambda i: (i, 0)),
                   pl.BlockSpec((tile, nk), lambda i: (i, 0))],
        out_shape=[jax.ShapeDtypeStruct((m, nq), F32), jax.ShapeDtypeStruct((m, nk), F32)],
        compiler_params=_cparams("arbitrary"),
        name="qk_prep",
    )(*args)


def _dup_halves(x):
    lane = lax.broadcasted_iota(I32, x.shape, 1)
    sw = pltpu.roll(x, HEAD_DIM, axis=1)
    lo = lane < HEAD_DIM
    return jnp.where(lo, x, sw), jnp.where(lo, sw, x)


def _attend(q, k_all, v_all, sink_ref, mask):
    scale = HEAD_DIM ** -0.5
    nt = (((1,), (1,)), ((), ()))
    kk = [a.astype(BF16) for a in _dup_halves(k_all)]
    vv = [a.astype(BF16) for a in _dup_halves(v_all)]
    lane = lax.broadcasted_iota(I32, (q.shape[0], LANES), 1)
    lo = lane < HEAD_DIM
    tiles = []
    for t in range(q.shape[1] // LANES):
        qt = q[:, t * LANES:(t + 1) * LANES]
        g = (2 * t) // GQA_GROUP
        halves = []
        for hh in range(2):
            head = 2 * t + hh
            qm = jnp.where(lo if hh == 0 else ~lo, qt, 0.0).astype(BF16)
            s = lax.dot_general(qm, kk[g], nt, preferred_element_type=F32) * scale
            if mask is not None:
                s = jnp.where(mask, s, NEG_BIG)
            sink = sink_ref[head]
            mx = jnp.maximum(jnp.max(s, axis=1, keepdims=True), sink)
            p = jnp.exp(s - mx)
            den = jnp.sum(p, axis=1, keepdims=True) + jnp.exp(sink - mx)
            p = (p / den).astype(BF16)
            halves.append(jnp.dot(p, vv[g], preferred_element_type=F32))
        tiles.append(jnp.where(lo, halves[0], halves[1]))
    return jnp.concatenate(tiles, axis=1)


def _ctx_attn_kernel(sink_ref, q_ref, k_ref, v_ref, o_ref):
    o_ref[...] = _attend(q_ref[...], k_ref[...], v_ref[...], sink_ref, None)


def _ctx_attention(q, k, v, sink, seq):
    m, nq = q.shape
    nk = k.shape[1]
    return pl.pallas_call(
        _ctx_attn_kernel,
        grid=(m // seq,),
        in_specs=[pl.BlockSpec(memory_space=pltpu.SMEM),
                  pl.BlockSpec((seq, nq), lambda b: (b, 0)),
                  pl.BlockSpec((seq, nk), lambda b: (b, 0)),
                  pl.BlockSpec((seq, nk), lambda b: (b, 0))],
        out_specs=pl.BlockSpec((seq, nq), lambda b: (b, 0)),
        out_shape=jax.ShapeDtypeStruct((m, nq), F32),
        compiler_params=_cparams("arbitrary"),
        name="ctx_attention",
    )(sink, q, k, v)


def _lat_attn_kernel(sink_ref, q_ref, kc_ref, vc_ref, kp_ref, k0_ref, kn_ref, vp_ref, v0_ref, vn_ref,
                     o_ref, *, seq):
    qb = pl.program_id(1)
    blk = q_ref.shape[0]
    n_ctx = kc_ref.shape[1]
    k_all = jnp.concatenate([kc_ref[0], kp_ref[...], k0_ref[...], kn_ref[...]], axis=0)
    v_all = jnp.concatenate([vc_ref[0], vp_ref[...], v0_ref[...], vn_ref[...]], axis=0)
    tk = k_all.shape[0]
    qpos = qb * blk + lax.broadcasted_iota(I32, (blk, tk), 0)
    col = lax.broadcasted_iota(I32, (blk, tk), 1)
    kpos = (qb - 1) * blk + col - n_ctx
    local_ok = (jnp.abs(qpos - kpos) <= WINDOW) & (kpos >= 0) & (kpos < seq)
    mask = (col < n_ctx) | local_ok
    o_ref[...] = _attend(q_ref[...], k_all, v_all, sink_ref, mask)


def _lat_attention(q, k, v, k_ctx, v_ctx, sink, seq):
    m, nq = q.shape
    nk = k.shape[1]
    blk = WINDOW
    nb = seq // blk
    n_ctx = k_ctx.shape[1]
    last = m // blk - 1

    def kv_spec(shift):
        return pl.BlockSpec((blk, nk), lambda b, i: (jnp.clip(b * nb + i + shift, 0, last), 0))

    ctx_spec = pl.BlockSpec((1, n_ctx, nk), lambda b, i: (b, 0, 0))
    return pl.pallas_call(
        functools.partial(_lat_attn_kernel, seq=seq),
        grid=(m // seq, nb),
        in_specs=[pl.BlockSpec(memory_space=pltpu.SMEM),
                  pl.BlockSpec((blk, nq), lambda b, i: (b * nb + i, 0)),
                  ctx_spec, ctx_spec,
                  kv_spec(-1), kv_spec(0), kv_spec(1),
                  kv_spec(-1), kv_spec(0), kv_spec(1)],
        out_specs=pl.BlockSpec((blk, nq), lambda b, i: (b * nb + i, 0)),
        out_shape=jax.ShapeDtypeStruct((m, nq), F32),
        compiler_params=_cparams("arbitrary", "arbitrary"),
        name="lat_attention",
    )(sink, q, k_ctx, v_ctx, k, k, k, v, v, v)

SSD_BLOCK = 256
SSD_PAIRS = SSD_HEADS // 2
SSD_INNER = SSD_HEADS * SSD_HEAD_DIM
HALO = SUBLANES


def _softplus(x):
    return jnp.maximum(x, 0.0) + jnp.log1p(jnp.exp(-jnp.abs(x)))


def _silu(x):
    return x * jax.nn.sigmoid(x)


def _ssd_decays(dt_raw, bias, a_log):
    n = dt_raw.shape[0]
    dt = _softplus(dt_raw + bias)
    log_a = dt * (-jnp.exp(a_log))
    r = lax.broadcasted_iota(I32, (n, n), 0)
    c = lax.broadcasted_iota(I32, (n, n), 1)
    lower = jnp.where(c <= r, 1.0, 0.0).astype(F32)
    upper = jnp.where(r <= c, 1.0, 0.0).astype(F32)
    cum_col = jnp.dot(lower, log_a, precision=HIGHEST, preferred_element_type=F32)
    dt_row = dt.T
    la_row = log_a.T
    cum_row = jnp.dot(la_row, upper, precision=HIGHEST, preferred_element_type=F32)
    return dt, log_a, cum_col, dt_row, la_row, cum_row


def _ssd_scan_chunk(xs, bmat, cmat, w_of, q_scale_of, k_scale_of, carry_of, s_ref):
    nt = (((1,), (1,)), ((), ()))
    n = xs.shape[0]
    lane = lax.broadcasted_iota(I32, (n, LANES), 1)
    lo = lane < SSD_HEAD_DIM
    lane_s = lax.broadcasted_iota(I32, (D_STATE, LANES), 1)
    lo_s = lane_s < SSD_HEAD_DIM
    b_t = bmat.T
    cb16 = cmat.astype(BF16)
    ys = []
    for pair in range(SSD_PAIRS):
        g = (2 * pair) // (SSD_HEADS // SSD_GROUPS)
        in_g = (lane // D_STATE) == g
        cg = jnp.where(in_g, cmat, 0.0)
        cb = lax.dot_general(cg.astype(BF16), bmat.astype(BF16), nt, preferred_element_type=F32)
        x_pair = xs[:, pair * LANES:(pair + 1) * LANES]
        x16 = x_pair.astype(BF16)
        s_old = s_ref[pair]
        s2 = jnp.concatenate([s_old, s_old], axis=0).astype(BF16)
        bg_t = b_t[g * D_STATE:(g + 1) * D_STATE, :]
        y_h, s_h = [], []
        for hh in range(2):
            h = 2 * pair + hh
            w = (cb * w_of(h)).astype(BF16)
            y = jnp.dot(w, x16, preferred_element_type=F32)
            cq = (cg * q_scale_of(h)).astype(BF16)
            y = y + jnp.dot(cq, s2, preferred_element_type=F32)
            y_h.append(y)
            kt = (bg_t * k_scale_of(h)).astype(BF16)
            s_h.append(carry_of(h) * s_old + jnp.dot(kt, x16, preferred_element_type=F32))
        ys.append(jnp.where(lo, y_h[0], y_h[1]))
        s_ref[pair] = jnp.where(lo_s, s_h[0], s_h[1])
    return jnp.concatenate(ys, axis=1)


def _ssd_fwd_kernel(x_ref, xp_ref, xn_ref, dt_ref, s0_ref, cw_ref, cb_ref, bias_ref, alog_ref,
                    y_ref, xc_ref, sfin_ref, s_ref):
    c = pl.program_id(1)
    nc = pl.num_programs(1)
    n = x_ref.shape[0]

    @pl.when(c == 0)
    def _():
        s_ref[...] = s0_ref[0]

    prev = jnp.where(c > 0, xp_ref[...], 0.0)
    nxt = jnp.where(c < nc - 1, xn_ref[...], 0.0)
    xe = jnp.concatenate([prev, x_ref[...], nxt], axis=0)
    pad = (CONV_K - 1) // 2
    acc = cb_ref[...] + cw_ref[0:1, :] * xe[HALO - pad:HALO - pad + n, :]
    for k in range(1, CONV_K):
        acc = acc + cw_ref[k:k + 1, :] * xe[HALO - pad + k:HALO - pad + k + n, :]
    xc = _silu(acc)
    xc_ref[...] = xc
    xs = xc[:, :SSD_INNER]
    bmat = xc[:, SSD_INNER:SSD_INNER + LANES]
    cmat = xc[:, SSD_INNER + LANES:SSD_INNER + 2 * LANES]

    dt, log_a, cum_col, dt_row, la_row, cum_row = _ssd_decays(dt_ref[...], bias_ref[...], alog_ref[...])
    r = lax.broadcasted_iota(I32, (n, n), 0)
    cc = lax.broadcasted_iota(I32, (n, n), 1)
    causal = cc <= r
    last_col = cum_col[n - 1:n, :]

    def w_of(h):
        seg = cum_col[:, h:h + 1] - cum_row[h:h + 1, :]
        return jnp.exp(jnp.where(causal, seg, NEG_BIG)) * dt_row[h:h + 1, :]

    def q_scale_of(h):
        return jnp.exp(cum_col[:, h:h + 1])

    def k_scale_of(h):
        return dt_row[h:h + 1, :] * jnp.exp(cum_row[h:h + 1, n - 1:n] - cum_row[h:h + 1, :])

    def carry_of(h):
        return jnp.exp(last_col[:, h:h + 1])

    y_ref[...] = _ssd_scan_chunk(xs, bmat, cmat, w_of, q_scale_of, k_scale_of, carry_of, s_ref)

    @pl.when(c == nc - 1)
    def _():
        sfin_ref[0] = s_ref[...]


def _ssd_bwd_kernel(xc_ref, dt_ref, yf_ref, z_ref, s0_ref, bias_ref, alog_ref, dskip_ref, nw_ref,
                    y_ref, sfin_ref, s_ref):
    c = pl.program_id(1)
    nc = pl.num_programs(1)
    n = xc_ref.shape[0]

    @pl.when(c == 0)
    def _():
        s_ref[...] = s0_ref[0]

    xc = xc_ref[...]
    xs = xc[:, :SSD_INNER]
    bmat = xc[:, SSD_INNER:SSD_INNER + LANES]
    cmat = xc[:, SSD_INNER + LANES:SSD_INNER + 2 * LANES]
    dt, log_a, cum_col, dt_row, la_row, cum_row = _ssd_decays(dt_ref[...], bias_ref[...], alog_ref[...])
    ex_col = cum_col - log_a
    ex_row = cum_row - la_row
    r = lax.broadcasted_iota(I32, (n, n), 0)
    cc = lax.broadcasted_iota(I32, (n, n), 1)
    anti = cc >= r
    tot_col = cum_col[n - 1:n, :]
    off = SSD_HEADS

    def w_of(h):
        j = off + h
        seg = ex_row[j:j + 1, :] - ex_col[:, j:j + 1]
        return jnp.exp(jnp.where(anti, seg, NEG_BIG)) * dt_row[j:j + 1, :]

    def q_scale_of(h):
        j = off + h
        return jnp.exp(tot_col[:, j:j + 1] - ex_col[:, j:j + 1])

    def k_scale_of(h):
        j = off + h
        return dt_row[j:j + 1, :] * jnp.exp(ex_row[j:j + 1, :])

    def carry_of(h):
        j = off + h
        return jnp.exp(tot_col[:, j:j + 1])

    y_b = _ssd_scan_chunk(xs, bmat, cmat, w_of, q_scale_of, k_scale_of, carry_of, s_ref)
    y = yf_ref[...] + y_b + dskip_ref[...] * xs
    y = y * _silu(z_ref[...])
    ms = jnp.mean(y * y, axis=-1, keepdims=True)
    y_ref[...] = y * lax.rsqrt(ms + NORM_EPS) * nw_ref[...]

    @pl.when(c == nc - 1)
    def _():
        sfin_ref[0] = s_ref[...]


def _pair_states(s):
    b, h, n, p = s.shape
    return s.reshape(b, h // 2, 2, n, p).transpose(0, 1, 3, 2, 4).reshape(b, h // 2, n, 2 * p)


def _unpair_states(s):
    b, hp, n, p2 = s.shape
    return s.reshape(b, hp, n, 2, p2 // 2).transpose(0, 1, 3, 2, 4).reshape(b, hp * 2, n, p2 // 2)


def _ssd(xbc, dt, z, s0_f, s0_b, conv_w, conv_b, dt_bias, a_log, d_skip, ssd_norm, seq):
    m, nx = xbc.shape
    nb = m // seq
    blk = min(SSD_BLOCK, seq)
    nc = seq // blk
    hb = blk // HALO
    n_halo = m // HALO
    pad16 = lambda a: jnp.pad(a.reshape(1, -1), ((0, 0), (0, LANES - a.size)))
    bias = pad16(dt_bias)
    alog = pad16(a_log)
    state_spec = pl.BlockSpec((1, SSD_PAIRS, D_STATE, LANES), lambda b, c: (b, 0, 0, 0))
    state_shape = jax.ShapeDtypeStruct((nb, SSD_PAIRS, D_STATE, LANES), F32)
    row = lambda width: pl.BlockSpec((1, width), lambda b, c: (0, 0))

    def fwd_rows(width):
        return pl.BlockSpec((blk, width), lambda b, c: (b * nc + c, 0))

    def bwd_rows(width):
        return pl.BlockSpec((blk, width), lambda b, c: (b * nc + nc - 1 - c, 0))

    y_f, xc, s_f = pl.pallas_call(
        _ssd_fwd_kernel,
        grid=(nb, nc),
        in_specs=[fwd_rows(nx),
                  pl.BlockSpec((HALO, nx), lambda b, c: (jnp.maximum((b * nc + c) * hb - 1, 0), 0)),
                  pl.BlockSpec((HALO, nx), lambda b, c: (jnp.minimum((b * nc + c + 1) * hb, n_halo - 1), 0)),
                  fwd_rows(LANES), state_spec,
                  pl.BlockSpec((CONV_K, nx), lambda b, c: (0, 0)), row(nx), row(LANES), row(LANES)],
        out_specs=[fwd_rows(SSD_INNER), fwd_rows(nx), state_spec],
        out_shape=[jax.ShapeDtypeStruct((m, SSD_INNER), F32), jax.ShapeDtypeStruct((m, nx), F32), state_shape],
        scratch_shapes=[pltpu.VMEM((SSD_PAIRS, D_STATE, LANES), F32)],
        compiler_params=_cparams("arbitrary", "arbitrary"),
        name="ssd_forward",
    )(xbc, xbc, xbc, dt, _pair_states(s0_f), conv_w, conv_b.reshape(1, nx), bias, alog)

    dskip = jnp.repeat(d_skip, SSD_HEAD_DIM).reshape(1, SSD_INNER)
    y, s_b = pl.pallas_call(
        _ssd_bwd_kernel,
        grid=(nb, nc),
        in_specs=[bwd_rows(nx), bwd_rows(LANES), bwd_rows(SSD_INNER), bwd_rows(SSD_INNER), state_spec,
                  row(LANES), row(LANES), row(SSD_INNER), row(SSD_INNER)],
        out_specs=[bwd_rows(SSD_INNER), state_spec],
        out_shape=[jax.ShapeDtypeStruct((m, SSD_INNER), F32), state_shape],
        scratch_shapes=[pltpu.VMEM((SSD_PAIRS, D_STATE, LANES), F32)],
        compiler_params=_cparams("arbitrary", "arbitrary"),
        name="ssd_backward",
    )(xc, dt, y_f, z, _pair_states(s0_b), bias, alog, dskip, ssd_norm.reshape(1, SSD_INNER))
    return y, _unpair_states(s_f), _unpair_states(s_b)

def _hgrn_kernel(q_ref, ff_ref, fb_ref, i_ref, g_ref, lb_ref, s0_ref, nw_ref, o_ref, sfin_ref,
                 s_ref, *, layer):
    t_len = q_ref.shape[0]
    n = HGRN_CHUNK
    n_chunks = t_len // n
    tn = (((0,), (0,)), ((), ()))
    nt = (((1,), (1,)), ((), ()))

    lbp = lb_ref[...]
    e = jnp.exp(lbp - jnp.max(lbp, axis=0, keepdims=True))
    sm = e / jnp.sum(e, axis=0, keepdims=True)
    lb = sm[0] * 0.0
    for j in range(1, layer + 1):
        lb = lb + sm[j]

    r = lax.broadcasted_iota(I32, (n, n), 0)
    c = lax.broadcasted_iota(I32, (n, n), 1)
    lower = jnp.where(c <= r, 1.0, 0.0).astype(F32)
    srow = lax.broadcasted_iota(I32, (n, HGRN_DK), 0)
    qscale = HGRN_DK ** -0.5

    def chunk(row0, f_ref, lb_d, reverse):
        q = _silu(q_ref[pl.ds(row0, n), :]) * qscale
        f = f_ref[pl.ds(row0, n), :]
        v = i_ref[pl.ds(row0, n), :]
        k = (1.0 - lb_d) * jax.nn.sigmoid(-f)
        lf = jnp.log(lb_d + (1.0 - lb_d) * jax.nn.sigmoid(f))
        cum = jnp.dot(lower, lf, precision=HIGHEST, preferred_element_type=F32)
        tot = cum[n - 1:n, :]
        if reverse:
            cum = cum - lf
        rows = []
        for t in range(n):
            if reverse:
                seg = jnp.where(srow >= t, cum - cum[t:t + 1, :], NEG_BIG)
            else:
                seg = jnp.where(srow <= t, cum[t:t + 1, :] - cum, NEG_BIG)
            a = q[t:t + 1, :] * k * jnp.exp(seg)
            sc = jnp.sum(a, axis=1, keepdims=True)
            rows.append(jnp.sum(sc * v, axis=0, keepdims=True))
        o = jnp.concatenate(rows, axis=0)
        s_old = s_ref[...]
        if reverse:
            q_in = q * jnp.exp(tot - cum)
            k_out = k * jnp.exp(cum)
        else:
            q_in = q * jnp.exp(cum)
            k_out = k * jnp.exp(tot - cum)
        o = o + lax.dot_general(q_in.astype(BF16), s_old.astype(BF16), nt, preferred_element_type=F32)
        s_ref[...] = jnp.exp(tot) * s_old + lax.dot_general(
            v.astype(BF16), k_out.astype(BF16), tn, preferred_element_type=F32)
        return o

    s_ref[...] = s0_ref[0, 0, 0].T

    def fwd_body(ci, carry):
        row0 = pl.multiple_of(ci * n, n)
        o_ref[pl.ds(row0, n), :] = chunk(row0, ff_ref, lb[0:1, :], False)
        return carry

    lax.fori_loop(0, n_chunks, fwd_body, 0)
    sfin_ref[0, 0, 0] = s_ref[...].T

    s_ref[...] = s0_ref[0, 1, 0].T
    nw = nw_ref[...]

    def bwd_body(ci, carry):
        row0 = pl.multiple_of((n_chunks - 1 - ci) * n, n)
        o = o_ref[pl.ds(row0, n), :] + chunk(row0, fb_ref, lb[1:2, :], True)
        ms = jnp.mean(o * o, axis=-1, keepdims=True)
        o = o * lax.rsqrt(ms + NORM_EPS) * nw
        o_ref[pl.ds(row0, n), :] = o * _silu(g_ref[pl.ds(row0, n), :])
        return carry

    lax.fori_loop(0, n_chunks, bwd_body, 0)
    sfin_ref[0, 1, 0] = s_ref[...].T


def _hgrn(q, f_fw, f_bw, iv, g, o_lb, state0, g_norm, seq, layer):
    m, width = q.shape
    nb = m // seq
    dv = width // HGRN_HEADS
    col = pl.BlockSpec((seq, dv), lambda b, h: (b, h))
    state_spec = pl.BlockSpec((1, 2, 1, HGRN_DK, dv), lambda b, h: (b, 0, h, 0, 0))
    return pl.pallas_call(
        functools.partial(_hgrn_kernel, layer=layer),
        grid=(nb, HGRN_HEADS),
        in_specs=[col, col, col, col, col,
                  pl.BlockSpec((o_lb.shape[0], 2, HGRN_DK), lambda b, h: (0, 0, h)),
                  state_spec,
                  pl.BlockSpec((1, dv), lambda b, h: (0, 0))],
        out_specs=[col, state_spec],
        out_shape=[jax.ShapeDtypeStruct((m, width), F32),
                   jax.ShapeDtypeStruct((nb, 2, HGRN_HEADS, HGRN_DK, dv), F32)],
        scratch_shapes=[pltpu.VMEM((dv, HGRN_DK), F32)],
        compiler_params=_cparams("arbitrary", "arbitrary"),
        name="hgrn2",
    )(q, f_fw, f_bw, iv, g, o_lb, state0, g_norm.reshape(1, dv))

EVEN_SPLITS = ((0, 512), (512, 640), (640, 768), (768, 1280), (1280, 2048), (2048, 2176))
HGRN_SPLITS = tuple((i * 1024, (i + 1) * 1024) for i in range(5))


def _even_weight(w):
    main = EVEN_SPLITS[-1][0]
    return jnp.pad(w, ((0, 0), (0, LANES - (w.shape[1] - main)))).astype(BF16)


def _run_trunk(x3, mods, mod_row0, P, cache):
    nb, seq, d = x3.shape
    x = x3.reshape(nb * seq, d)
    depth = P['norm_mix'].shape[0]
    ks, vs, ssd_states, hgrn_states = [], [], [], []
    for l in range(depth):
        j = l // 2
        row0 = (l * SUBLANES + mod_row0, 0 if cache is None else 1)
        if l % 2 == 0:
            q, k, v, z, xbc, dt = _inproj(x, mods, row0, seq, P['norm_mix'][l], P['e_w_in'][j],
                                          EVEN_SPLITS, "even_in_proj")
            q, k = _qkprep(q, k, P['e_q_norm'][j], P['e_k_norm'][j], seq, rope=cache is not None)
            if cache is None:
                s0_f = jnp.zeros((nb, SSD_HEADS, D_STATE, SSD_HEAD_DIM), F32)
                s0_b = s0_f
                o_attn = _ctx_attention(q, k, v, P['e_sink'][j], seq)
            else:
                s0_f, s0_b = cache[2][:, j, 0], cache[2][:, j, 1]
                n_ctx = cache[0].shape[2]
                o_attn = _lat_attention(q, k, v, cache[0][:, j].reshape(nb, n_ctx, -1),
                                        cache[1][:, j].reshape(nb, n_ctx, -1), P['e_sink'][j], seq)
            y, s_f, s_b = _ssd(xbc, dt, z, s0_f, s0_b, P['e_conv_w'][j], P['e_conv_b'][j],
                               P['e_dt_bias'][j], P['e_a_log'][j], P['e_d_skip'][j], P['e_ssd_norm'][j], seq)
            if cache is None:
                ks.append(k.reshape(nb, seq, N_KV_HEADS, HEAD_DIM))
                vs.append(v.reshape(nb, seq, N_KV_HEADS, HEAD_DIM))
                ssd_states.append(jnp.stack([s_f, s_b], axis=1))
            mix = jnp.concatenate([o_attn, y], axis=1)
            x = _outproj(mix, x, mods, row0, seq, P['e_w_out'][j], "even_out_proj")
        else:
            q, f_fw, f_bw, iv, g = _inproj(x, mods, row0, seq, P['norm_mix'][l], P['o_w_in'][j],
                                           HGRN_SPLITS, "odd_in_proj")
            if cache is None:
                s0 = jnp.zeros((nb, 2, HGRN_HEADS, HGRN_DK, d // HGRN_HEADS), F32)
            else:
                s0 = cache[3][:, j]
            o, s_new = _hgrn(q, f_fw, f_bw, iv, g, P['o_lb'], s0, P['o_g_norm'][j], seq, j)
            if cache is None:
                hgrn_states.append(s_new)
            x = _outproj(o, x, mods, row0, seq, P['o_w_out'][j], "odd_out_proj")
        x = _peer(x, mods, row0, seq, l, P['norm_ffn'][l], P['p_w_q'][l], P['p_sub_keys'][l], P['p_uv'])
    y = x.reshape(nb, seq, d)
    if cache is not None:
        return y, None
    return y, (jnp.stack(ks, axis=1), jnp.stack(vs, axis=1),
               jnp.stack(ssd_states, axis=1), jnp.stack(hgrn_states, axis=1))


def kernel(x_prompt, x_sample, cache_k, cache_v, state_ssd, state_hgrn, c, c_ctx, w_ada, b_ada, norm_mix, norm_ffn, e_w_in, e_q_norm, e_k_norm, e_sink, e_conv_w, e_conv_b, e_dt_bias, e_a_log, e_d_skip, e_ssd_norm, e_w_out, o_w_in, o_lb, o_g_norm, o_w_out, p_w_q, p_sub_keys, p_u, p_v):
    depth, d, d6 = w_ada.shape
    b_lat = x_sample.shape[0]
    cond_rows = jnp.concatenate([c_ctx[None, :], c, jnp.zeros((SUBLANES - 1 - b_lat, d), F32)], axis=0)
    mods = _modulation(cond_rows, w_ada, b_ada).reshape(depth * SUBLANES, 1, d6)
    P = {
        'norm_mix': norm_mix, 'norm_ffn': norm_ffn,
        'e_w_in': jnp.stack([_even_weight(w) for w in e_w_in]), 'e_q_norm': e_q_norm, 'e_k_norm': e_k_norm,
        'e_sink': e_sink, 'e_conv_w': e_conv_w, 'e_conv_b': e_conv_b, 'e_dt_bias': e_dt_bias,
        'e_a_log': e_a_log, 'e_d_skip': e_d_skip, 'e_ssd_norm': e_ssd_norm,
        'e_w_out': e_w_out.astype(BF16),
        'o_w_in': o_w_in.astype(BF16), 'o_lb': o_lb, 'o_g_norm': o_g_norm, 'o_w_out': o_w_out.astype(BF16),
        'p_w_q': p_w_q.astype(BF16),
        'p_sub_keys': p_sub_keys.astype(BF16).reshape(depth, PEER_HEADS * 2, PEER_NKEYS, PEER_DKEY),
        'p_uv': jnp.stack([p_u, p_v], axis=2).reshape(depth, p_u.shape[1] * 2 * d // LANES, 1, LANES),
    }
    y_prompt, new_state = _run_trunk(x_prompt, mods, 0, P, None)
    y_sample, _ = _run_trunk(x_sample, mods, 1, P, (cache_k, cache_v, state_ssd, state_hgrn))
    return (y_prompt, y_sample) + new_state
```

```python
import functools
import math

import jax
import jax.numpy as jnp
from jax import lax
from jax.experimental import pallas as pl
from jax.experimental.pallas import tpu as pltpu
from jax.experimental.pallas import tpu_sc as plsc

F32 = jnp.float32
BF16 = jnp.bfloat16
I32 = jnp.int32
HIGHEST = lax.Precision.HIGHEST

NORM_EPS = 1e-6
NEG_BIG = -1e30
LANES = 128
SUBLANES = 8
VMEM_LIMIT = 48 * 1024 * 1024

GRID_W = 64
HEAD_DIM = 64
N_Q_HEADS = 8
N_KV_HEADS = 2
GQA_GROUP = 4
WINDOW = 128
ROPE_THETA = 10000.0
SSD_HEADS = 8
SSD_HEAD_DIM = 64
SSD_GROUPS = 2
D_STATE = 64
CONV_K = 5
HGRN_HEADS = 8
HGRN_DK = 128
HGRN_CHUNK = 32
PEER_HEADS = 8
PEER_NKEYS = 128
PEER_TOPK = 16
PEER_DKEY = 128
PEER_PAIRS = PEER_HEADS * PEER_TOPK

ROW_TILE = 256
PEER_BLOCK = 128
SC_LANES = 16
SC_CHUNK = 16
SC_SHARE = (1, 2)
GELU_C = math.sqrt(2.0 / math.pi)


def _cparams(*sem):
    return pltpu.CompilerParams(dimension_semantics=sem, vmem_limit_bytes=VMEM_LIMIT)


def _norm_mod(x, nw, scale, shift):
    ms = jnp.mean(x * x, axis=-1, keepdims=True)
    return (x * lax.rsqrt(ms + NORM_EPS)) * nw * (1.0 + scale) + shift


def _mod_kernel(c_ref, w_ref, b_ref, o_ref):
    c = c_ref[...]
    s = c * jax.nn.sigmoid(c)
    o_ref[0] = jnp.dot(s, w_ref[0], precision=HIGHEST, preferred_element_type=F32) + b_ref[0]


def _modulation(cond_rows, w_ada, b_ada):
    depth, d, n = w_ada.shape
    rows = cond_rows.shape[0]
    return pl.pallas_call(
        _mod_kernel,
        grid=(depth, n // d),
        in_specs=[pl.BlockSpec((rows, d), lambda l, j: (0, 0)),
                  pl.BlockSpec((1, d, d), lambda l, j: (l, 0, j)),
                  pl.BlockSpec((1, 1, d), lambda l, j: (l, 0, j))],
        out_specs=pl.BlockSpec((1, rows, d), lambda l, j: (l, 0, j)),
        out_shape=jax.ShapeDtypeStruct((depth, rows, n), F32),
        compiler_params=_cparams("arbitrary", "arbitrary"),
        name="modulation",
    )(cond_rows, w_ada, b_ada.reshape(depth, 1, n))


def _mod_spec(mod_row0, seq, tile, d6, first_tile=0):
    row0, per_batch = mod_row0
    return pl.BlockSpec((1, 1, d6),
                        lambda i: (row0 + per_batch * (((i + first_tile) * tile) // seq), 0, 0))


def _inproj_kernel(x_ref, m_ref, nw_ref, w_ref, *o_refs, splits, d):
    m = m_ref[0]
    h = _norm_mod(x_ref[...], nw_ref[...], m[:, d:2 * d], m[:, 0:d]).astype(BF16)
    for o_ref, (a, b) in zip(o_refs, splits):
        o_ref[...] = jnp.dot(h, w_ref[:, a:b], preferred_element_type=F32)


def _inproj(x, mods, mod_row0, seq, nw, w_bf16, splits, name):
    m, d = x.shape
    n = w_bf16.shape[1]
    tile = min(ROW_TILE, seq)
    return pl.pallas_call(
        functools.partial(_inproj_kernel, splits=splits, d=d),
        grid=(m // tile,),
        in_specs=[pl.BlockSpec((tile, d), lambda i: (i, 0)),
                  _mod_spec(mod_row0, seq, tile, mods.shape[-1]),
                  pl.BlockSpec((1, d), lambda i: (0, 0)),
                  pl.BlockSpec((d, n), lambda i: (0, 0))],
        out_specs=[pl.BlockSpec((tile, b - a), lambda i: (i, 0)) for a, b in splits],
        out_shape=[jax.ShapeDtypeStruct((m, b - a), F32) for a, b in splits],
        compiler_params=_cparams("arbitrary"),
        name=name,
    )(x, mods, nw.reshape(1, d), w_bf16)


def _outproj_kernel(mix_ref, x_ref, m_ref, w_ref, o_ref, *, d):
    y = jnp.dot(mix_ref[...].astype(BF16), w_ref[...], preferred_element_type=F32)
    o_ref[...] = x_ref[...] + m_ref[0][:, 2 * d:3 * d] * y


def _outproj(mix, x, mods, mod_row0, seq, w_bf16, name):
    m, d = x.shape
    k = mix.shape[1]
    tile = min(ROW_TILE, seq)
    return pl.pallas_call(
        functools.partial(_outproj_kernel, d=d),
        grid=(m // tile,),
        in_specs=[pl.BlockSpec((tile, k), lambda i: (i, 0)),
                  pl.BlockSpec((tile, d), lambda i: (i, 0)),
                  _mod_spec(mod_row0, seq, tile, mods.shape[-1]),
                  pl.BlockSpec((k, d), lambda i: (0, 0))],
        out_specs=pl.BlockSpec((tile, d), lambda i: (i, 0)),
        out_shape=jax.ShapeDtypeStruct((m, d), F32),
        compiler_params=_cparams("arbitrary"),
        name=name,
    )(mix, x, mods, w_bf16)


def _topk_over_rows(s, k, payload=None):
    n = s.shape[0]
    iota = lax.broadcasted_iota(I32, s.shape, 0)
    vals, idxs, pays = [], [], []
    for _ in range(k):
        m = jnp.max(s, axis=0, keepdims=True)
        i = jnp.min(jnp.where(s == m, iota, n), axis=0, keepdims=True)
        hit = iota == i
        vals.append(m)
        idxs.append(i)
        if payload is not None:
            pays.append(jnp.max(jnp.where(hit, payload, -1), axis=0, keepdims=True))
        s = jnp.where(hit, -jnp.inf, s)
    out = (jnp.concatenate(vals, axis=0), jnp.concatenate(idxs, axis=0))
    if payload is not None:
        out += (jnp.concatenate(pays, axis=0),)
    return out


def _peer_route_kernel(x_ref, m_ref, nw_ref, wq_ref, keys_ref, h_ref, e_ref, g_ref, *, d):
    m = m_ref[0]
    h = _norm_mod(x_ref[...], nw_ref[...], m[:, 4 * d:5 * d], m[:, 3 * d:4 * d])
    h_ref[...] = h
    hb = h.astype(BF16)
    nt = (((1,), (1,)), ((), ()))
    for head in range(PEER_HEADS):
        tops = []
        for half in range(2):
            c0 = (head * 2 + half) * PEER_DKEY
            q = jnp.dot(hb, wq_ref[:, c0:c0 + PEER_DKEY], preferred_element_type=F32)
            s = lax.dot_general(keys_ref[head * 2 + half], q.astype(BF16), nt,
                                preferred_element_type=F32)
            tops.append(_topk_over_rows(s, PEER_TOPK))
        (s0, i0), (s1, i1) = tops
        cand_s = jnp.concatenate([s0[a:a + 1] + s1 for a in range(PEER_TOPK)], axis=0)
        cand_e = jnp.concatenate([i0[a:a + 1] * PEER_NKEYS + i1 for a in range(PEER_TOPK)], axis=0)
        best_s, _, best_e = _topk_over_rows(cand_s, PEER_TOPK, payload=cand_e)
        p = jnp.exp(best_s - best_s[0:1])
        r0 = head * PEER_TOPK
        e_ref[r0:r0 + PEER_TOPK, :] = best_e
        g_ref[r0:r0 + PEER_TOPK, :] = p / jnp.sum(p, axis=0, keepdims=True)


def _peer_route(x, mods, mod_row0, seq, nw, wq_bf16, keys_bf16):
    m, d = x.shape
    tile = min(ROW_TILE, seq)
    nq = wq_bf16.shape[1]
    return pl.pallas_call(
        functools.partial(_peer_route_kernel, d=d),
        grid=(m // tile,),
        in_specs=[pl.BlockSpec((tile, d), lambda i: (i, 0)),
                  _mod_spec(mod_row0, seq, tile, mods.shape[-1]),
                  pl.BlockSpec((1, d), lambda i: (0, 0)),
                  pl.BlockSpec((d, nq), lambda i: (0, 0)),
                  pl.BlockSpec(keys_bf16.shape, lambda i: (0, 0, 0))],
        out_specs=[pl.BlockSpec((tile, d), lambda i: (i, 0)),
                   pl.BlockSpec((PEER_PAIRS, tile), lambda i: (0, i)),
                   pl.BlockSpec((PEER_PAIRS, tile), lambda i: (0, i))],
        out_shape=[jax.ShapeDtypeStruct((m, d), F32),
                   jax.ShapeDtypeStruct((PEER_PAIRS, m), I32),
                   jax.ShapeDtypeStruct((PEER_PAIRS, m), F32)],
        compiler_params=_cparams("arbitrary"),
        name="peer_route",
    )(x, mods, nw.reshape(1, d), wq_bf16, keys_bf16)


def _peer_gather_kernel(idx_hbm, h_ref, g_ref, x_ref, m_ref, uv_hbm, o_ref,
                        idx_smem, buf, sem_idx, sem, *, layer, d):
    blk = pl.program_id(0)
    n_groups = PEER_BLOCK // SUBLANES
    chunks = d // LANES
    span = 2 * chunks

    cp = pltpu.make_async_copy(idx_hbm.at[blk], idx_smem, sem_idx)
    cp.start()
    cp.wait()

    def issue(tok, slot):
        def body(pair, carry):
            row = pl.multiple_of(idx_smem[tok, pair], span)
            pltpu.make_async_copy(uv_hbm.at[layer, pl.ds(row, span)],
                                  buf.at[slot, :, pl.ds(pair, 1), :],
                                  sem.at[slot]).start()
            return carry
        lax.fori_loop(0, PEER_PAIRS, body, 0, unroll=8)

    def wait(slot):
        pltpu.make_async_copy(buf.at[1 - slot], buf.at[slot], sem.at[slot]).wait()

    def rows_of(slot, first):
        return jnp.concatenate([buf[slot, first + c] for c in range(chunks)], axis=1)

    gate2 = m_ref[0][:, 5 * d:6 * d]
    lane = lax.broadcasted_iota(I32, (PEER_PAIRS, PEER_BLOCK), 1)

    issue(0, 0)

    def group(grp, carry):
        base = pl.multiple_of(grp * SUBLANES, SUBLANES)
        h8 = h_ref[pl.ds(base, SUBLANES), :]
        rows = []
        for r in range(SUBLANES):
            tok = base + r
            slot = r % 2
            if r < SUBLANES - 1:
                issue(tok + 1, 1 - slot)
            else:
                @pl.when(grp < n_groups - 1)
                def _():
                    issue(tok + 1, 1 - slot)
            wait(slot)
            act = jnp.sum(rows_of(slot, 0) * h8[r:r + 1, :], axis=1, keepdims=True)
            gate = jnp.sum(jnp.where(lane == tok, g_ref[...], 0.0), axis=1, keepdims=True)
            w = jax.nn.gelu(act) * gate
            rows.append(jnp.sum(rows_of(slot, chunks) * w, axis=0, keepdims=True))
        out8 = jnp.concatenate(rows, axis=0)
        o_ref[pl.ds(base, SUBLANES), :] = x_ref[pl.ds(base, SUBLANES), :] + gate2 * out8
        return carry

    lax.fori_loop(0, n_groups, group, 0)


def _peer_gather(idx, h, gates, x, mods, mod_row0, seq, p_uv, layer, first_block):
    m, d = x.shape
    nblk = m // PEER_BLOCK - first_block
    span = 2 * d // LANES
    idx3 = (span * idx[:, first_block * PEER_BLOCK:]).T.reshape(nblk, PEER_BLOCK, PEER_PAIRS)
    rows = pl.BlockSpec((PEER_BLOCK, d), lambda i: (i + first_block, 0))
    return pl.pallas_call(
        functools.partial(_peer_gather_kernel, layer=layer, d=d),
        grid=(nblk,),
        in_specs=[pl.BlockSpec(memory_space=pl.ANY),
                  rows,
                  pl.BlockSpec((PEER_PAIRS, PEER_BLOCK), lambda i: (0, i + first_block)),
                  rows,
                  _mod_spec(mod_row0, seq, PEER_BLOCK, mods.shape[-1], first_block),
                  pl.BlockSpec(memory_space=pl.ANY)],
        out_specs=pl.BlockSpec((PEER_BLOCK, d), lambda i: (i, 0)),
        out_shape=jax.ShapeDtypeStruct((nblk * PEER_BLOCK, d), F32),
        scratch_shapes=[pltpu.SMEM((PEER_BLOCK, PEER_PAIRS), I32),
                        pltpu.VMEM((2, span, PEER_PAIRS, LANES), F32),
                        pltpu.SemaphoreType.DMA,
                        pltpu.SemaphoreType.DMA((2,))],
        compiler_params=_cparams("arbitrary"),
        name="peer_gather",
    )(idx3, h, gates, x, mods, p_uv)


def _peer_experts_sc(idx, h, gates, u_rows, v_rows):
    m = idx.shape[0]
    d = h.shape[1]
    info = plsc.get_sparse_core_info()
    n_workers = info.num_cores * info.num_subcores
    per = m // n_workers
    n_chunks = PEER_PAIRS // SC_CHUNK
    n_vec = d // SC_LANES
    mesh = plsc.VectorSubcoreMesh(core_axis_name="c", subcore_axis_name="s")

    @functools.partial(
        pl.kernel, out_type=jax.ShapeDtypeStruct((m, d), F32), mesh=mesh,
        scratch_types=[pltpu.VMEM((2, PEER_PAIRS), I32), pltpu.VMEM((2, d), F32),
                       pltpu.VMEM((2, PEER_PAIRS), F32), pltpu.VMEM((2, d), F32),
                       pltpu.VMEM((2, SC_CHUNK, d), F32), pltpu.VMEM((2, SC_CHUNK, d), F32),
                       pltpu.SemaphoreType.DMA((2,)), pltpu.SemaphoreType.DMA((2,)),
                       pltpu.SemaphoreType.DMA((2,)), pltpu.SemaphoreType.DMA((2,))],
        compiler_params=pltpu.CompilerParams(needs_layout_passes=False),
        name="peer_experts_sc")
    def body(idx_hbm, h_hbm, g_hbm, u_hbm, v_hbm, o_hbm,
             idx_v, x_v, g_v, out_v, ubuf, vbuf, sem_meta, sem_out, sem_u, sem_v):
        wid = lax.axis_index("c") * info.num_subcores + lax.axis_index("s")
        tok0 = wid * per
        lane = lax.iota(I32, SC_LANES)

        def meta_copies(ti, ms):
            t = tok0 + ti
            return (pltpu.make_async_copy(idx_hbm.at[t], idx_v.at[ms], sem_meta.at[ms]),
                    pltpu.make_async_copy(h_hbm.at[t], x_v.at[ms], sem_meta.at[ms]),
                    pltpu.make_async_copy(g_hbm.at[t], g_v.at[ms], sem_meta.at[ms]))

        def gather_copies(ms, c, slot):
            ids = idx_v.at[ms, pl.ds(c * SC_CHUNK, SC_CHUNK)]
            return (pltpu.make_async_copy(u_hbm.at[ids], ubuf.at[slot], sem_u.at[slot]),
                    pltpu.make_async_copy(v_hbm.at[ids], vbuf.at[slot], sem_v.at[slot]))

        def out_copy(ti, ms):
            return pltpu.make_async_copy(out_v.at[ms], o_hbm.at[tok0 + ti], sem_out.at[ms])

        for cp in meta_copies(0, 0):
            cp.start()
        for cp in meta_copies(0, 0):
            cp.wait()
        for cp in gather_copies(0, 0, 0):
            cp.start()

        def token(ti, carry):
            ms = ti % 2
            nxt = 1 - ms

            @pl.when(ti + 1 < per)
            def _():
                for cp in meta_copies(ti + 1, nxt):
                    cp.start()

            @pl.when(ti >= 2)
            def _():
                out_copy(ti - 2, ms).wait()

            def zero(j, c):
                out_v[ms, pl.ds(j * SC_LANES, SC_LANES)] = jnp.zeros((SC_LANES,), F32)
                return c
            lax.fori_loop(0, n_vec, zero, 0)

            for c in range(n_chunks):
                slot = c % 2
                if c + 1 < n_chunks:
                    for cp in gather_copies(ms, c + 1, 1 - slot):
                        cp.start()
                else:
                    @pl.when(ti + 1 < per)
                    def _():
                        for cp in meta_copies(ti + 1, nxt):
                            cp.wait()
                        for cp in gather_copies(nxt, 0, 1 - slot):
                            cp.start()
                cu, cv = gather_copies(ms, c, slot)
                cu.wait()

                def udot(j, accs):
                    xj = x_v[ms, pl.ds(j * SC_LANES, SC_LANES)]
                    return tuple(accs[r] + ubuf[slot, r, pl.ds(j * SC_LANES, SC_LANES)] * xj
                                 for r in range(SC_CHUNK))
                accs = lax.fori_loop(0, n_vec, udot,
                                     tuple(jnp.zeros((SC_LANES,), F32) for _ in range(SC_CHUNK)))
                act = jnp.zeros((SC_LANES,), F32)
                for r in range(SC_CHUNK):
                    act = jnp.where(lane == r, jnp.sum(accs[r]), act)
                y = GELU_C * (act + 0.044715 * (act * act * act))
                w = act / (1.0 + jnp.exp(-2.0 * y)) * g_v[ms, pl.ds(c * SC_CHUNK, SC_CHUNK)]
                ws = [jnp.sum(jnp.where(lane == r, w, 0.0)) for r in range(SC_CHUNK)]
                cv.wait()

                @plsc.parallel_loop(0, n_vec, unroll=2)
                def _(j):
                    parts = [ws[r] * vbuf[slot, r, pl.ds(j * SC_LANES, SC_LANES)] for r in range(SC_CHUNK)]
                    while len(parts) > 1:
                        parts = [parts[i] + parts[i + 1] for i in range(0, len(parts), 2)]
                    plsc.addupdate(out_v.at[ms, pl.ds(j * SC_LANES, SC_LANES)], parts[0])

            out_copy(ti, ms).start()
            return carry

        lax.fori_loop(0, per, token, 0)
        for back in (2, 1):
            if per >= back:
                out_copy(per - back, (per - back) % 2).wait()

    return body(idx, h, gates, u_rows, v_rows)


def _residual_kernel(x_ref, y_ref, m_ref, o_ref, *, d):
    o_ref[...] = x_ref[...] + m_ref[0][:, 5 * d:6 * d] * y_ref[...]


def _residual(x, y, mods, mod_row0, seq):
    m, d = y.shape
    tile = min(ROW_TILE, seq)
    return pl.pallas_call(
        functools.partial(_residual_kernel, d=d),
        grid=(m // tile,),
        in_specs=[pl.BlockSpec((tile, d), lambda i: (i, 0)),
                  pl.BlockSpec((tile, d), lambda i: (i, 0)),
                  _mod_spec(mod_row0, seq, tile, mods.shape[-1])],
        out_specs=pl.BlockSpec((tile, d), lambda i: (i, 0)),
        out_shape=jax.ShapeDtypeStruct((m, d), F32),
        compiler_params=_cparams("arbitrary"),
        name="peer_residual",
    )(x, y, mods)


def _peer(x, mods, mod_row0, seq, layer, nw, wq_bf16, keys_bf16, p_uv, p_u, p_v):
    m, d = x.shape
    h, idx, gates = _peer_route(x, mods, mod_row0, seq, nw, wq_bf16, keys_bf16)
    n_blocks = m // PEER_BLOCK
    sc_blocks = (n_blocks * SC_SHARE[0]) // SC_SHARE[1]
    m_sc = sc_blocks * PEER_BLOCK
    n_experts = p_u.shape[1]
    idx_sc = idx[:, :m_sc].T + layer * n_experts
    y_sc = _peer_experts_sc(idx_sc, h, gates[:, :m_sc].T,
                            p_u.reshape(-1, d), p_v.reshape(-1, d))
    x_sc = _residual(x, y_sc, mods, mod_row0, seq)
    x_tc = _peer_gather(idx, h, gates, x, mods, mod_row0, seq, p_uv, layer, sc_blocks)
    return jnp.concatenate([x_sc, x_tc], axis=0)


def _head_mean_square(x):
    n = x.shape[1]
    r = lax.broadcasted_iota(I32, (n, n), 0) // HEAD_DIM
    c = lax.broadcasted_iota(I32, (n, n), 1) // HEAD_DIM
    seg = jnp.where(r == c, 1.0 / HEAD_DIM, 0.0).astype(F32)
    return jnp.dot(x * x, seg, precision=HIGHEST, preferred_element_type=F32)


def _swap_rot_halves(x):
    n = x.shape[1]
    quarter = HEAD_DIM // 4
    lane = lax.broadcasted_iota(I32, x.shape, 1)
    lo = (lane % (2 * quarter)) < quarter
    return jnp.where(lo, pltpu.roll(x, n - quarter, axis=1), pltpu.roll(x, quarter, axis=1))


def _qkprep_kernel(q_ref, k_ref, qw_ref, kw_ref, *rest, rope):
    if rope:
        cos_ref, sin_ref, qo_ref, ko_ref = rest
    else:
        qo_ref, ko_ref = rest
    q = q_ref[...]
    k = k_ref[...]
    q = q * lax.rsqrt(_head_mean_square(q) + NORM_EPS) * qw_ref[...]
    k = k * lax.rsqrt(_head_mean_square(k) + NORM_EPS) * kw_ref[...]
    if rope:
        cos = cos_ref[...]
        sin = sin_ref[...]
        cq = jnp.concatenate([cos] * (q.shape[1] // LANES), axis=1)
        sq = jnp.concatenate([sin] * (q.shape[1] // LANES), axis=1)
        q = q * cq + _swap_rot_halves(q) * sq
        k = k * cos + _swap_rot_halves(k) * sin
    qo_ref[...] = q
    ko_ref[...] = k


def _rope_tables(seq):
    axis_dim = HEAD_DIM // 2
    inv_freq = ROPE_THETA ** (-jnp.arange(0, axis_dim, 2, dtype=F32) / axis_dim)
    t = jnp.arange(seq)
    pos = jnp.stack([(t // GRID_W).astype(F32), (t % GRID_W).astype(F32)], axis=1)
    lane = jnp.arange(LANES)
    dd = lane % HEAD_DIM
    ang = pos[:, dd // axis_dim] * inv_freq[dd % (axis_dim // 2)][None, :]
    sign = jnp.where((dd % axis_dim) < axis_dim // 2, -1.0, 1.0).astype(F32)
    return jnp.cos(ang), jnp.sin(ang) * sign[None, :]


def _qkprep(q, k, qw, kw, seq, rope):
    m, nq = q.shape
    nk = k.shape[1]
    tile = min(ROW_TILE, seq)
    qw_row = jnp.tile(qw, nq // HEAD_DIM).reshape(1, nq)
    kw_row = jnp.tile(kw, nk // HEAD_DIM).reshape(1, nk)
    in_specs = [pl.BlockSpec((tile, nq), lambda i: (i, 0)),
                pl.BlockSpec((tile, nk), lambda i: (i, 0)),
                pl.BlockSpec((1, nq), lambda i: (0, 0)),
                pl.BlockSpec((1, nk), lambda i: (0, 0))]
    args = [q, k, qw_row, kw_row]
    if rope:
        cos, sin = _rope_tables(seq)
        per_seq = seq // tile
        in_specs += [pl.BlockSpec((tile, LANES), lambda i: (i % per_seq, 0)),
                     pl.BlockSpec((tile, LANES), lambda i: (i % per_seq, 0))]
        args += [cos, sin]
    return pl.pallas_call(
        functools.partial(_qkprep_kernel, rope=rope),
        grid=(m // tile,),
        in_specs=in_specs,
        out_specs=[pl.BlockSpec((tile, nq), lambda i: (i, 0)),
                   pl.BlockSpec((tile, nk), lambda i: (i, 0))],
        out_shape=[jax.ShapeDtypeStruct((m, nq), F32), jax.ShapeDtypeStruct((m, nk), F32)],
        compiler_params=_cparams("arbitrary"),
        name="qk_prep",
    )(*args)


def _dup_halves(x):
    lane = lax.broadcasted_iota(I32, x.shape, 1)
    sw = pltpu.roll(x, HEAD_DIM, axis=1)
    lo = lane < HEAD_DIM
    return jnp.where(lo, x, sw), jnp.where(lo, sw, x)


def _attend(q, k_all, v_all, sink_ref, mask):
    scale = HEAD_DIM ** -0.5
    nt = (((1,), (1,)), ((), ()))
    kk = [a.astype(BF16) for a in _dup_halves(k_all)]
    vv = [a.astype(BF16) for a in _dup_halves(v_all)]
    lane = lax.broadcasted_iota(I32, (q.shape[0], LANES), 1)
    lo = lane < HEAD_DIM
    tiles = []
    for t in range(q.shape[1] // LANES):
        qt = q[:, t * LANES:(t + 1) * LANES]
        g = (2 * t) // GQA_GROUP
        halves = []
        for hh in range(2):
            head = 2 * t + hh
            qm = jnp.where(lo if hh == 0 else ~lo, qt, 0.0).astype(BF16)
            s = lax.dot_general(qm, kk[g], nt, preferred_element_type=F32) * scale
            if mask is not None:
                s = jnp.where(mask, s, NEG_BIG)
            sink = sink_ref[head]
            mx = jnp.maximum(jnp.max(s, axis=1, keepdims=True), sink)
            p = jnp.exp(s - mx)
            den = jnp.sum(p, axis=1, keepdims=True) + jnp.exp(sink - mx)
            p = (p / den).astype(BF16)
            halves.append(jnp.dot(p, vv[g], preferred_element_type=F32))
        tiles.append(jnp.where(lo, halves[0], halves[1]))
    return jnp.concatenate(tiles, axis=1)


def _ctx_attn_kernel(sink_ref, q_ref, k_ref, v_ref, o_ref):
    o_ref[...] = _attend(q_ref[...], k_ref[...], v_ref[...], sink_ref, None)


def _ctx_attention(q, k, v, sink, seq):
    m, nq = q.shape
    nk = k.shape[1]
    return pl.pallas_call(
        _ctx_attn_kernel,
        grid=(m // seq,),
        in_specs=[pl.BlockSpec(memory_space=pltpu.SMEM),
                  pl.BlockSpec((seq, nq), lambda b: (b, 0)),
                  pl.BlockSpec((seq, nk), lambda b: (b, 0)),
                  pl.BlockSpec((seq, nk), lambda b: (b, 0))],
        out_specs=pl.BlockSpec((seq, nq), lambda b: (b, 0)),
        out_shape=jax.ShapeDtypeStruct((m, nq), F32),
        compiler_params=_cparams("arbitrary"),
        name="ctx_attention",
    )(sink, q, k, v)


def _lat_attn_kernel(sink_ref, q_ref, kc_ref, vc_ref, kp_ref, k0_ref, kn_ref, vp_ref, v0_ref, vn_ref,
                     o_ref, *, seq):
    qb = pl.program_id(1)
    blk = q_ref.shape[0]
    n_ctx = kc_ref.shape[1]
    k_all = jnp.concatenate([kc_ref[0], kp_ref[...], k0_ref[...], kn_ref[...]], axis=0)
    v_all = jnp.concatenate([vc_ref[0], vp_ref[...], v0_ref[...], vn_ref[...]], axis=0)
    tk = k_all.shape[0]
    qpos = qb * blk + lax.broadcasted_iota(I32, (blk, tk), 0)
    col = lax.broadcasted_iota(I32, (blk, tk), 1)
    kpos = (qb - 1) * blk + col - n_ctx
    local_ok = (jnp.abs(qpos - kpos) <= WINDOW) & (kpos >= 0) & (kpos < seq)
    mask = (col < n_ctx) | local_ok
    o_ref[...] = _attend(q_ref[...], k_all, v_all, sink_ref, mask)


def _lat_attention(q, k, v, k_ctx, v_ctx, sink, seq):
    m, nq = q.shape
    nk = k.shape[1]
    blk = WINDOW
    nb = seq // blk
    n_ctx = k_ctx.shape[1]
    last = m // blk - 1

    def kv_spec(shift):
        return pl.BlockSpec((blk, nk), lambda b, i: (jnp.clip(b * nb + i + shift, 0, last), 0))

    ctx_spec = pl.BlockSpec((1, n_ctx, nk), lambda b, i: (b, 0, 0))
    return pl.pallas_call(
        functools.partial(_lat_attn_kernel, seq=seq),
        grid=(m // seq, nb),
        in_specs=[pl.BlockSpec(memory_space=pltpu.SMEM),
                  pl.BlockSpec((blk, nq), lambda b, i: (b * nb + i, 0)),
                  ctx_spec, ctx_spec,
                  kv_spec(-1), kv_spec(0), kv_spec(1),
                  kv_spec(-1), kv_spec(0), kv_spec(1)],
        out_specs=pl.BlockSpec((blk, nq), lambda b, i: (b * nb + i, 0)),
        out_shape=jax.ShapeDtypeStruct((m, nq), F32),
        compiler_params=_cparams("arbitrary", "arbitrary"),
        name="lat_attention",
    )(sink, q, k_ctx, v_ctx, k, k, k, v, v, v)

SSD_BLOCK = 256
SSD_PAIRS = SSD_HEADS // 2
SSD_INNER = SSD_HEADS * SSD_HEAD_DIM
HALO = SUBLANES


def _softplus(x):
    return jnp.maximum(x, 0.0) + jnp.log1p(jnp.exp(-jnp.abs(x)))


def _silu(x):
    return x * jax.nn.sigmoid(x)


def _ssd_decays(dt_raw, bias, a_log):
    n = dt_raw.shape[0]
    dt = _softplus(dt_raw + bias)
    log_a = dt * (-jnp.exp(a_log))
    r = lax.broadcasted_iota(I32, (n, n), 0)
    c = lax.broadcasted_iota(I32, (n, n), 1)
    lower = jnp.where(c <= r, 1.0, 0.0).astype(F32)
    upper = jnp.where(r <= c, 1.0, 0.0).astype(F32)
    cum_col = jnp.dot(lower, log_a, precision=HIGHEST, preferred_element_type=F32)
    dt_row = dt.T
    la_row = log_a.T
    cum_row = jnp.dot(la_row, upper, precision=HIGHEST, preferred_element_type=F32)
    return dt, log_a, cum_col, dt_row, la_row, cum_row


def _ssd_scan_chunk(xs, bmat, cmat, w_of, q_scale_of, k_scale_of, carry_of, s_ref):
    nt = (((1,), (1,)), ((), ()))
    n = xs.shape[0]
    lane = lax.broadcasted_iota(I32, (n, LANES), 1)
    lo = lane < SSD_HEAD_DIM
    lane_s = lax.broadcasted_iota(I32, (D_STATE, LANES), 1)
    lo_s = lane_s < SSD_HEAD_DIM
    b_t = bmat.T
    cb16 = cmat.astype(BF16)
    ys = []
    for pair in range(SSD_PAIRS):
        g = (2 * pair) // (SSD_HEADS // SSD_GROUPS)
        in_g = (lane // D_STATE) == g
        cg = jnp.where(in_g, cmat, 0.0)
        cb = lax.dot_general(cg.astype(BF16), bmat.astype(BF16), nt, preferred_element_type=F32)
        x_pair = xs[:, pair * LANES:(pair + 1) * LANES]
        x16 = x_pair.astype(BF16)
        s_old = s_ref[pair]
        s2 = jnp.concatenate([s_old, s_old], axis=0).astype(BF16)
        bg_t = b_t[g * D_STATE:(g + 1) * D_STATE, :]
        y_h, s_h = [], []
        for hh in range(2):
            h = 2 * pair + hh
            w = (cb * w_of(h)).astype(BF16)
            y = jnp.dot(w, x16, preferred_element_type=F32)
            cq = (cg * q_scale_of(h)).astype(BF16)
            y = y + jnp.dot(cq, s2, preferred_element_type=F32)
            y_h.append(y)
            kt = (bg_t * k_scale_of(h)).astype(BF16)
            s_h.append(carry_of(h) * s_old + jnp.dot(kt, x16, preferred_element_type=F32))
        ys.append(jnp.where(lo, y_h[0], y_h[1]))
        s_ref[pair] = jnp.where(lo_s, s_h[0], s_h[1])
    return jnp.concatenate(ys, axis=1)


def _ssd_fwd_kernel(x_ref, xp_ref, xn_ref, dt_ref, s0_ref, cw_ref, cb_ref, bias_ref, alog_ref,
                    y_ref, xc_ref, sfin_ref, s_ref):
    c = pl.program_id(1)
    nc = pl.num_programs(1)
    n = x_ref.shape[0]

    @pl.when(c == 0)
    def _():
        s_ref[...] = s0_ref[0]

    prev = jnp.where(c > 0, xp_ref[...], 0.0)
    nxt = jnp.where(c < nc - 1, xn_ref[...], 0.0)
    xe = jnp.concatenate([prev, x_ref[...], nxt], axis=0)
    pad = (CONV_K - 1) // 2
    acc = cb_ref[...] + cw_ref[0:1, :] * xe[HALO - pad:HALO - pad + n, :]
    for k in range(1, CONV_K):
        acc = acc + cw_ref[k:k + 1, :] * xe[HALO - pad + k:HALO - pad + k + n, :]
    xc = _silu(acc)
    xc_ref[...] = xc
    xs = xc[:, :SSD_INNER]
    bmat = xc[:, SSD_INNER:SSD_INNER + LANES]
    cmat = xc[:, SSD_INNER + LANES:SSD_INNER + 2 * LANES]

    dt, log_a, cum_col, dt_row, la_row, cum_row = _ssd_decays(dt_ref[...], bias_ref[...], alog_ref[...])
    r = lax.broadcasted_iota(I32, (n, n), 0)
    cc = lax.broadcasted_iota(I32, (n, n), 1)
    causal = cc <= r
    last_col = cum_col[n - 1:n, :]

    def w_of(h):
        seg = cum_col[:, h:h + 1] - cum_row[h:h + 1, :]
        return jnp.exp(jnp.where(causal, seg, NEG_BIG)) * dt_row[h:h + 1, :]

    def q_scale_of(h):
        return jnp.exp(cum_col[:, h:h + 1])

    def k_scale_of(h):
        return dt_row[h:h + 1, :] * jnp.exp(cum_row[h:h + 1, n - 1:n] - cum_row[h:h + 1, :])

    def carry_of(h):
        return jnp.exp(last_col[:, h:h + 1])

    y_ref[...] = _ssd_scan_chunk(xs, bmat, cmat, w_of, q_scale_of, k_scale_of, carry_of, s_ref)

    @pl.when(c == nc - 1)
    def _():
        sfin_ref[0] = s_ref[...]


def _ssd_bwd_kernel(xc_ref, dt_ref, yf_ref, z_ref, s0_ref, bias_ref, alog_ref, dskip_ref, nw_ref,
                    y_ref, sfin_ref, s_ref):
    c = pl.program_id(1)
    nc = pl.num_programs(1)
    n = xc_ref.shape[0]

    @pl.when(c == 0)
    def _():
        s_ref[...] = s0_ref[0]

    xc = xc_ref[...]
    xs = xc[:, :SSD_INNER]
    bmat = xc[:, SSD_INNER:SSD_INNER + LANES]
    cmat = xc[:, SSD_INNER + LANES:SSD_INNER + 2 * LANES]
    dt, log_a, cum_col, dt_row, la_row, cum_row = _ssd_decays(dt_ref[...], bias_ref[...], alog_ref[...])
    ex_col = cum_col - log_a
    ex_row = cum_row - la_row
    r = lax.broadcasted_iota(I32, (n, n), 0)
    cc = lax.broadcasted_iota(I32, (n, n), 1)
    anti = cc >= r
    tot_col = cum_col[n - 1:n, :]
    off = SSD_HEADS

    def w_of(h):
        j = off + h
        seg = ex_row[j:j + 1, :] - ex_col[:, j:j + 1]
        return jnp.exp(jnp.where(anti, seg, NEG_BIG)) * dt_row[j:j + 1, :]

    def q_scale_of(h):
        j = off + h
        return jnp.exp(tot_col[:, j:j + 1] - ex_col[:, j:j + 1])

    def k_scale_of(h):
        j = off + h
        return dt_row[j:j + 1, :] * jnp.exp(ex_row[j:j + 1, :])

    def carry_of(h):
        j = off + h
        return jnp.exp(tot_col[:, j:j + 1])

    y_b = _ssd_scan_chunk(xs, bmat, cmat, w_of, q_scale_of, k_scale_of, carry_of, s_ref)
    y = yf_ref[...] + y_b + dskip_ref[...] * xs
    y = y * _silu(z_ref[...])
    ms = jnp.mean(y * y, axis=-1, keepdims=True)
    y_ref[...] = y * lax.rsqrt(ms + NORM_EPS) * nw_ref[...]

    @pl.when(c == nc - 1)
    def _():
        sfin_ref[0] = s_ref[...]


def _pair_states(s):
    b, h, n, p = s.shape
    return s.reshape(b, h // 2, 2, n, p).transpose(0, 1, 3, 2, 4).reshape(b, h // 2, n, 2 * p)


def _unpair_states(s):
    b, hp, n, p2 = s.shape
    return s.reshape(b, hp, n, 2, p2 // 2).transpose(0, 1, 3, 2, 4).reshape(b, hp * 2, n, p2 // 2)


def _ssd(xbc, dt, z, s0_f, s0_b, conv_w, conv_b, dt_bias, a_log, d_skip, ssd_norm, seq):
    m, nx = xbc.shape
    nb = m // seq
    blk = min(SSD_BLOCK, seq)
    nc = seq // blk
    hb = blk // HALO
    n_halo = m // HALO
    pad16 = lambda a: jnp.pad(a.reshape(1, -1), ((0, 0), (0, LANES - a.size)))
    bias = pad16(dt_bias)
    alog = pad16(a_log)
    state_spec = pl.BlockSpec((1, SSD_PAIRS, D_STATE, LANES), lambda b, c: (b, 0, 0, 0))
    state_shape = jax.ShapeDtypeStruct((nb, SSD_PAIRS, D_STATE, LANES), F32)
    row = lambda width: pl.BlockSpec((1, width), lambda b, c: (0, 0))

    def fwd_rows(width):
        return pl.BlockSpec((blk, width), lambda b, c: (b * nc + c, 0))

    def bwd_rows(width):
        return pl.BlockSpec((blk, width), lambda b, c: (b * nc + nc - 1 - c, 0))

    y_f, xc, s_f = pl.pallas_call(
        _ssd_fwd_kernel,
        grid=(nb, nc),
        in_specs=[fwd_rows(nx),
                  pl.BlockSpec((HALO, nx), lambda b, c: (jnp.maximum((b * nc + c) * hb - 1, 0), 0)),
                  pl.BlockSpec((HALO, nx), lambda b, c: (jnp.minimum((b * nc + c + 1) * hb, n_halo - 1), 0)),
                  fwd_rows(LANES), state_spec,
                  pl.BlockSpec((CONV_K, nx), lambda b, c: (0, 0)), row(nx), row(LANES), row(LANES)],
        out_specs=[fwd_rows(SSD_INNER), fwd_rows(nx), state_spec],
        out_shape=[jax.ShapeDtypeStruct((m, SSD_INNER), F32), jax.ShapeDtypeStruct((m, nx), F32), state_shape],
        scratch_shapes=[pltpu.VMEM((SSD_PAIRS, D_STATE, LANES), F32)],
        compiler_params=_cparams("arbitrary", "arbitrary"),
        name="ssd_forward",
    )(xbc, xbc, xbc, dt, _pair_states(s0_f), conv_w, conv_b.reshape(1, nx), bias, alog)

    dskip = jnp.repeat(d_skip, SSD_HEAD_DIM).reshape(1, SSD_INNER)
    y, s_b = pl.pallas_call(
        _ssd_bwd_kernel,
        grid=(nb, nc),
        in_specs=[bwd_rows(nx), bwd_rows(LANES), bwd_rows(SSD_INNER), bwd_rows(SSD_INNER), state_spec,
                  row(LANES), row(LANES), row(SSD_INNER), row(SSD_INNER)],
        out_specs=[bwd_rows(SSD_INNER), state_spec],
        out_shape=[jax.ShapeDtypeStruct((m, SSD_INNER), F32), state_shape],
        scratch_shapes=[pltpu.VMEM((SSD_PAIRS, D_STATE, LANES), F32)],
        compiler_params=_cparams("arbitrary", "arbitrary"),
        name="ssd_backward",
    )(xc, dt, y_f, z, _pair_states(s0_b), bias, alog, dskip, ssd_norm.reshape(1, SSD_INNER))
    return y, _unpair_states(s_f), _unpair_states(s_b)

def _hgrn_kernel(q_ref, ff_ref, fb_ref, i_ref, g_ref, lb_ref, s0_ref, nw_ref, o_ref, sfin_ref,
                 s_ref, *, layer):
    t_len = q_ref.shape[0]
    n = HGRN_CHUNK
    n_chunks = t_len // n
    tn = (((0,), (0,)), ((), ()))
    nt = (((1,), (1,)), ((), ()))

    lbp = lb_ref[...]
    e = jnp.exp(lbp - jnp.max(lbp, axis=0, keepdims=True))
    sm = e / jnp.sum(e, axis=0, keepdims=True)
    lb = sm[0] * 0.0
    for j in range(1, layer + 1):
        lb = lb + sm[j]

    r = lax.broadcasted_iota(I32, (n, n), 0)
    c = lax.broadcasted_iota(I32, (n, n), 1)
    lower = jnp.where(c <= r, 1.0, 0.0).astype(F32)
    srow = lax.broadcasted_iota(I32, (n, HGRN_DK), 0)
    qscale = HGRN_DK ** -0.5

    def chunk(row0, f_ref, lb_d, reverse):
        q = _silu(q_ref[pl.ds(row0, n), :]) * qscale
        f = f_ref[pl.ds(row0, n), :]
        v = i_ref[pl.ds(row0, n), :]
        k = (1.0 - lb_d) * jax.nn.sigmoid(-f)
        lf = jnp.log(lb_d + (1.0 - lb_d) * jax.nn.sigmoid(f))
        cum = jnp.dot(lower, lf, precision=HIGHEST, preferred_element_type=F32)
        tot = cum[n - 1:n, :]
        if reverse:
            cum = cum - lf
        rows = []
        for t in range(n):
            if reverse:
                seg = jnp.where(srow >= t, cum - cum[t:t + 1, :], NEG_BIG)
            else:
                seg = jnp.where(srow <= t, cum[t:t + 1, :] - cum, NEG_BIG)
            a = q[t:t + 1, :] * k * jnp.exp(seg)
            sc = jnp.sum(a, axis=1, keepdims=True)
            rows.append(jnp.sum(sc * v, axis=0, keepdims=True))
        o = jnp.concatenate(rows, axis=0)
        s_old = s_ref[...]
        if reverse:
            q_in = q * jnp.exp(tot - cum)
            k_out = k * jnp.exp(cum)
        else:
            q_in = q * jnp.exp(cum)
            k_out = k * jnp.exp(tot - cum)
        o = o + lax.dot_general(q_in.astype(BF16), s_old.astype(BF16), nt, preferred_element_type=F32)
        s_ref[...] = jnp.exp(tot) * s_old + lax.dot_general(
            v.astype(BF16), k_out.astype(BF16), tn, preferred_element_type=F32)
        return o

    s_ref[...] = s0_ref[0, 0, 0].T

    def fwd_body(ci, carry):
        row0 = pl.multiple_of(ci * n, n)
        o_ref[pl.ds(row0, n), :] = chunk(row0, ff_ref, lb[0:1, :], False)
        return carry

    lax.fori_loop(0, n_chunks, fwd_body, 0)
    sfin_ref[0, 0, 0] = s_ref[...].T

    s_ref[...] = s0_ref[0, 1, 0].T
    nw = nw_ref[...]

    def bwd_body(ci, carry):
        row0 = pl.multiple_of((n_chunks - 1 - ci) * n, n)
        o = o_ref[pl.ds(row0, n), :] + chunk(row0, fb_ref, lb[1:2, :], True)
        ms = jnp.mean(o * o, axis=-1, keepdims=True)
        o = o * lax.rsqrt(ms + NORM_EPS) * nw
        o_ref[pl.ds(row0, n), :] = o * _silu(g_ref[pl.ds(row0, n), :])
        return carry

    lax.fori_loop(0, n_chunks, bwd_body, 0)
    sfin_ref[0, 1, 0] = s_ref[...].T


def _hgrn(q, f_fw, f_bw, iv, g, o_lb, state0, g_norm, seq, layer):
    m, width = q.shape
    nb = m // seq
    dv = width // HGRN_HEADS
    col = pl.BlockSpec((seq, dv), lambda b, h: (b, h))
    state_spec = pl.BlockSpec((1, 2, 1, HGRN_DK, dv), lambda b, h: (b, 0, h, 0, 0))
    return pl.pallas_call(
        functools.partial(_hgrn_kernel, layer=layer),
        grid=(nb, HGRN_HEADS),
        in_specs=[col, col, col, col, col,
                  pl.BlockSpec((o_lb.shape[0], 2, HGRN_DK), lambda b, h: (0, 0, h)),
                  state_spec,
                  pl.BlockSpec((1, dv), lambda b, h: (0, 0))],
        out_specs=[col, state_spec],
        out_shape=[jax.ShapeDtypeStruct((m, width), F32),
                   jax.ShapeDtypeStruct((nb, 2, HGRN_HEADS, HGRN_DK, dv), F32)],
        scratch_shapes=[pltpu.VMEM((dv, HGRN_DK), F32)],
        compiler_params=_cparams("arbitrary", "arbitrary"),
        name="hgrn2",
    )(q, f_fw, f_bw, iv, g, o_lb, state0, g_norm.reshape(1, dv))

EVEN_SPLITS = ((0, 512), (512, 640), (640, 768), (768, 1280), (1280, 2048), (2048, 2176))
HGRN_SPLITS = tuple((i * 1024, (i + 1) * 1024) for i in range(5))


def _even_weight(w):
    main = EVEN_SPLITS[-1][0]
    return jnp.pad(w, ((0, 0), (0, LANES - (w.shape[1] - main)))).astype(BF16)


def _run_trunk(x3, mods, mod_row0, P, cache):
    nb, seq, d = x3.shape
    x = x3.reshape(nb * seq, d)
    depth = P['norm_mix'].shape[0]
    ks, vs, ssd_states, hgrn_states = [], [], [], []
    for l in range(depth):
        j = l // 2
        row0 = (l * SUBLANES + mod_row0, 0 if cache is None else 1)
        if l % 2 == 0:
            q, k, v, z, xbc, dt = _inproj(x, mods, row0, seq, P['norm_mix'][l], P['e_w_in'][j],
                                          EVEN_SPLITS, "even_in_proj")
            q, k = _qkprep(q, k, P['e_q_norm'][j], P['e_k_norm'][j], seq, rope=cache is not None)
            if cache is None:
                s0_f = jnp.zeros((nb, SSD_HEADS, D_STATE, SSD_HEAD_DIM), F32)
                s0_b = s0_f
                o_attn = _ctx_attention(q, k, v, P['e_sink'][j], seq)
            else:
                s0_f, s0_b = cache[2][:, j, 0], cache[2][:, j, 1]
                n_ctx = cache[0].shape[2]
                o_attn = _lat_attention(q, k, v, cache[0][:, j].reshape(nb, n_ctx, -1),
                                        cache[1][:, j].reshape(nb, n_ctx, -1), P['e_sink'][j], seq)
            y, s_f, s_b = _ssd(xbc, dt, z, s0_f, s0_b, P['e_conv_w'][j], P['e_conv_b'][j],
                               P['e_dt_bias'][j], P['e_a_log'][j], P['e_d_skip'][j], P['e_ssd_norm'][j], seq)
            if cache is None:
                ks.append(k.reshape(nb, seq, N_KV_HEADS, HEAD_DIM))
                vs.append(v.reshape(nb, seq, N_KV_HEADS, HEAD_DIM))
                ssd_states.append(jnp.stack([s_f, s_b], axis=1))
            mix = jnp.concatenate([o_attn, y], axis=1)
            x = _outproj(mix, x, mods, row0, seq, P['e_w_out'][j], "even_out_proj")
        else:
            q, f_fw, f_bw, iv, g = _inproj(x, mods, row0, seq, P['norm_mix'][l], P['o_w_in'][j],
                                           HGRN_SPLITS, "odd_in_proj")
            if cache is None:
                s0 = jnp.zeros((nb, 2, HGRN_HEADS, HGRN_DK, d // HGRN_HEADS), F32)
            else:
                s0 = cache[3][:, j]
            o, s_new = _hgrn(q, f_fw, f_bw, iv, g, P['o_lb'], s0, P['o_g_norm'][j], seq, j)
            if cache is None:
                hgrn_states.append(s_new)
            x = _outproj(o, x, mods, row0, seq, P['o_w_out'][j], "odd_out_proj")
        x = _peer(x, mods, row0, seq, l, P['norm_ffn'][l], P['p_w_q'][l], P['p_sub_keys'][l],
                  P['p_uv'], P['p_u'], P['p_v'])
    y = x.reshape(nb, seq, d)
    if cache is not None:
        return y, None
    return y, (jnp.stack(ks, axis=1), jnp.stack(vs, axis=1),
               jnp.stack(ssd_states, axis=1), jnp.stack(hgrn_states, axis=1))


def kernel(x_prompt, x_sample, cache_k, cache_v, state_ssd, state_hgrn, c, c_ctx, w_ada, b_ada, norm_mix, norm_ffn, e_w_in, e_q_norm, e_k_norm, e_sink, e_conv_w, e_conv_b, e_dt_bias, e_a_log, e_d_skip, e_ssd_norm, e_w_out, o_w_in, o_lb, o_g_norm, o_w_out, p_w_q, p_sub_keys, p_u, p_v):
    depth, d, d6 = w_ada.shape
    b_lat = x_sample.shape[0]
    cond_rows = jnp.concatenate([c_ctx[None, :], c, jnp.zeros((SUBLANES - 1 - b_lat, d), F32)], axis=0)
    mods = _modulation(cond_rows, w_ada, b_ada).reshape(depth * SUBLANES, 1, d6)
    P = {
        'norm_mix': norm_mix, 'norm_ffn': norm_ffn,
        'e_w_in': jnp.stack([_even_weight(w) for w in e_w_in]), 'e_q_norm': e_q_norm, 'e_k_norm': e_k_norm,
        'e_sink': e_sink, 'e_conv_w': e_conv_w, 'e_conv_b': e_conv_b, 'e_dt_bias': e_dt_bias,
        'e_a_log': e_a_log, 'e_d_skip': e_d_skip, 'e_ssd_norm': e_ssd_norm,
        'e_w_out': e_w_out.astype(BF16),
        'o_w_in': o_w_in.astype(BF16), 'o_lb': o_lb, 'o_g_norm': o_g_norm, 'o_w_out': o_w_out.astype(BF16),
        'p_w_q': p_w_q.astype(BF16),
        'p_sub_keys': p_sub_keys.astype(BF16).reshape(depth, PEER_HEADS * 2, PEER_NKEYS, PEER_DKEY),
        'p_uv': jnp.stack([p_u, p_v], axis=2).reshape(depth, p_u.shape[1] * 2 * d // LANES, 1, LANES),
        'p_u': p_u, 'p_v': p_v,
    }
    y_prompt, new_state = _run_trunk(x_prompt, mods, 0, P, None)
    y_sample, _ = _run_trunk(x_sample, mods, 1, P, (cache_k, cache_v, state_ssd, state_hgrn))
    return (y_prompt, y_sample) + new_state
```

```python
import functools
import math

import jax
import jax.numpy as jnp
from jax import lax
from jax.experimental import pallas as pl
from jax.experimental.pallas import tpu as pltpu
from jax.experimental.pallas import tpu_sc as plsc

F32 = jnp.float32
BF16 = jnp.bfloat16
I32 = jnp.int32
HIGHEST = lax.Precision.HIGHEST

NORM_EPS = 1e-6
NEG_BIG = -1e30
LANES = 128
SUBLANES = 8
VMEM_LIMIT = 48 * 1024 * 1024

GRID_W = 64
HEAD_DIM = 64
N_Q_HEADS = 8
N_KV_HEADS = 2
GQA_GROUP = 4
WINDOW = 128
ROPE_THETA = 10000.0
SSD_HEADS = 8
SSD_HEAD_DIM = 64
SSD_GROUPS = 2
D_STATE = 64
CONV_K = 5
HGRN_HEADS = 8
HGRN_DK = 128
HGRN_CHUNK = 32
PEER_HEADS = 8
PEER_NKEYS = 128
PEER_TOPK = 16
PEER_DKEY = 128
PEER_PAIRS = PEER_HEADS * PEER_TOPK

ROW_TILE = 256
PEER_BLOCK = 128
SC_LANES = 16
SC_CHUNK = 16
SC_SHARE_CONTEXT = (1, 1)
SC_SHARE_LATENT = (13, 16)
GELU_C = math.sqrt(2.0 / math.pi)


def _cparams(*sem):
    return pltpu.CompilerParams(dimension_semantics=sem, vmem_limit_bytes=VMEM_LIMIT)


def _norm_mod(x, nw, scale, shift):
    ms = jnp.mean(x * x, axis=-1, keepdims=True)
    return (x * lax.rsqrt(ms + NORM_EPS)) * nw * (1.0 + scale) + shift


def _mod_kernel(c_ref, w_ref, b_ref, o_ref):
    c = c_ref[...]
    s = c * jax.nn.sigmoid(c)
    o_ref[0] = jnp.dot(s, w_ref[0], precision=HIGHEST, preferred_element_type=F32) + b_ref[0]


def _modulation(cond_rows, w_ada, b_ada):
    depth, d, n = w_ada.shape
    rows = cond_rows.shape[0]
    return pl.pallas_call(
        _mod_kernel,
        grid=(depth, n // d),
        in_specs=[pl.BlockSpec((rows, d), lambda l, j: (0, 0)),
                  pl.BlockSpec((1, d, d), lambda l, j: (l, 0, j)),
                  pl.BlockSpec((1, 1, d), lambda l, j: (l, 0, j))],
        out_specs=pl.BlockSpec((1, rows, d), lambda l, j: (l, 0, j)),
        out_shape=jax.ShapeDtypeStruct((depth, rows, n), F32),
        compiler_params=_cparams("arbitrary", "arbitrary"),
        name="modulation",
    )(cond_rows, w_ada, b_ada.reshape(depth, 1, n))


def _mod_spec(mod_row0, seq, tile, d6, first_tile=0):
    row0, per_batch = mod_row0
    return pl.BlockSpec((1, 1, d6),
                        lambda i: (row0 + per_batch * (((i + first_tile) * tile) // seq), 0, 0))


def _inproj_kernel(x_ref, m_ref, nw_ref, w_ref, *o_refs, splits, d):
    m = m_ref[0]
    h = _norm_mod(x_ref[...], nw_ref[...], m[:, d:2 * d], m[:, 0:d]).astype(BF16)
    for o_ref, (a, b) in zip(o_refs, splits):
        o_ref[...] = jnp.dot(h, w_ref[:, a:b], preferred_element_type=F32)


def _inproj(x, mods, mod_row0, seq, nw, w_bf16, splits, name):
    m, d = x.shape
    n = w_bf16.shape[1]
    tile = min(ROW_TILE, seq)
    return pl.pallas_call(
        functools.partial(_inproj_kernel, splits=splits, d=d),
        grid=(m // tile,),
        in_specs=[pl.BlockSpec((tile, d), lambda i: (i, 0)),
                  _mod_spec(mod_row0, seq, tile, mods.shape[-1]),
                  pl.BlockSpec((1, d), lambda i: (0, 0)),
                  pl.BlockSpec((d, n), lambda i: (0, 0))],
        out_specs=[pl.BlockSpec((tile, b - a), lambda i: (i, 0)) for a, b in splits],
        out_shape=[jax.ShapeDtypeStruct((m, b - a), F32) for a, b in splits],
        compiler_params=_cparams("arbitrary"),
        name=name,
    )(x, mods, nw.reshape(1, d), w_bf16)


def _outproj_kernel(mix_ref, x_ref, m_ref, w_ref, o_ref, *, d):
    y = jnp.dot(mix_ref[...].astype(BF16), w_ref[...], preferred_element_type=F32)
    o_ref[...] = x_ref[...] + m_ref[0][:, 2 * d:3 * d] * y


def _outproj(mix, x, mods, mod_row0, seq, w_bf16, name):
    m, d = x.shape
    k = mix.shape[1]
    tile = min(ROW_TILE, seq)
    return pl.pallas_call(
        functools.partial(_outproj_kernel, d=d),
        grid=(m // tile,),
        in_specs=[pl.BlockSpec((tile, k), lambda i: (i, 0)),
                  pl.BlockSpec((tile, d), lambda i: (i, 0)),
                  _mod_spec(mod_row0, seq, tile, mods.shape[-1]),
                  pl.BlockSpec((k, d), lambda i: (0, 0))],
        out_specs=pl.BlockSpec((tile, d), lambda i: (i, 0)),
        out_shape=jax.ShapeDtypeStruct((m, d), F32),
        compiler_params=_cparams("arbitrary"),
        name=name,
    )(mix, x, mods, w_bf16)


def _topk_over_rows(s, k, payload=None):
    n = s.shape[0]
    iota = lax.broadcasted_iota(I32, s.shape, 0)
    vals, idxs, pays = [], [], []
    for _ in range(k):
        m = jnp.max(s, axis=0, keepdims=True)
        i = jnp.min(jnp.where(s == m, iota, n), axis=0, keepdims=True)
        hit = iota == i
        vals.append(m)
        idxs.append(i)
        if payload is not None:
            pays.append(jnp.max(jnp.where(hit, payload, -1), axis=0, keepdims=True))
        s = jnp.where(hit, -jnp.inf, s)
    out = (jnp.concatenate(vals, axis=0), jnp.concatenate(idxs, axis=0))
    if payload is not None:
        out += (jnp.concatenate(pays, axis=0),)
    return out


def _peer_route_kernel(x_ref, m_ref, nw_ref, wq_ref, keys_ref, h_ref, e_ref, g_ref, *, d):
    m = m_ref[0]
    h = _norm_mod(x_ref[...], nw_ref[...], m[:, 4 * d:5 * d], m[:, 3 * d:4 * d])
    h_ref[...] = h
    hb = h.astype(BF16)
    nt = (((1,), (1,)), ((), ()))
    for head in range(PEER_HEADS):
        tops = []
        for half in range(2):
            c0 = (head * 2 + half) * PEER_DKEY
            q = jnp.dot(hb, wq_ref[:, c0:c0 + PEER_DKEY], preferred_element_type=F32)
            s = lax.dot_general(keys_ref[head * 2 + half], q.astype(BF16), nt,
                                preferred_element_type=F32)
            tops.append(_topk_over_rows(s, PEER_TOPK))
        (s0, i0), (s1, i1) = tops
        widths = [PEER_TOPK // (a + 1) for a in range(PEER_TOPK)]
        n_pad = -sum(widths) % SUBLANES
        cand_s = jnp.concatenate([s0[a:a + 1] + s1[:w] for a, w in enumerate(widths)]
                                 + [jnp.full((n_pad, s0.shape[1]), -jnp.inf, F32)], axis=0)
        cand_e = jnp.concatenate([i0[a:a + 1] * PEER_NKEYS + i1[:w] for a, w in enumerate(widths)]
                                 + [jnp.zeros((n_pad, s0.shape[1]), I32)], axis=0)
        best_s, _, best_e = _topk_over_rows(cand_s, PEER_TOPK, payload=cand_e)
        p = jnp.exp(best_s - best_s[0:1])
        r0 = head * PEER_TOPK
        e_ref[r0:r0 + PEER_TOPK, :] = best_e
        g_ref[r0:r0 + PEER_TOPK, :] = p / jnp.sum(p, axis=0, keepdims=True)


def _peer_route(x, mods, mod_row0, seq, nw, wq_bf16, keys_bf16):
    m, d = x.shape
    tile = min(ROW_TILE, seq)
    nq = wq_bf16.shape[1]
    return pl.pallas_call(
        functools.partial(_peer_route_kernel, d=d),
        grid=(m // tile,),
        in_specs=[pl.BlockSpec((tile, d), lambda i: (i, 0)),
                  _mod_spec(mod_row0, seq, tile, mods.shape[-1]),
                  pl.BlockSpec((1, d), lambda i: (0, 0)),
                  pl.BlockSpec((d, nq), lambda i: (0, 0)),
                  pl.BlockSpec(keys_bf16.shape, lambda i: (0, 0, 0))],
        out_specs=[pl.BlockSpec((tile, d), lambda i: (i, 0)),
                   pl.BlockSpec((PEER_PAIRS, tile), lambda i: (0, i)),
                   pl.BlockSpec((PEER_PAIRS, tile), lambda i: (0, i))],
        out_shape=[jax.ShapeDtypeStruct((m, d), F32),
                   jax.ShapeDtypeStruct((PEER_PAIRS, m), I32),
                   jax.ShapeDtypeStruct((PEER_PAIRS, m), F32)],
        compiler_params=_cparams("arbitrary"),
        name="peer_route",
    )(x, mods, nw.reshape(1, d), wq_bf16, keys_bf16)


def _peer_gather_kernel(idx_hbm, h_ref, g_ref, x_ref, m_ref, uv_hbm, o_ref,
                        idx_smem, buf, sem_idx, sem, *, layer, d):
    blk = pl.program_id(0)
    n_groups = PEER_BLOCK // SUBLANES
    chunks = d // LANES
    span = 2 * chunks

    cp = pltpu.make_async_copy(idx_hbm.at[blk], idx_smem, sem_idx)
    cp.start()
    cp.wait()

    def issue(tok, slot):
        def body(pair, carry):
            row = pl.multiple_of(idx_smem[tok, pair], span)
            pltpu.make_async_copy(uv_hbm.at[layer, pl.ds(row, span)],
                                  buf.at[slot, :, pl.ds(pair, 1), :],
                                  sem.at[slot]).start()
            return carry
        lax.fori_loop(0, PEER_PAIRS, body, 0, unroll=8)

    def wait(slot):
        pltpu.make_async_copy(buf.at[1 - slot], buf.at[slot], sem.at[slot]).wait()

    def rows_of(slot, first):
        return jnp.concatenate([buf[slot, first + c] for c in range(chunks)], axis=1)

    gate2 = m_ref[0][:, 5 * d:6 * d]
    lane = lax.broadcasted_iota(I32, (PEER_PAIRS, PEER_BLOCK), 1)

    issue(0, 0)

    def group(grp, carry):
        base = pl.multiple_of(grp * SUBLANES, SUBLANES)
        h8 = h_ref[pl.ds(base, SUBLANES), :]
        rows = []
        for r in range(SUBLANES):
            tok = base + r
            slot = r % 2
            if r < SUBLANES - 1:
                issue(tok + 1, 1 - slot)
            else:
                @pl.when(grp < n_groups - 1)
                def _():
                    issue(tok + 1, 1 - slot)
            wait(slot)
            act = jnp.sum(rows_of(slot, 0) * h8[r:r + 1, :], axis=1, keepdims=True)
            gate = jnp.sum(jnp.where(lane == tok, g_ref[...], 0.0), axis=1, keepdims=True)
            w = jax.nn.gelu(act) * gate
            rows.append(jnp.sum(rows_of(slot, chunks) * w, axis=0, keepdims=True))
        out8 = jnp.concatenate(rows, axis=0)
        o_ref[pl.ds(base, SUBLANES), :] = x_ref[pl.ds(base, SUBLANES), :] + gate2 * out8
        return carry

    lax.fori_loop(0, n_groups, group, 0)


def _peer_gather(idx, h, gates, x, mods, mod_row0, seq, p_uv, layer, first_block):
    m, d = x.shape
    nblk = m // PEER_BLOCK - first_block
    span = 2 * d // LANES
    idx3 = (span * idx[:, first_block * PEER_BLOCK:]).T.reshape(nblk, PEER_BLOCK, PEER_PAIRS)
    rows = pl.BlockSpec((PEER_BLOCK, d), lambda i: (i + first_block, 0))
    return pl.pallas_call(
        functools.partial(_peer_gather_kernel, layer=layer, d=d),
        grid=(nblk,),
        in_specs=[pl.BlockSpec(memory_space=pl.ANY),
                  rows,
                  pl.BlockSpec((PEER_PAIRS, PEER_BLOCK), lambda i: (0, i + first_block)),
                  rows,
                  _mod_spec(mod_row0, seq, PEER_BLOCK, mods.shape[-1], first_block),
                  pl.BlockSpec(memory_space=pl.ANY)],
        out_specs=pl.BlockSpec((PEER_BLOCK, d), lambda i: (i, 0)),
        out_shape=jax.ShapeDtypeStruct((nblk * PEER_BLOCK, d), F32),
        scratch_shapes=[pltpu.SMEM((PEER_BLOCK, PEER_PAIRS), I32),
                        pltpu.VMEM((2, span, PEER_PAIRS, LANES), F32),
                        pltpu.SemaphoreType.DMA,
                        pltpu.SemaphoreType.DMA((2,))],
        compiler_params=_cparams("arbitrary"),
        name="peer_gather",
    )(idx3, h, gates, x, mods, p_uv)


def _peer_experts_sc(idx, h, gates, u_rows, v_rows):
    m = idx.shape[0]
    d = h.shape[1]
    info = plsc.get_sparse_core_info()
    n_workers = info.num_cores * info.num_subcores
    per = m // n_workers
    n_chunks = PEER_PAIRS // SC_CHUNK
    n_vec = d // SC_LANES
    mesh = plsc.VectorSubcoreMesh(core_axis_name="c", subcore_axis_name="s")

    @functools.partial(
        pl.kernel, out_type=jax.ShapeDtypeStruct((m, d), F32), mesh=mesh,
        scratch_types=[pltpu.VMEM((2, PEER_PAIRS), I32), pltpu.VMEM((2, d), F32),
                       pltpu.VMEM((2, PEER_PAIRS), F32), pltpu.VMEM((2, d), F32),
                       pltpu.VMEM((2, SC_CHUNK, d), F32), pltpu.VMEM((2, SC_CHUNK, d), F32),
                       pltpu.SemaphoreType.DMA((2,)), pltpu.SemaphoreType.DMA((2,)),
                       pltpu.SemaphoreType.DMA((2,)), pltpu.SemaphoreType.DMA((2,))],
        compiler_params=pltpu.CompilerParams(needs_layout_passes=False),
        name="peer_experts_sc")
    def body(idx_hbm, h_hbm, g_hbm, u_hbm, v_hbm, o_hbm,
             idx_v, x_v, g_v, out_v, ubuf, vbuf, sem_meta, sem_out, sem_u, sem_v):
        wid = lax.axis_index("c") * info.num_subcores + lax.axis_index("s")
        tok0 = wid * per
        lane = lax.iota(I32, SC_LANES)

        def meta_copies(ti, ms):
            t = tok0 + ti
            return (pltpu.make_async_copy(idx_hbm.at[t], idx_v.at[ms], sem_meta.at[ms]),
                    pltpu.make_async_copy(h_hbm.at[t], x_v.at[ms], sem_meta.at[ms]),
                    pltpu.make_async_copy(g_hbm.at[t], g_v.at[ms], sem_meta.at[ms]))

        def gather_copies(ms, c, slot):
            ids = idx_v.at[ms, pl.ds(c * SC_CHUNK, SC_CHUNK)]
            return (pltpu.make_async_copy(u_hbm.at[ids], ubuf.at[slot], sem_u.at[slot]),
                    pltpu.make_async_copy(v_hbm.at[ids], vbuf.at[slot], sem_v.at[slot]))

        def out_copy(ti, ms):
            return pltpu.make_async_copy(out_v.at[ms], o_hbm.at[tok0 + ti], sem_out.at[ms])

        for cp in meta_copies(0, 0):
            cp.start()
        for cp in meta_copies(0, 0):
            cp.wait()
        for cp in gather_copies(0, 0, 0):
            cp.start()

        def token(ti, carry):
            ms = ti % 2
            nxt = 1 - ms

            @pl.when(ti + 1 < per)
            def _():
                for cp in meta_copies(ti + 1, nxt):
                    cp.start()

            @pl.when(ti >= 2)
            def _():
                out_copy(ti - 2, ms).wait()

            def zero(j, c):
                out_v[ms, pl.ds(j * SC_LANES, SC_LANES)] = jnp.zeros((SC_LANES,), F32)
                return c
            lax.fori_loop(0, n_vec, zero, 0)

            for c in range(n_chunks):
                slot = c % 2
                if c + 1 < n_chunks:
                    for cp in gather_copies(ms, c + 1, 1 - slot):
                        cp.start()
                else:
                    @pl.when(ti + 1 < per)
                    def _():
                        for cp in meta_copies(ti + 1, nxt):
                            cp.wait()
                        for cp in gather_copies(nxt, 0, 1 - slot):
                            cp.start()
                cu, cv = gather_copies(ms, c, slot)
                cu.wait()

                def udot(j, accs):
                    xj = x_v[ms, pl.ds(j * SC_LANES, SC_LANES)]
                    return tuple(accs[r] + ubuf[slot, r, pl.ds(j * SC_LANES, SC_LANES)] * xj
                                 for r in range(SC_CHUNK))
                accs = lax.fori_loop(0, n_vec, udot,
                                     tuple(jnp.zeros((SC_LANES,), F32) for _ in range(SC_CHUNK)))
                act = jnp.zeros((SC_LANES,), F32)
                for r in range(SC_CHUNK):
                    act = jnp.where(lane == r, jnp.sum(accs[r]), act)
                y = GELU_C * (act + 0.044715 * (act * act * act))
                w = act / (1.0 + jnp.exp(-2.0 * y)) * g_v[ms, pl.ds(c * SC_CHUNK, SC_CHUNK)]
                ws = [jnp.sum(jnp.where(lane == r, w, 0.0)) for r in range(SC_CHUNK)]
                cv.wait()

                @plsc.parallel_loop(0, n_vec, unroll=2)
                def _(j):
                    parts = [ws[r] * vbuf[slot, r, pl.ds(j * SC_LANES, SC_LANES)] for r in range(SC_CHUNK)]
                    while len(parts) > 1:
                        parts = [parts[i] + parts[i + 1] for i in range(0, len(parts), 2)]
                    plsc.addupdate(out_v.at[ms, pl.ds(j * SC_LANES, SC_LANES)], parts[0])

            out_copy(ti, ms).start()
            return carry

        lax.fori_loop(0, per, token, 0)
        for back in (2, 1):
            if per >= back:
                out_copy(per - back, (per - back) % 2).wait()

    return body(idx, h, gates, u_rows, v_rows)


def _residual_kernel(x_ref, y_ref, m_ref, o_ref, *, d):
    o_ref[...] = x_ref[...] + m_ref[0][:, 5 * d:6 * d] * y_ref[...]


def _residual(x, y, mods, mod_row0, seq):
    m, d = y.shape
    tile = min(ROW_TILE, seq)
    return pl.pallas_call(
        functools.partial(_residual_kernel, d=d),
        grid=(m // tile,),
        in_specs=[pl.BlockSpec((tile, d), lambda i: (i, 0)),
                  pl.BlockSpec((tile, d), lambda i: (i, 0)),
                  _mod_spec(mod_row0, seq, tile, mods.shape[-1])],
        out_specs=pl.BlockSpec((tile, d), lambda i: (i, 0)),
        out_shape=jax.ShapeDtypeStruct((m, d), F32),
        compiler_params=_cparams("arbitrary"),
        name="peer_residual",
    )(x, y, mods)


def _peer(x, mods, mod_row0, seq, layer, nw, wq_bf16, keys_bf16, p_uv, p_u, p_v, sc_share):
    m, d = x.shape
    h, idx, gates = _peer_route(x, mods, mod_row0, seq, nw, wq_bf16, keys_bf16)
    n_blocks = m // PEER_BLOCK
    sc_blocks = (n_blocks * sc_share[0]) // sc_share[1]
    m_sc = sc_blocks * PEER_BLOCK
    n_experts = p_u.shape[1]
    idx_sc = idx[:, :m_sc].T + layer * n_experts
    y_sc = _peer_experts_sc(idx_sc, h, gates[:, :m_sc].T,
                            p_u.reshape(-1, d), p_v.reshape(-1, d))
    x_sc = _residual(x, y_sc, mods, mod_row0, seq)
    if sc_blocks == n_blocks:
        return x_sc
    x_tc = _peer_gather(idx, h, gates, x, mods, mod_row0, seq, p_uv, layer, sc_blocks)
    return jnp.concatenate([x_sc, x_tc], axis=0)


def _head_mean_square(x):
    n = x.shape[1]
    r = lax.broadcasted_iota(I32, (n, n), 0) // HEAD_DIM
    c = lax.broadcasted_iota(I32, (n, n), 1) // HEAD_DIM
    seg = jnp.where(r == c, 1.0 / HEAD_DIM, 0.0).astype(F32)
    return jnp.dot(x * x, seg, precision=HIGHEST, preferred_element_type=F32)


def _swap_rot_halves(x):
    n = x.shape[1]
    quarter = HEAD_DIM // 4
    lane = lax.broadcasted_iota(I32, x.shape, 1)
    lo = (lane % (2 * quarter)) < quarter
    return jnp.where(lo, pltpu.roll(x, n - quarter, axis=1), pltpu.roll(x, quarter, axis=1))


def _qkprep_kernel(q_ref, k_ref, qw_ref, kw_ref, *rest, rope):
    if rope:
        cos_ref, sin_ref, qo_ref, ko_ref = rest
    else:
        qo_ref, ko_ref = rest
    q = q_ref[...]
    k = k_ref[...]
    q = q * lax.rsqrt(_head_mean_square(q) + NORM_EPS) * qw_ref[...]
    k = k * lax.rsqrt(_head_mean_square(k) + NORM_EPS) * kw_ref[...]
    if rope:
        cos = cos_ref[...]
        sin = sin_ref[...]
        cq = jnp.concatenate([cos] * (q.shape[1] // LANES), axis=1)
        sq = jnp.concatenate([sin] * (q.shape[1] // LANES), axis=1)
        q = q * cq + _swap_rot_halves(q) * sq
        k = k * cos + _swap_rot_halves(k) * sin
    qo_ref[...] = q
    ko_ref[...] = k


def _rope_tables(seq):
    axis_dim = HEAD_DIM // 2
    inv_freq = ROPE_THETA ** (-jnp.arange(0, axis_dim, 2, dtype=F32) / axis_dim)
    t = jnp.arange(seq)
    pos = jnp.stack([(t // GRID_W).astype(F32), (t % GRID_W).astype(F32)], axis=1)
    lane = jnp.arange(LANES)
    dd = lane % HEAD_DIM
    ang = pos[:, dd // axis_dim] * inv_freq[dd % (axis_dim // 2)][None, :]
    sign = jnp.where((dd % axis_dim) < axis_dim // 2, -1.0, 1.0).astype(F32)
    return jnp.cos(ang), jnp.sin(ang) * sign[None, :]


def _qkprep(q, k, qw, kw, seq, rope):
    m, nq = q.shape
    nk = k.shape[1]
    tile = min(ROW_TILE, seq)
    qw_row = jnp.tile(qw, nq // HEAD_DIM).reshape(1, nq)
    kw_row = jnp.tile(kw, nk // HEAD_DIM).reshape(1, nk)
    in_specs = [pl.BlockSpec((tile, nq), lambda i: (i, 0)),
                pl.BlockSpec((tile, nk), lambda i: (i, 0)),
                pl.BlockSpec((1, nq), lambda i: (0, 0)),
                pl.BlockSpec((1, nk), lambda i: (0, 0))]
    args = [q, k, qw_row, kw_row]
    if rope:
        cos, sin = _rope_tables(seq)
        per_seq = seq // tile
        in_specs += [pl.BlockSpec((tile, LANES), lambda i: (i % per_seq, 0)),
                     pl.BlockSpec((tile, LANES), lambda i: (i % per_seq, 0))]
        args += [cos, sin]
    return pl.pallas_call(
        functools.partial(_qkprep_kernel, rope=rope),
        grid=(m // tile,),
        in_specs=in_specs,
        out_specs=[pl.BlockSpec((tile, nq), lambda i: (i, 0)),
                   pl.BlockSpec((tile, nk), lambda i: (i, 0))],
        out_shape=[jax.ShapeDtypeStruct((m, nq), F32), jax.ShapeDtypeStruct((m, nk), F32)],
        compiler_params=_cparams("arbitrary"),
        name="qk_prep",
    )(*args)


def _dup_halves(x):
    lane = lax.broadcasted_iota(I32, x.shape, 1)
    sw = pltpu.roll(x, HEAD_DIM, axis=1)
    lo = lane < HEAD_DIM
    return jnp.where(lo, x, sw), jnp.where(lo, sw, x)


def _attend(q, k_all, v_all, sink_ref, mask):
    scale = HEAD_DIM ** -0.5
    nt = (((1,), (1,)), ((), ()))
    kk = [a.astype(BF16) for a in _dup_halves(k_all)]
    vv = [a.astype(BF16) for a in _dup_halves(v_all)]
    lane = lax.broadcasted_iota(I32, (q.shape[0], LANES), 1)
    lo = lane < HEAD_DIM
    tiles = []
    for t in range(q.shape[1] // LANES):
        qt = q[:, t * LANES:(t + 1) * LANES]
        g = (2 * t) // GQA_GROUP
        halves = []
        for hh in range(2):
            head = 2 * t + hh
            qm = jnp.where(lo if hh == 0 else ~lo, qt, 0.0).astype(BF16)
            s = lax.dot_general(qm, kk[g], nt, preferred_element_type=F32) * scale
            if mask is not None:
                s = jnp.where(mask, s, NEG_BIG)
            sink = sink_ref[head]
            mx = jnp.maximum(jnp.max(s, axis=1, keepdims=True), sink)
            p = jnp.exp(s - mx)
            den = jnp.sum(p, axis=1, keepdims=True) + jnp.exp(sink - mx)
            p = (p / den).astype(BF16)
            halves.append(jnp.dot(p, vv[g], preferred_element_type=F32))
        tiles.append(jnp.where(lo, halves[0], halves[1]))
    return jnp.concatenate(tiles, axis=1)


def _ctx_attn_kernel(sink_ref, q_ref, k_ref, v_ref, o_ref):
    o_ref[...] = _attend(q_ref[...], k_ref[...], v_ref[...], sink_ref, None)


def _ctx_attention(q, k, v, sink, seq):
    m, nq = q.shape
    nk = k.shape[1]
    return pl.pallas_call(
        _ctx_attn_kernel,
        grid=(m // seq,),
        in_specs=[pl.BlockSpec(memory_space=pltpu.SMEM),
                  pl.BlockSpec((seq, nq), lambda b: (b, 0)),
                  pl.BlockSpec((seq, nk), lambda b: (b, 0)),
                  pl.BlockSpec((seq, nk), lambda b: (b, 0))],
        out_specs=pl.BlockSpec((seq, nq), lambda b: (b, 0)),
        out_shape=jax.ShapeDtypeStruct((m, nq), F32),
        compiler_params=_cparams("arbitrary"),
        name="ctx_attention",
    )(sink, q, k, v)


def _lat_attn_kernel(sink_ref, q_ref, kc_ref, vc_ref, kp_ref, k0_ref, kn_ref, vp_ref, v0_ref, vn_ref,
                     o_ref, *, seq):
    qb = pl.program_id(1)
    blk = q_ref.shape[0]
    n_ctx = kc_ref.shape[1]
    k_all = jnp.concatenate([kc_ref[0], kp_ref[...], k0_ref[...], kn_ref[...]], axis=0)
    v_all = jnp.concatenate([vc_ref[0], vp_ref[...], v0_ref[...], vn_ref[...]], axis=0)
    tk = k_all.shape[0]
    qpos = qb * blk + lax.broadcasted_iota(I32, (blk, tk), 0)
    col = lax.broadcasted_iota(I32, (blk, tk), 1)
    kpos = (qb - 1) * blk + col - n_ctx
    local_ok = (jnp.abs(qpos - kpos) <= WINDOW) & (kpos >= 0) & (kpos < seq)
    mask = (col < n_ctx) | local_ok
    o_ref[...] = _attend(q_ref[...], k_all, v_all, sink_ref, mask)


def _lat_attention(q, k, v, k_ctx, v_ctx, sink, seq):
    m, nq = q.shape
    nk = k.shape[1]
    blk = WINDOW
    nb = seq // blk
    n_ctx = k_ctx.shape[1]
    last = m // blk - 1

    def kv_spec(shift):
        return pl.BlockSpec((blk, nk), lambda b, i: (jnp.clip(b * nb + i + shift, 0, last), 0))

    ctx_spec = pl.BlockSpec((1, n_ctx, nk), lambda b, i: (b, 0, 0))
    return pl.pallas_call(
        functools.partial(_lat_attn_kernel, seq=seq),
        grid=(m // seq, nb),
        in_specs=[pl.BlockSpec(memory_space=pltpu.SMEM),
                  pl.BlockSpec((blk, nq), lambda b, i: (b * nb + i, 0)),
                  ctx_spec, ctx_spec,
                  kv_spec(-1), kv_spec(0), kv_spec(1),
                  kv_spec(-1), kv_spec(0), kv_spec(1)],
        out_specs=pl.BlockSpec((blk, nq), lambda b, i: (b * nb + i, 0)),
        out_shape=jax.ShapeDtypeStruct((m, nq), F32),
        compiler_params=_cparams("arbitrary", "arbitrary"),
        name="lat_attention",
    )(sink, q, k_ctx, v_ctx, k, k, k, v, v, v)

SSD_BLOCK = 256
SSD_PAIRS = SSD_HEADS // 2
SSD_INNER = SSD_HEADS * SSD_HEAD_DIM
HALO = SUBLANES


def _softplus(x):
    return jnp.maximum(x, 0.0) + jnp.log1p(jnp.exp(-jnp.abs(x)))


def _silu(x):
    return x * jax.nn.sigmoid(x)


def _ssd_decays(dt_raw, bias, a_log):
    n = dt_raw.shape[0]
    dt = _softplus(dt_raw + bias)
    log_a = dt * (-jnp.exp(a_log))
    r = lax.broadcasted_iota(I32, (n, n), 0)
    c = lax.broadcasted_iota(I32, (n, n), 1)
    lower = jnp.where(c <= r, 1.0, 0.0).astype(F32)
    upper = jnp.where(r <= c, 1.0, 0.0).astype(F32)
    cum_col = jnp.dot(lower, log_a, precision=HIGHEST, preferred_element_type=F32)
    dt_row = dt.T
    la_row = log_a.T
    cum_row = jnp.dot(la_row, upper, precision=HIGHEST, preferred_element_type=F32)
    return dt, log_a, cum_col, dt_row, la_row, cum_row


def _ssd_scan_chunk(xs, bmat, cmat, w_of, q_scale_of, k_scale_of, carry_of, s_ref):
    nt = (((1,), (1,)), ((), ()))
    n = xs.shape[0]
    lane = lax.broadcasted_iota(I32, (n, LANES), 1)
    lo = lane < SSD_HEAD_DIM
    lane_s = lax.broadcasted_iota(I32, (D_STATE, LANES), 1)
    lo_s = lane_s < SSD_HEAD_DIM
    b_t = bmat.T
    cb16 = cmat.astype(BF16)
    ys = []
    for pair in range(SSD_PAIRS):
        g = (2 * pair) // (SSD_HEADS // SSD_GROUPS)
        in_g = (lane // D_STATE) == g
        cg = jnp.where(in_g, cmat, 0.0)
        cb = lax.dot_general(cg.astype(BF16), bmat.astype(BF16), nt, preferred_element_type=F32)
        x_pair = xs[:, pair * LANES:(pair + 1) * LANES]
        x16 = x_pair.astype(BF16)
        s_old = s_ref[pair]
        s2 = jnp.concatenate([s_old, s_old], axis=0).astype(BF16)
        bg_t = b_t[g * D_STATE:(g + 1) * D_STATE, :]
        y_h, s_h = [], []
        for hh in range(2):
            h = 2 * pair + hh
            w = (cb * w_of(h)).astype(BF16)
            y = jnp.dot(w, x16, preferred_element_type=F32)
            cq = (cg * q_scale_of(h)).astype(BF16)
            y = y + jnp.dot(cq, s2, preferred_element_type=F32)
            y_h.append(y)
            kt = (bg_t * k_scale_of(h)).astype(BF16)
            s_h.append(carry_of(h) * s_old + jnp.dot(kt, x16, preferred_element_type=F32))
        ys.append(jnp.where(lo, y_h[0], y_h[1]))
        s_ref[pair] = jnp.where(lo_s, s_h[0], s_h[1])
    return jnp.concatenate(ys, axis=1)


def _ssd_fwd_kernel(x_ref, xp_ref, xn_ref, dt_ref, s0_ref, cw_ref, cb_ref, bias_ref, alog_ref,
                    y_ref, xc_ref, sfin_ref, s_ref):
    c = pl.program_id(1)
    nc = pl.num_programs(1)
    n = x_ref.shape[0]

    @pl.when(c == 0)
    def _():
        s_ref[...] = s0_ref[0]

    prev = jnp.where(c > 0, xp_ref[...], 0.0)
    nxt = jnp.where(c < nc - 1, xn_ref[...], 0.0)
    xe = jnp.concatenate([prev, x_ref[...], nxt], axis=0)
    pad = (CONV_K - 1) // 2
    acc = cb_ref[...] + cw_ref[0:1, :] * xe[HALO - pad:HALO - pad + n, :]
    for k in range(1, CONV_K):
        acc = acc + cw_ref[k:k + 1, :] * xe[HALO - pad + k:HALO - pad + k + n, :]
    xc = _silu(acc)
    xc_ref[...] = xc
    xs = xc[:, :SSD_INNER]
    bmat = xc[:, SSD_INNER:SSD_INNER + LANES]
    cmat = xc[:, SSD_INNER + LANES:SSD_INNER + 2 * LANES]

    dt, log_a, cum_col, dt_row, la_row, cum_row = _ssd_decays(dt_ref[...], bias_ref[...], alog_ref[...])
    r = lax.broadcasted_iota(I32, (n, n), 0)
    cc = lax.broadcasted_iota(I32, (n, n), 1)
    causal = cc <= r
    last_col = cum_col[n - 1:n, :]

    def w_of(h):
        seg = cum_col[:, h:h + 1] - cum_row[h:h + 1, :]
        return jnp.exp(jnp.where(causal, seg, NEG_BIG)) * dt_row[h:h + 1, :]

    def q_scale_of(h):
        return jnp.exp(cum_col[:, h:h + 1])

    def k_scale_of(h):
        return dt_row[h:h + 1, :] * jnp.exp(cum_row[h:h + 1, n - 1:n] - cum_row[h:h + 1, :])

    def carry_of(h):
        return jnp.exp(last_col[:, h:h + 1])

    y_ref[...] = _ssd_scan_chunk(xs, bmat, cmat, w_of, q_scale_of, k_scale_of, carry_of, s_ref)

    @pl.when(c == nc - 1)
    def _():
        sfin_ref[0] = s_ref[...]


def _ssd_bwd_kernel(xc_ref, dt_ref, yf_ref, z_ref, s0_ref, bias_ref, alog_ref, dskip_ref, nw_ref,
                    y_ref, sfin_ref, s_ref):
    c = pl.program_id(1)
    nc = pl.num_programs(1)
    n = xc_ref.shape[0]

    @pl.when(c == 0)
    def _():
        s_ref[...] = s0_ref[0]

    xc = xc_ref[...]
    xs = xc[:, :SSD_INNER]
    bmat = xc[:, SSD_INNER:SSD_INNER + LANES]
    cmat = xc[:, SSD_INNER + LANES:SSD_INNER + 2 * LANES]
    dt, log_a, cum_col, dt_row, la_row, cum_row = _ssd_decays(dt_ref[...], bias_ref[...], alog_ref[...])
    ex_col = cum_col - log_a
    ex_row = cum_row - la_row
    r = lax.broadcasted_iota(I32, (n, n), 0)
    cc = lax.broadcasted_iota(I32, (n, n), 1)
    anti = cc >= r
    tot_col = cum_col[n - 1:n, :]
    off = SSD_HEADS

    def w_of(h):
        j = off + h
        seg = ex_row[j:j + 1, :] - ex_col[:, j:j + 1]
        return jnp.exp(jnp.where(anti, seg, NEG_BIG)) * dt_row[j:j + 1, :]

    def q_scale_of(h):
        j = off + h
        return jnp.exp(tot_col[:, j:j + 1] - ex_col[:, j:j + 1])

    def k_scale_of(h):
        j = off + h
        return dt_row[j:j + 1, :] * jnp.exp(ex_row[j:j + 1, :])

    def carry_of(h):
        j = off + h
        return jnp.exp(tot_col[:, j:j + 1])

    y_b = _ssd_scan_chunk(xs, bmat, cmat, w_of, q_scale_of, k_scale_of, carry_of, s_ref)
    y = yf_ref[...] + y_b + dskip_ref[...] * xs
    y = y * _silu(z_ref[...])
    ms = jnp.mean(y * y, axis=-1, keepdims=True)
    y_ref[...] = y * lax.rsqrt(ms + NORM_EPS) * nw_ref[...]

    @pl.when(c == nc - 1)
    def _():
        sfin_ref[0] = s_ref[...]


def _pair_states(s):
    b, h, n, p = s.shape
    return s.reshape(b, h // 2, 2, n, p).transpose(0, 1, 3, 2, 4).reshape(b, h // 2, n, 2 * p)


def _unpair_states(s):
    b, hp, n, p2 = s.shape
    return s.reshape(b, hp, n, 2, p2 // 2).transpose(0, 1, 3, 2, 4).reshape(b, hp * 2, n, p2 // 2)


def _ssd(xbc, dt, z, s0_f, s0_b, conv_w, conv_b, dt_bias, a_log, d_skip, ssd_norm, seq):
    m, nx = xbc.shape
    nb = m // seq
    blk = min(SSD_BLOCK, seq)
    nc = seq // blk
    hb = blk // HALO
    n_halo = m // HALO
    pad16 = lambda a: jnp.pad(a.reshape(1, -1), ((0, 0), (0, LANES - a.size)))
    bias = pad16(dt_bias)
    alog = pad16(a_log)
    state_spec = pl.BlockSpec((1, SSD_PAIRS, D_STATE, LANES), lambda b, c: (b, 0, 0, 0))
    state_shape = jax.ShapeDtypeStruct((nb, SSD_PAIRS, D_STATE, LANES), F32)
    row = lambda width: pl.BlockSpec((1, width), lambda b, c: (0, 0))

    def fwd_rows(width):
        return pl.BlockSpec((blk, width), lambda b, c: (b * nc + c, 0))

    def bwd_rows(width):
        return pl.BlockSpec((blk, width), lambda b, c: (b * nc + nc - 1 - c, 0))

    y_f, xc, s_f = pl.pallas_call(
        _ssd_fwd_kernel,
        grid=(nb, nc),
        in_specs=[fwd_rows(nx),
                  pl.BlockSpec((HALO, nx), lambda b, c: (jnp.maximum((b * nc + c) * hb - 1, 0), 0)),
                  pl.BlockSpec((HALO, nx), lambda b, c: (jnp.minimum((b * nc + c + 1) * hb, n_halo - 1), 0)),
                  fwd_rows(LANES), state_spec,
                  pl.BlockSpec((CONV_K, nx), lambda b, c: (0, 0)), row(nx), row(LANES), row(LANES)],
        out_specs=[fwd_rows(SSD_INNER), fwd_rows(nx), state_spec],
        out_shape=[jax.ShapeDtypeStruct((m, SSD_INNER), F32), jax.ShapeDtypeStruct((m, nx), F32), state_shape],
        scratch_shapes=[pltpu.VMEM((SSD_PAIRS, D_STATE, LANES), F32)],
        compiler_params=_cparams("arbitrary", "arbitrary"),
        name="ssd_forward",
    )(xbc, xbc, xbc, dt, _pair_states(s0_f), conv_w, conv_b.reshape(1, nx), bias, alog)

    dskip = jnp.repeat(d_skip, SSD_HEAD_DIM).reshape(1, SSD_INNER)
    y, s_b = pl.pallas_call(
        _ssd_bwd_kernel,
        grid=(nb, nc),
        in_specs=[bwd_rows(nx), bwd_rows(LANES), bwd_rows(SSD_INNER), bwd_rows(SSD_INNER), state_spec,
                  row(LANES), row(LANES), row(SSD_INNER), row(SSD_INNER)],
        out_specs=[bwd_rows(SSD_INNER), state_spec],
        out_shape=[jax.ShapeDtypeStruct((m, SSD_INNER), F32), state_shape],
        scratch_shapes=[pltpu.VMEM((SSD_PAIRS, D_STATE, LANES), F32)],
        compiler_params=_cparams("arbitrary", "arbitrary"),
        name="ssd_backward",
    )(xc, dt, y_f, z, _pair_states(s0_b), bias, alog, dskip, ssd_norm.reshape(1, SSD_INNER))
    return y, _unpair_states(s_f), _unpair_states(s_b)

def _hgrn_kernel(q_ref, ff_ref, fb_ref, i_ref, g_ref, lb_ref, s0_ref, nw_ref, o_ref, sfin_ref,
                 s_ref, *, layer):
    t_len = q_ref.shape[0]
    n = HGRN_CHUNK
    n_chunks = t_len // n
    tn = (((0,), (0,)), ((), ()))
    nt = (((1,), (1,)), ((), ()))

    lbp = lb_ref[...]
    e = jnp.exp(lbp - jnp.max(lbp, axis=0, keepdims=True))
    sm = e / jnp.sum(e, axis=0, keepdims=True)
    lb = sm[0] * 0.0
    for j in range(1, layer + 1):
        lb = lb + sm[j]

    r = lax.broadcasted_iota(I32, (n, n), 0)
    c = lax.broadcasted_iota(I32, (n, n), 1)
    lower = jnp.where(c <= r, 1.0, 0.0).astype(F32)
    srow = lax.broadcasted_iota(I32, (n, HGRN_DK), 0)
    qscale = HGRN_DK ** -0.5

    def chunk(row0, f_ref, lb_d, reverse):
        q = _silu(q_ref[pl.ds(row0, n), :]) * qscale
        f = f_ref[pl.ds(row0, n), :]
        v = i_ref[pl.ds(row0, n), :]
        k = (1.0 - lb_d) * jax.nn.sigmoid(-f)
        lf = jnp.log(lb_d + (1.0 - lb_d) * jax.nn.sigmoid(f))
        cum = jnp.dot(lower, lf, precision=HIGHEST, preferred_element_type=F32)
        tot = cum[n - 1:n, :]
        if reverse:
            cum = cum - lf
        rows = []
        for t in range(n):
            if reverse:
                seg = jnp.where(srow >= t, cum - cum[t:t + 1, :], NEG_BIG)
            else:
                seg = jnp.where(srow <= t, cum[t:t + 1, :] - cum, NEG_BIG)
            a = q[t:t + 1, :] * k * jnp.exp(seg)
            sc = jnp.sum(a, axis=1, keepdims=True)
            rows.append(jnp.sum(sc * v, axis=0, keepdims=True))
        o = jnp.concatenate(rows, axis=0)
        s_old = s_ref[...]
        if reverse:
            q_in = q * jnp.exp(tot - cum)
            k_out = k * jnp.exp(cum)
        else:
            q_in = q * jnp.exp(cum)
            k_out = k * jnp.exp(tot - cum)
        o = o + lax.dot_general(q_in.astype(BF16), s_old.astype(BF16), nt, preferred_element_type=F32)
        s_ref[...] = jnp.exp(tot) * s_old + lax.dot_general(
            v.astype(BF16), k_out.astype(BF16), tn, preferred_element_type=F32)
        return o

    s_ref[...] = s0_ref[0, 0, 0].T

    def fwd_body(ci, carry):
        row0 = pl.multiple_of(ci * n, n)
        o_ref[pl.ds(row0, n), :] = chunk(row0, ff_ref, lb[0:1, :], False)
        return carry

    lax.fori_loop(0, n_chunks, fwd_body, 0)
    sfin_ref[0, 0, 0] = s_ref[...].T

    s_ref[...] = s0_ref[0, 1, 0].T
    nw = nw_ref[...]

    def bwd_body(ci, carry):
        row0 = pl.multiple_of((n_chunks - 1 - ci) * n, n)
        o = o_ref[pl.ds(row0, n), :] + chunk(row0, fb_ref, lb[1:2, :], True)
        ms = jnp.mean(o * o, axis=-1, keepdims=True)
        o = o * lax.rsqrt(ms + NORM_EPS) * nw
        o_ref[pl.ds(row0, n), :] = o * _silu(g_ref[pl.ds(row0, n), :])
        return carry

    lax.fori_loop(0, n_chunks, bwd_body, 0)
    sfin_ref[0, 1, 0] = s_ref[...].T


def _hgrn(q, f_fw, f_bw, iv, g, o_lb, state0, g_norm, seq, layer):
    m, width = q.shape
    nb = m // seq
    dv = width // HGRN_HEADS
    col = pl.BlockSpec((seq, dv), lambda b, h: (b, h))
    state_spec = pl.BlockSpec((1, 2, 1, HGRN_DK, dv), lambda b, h: (b, 0, h, 0, 0))
    return pl.pallas_call(
        functools.partial(_hgrn_kernel, layer=layer),
        grid=(nb, HGRN_HEADS),
        in_specs=[col, col, col, col, col,
                  pl.BlockSpec((o_lb.shape[0], 2, HGRN_DK), lambda b, h: (0, 0, h)),
                  state_spec,
                  pl.BlockSpec((1, dv), lambda b, h: (0, 0))],
        out_specs=[col, state_spec],
        out_shape=[jax.ShapeDtypeStruct((m, width), F32),
                   jax.ShapeDtypeStruct((nb, 2, HGRN_HEADS, HGRN_DK, dv), F32)],
        scratch_shapes=[pltpu.VMEM((dv, HGRN_DK), F32)],
        compiler_params=_cparams("arbitrary", "arbitrary"),
        name="hgrn2",
    )(q, f_fw, f_bw, iv, g, o_lb, state0, g_norm.reshape(1, dv))

EVEN_SPLITS = ((0, 512), (512, 640), (640, 768), (768, 1280), (1280, 2048), (2048, 2176))
HGRN_SPLITS = tuple((i * 1024, (i + 1) * 1024) for i in range(5))


def _even_weight(w):
    main = EVEN_SPLITS[-1][0]
    return jnp.pad(w, ((0, 0), (0, LANES - (w.shape[1] - main)))).astype(BF16)


def _run_trunk(x3, mods, mod_row0, P, cache):
    nb, seq, d = x3.shape
    x = x3.reshape(nb * seq, d)
    depth = P['norm_mix'].shape[0]
    ks, vs, ssd_states, hgrn_states = [], [], [], []
    for l in range(depth):
        j = l // 2
        row0 = (l * SUBLANES + mod_row0, 0 if cache is None else 1)
        if l % 2 == 0:
            q, k, v, z, xbc, dt = _inproj(x, mods, row0, seq, P['norm_mix'][l], P['e_w_in'][j],
                                          EVEN_SPLITS, "even_in_proj")
            q, k = _qkprep(q, k, P['e_q_norm'][j], P['e_k_norm'][j], seq, rope=cache is not None)
            if cache is None:
                s0_f = jnp.zeros((nb, SSD_HEADS, D_STATE, SSD_HEAD_DIM), F32)
                s0_b = s0_f
                o_attn = _ctx_attention(q, k, v, P['e_sink'][j], seq)
            else:
                s0_f, s0_b = cache[2][:, j, 0], cache[2][:, j, 1]
                n_ctx = cache[0].shape[2]
                o_attn = _lat_attention(q, k, v, cache[0][:, j].reshape(nb, n_ctx, -1),
                                        cache[1][:, j].reshape(nb, n_ctx, -1), P['e_sink'][j], seq)
            y, s_f, s_b = _ssd(xbc, dt, z, s0_f, s0_b, P['e_conv_w'][j], P['e_conv_b'][j],
                               P['e_dt_bias'][j], P['e_a_log'][j], P['e_d_skip'][j], P['e_ssd_norm'][j], seq)
            if cache is None:
                ks.append(k.reshape(nb, seq, N_KV_HEADS, HEAD_DIM))
                vs.append(v.reshape(nb, seq, N_KV_HEADS, HEAD_DIM))
                ssd_states.append(jnp.stack([s_f, s_b], axis=1))
            mix = jnp.concatenate([o_attn, y], axis=1)
            x = _outproj(mix, x, mods, row0, seq, P['e_w_out'][j], "even_out_proj")
        else:
            q, f_fw, f_bw, iv, g = _inproj(x, mods, row0, seq, P['norm_mix'][l], P['o_w_in'][j],
                                           HGRN_SPLITS, "odd_in_proj")
            if cache is None:
                s0 = jnp.zeros((nb, 2, HGRN_HEADS, HGRN_DK, d // HGRN_HEADS), F32)
            else:
                s0 = cache[3][:, j]
            o, s_new = _hgrn(q, f_fw, f_bw, iv, g, P['o_lb'], s0, P['o_g_norm'][j], seq, j)
            if cache is None:
                hgrn_states.append(s_new)
            x = _outproj(o, x, mods, row0, seq, P['o_w_out'][j], "odd_out_proj")
        x = _peer(x, mods, row0, seq, l, P['norm_ffn'][l], P['p_w_q'][l], P['p_sub_keys'][l],
                  P['p_uv'], P['p_u'], P['p_v'],
                  SC_SHARE_CONTEXT if cache is None else SC_SHARE_LATENT)
    y = x.reshape(nb, seq, d)
    if cache is not None:
        return y, None
    return y, (jnp.stack(ks, axis=1), jnp.stack(vs, axis=1),
               jnp.stack(ssd_states, axis=1), jnp.stack(hgrn_states, axis=1))


def kernel(x_prompt, x_sample, cache_k, cache_v, state_ssd, state_hgrn, c, c_ctx, w_ada, b_ada, norm_mix, norm_ffn, e_w_in, e_q_norm, e_k_norm, e_sink, e_conv_w, e_conv_b, e_dt_bias, e_a_log, e_d_skip, e_ssd_norm, e_w_out, o_w_in, o_lb, o_g_norm, o_w_out, p_w_q, p_sub_keys, p_u, p_v):
    depth, d, d6 = w_ada.shape
    b_lat = x_sample.shape[0]
    cond_rows = jnp.concatenate([c_ctx[None, :], c, jnp.zeros((SUBLANES - 1 - b_lat, d), F32)], axis=0)
    mods = _modulation(cond_rows, w_ada, b_ada).reshape(depth * SUBLANES, 1, d6)
    P = {
        'norm_mix': norm_mix, 'norm_ffn': norm_ffn,
        'e_w_in': jnp.stack([_even_weight(w) for w in e_w_in]), 'e_q_norm': e_q_norm, 'e_k_norm': e_k_norm,
        'e_sink': e_sink, 'e_conv_w': e_conv_w, 'e_conv_b': e_conv_b, 'e_dt_bias': e_dt_bias,
        'e_a_log': e_a_log, 'e_d_skip': e_d_skip, 'e_ssd_norm': e_ssd_norm,
        'e_w_out': e_w_out.astype(BF16),
        'o_w_in': o_w_in.astype(BF16), 'o_lb': o_lb, 'o_g_norm': o_g_norm, 'o_w_out': o_w_out.astype(BF16),
        'p_w_q': p_w_q.astype(BF16),
        'p_sub_keys': p_sub_keys.astype(BF16).reshape(depth, PEER_HEADS * 2, PEER_NKEYS, PEER_DKEY),
        'p_uv': jnp.stack([p_u, p_v], axis=2).reshape(depth, p_u.shape[1] * 2 * d // LANES, 1, LANES),
        'p_u': p_u, 'p_v': p_v,
    }
    y_prompt, new_state = _run_trunk(x_prompt, mods, 0, P, None)
    y_sample, _ = _run_trunk(x_sample, mods, 1, P, (cache_k, cache_v, state_ssd, state_hgrn))
    return (y_prompt, y_sample) + new_state
```

```python
import functools
import math

import jax
import jax.numpy as jnp
from jax import lax
from jax.experimental import pallas as pl
from jax.experimental.pallas import tpu as pltpu
from jax.experimental.pallas import tpu_sc as plsc

F32 = jnp.float32
BF16 = jnp.bfloat16
I32 = jnp.int32
HIGHEST = lax.Precision.HIGHEST

NORM_EPS = 1e-6
NEG_BIG = -1e30
LANES = 128
SUBLANES = 8
VMEM_LIMIT = 48 * 1024 * 1024

GRID_W = 64
HEAD_DIM = 64
N_Q_HEADS = 8
N_KV_HEADS = 2
GQA_GROUP = 4
WINDOW = 128
ROPE_THETA = 10000.0
SSD_HEADS = 8
SSD_HEAD_DIM = 64
SSD_GROUPS = 2
D_STATE = 64
CONV_K = 5
HGRN_HEADS = 8
HGRN_DK = 128
HGRN_CHUNK = 32
PEER_HEADS = 8
PEER_NKEYS = 128
PEER_TOPK = 16
PEER_DKEY = 128
PEER_PAIRS = PEER_HEADS * PEER_TOPK

ROW_TILE = 256
PEER_BLOCK = 128
SC_LANES = 16
SC_CHUNK = 16
SC_SHARE_CONTEXT = (1, 1)
SC_SHARE_LATENT = (13, 16)
GELU_C = math.sqrt(2.0 / math.pi)


def _cparams(*sem):
    return pltpu.CompilerParams(dimension_semantics=sem, vmem_limit_bytes=VMEM_LIMIT)


def _norm_mod(x, nw, scale, shift):
    ms = jnp.mean(x * x, axis=-1, keepdims=True)
    return (x * lax.rsqrt(ms + NORM_EPS)) * nw * (1.0 + scale) + shift


def _mod_kernel(c_ref, w_ref, b_ref, o_ref):
    c = c_ref[...]
    s = c * jax.nn.sigmoid(c)
    o_ref[0] = jnp.dot(s, w_ref[0], precision=HIGHEST, preferred_element_type=F32) + b_ref[0]


def _modulation(cond_rows, w_ada, b_ada):
    depth, d, n = w_ada.shape
    rows = cond_rows.shape[0]
    return pl.pallas_call(
        _mod_kernel,
        grid=(depth, n // d),
        in_specs=[pl.BlockSpec((rows, d), lambda l, j: (0, 0)),
                  pl.BlockSpec((1, d, d), lambda l, j: (l, 0, j)),
                  pl.BlockSpec((1, 1, d), lambda l, j: (l, 0, j))],
        out_specs=pl.BlockSpec((1, rows, d), lambda l, j: (l, 0, j)),
        out_shape=jax.ShapeDtypeStruct((depth, rows, n), F32),
        compiler_params=_cparams("arbitrary", "arbitrary"),
        name="modulation",
    )(cond_rows, w_ada, b_ada.reshape(depth, 1, n))


def _mod_spec(mod_row0, seq, tile, d6, first_tile=0):
    row0, per_batch = mod_row0
    return pl.BlockSpec((1, 1, d6),
                        lambda i: (row0 + per_batch * (((i + first_tile) * tile) // seq), 0, 0))


def _inproj_kernel(x_ref, m_ref, nw_ref, w_ref, *o_refs, splits, d):
    m = m_ref[0]
    h = _norm_mod(x_ref[...], nw_ref[...], m[:, d:2 * d], m[:, 0:d]).astype(BF16)
    for o_ref, (a, b) in zip(o_refs, splits):
        o_ref[...] = jnp.dot(h, w_ref[:, a:b], preferred_element_type=F32)


def _inproj(x, mods, mod_row0, seq, nw, w_bf16, splits, name):
    m, d = x.shape
    n = w_bf16.shape[1]
    tile = min(ROW_TILE, seq)
    return pl.pallas_call(
        functools.partial(_inproj_kernel, splits=splits, d=d),
        grid=(m // tile,),
        in_specs=[pl.BlockSpec((tile, d), lambda i: (i, 0)),
                  _mod_spec(mod_row0, seq, tile, mods.shape[-1]),
                  pl.BlockSpec((1, d), lambda i: (0, 0)),
                  pl.BlockSpec((d, n), lambda i: (0, 0))],
        out_specs=[pl.BlockSpec((tile, b - a), lambda i: (i, 0)) for a, b in splits],
        out_shape=[jax.ShapeDtypeStruct((m, b - a), F32) for a, b in splits],
        compiler_params=_cparams("arbitrary"),
        name=name,
    )(x, mods, nw.reshape(1, d), w_bf16)


def _outproj_kernel(mix_ref, x_ref, m_ref, w_ref, o_ref, *, d):
    y = jnp.dot(mix_ref[...].astype(BF16), w_ref[...], preferred_element_type=F32)
    o_ref[...] = x_ref[...] + m_ref[0][:, 2 * d:3 * d] * y


def _outproj(mix, x, mods, mod_row0, seq, w_bf16, name):
    m, d = x.shape
    k = mix.shape[1]
    tile = min(ROW_TILE, seq)
    return pl.pallas_call(
        functools.partial(_outproj_kernel, d=d),
        grid=(m // tile,),
        in_specs=[pl.BlockSpec((tile, k), lambda i: (i, 0)),
                  pl.BlockSpec((tile, d), lambda i: (i, 0)),
                  _mod_spec(mod_row0, seq, tile, mods.shape[-1]),
                  pl.BlockSpec((k, d), lambda i: (0, 0))],
        out_specs=pl.BlockSpec((tile, d), lambda i: (i, 0)),
        out_shape=jax.ShapeDtypeStruct((m, d), F32),
        compiler_params=_cparams("arbitrary"),
        name=name,
    )(mix, x, mods, w_bf16)


def _topk_over_rows(s, k, payload=None):
    n = s.shape[0]
    iota = lax.broadcasted_iota(I32, s.shape, 0)
    vals, idxs, pays = [], [], []
    for _ in range(k):
        m = jnp.max(s, axis=0, keepdims=True)
        i = jnp.min(jnp.where(s == m, iota, n), axis=0, keepdims=True)
        hit = iota == i
        vals.append(m)
        idxs.append(i)
        if payload is not None:
            pays.append(jnp.max(jnp.where(hit, payload, -1), axis=0, keepdims=True))
        s = jnp.where(hit, -jnp.inf, s)
    out = (jnp.concatenate(vals, axis=0), jnp.concatenate(idxs, axis=0))
    if payload is not None:
        out += (jnp.concatenate(pays, axis=0),)
    return out


def _peer_route_kernel(x_ref, m_ref, nw_ref, wq_ref, keys_ref, h_ref, e_ref, g_ref, *, d):
    m = m_ref[0]
    h = _norm_mod(x_ref[...], nw_ref[...], m[:, 4 * d:5 * d], m[:, 3 * d:4 * d])
    h_ref[...] = h
    hb = h.astype(BF16)
    nt = (((1,), (1,)), ((), ()))
    for head in range(PEER_HEADS):
        tops = []
        for half in range(2):
            c0 = (head * 2 + half) * PEER_DKEY
            q = jnp.dot(hb, wq_ref[:, c0:c0 + PEER_DKEY], preferred_element_type=F32)
            s = lax.dot_general(keys_ref[head * 2 + half], q.astype(BF16), nt,
                                preferred_element_type=F32)
            tops.append(_topk_over_rows(s, PEER_TOPK))
        (s0, i0), (s1, i1) = tops
        widths = [PEER_TOPK // (a + 1) for a in range(PEER_TOPK)]
        n_pad = -sum(widths) % SUBLANES
        cand_s = jnp.concatenate([s0[a:a + 1] + s1[:w] for a, w in enumerate(widths)]
                                 + [jnp.full((n_pad, s0.shape[1]), -jnp.inf, F32)], axis=0)
        cand_e = jnp.concatenate([i0[a:a + 1] * PEER_NKEYS + i1[:w] for a, w in enumerate(widths)]
                                 + [jnp.zeros((n_pad, s0.shape[1]), I32)], axis=0)
        best_s, _, best_e = _topk_over_rows(cand_s, PEER_TOPK, payload=cand_e)
        p = jnp.exp(best_s - best_s[0:1])
        r0 = head * PEER_TOPK
        e_ref[r0:r0 + PEER_TOPK, :] = best_e
        g_ref[r0:r0 + PEER_TOPK, :] = p / jnp.sum(p, axis=0, keepdims=True)


def _peer_route(x, mods, mod_row0, seq, nw, wq_bf16, keys_bf16):
    m, d = x.shape
    tile = min(ROW_TILE, seq)
    nq = wq_bf16.shape[1]
    return pl.pallas_call(
        functools.partial(_peer_route_kernel, d=d),
        grid=(m // tile,),
        in_specs=[pl.BlockSpec((tile, d), lambda i: (i, 0)),
                  _mod_spec(mod_row0, seq, tile, mods.shape[-1]),
                  pl.BlockSpec((1, d), lambda i: (0, 0)),
                  pl.BlockSpec((d, nq), lambda i: (0, 0)),
                  pl.BlockSpec(keys_bf16.shape, lambda i: (0, 0, 0))],
        out_specs=[pl.BlockSpec((tile, d), lambda i: (i, 0)),
                   pl.BlockSpec((PEER_PAIRS, tile), lambda i: (0, i)),
                   pl.BlockSpec((PEER_PAIRS, tile), lambda i: (0, i))],
        out_shape=[jax.ShapeDtypeStruct((m, d), F32),
                   jax.ShapeDtypeStruct((PEER_PAIRS, m), I32),
                   jax.ShapeDtypeStruct((PEER_PAIRS, m), F32)],
        compiler_params=_cparams("arbitrary"),
        name="peer_route",
    )(x, mods, nw.reshape(1, d), wq_bf16, keys_bf16)


def _peer_gather_kernel(idx_hbm, h_ref, g_ref, x_ref, m_ref, u_hbm, v_hbm, o_ref,
                        idx_smem, ubuf, vbuf, sem_idx, sem_u, sem_v, *, layer, d):
    blk = pl.program_id(0)
    n_groups = PEER_BLOCK // SUBLANES

    cp = pltpu.make_async_copy(idx_hbm.at[blk], idx_smem, sem_idx)
    cp.start()
    cp.wait()

    def issue(tok, slot):
        def body(pair, carry):
            e = idx_smem[tok, pair]
            pltpu.make_async_copy(u_hbm.at[layer, pl.ds(e, 1)], ubuf.at[slot, pl.ds(pair, 1)],
                                  sem_u.at[slot]).start()
            pltpu.make_async_copy(v_hbm.at[layer, pl.ds(e, 1)], vbuf.at[slot, pl.ds(pair, 1)],
                                  sem_v.at[slot]).start()
            return carry
        lax.fori_loop(0, PEER_PAIRS, body, 0, unroll=8)

    def wait(slot):
        pltpu.make_async_copy(u_hbm.at[layer, pl.ds(0, PEER_PAIRS)], ubuf.at[slot], sem_u.at[slot]).wait()
        pltpu.make_async_copy(v_hbm.at[layer, pl.ds(0, PEER_PAIRS)], vbuf.at[slot], sem_v.at[slot]).wait()

    gate2 = m_ref[0][:, 5 * d:6 * d]
    lane = lax.broadcasted_iota(I32, (PEER_PAIRS, PEER_BLOCK), 1)

    issue(0, 0)

    def group(grp, carry):
        base = pl.multiple_of(grp * SUBLANES, SUBLANES)
        h8 = h_ref[pl.ds(base, SUBLANES), :]
        rows = []
        for r in range(SUBLANES):
            tok = base + r
            slot = r % 2
            if r < SUBLANES - 1:
                issue(tok + 1, 1 - slot)
            else:
                @pl.when(grp < n_groups - 1)
                def _():
                    issue(tok + 1, 1 - slot)
            wait(slot)
            act = jnp.sum(ubuf[slot] * h8[r:r + 1, :], axis=1, keepdims=True)
            gate = jnp.sum(jnp.where(lane == tok, g_ref[...], 0.0), axis=1, keepdims=True)
            w = jax.nn.gelu(act) * gate
            rows.append(jnp.sum(vbuf[slot] * w, axis=0, keepdims=True))
        out8 = jnp.concatenate(rows, axis=0)
        o_ref[pl.ds(base, SUBLANES), :] = x_ref[pl.ds(base, SUBLANES), :] + gate2 * out8
        return carry

    lax.fori_loop(0, n_groups, group, 0)


def _peer_gather(idx, h, gates, x, mods, mod_row0, seq, p_u, p_v, layer, first_block):
    m, d = x.shape
    nblk = m // PEER_BLOCK - first_block
    idx3 = idx[:, first_block * PEER_BLOCK:].T.reshape(nblk, PEER_BLOCK, PEER_PAIRS)
    rows = pl.BlockSpec((PEER_BLOCK, d), lambda i: (i + first_block, 0))
    return pl.pallas_call(
        functools.partial(_peer_gather_kernel, layer=layer, d=d),
        grid=(nblk,),
        in_specs=[pl.BlockSpec(memory_space=pl.ANY),
                  rows,
                  pl.BlockSpec((PEER_PAIRS, PEER_BLOCK), lambda i: (0, i + first_block)),
                  rows,
                  _mod_spec(mod_row0, seq, PEER_BLOCK, mods.shape[-1], first_block),
                  pl.BlockSpec(memory_space=pl.ANY),
                  pl.BlockSpec(memory_space=pl.ANY)],
        out_specs=pl.BlockSpec((PEER_BLOCK, d), lambda i: (i, 0)),
        out_shape=jax.ShapeDtypeStruct((nblk * PEER_BLOCK, d), F32),
        scratch_shapes=[pltpu.SMEM((PEER_BLOCK, PEER_PAIRS), I32),
                        pltpu.VMEM((2, PEER_PAIRS, d), F32),
                        pltpu.VMEM((2, PEER_PAIRS, d), F32),
                        pltpu.SemaphoreType.DMA,
                        pltpu.SemaphoreType.DMA((2,)),
                        pltpu.SemaphoreType.DMA((2,))],
        compiler_params=_cparams("arbitrary"),
        name="peer_gather",
    )(idx3, h, gates, x, mods, p_u, p_v)


def _peer_experts_sc(idx, h, gates, u_rows, v_rows):
    m = idx.shape[0]
    d = h.shape[1]
    info = plsc.get_sparse_core_info()
    n_workers = info.num_cores * info.num_subcores
    per = m // n_workers
    n_chunks = PEER_PAIRS // SC_CHUNK
    n_vec = d // SC_LANES
    mesh = plsc.VectorSubcoreMesh(core_axis_name="c", subcore_axis_name="s")

    @functools.partial(
        pl.kernel, out_type=jax.ShapeDtypeStruct((m, d), F32), mesh=mesh,
        scratch_types=[pltpu.VMEM((2, PEER_PAIRS), I32), pltpu.VMEM((2, d), F32),
                       pltpu.VMEM((2, PEER_PAIRS), F32), pltpu.VMEM((2, d), F32),
                       pltpu.VMEM((2, SC_CHUNK, d), F32), pltpu.VMEM((2, SC_CHUNK, d), F32),
                       pltpu.SemaphoreType.DMA((2,)), pltpu.SemaphoreType.DMA((2,)),
                       pltpu.SemaphoreType.DMA((2,)), pltpu.SemaphoreType.DMA((2,))],
        compiler_params=pltpu.CompilerParams(needs_layout_passes=False),
        name="peer_experts_sc")
    def body(idx_hbm, h_hbm, g_hbm, u_hbm, v_hbm, o_hbm,
             idx_v, x_v, g_v, out_v, ubuf, vbuf, sem_meta, sem_out, sem_u, sem_v):
        wid = lax.axis_index("c") * info.num_subcores + lax.axis_index("s")
        tok0 = wid * per
        lane = lax.iota(I32, SC_LANES)

        def meta_copies(ti, ms):
            t = tok0 + ti
            return (pltpu.make_async_copy(idx_hbm.at[t], idx_v.at[ms], sem_meta.at[ms]),
                    pltpu.make_async_copy(h_hbm.at[t], x_v.at[ms], sem_meta.at[ms]),
                    pltpu.make_async_copy(g_hbm.at[t], g_v.at[ms], sem_meta.at[ms]))

        def gather_copies(ms, c, slot):
            ids = idx_v.at[ms, pl.ds(c * SC_CHUNK, SC_CHUNK)]
            return (pltpu.make_async_copy(u_hbm.at[ids], ubuf.at[slot], sem_u.at[slot]),
                    pltpu.make_async_copy(v_hbm.at[ids], vbuf.at[slot], sem_v.at[slot]))

        def out_copy(ti, ms):
            return pltpu.make_async_copy(out_v.at[ms], o_hbm.at[tok0 + ti], sem_out.at[ms])

        for cp in meta_copies(0, 0):
            cp.start()
        for cp in meta_copies(0, 0):
            cp.wait()
        for cp in gather_copies(0, 0, 0):
            cp.start()

        def token(ti, carry):
            ms = ti % 2
            nxt = 1 - ms

            @pl.when(ti + 1 < per)
            def _():
                for cp in meta_copies(ti + 1, nxt):
                    cp.start()

            @pl.when(ti >= 2)
            def _():
                out_copy(ti - 2, ms).wait()

            def zero(j, c):
                out_v[ms, pl.ds(j * SC_LANES, SC_LANES)] = jnp.zeros((SC_LANES,), F32)
                return c
            lax.fori_loop(0, n_vec, zero, 0)

            for c in range(n_chunks):
                slot = c % 2
                if c + 1 < n_chunks:
                    for cp in gather_copies(ms, c + 1, 1 - slot):
                        cp.start()
                else:
                    @pl.when(ti + 1 < per)
                    def _():
                        for cp in meta_copies(ti + 1, nxt):
                            cp.wait()
                        for cp in gather_copies(nxt, 0, 1 - slot):
                            cp.start()
                cu, cv = gather_copies(ms, c, slot)
                cu.wait()

                def udot(j, accs):
                    xj = x_v[ms, pl.ds(j * SC_LANES, SC_LANES)]
                    return tuple(accs[r] + ubuf[slot, r, pl.ds(j * SC_LANES, SC_LANES)] * xj
                                 for r in range(SC_CHUNK))
                accs = lax.fori_loop(0, n_vec, udot,
                                     tuple(jnp.zeros((SC_LANES,), F32) for _ in range(SC_CHUNK)))
                act = jnp.zeros((SC_LANES,), F32)
                for r in range(SC_CHUNK):
                    act = jnp.where(lane == r, jnp.sum(accs[r]), act)
                y = GELU_C * (act + 0.044715 * (act * act * act))
                w = act / (1.0 + jnp.exp(-2.0 * y)) * g_v[ms, pl.ds(c * SC_CHUNK, SC_CHUNK)]
                ws = [jnp.sum(jnp.where(lane == r, w, 0.0)) for r in range(SC_CHUNK)]
                cv.wait()

                @plsc.parallel_loop(0, n_vec, unroll=2)
                def _(j):
                    parts = [ws[r] * vbuf[slot, r, pl.ds(j * SC_LANES, SC_LANES)] for r in range(SC_CHUNK)]
                    while len(parts) > 1:
                        parts = [parts[i] + parts[i + 1] for i in range(0, len(parts), 2)]
                    plsc.addupdate(out_v.at[ms, pl.ds(j * SC_LANES, SC_LANES)], parts[0])

            out_copy(ti, ms).start()
            return carry

        lax.fori_loop(0, per, token, 0)
        for back in (2, 1):
            if per >= back:
                out_copy(per - back, (per - back) % 2).wait()

    return body(idx, h, gates, u_rows, v_rows)


def _residual_kernel(x_ref, y_ref, m_ref, o_ref, *, d):
    o_ref[...] = x_ref[...] + m_ref[0][:, 5 * d:6 * d] * y_ref[...]


def _residual(x, y, mods, mod_row0, seq):
    m, d = y.shape
    tile = min(ROW_TILE, seq)
    return pl.pallas_call(
        functools.partial(_residual_kernel, d=d),
        grid=(m // tile,),
        in_specs=[pl.BlockSpec((tile, d), lambda i: (i, 0)),
                  pl.BlockSpec((tile, d), lambda i: (i, 0)),
                  _mod_spec(mod_row0, seq, tile, mods.shape[-1])],
        out_specs=pl.BlockSpec((tile, d), lambda i: (i, 0)),
        out_shape=jax.ShapeDtypeStruct((m, d), F32),
        compiler_params=_cparams("arbitrary"),
        name="peer_residual",
    )(x, y, mods)


def _peer(x, mods, mod_row0, seq, layer, nw, wq_bf16, keys_bf16, p_u, p_v, sc_share):
    m, d = x.shape
    h, idx, gates = _peer_route(x, mods, mod_row0, seq, nw, wq_bf16, keys_bf16)
    n_blocks = m // PEER_BLOCK
    sc_blocks = (n_blocks * sc_share[0]) // sc_share[1]
    m_sc = sc_blocks * PEER_BLOCK
    n_experts = p_u.shape[1]
    idx_sc = idx[:, :m_sc].T + layer * n_experts
    y_sc = _peer_experts_sc(idx_sc, h, gates[:, :m_sc].T,
                            p_u.reshape(-1, d), p_v.reshape(-1, d))
    x_sc = _residual(x, y_sc, mods, mod_row0, seq)
    if sc_blocks == n_blocks:
        return x_sc
    x_tc = _peer_gather(idx, h, gates, x, mods, mod_row0, seq, p_u, p_v, layer, sc_blocks)
    return jnp.concatenate([x_sc, x_tc], axis=0)


def _head_mean_square(x):
    n = x.shape[1]
    r = lax.broadcasted_iota(I32, (n, n), 0) // HEAD_DIM
    c = lax.broadcasted_iota(I32, (n, n), 1) // HEAD_DIM
    seg = jnp.where(r == c, 1.0 / HEAD_DIM, 0.0).astype(F32)
    return jnp.dot(x * x, seg, precision=HIGHEST, preferred_element_type=F32)


def _swap_rot_halves(x):
    n = x.shape[1]
    quarter = HEAD_DIM // 4
    lane = lax.broadcasted_iota(I32, x.shape, 1)
    lo = (lane % (2 * quarter)) < quarter
    return jnp.where(lo, pltpu.roll(x, n - quarter, axis=1), pltpu.roll(x, quarter, axis=1))


def _qkprep_kernel(q_ref, k_ref, qw_ref, kw_ref, *rest, rope):
    if rope:
        cos_ref, sin_ref, qo_ref, ko_ref = rest
    else:
        qo_ref, ko_ref = rest
    q = q_ref[...]
    k = k_ref[...]
    q = q * lax.rsqrt(_head_mean_square(q) + NORM_EPS) * qw_ref[...]
    k = k * lax.rsqrt(_head_mean_square(k) + NORM_EPS) * kw_ref[...]
    if rope:
        cos = cos_ref[...]
        sin = sin_ref[...]
        cq = jnp.concatenate([cos] * (q.shape[1] // LANES), axis=1)
        sq = jnp.concatenate([sin] * (q.shape[1] // LANES), axis=1)
        q = q * cq + _swap_rot_halves(q) * sq
        k = k * cos + _swap_rot_halves(k) * sin
    qo_ref[...] = q
    ko_ref[...] = k


def _rope_tables(seq):
    axis_dim = HEAD_DIM // 2
    inv_freq = ROPE_THETA ** (-jnp.arange(0, axis_dim, 2, dtype=F32) / axis_dim)
    t = jnp.arange(seq)
    pos = jnp.stack([(t // GRID_W).astype(F32), (t % GRID_W).astype(F32)], axis=1)
    lane = jnp.arange(LANES)
    dd = lane % HEAD_DIM
    ang = pos[:, dd // axis_dim] * inv_freq[dd % (axis_dim // 2)][None, :]
    sign = jnp.where((dd % axis_dim) < axis_dim // 2, -1.0, 1.0).astype(F32)
    return jnp.cos(ang), jnp.sin(ang) * sign[None, :]


def _qkprep(q, k, qw, kw, seq, rope):
    m, nq = q.shape
    nk = k.shape[1]
    tile = min(ROW_TILE, seq)
    qw_row = jnp.tile(qw, nq // HEAD_DIM).reshape(1, nq)
    kw_row = jnp.tile(kw, nk // HEAD_DIM).reshape(1, nk)
    in_specs = [pl.BlockSpec((tile, nq), lambda i: (i, 0)),
                pl.BlockSpec((tile, nk), lambda i: (i, 0)),
                pl.BlockSpec((1, nq), lambda i: (0, 0)),
                pl.BlockSpec((1, nk), lambda i: (0, 0))]
    args = [q, k, qw_row, kw_row]
    if rope:
        cos, sin = _rope_tables(seq)
        per_seq = seq // tile
        in_specs += [pl.BlockSpec((tile, LANES), lambda i: (i % per_seq, 0)),
                     pl.BlockSpec((tile, LANES), lambda i: (i % per_seq, 0))]
        args += [cos, sin]
    return pl.pallas_call(
        functools.partial(_qkprep_kernel, rope=rope),
        grid=(m // tile,),
        in_specs=in_specs,
        out_specs=[pl.BlockSpec((tile, nq), lambda i: (i, 0)),
                   pl.BlockSpec((tile, nk), lambda i: (i, 0))],
        out_shape=[jax.ShapeDtypeStruct((m, nq), F32), jax.ShapeDtypeStruct((m, nk), F32)],
        compiler_params=_cparams("arbitrary"),
        name="qk_prep",
    )(*args)


def _dup_halves(x):
    lane = lax.broadcasted_iota(I32, x.shape, 1)
    sw = pltpu.roll(x, HEAD_DIM, axis=1)
    lo = lane < HEAD_DIM
    return jnp.where(lo, x, sw), jnp.where(lo, sw, x)


def _attend(q, k_all, v_all, sink_ref, mask):
    scale = HEAD_DIM ** -0.5
    nt = (((1,), (1,)), ((), ()))
    kk = [a.astype(BF16) for a in _dup_halves(k_all)]
    vv = [a.astype(BF16) for a in _dup_halves(v_all)]
    lane = lax.broadcasted_iota(I32, (q.shape[0], LANES), 1)
    lo = lane < HEAD_DIM
    tiles = []
    for t in range(q.shape[1] // LANES):
        qt = q[:, t * LANES:(t + 1) * LANES]
        g = (2 * t) // GQA_GROUP
        halves = []
        for hh in range(2):
            head = 2 * t + hh
            qm = jnp.where(lo if hh == 0 else ~lo, qt, 0.0).astype(BF16)
            s = lax.dot_general(qm, kk[g], nt, preferred_element_type=F32) * scale
            if mask is not None:
                s = jnp.where(mask, s, NEG_BIG)
            sink = sink_ref[head]
            mx = jnp.maximum(jnp.max(s, axis=1, keepdims=True), sink)
            p = jnp.exp(s - mx)
            den = jnp.sum(p, axis=1, keepdims=True) + jnp.exp(sink - mx)
            p = (p / den).astype(BF16)
            halves.append(jnp.dot(p, vv[g], preferred_element_type=F32))
        tiles.append(jnp.where(lo, halves[0], halves[1]))
    return jnp.concatenate(tiles, axis=1)


def _ctx_attn_kernel(sink_ref, q_ref, k_ref, v_ref, o_ref):
    o_ref[...] = _attend(q_ref[...], k_ref[...], v_ref[...], sink_ref, None)


def _ctx_attention(q, k, v, sink, seq):
    m, nq = q.shape
    nk = k.shape[1]
    return pl.pallas_call(
        _ctx_attn_kernel,
        grid=(m // seq,),
        in_specs=[pl.BlockSpec(memory_space=pltpu.SMEM),
                  pl.BlockSpec((seq, nq), lambda b: (b, 0)),
                  pl.BlockSpec((seq, nk), lambda b: (b, 0)),
                  pl.BlockSpec((seq, nk), lambda b: (b, 0))],
        out_specs=pl.BlockSpec((seq, nq), lambda b: (b, 0)),
        out_shape=jax.ShapeDtypeStruct((m, nq), F32),
        compiler_params=_cparams("arbitrary"),
        name="ctx_attention",
    )(sink, q, k, v)


def _lat_attn_kernel(sink_ref, q_ref, kc_ref, vc_ref, kp_ref, k0_ref, kn_ref, vp_ref, v0_ref, vn_ref,
                     o_ref, *, seq):
    qb = pl.program_id(1)
    blk = q_ref.shape[0]
    n_ctx = kc_ref.shape[1]
    k_all = jnp.concatenate([kc_ref[0], kp_ref[...], k0_ref[...], kn_ref[...]], axis=0)
    v_all = jnp.concatenate([vc_ref[0], vp_ref[...], v0_ref[...], vn_ref[...]], axis=0)
    tk = k_all.shape[0]
    qpos = qb * blk + lax.broadcasted_iota(I32, (blk, tk), 0)
    col = lax.broadcasted_iota(I32, (blk, tk), 1)
    kpos = (qb - 1) * blk + col - n_ctx
    local_ok = (jnp.abs(qpos - kpos) <= WINDOW) & (kpos >= 0) & (kpos < seq)
    mask = (col < n_ctx) | local_ok
    o_ref[...] = _attend(q_ref[...], k_all, v_all, sink_ref, mask)


def _lat_attention(q, k, v, k_ctx, v_ctx, sink, seq):
    m, nq = q.shape
    nk = k.shape[1]
    blk = WINDOW
    nb = seq // blk
    n_ctx = k_ctx.shape[1]
    last = m // blk - 1

    def kv_spec(shift):
        return pl.BlockSpec((blk, nk), lambda b, i: (jnp.clip(b * nb + i + shift, 0, last), 0))

    ctx_spec = pl.BlockSpec((1, n_ctx, nk), lambda b, i: (b, 0, 0))
    return pl.pallas_call(
        functools.partial(_lat_attn_kernel, seq=seq),
        grid=(m // seq, nb),
        in_specs=[pl.BlockSpec(memory_space=pltpu.SMEM),
                  pl.BlockSpec((blk, nq), lambda b, i: (b * nb + i, 0)),
                  ctx_spec, ctx_spec,
                  kv_spec(-1), kv_spec(0), kv_spec(1),
                  kv_spec(-1), kv_spec(0), kv_spec(1)],
        out_specs=pl.BlockSpec((blk, nq), lambda b, i: (b * nb + i, 0)),
        out_shape=jax.ShapeDtypeStruct((m, nq), F32),
        compiler_params=_cparams("arbitrary", "arbitrary"),
        name="lat_attention",
    )(sink, q, k_ctx, v_ctx, k, k, k, v, v, v)

SSD_BLOCK = 256
SSD_PAIRS = SSD_HEADS // 2
SSD_INNER = SSD_HEADS * SSD_HEAD_DIM
HALO = SUBLANES


def _softplus(x):
    return jnp.maximum(x, 0.0) + jnp.log1p(jnp.exp(-jnp.abs(x)))


def _silu(x):
    return x * jax.nn.sigmoid(x)


def _ssd_decays(dt_raw, bias, a_log):
    n = dt_raw.shape[0]
    dt = _softplus(dt_raw + bias)
    log_a = dt * (-jnp.exp(a_log))
    r = lax.broadcasted_iota(I32, (n, n), 0)
    c = lax.broadcasted_iota(I32, (n, n), 1)
    lower = jnp.where(c <= r, 1.0, 0.0).astype(F32)
    upper = jnp.where(r <= c, 1.0, 0.0).astype(F32)
    cum_col = jnp.dot(lower, log_a, precision=HIGHEST, preferred_element_type=F32)
    dt_row = dt.T
    la_row = log_a.T
    cum_row = jnp.dot(la_row, upper, precision=HIGHEST, preferred_element_type=F32)
    return dt, log_a, cum_col, dt_row, la_row, cum_row


def _ssd_scan_chunk(xs, bmat, cmat, w_of, q_scale_of, k_scale_of, carry_of, s_ref):
    nt = (((1,), (1,)), ((), ()))
    n = xs.shape[0]
    lane = lax.broadcasted_iota(I32, (n, LANES), 1)
    lo = lane < SSD_HEAD_DIM
    lane_s = lax.broadcasted_iota(I32, (D_STATE, LANES), 1)
    lo_s = lane_s < SSD_HEAD_DIM
    b_t = bmat.T
    cb16 = cmat.astype(BF16)
    ys = []
    for pair in range(SSD_PAIRS):
        g = (2 * pair) // (SSD_HEADS // SSD_GROUPS)
        in_g = (lane // D_STATE) == g
        cg = jnp.where(in_g, cmat, 0.0)
        cb = lax.dot_general(cg.astype(BF16), bmat.astype(BF16), nt, preferred_element_type=F32)
        x_pair = xs[:, pair * LANES:(pair + 1) * LANES]
        x16 = x_pair.astype(BF16)
        s_old = s_ref[pair]
        s2 = jnp.concatenate([s_old, s_old], axis=0).astype(BF16)
        bg_t = b_t[g * D_STATE:(g + 1) * D_STATE, :]
        y_h, s_h = [], []
        for hh in range(2):
            h = 2 * pair + hh
            w = (cb * w_of(h)).astype(BF16)
            y = jnp.dot(w, x16, preferred_element_type=F32)
            cq = (cg * q_scale_of(h)).astype(BF16)
            y = y + jnp.dot(cq, s2, preferred_element_type=F32)
            y_h.append(y)
            kt = (bg_t * k_scale_of(h)).astype(BF16)
            s_h.append(carry_of(h) * s_old + jnp.dot(kt, x16, preferred_element_type=F32))
        ys.append(jnp.where(lo, y_h[0], y_h[1]))
        s_ref[pair] = jnp.where(lo_s, s_h[0], s_h[1])
    return jnp.concatenate(ys, axis=1)


def _ssd_fwd_kernel(x_ref, xp_ref, xn_ref, dt_ref, s0_ref, cw_ref, cb_ref, bias_ref, alog_ref,
                    y_ref, xc_ref, sfin_ref, s_ref):
    c = pl.program_id(1)
    nc = pl.num_programs(1)
    n = x_ref.shape[0]

    @pl.when(c == 0)
    def _():
        s_ref[...] = s0_ref[0]

    prev = jnp.where(c > 0, xp_ref[...], 0.0)
    nxt = jnp.where(c < nc - 1, xn_ref[...], 0.0)
    xe = jnp.concatenate([prev, x_ref[...], nxt], axis=0)
    pad = (CONV_K - 1) // 2
    acc = cb_ref[...] + cw_ref[0:1, :] * xe[HALO - pad:HALO - pad + n, :]
    for k in range(1, CONV_K):
        acc = acc + cw_ref[k:k + 1, :] * xe[HALO - pad + k:HALO - pad + k + n, :]
    xc = _silu(acc)
    xc_ref[...] = xc
    xs = xc[:, :SSD_INNER]
    bmat = xc[:, SSD_INNER:SSD_INNER + LANES]
    cmat = xc[:, SSD_INNER + LANES:SSD_INNER + 2 * LANES]

    dt, log_a, cum_col, dt_row, la_row, cum_row = _ssd_decays(dt_ref[...], bias_ref[...], alog_ref[...])
    r = lax.broadcasted_iota(I32, (n, n), 0)
    cc = lax.broadcasted_iota(I32, (n, n), 1)
    causal = cc <= r
    last_col = cum_col[n - 1:n, :]

    def w_of(h):
        seg = cum_col[:, h:h + 1] - cum_row[h:h + 1, :]
        return jnp.exp(jnp.where(causal, seg, NEG_BIG)) * dt_row[h:h + 1, :]

    def q_scale_of(h):
        return jnp.exp(cum_col[:, h:h + 1])

    def k_scale_of(h):
        return dt_row[h:h + 1, :] * jnp.exp(cum_row[h:h + 1, n - 1:n] - cum_row[h:h + 1, :])

    def carry_of(h):
        return jnp.exp(last_col[:, h:h + 1])

    y_ref[...] = _ssd_scan_chunk(xs, bmat, cmat, w_of, q_scale_of, k_scale_of, carry_of, s_ref)

    @pl.when(c == nc - 1)
    def _():
        sfin_ref[0] = s_ref[...]


def _ssd_bwd_kernel(xc_ref, dt_ref, yf_ref, z_ref, s0_ref, bias_ref, alog_ref, dskip_ref, nw_ref,
                    y_ref, sfin_ref, s_ref):
    c = pl.program_id(1)
    nc = pl.num_programs(1)
    n = xc_ref.shape[0]

    @pl.when(c == 0)
    def _():
        s_ref[...] = s0_ref[0]

    xc = xc_ref[...]
    xs = xc[:, :SSD_INNER]
    bmat = xc[:, SSD_INNER:SSD_INNER + LANES]
    cmat = xc[:, SSD_INNER + LANES:SSD_INNER + 2 * LANES]
    dt, log_a, cum_col, dt_row, la_row, cum_row = _ssd_decays(dt_ref[...], bias_ref[...], alog_ref[...])
    ex_col = cum_col - log_a
    ex_row = cum_row - la_row
    r = lax.broadcasted_iota(I32, (n, n), 0)
    cc = lax.broadcasted_iota(I32, (n, n), 1)
    anti = cc >= r
    tot_col = cum_col[n - 1:n, :]
    off = SSD_HEADS

    def w_of(h):
        j = off + h
        seg = ex_row[j:j + 1, :] - ex_col[:, j:j + 1]
        return jnp.exp(jnp.where(anti, seg, NEG_BIG)) * dt_row[j:j + 1, :]

    def q_scale_of(h):
        j = off + h
        return jnp.exp(tot_col[:, j:j + 1] - ex_col[:, j:j + 1])

    def k_scale_of(h):
        j = off + h
        return dt_row[j:j + 1, :] * jnp.exp(ex_row[j:j + 1, :])

    def carry_of(h):
        j = off + h
        return jnp.exp(tot_col[:, j:j + 1])

    y_b = _ssd_scan_chunk(xs, bmat, cmat, w_of, q_scale_of, k_scale_of, carry_of, s_ref)
    y = yf_ref[...] + y_b + dskip_ref[...] * xs
    y = y * _silu(z_ref[...])
    ms = jnp.mean(y * y, axis=-1, keepdims=True)
    y_ref[...] = y * lax.rsqrt(ms + NORM_EPS) * nw_ref[...]

    @pl.when(c == nc - 1)
    def _():
        sfin_ref[0] = s_ref[...]


def _pair_states(s):
    b, h, n, p = s.shape
    return s.reshape(b, h // 2, 2, n, p).transpose(0, 1, 3, 2, 4).reshape(b, h // 2, n, 2 * p)


def _unpair_states(s):
    b, hp, n, p2 = s.shape
    return s.reshape(b, hp, n, 2, p2 // 2).transpose(0, 1, 3, 2, 4).reshape(b, hp * 2, n, p2 // 2)


def _ssd(xbc, dt, z, s0_f, s0_b, conv_w, conv_b, dt_bias, a_log, d_skip, ssd_norm, seq):
    m, nx = xbc.shape
    nb = m // seq
    blk = min(SSD_BLOCK, seq)
    nc = seq // blk
    hb = blk // HALO
    n_halo = m // HALO
    pad16 = lambda a: jnp.pad(a.reshape(1, -1), ((0, 0), (0, LANES - a.size)))
    bias = pad16(dt_bias)
    alog = pad16(a_log)
    state_spec = pl.BlockSpec((1, SSD_PAIRS, D_STATE, LANES), lambda b, c: (b, 0, 0, 0))
    state_shape = jax.ShapeDtypeStruct((nb, SSD_PAIRS, D_STATE, LANES), F32)
    row = lambda width: pl.BlockSpec((1, width), lambda b, c: (0, 0))

    def fwd_rows(width):
        return pl.BlockSpec((blk, width), lambda b, c: (b * nc + c, 0))

    def bwd_rows(width):
        return pl.BlockSpec((blk, width), lambda b, c: (b * nc + nc - 1 - c, 0))

    y_f, xc, s_f = pl.pallas_call(
        _ssd_fwd_kernel,
        grid=(nb, nc),
        in_specs=[fwd_rows(nx),
                  pl.BlockSpec((HALO, nx), lambda b, c: (jnp.maximum((b * nc + c) * hb - 1, 0), 0)),
                  pl.BlockSpec((HALO, nx), lambda b, c: (jnp.minimum((b * nc + c + 1) * hb, n_halo - 1), 0)),
                  fwd_rows(LANES), state_spec,
                  pl.BlockSpec((CONV_K, nx), lambda b, c: (0, 0)), row(nx), row(LANES), row(LANES)],
        out_specs=[fwd_rows(SSD_INNER), fwd_rows(nx), state_spec],
        out_shape=[jax.ShapeDtypeStruct((m, SSD_INNER), F32), jax.ShapeDtypeStruct((m, nx), F32), state_shape],
        scratch_shapes=[pltpu.VMEM((SSD_PAIRS, D_STATE, LANES), F32)],
        compiler_params=_cparams("arbitrary", "arbitrary"),
        name="ssd_forward",
    )(xbc, xbc, xbc, dt, _pair_states(s0_f), conv_w, conv_b.reshape(1, nx), bias, alog)

    dskip = jnp.repeat(d_skip, SSD_HEAD_DIM).reshape(1, SSD_INNER)
    y, s_b = pl.pallas_call(
        _ssd_bwd_kernel,
        grid=(nb, nc),
        in_specs=[bwd_rows(nx), bwd_rows(LANES), bwd_rows(SSD_INNER), bwd_rows(SSD_INNER), state_spec,
                  row(LANES), row(LANES), row(SSD_INNER), row(SSD_INNER)],
        out_specs=[bwd_rows(SSD_INNER), state_spec],
        out_shape=[jax.ShapeDtypeStruct((m, SSD_INNER), F32), state_shape],
        scratch_shapes=[pltpu.VMEM((SSD_PAIRS, D_STATE, LANES), F32)],
        compiler_params=_cparams("arbitrary", "arbitrary"),
        name="ssd_backward",
    )(xc, dt, y_f, z, _pair_states(s0_b), bias, alog, dskip, ssd_norm.reshape(1, SSD_INNER))
    return y, _unpair_states(s_f), _unpair_states(s_b)

def _hgrn_kernel(q_ref, ff_ref, fb_ref, i_ref, g_ref, lb_ref, s0_ref, nw_ref, o_ref, sfin_ref,
                 sf_ref, sb_ref, ob_ref, *, layer):
    t_len = q_ref.shape[0]
    n = HGRN_CHUNK
    n_chunks = t_len // n
    tn = (((0,), (0,)), ((), ()))
    nt = (((1,), (1,)), ((), ()))

    lbp = lb_ref[...]
    e = jnp.exp(lbp - jnp.max(lbp, axis=0, keepdims=True))
    sm = e / jnp.sum(e, axis=0, keepdims=True)
    lb = sm[0] * 0.0
    for j in range(1, layer + 1):
        lb = lb + sm[j]

    r = lax.broadcasted_iota(I32, (n, n), 0)
    c = lax.broadcasted_iota(I32, (n, n), 1)
    lower = jnp.where(c <= r, 1.0, 0.0).astype(F32)
    srow = lax.broadcasted_iota(I32, (n, HGRN_DK), 0)
    qscale = HGRN_DK ** -0.5

    def chunk(row0, f_ref, lb_d, reverse, s_ref):
        q = _silu(q_ref[pl.ds(row0, n), :]) * qscale
        f = f_ref[pl.ds(row0, n), :]
        v = i_ref[pl.ds(row0, n), :]
        k = (1.0 - lb_d) * jax.nn.sigmoid(-f)
        lf = jnp.log(lb_d + (1.0 - lb_d) * jax.nn.sigmoid(f))
        cum = jnp.dot(lower, lf, precision=HIGHEST, preferred_element_type=F32)
        tot = cum[n - 1:n, :]
        if reverse:
            cum = cum - lf
        rows = []
        for t in range(n):
            tile0 = (t // SUBLANES) * SUBLANES
            lo, hi = (tile0, n) if reverse else (0, tile0 + SUBLANES)
            cum_s = cum[lo:hi]
            if reverse:
                seg = jnp.where(srow[lo:hi] >= t, cum_s - cum[t:t + 1, :], NEG_BIG)
            else:
                seg = jnp.where(srow[lo:hi] <= t, cum[t:t + 1, :] - cum_s, NEG_BIG)
            a = q[t:t + 1, :] * k[lo:hi] * jnp.exp(seg)
            sc = jnp.sum(a, axis=1, keepdims=True)
            rows.append(jnp.sum(sc * v[lo:hi], axis=0, keepdims=True))
        o = jnp.concatenate(rows, axis=0)
        s_old = s_ref[...]
        if reverse:
            q_in = q * jnp.exp(tot - cum)
            k_out = k * jnp.exp(cum)
        else:
            q_in = q * jnp.exp(cum)
            k_out = k * jnp.exp(tot - cum)
        o = o + lax.dot_general(q_in.astype(BF16), s_old.astype(BF16), nt, preferred_element_type=F32)
        s_ref[...] = jnp.exp(tot) * s_old + lax.dot_general(
            v.astype(BF16), k_out.astype(BF16), tn, preferred_element_type=F32)
        return o

    sf_ref[...] = s0_ref[0, 0, 0].T
    sb_ref[...] = s0_ref[0, 1, 0].T

    def body(ci, carry):
        row_f = pl.multiple_of(ci * n, n)
        row_b = pl.multiple_of((n_chunks - 1 - ci) * n, n)
        o_ref[pl.ds(row_f, n), :] = chunk(row_f, ff_ref, lb[0:1, :], False, sf_ref)
        ob_ref[pl.ds(row_b, n), :] = chunk(row_b, fb_ref, lb[1:2, :], True, sb_ref)
        return carry

    lax.fori_loop(0, n_chunks, body, 0)
    sfin_ref[0, 0, 0] = sf_ref[...].T
    sfin_ref[0, 1, 0] = sb_ref[...].T

    nw = nw_ref[...]
    blk = min(t_len, ROW_TILE)

    def finish(bi, carry):
        row0 = pl.multiple_of(bi * blk, blk)
        o = o_ref[pl.ds(row0, blk), :] + ob_ref[pl.ds(row0, blk), :]
        ms = jnp.mean(o * o, axis=-1, keepdims=True)
        o = o * lax.rsqrt(ms + NORM_EPS) * nw
        o_ref[pl.ds(row0, blk), :] = o * _silu(g_ref[pl.ds(row0, blk), :])
        return carry

    lax.fori_loop(0, t_len // blk, finish, 0)


def _hgrn(q, f_fw, f_bw, iv, g, o_lb, state0, g_norm, seq, layer):
    m, width = q.shape
    nb = m // seq
    dv = width // HGRN_HEADS
    col = pl.BlockSpec((seq, dv), lambda b, h: (b, h))
    state_spec = pl.BlockSpec((1, 2, 1, HGRN_DK, dv), lambda b, h: (b, 0, h, 0, 0))
    return pl.pallas_call(
        functools.partial(_hgrn_kernel, layer=layer),
        grid=(nb, HGRN_HEADS),
        in_specs=[col, col, col, col, col,
                  pl.BlockSpec((o_lb.shape[0], 2, HGRN_DK), lambda b, h: (0, 0, h)),
                  state_spec,
                  pl.BlockSpec((1, dv), lambda b, h: (0, 0))],
        out_specs=[col, state_spec],
        out_shape=[jax.ShapeDtypeStruct((m, width), F32),
                   jax.ShapeDtypeStruct((nb, 2, HGRN_HEADS, HGRN_DK, dv), F32)],
        scratch_shapes=[pltpu.VMEM((dv, HGRN_DK), F32), pltpu.VMEM((dv, HGRN_DK), F32),
                        pltpu.VMEM((seq, dv), F32)],
        compiler_params=_cparams("arbitrary", "arbitrary"),
        name="hgrn2",
    )(q, f_fw, f_bw, iv, g, o_lb, state0, g_norm.reshape(1, dv))

EVEN_SPLITS = ((0, 512), (512, 640), (640, 768), (768, 1280), (1280, 2048), (2048, 2176))
HGRN_SPLITS = tuple((i * 1024, (i + 1) * 1024) for i in range(5))


def _even_weight(w):
    main = EVEN_SPLITS[-1][0]
    return jnp.pad(w, ((0, 0), (0, LANES - (w.shape[1] - main)))).astype(BF16)


def _run_trunk(x3, mods, mod_row0, P, cache):
    nb, seq, d = x3.shape
    x = x3.reshape(nb * seq, d)
    depth = P['norm_mix'].shape[0]
    ks, vs, ssd_states, hgrn_states = [], [], [], []
    for l in range(depth):
        j = l // 2
        row0 = (l * SUBLANES + mod_row0, 0 if cache is None else 1)
        if l % 2 == 0:
            q, k, v, z, xbc, dt = _inproj(x, mods, row0, seq, P['norm_mix'][l], P['e_w_in'][j],
                                          EVEN_SPLITS, "even_in_proj")
            q, k = _qkprep(q, k, P['e_q_norm'][j], P['e_k_norm'][j], seq, rope=cache is not None)
            if cache is None:
                s0_f = jnp.zeros((nb, SSD_HEADS, D_STATE, SSD_HEAD_DIM), F32)
                s0_b = s0_f
                o_attn = _ctx_attention(q, k, v, P['e_sink'][j], seq)
            else:
                s0_f, s0_b = cache[2][:, j, 0], cache[2][:, j, 1]
                n_ctx = cache[0].shape[2]
                o_attn = _lat_attention(q, k, v, cache[0][:, j].reshape(nb, n_ctx, -1),
                                        cache[1][:, j].reshape(nb, n_ctx, -1), P['e_sink'][j], seq)
            y, s_f, s_b = _ssd(xbc, dt, z, s0_f, s0_b, P['e_conv_w'][j], P['e_conv_b'][j],
                               P['e_dt_bias'][j], P['e_a_log'][j], P['e_d_skip'][j], P['e_ssd_norm'][j], seq)
            if cache is None:
                ks.append(k.reshape(nb, seq, N_KV_HEADS, HEAD_DIM))
                vs.append(v.reshape(nb, seq, N_KV_HEADS, HEAD_DIM))
                ssd_states.append(jnp.stack([s_f, s_b], axis=1))
            mix = jnp.concatenate([o_attn, y], axis=1)
            x = _outproj(mix, x, mods, row0, seq, P['e_w_out'][j], "even_out_proj")
        else:
            q, f_fw, f_bw, iv, g = _inproj(x, mods, row0, seq, P['norm_mix'][l], P['o_w_in'][j],
                                           HGRN_SPLITS, "odd_in_proj")
            if cache is None:
                s0 = jnp.zeros((nb, 2, HGRN_HEADS, HGRN_DK, d // HGRN_HEADS), F32)
            else:
                s0 = cache[3][:, j]
            o, s_new = _hgrn(q, f_fw, f_bw, iv, g, P['o_lb'], s0, P['o_g_norm'][j], seq, j)
            if cache is None:
                hgrn_states.append(s_new)
            x = _outproj(o, x, mods, row0, seq, P['o_w_out'][j], "odd_out_proj")
        x = _peer(x, mods, row0, seq, l, P['norm_ffn'][l], P['p_w_q'][l], P['p_sub_keys'][l],
                  P['p_u'], P['p_v'],
                  SC_SHARE_CONTEXT if cache is None else SC_SHARE_LATENT)
    y = x.reshape(nb, seq, d)
    if cache is not None:
        return y, None
    return y, (jnp.stack(ks, axis=1), jnp.stack(vs, axis=1),
               jnp.stack(ssd_states, axis=1), jnp.stack(hgrn_states, axis=1))


def kernel(x_prompt, x_sample, cache_k, cache_v, state_ssd, state_hgrn, c, c_ctx, w_ada, b_ada, norm_mix, norm_ffn, e_w_in, e_q_norm, e_k_norm, e_sink, e_conv_w, e_conv_b, e_dt_bias, e_a_log, e_d_skip, e_ssd_norm, e_w_out, o_w_in, o_lb, o_g_norm, o_w_out, p_w_q, p_sub_keys, p_u, p_v):
    depth, d, d6 = w_ada.shape
    b_lat = x_sample.shape[0]
    cond_rows = jnp.concatenate([c_ctx[None, :], c, jnp.zeros((SUBLANES - 1 - b_lat, d), F32)], axis=0)
    mods = _modulation(cond_rows, w_ada, b_ada).reshape(depth * SUBLANES, 1, d6)
    P = {
        'norm_mix': norm_mix, 'norm_ffn': norm_ffn,
        'e_w_in': jnp.stack([_even_weight(w) for w in e_w_in]), 'e_q_norm': e_q_norm, 'e_k_norm': e_k_norm,
        'e_sink': e_sink, 'e_conv_w': e_conv_w, 'e_conv_b': e_conv_b, 'e_dt_bias': e_dt_bias,
        'e_a_log': e_a_log, 'e_d_skip': e_d_skip, 'e_ssd_norm': e_ssd_norm,
        'e_w_out': e_w_out.astype(BF16),
        'o_w_in': o_w_in.astype(BF16), 'o_lb': o_lb, 'o_g_norm': o_g_norm, 'o_w_out': o_w_out.astype(BF16),
        'p_w_q': p_w_q.astype(BF16),
        'p_sub_keys': p_sub_keys.astype(BF16).reshape(depth, PEER_HEADS * 2, PEER_NKEYS, PEER_DKEY),
        'p_u': p_u, 'p_v': p_v,
    }
    y_prompt, new_state = _run_trunk(x_prompt, mods, 0, P, None)
    y_sample, _ = _run_trunk(x_sample, mods, 1, P, (cache_k, cache_v, state_ssd, state_hgrn))
    return (y_prompt, y_sample) + new_state
```

```python
import functools
import math

import jax
import jax.numpy as jnp
from jax import lax
from jax.experimental import pallas as pl
from jax.experimental.pallas import tpu as pltpu
from jax.experimental.pallas import tpu_sc as plsc

F32 = jnp.float32
BF16 = jnp.bfloat16
I32 = jnp.int32
HIGHEST = lax.Precision.HIGHEST

NORM_EPS = 1e-6
NEG_BIG = -1e30
LANES = 128
SUBLANES = 8
VMEM_LIMIT = 48 * 1024 * 1024

GRID_W = 64
HEAD_DIM = 64
N_Q_HEADS = 8
N_KV_HEADS = 2
GQA_GROUP = 4
WINDOW = 128
ROPE_THETA = 10000.0
SSD_HEADS = 8
SSD_HEAD_DIM = 64
SSD_GROUPS = 2
D_STATE = 64
CONV_K = 5
HGRN_HEADS = 8
HGRN_DK = 128
HGRN_CHUNK = 32
PEER_HEADS = 8
PEER_NKEYS = 128
PEER_TOPK = 16
PEER_DKEY = 128
PEER_PAIRS = PEER_HEADS * PEER_TOPK

ROW_TILE = 256
PEER_BLOCK = 128
SC_LANES = 16
SC_CHUNK = 16
SC_SHARE_CONTEXT = (7, 8)
SC_SHARE_LATENT = (3, 4)
GELU_C = math.sqrt(2.0 / math.pi)


def _cparams(*sem):
    return pltpu.CompilerParams(dimension_semantics=sem, vmem_limit_bytes=VMEM_LIMIT)


def _norm_mod(x, nw, scale, shift):
    ms = jnp.mean(x * x, axis=-1, keepdims=True)
    return (x * lax.rsqrt(ms + NORM_EPS)) * nw * (1.0 + scale) + shift


def _mod_kernel(c_ref, w_ref, b_ref, o_ref):
    c = c_ref[...]
    s = c * jax.nn.sigmoid(c)
    o_ref[0] = jnp.dot(s, w_ref[0], precision=HIGHEST, preferred_element_type=F32) + b_ref[0]


def _modulation(cond_rows, w_ada, b_ada):
    depth, d, n = w_ada.shape
    rows = cond_rows.shape[0]
    return pl.pallas_call(
        _mod_kernel,
        grid=(depth, n // d),
        in_specs=[pl.BlockSpec((rows, d), lambda l, j: (0, 0)),
                  pl.BlockSpec((1, d, d), lambda l, j: (l, 0, j)),
                  pl.BlockSpec((1, 1, d), lambda l, j: (l, 0, j))],
        out_specs=pl.BlockSpec((1, rows, d), lambda l, j: (l, 0, j)),
        out_shape=jax.ShapeDtypeStruct((depth, rows, n), F32),
        compiler_params=_cparams("arbitrary", "arbitrary"),
        name="modulation",
    )(cond_rows, w_ada, b_ada.reshape(depth, 1, n))


def _mod_spec(mod_row0, seq, tile, d6, first_tile=0):
    row0, per_batch = mod_row0
    return pl.BlockSpec((1, 1, d6),
                        lambda i: (row0 + per_batch * (((i + first_tile) * tile) // seq), 0, 0))


def _inproj_kernel(x_ref, m_ref, nw_ref, w_ref, *o_refs, splits, d):
    m = m_ref[0]
    h = _norm_mod(x_ref[...], nw_ref[...], m[:, d:2 * d], m[:, 0:d]).astype(BF16)
    for o_ref, (a, b) in zip(o_refs, splits):
        o_ref[...] = jnp.dot(h, w_ref[:, a:b], preferred_element_type=F32)


def _inproj(x, mods, mod_row0, seq, nw, w_bf16, splits, name):
    m, d = x.shape
    n = w_bf16.shape[1]
    tile = min(ROW_TILE, seq)
    return pl.pallas_call(
        functools.partial(_inproj_kernel, splits=splits, d=d),
        grid=(m // tile,),
        in_specs=[pl.BlockSpec((tile, d), lambda i: (i, 0)),
                  _mod_spec(mod_row0, seq, tile, mods.shape[-1]),
                  pl.BlockSpec((1, d), lambda i: (0, 0)),
                  pl.BlockSpec((d, n), lambda i: (0, 0))],
        out_specs=[pl.BlockSpec((tile, b - a), lambda i: (i, 0)) for a, b in splits],
        out_shape=[jax.ShapeDtypeStruct((m, b - a), F32) for a, b in splits],
        compiler_params=_cparams("arbitrary"),
        name=name,
    )(x, mods, nw.reshape(1, d), w_bf16)


def _outproj_kernel(mix_ref, x_ref, m_ref, w_ref, o_ref, *, d):
    y = jnp.dot(mix_ref[...].astype(BF16), w_ref[...], preferred_element_type=F32)
    o_ref[...] = x_ref[...] + m_ref[0][:, 2 * d:3 * d] * y


def _outproj(mix, x, mods, mod_row0, seq, w_bf16, name):
    m, d = x.shape
    k = mix.shape[1]
    tile = min(ROW_TILE, seq)
    return pl.pallas_call(
        functools.partial(_outproj_kernel, d=d),
        grid=(m // tile,),
        in_specs=[pl.BlockSpec((tile, k), lambda i: (i, 0)),
                  pl.BlockSpec((tile, d), lambda i: (i, 0)),
                  _mod_spec(mod_row0, seq, tile, mods.shape[-1]),
                  pl.BlockSpec((k, d), lambda i: (0, 0))],
        out_specs=pl.BlockSpec((tile, d), lambda i: (i, 0)),
        out_shape=jax.ShapeDtypeStruct((m, d), F32),
        compiler_params=_cparams("arbitrary"),
        name=name,
    )(mix, x, mods, w_bf16)


def _topk_over_rows(s, k, payload=None):
    n = s.shape[0]
    iota = lax.broadcasted_iota(I32, s.shape, 0)
    vals, idxs, pays = [], [], []
    for _ in range(k):
        m = jnp.max(s, axis=0, keepdims=True)
        i = jnp.min(jnp.where(s == m, iota, n), axis=0, keepdims=True)
        hit = iota == i
        vals.append(m)
        idxs.append(i)
        if payload is not None:
            pays.append(jnp.max(jnp.where(hit, payload, -1), axis=0, keepdims=True))
        s = jnp.where(hit, -jnp.inf, s)
    out = (jnp.concatenate(vals, axis=0), jnp.concatenate(idxs, axis=0))
    if payload is not None:
        out += (jnp.concatenate(pays, axis=0),)
    return out


def _peer_route_kernel(x_ref, m_ref, nw_ref, wq_ref, keys_ref, h_ref, e_ref, g_ref, *, d):
    m = m_ref[0]
    h = _norm_mod(x_ref[...], nw_ref[...], m[:, 4 * d:5 * d], m[:, 3 * d:4 * d])
    h_ref[...] = h
    hb = h.astype(BF16)
    nt = (((1,), (1,)), ((), ()))
    for head in range(PEER_HEADS):
        tops = []
        for half in range(2):
            c0 = (head * 2 + half) * PEER_DKEY
            q = jnp.dot(hb, wq_ref[:, c0:c0 + PEER_DKEY], preferred_element_type=F32)
            s = lax.dot_general(keys_ref[head * 2 + half], q.astype(BF16), nt,
                                preferred_element_type=F32)
            tops.append(_topk_over_rows(s, PEER_TOPK))
        (s0, i0), (s1, i1) = tops
        widths = [PEER_TOPK // (a + 1) for a in range(PEER_TOPK)]
        n_pad = -sum(widths) % SUBLANES
        cand_s = jnp.concatenate([s0[a:a + 1] + s1[:w] for a, w in enumerate(widths)]
                                 + [jnp.full((n_pad, s0.shape[1]), -jnp.inf, F32)], axis=0)
        cand_e = jnp.concatenate([i0[a:a + 1] * PEER_NKEYS + i1[:w] for a, w in enumerate(widths)]
                                 + [jnp.zeros((n_pad, s0.shape[1]), I32)], axis=0)
        best_s, _, best_e = _topk_over_rows(cand_s, PEER_TOPK, payload=cand_e)
        p = jnp.exp(best_s - best_s[0:1])
        r0 = head * PEER_TOPK
        e_ref[r0:r0 + PEER_TOPK, :] = best_e
        g_ref[r0:r0 + PEER_TOPK, :] = p / jnp.sum(p, axis=0, keepdims=True)


def _peer_route(x, mods, mod_row0, seq, nw, wq_bf16, keys_bf16):
    m, d = x.shape
    tile = min(ROW_TILE, seq)
    nq = wq_bf16.shape[1]
    return pl.pallas_call(
        functools.partial(_peer_route_kernel, d=d),
        grid=(m // tile,),
        in_specs=[pl.BlockSpec((tile, d), lambda i: (i, 0)),
                  _mod_spec(mod_row0, seq, tile, mods.shape[-1]),
                  pl.BlockSpec((1, d), lambda i: (0, 0)),
                  pl.BlockSpec((d, nq), lambda i: (0, 0)),
                  pl.BlockSpec(keys_bf16.shape, lambda i: (0, 0, 0))],
        out_specs=[pl.BlockSpec((tile, d), lambda i: (i, 0)),
                   pl.BlockSpec((PEER_PAIRS, tile), lambda i: (0, i)),
                   pl.BlockSpec((PEER_PAIRS, tile), lambda i: (0, i))],
        out_shape=[jax.ShapeDtypeStruct((m, d), F32),
                   jax.ShapeDtypeStruct((PEER_PAIRS, m), I32),
                   jax.ShapeDtypeStruct((PEER_PAIRS, m), F32)],
        compiler_params=_cparams("arbitrary"),
        name="peer_route",
    )(x, mods, nw.reshape(1, d), wq_bf16, keys_bf16)


def _peer_gather_kernel(idx_hbm, h_ref, g_ref, x_ref, m_ref, uv_hbm, o_ref,
                        idx_smem, buf, sem_idx, sem, *, d):
    blk = pl.program_id(0)
    n_groups = PEER_BLOCK // SUBLANES

    cp = pltpu.make_async_copy(idx_hbm.at[blk], idx_smem, sem_idx)
    cp.start()
    cp.wait()

    def issue(tok, slot):
        first = tok * PEER_PAIRS

        def body(grp, carry):
            base = pl.multiple_of(grp * SUBLANES, SUBLANES)
            for r in range(SUBLANES):
                e = idx_smem[first + base + r]
                pltpu.make_async_copy(uv_hbm.at[pl.ds(e, 1)], buf.at[slot, pl.ds(base + r, 1)],
                                      sem.at[slot]).start()
            return carry
        lax.fori_loop(0, PEER_PAIRS // SUBLANES, body, 0)

    def wait(slot):
        pltpu.make_async_copy(uv_hbm.at[pl.ds(0, PEER_PAIRS)], buf.at[slot], sem.at[slot]).wait()

    gate2 = m_ref[0][:, 5 * d:6 * d]
    lane = lax.broadcasted_iota(I32, (PEER_PAIRS, PEER_BLOCK), 1)

    issue(0, 0)

    def group(grp, carry):
        base = pl.multiple_of(grp * SUBLANES, SUBLANES)
        h8 = h_ref[pl.ds(base, SUBLANES), :]
        rows = []
        for r in range(SUBLANES):
            tok = base + r
            slot = r % 2
            if r < SUBLANES - 1:
                issue(tok + 1, 1 - slot)
            else:
                @pl.when(grp < n_groups - 1)
                def _():
                    issue(tok + 1, 1 - slot)
            wait(slot)
            act = jnp.sum(buf[slot, :, :d] * h8[r:r + 1, :], axis=1, keepdims=True)
            gate = jnp.sum(jnp.where(lane == tok, g_ref[...], 0.0), axis=1, keepdims=True)
            w = jax.nn.gelu(act) * gate
            rows.append(jnp.sum(buf[slot, :, d:] * w, axis=0, keepdims=True))
        out8 = jnp.concatenate(rows, axis=0)
        o_ref[pl.ds(base, SUBLANES), :] = x_ref[pl.ds(base, SUBLANES), :] + gate2 * out8
        return carry

    lax.fori_loop(0, n_groups, group, 0)


def _peer_gather(idx, h, gates, x, mods, mod_row0, seq, p_uv, first_block):
    m, d = x.shape
    nblk = m // PEER_BLOCK - first_block
    idx3 = idx[:, first_block * PEER_BLOCK:].T.reshape(nblk, PEER_BLOCK * PEER_PAIRS)
    rows = pl.BlockSpec((PEER_BLOCK, d), lambda i: (i + first_block, 0))
    return pl.pallas_call(
        functools.partial(_peer_gather_kernel, d=d),
        grid=(nblk,),
        in_specs=[pl.BlockSpec(memory_space=pl.ANY),
                  rows,
                  pl.BlockSpec((PEER_PAIRS, PEER_BLOCK), lambda i: (0, i + first_block)),
                  rows,
                  _mod_spec(mod_row0, seq, PEER_BLOCK, mods.shape[-1], first_block),
                  pl.BlockSpec(memory_space=pl.ANY)],
        out_specs=pl.BlockSpec((PEER_BLOCK, d), lambda i: (i, 0)),
        out_shape=jax.ShapeDtypeStruct((nblk * PEER_BLOCK, d), F32),
        scratch_shapes=[pltpu.SMEM((PEER_BLOCK * PEER_PAIRS,), I32),
                        pltpu.VMEM((2, PEER_PAIRS, 2 * d), F32),
                        pltpu.SemaphoreType.DMA,
                        pltpu.SemaphoreType.DMA((2,))],
        compiler_params=_cparams("arbitrary"),
        name="peer_gather",
    )(idx3, h, gates, x, mods, p_uv)


def _peer_experts_sc(idx, h, gates, u_rows, v_rows):
    m = idx.shape[0]
    d = h.shape[1]
    info = plsc.get_sparse_core_info()
    n_workers = info.num_cores * info.num_subcores
    per = m // n_workers
    n_chunks = PEER_PAIRS // SC_CHUNK
    n_vec = d // SC_LANES
    mesh = plsc.VectorSubcoreMesh(core_axis_name="c", subcore_axis_name="s")

    @functools.partial(
        pl.kernel, out_type=jax.ShapeDtypeStruct((m, d), F32), mesh=mesh,
        scratch_types=[pltpu.VMEM((2, PEER_PAIRS), I32), pltpu.VMEM((2, d), F32),
                       pltpu.VMEM((2, PEER_PAIRS), F32), pltpu.VMEM((2, d), F32),
                       pltpu.VMEM((2, SC_CHUNK, d), F32), pltpu.VMEM((2, SC_CHUNK, d), F32),
                       pltpu.SemaphoreType.DMA((2,)), pltpu.SemaphoreType.DMA((2,)),
                       pltpu.SemaphoreType.DMA((2,)), pltpu.SemaphoreType.DMA((2,))],
        compiler_params=pltpu.CompilerParams(needs_layout_passes=False),
        name="peer_experts_sc")
    def body(idx_hbm, h_hbm, g_hbm, u_hbm, v_hbm, o_hbm,
             idx_v, x_v, g_v, out_v, ubuf, vbuf, sem_meta, sem_out, sem_u, sem_v):
        wid = lax.axis_index("c") * info.num_subcores + lax.axis_index("s")
        tok0 = wid * per
        lane = lax.iota(I32, SC_LANES)

        def meta_copies(ti, ms):
            t = tok0 + ti
            return (pltpu.make_async_copy(idx_hbm.at[t], idx_v.at[ms], sem_meta.at[ms]),
                    pltpu.make_async_copy(h_hbm.at[t], x_v.at[ms], sem_meta.at[ms]),
                    pltpu.make_async_copy(g_hbm.at[t], g_v.at[ms], sem_meta.at[ms]))

        def gather_copies(ms, c, slot):
            ids = idx_v.at[ms, pl.ds(c * SC_CHUNK, SC_CHUNK)]
            return (pltpu.make_async_copy(u_hbm.at[ids], ubuf.at[slot], sem_u.at[slot]),
                    pltpu.make_async_copy(v_hbm.at[ids], vbuf.at[slot], sem_v.at[slot]))

        def out_copy(ti, ms):
            return pltpu.make_async_copy(out_v.at[ms], o_hbm.at[tok0 + ti], sem_out.at[ms])

        for cp in meta_copies(0, 0):
            cp.start()
        for cp in meta_copies(0, 0):
            cp.wait()
        for cp in gather_copies(0, 0, 0):
            cp.start()

        def token(ti, carry):
            ms = ti % 2
            nxt = 1 - ms

            @pl.when(ti + 1 < per)
            def _():
                for cp in meta_copies(ti + 1, nxt):
                    cp.start()

            @pl.when(ti >= 2)
            def _():
                out_copy(ti - 2, ms).wait()

            def zero(j, c):
                out_v[ms, pl.ds(j * SC_LANES, SC_LANES)] = jnp.zeros((SC_LANES,), F32)
                return c
            lax.fori_loop(0, n_vec, zero, 0)

            for c in range(n_chunks):
                slot = c % 2
                if c + 1 < n_chunks:
                    for cp in gather_copies(ms, c + 1, 1 - slot):
                        cp.start()
                else:
                    @pl.when(ti + 1 < per)
                    def _():
                        for cp in meta_copies(ti + 1, nxt):
                            cp.wait()
                        for cp in gather_copies(nxt, 0, 1 - slot):
                            cp.start()
                cu, cv = gather_copies(ms, c, slot)
                cu.wait()

                def udot(j, accs):
                    xj = x_v[ms, pl.ds(j * SC_LANES, SC_LANES)]
                    return tuple(accs[r] + ubuf[slot, r, pl.ds(j * SC_LANES, SC_LANES)] * xj
                                 for r in range(SC_CHUNK))
                accs = lax.fori_loop(0, n_vec, udot,
                                     tuple(jnp.zeros((SC_LANES,), F32) for _ in range(SC_CHUNK)))
                act = jnp.zeros((SC_LANES,), F32)
                for r in range(SC_CHUNK):
                    act = jnp.where(lane == r, jnp.sum(accs[r]), act)
                y = GELU_C * (act + 0.044715 * (act * act * act))
                w = act / (1.0 + jnp.exp(-2.0 * y)) * g_v[ms, pl.ds(c * SC_CHUNK, SC_CHUNK)]
                ws = [jnp.sum(jnp.where(lane == r, w, 0.0)) for r in range(SC_CHUNK)]
                cv.wait()

                @plsc.parallel_loop(0, n_vec, unroll=2)
                def _(j):
                    parts = [ws[r] * vbuf[slot, r, pl.ds(j * SC_LANES, SC_LANES)] for r in range(SC_CHUNK)]
                    while len(parts) > 1:
                        parts = [parts[i] + parts[i + 1] for i in range(0, len(parts), 2)]
                    plsc.addupdate(out_v.at[ms, pl.ds(j * SC_LANES, SC_LANES)], parts[0])

            out_copy(ti, ms).start()
            return carry

        lax.fori_loop(0, per, token, 0)
        for back in (2, 1):
            if per >= back:
                out_copy(per - back, (per - back) % 2).wait()

    return body(idx, h, gates, u_rows, v_rows)


def _residual_kernel(x_ref, y_ref, m_ref, o_ref, *, d):
    o_ref[...] = x_ref[...] + m_ref[0][:, 5 * d:6 * d] * y_ref[...]


def _residual(x, y, mods, mod_row0, seq):
    m, d = y.shape
    tile = min(ROW_TILE, seq)
    return pl.pallas_call(
        functools.partial(_residual_kernel, d=d),
        grid=(m // tile,),
        in_specs=[pl.BlockSpec((tile, d), lambda i: (i, 0)),
                  pl.BlockSpec((tile, d), lambda i: (i, 0)),
                  _mod_spec(mod_row0, seq, tile, mods.shape[-1])],
        out_specs=pl.BlockSpec((tile, d), lambda i: (i, 0)),
        out_shape=jax.ShapeDtypeStruct((m, d), F32),
        compiler_params=_cparams("arbitrary"),
        name="peer_residual",
    )(x, y, mods)


def _peer(x, mods, mod_row0, seq, layer, nw, wq_bf16, keys_bf16, p_u, p_v, p_uv, sc_share):
    m, d = x.shape
    h, idx, gates = _peer_route(x, mods, mod_row0, seq, nw, wq_bf16, keys_bf16)
    n_blocks = m // PEER_BLOCK
    sc_blocks = (n_blocks * sc_share[0]) // sc_share[1]
    m_sc = sc_blocks * PEER_BLOCK
    idx = idx + layer * p_u.shape[1]
    y_sc = _peer_experts_sc(idx[:, :m_sc].T, h, gates[:, :m_sc].T,
                            p_u.reshape(-1, d), p_v.reshape(-1, d))
    x_sc = _residual(x, y_sc, mods, mod_row0, seq)
    if sc_blocks == n_blocks:
        return x_sc
    x_tc = _peer_gather(idx, h, gates, x, mods, mod_row0, seq, p_uv, sc_blocks)
    return jnp.concatenate([x_sc, x_tc], axis=0)


def _head_mean_square(x):
    n = x.shape[1]
    r = lax.broadcasted_iota(I32, (n, n), 0) // HEAD_DIM
    c = lax.broadcasted_iota(I32, (n, n), 1) // HEAD_DIM
    seg = jnp.where(r == c, 1.0 / HEAD_DIM, 0.0).astype(F32)
    return jnp.dot(x * x, seg, precision=HIGHEST, preferred_element_type=F32)


def _swap_rot_halves(x):
    n = x.shape[1]
    quarter = HEAD_DIM // 4
    lane = lax.broadcasted_iota(I32, x.shape, 1)
    lo = (lane % (2 * quarter)) < quarter
    return jnp.where(lo, pltpu.roll(x, n - quarter, axis=1), pltpu.roll(x, quarter, axis=1))


def _qkprep_kernel(q_ref, k_ref, qw_ref, kw_ref, *rest, rope):
    if rope:
        cos_ref, sin_ref, qo_ref, ko_ref = rest
    else:
        qo_ref, ko_ref = rest
    q = q_ref[...]
    k = k_ref[...]
    q = q * lax.rsqrt(_head_mean_square(q) + NORM_EPS) * qw_ref[...]
    k = k * lax.rsqrt(_head_mean_square(k) + NORM_EPS) * kw_ref[...]
    if rope:
        cos = cos_ref[...]
        sin = sin_ref[...]
        cq = jnp.concatenate([cos] * (q.shape[1] // LANES), axis=1)
        sq = jnp.concatenate([sin] * (q.shape[1] // LANES), axis=1)
        q = q * cq + _swap_rot_halves(q) * sq
        k = k * cos + _swap_rot_halves(k) * sin
    qo_ref[...] = q
    ko_ref[...] = k


def _rope_tables(seq):
    axis_dim = HEAD_DIM // 2
    inv_freq = ROPE_THETA ** (-jnp.arange(0, axis_dim, 2, dtype=F32) / axis_dim)
    t = jnp.arange(seq)
    pos = jnp.stack([(t // GRID_W).astype(F32), (t % GRID_W).astype(F32)], axis=1)
    lane = jnp.arange(LANES)
    dd = lane % HEAD_DIM
    ang = pos[:, dd // axis_dim] * inv_freq[dd % (axis_dim // 2)][None, :]
    sign = jnp.where((dd % axis_dim) < axis_dim // 2, -1.0, 1.0).astype(F32)
    return jnp.cos(ang), jnp.sin(ang) * sign[None, :]


def _qkprep(q, k, qw, kw, seq, rope):
    m, nq = q.shape
    nk = k.shape[1]
    tile = min(ROW_TILE, seq)
    qw_row = jnp.tile(qw, nq // HEAD_DIM).reshape(1, nq)
    kw_row = jnp.tile(kw, nk // HEAD_DIM).reshape(1, nk)
    in_specs = [pl.BlockSpec((tile, nq), lambda i: (i, 0)),
                pl.BlockSpec((tile, nk), lambda i: (i, 0)),
                pl.BlockSpec((1, nq), lambda i: (0, 0)),
                pl.BlockSpec((1, nk), lambda i: (0, 0))]
    args = [q, k, qw_row, kw_row]
    if rope:
        cos, sin = _rope_tables(seq)
        per_seq = seq // tile
        in_specs += [pl.BlockSpec((tile, LANES), lambda i: (i % per_seq, 0)),
                     pl.BlockSpec((tile, LANES), lambda i: (i % per_seq, 0))]
        args += [cos, sin]
    return pl.pallas_call(
        functools.partial(_qkprep_kernel, rope=rope),
        grid=(m // tile,),
        in_specs=in_specs,
        out_specs=[pl.BlockSpec((tile, nq), lambda i: (i, 0)),
                   pl.BlockSpec((tile, nk), lambda i: (i, 0))],
        out_shape=[jax.ShapeDtypeStruct((m, nq), F32), jax.ShapeDtypeStruct((m, nk), F32)],
        compiler_params=_cparams("arbitrary"),
        name="qk_prep",
    )(*args)


def _dup_halves(x):
    lane = lax.broadcasted_iota(I32, x.shape, 1)
    sw = pltpu.roll(x, HEAD_DIM, axis=1)
    lo = lane < HEAD_DIM
    return jnp.where(lo, x, sw), jnp.where(lo, sw, x)


def _attend(q, k_all, v_all, sink_ref, mask):
    scale = HEAD_DIM ** -0.5
    nt = (((1,), (1,)), ((), ()))
    kk = [a.astype(BF16) for a in _dup_halves(k_all)]
    vv = [a.astype(BF16) for a in _dup_halves(v_all)]
    lane = lax.broadcasted_iota(I32, (q.shape[0], LANES), 1)
    lo = lane < HEAD_DIM
    tiles = []
    for t in range(q.shape[1] // LANES):
        qt = q[:, t * LANES:(t + 1) * LANES]
        g = (2 * t) // GQA_GROUP
        halves = []
        for hh in range(2):
            head = 2 * t + hh
            qm = jnp.where(lo if hh == 0 else ~lo, qt, 0.0).astype(BF16)
            s = lax.dot_general(qm, kk[g], nt, preferred_element_type=F32) * scale
            if mask is not None:
                s = jnp.where(mask, s, NEG_BIG)
            sink = sink_ref[head]
            mx = jnp.maximum(jnp.max(s, axis=1, keepdims=True), sink)
            p = jnp.exp(s - mx)
            den = jnp.sum(p, axis=1, keepdims=True) + jnp.exp(sink - mx)
            p = (p / den).astype(BF16)
            halves.append(jnp.dot(p, vv[g], preferred_element_type=F32))
        tiles.append(jnp.where(lo, halves[0], halves[1]))
    return jnp.concatenate(tiles, axis=1)


def _ctx_attn_kernel(sink_ref, q_ref, k_ref, v_ref, o_ref):
    o_ref[...] = _attend(q_ref[...], k_ref[...], v_ref[...], sink_ref, None)


def _ctx_attention(q, k, v, sink, seq):
    m, nq = q.shape
    nk = k.shape[1]
    return pl.pallas_call(
        _ctx_attn_kernel,
        grid=(m // seq,),
        in_specs=[pl.BlockSpec(memory_space=pltpu.SMEM),
                  pl.BlockSpec((seq, nq), lambda b: (b, 0)),
                  pl.BlockSpec((seq, nk), lambda b: (b, 0)),
                  pl.BlockSpec((seq, nk), lambda b: (b, 0))],
        out_specs=pl.BlockSpec((seq, nq), lambda b: (b, 0)),
        out_shape=jax.ShapeDtypeStruct((m, nq), F32),
        compiler_params=_cparams("arbitrary"),
        name="ctx_attention",
    )(sink, q, k, v)


def _lat_attn_kernel(sink_ref, q_ref, kc_ref, vc_ref, kp_ref, k0_ref, kn_ref, vp_ref, v0_ref, vn_ref,
                     o_ref, *, seq):
    qb = pl.program_id(1)
    blk = q_ref.shape[0]
    n_ctx = kc_ref.shape[1]
    k_all = jnp.concatenate([kc_ref[0], kp_ref[...], k0_ref[...], kn_ref[...]], axis=0)
    v_all = jnp.concatenate([vc_ref[0], vp_ref[...], v0_ref[...], vn_ref[...]], axis=0)
    tk = k_all.shape[0]
    qpos = qb * blk + lax.broadcasted_iota(I32, (blk, tk), 0)
    col = lax.broadcasted_iota(I32, (blk, tk), 1)
    kpos = (qb - 1) * blk + col - n_ctx
    local_ok = (jnp.abs(qpos - kpos) <= WINDOW) & (kpos >= 0) & (kpos < seq)
    mask = (col < n_ctx) | local_ok
    o_ref[...] = _attend(q_ref[...], k_all, v_all, sink_ref, mask)


def _lat_attention(q, k, v, k_ctx, v_ctx, sink, seq):
    m, nq = q.shape
    nk = k.shape[1]
    blk = WINDOW
    nb = seq // blk
    n_ctx = k_ctx.shape[1]
    last = m // blk - 1

    def kv_spec(shift):
        return pl.BlockSpec((blk, nk), lambda b, i: (jnp.clip(b * nb + i + shift, 0, last), 0))

    ctx_spec = pl.BlockSpec((1, n_ctx, nk), lambda b, i: (b, 0, 0))
    return pl.pallas_call(
        functools.partial(_lat_attn_kernel, seq=seq),
        grid=(m // seq, nb),
        in_specs=[pl.BlockSpec(memory_space=pltpu.SMEM),
                  pl.BlockSpec((blk, nq), lambda b, i: (b * nb + i, 0)),
                  ctx_spec, ctx_spec,
                  kv_spec(-1), kv_spec(0), kv_spec(1),
                  kv_spec(-1), kv_spec(0), kv_spec(1)],
        out_specs=pl.BlockSpec((blk, nq), lambda b, i: (b * nb + i, 0)),
        out_shape=jax.ShapeDtypeStruct((m, nq), F32),
        compiler_params=_cparams("arbitrary", "arbitrary"),
        name="lat_attention",
    )(sink, q, k_ctx, v_ctx, k, k, k, v, v, v)

SSD_BLOCK = 256
SSD_PAIRS = SSD_HEADS // 2
SSD_INNER = SSD_HEADS * SSD_HEAD_DIM
HALO = SUBLANES


def _softplus(x):
    return jnp.maximum(x, 0.0) + jnp.log1p(jnp.exp(-jnp.abs(x)))


def _silu(x):
    return x * jax.nn.sigmoid(x)


def _ssd_decays(dt_raw, bias, a_log):
    n = dt_raw.shape[0]
    dt = _softplus(dt_raw + bias)
    log_a = dt * (-jnp.exp(a_log))
    r = lax.broadcasted_iota(I32, (n, n), 0)
    c = lax.broadcasted_iota(I32, (n, n), 1)
    lower = jnp.where(c <= r, 1.0, 0.0).astype(F32)
    upper = jnp.where(r <= c, 1.0, 0.0).astype(F32)
    cum_col = jnp.dot(lower, log_a, precision=HIGHEST, preferred_element_type=F32)
    dt_row = dt.T
    la_row = log_a.T
    cum_row = jnp.dot(la_row, upper, precision=HIGHEST, preferred_element_type=F32)
    return dt, log_a, cum_col, dt_row, la_row, cum_row


def _ssd_scan_chunk(xs, bmat, cmat, w_of, q_scale_of, k_scale_of, carry_of, s_ref):
    nt = (((1,), (1,)), ((), ()))
    n = xs.shape[0]
    lane = lax.broadcasted_iota(I32, (n, LANES), 1)
    lo = lane < SSD_HEAD_DIM
    lane_s = lax.broadcasted_iota(I32, (D_STATE, LANES), 1)
    lo_s = lane_s < SSD_HEAD_DIM
    b_t = bmat.T
    cb16 = cmat.astype(BF16)
    ys = []
    for pair in range(SSD_PAIRS):
        g = (2 * pair) // (SSD_HEADS // SSD_GROUPS)
        in_g = (lane // D_STATE) == g
        cg = jnp.where(in_g, cmat, 0.0)
        cb = lax.dot_general(cg.astype(BF16), bmat.astype(BF16), nt, preferred_element_type=F32)
        x_pair = xs[:, pair * LANES:(pair + 1) * LANES]
        x16 = x_pair.astype(BF16)
        s_old = s_ref[pair]
        s2 = jnp.concatenate([s_old, s_old], axis=0).astype(BF16)
        bg_t = b_t[g * D_STATE:(g + 1) * D_STATE, :]
        y_h, s_h = [], []
        for hh in range(2):
            h = 2 * pair + hh
            w = (cb * w_of(h)).astype(BF16)
            y = jnp.dot(w, x16, preferred_element_type=F32)
            cq = (cg * q_scale_of(h)).astype(BF16)
            y = y + jnp.dot(cq, s2, preferred_element_type=F32)
            y_h.append(y)
            kt = (bg_t * k_scale_of(h)).astype(BF16)
            s_h.append(carry_of(h) * s_old + jnp.dot(kt, x16, preferred_element_type=F32))
        ys.append(jnp.where(lo, y_h[0], y_h[1]))
        s_ref[pair] = jnp.where(lo_s, s_h[0], s_h[1])
    return jnp.concatenate(ys, axis=1)


def _ssd_fwd_kernel(x_ref, xp_ref, xn_ref, dt_ref, s0_ref, cw_ref, cb_ref, bias_ref, alog_ref,
                    y_ref, xc_ref, sfin_ref, s_ref):
    c = pl.program_id(1)
    nc = pl.num_programs(1)
    n = x_ref.shape[0]

    @pl.when(c == 0)
    def _():
        s_ref[...] = s0_ref[0]

    prev = jnp.where(c > 0, xp_ref[...], 0.0)
    nxt = jnp.where(c < nc - 1, xn_ref[...], 0.0)
    xe = jnp.concatenate([prev, x_ref[...], nxt], axis=0)
    pad = (CONV_K - 1) // 2
    acc = cb_ref[...] + cw_ref[0:1, :] * xe[HALO - pad:HALO - pad + n, :]
    for k in range(1, CONV_K):
        acc = acc + cw_ref[k:k + 1, :] * xe[HALO - pad + k:HALO - pad + k + n, :]
    xc = _silu(acc)
    xc_ref[...] = xc
    xs = xc[:, :SSD_INNER]
    bmat = xc[:, SSD_INNER:SSD_INNER + LANES]
    cmat = xc[:, SSD_INNER + LANES:SSD_INNER + 2 * LANES]

    dt, log_a, cum_col, dt_row, la_row, cum_row = _ssd_decays(dt_ref[...], bias_ref[...], alog_ref[...])
    r = lax.broadcasted_iota(I32, (n, n), 0)
    cc = lax.broadcasted_iota(I32, (n, n), 1)
    causal = cc <= r
    last_col = cum_col[n - 1:n, :]

    def w_of(h):
        seg = cum_col[:, h:h + 1] - cum_row[h:h + 1, :]
        return jnp.exp(jnp.where(causal, seg, NEG_BIG)) * dt_row[h:h + 1, :]

    def q_scale_of(h):
        return jnp.exp(cum_col[:, h:h + 1])

    def k_scale_of(h):
        return dt_row[h:h + 1, :] * jnp.exp(cum_row[h:h + 1, n - 1:n] - cum_row[h:h + 1, :])

    def carry_of(h):
        return jnp.exp(last_col[:, h:h + 1])

    y_ref[...] = _ssd_scan_chunk(xs, bmat, cmat, w_of, q_scale_of, k_scale_of, carry_of, s_ref)

    @pl.when(c == nc - 1)
    def _():
        sfin_ref[0] = s_ref[...]


def _ssd_bwd_kernel(xc_ref, dt_ref, yf_ref, z_ref, s0_ref, bias_ref, alog_ref, dskip_ref, nw_ref,
                    y_ref, sfin_ref, s_ref):
    c = pl.program_id(1)
    nc = pl.num_programs(1)
    n = xc_ref.shape[0]

    @pl.when(c == 0)
    def _():
        s_ref[...] = s0_ref[0]

    xc = xc_ref[...]
    xs = xc[:, :SSD_INNER]
    bmat = xc[:, SSD_INNER:SSD_INNER + LANES]
    cmat = xc[:, SSD_INNER + LANES:SSD_INNER + 2 * LANES]
    dt, log_a, cum_col, dt_row, la_row, cum_row = _ssd_decays(dt_ref[...], bias_ref[...], alog_ref[...])
    ex_col = cum_col - log_a
    ex_row = cum_row - la_row
    r = lax.broadcasted_iota(I32, (n, n), 0)
    cc = lax.broadcasted_iota(I32, (n, n), 1)
    anti = cc >= r
    tot_col = cum_col[n - 1:n, :]
    off = SSD_HEADS

    def w_of(h):
        j = off + h
        seg = ex_row[j:j + 1, :] - ex_col[:, j:j + 1]
        return jnp.exp(jnp.where(anti, seg, NEG_BIG)) * dt_row[j:j + 1, :]

    def q_scale_of(h):
        j = off + h
        return jnp.exp(tot_col[:, j:j + 1] - ex_col[:, j:j + 1])

    def k_scale_of(h):
        j = off + h
        return dt_row[j:j + 1, :] * jnp.exp(ex_row[j:j + 1, :])

    def carry_of(h):
        j = off + h
        return jnp.exp(tot_col[:, j:j + 1])

    y_b = _ssd_scan_chunk(xs, bmat, cmat, w_of, q_scale_of, k_scale_of, carry_of, s_ref)
    y = yf_ref[...] + y_b + dskip_ref[...] * xs
    y = y * _silu(z_ref[...])
    ms = jnp.mean(y * y, axis=-1, keepdims=True)
    y_ref[...] = y * lax.rsqrt(ms + NORM_EPS) * nw_ref[...]

    @pl.when(c == nc - 1)
    def _():
        sfin_ref[0] = s_ref[...]


def _pair_states(s):
    b, h, n, p = s.shape
    return s.reshape(b, h // 2, 2, n, p).transpose(0, 1, 3, 2, 4).reshape(b, h // 2, n, 2 * p)


def _unpair_states(s):
    b, hp, n, p2 = s.shape
    return s.reshape(b, hp, n, 2, p2 // 2).transpose(0, 1, 3, 2, 4).reshape(b, hp * 2, n, p2 // 2)


def _ssd(xbc, dt, z, s0_f, s0_b, conv_w, conv_b, dt_bias, a_log, d_skip, ssd_norm, seq):
    m, nx = xbc.shape
    nb = m // seq
    blk = min(SSD_BLOCK, seq)
    nc = seq // blk
    hb = blk // HALO
    n_halo = m // HALO
    pad16 = lambda a: jnp.pad(a.reshape(1, -1), ((0, 0), (0, LANES - a.size)))
    bias = pad16(dt_bias)
    alog = pad16(a_log)
    state_spec = pl.BlockSpec((1, SSD_PAIRS, D_STATE, LANES), lambda b, c: (b, 0, 0, 0))
    state_shape = jax.ShapeDtypeStruct((nb, SSD_PAIRS, D_STATE, LANES), F32)
    row = lambda width: pl.BlockSpec((1, width), lambda b, c: (0, 0))

    def fwd_rows(width):
        return pl.BlockSpec((blk, width), lambda b, c: (b * nc + c, 0))

    def bwd_rows(width):
        return pl.BlockSpec((blk, width), lambda b, c: (b * nc + nc - 1 - c, 0))

    y_f, xc, s_f = pl.pallas_call(
        _ssd_fwd_kernel,
        grid=(nb, nc),
        in_specs=[fwd_rows(nx),
                  pl.BlockSpec((HALO, nx), lambda b, c: (jnp.maximum((b * nc + c) * hb - 1, 0), 0)),
                  pl.BlockSpec((HALO, nx), lambda b, c: (jnp.minimum((b * nc + c + 1) * hb, n_halo - 1), 0)),
                  fwd_rows(LANES), state_spec,
                  pl.BlockSpec((CONV_K, nx), lambda b, c: (0, 0)), row(nx), row(LANES), row(LANES)],
        out_specs=[fwd_rows(SSD_INNER), fwd_rows(nx), state_spec],
        out_shape=[jax.ShapeDtypeStruct((m, SSD_INNER), F32), jax.ShapeDtypeStruct((m, nx), F32), state_shape],
        scratch_shapes=[pltpu.VMEM((SSD_PAIRS, D_STATE, LANES), F32)],
        compiler_params=_cparams("arbitrary", "arbitrary"),
        name="ssd_forward",
    )(xbc, xbc, xbc, dt, _pair_states(s0_f), conv_w, conv_b.reshape(1, nx), bias, alog)

    dskip = jnp.repeat(d_skip, SSD_HEAD_DIM).reshape(1, SSD_INNER)
    y, s_b = pl.pallas_call(
        _ssd_bwd_kernel,
        grid=(nb, nc),
        in_specs=[bwd_rows(nx), bwd_rows(LANES), bwd_rows(SSD_INNER), bwd_rows(SSD_INNER), state_spec,
                  row(LANES), row(LANES), row(SSD_INNER), row(SSD_INNER)],
        out_specs=[bwd_rows(SSD_INNER), state_spec],
        out_shape=[jax.ShapeDtypeStruct((m, SSD_INNER), F32), state_shape],
        scratch_shapes=[pltpu.VMEM((SSD_PAIRS, D_STATE, LANES), F32)],
        compiler_params=_cparams("arbitrary", "arbitrary"),
        name="ssd_backward",
    )(xc, dt, y_f, z, _pair_states(s0_b), bias, alog, dskip, ssd_norm.reshape(1, SSD_INNER))
    return y, _unpair_states(s_f), _unpair_states(s_b)

def _hgrn_kernel(q_ref, ff_ref, fb_ref, i_ref, g_ref, lb_ref, s0_ref, nw_ref, o_ref, sfin_ref,
                 sf_ref, sb_ref, ob_ref, *, layer):
    t_len = q_ref.shape[0]
    n = HGRN_CHUNK
    n_chunks = t_len // n
    tn = (((0,), (0,)), ((), ()))
    nt = (((1,), (1,)), ((), ()))

    lbp = lb_ref[...]
    e = jnp.exp(lbp - jnp.max(lbp, axis=0, keepdims=True))
    sm = e / jnp.sum(e, axis=0, keepdims=True)
    lb = sm[0] * 0.0
    for j in range(1, layer + 1):
        lb = lb + sm[j]

    r = lax.broadcasted_iota(I32, (n, n), 0)
    c = lax.broadcasted_iota(I32, (n, n), 1)
    lower = jnp.where(c <= r, 1.0, 0.0).astype(F32)
    srow = lax.broadcasted_iota(I32, (n, HGRN_DK), 0)
    qscale = HGRN_DK ** -0.5

    def chunk(row0, f_ref, lb_d, reverse, s_ref):
        q = _silu(q_ref[pl.ds(row0, n), :]) * qscale
        f = f_ref[pl.ds(row0, n), :]
        v = i_ref[pl.ds(row0, n), :]
        k = (1.0 - lb_d) * jax.nn.sigmoid(-f)
        lf = jnp.log(lb_d + (1.0 - lb_d) * jax.nn.sigmoid(f))
        cum = jnp.dot(lower, lf, precision=HIGHEST, preferred_element_type=F32)
        tot = cum[n - 1:n, :]
        if reverse:
            cum = cum - lf
        rows = []
        for t in range(n):
            tile0 = (t // SUBLANES) * SUBLANES
            lo, hi = (tile0, n) if reverse else (0, tile0 + SUBLANES)
            cum_s = cum[lo:hi]
            if reverse:
                seg = jnp.where(srow[lo:hi] >= t, cum_s - cum[t:t + 1, :], NEG_BIG)
            else:
                seg = jnp.where(srow[lo:hi] <= t, cum[t:t + 1, :] - cum_s, NEG_BIG)
            a = q[t:t + 1, :] * k[lo:hi] * jnp.exp(seg)
            sc = jnp.sum(a, axis=1, keepdims=True)
            rows.append(jnp.sum(sc * v[lo:hi], axis=0, keepdims=True))
        o = jnp.concatenate(rows, axis=0)
        s_old = s_ref[...]
        if reverse:
            q_in = q * jnp.exp(tot - cum)
            k_out = k * jnp.exp(cum)
        else:
            q_in = q * jnp.exp(cum)
            k_out = k * jnp.exp(tot - cum)
        o = o + lax.dot_general(q_in.astype(BF16), s_old.astype(BF16), nt, preferred_element_type=F32)
        s_ref[...] = jnp.exp(tot) * s_old + lax.dot_general(
            v.astype(BF16), k_out.astype(BF16), tn, preferred_element_type=F32)
        return o

    sf_ref[...] = s0_ref[0, 0, 0].T
    sb_ref[...] = s0_ref[0, 1, 0].T

    def body(ci, carry):
        row_f = pl.multiple_of(ci * n, n)
        row_b = pl.multiple_of((n_chunks - 1 - ci) * n, n)
        o_ref[pl.ds(row_f, n), :] = chunk(row_f, ff_ref, lb[0:1, :], False, sf_ref)
        ob_ref[pl.ds(row_b, n), :] = chunk(row_b, fb_ref, lb[1:2, :], True, sb_ref)
        return carry

    lax.fori_loop(0, n_chunks, body, 0)
    sfin_ref[0, 0, 0] = sf_ref[...].T
    sfin_ref[0, 1, 0] = sb_ref[...].T

    nw = nw_ref[...]
    blk = min(t_len, ROW_TILE)

    def finish(bi, carry):
        row0 = pl.multiple_of(bi * blk, blk)
        o = o_ref[pl.ds(row0, blk), :] + ob_ref[pl.ds(row0, blk), :]
        ms = jnp.mean(o * o, axis=-1, keepdims=True)
        o = o * lax.rsqrt(ms + NORM_EPS) * nw
        o_ref[pl.ds(row0, blk), :] = o * _silu(g_ref[pl.ds(row0, blk), :])
        return carry

    lax.fori_loop(0, t_len // blk, finish, 0)


def _hgrn(q, f_fw, f_bw, iv, g, o_lb, state0, g_norm, seq, layer):
    m, width = q.shape
    nb = m // seq
    dv = width // HGRN_HEADS
    col = pl.BlockSpec((seq, dv), lambda b, h: (b, h))
    state_spec = pl.BlockSpec((1, 2, 1, HGRN_DK, dv), lambda b, h: (b, 0, h, 0, 0))
    return pl.pallas_call(
        functools.partial(_hgrn_kernel, layer=layer),
        grid=(nb, HGRN_HEADS),
        in_specs=[col, col, col, col, col,
                  pl.BlockSpec((o_lb.shape[0], 2, HGRN_DK), lambda b, h: (0, 0, h)),
                  state_spec,
                  pl.BlockSpec((1, dv), lambda b, h: (0, 0))],
        out_specs=[col, state_spec],
        out_shape=[jax.ShapeDtypeStruct((m, width), F32),
                   jax.ShapeDtypeStruct((nb, 2, HGRN_HEADS, HGRN_DK, dv), F32)],
        scratch_shapes=[pltpu.VMEM((dv, HGRN_DK), F32), pltpu.VMEM((dv, HGRN_DK), F32),
                        pltpu.VMEM((seq, dv), F32)],
        compiler_params=_cparams("arbitrary", "arbitrary"),
        name="hgrn2",
    )(q, f_fw, f_bw, iv, g, o_lb, state0, g_norm.reshape(1, dv))

EVEN_SPLITS = ((0, 512), (512, 640), (640, 768), (768, 1280), (1280, 2048), (2048, 2176))
HGRN_SPLITS = tuple((i * 1024, (i + 1) * 1024) for i in range(5))


def _even_weight(w):
    main = EVEN_SPLITS[-1][0]
    return jnp.pad(w, ((0, 0), (0, LANES - (w.shape[1] - main)))).astype(BF16)


def _run_trunk(x3, mods, mod_row0, P, cache):
    nb, seq, d = x3.shape
    x = x3.reshape(nb * seq, d)
    depth = P['norm_mix'].shape[0]
    ks, vs, ssd_states, hgrn_states = [], [], [], []
    for l in range(depth):
        j = l // 2
        row0 = (l * SUBLANES + mod_row0, 0 if cache is None else 1)
        if l % 2 == 0:
            q, k, v, z, xbc, dt = _inproj(x, mods, row0, seq, P['norm_mix'][l], P['e_w_in'][j],
                                          EVEN_SPLITS, "even_in_proj")
            q, k = _qkprep(q, k, P['e_q_norm'][j], P['e_k_norm'][j], seq, rope=cache is not None)
            if cache is None:
                s0_f = jnp.zeros((nb, SSD_HEADS, D_STATE, SSD_HEAD_DIM), F32)
                s0_b = s0_f
                o_attn = _ctx_attention(q, k, v, P['e_sink'][j], seq)
            else:
                s0_f, s0_b = cache[2][:, j, 0], cache[2][:, j, 1]
                n_ctx = cache[0].shape[2]
                o_attn = _lat_attention(q, k, v, cache[0][:, j].reshape(nb, n_ctx, -1),
                                        cache[1][:, j].reshape(nb, n_ctx, -1), P['e_sink'][j], seq)
            y, s_f, s_b = _ssd(xbc, dt, z, s0_f, s0_b, P['e_conv_w'][j], P['e_conv_b'][j],
                               P['e_dt_bias'][j], P['e_a_log'][j], P['e_d_skip'][j], P['e_ssd_norm'][j], seq)
            if cache is None:
                ks.append(k.reshape(nb, seq, N_KV_HEADS, HEAD_DIM))
                vs.append(v.reshape(nb, seq, N_KV_HEADS, HEAD_DIM))
                ssd_states.append(jnp.stack([s_f, s_b], axis=1))
            mix = jnp.concatenate([o_attn, y], axis=1)
            x = _outproj(mix, x, mods, row0, seq, P['e_w_out'][j], "even_out_proj")
        else:
            q, f_fw, f_bw, iv, g = _inproj(x, mods, row0, seq, P['norm_mix'][l], P['o_w_in'][j],
                                           HGRN_SPLITS, "odd_in_proj")
            if cache is None:
                s0 = jnp.zeros((nb, 2, HGRN_HEADS, HGRN_DK, d // HGRN_HEADS), F32)
            else:
                s0 = cache[3][:, j]
            o, s_new = _hgrn(q, f_fw, f_bw, iv, g, P['o_lb'], s0, P['o_g_norm'][j], seq, j)
            if cache is None:
                hgrn_states.append(s_new)
            x = _outproj(o, x, mods, row0, seq, P['o_w_out'][j], "odd_out_proj")
        x = _peer(x, mods, row0, seq, l, P['norm_ffn'][l], P['p_w_q'][l], P['p_sub_keys'][l],
                  P['p_u'], P['p_v'], P['p_uv'],
                  SC_SHARE_CONTEXT if cache is None else SC_SHARE_LATENT)
    y = x.reshape(nb, seq, d)
    if cache is not None:
        return y, None
    return y, (jnp.stack(ks, axis=1), jnp.stack(vs, axis=1),
               jnp.stack(ssd_states, axis=1), jnp.stack(hgrn_states, axis=1))


def kernel(x_prompt, x_sample, cache_k, cache_v, state_ssd, state_hgrn, c, c_ctx, w_ada, b_ada, norm_mix, norm_ffn, e_w_in, e_q_norm, e_k_norm, e_sink, e_conv_w, e_conv_b, e_dt_bias, e_a_log, e_d_skip, e_ssd_norm, e_w_out, o_w_in, o_lb, o_g_norm, o_w_out, p_w_q, p_sub_keys, p_u, p_v):
    depth, d, d6 = w_ada.shape
    b_lat = x_sample.shape[0]
    cond_rows = jnp.concatenate([c_ctx[None, :], c, jnp.zeros((SUBLANES - 1 - b_lat, d), F32)], axis=0)
    mods = _modulation(cond_rows, w_ada, b_ada).reshape(depth * SUBLANES, 1, d6)
    P = {
        'norm_mix': norm_mix, 'norm_ffn': norm_ffn,
        'e_w_in': jnp.stack([_even_weight(w) for w in e_w_in]), 'e_q_norm': e_q_norm, 'e_k_norm': e_k_norm,
        'e_sink': e_sink, 'e_conv_w': e_conv_w, 'e_conv_b': e_conv_b, 'e_dt_bias': e_dt_bias,
        'e_a_log': e_a_log, 'e_d_skip': e_d_skip, 'e_ssd_norm': e_ssd_norm,
        'e_w_out': e_w_out.astype(BF16),
        'o_w_in': o_w_in.astype(BF16), 'o_lb': o_lb, 'o_g_norm': o_g_norm, 'o_w_out': o_w_out.astype(BF16),
        'p_w_q': p_w_q.astype(BF16),
        'p_sub_keys': p_sub_keys.astype(BF16).reshape(depth, PEER_HEADS * 2, PEER_NKEYS, PEER_DKEY),
        'p_u': p_u, 'p_v': p_v,
        'p_uv': jnp.concatenate([p_u.reshape(-1, d), p_v.reshape(-1, d)], axis=1),
    }
    y_prompt, new_state = _run_trunk(x_prompt, mods, 0, P, None)
    y_sample, _ = _run_trunk(x_sample, mods, 1, P, (cache_k, cache_v, state_ssd, state_hgrn))
    return (y_prompt, y_sample) + new_state
```

```python
import functools
import math

import jax
import jax.numpy as jnp
from jax import lax
from jax.experimental import pallas as pl
from jax.experimental.pallas import tpu as pltpu
from jax.experimental.pallas import tpu_sc as plsc

F32 = jnp.float32
BF16 = jnp.bfloat16
I32 = jnp.int32
HIGHEST = lax.Precision.HIGHEST

NORM_EPS = 1e-6
NEG_BIG = -1e30
LANES = 128
SUBLANES = 8
VMEM_LIMIT = 48 * 1024 * 1024

GRID_W = 64
HEAD_DIM = 64
N_Q_HEADS = 8
N_KV_HEADS = 2
GQA_GROUP = 4
WINDOW = 128
ROPE_THETA = 10000.0
SSD_HEADS = 8
SSD_HEAD_DIM = 64
SSD_GROUPS = 2
D_STATE = 64
CONV_K = 5
HGRN_HEADS = 8
HGRN_DK = 128
HGRN_CHUNK = 32
PEER_HEADS = 8
PEER_NKEYS = 128
PEER_TOPK = 16
PEER_DKEY = 128
PEER_PAIRS = PEER_HEADS * PEER_TOPK

ROW_TILE = 256
PEER_BLOCK = 128
SC_LANES = 16
SC_CHUNK = 16
SC_SHARE_CONTEXT = (7, 8)
SC_SHARE_LATENT = (13, 16)
GELU_C = math.sqrt(2.0 / math.pi)


def _cparams(*sem):
    return pltpu.CompilerParams(dimension_semantics=sem, vmem_limit_bytes=VMEM_LIMIT)


def _norm_mod(x, nw, scale, shift):
    ms = jnp.mean(x * x, axis=-1, keepdims=True)
    return (x * lax.rsqrt(ms + NORM_EPS)) * nw * (1.0 + scale) + shift


def _mod_kernel(c_ref, w_ref, b_ref, o_ref):
    c = c_ref[...]
    s = c * jax.nn.sigmoid(c)
    o_ref[0] = jnp.dot(s, w_ref[0], precision=HIGHEST, preferred_element_type=F32) + b_ref[0]


def _modulation(cond_rows, w_ada, b_ada):
    depth, d, n = w_ada.shape
    rows = cond_rows.shape[0]
    return pl.pallas_call(
        _mod_kernel,
        grid=(depth, n // d),
        in_specs=[pl.BlockSpec((rows, d), lambda l, j: (0, 0)),
                  pl.BlockSpec((1, d, d), lambda l, j: (l, 0, j)),
                  pl.BlockSpec((1, 1, d), lambda l, j: (l, 0, j))],
        out_specs=pl.BlockSpec((1, rows, d), lambda l, j: (l, 0, j)),
        out_shape=jax.ShapeDtypeStruct((depth, rows, n), F32),
        compiler_params=_cparams("arbitrary", "arbitrary"),
        name="modulation",
    )(cond_rows, w_ada, b_ada.reshape(depth, 1, n))


def _mod_spec(mod_row0, seq, tile, d6, first_tile=0):
    row0, per_batch = mod_row0
    return pl.BlockSpec((1, 1, d6),
                        lambda i: (row0 + per_batch * (((i + first_tile) * tile) // seq), 0, 0))


def _inproj_kernel(x_ref, m_ref, nw_ref, w_ref, *o_refs, splits, d):
    m = m_ref[0]
    h = _norm_mod(x_ref[...], nw_ref[...], m[:, d:2 * d], m[:, 0:d]).astype(BF16)
    for o_ref, (a, b) in zip(o_refs, splits):
        o_ref[...] = jnp.dot(h, w_ref[:, a:b], preferred_element_type=F32)


def _inproj(x, mods, mod_row0, seq, nw, w_bf16, splits, name):
    m, d = x.shape
    n = w_bf16.shape[1]
    tile = min(ROW_TILE, seq)
    return pl.pallas_call(
        functools.partial(_inproj_kernel, splits=splits, d=d),
        grid=(m // tile,),
        in_specs=[pl.BlockSpec((tile, d), lambda i: (i, 0)),
                  _mod_spec(mod_row0, seq, tile, mods.shape[-1]),
                  pl.BlockSpec((1, d), lambda i: (0, 0)),
                  pl.BlockSpec((d, n), lambda i: (0, 0))],
        out_specs=[pl.BlockSpec((tile, b - a), lambda i: (i, 0)) for a, b in splits],
        out_shape=[jax.ShapeDtypeStruct((m, b - a), F32) for a, b in splits],
        compiler_params=_cparams("arbitrary"),
        name=name,
    )(x, mods, nw.reshape(1, d), w_bf16)


def _outproj_kernel(mix_ref, x_ref, m_ref, w_ref, o_ref, *, d):
    y = jnp.dot(mix_ref[...].astype(BF16), w_ref[...], preferred_element_type=F32)
    o_ref[...] = x_ref[...] + m_ref[0][:, 2 * d:3 * d] * y


def _outproj(mix, x, mods, mod_row0, seq, w_bf16, name):
    m, d = x.shape
    k = mix.shape[1]
    tile = min(ROW_TILE, seq)
    return pl.pallas_call(
        functools.partial(_outproj_kernel, d=d),
        grid=(m // tile,),
        in_specs=[pl.BlockSpec((tile, k), lambda i: (i, 0)),
                  pl.BlockSpec((tile, d), lambda i: (i, 0)),
                  _mod_spec(mod_row0, seq, tile, mods.shape[-1]),
                  pl.BlockSpec((k, d), lambda i: (0, 0))],
        out_specs=pl.BlockSpec((tile, d), lambda i: (i, 0)),
        out_shape=jax.ShapeDtypeStruct((m, d), F32),
        compiler_params=_cparams("arbitrary"),
        name=name,
    )(mix, x, mods, w_bf16)


def _topk_over_rows(s, k, payload=None):
    n = s.shape[0]
    iota = lax.broadcasted_iota(I32, s.shape, 0)
    vals, idxs, pays = [], [], []
    for _ in range(k):
        m = jnp.max(s, axis=0, keepdims=True)
        i = jnp.min(jnp.where(s == m, iota, n), axis=0, keepdims=True)
        hit = iota == i
        vals.append(m)
        idxs.append(i)
        if payload is not None:
            pays.append(jnp.max(jnp.where(hit, payload, -1), axis=0, keepdims=True))
        s = jnp.where(hit, -jnp.inf, s)
    out = (jnp.concatenate(vals, axis=0), jnp.concatenate(idxs, axis=0))
    if payload is not None:
        out += (jnp.concatenate(pays, axis=0),)
    return out


def _peer_route_kernel(x_ref, m_ref, nw_ref, wq_ref, keys_ref, h_ref, e_ref, g_ref, *, d):
    m = m_ref[0]
    h = _norm_mod(x_ref[...], nw_ref[...], m[:, 4 * d:5 * d], m[:, 3 * d:4 * d])
    h_ref[...] = h
    hb = h.astype(BF16)
    nt = (((1,), (1,)), ((), ()))
    for head in range(PEER_HEADS):
        tops = []
        for half in range(2):
            c0 = (head * 2 + half) * PEER_DKEY
            q = jnp.dot(hb, wq_ref[:, c0:c0 + PEER_DKEY], preferred_element_type=F32)
            s = lax.dot_general(keys_ref[head * 2 + half], q.astype(BF16), nt,
                                preferred_element_type=F32)
            tops.append(_topk_over_rows(s, PEER_TOPK))
        (s0, i0), (s1, i1) = tops
        widths = [PEER_TOPK // (a + 1) for a in range(PEER_TOPK)]
        n_pad = -sum(widths) % SUBLANES
        cand_s = jnp.concatenate([s0[a:a + 1] + s1[:w] for a, w in enumerate(widths)]
                                 + [jnp.full((n_pad, s0.shape[1]), -jnp.inf, F32)], axis=0)
        cand_e = jnp.concatenate([i0[a:a + 1] * PEER_NKEYS + i1[:w] for a, w in enumerate(widths)]
                                 + [jnp.zeros((n_pad, s0.shape[1]), I32)], axis=0)
        best_s, _, best_e = _topk_over_rows(cand_s, PEER_TOPK, payload=cand_e)
        p = jnp.exp(best_s - best_s[0:1])
        r0 = head * PEER_TOPK
        e_ref[r0:r0 + PEER_TOPK, :] = best_e
        g_ref[r0:r0 + PEER_TOPK, :] = p / jnp.sum(p, axis=0, keepdims=True)


def _peer_route(x, mods, mod_row0, seq, nw, wq_bf16, keys_bf16):
    m, d = x.shape
    tile = min(ROW_TILE, seq)
    nq = wq_bf16.shape[1]
    return pl.pallas_call(
        functools.partial(_peer_route_kernel, d=d),
        grid=(m // tile,),
        in_specs=[pl.BlockSpec((tile, d), lambda i: (i, 0)),
                  _mod_spec(mod_row0, seq, tile, mods.shape[-1]),
                  pl.BlockSpec((1, d), lambda i: (0, 0)),
                  pl.BlockSpec((d, nq), lambda i: (0, 0)),
                  pl.BlockSpec(keys_bf16.shape, lambda i: (0, 0, 0))],
        out_specs=[pl.BlockSpec((tile, d), lambda i: (i, 0)),
                   pl.BlockSpec((PEER_PAIRS, tile), lambda i: (0, i)),
                   pl.BlockSpec((PEER_PAIRS, tile), lambda i: (0, i))],
        out_shape=[jax.ShapeDtypeStruct((m, d), F32),
                   jax.ShapeDtypeStruct((PEER_PAIRS, m), I32),
                   jax.ShapeDtypeStruct((PEER_PAIRS, m), F32)],
        compiler_params=_cparams("arbitrary"),
        name="peer_route",
    )(x, mods, nw.reshape(1, d), wq_bf16, keys_bf16)


def _peer_gather_kernel(idx_hbm, h_ref, g_ref, x_ref, m_ref, uv_hbm, o_ref,
                        idx_smem, buf, sem_idx, sem, *, d):
    blk = pl.program_id(0)
    n_groups = PEER_BLOCK // SUBLANES

    cp = pltpu.make_async_copy(idx_hbm.at[blk], idx_smem, sem_idx)
    cp.start()
    cp.wait()

    def issue(tok, slot):
        first = tok * PEER_PAIRS

        def body(grp, carry):
            base = pl.multiple_of(grp * SUBLANES, SUBLANES)
            for r in range(SUBLANES):
                e = idx_smem[first + base + r]
                pltpu.make_async_copy(uv_hbm.at[pl.ds(e, 1)], buf.at[slot, pl.ds(base + r, 1)],
                                      sem.at[slot]).start()
            return carry
        lax.fori_loop(0, PEER_PAIRS // SUBLANES, body, 0)

    def wait(slot):
        pltpu.make_async_copy(uv_hbm.at[pl.ds(0, PEER_PAIRS)], buf.at[slot], sem.at[slot]).wait()

    gate2 = m_ref[0][:, 5 * d:6 * d]
    lane = lax.broadcasted_iota(I32, (PEER_PAIRS, PEER_BLOCK), 1)

    issue(0, 0)

    def group(grp, carry):
        base = pl.multiple_of(grp * SUBLANES, SUBLANES)
        h8 = h_ref[pl.ds(base, SUBLANES), :]
        rows = []
        for r in range(SUBLANES):
            tok = base + r
            slot = r % 2
            if r < SUBLANES - 1:
                issue(tok + 1, 1 - slot)
            else:
                @pl.when(grp < n_groups - 1)
                def _():
                    issue(tok + 1, 1 - slot)
            wait(slot)
            act = jnp.sum(buf[slot, :, :d] * h8[r:r + 1, :], axis=1, keepdims=True)
            gate = jnp.sum(jnp.where(lane == tok, g_ref[...], 0.0), axis=1, keepdims=True)
            w = jax.nn.gelu(act) * gate
            rows.append(jnp.sum(buf[slot, :, d:] * w, axis=0, keepdims=True))
        out8 = jnp.concatenate(rows, axis=0)
        o_ref[pl.ds(base, SUBLANES), :] = x_ref[pl.ds(base, SUBLANES), :] + gate2 * out8
        return carry

    lax.fori_loop(0, n_groups, group, 0)


def _peer_gather(idx, h, gates, x, mods, mod_row0, seq, p_uv, first_block):
    m, d = x.shape
    nblk = m // PEER_BLOCK - first_block
    idx3 = idx[:, first_block * PEER_BLOCK:].T.reshape(nblk, PEER_BLOCK * PEER_PAIRS)
    rows = pl.BlockSpec((PEER_BLOCK, d), lambda i: (i + first_block, 0))
    return pl.pallas_call(
        functools.partial(_peer_gather_kernel, d=d),
        grid=(nblk,),
        in_specs=[pl.BlockSpec(memory_space=pl.ANY),
                  rows,
                  pl.BlockSpec((PEER_PAIRS, PEER_BLOCK), lambda i: (0, i + first_block)),
                  rows,
                  _mod_spec(mod_row0, seq, PEER_BLOCK, mods.shape[-1], first_block),
                  pl.BlockSpec(memory_space=pl.ANY)],
        out_specs=pl.BlockSpec((PEER_BLOCK, d), lambda i: (i, 0)),
        out_shape=jax.ShapeDtypeStruct((nblk * PEER_BLOCK, d), F32),
        scratch_shapes=[pltpu.SMEM((PEER_BLOCK * PEER_PAIRS,), I32),
                        pltpu.VMEM((2, PEER_PAIRS, 2 * d), F32),
                        pltpu.SemaphoreType.DMA,
                        pltpu.SemaphoreType.DMA((2,))],
        compiler_params=_cparams("arbitrary"),
        name="peer_gather",
    )(idx3, h, gates, x, mods, p_uv)


def _peer_experts_sc(idx, h, gates, u_rows, v_rows):
    m = idx.shape[0]
    d = h.shape[1]
    info = plsc.get_sparse_core_info()
    n_workers = info.num_cores * info.num_subcores
    per = m // n_workers
    n_chunks = PEER_PAIRS // SC_CHUNK
    n_vec = d // SC_LANES
    mesh = plsc.VectorSubcoreMesh(core_axis_name="c", subcore_axis_name="s")

    @functools.partial(
        pl.kernel, out_type=jax.ShapeDtypeStruct((m, d), F32), mesh=mesh,
        scratch_types=[pltpu.VMEM((2, PEER_PAIRS), I32), pltpu.VMEM((2, d), F32),
                       pltpu.VMEM((2, PEER_PAIRS), F32), pltpu.VMEM((2, d), F32),
                       pltpu.VMEM((2, SC_CHUNK, d), F32), pltpu.VMEM((2, SC_CHUNK, d), F32),
                       pltpu.SemaphoreType.DMA((2,)), pltpu.SemaphoreType.DMA((2,)),
                       pltpu.SemaphoreType.DMA((2,)), pltpu.SemaphoreType.DMA((2,))],
        compiler_params=pltpu.CompilerParams(needs_layout_passes=False),
        name="peer_experts_sc")
    def body(idx_hbm, h_hbm, g_hbm, u_hbm, v_hbm, o_hbm,
             idx_v, x_v, g_v, out_v, ubuf, vbuf, sem_meta, sem_out, sem_u, sem_v):
        wid = lax.axis_index("c") * info.num_subcores + lax.axis_index("s")
        tok0 = wid * per
        lane = lax.iota(I32, SC_LANES)

        def meta_copies(ti, ms):
            t = tok0 + ti
            return (pltpu.make_async_copy(idx_hbm.at[t], idx_v.at[ms], sem_meta.at[ms]),
                    pltpu.make_async_copy(h_hbm.at[t], x_v.at[ms], sem_meta.at[ms]),
                    pltpu.make_async_copy(g_hbm.at[t], g_v.at[ms], sem_meta.at[ms]))

        def gather_copies(ms, c, slot):
            ids = idx_v.at[ms, pl.ds(c * SC_CHUNK, SC_CHUNK)]
            return (pltpu.make_async_copy(u_hbm.at[ids], ubuf.at[slot], sem_u.at[slot]),
                    pltpu.make_async_copy(v_hbm.at[ids], vbuf.at[slot], sem_v.at[slot]))

        def out_copy(ti, ms):
            return pltpu.make_async_copy(out_v.at[ms], o_hbm.at[tok0 + ti], sem_out.at[ms])

        for cp in meta_copies(0, 0):
            cp.start()
        for cp in meta_copies(0, 0):
            cp.wait()
        for cp in gather_copies(0, 0, 0):
            cp.start()

        def token(ti, carry):
            ms = ti % 2
            nxt = 1 - ms

            @pl.when(ti + 1 < per)
            def _():
                for cp in meta_copies(ti + 1, nxt):
                    cp.start()

            @pl.when(ti >= 2)
            def _():
                out_copy(ti - 2, ms).wait()

            def zero(j, c):
                out_v[ms, pl.ds(j * SC_LANES, SC_LANES)] = jnp.zeros((SC_LANES,), F32)
                return c
            lax.fori_loop(0, n_vec, zero, 0)

            for c in range(n_chunks):
                slot = c % 2
                if c + 1 < n_chunks:
                    for cp in gather_copies(ms, c + 1, 1 - slot):
                        cp.start()
                else:
                    @pl.when(ti + 1 < per)
                    def _():
                        for cp in meta_copies(ti + 1, nxt):
                            cp.wait()
                        for cp in gather_copies(nxt, 0, 1 - slot):
                            cp.start()
                cu, cv = gather_copies(ms, c, slot)
                cu.wait()

                def udot(j, accs):
                    xj = x_v[ms, pl.ds(j * SC_LANES, SC_LANES)]
                    return tuple(accs[r] + ubuf[slot, r, pl.ds(j * SC_LANES, SC_LANES)] * xj
                                 for r in range(SC_CHUNK))
                accs = lax.fori_loop(0, n_vec, udot,
                                     tuple(jnp.zeros((SC_LANES,), F32) for _ in range(SC_CHUNK)))
                act = jnp.zeros((SC_LANES,), F32)
                for r in range(SC_CHUNK):
                    act = jnp.where(lane == r, jnp.sum(accs[r]), act)
                y = GELU_C * (act + 0.044715 * (act * act * act))
                w = act / (1.0 + jnp.exp(-2.0 * y)) * g_v[ms, pl.ds(c * SC_CHUNK, SC_CHUNK)]
                ws = [jnp.sum(jnp.where(lane == r, w, 0.0)) for r in range(SC_CHUNK)]
                cv.wait()

                @plsc.parallel_loop(0, n_vec, unroll=2)
                def _(j):
                    parts = [ws[r] * vbuf[slot, r, pl.ds(j * SC_LANES, SC_LANES)] for r in range(SC_CHUNK)]
                    while len(parts) > 1:
                        parts = [parts[i] + parts[i + 1] for i in range(0, len(parts), 2)]
                    plsc.addupdate(out_v.at[ms, pl.ds(j * SC_LANES, SC_LANES)], parts[0])

            out_copy(ti, ms).start()
            return carry

        lax.fori_loop(0, per, token, 0)
        for back in (2, 1):
            if per >= back:
                out_copy(per - back, (per - back) % 2).wait()

    return body(idx, h, gates, u_rows, v_rows)


def _residual_kernel(x_ref, y_ref, m_ref, o_ref, *, d):
    o_ref[...] = x_ref[...] + m_ref[0][:, 5 * d:6 * d] * y_ref[...]


def _residual(x, y, mods, mod_row0, seq):
    m, d = y.shape
    tile = min(ROW_TILE, seq)
    return pl.pallas_call(
        functools.partial(_residual_kernel, d=d),
        grid=(m // tile,),
        in_specs=[pl.BlockSpec((tile, d), lambda i: (i, 0)),
                  pl.BlockSpec((tile, d), lambda i: (i, 0)),
                  _mod_spec(mod_row0, seq, tile, mods.shape[-1])],
        out_specs=pl.BlockSpec((tile, d), lambda i: (i, 0)),
        out_shape=jax.ShapeDtypeStruct((m, d), F32),
        compiler_params=_cparams("arbitrary"),
        name="peer_residual",
    )(x, y, mods)


def _peer(x, mods, mod_row0, seq, layer, nw, wq_bf16, keys_bf16, p_u, p_v, p_uv, sc_share):
    m, d = x.shape
    h, idx, gates = _peer_route(x, mods, mod_row0, seq, nw, wq_bf16, keys_bf16)
    n_blocks = m // PEER_BLOCK
    sc_blocks = (n_blocks * sc_share[0]) // sc_share[1]
    m_sc = sc_blocks * PEER_BLOCK
    idx = idx + layer * p_u.shape[1]
    y_sc = _peer_experts_sc(idx[:, :m_sc].T, h, gates[:, :m_sc].T,
                            p_u.reshape(-1, d), p_v.reshape(-1, d))
    x_sc = _residual(x, y_sc, mods, mod_row0, seq)
    if sc_blocks == n_blocks:
        return x_sc
    x_tc = _peer_gather(idx, h, gates, x, mods, mod_row0, seq, p_uv, sc_blocks)
    return jnp.concatenate([x_sc, x_tc], axis=0)


def _head_mean_square(x):
    n = x.shape[1]
    r = lax.broadcasted_iota(I32, (n, n), 0) // HEAD_DIM
    c = lax.broadcasted_iota(I32, (n, n), 1) // HEAD_DIM
    seg = jnp.where(r == c, 1.0 / HEAD_DIM, 0.0).astype(F32)
    return jnp.dot(x * x, seg, precision=HIGHEST, preferred_element_type=F32)


def _swap_rot_halves(x):
    n = x.shape[1]
    quarter = HEAD_DIM // 4
    lane = lax.broadcasted_iota(I32, x.shape, 1)
    lo = (lane % (2 * quarter)) < quarter
    return jnp.where(lo, pltpu.roll(x, n - quarter, axis=1), pltpu.roll(x, quarter, axis=1))


def _qkprep_kernel(q_ref, k_ref, qw_ref, kw_ref, *rest, rope):
    if rope:
        cos_ref, sin_ref, qo_ref, ko_ref = rest
    else:
        qo_ref, ko_ref = rest
    q = q_ref[...]
    k = k_ref[...]
    q = q * lax.rsqrt(_head_mean_square(q) + NORM_EPS) * qw_ref[...]
    k = k * lax.rsqrt(_head_mean_square(k) + NORM_EPS) * kw_ref[...]
    if rope:
        cos = cos_ref[...]
        sin = sin_ref[...]
        cq = jnp.concatenate([cos] * (q.shape[1] // LANES), axis=1)
        sq = jnp.concatenate([sin] * (q.shape[1] // LANES), axis=1)
        q = q * cq + _swap_rot_halves(q) * sq
        k = k * cos + _swap_rot_halves(k) * sin
    qo_ref[...] = q
    ko_ref[...] = k


def _rope_tables(seq):
    axis_dim = HEAD_DIM // 2
    inv_freq = ROPE_THETA ** (-jnp.arange(0, axis_dim, 2, dtype=F32) / axis_dim)
    t = jnp.arange(seq)
    pos = jnp.stack([(t // GRID_W).astype(F32), (t % GRID_W).astype(F32)], axis=1)
    lane = jnp.arange(LANES)
    dd = lane % HEAD_DIM
    ang = pos[:, dd // axis_dim] * inv_freq[dd % (axis_dim // 2)][None, :]
    sign = jnp.where((dd % axis_dim) < axis_dim // 2, -1.0, 1.0).astype(F32)
    return jnp.cos(ang), jnp.sin(ang) * sign[None, :]


def _qkprep(q, k, qw, kw, seq, rope):
    m, nq = q.shape
    nk = k.shape[1]
    tile = min(ROW_TILE, seq)
    qw_row = jnp.tile(qw, nq // HEAD_DIM).reshape(1, nq)
    kw_row = jnp.tile(kw, nk // HEAD_DIM).reshape(1, nk)
    in_specs = [pl.BlockSpec((tile, nq), lambda i: (i, 0)),
                pl.BlockSpec((tile, nk), lambda i: (i, 0)),
                pl.BlockSpec((1, nq), lambda i: (0, 0)),
                pl.BlockSpec((1, nk), lambda i: (0, 0))]
    args = [q, k, qw_row, kw_row]
    if rope:
        cos, sin = _rope_tables(seq)
        per_seq = seq // tile
        in_specs += [pl.BlockSpec((tile, LANES), lambda i: (i % per_seq, 0)),
                     pl.BlockSpec((tile, LANES), lambda i: (i % per_seq, 0))]
        args += [cos, sin]
    return pl.pallas_call(
        functools.partial(_qkprep_kernel, rope=rope),
        grid=(m // tile,),
        in_specs=in_specs,
        out_specs=[pl.BlockSpec((tile, nq), lambda i: (i, 0)),
                   pl.BlockSpec((tile, nk), lambda i: (i, 0))],
        out_shape=[jax.ShapeDtypeStruct((m, nq), F32), jax.ShapeDtypeStruct((m, nk), F32)],
        compiler_params=_cparams("arbitrary"),
        name="qk_prep",
    )(*args)


def _dup_halves(x):
    lane = lax.broadcasted_iota(I32, x.shape, 1)
    sw = pltpu.roll(x, HEAD_DIM, axis=1)
    lo = lane < HEAD_DIM
    return jnp.where(lo, x, sw), jnp.where(lo, sw, x)


def _attend(q, k_all, v_all, sink_ref, mask):
    scale = HEAD_DIM ** -0.5
    nt = (((1,), (1,)), ((), ()))
    kk = [a.astype(BF16) for a in _dup_halves(k_all)]
    vv = [a.astype(BF16) for a in _dup_halves(v_all)]
    lane = lax.broadcasted_iota(I32, (q.shape[0], LANES), 1)
    lo = lane < HEAD_DIM
    tiles = []
    for t in range(q.shape[1] // LANES):
        qt = q[:, t * LANES:(t + 1) * LANES]
        g = (2 * t) // GQA_GROUP
        halves = []
        for hh in range(2):
            head = 2 * t + hh
            qm = jnp.where(lo if hh == 0 else ~lo, qt, 0.0).astype(BF16)
            s = lax.dot_general(qm, kk[g], nt, preferred_element_type=F32) * scale
            if mask is not None:
                s = jnp.where(mask, s, NEG_BIG)
            sink = sink_ref[head]
            mx = jnp.maximum(jnp.max(s, axis=1, keepdims=True), sink)
            p = jnp.exp(s - mx)
            den = jnp.sum(p, axis=1, keepdims=True) + jnp.exp(sink - mx)
            p = (p / den).astype(BF16)
            halves.append(jnp.dot(p, vv[g], preferred_element_type=F32))
        tiles.append(jnp.where(lo, halves[0], halves[1]))
    return jnp.concatenate(tiles, axis=1)


def _ctx_attn_kernel(sink_ref, q_ref, k_ref, v_ref, o_ref):
    o_ref[...] = _attend(q_ref[...], k_ref[...], v_ref[...], sink_ref, None)


def _ctx_attention(q, k, v, sink, seq):
    m, nq = q.shape
    nk = k.shape[1]
    return pl.pallas_call(
        _ctx_attn_kernel,
        grid=(m // seq,),
        in_specs=[pl.BlockSpec(memory_space=pltpu.SMEM),
                  pl.BlockSpec((seq, nq), lambda b: (b, 0)),
                  pl.BlockSpec((seq, nk), lambda b: (b, 0)),
                  pl.BlockSpec((seq, nk), lambda b: (b, 0))],
        out_specs=pl.BlockSpec((seq, nq), lambda b: (b, 0)),
        out_shape=jax.ShapeDtypeStruct((m, nq), F32),
        compiler_params=_cparams("arbitrary"),
        name="ctx_attention",
    )(sink, q, k, v)


def _lat_attn_kernel(sink_ref, q_ref, kc_ref, vc_ref, kp_ref, k0_ref, kn_ref, vp_ref, v0_ref, vn_ref,
                     o_ref, *, seq):
    qb = pl.program_id(1)
    blk = q_ref.shape[0]
    n_ctx = kc_ref.shape[1]
    k_all = jnp.concatenate([kc_ref[0], kp_ref[...], k0_ref[...], kn_ref[...]], axis=0)
    v_all = jnp.concatenate([vc_ref[0], vp_ref[...], v0_ref[...], vn_ref[...]], axis=0)
    tk = k_all.shape[0]
    qpos = qb * blk + lax.broadcasted_iota(I32, (blk, tk), 0)
    col = lax.broadcasted_iota(I32, (blk, tk), 1)
    kpos = (qb - 1) * blk + col - n_ctx
    local_ok = (jnp.abs(qpos - kpos) <= WINDOW) & (kpos >= 0) & (kpos < seq)
    mask = (col < n_ctx) | local_ok
    o_ref[...] = _attend(q_ref[...], k_all, v_all, sink_ref, mask)


def _lat_attention(q, k, v, k_ctx, v_ctx, sink, seq):
    m, nq = q.shape
    nk = k.shape[1]
    blk = WINDOW
    nb = seq // blk
    n_ctx = k_ctx.shape[1]
    last = m // blk - 1

    def kv_spec(shift):
        return pl.BlockSpec((blk, nk), lambda b, i: (jnp.clip(b * nb + i + shift, 0, last), 0))

    ctx_spec = pl.BlockSpec((1, n_ctx, nk), lambda b, i: (b, 0, 0))
    return pl.pallas_call(
        functools.partial(_lat_attn_kernel, seq=seq),
        grid=(m // seq, nb),
        in_specs=[pl.BlockSpec(memory_space=pltpu.SMEM),
                  pl.BlockSpec((blk, nq), lambda b, i: (b * nb + i, 0)),
                  ctx_spec, ctx_spec,
                  kv_spec(-1), kv_spec(0), kv_spec(1),
                  kv_spec(-1), kv_spec(0), kv_spec(1)],
        out_specs=pl.BlockSpec((blk, nq), lambda b, i: (b * nb + i, 0)),
        out_shape=jax.ShapeDtypeStruct((m, nq), F32),
        compiler_params=_cparams("arbitrary", "arbitrary"),
        name="lat_attention",
    )(sink, q, k_ctx, v_ctx, k, k, k, v, v, v)

SSD_BLOCK = 256
SSD_PAIRS = SSD_HEADS // 2
SSD_INNER = SSD_HEADS * SSD_HEAD_DIM
HALO = SUBLANES


def _softplus(x):
    return jnp.maximum(x, 0.0) + jnp.log1p(jnp.exp(-jnp.abs(x)))


def _silu(x):
    return x * jax.nn.sigmoid(x)


def _ssd_decays(dt_raw, bias, a_log):
    n = dt_raw.shape[0]
    dt = _softplus(dt_raw + bias)
    log_a = dt * (-jnp.exp(a_log))
    r = lax.broadcasted_iota(I32, (n, n), 0)
    c = lax.broadcasted_iota(I32, (n, n), 1)
    lower = jnp.where(c <= r, 1.0, 0.0).astype(F32)
    upper = jnp.where(r <= c, 1.0, 0.0).astype(F32)
    cum_col = jnp.dot(lower, log_a, precision=HIGHEST, preferred_element_type=F32)
    dt_row = dt.T
    la_row = log_a.T
    cum_row = jnp.dot(la_row, upper, precision=HIGHEST, preferred_element_type=F32)
    return dt, log_a, cum_col, dt_row, la_row, cum_row


def _ssd_scan_chunk(xs, bmat, cmat, w_of, q_scale_of, k_scale_of, carry_of, s_ref):
    nt = (((1,), (1,)), ((), ()))
    n = xs.shape[0]
    lane = lax.broadcasted_iota(I32, (n, LANES), 1)
    lo = lane < SSD_HEAD_DIM
    lane_s = lax.broadcasted_iota(I32, (D_STATE, LANES), 1)
    lo_s = lane_s < SSD_HEAD_DIM
    b_t = bmat.T
    cb16 = cmat.astype(BF16)
    ys = []
    for pair in range(SSD_PAIRS):
        g = (2 * pair) // (SSD_HEADS // SSD_GROUPS)
        in_g = (lane // D_STATE) == g
        cg = jnp.where(in_g, cmat, 0.0)
        cb = lax.dot_general(cg.astype(BF16), bmat.astype(BF16), nt, preferred_element_type=F32)
        x_pair = xs[:, pair * LANES:(pair + 1) * LANES]
        x16 = x_pair.astype(BF16)
        s_old = s_ref[pair]
        s2 = jnp.concatenate([s_old, s_old], axis=0).astype(BF16)
        bg_t = b_t[g * D_STATE:(g + 1) * D_STATE, :]
        y_h, s_h = [], []
        for hh in range(2):
            h = 2 * pair + hh
            w = (cb * w_of(h)).astype(BF16)
            y = jnp.dot(w, x16, preferred_element_type=F32)
            cq = (cg * q_scale_of(h)).astype(BF16)
            y = y + jnp.dot(cq, s2, preferred_element_type=F32)
            y_h.append(y)
            kt = (bg_t * k_scale_of(h)).astype(BF16)
            s_h.append(carry_of(h) * s_old + jnp.dot(kt, x16, preferred_element_type=F32))
        ys.append(jnp.where(lo, y_h[0], y_h[1]))
        s_ref[pair] = jnp.where(lo_s, s_h[0], s_h[1])
    return jnp.concatenate(ys, axis=1)


def _ssd_fwd_kernel(x_ref, xp_ref, xn_ref, dt_ref, s0_ref, cw_ref, cb_ref, bias_ref, alog_ref,
                    y_ref, xc_ref, sfin_ref, s_ref):
    c = pl.program_id(1)
    nc = pl.num_programs(1)
    n = x_ref.shape[0]

    @pl.when(c == 0)
    def _():
        s_ref[...] = s0_ref[0]

    prev = jnp.where(c > 0, xp_ref[...], 0.0)
    nxt = jnp.where(c < nc - 1, xn_ref[...], 0.0)
    xe = jnp.concatenate([prev, x_ref[...], nxt], axis=0)
    pad = (CONV_K - 1) // 2
    acc = cb_ref[...] + cw_ref[0:1, :] * xe[HALO - pad:HALO - pad + n, :]
    for k in range(1, CONV_K):
        acc = acc + cw_ref[k:k + 1, :] * xe[HALO - pad + k:HALO - pad + k + n, :]
    xc = _silu(acc)
    xc_ref[...] = xc
    xs = xc[:, :SSD_INNER]
    bmat = xc[:, SSD_INNER:SSD_INNER + LANES]
    cmat = xc[:, SSD_INNER + LANES:SSD_INNER + 2 * LANES]

    dt, log_a, cum_col, dt_row, la_row, cum_row = _ssd_decays(dt_ref[...], bias_ref[...], alog_ref[...])
    r = lax.broadcasted_iota(I32, (n, n), 0)
    cc = lax.broadcasted_iota(I32, (n, n), 1)
    causal = cc <= r
    last_col = cum_col[n - 1:n, :]

    def w_of(h):
        seg = cum_col[:, h:h + 1] - cum_row[h:h + 1, :]
        return jnp.exp(jnp.where(causal, seg, NEG_BIG)) * dt_row[h:h + 1, :]

    def q_scale_of(h):
        return jnp.exp(cum_col[:, h:h + 1])

    def k_scale_of(h):
        return dt_row[h:h + 1, :] * jnp.exp(cum_row[h:h + 1, n - 1:n] - cum_row[h:h + 1, :])

    def carry_of(h):
        return jnp.exp(last_col[:, h:h + 1])

    y_ref[...] = _ssd_scan_chunk(xs, bmat, cmat, w_of, q_scale_of, k_scale_of, carry_of, s_ref)

    @pl.when(c == nc - 1)
    def _():
        sfin_ref[0] = s_ref[...]


def _ssd_bwd_kernel(xc_ref, dt_ref, yf_ref, z_ref, s0_ref, bias_ref, alog_ref, dskip_ref, nw_ref,
                    y_ref, sfin_ref, s_ref):
    c = pl.program_id(1)
    nc = pl.num_programs(1)
    n = xc_ref.shape[0]

    @pl.when(c == 0)
    def _():
        s_ref[...] = s0_ref[0]

    xc = xc_ref[...]
    xs = xc[:, :SSD_INNER]
    bmat = xc[:, SSD_INNER:SSD_INNER + LANES]
    cmat = xc[:, SSD_INNER + LANES:SSD_INNER + 2 * LANES]
    dt, log_a, cum_col, dt_row, la_row, cum_row = _ssd_decays(dt_ref[...], bias_ref[...], alog_ref[...])
    ex_col = cum_col - log_a
    ex_row = cum_row - la_row
    r = lax.broadcasted_iota(I32, (n, n), 0)
    cc = lax.broadcasted_iota(I32, (n, n), 1)
    anti = cc >= r
    tot_col = cum_col[n - 1:n, :]
    off = SSD_HEADS

    def w_of(h):
        j = off + h
        seg = ex_row[j:j + 1, :] - ex_col[:, j:j + 1]
        return jnp.exp(jnp.where(anti, seg, NEG_BIG)) * dt_row[j:j + 1, :]

    def q_scale_of(h):
        j = off + h
        return jnp.exp(tot_col[:, j:j + 1] - ex_col[:, j:j + 1])

    def k_scale_of(h):
        j = off + h
        return dt_row[j:j + 1, :] * jnp.exp(ex_row[j:j + 1, :])

    def carry_of(h):
        j = off + h
        return jnp.exp(tot_col[:, j:j + 1])

    y_b = _ssd_scan_chunk(xs, bmat, cmat, w_of, q_scale_of, k_scale_of, carry_of, s_ref)
    y = yf_ref[...] + y_b + dskip_ref[...] * xs
    y = y * _silu(z_ref[...])
    ms = jnp.mean(y * y, axis=-1, keepdims=True)
    y_ref[...] = y * lax.rsqrt(ms + NORM_EPS) * nw_ref[...]

    @pl.when(c == nc - 1)
    def _():
        sfin_ref[0] = s_ref[...]


def _pair_states(s):
    b, h, n, p = s.shape
    return s.reshape(b, h // 2, 2, n, p).transpose(0, 1, 3, 2, 4).reshape(b, h // 2, n, 2 * p)


def _unpair_states(s):
    b, hp, n, p2 = s.shape
    return s.reshape(b, hp, n, 2, p2 // 2).transpose(0, 1, 3, 2, 4).reshape(b, hp * 2, n, p2 // 2)


def _ssd(xbc, dt, z, s0_f, s0_b, conv_w, conv_b, dt_bias, a_log, d_skip, ssd_norm, seq):
    m, nx = xbc.shape
    nb = m // seq
    blk = min(SSD_BLOCK, seq)
    nc = seq // blk
    hb = blk // HALO
    n_halo = m // HALO
    pad16 = lambda a: jnp.pad(a.reshape(1, -1), ((0, 0), (0, LANES - a.size)))
    bias = pad16(dt_bias)
    alog = pad16(a_log)
    state_spec = pl.BlockSpec((1, SSD_PAIRS, D_STATE, LANES), lambda b, c: (b, 0, 0, 0))
    state_shape = jax.ShapeDtypeStruct((nb, SSD_PAIRS, D_STATE, LANES), F32)
    row = lambda width: pl.BlockSpec((1, width), lambda b, c: (0, 0))

    def fwd_rows(width):
        return pl.BlockSpec((blk, width), lambda b, c: (b * nc + c, 0))

    def bwd_rows(width):
        return pl.BlockSpec((blk, width), lambda b, c: (b * nc + nc - 1 - c, 0))

    y_f, xc, s_f = pl.pallas_call(
        _ssd_fwd_kernel,
        grid=(nb, nc),
        in_specs=[fwd_rows(nx),
                  pl.BlockSpec((HALO, nx), lambda b, c: (jnp.maximum((b * nc + c) * hb - 1, 0), 0)),
                  pl.BlockSpec((HALO, nx), lambda b, c: (jnp.minimum((b * nc + c + 1) * hb, n_halo - 1), 0)),
                  fwd_rows(LANES), state_spec,
                  pl.BlockSpec((CONV_K, nx), lambda b, c: (0, 0)), row(nx), row(LANES), row(LANES)],
        out_specs=[fwd_rows(SSD_INNER), fwd_rows(nx), state_spec],
        out_shape=[jax.ShapeDtypeStruct((m, SSD_INNER), F32), jax.ShapeDtypeStruct((m, nx), F32), state_shape],
        scratch_shapes=[pltpu.VMEM((SSD_PAIRS, D_STATE, LANES), F32)],
        compiler_params=_cparams("arbitrary", "arbitrary"),
        name="ssd_forward",
    )(xbc, xbc, xbc, dt, _pair_states(s0_f), conv_w, conv_b.reshape(1, nx), bias, alog)

    dskip = jnp.repeat(d_skip, SSD_HEAD_DIM).reshape(1, SSD_INNER)
    y, s_b = pl.pallas_call(
        _ssd_bwd_kernel,
        grid=(nb, nc),
        in_specs=[bwd_rows(nx), bwd_rows(LANES), bwd_rows(SSD_INNER), bwd_rows(SSD_INNER), state_spec,
                  row(LANES), row(LANES), row(SSD_INNER), row(SSD_INNER)],
        out_specs=[bwd_rows(SSD_INNER), state_spec],
        out_shape=[jax.ShapeDtypeStruct((m, SSD_INNER), F32), state_shape],
        scratch_shapes=[pltpu.VMEM((SSD_PAIRS, D_STATE, LANES), F32)],
        compiler_params=_cparams("arbitrary", "arbitrary"),
        name="ssd_backward",
    )(xc, dt, y_f, z, _pair_states(s0_b), bias, alog, dskip, ssd_norm.reshape(1, SSD_INNER))
    return y, _unpair_states(s_f), _unpair_states(s_b)

def _hgrn_kernel(q_ref, ff_ref, fb_ref, i_ref, g_ref, lb_ref, s0_ref, nw_ref, o_ref, sfin_ref,
                 sf_ref, sb_ref, ob_ref, *, layer):
    t_len = q_ref.shape[0]
    n = HGRN_CHUNK
    n_chunks = t_len // n
    tn = (((0,), (0,)), ((), ()))
    nt = (((1,), (1,)), ((), ()))

    lbp = lb_ref[...]
    e = jnp.exp(lbp - jnp.max(lbp, axis=0, keepdims=True))
    sm = e / jnp.sum(e, axis=0, keepdims=True)
    lb = sm[0] * 0.0
    for j in range(1, layer + 1):
        lb = lb + sm[j]

    r = lax.broadcasted_iota(I32, (n, n), 0)
    c = lax.broadcasted_iota(I32, (n, n), 1)
    lower = jnp.where(c <= r, 1.0, 0.0).astype(F32)
    srow = lax.broadcasted_iota(I32, (n, HGRN_DK), 0)
    qscale = HGRN_DK ** -0.5

    def chunk(row0, f_ref, lb_d, reverse, s_ref):
        q = _silu(q_ref[pl.ds(row0, n), :]) * qscale
        f = f_ref[pl.ds(row0, n), :]
        v = i_ref[pl.ds(row0, n), :]
        k = (1.0 - lb_d) * jax.nn.sigmoid(-f)
        lf = jnp.log(lb_d + (1.0 - lb_d) * jax.nn.sigmoid(f))
        cum = jnp.dot(lower, lf, precision=HIGHEST, preferred_element_type=F32)
        tot = cum[n - 1:n, :]
        if reverse:
            cum = cum - lf
        rows = []
        for t in range(n):
            tile0 = (t // SUBLANES) * SUBLANES
            lo, hi = (tile0, n) if reverse else (0, tile0 + SUBLANES)
            cum_s = cum[lo:hi]
            if reverse:
                seg = jnp.where(srow[lo:hi] >= t, cum_s - cum[t:t + 1, :], NEG_BIG)
            else:
                seg = jnp.where(srow[lo:hi] <= t, cum[t:t + 1, :] - cum_s, NEG_BIG)
            a = q[t:t + 1, :] * k[lo:hi] * jnp.exp(seg)
            sc = jnp.sum(a, axis=1, keepdims=True)
            rows.append(jnp.sum(sc * v[lo:hi], axis=0, keepdims=True))
        o = jnp.concatenate(rows, axis=0)
        s_old = s_ref[...]
        if reverse:
            q_in = q * jnp.exp(tot - cum)
            k_out = k * jnp.exp(cum)
        else:
            q_in = q * jnp.exp(cum)
            k_out = k * jnp.exp(tot - cum)
        o = o + lax.dot_general(q_in.astype(BF16), s_old.astype(BF16), nt, preferred_element_type=F32)
        s_ref[...] = jnp.exp(tot) * s_old + lax.dot_general(
            v.astype(BF16), k_out.astype(BF16), tn, preferred_element_type=F32)
        return o

    sf_ref[...] = s0_ref[0, 0, 0].T
    sb_ref[...] = s0_ref[0, 1, 0].T

    def body(ci, carry):
        row_f = pl.multiple_of(ci * n, n)
        row_b = pl.multiple_of((n_chunks - 1 - ci) * n, n)
        o_ref[pl.ds(row_f, n), :] = chunk(row_f, ff_ref, lb[0:1, :], False, sf_ref)
        ob_ref[pl.ds(row_b, n), :] = chunk(row_b, fb_ref, lb[1:2, :], True, sb_ref)
        return carry

    lax.fori_loop(0, n_chunks, body, 0)
    sfin_ref[0, 0, 0] = sf_ref[...].T
    sfin_ref[0, 1, 0] = sb_ref[...].T

    nw = nw_ref[...]
    blk = min(t_len, ROW_TILE)

    def finish(bi, carry):
        row0 = pl.multiple_of(bi * blk, blk)
        o = o_ref[pl.ds(row0, blk), :] + ob_ref[pl.ds(row0, blk), :]
        ms = jnp.mean(o * o, axis=-1, keepdims=True)
        o = o * lax.rsqrt(ms + NORM_EPS) * nw
        o_ref[pl.ds(row0, blk), :] = o * _silu(g_ref[pl.ds(row0, blk), :])
        return carry

    lax.fori_loop(0, t_len // blk, finish, 0)


def _hgrn(q, f_fw, f_bw, iv, g, o_lb, state0, g_norm, seq, layer):
    m, width = q.shape
    nb = m // seq
    dv = width // HGRN_HEADS
    col = pl.BlockSpec((seq, dv), lambda b, h: (b, h))
    state_spec = pl.BlockSpec((1, 2, 1, HGRN_DK, dv), lambda b, h: (b, 0, h, 0, 0))
    return pl.pallas_call(
        functools.partial(_hgrn_kernel, layer=layer),
        grid=(nb, HGRN_HEADS),
        in_specs=[col, col, col, col, col,
                  pl.BlockSpec((o_lb.shape[0], 2, HGRN_DK), lambda b, h: (0, 0, h)),
                  state_spec,
                  pl.BlockSpec((1, dv), lambda b, h: (0, 0))],
        out_specs=[col, state_spec],
        out_shape=[jax.ShapeDtypeStruct((m, width), F32),
                   jax.ShapeDtypeStruct((nb, 2, HGRN_HEADS, HGRN_DK, dv), F32)],
        scratch_shapes=[pltpu.VMEM((dv, HGRN_DK), F32), pltpu.VMEM((dv, HGRN_DK), F32),
                        pltpu.VMEM((seq, dv), F32)],
        compiler_params=_cparams("arbitrary", "arbitrary"),
        name="hgrn2",
    )(q, f_fw, f_bw, iv, g, o_lb, state0, g_norm.reshape(1, dv))

EVEN_SPLITS = ((0, 512), (512, 640), (640, 768), (768, 1280), (1280, 2048), (2048, 2176))
HGRN_SPLITS = tuple((i * 1024, (i + 1) * 1024) for i in range(5))


def _even_weight(w):
    main = EVEN_SPLITS[-1][0]
    return jnp.pad(w, ((0, 0), (0, LANES - (w.shape[1] - main)))).astype(BF16)


def _run_trunk(x3, mods, mod_row0, P, cache):
    nb, seq, d = x3.shape
    x = x3.reshape(nb * seq, d)
    depth = P['norm_mix'].shape[0]
    ks, vs, ssd_states, hgrn_states = [], [], [], []
    for l in range(depth):
        j = l // 2
        row0 = (l * SUBLANES + mod_row0, 0 if cache is None else 1)
        if l % 2 == 0:
            q, k, v, z, xbc, dt = _inproj(x, mods, row0, seq, P['norm_mix'][l], P['e_w_in'][j],
                                          EVEN_SPLITS, "even_in_proj")
            q, k = _qkprep(q, k, P['e_q_norm'][j], P['e_k_norm'][j], seq, rope=cache is not None)
            if cache is None:
                s0_f = jnp.zeros((nb, SSD_HEADS, D_STATE, SSD_HEAD_DIM), F32)
                s0_b = s0_f
                o_attn = _ctx_attention(q, k, v, P['e_sink'][j], seq)
            else:
                s0_f, s0_b = cache[2][:, j, 0], cache[2][:, j, 1]
                n_ctx = cache[0].shape[2]
                o_attn = _lat_attention(q, k, v, cache[0][:, j].reshape(nb, n_ctx, -1),
                                        cache[1][:, j].reshape(nb, n_ctx, -1), P['e_sink'][j], seq)
            y, s_f, s_b = _ssd(xbc, dt, z, s0_f, s0_b, P['e_conv_w'][j], P['e_conv_b'][j],
                               P['e_dt_bias'][j], P['e_a_log'][j], P['e_d_skip'][j], P['e_ssd_norm'][j], seq)
            if cache is None:
                ks.append(k.reshape(nb, seq, N_KV_HEADS, HEAD_DIM))
                vs.append(v.reshape(nb, seq, N_KV_HEADS, HEAD_DIM))
                ssd_states.append(jnp.stack([s_f, s_b], axis=1))
            mix = jnp.concatenate([o_attn, y], axis=1)
            x = _outproj(mix, x, mods, row0, seq, P['e_w_out'][j], "even_out_proj")
        else:
            q, f_fw, f_bw, iv, g = _inproj(x, mods, row0, seq, P['norm_mix'][l], P['o_w_in'][j],
                                           HGRN_SPLITS, "odd_in_proj")
            if cache is None:
                s0 = jnp.zeros((nb, 2, HGRN_HEADS, HGRN_DK, d // HGRN_HEADS), F32)
            else:
                s0 = cache[3][:, j]
            o, s_new = _hgrn(q, f_fw, f_bw, iv, g, P['o_lb'], s0, P['o_g_norm'][j], seq, j)
            if cache is None:
                hgrn_states.append(s_new)
            x = _outproj(o, x, mods, row0, seq, P['o_w_out'][j], "odd_out_proj")
        x = _peer(x, mods, row0, seq, l, P['norm_ffn'][l], P['p_w_q'][l], P['p_sub_keys'][l],
                  P['p_u'], P['p_v'], P['p_uv'],
                  SC_SHARE_CONTEXT if cache is None else SC_SHARE_LATENT)
    y = x.reshape(nb, seq, d)
    if cache is not None:
        return y, None
    return y, (jnp.stack(ks, axis=1), jnp.stack(vs, axis=1),
               jnp.stack(ssd_states, axis=1), jnp.stack(hgrn_states, axis=1))


def kernel(x_prompt, x_sample, cache_k, cache_v, state_ssd, state_hgrn, c, c_ctx, w_ada, b_ada, norm_mix, norm_ffn, e_w_in, e_q_norm, e_k_norm, e_sink, e_conv_w, e_conv_b, e_dt_bias, e_a_log, e_d_skip, e_ssd_norm, e_w_out, o_w_in, o_lb, o_g_norm, o_w_out, p_w_q, p_sub_keys, p_u, p_v):
    depth, d, d6 = w_ada.shape
    b_lat = x_sample.shape[0]
    cond_rows = jnp.concatenate([c_ctx[None, :], c, jnp.zeros((SUBLANES - 1 - b_lat, d), F32)], axis=0)
    mods = _modulation(cond_rows, w_ada, b_ada).reshape(depth * SUBLANES, 1, d6)
    P = {
        'norm_mix': norm_mix, 'norm_ffn': norm_ffn,
        'e_w_in': jnp.stack([_even_weight(w) for w in e_w_in]), 'e_q_norm': e_q_norm, 'e_k_norm': e_k_norm,
        'e_sink': e_sink, 'e_conv_w': e_conv_w, 'e_conv_b': e_conv_b, 'e_dt_bias': e_dt_bias,
        'e_a_log': e_a_log, 'e_d_skip': e_d_skip, 'e_ssd_norm': e_ssd_norm,
        'e_w_out': e_w_out.astype(BF16),
        'o_w_in': o_w_in.astype(BF16), 'o_lb': o_lb, 'o_g_norm': o_g_norm, 'o_w_out': o_w_out.astype(BF16),
        'p_w_q': p_w_q.astype(BF16),
        'p_sub_keys': p_sub_keys.astype(BF16).reshape(depth, PEER_HEADS * 2, PEER_NKEYS, PEER_DKEY),
        'p_u': p_u, 'p_v': p_v,
        'p_uv': jnp.concatenate([p_u.reshape(-1, d), p_v.reshape(-1, d)], axis=1),
    }
    y_prompt, new_state = _run_trunk(x_prompt, mods, 0, P, None)
    y_sample, _ = _run_trunk(x_sample, mods, 1, P, (cache_k, cache_v, state_ssd, state_hgrn))
    return (y_prompt, y_sample) + new_state
```

```python
import functools
import math

import jax
import jax.numpy as jnp
from jax import lax
from jax.experimental import pallas as pl
from jax.experimental.pallas import tpu as pltpu
from jax.experimental.pallas import tpu_sc as plsc

F32 = jnp.float32
BF16 = jnp.bfloat16
I32 = jnp.int32
HIGHEST = lax.Precision.HIGHEST

NORM_EPS = 1e-6
NEG_BIG = -1e30
LANES = 128
SUBLANES = 8
VMEM_LIMIT = 48 * 1024 * 1024

GRID_W = 64
HEAD_DIM = 64
N_Q_HEADS = 8
N_KV_HEADS = 2
GQA_GROUP = 4
WINDOW = 128
ROPE_THETA = 10000.0
SSD_HEADS = 8
SSD_HEAD_DIM = 64
SSD_GROUPS = 2
D_STATE = 64
CONV_K = 5
HGRN_HEADS = 8
HGRN_DK = 128
HGRN_CHUNK = 32
PEER_HEADS = 8
PEER_NKEYS = 128
PEER_TOPK = 16
PEER_DKEY = 128
PEER_PAIRS = PEER_HEADS * PEER_TOPK

ROW_TILE = 256
PEER_BLOCK = 128
SC_LANES = 16
SC_CHUNK = 16
SC_SHARE_CONTEXT = (7, 8)
SC_SHARE_LATENT = (13, 16)
GELU_C = math.sqrt(2.0 / math.pi)


def _cparams(*sem):
    return pltpu.CompilerParams(dimension_semantics=sem, vmem_limit_bytes=VMEM_LIMIT)


def _norm_mod(x, nw, scale, shift):
    ms = jnp.mean(x * x, axis=-1, keepdims=True)
    return (x * lax.rsqrt(ms + NORM_EPS)) * nw * (1.0 + scale) + shift


def _mod_kernel(c_ref, w_ref, b_ref, o_ref):
    c = c_ref[...]
    s = c * jax.nn.sigmoid(c)
    o_ref[0] = jnp.dot(s, w_ref[0], precision=HIGHEST, preferred_element_type=F32) + b_ref[0]


def _modulation(cond_rows, w_ada, b_ada):
    depth, d, n = w_ada.shape
    rows = cond_rows.shape[0]
    return pl.pallas_call(
        _mod_kernel,
        grid=(depth, n // d),
        in_specs=[pl.BlockSpec((rows, d), lambda l, j: (0, 0)),
                  pl.BlockSpec((1, d, d), lambda l, j: (l, 0, j)),
                  pl.BlockSpec((1, 1, d), lambda l, j: (l, 0, j))],
        out_specs=pl.BlockSpec((1, rows, d), lambda l, j: (l, 0, j)),
        out_shape=jax.ShapeDtypeStruct((depth, rows, n), F32),
        compiler_params=_cparams("arbitrary", "arbitrary"),
        name="modulation",
    )(cond_rows, w_ada, b_ada.reshape(depth, 1, n))


def _mod_spec(mod_row0, seq, tile, d6, first_tile=0):
    row0, per_batch = mod_row0
    return pl.BlockSpec((1, 1, d6),
                        lambda i: (row0 + per_batch * (((i + first_tile) * tile) // seq), 0, 0))


def _inproj_kernel(x_ref, m_ref, nw_ref, w_ref, *o_refs, splits, d):
    m = m_ref[0]
    h = _norm_mod(x_ref[...], nw_ref[...], m[:, d:2 * d], m[:, 0:d]).astype(BF16)
    for o_ref, (a, b) in zip(o_refs, splits):
        o_ref[...] = jnp.dot(h, w_ref[:, a:b], preferred_element_type=F32)


def _inproj(x, mods, mod_row0, seq, nw, w_bf16, splits, name):
    m, d = x.shape
    n = w_bf16.shape[1]
    tile = min(ROW_TILE, seq)
    return pl.pallas_call(
        functools.partial(_inproj_kernel, splits=splits, d=d),
        grid=(m // tile,),
        in_specs=[pl.BlockSpec((tile, d), lambda i: (i, 0)),
                  _mod_spec(mod_row0, seq, tile, mods.shape[-1]),
                  pl.BlockSpec((1, d), lambda i: (0, 0)),
                  pl.BlockSpec((d, n), lambda i: (0, 0))],
        out_specs=[pl.BlockSpec((tile, b - a), lambda i: (i, 0)) for a, b in splits],
        out_shape=[jax.ShapeDtypeStruct((m, b - a), F32) for a, b in splits],
        compiler_params=_cparams("arbitrary"),
        name=name,
    )(x, mods, nw.reshape(1, d), w_bf16)


def _outproj_kernel(mix_ref, x_ref, m_ref, w_ref, o_ref, *, d):
    y = jnp.dot(mix_ref[...].astype(BF16), w_ref[...], preferred_element_type=F32)
    o_ref[...] = x_ref[...] + m_ref[0][:, 2 * d:3 * d] * y


def _outproj(mix, x, mods, mod_row0, seq, w_bf16, name):
    m, d = x.shape
    k = mix.shape[1]
    tile = min(ROW_TILE, seq)
    return pl.pallas_call(
        functools.partial(_outproj_kernel, d=d),
        grid=(m // tile,),
        in_specs=[pl.BlockSpec((tile, k), lambda i: (i, 0)),
                  pl.BlockSpec((tile, d), lambda i: (i, 0)),
                  _mod_spec(mod_row0, seq, tile, mods.shape[-1]),
                  pl.BlockSpec((k, d), lambda i: (0, 0))],
        out_specs=pl.BlockSpec((tile, d), lambda i: (i, 0)),
        out_shape=jax.ShapeDtypeStruct((m, d), F32),
        compiler_params=_cparams("arbitrary"),
        name=name,
    )(mix, x, mods, w_bf16)


def _topk_over_rows(s, k, payload=None):
    n = s.shape[0]
    iota = lax.broadcasted_iota(I32, s.shape, 0)
    vals, idxs, pays = [], [], []
    for _ in range(k):
        m = jnp.max(s, axis=0, keepdims=True)
        i = jnp.min(jnp.where(s == m, iota, n), axis=0, keepdims=True)
        hit = iota == i
        vals.append(m)
        idxs.append(i)
        if payload is not None:
            pays.append(jnp.max(jnp.where(hit, payload, -1), axis=0, keepdims=True))
        s = jnp.where(hit, -jnp.inf, s)
    out = (jnp.concatenate(vals, axis=0), jnp.concatenate(idxs, axis=0))
    if payload is not None:
        out += (jnp.concatenate(pays, axis=0),)
    return out


def _peer_route_kernel(x_ref, m_ref, nw_ref, wq_ref, keys_ref, h_ref, e_ref, g_ref, *, d):
    m = m_ref[0]
    h = _norm_mod(x_ref[...], nw_ref[...], m[:, 4 * d:5 * d], m[:, 3 * d:4 * d])
    h_ref[...] = h
    hb = h.astype(BF16)
    nt = (((1,), (1,)), ((), ()))
    for head in range(PEER_HEADS):
        tops = []
        for half in range(2):
            c0 = (head * 2 + half) * PEER_DKEY
            q = jnp.dot(hb, wq_ref[:, c0:c0 + PEER_DKEY], preferred_element_type=F32)
            s = lax.dot_general(keys_ref[head * 2 + half], q.astype(BF16), nt,
                                preferred_element_type=F32)
            tops.append(_topk_over_rows(s, PEER_TOPK))
        (s0, i0), (s1, i1) = tops
        widths = [PEER_TOPK // (a + 1) for a in range(PEER_TOPK)]
        n_pad = -sum(widths) % SUBLANES
        cand_s = jnp.concatenate([s0[a:a + 1] + s1[:w] for a, w in enumerate(widths)]
                                 + [jnp.full((n_pad, s0.shape[1]), -jnp.inf, F32)], axis=0)
        cand_e = jnp.concatenate([i0[a:a + 1] * PEER_NKEYS + i1[:w] for a, w in enumerate(widths)]
                                 + [jnp.zeros((n_pad, s0.shape[1]), I32)], axis=0)
        best_s, _, best_e = _topk_over_rows(cand_s, PEER_TOPK, payload=cand_e)
        p = jnp.exp(best_s - best_s[0:1])
        r0 = head * PEER_TOPK
        e_ref[r0:r0 + PEER_TOPK, :] = best_e
        g_ref[r0:r0 + PEER_TOPK, :] = p / jnp.sum(p, axis=0, keepdims=True)


def _peer_route(x, mods, mod_row0, seq, nw, wq_bf16, keys_bf16):
    m, d = x.shape
    tile = min(ROW_TILE, seq)
    nq = wq_bf16.shape[1]
    return pl.pallas_call(
        functools.partial(_peer_route_kernel, d=d),
        grid=(m // tile,),
        in_specs=[pl.BlockSpec((tile, d), lambda i: (i, 0)),
                  _mod_spec(mod_row0, seq, tile, mods.shape[-1]),
                  pl.BlockSpec((1, d), lambda i: (0, 0)),
                  pl.BlockSpec((d, nq), lambda i: (0, 0)),
                  pl.BlockSpec(keys_bf16.shape, lambda i: (0, 0, 0))],
        out_specs=[pl.BlockSpec((tile, d), lambda i: (i, 0)),
                   pl.BlockSpec((PEER_PAIRS, tile), lambda i: (0, i)),
                   pl.BlockSpec((PEER_PAIRS, tile), lambda i: (0, i))],
        out_shape=[jax.ShapeDtypeStruct((m, d), F32),
                   jax.ShapeDtypeStruct((PEER_PAIRS, m), I32),
                   jax.ShapeDtypeStruct((PEER_PAIRS, m), F32)],
        compiler_params=_cparams("arbitrary"),
        name="peer_route",
    )(x, mods, nw.reshape(1, d), wq_bf16, keys_bf16)


def _peer_gather_kernel(idx_hbm, h_ref, g_ref, x_ref, m_ref, uv_hbm, o_ref,
                        idx_smem, buf, sem_idx, sem, *, d):
    blk = pl.program_id(0)
    n_groups = PEER_BLOCK // SUBLANES

    cp = pltpu.make_async_copy(idx_hbm.at[blk], idx_smem, sem_idx)
    cp.start()
    cp.wait()

    def issue(tok, slot):
        first = tok * PEER_PAIRS

        def body(grp, carry):
            base = pl.multiple_of(grp * SUBLANES, SUBLANES)
            for r in range(SUBLANES):
                e = idx_smem[first + base + r]
                pltpu.make_async_copy(uv_hbm.at[pl.ds(e, 1)], buf.at[slot, pl.ds(base + r, 1)],
                                      sem.at[slot]).start()
            return carry
        lax.fori_loop(0, PEER_PAIRS // SUBLANES, body, 0)

    def wait(slot):
        pltpu.make_async_copy(uv_hbm.at[pl.ds(0, PEER_PAIRS)], buf.at[slot], sem.at[slot]).wait()

    gate2 = m_ref[0][:, 5 * d:6 * d]
    lane = lax.broadcasted_iota(I32, (PEER_PAIRS, PEER_BLOCK), 1)

    issue(0, 0)

    def group(grp, carry):
        base = pl.multiple_of(grp * SUBLANES, SUBLANES)
        h8 = h_ref[pl.ds(base, SUBLANES), :]
        rows = []
        for r in range(SUBLANES):
            tok = base + r
            slot = r % 2
            if r < SUBLANES - 1:
                issue(tok + 1, 1 - slot)
            else:
                @pl.when(grp < n_groups - 1)
                def _():
                    issue(tok + 1, 1 - slot)
            wait(slot)
            act = jnp.sum(buf[slot, :, :d] * h8[r:r + 1, :], axis=1, keepdims=True)
            gate = jnp.sum(jnp.where(lane == tok, g_ref[...], 0.0), axis=1, keepdims=True)
            w = jax.nn.gelu(act) * gate
            rows.append(jnp.sum(buf[slot, :, d:] * w, axis=0, keepdims=True))
        out8 = jnp.concatenate(rows, axis=0)
        o_ref[pl.ds(base, SUBLANES), :] = x_ref[pl.ds(base, SUBLANES), :] + gate2 * out8
        return carry

    lax.fori_loop(0, n_groups, group, 0)


def _peer_gather(idx, h, gates, x, mods, mod_row0, seq, p_uv, first_block):
    m, d = x.shape
    nblk = m // PEER_BLOCK - first_block
    idx3 = idx[:, first_block * PEER_BLOCK:].T.reshape(nblk, PEER_BLOCK * PEER_PAIRS)
    rows = pl.BlockSpec((PEER_BLOCK, d), lambda i: (i + first_block, 0))
    return pl.pallas_call(
        functools.partial(_peer_gather_kernel, d=d),
        grid=(nblk,),
        in_specs=[pl.BlockSpec(memory_space=pl.ANY),
                  rows,
                  pl.BlockSpec((PEER_PAIRS, PEER_BLOCK), lambda i: (0, i + first_block)),
                  rows,
                  _mod_spec(mod_row0, seq, PEER_BLOCK, mods.shape[-1], first_block),
                  pl.BlockSpec(memory_space=pl.ANY)],
        out_specs=pl.BlockSpec((PEER_BLOCK, d), lambda i: (i, 0)),
        out_shape=jax.ShapeDtypeStruct((nblk * PEER_BLOCK, d), F32),
        scratch_shapes=[pltpu.SMEM((PEER_BLOCK * PEER_PAIRS,), I32),
                        pltpu.VMEM((2, PEER_PAIRS, 2 * d), F32),
                        pltpu.SemaphoreType.DMA,
                        pltpu.SemaphoreType.DMA((2,))],
        compiler_params=_cparams("arbitrary"),
        name="peer_gather",
    )(idx3, h, gates, x, mods, p_uv)


def _peer_experts_sc(idx, h, gates, u_rows, v_rows):
    m = idx.shape[0]
    d = h.shape[1]
    info = plsc.get_sparse_core_info()
    n_workers = info.num_cores * info.num_subcores
    per = m // n_workers
    n_chunks = PEER_PAIRS // SC_CHUNK
    n_vec = d // SC_LANES
    mesh = plsc.VectorSubcoreMesh(core_axis_name="c", subcore_axis_name="s")

    @functools.partial(
        pl.kernel, out_type=jax.ShapeDtypeStruct((m, d), F32), mesh=mesh,
        scratch_types=[pltpu.VMEM((2, PEER_PAIRS), I32), pltpu.VMEM((2, d), F32),
                       pltpu.VMEM((2, PEER_PAIRS), F32), pltpu.VMEM((2, d), F32),
                       pltpu.VMEM((2, SC_CHUNK, d), F32), pltpu.VMEM((2, SC_CHUNK, d), F32),
                       pltpu.SemaphoreType.DMA((2,)), pltpu.SemaphoreType.DMA((2,)),
                       pltpu.SemaphoreType.DMA((2,)), pltpu.SemaphoreType.DMA((2,))],
        compiler_params=pltpu.CompilerParams(needs_layout_passes=False),
        name="peer_experts_sc")
    def body(idx_hbm, h_hbm, g_hbm, u_hbm, v_hbm, o_hbm,
             idx_v, x_v, g_v, out_v, ubuf, vbuf, sem_meta, sem_out, sem_u, sem_v):
        wid = lax.axis_index("c") * info.num_subcores + lax.axis_index("s")
        tok0 = wid * per
        lane = lax.iota(I32, SC_LANES)

        def meta_copies(ti, ms):
            t = tok0 + ti
            return (pltpu.make_async_copy(idx_hbm.at[t], idx_v.at[ms], sem_meta.at[ms]),
                    pltpu.make_async_copy(h_hbm.at[t], x_v.at[ms], sem_meta.at[ms]),
                    pltpu.make_async_copy(g_hbm.at[t], g_v.at[ms], sem_meta.at[ms]))

        def gather_copies(ms, c, slot):
            ids = idx_v.at[ms, pl.ds(c * SC_CHUNK, SC_CHUNK)]
            return (pltpu.make_async_copy(u_hbm.at[ids], ubuf.at[slot], sem_u.at[slot]),
                    pltpu.make_async_copy(v_hbm.at[ids], vbuf.at[slot], sem_v.at[slot]))

        def out_copy(ti, ms):
            return pltpu.make_async_copy(out_v.at[ms], o_hbm.at[tok0 + ti], sem_out.at[ms])

        for cp in meta_copies(0, 0):
            cp.start()
        for cp in meta_copies(0, 0):
            cp.wait()
        for cp in gather_copies(0, 0, 0):
            cp.start()

        def token(ti, carry):
            ms = ti % 2
            nxt = 1 - ms

            @pl.when(ti + 1 < per)
            def _():
                for cp in meta_copies(ti + 1, nxt):
                    cp.start()

            @pl.when(ti >= 2)
            def _():
                out_copy(ti - 2, ms).wait()

            def zero(j, c):
                out_v[ms, pl.ds(j * SC_LANES, SC_LANES)] = jnp.zeros((SC_LANES,), F32)
                return c
            lax.fori_loop(0, n_vec, zero, 0)

            for c in range(n_chunks):
                slot = c % 2
                if c + 1 < n_chunks:
                    for cp in gather_copies(ms, c + 1, 1 - slot):
                        cp.start()
                else:
                    @pl.when(ti + 1 < per)
                    def _():
                        for cp in meta_copies(ti + 1, nxt):
                            cp.wait()
                        for cp in gather_copies(nxt, 0, 1 - slot):
                            cp.start()
                cu, cv = gather_copies(ms, c, slot)
                cu.wait()

                def udot(j, accs):
                    xj = x_v[ms, pl.ds(j * SC_LANES, SC_LANES)]
                    return tuple(accs[r] + ubuf[slot, r, pl.ds(j * SC_LANES, SC_LANES)] * xj
                                 for r in range(SC_CHUNK))
                accs = lax.fori_loop(0, n_vec, udot,
                                     tuple(jnp.zeros((SC_LANES,), F32) for _ in range(SC_CHUNK)))
                act = jnp.zeros((SC_LANES,), F32)
                for r in range(SC_CHUNK):
                    act = jnp.where(lane == r, jnp.sum(accs[r]), act)
                y = GELU_C * (act + 0.044715 * (act * act * act))
                w = act / (1.0 + jnp.exp(-2.0 * y)) * g_v[ms, pl.ds(c * SC_CHUNK, SC_CHUNK)]
                ws = [jnp.sum(jnp.where(lane == r, w, 0.0)) for r in range(SC_CHUNK)]
                cv.wait()

                @plsc.parallel_loop(0, n_vec, unroll=2)
                def _(j):
                    parts = [ws[r] * vbuf[slot, r, pl.ds(j * SC_LANES, SC_LANES)] for r in range(SC_CHUNK)]
                    while len(parts) > 1:
                        parts = [parts[i] + parts[i + 1] for i in range(0, len(parts), 2)]
                    plsc.addupdate(out_v.at[ms, pl.ds(j * SC_LANES, SC_LANES)], parts[0])

            out_copy(ti, ms).start()
            return carry

        lax.fori_loop(0, per, token, 0)
        for back in (2, 1):
            if per >= back:
                out_copy(per - back, (per - back) % 2).wait()

    return body(idx, h, gates, u_rows, v_rows)


def _residual_kernel(x_ref, y_ref, m_ref, o_ref, *, d):
    o_ref[...] = x_ref[...] + m_ref[0][:, 5 * d:6 * d] * y_ref[...]


def _residual(x, y, mods, mod_row0, seq):
    m, d = y.shape
    tile = min(ROW_TILE, seq)
    return pl.pallas_call(
        functools.partial(_residual_kernel, d=d),
        grid=(m // tile,),
        in_specs=[pl.BlockSpec((tile, d), lambda i: (i, 0)),
                  pl.BlockSpec((tile, d), lambda i: (i, 0)),
                  _mod_spec(mod_row0, seq, tile, mods.shape[-1])],
        out_specs=pl.BlockSpec((tile, d), lambda i: (i, 0)),
        out_shape=jax.ShapeDtypeStruct((m, d), F32),
        compiler_params=_cparams("arbitrary"),
        name="peer_residual",
    )(x, y, mods)


def _peer(x, mods, mod_row0, seq, layer, nw, wq_bf16, keys_bf16, p_u, p_v, p_uv, sc_share):
    m, d = x.shape
    h, idx, gates = _peer_route(x, mods, mod_row0, seq, nw, wq_bf16, keys_bf16)
    n_blocks = m // PEER_BLOCK
    sc_blocks = (n_blocks * sc_share[0]) // sc_share[1]
    m_sc = sc_blocks * PEER_BLOCK
    idx = idx + layer * p_u.shape[1]
    y_sc = _peer_experts_sc(idx[:, :m_sc].T, h, gates[:, :m_sc].T,
                            p_u.reshape(-1, d), p_v.reshape(-1, d))
    x_sc = _residual(x, y_sc, mods, mod_row0, seq)
    if sc_blocks == n_blocks:
        return x_sc, idx
    x_tc = _peer_gather(idx, h, gates, x, mods, mod_row0, seq, p_uv, sc_blocks)
    return jnp.concatenate([x_sc, x_tc], axis=0), idx


def _head_mean_square(x):
    n = x.shape[1]
    r = lax.broadcasted_iota(I32, (n, n), 0) // HEAD_DIM
    c = lax.broadcasted_iota(I32, (n, n), 1) // HEAD_DIM
    seg = jnp.where(r == c, 1.0 / HEAD_DIM, 0.0).astype(F32)
    return jnp.dot(x * x, seg, precision=HIGHEST, preferred_element_type=F32)


def _swap_rot_halves(x):
    n = x.shape[1]
    quarter = HEAD_DIM // 4
    lane = lax.broadcasted_iota(I32, x.shape, 1)
    lo = (lane % (2 * quarter)) < quarter
    return jnp.where(lo, pltpu.roll(x, n - quarter, axis=1), pltpu.roll(x, quarter, axis=1))


def _qkprep_kernel(q_ref, k_ref, qw_ref, kw_ref, *rest, rope):
    if rope:
        cos_ref, sin_ref, qo_ref, ko_ref = rest
    else:
        qo_ref, ko_ref = rest
    q = q_ref[...]
    k = k_ref[...]
    q = q * lax.rsqrt(_head_mean_square(q) + NORM_EPS) * qw_ref[...]
    k = k * lax.rsqrt(_head_mean_square(k) + NORM_EPS) * kw_ref[...]
    if rope:
        cos = cos_ref[...]
        sin = sin_ref[...]
        cq = jnp.concatenate([cos] * (q.shape[1] // LANES), axis=1)
        sq = jnp.concatenate([sin] * (q.shape[1] // LANES), axis=1)
        q = q * cq + _swap_rot_halves(q) * sq
        k = k * cos + _swap_rot_halves(k) * sin
    qo_ref[...] = q
    ko_ref[...] = k


def _rope_tables(seq):
    axis_dim = HEAD_DIM // 2
    inv_freq = ROPE_THETA ** (-jnp.arange(0, axis_dim, 2, dtype=F32) / axis_dim)
    t = jnp.arange(seq)
    pos = jnp.stack([(t // GRID_W).astype(F32), (t % GRID_W).astype(F32)], axis=1)
    lane = jnp.arange(LANES)
    dd = lane % HEAD_DIM
    ang = pos[:, dd // axis_dim] * inv_freq[dd % (axis_dim // 2)][None, :]
    sign = jnp.where((dd % axis_dim) < axis_dim // 2, -1.0, 1.0).astype(F32)
    return jnp.cos(ang), jnp.sin(ang) * sign[None, :]


def _qkprep(q, k, qw, kw, seq, rope):
    m, nq = q.shape
    nk = k.shape[1]
    tile = min(ROW_TILE, seq)
    qw_row = jnp.tile(qw, nq // HEAD_DIM).reshape(1, nq)
    kw_row = jnp.tile(kw, nk // HEAD_DIM).reshape(1, nk)
    in_specs = [pl.BlockSpec((tile, nq), lambda i: (i, 0)),
                pl.BlockSpec((tile, nk), lambda i: (i, 0)),
                pl.BlockSpec((1, nq), lambda i: (0, 0)),
                pl.BlockSpec((1, nk), lambda i: (0, 0))]
    args = [q, k, qw_row, kw_row]
    if rope:
        cos, sin = _rope_tables(seq)
        per_seq = seq // tile
        in_specs += [pl.BlockSpec((tile, LANES), lambda i: (i % per_seq, 0)),
                     pl.BlockSpec((tile, LANES), lambda i: (i % per_seq, 0))]
        args += [cos, sin]
    return pl.pallas_call(
        functools.partial(_qkprep_kernel, rope=rope),
        grid=(m // tile,),
        in_specs=in_specs,
        out_specs=[pl.BlockSpec((tile, nq), lambda i: (i, 0)),
                   pl.BlockSpec((tile, nk), lambda i: (i, 0))],
        out_shape=[jax.ShapeDtypeStruct((m, nq), F32), jax.ShapeDtypeStruct((m, nk), F32)],
        compiler_params=_cparams("arbitrary"),
        name="qk_prep",
    )(*args)


def _dup_halves(x):
    lane = lax.broadcasted_iota(I32, x.shape, 1)
    sw = pltpu.roll(x, HEAD_DIM, axis=1)
    lo = lane < HEAD_DIM
    return jnp.where(lo, x, sw), jnp.where(lo, sw, x)


def _attend(q, k_all, v_all, sink_ref, mask):
    scale = HEAD_DIM ** -0.5
    nt = (((1,), (1,)), ((), ()))
    kk = [a.astype(BF16) for a in _dup_halves(k_all)]
    vv = [a.astype(BF16) for a in _dup_halves(v_all)]
    lane = lax.broadcasted_iota(I32, (q.shape[0], LANES), 1)
    lo = lane < HEAD_DIM
    tiles = []
    for t in range(q.shape[1] // LANES):
        qt = q[:, t * LANES:(t + 1) * LANES]
        g = (2 * t) // GQA_GROUP
        halves = []
        for hh in range(2):
            head = 2 * t + hh
            qm = jnp.where(lo if hh == 0 else ~lo, qt, 0.0).astype(BF16)
            s = lax.dot_general(qm, kk[g], nt, preferred_element_type=F32) * scale
            if mask is not None:
                s = jnp.where(mask, s, NEG_BIG)
            sink = sink_ref[head]
            mx = jnp.maximum(jnp.max(s, axis=1, keepdims=True), sink)
            p = jnp.exp(s - mx)
            den = jnp.sum(p, axis=1, keepdims=True) + jnp.exp(sink - mx)
            p = (p / den).astype(BF16)
            halves.append(jnp.dot(p, vv[g], preferred_element_type=F32))
        tiles.append(jnp.where(lo, halves[0], halves[1]))
    return jnp.concatenate(tiles, axis=1)


def _ctx_attn_kernel(sink_ref, q_ref, k_ref, v_ref, o_ref):
    o_ref[...] = _attend(q_ref[...], k_ref[...], v_ref[...], sink_ref, None)


def _ctx_attention(q, k, v, sink, seq):
    m, nq = q.shape
    nk = k.shape[1]
    return pl.pallas_call(
        _ctx_attn_kernel,
        grid=(m // seq,),
        in_specs=[pl.BlockSpec(memory_space=pltpu.SMEM),
                  pl.BlockSpec((seq, nq), lambda b: (b, 0)),
                  pl.BlockSpec((seq, nk), lambda b: (b, 0)),
                  pl.BlockSpec((seq, nk), lambda b: (b, 0))],
        out_specs=pl.BlockSpec((seq, nq), lambda b: (b, 0)),
        out_shape=jax.ShapeDtypeStruct((m, nq), F32),
        compiler_params=_cparams("arbitrary"),
        name="ctx_attention",
    )(sink, q, k, v)


def _lat_attn_kernel(sink_ref, q_ref, kc_ref, vc_ref, kp_ref, k0_ref, kn_ref, vp_ref, v0_ref, vn_ref,
                     o_ref, *, seq):
    qb = pl.program_id(1)
    blk = q_ref.shape[0]
    n_ctx = kc_ref.shape[1]
    k_all = jnp.concatenate([kc_ref[0], kp_ref[...], k0_ref[...], kn_ref[...]], axis=0)
    v_all = jnp.concatenate([vc_ref[0], vp_ref[...], v0_ref[...], vn_ref[...]], axis=0)
    tk = k_all.shape[0]
    qpos = qb * blk + lax.broadcasted_iota(I32, (blk, tk), 0)
    col = lax.broadcasted_iota(I32, (blk, tk), 1)
    kpos = (qb - 1) * blk + col - n_ctx
    local_ok = (jnp.abs(qpos - kpos) <= WINDOW) & (kpos >= 0) & (kpos < seq)
    mask = (col < n_ctx) | local_ok
    o_ref[...] = _attend(q_ref[...], k_all, v_all, sink_ref, mask)


def _lat_attention(q, k, v, k_ctx, v_ctx, sink, seq):
    m, nq = q.shape
    nk = k.shape[1]
    blk = WINDOW
    nb = seq // blk
    n_ctx = k_ctx.shape[1]
    last = m // blk - 1

    def kv_spec(shift):
        return pl.BlockSpec((blk, nk), lambda b, i: (jnp.clip(b * nb + i + shift, 0, last), 0))

    ctx_spec = pl.BlockSpec((1, n_ctx, nk), lambda b, i: (b, 0, 0))
    return pl.pallas_call(
        functools.partial(_lat_attn_kernel, seq=seq),
        grid=(m // seq, nb),
        in_specs=[pl.BlockSpec(memory_space=pltpu.SMEM),
                  pl.BlockSpec((blk, nq), lambda b, i: (b * nb + i, 0)),
                  ctx_spec, ctx_spec,
                  kv_spec(-1), kv_spec(0), kv_spec(1),
                  kv_spec(-1), kv_spec(0), kv_spec(1)],
        out_specs=pl.BlockSpec((blk, nq), lambda b, i: (b * nb + i, 0)),
        out_shape=jax.ShapeDtypeStruct((m, nq), F32),
        compiler_params=_cparams("arbitrary", "arbitrary"),
        name="lat_attention",
    )(sink, q, k_ctx, v_ctx, k, k, k, v, v, v)

SSD_BLOCK = 256
SSD_PAIRS = SSD_HEADS // 2
SSD_INNER = SSD_HEADS * SSD_HEAD_DIM
HALO = SUBLANES


def _softplus(x):
    return jnp.maximum(x, 0.0) + jnp.log1p(jnp.exp(-jnp.abs(x)))


def _silu(x):
    return x * jax.nn.sigmoid(x)


def _ssd_decays(dt_raw, bias, a_log):
    n = dt_raw.shape[0]
    dt = _softplus(dt_raw + bias)
    log_a = dt * (-jnp.exp(a_log))
    r = lax.broadcasted_iota(I32, (n, n), 0)
    c = lax.broadcasted_iota(I32, (n, n), 1)
    lower = jnp.where(c <= r, 1.0, 0.0).astype(F32)
    upper = jnp.where(r <= c, 1.0, 0.0).astype(F32)
    cum_col = jnp.dot(lower, log_a, precision=HIGHEST, preferred_element_type=F32)
    dt_row = dt.T
    la_row = log_a.T
    cum_row = jnp.dot(la_row, upper, precision=HIGHEST, preferred_element_type=F32)
    return dt, log_a, cum_col, dt_row, la_row, cum_row


def _ssd_scan_chunk(xs, bmat, cmat, w_of, q_scale_of, k_scale_of, carry_of, s_ref):
    nt = (((1,), (1,)), ((), ()))
    n = xs.shape[0]
    lane = lax.broadcasted_iota(I32, (n, LANES), 1)
    lo = lane < SSD_HEAD_DIM
    lane_s = lax.broadcasted_iota(I32, (D_STATE, LANES), 1)
    lo_s = lane_s < SSD_HEAD_DIM
    b_t = bmat.T
    cb16 = cmat.astype(BF16)
    ys = []
    for pair in range(SSD_PAIRS):
        g = (2 * pair) // (SSD_HEADS // SSD_GROUPS)
        in_g = (lane // D_STATE) == g
        cg = jnp.where(in_g, cmat, 0.0)
        cb = lax.dot_general(cg.astype(BF16), bmat.astype(BF16), nt, preferred_element_type=F32)
        x_pair = xs[:, pair * LANES:(pair + 1) * LANES]
        x16 = x_pair.astype(BF16)
        s_old = s_ref[pair]
        s2 = jnp.concatenate([s_old, s_old], axis=0).astype(BF16)
        bg_t = b_t[g * D_STATE:(g + 1) * D_STATE, :]
        y_h, s_h = [], []
        for hh in range(2):
            h = 2 * pair + hh
            w = (cb * w_of(h)).astype(BF16)
            y = jnp.dot(w, x16, preferred_element_type=F32)
            cq = (cg * q_scale_of(h)).astype(BF16)
            y = y + jnp.dot(cq, s2, preferred_element_type=F32)
            y_h.append(y)
            kt = (bg_t * k_scale_of(h)).astype(BF16)
            s_h.append(carry_of(h) * s_old + jnp.dot(kt, x16, preferred_element_type=F32))
        ys.append(jnp.where(lo, y_h[0], y_h[1]))
        s_ref[pair] = jnp.where(lo_s, s_h[0], s_h[1])
    return jnp.concatenate(ys, axis=1)


def _ssd_fwd_kernel(x_ref, xp_ref, xn_ref, dt_ref, s0_ref, cw_ref, cb_ref, bias_ref, alog_ref,
                    y_ref, xc_ref, sfin_ref, s_ref):
    c = pl.program_id(1)
    nc = pl.num_programs(1)
    n = x_ref.shape[0]

    @pl.when(c == 0)
    def _():
        s_ref[...] = s0_ref[0]

    prev = jnp.where(c > 0, xp_ref[...], 0.0)
    nxt = jnp.where(c < nc - 1, xn_ref[...], 0.0)
    xe = jnp.concatenate([prev, x_ref[...], nxt], axis=0)
    pad = (CONV_K - 1) // 2
    acc = cb_ref[...] + cw_ref[0:1, :] * xe[HALO - pad:HALO - pad + n, :]
    for k in range(1, CONV_K):
        acc = acc + cw_ref[k:k + 1, :] * xe[HALO - pad + k:HALO - pad + k + n, :]
    xc = _silu(acc)
    xc_ref[...] = xc
    xs = xc[:, :SSD_INNER]
    bmat = xc[:, SSD_INNER:SSD_INNER + LANES]
    cmat = xc[:, SSD_INNER + LANES:SSD_INNER + 2 * LANES]

    dt, log_a, cum_col, dt_row, la_row, cum_row = _ssd_decays(dt_ref[...], bias_ref[...], alog_ref[...])
    r = lax.broadcasted_iota(I32, (n, n), 0)
    cc = lax.broadcasted_iota(I32, (n, n), 1)
    causal = cc <= r
    last_col = cum_col[n - 1:n, :]

    def w_of(h):
        seg = cum_col[:, h:h + 1] - cum_row[h:h + 1, :]
        return jnp.exp(jnp.where(causal, seg, NEG_BIG)) * dt_row[h:h + 1, :]

    def q_scale_of(h):
        return jnp.exp(cum_col[:, h:h + 1])

    def k_scale_of(h):
        return dt_row[h:h + 1, :] * jnp.exp(cum_row[h:h + 1, n - 1:n] - cum_row[h:h + 1, :])

    def carry_of(h):
        return jnp.exp(last_col[:, h:h + 1])

    y_ref[...] = _ssd_scan_chunk(xs, bmat, cmat, w_of, q_scale_of, k_scale_of, carry_of, s_ref)

    @pl.when(c == nc - 1)
    def _():
        sfin_ref[0] = s_ref[...]


def _ssd_bwd_kernel(xc_ref, dt_ref, yf_ref, z_ref, s0_ref, bias_ref, alog_ref, dskip_ref, nw_ref,
                    y_ref, sfin_ref, s_ref):
    c = pl.program_id(1)
    nc = pl.num_programs(1)
    n = xc_ref.shape[0]

    @pl.when(c == 0)
    def _():
        s_ref[...] = s0_ref[0]

    xc = xc_ref[...]
    xs = xc[:, :SSD_INNER]
    bmat = xc[:, SSD_INNER:SSD_INNER + LANES]
    cmat = xc[:, SSD_INNER + LANES:SSD_INNER + 2 * LANES]
    dt, log_a, cum_col, dt_row, la_row, cum_row = _ssd_decays(dt_ref[...], bias_ref[...], alog_ref[...])
    ex_col = cum_col - log_a
    ex_row = cum_row - la_row
    r = lax.broadcasted_iota(I32, (n, n), 0)
    cc = lax.broadcasted_iota(I32, (n, n), 1)
    anti = cc >= r
    tot_col = cum_col[n - 1:n, :]
    off = SSD_HEADS

    def w_of(h):
        j = off + h
        seg = ex_row[j:j + 1, :] - ex_col[:, j:j + 1]
        return jnp.exp(jnp.where(anti, seg, NEG_BIG)) * dt_row[j:j + 1, :]

    def q_scale_of(h):
        j = off + h
        return jnp.exp(tot_col[:, j:j + 1] - ex_col[:, j:j + 1])

    def k_scale_of(h):
        j = off + h
        return dt_row[j:j + 1, :] * jnp.exp(ex_row[j:j + 1, :])

    def carry_of(h):
        j = off + h
        return jnp.exp(tot_col[:, j:j + 1])

    y_b = _ssd_scan_chunk(xs, bmat, cmat, w_of, q_scale_of, k_scale_of, carry_of, s_ref)
    y = yf_ref[...] + y_b + dskip_ref[...] * xs
    y = y * _silu(z_ref[...])
    ms = jnp.mean(y * y, axis=-1, keepdims=True)
    y_ref[...] = y * lax.rsqrt(ms + NORM_EPS) * nw_ref[...]

    @pl.when(c == nc - 1)
    def _():
        sfin_ref[0] = s_ref[...]


def _pair_states(s):
    b, h, n, p = s.shape
    return s.reshape(b, h // 2, 2, n, p).transpose(0, 1, 3, 2, 4).reshape(b, h // 2, n, 2 * p)


def _unpair_states(s):
    b, hp, n, p2 = s.shape
    return s.reshape(b, hp, n, 2, p2 // 2).transpose(0, 1, 3, 2, 4).reshape(b, hp * 2, n, p2 // 2)


def _ssd(xbc, dt, z, s0_f, s0_b, conv_w, conv_b, dt_bias, a_log, d_skip, ssd_norm, seq):
    m, nx = xbc.shape
    nb = m // seq
    blk = min(SSD_BLOCK, seq)
    nc = seq // blk
    hb = blk // HALO
    n_halo = m // HALO
    pad16 = lambda a: jnp.pad(a.reshape(1, -1), ((0, 0), (0, LANES - a.size)))
    bias = pad16(dt_bias)
    alog = pad16(a_log)
    state_spec = pl.BlockSpec((1, SSD_PAIRS, D_STATE, LANES), lambda b, c: (b, 0, 0, 0))
    state_shape = jax.ShapeDtypeStruct((nb, SSD_PAIRS, D_STATE, LANES), F32)
    row = lambda width: pl.BlockSpec((1, width), lambda b, c: (0, 0))

    def fwd_rows(width):
        return pl.BlockSpec((blk, width), lambda b, c: (b * nc + c, 0))

    def bwd_rows(width):
        return pl.BlockSpec((blk, width), lambda b, c: (b * nc + nc - 1 - c, 0))

    y_f, xc, s_f = pl.pallas_call(
        _ssd_fwd_kernel,
        grid=(nb, nc),
        in_specs=[fwd_rows(nx),
                  pl.BlockSpec((HALO, nx), lambda b, c: (jnp.maximum((b * nc + c) * hb - 1, 0), 0)),
                  pl.BlockSpec((HALO, nx), lambda b, c: (jnp.minimum((b * nc + c + 1) * hb, n_halo - 1), 0)),
                  fwd_rows(LANES), state_spec,
                  pl.BlockSpec((CONV_K, nx), lambda b, c: (0, 0)), row(nx), row(LANES), row(LANES)],
        out_specs=[fwd_rows(SSD_INNER), fwd_rows(nx), state_spec],
        out_shape=[jax.ShapeDtypeStruct((m, SSD_INNER), F32), jax.ShapeDtypeStruct((m, nx), F32), state_shape],
        scratch_shapes=[pltpu.VMEM((SSD_PAIRS, D_STATE, LANES), F32)],
        compiler_params=_cparams("arbitrary", "arbitrary"),
        name="ssd_forward",
    )(xbc, xbc, xbc, dt, _pair_states(s0_f), conv_w, conv_b.reshape(1, nx), bias, alog)

    dskip = jnp.repeat(d_skip, SSD_HEAD_DIM).reshape(1, SSD_INNER)
    y, s_b = pl.pallas_call(
        _ssd_bwd_kernel,
        grid=(nb, nc),
        in_specs=[bwd_rows(nx), bwd_rows(LANES), bwd_rows(SSD_INNER), bwd_rows(SSD_INNER), state_spec,
                  row(LANES), row(LANES), row(SSD_INNER), row(SSD_INNER)],
        out_specs=[bwd_rows(SSD_INNER), state_spec],
        out_shape=[jax.ShapeDtypeStruct((m, SSD_INNER), F32), state_shape],
        scratch_shapes=[pltpu.VMEM((SSD_PAIRS, D_STATE, LANES), F32)],
        compiler_params=_cparams("arbitrary", "arbitrary"),
        name="ssd_backward",
    )(xc, dt, y_f, z, _pair_states(s0_b), bias, alog, dskip, ssd_norm.reshape(1, SSD_INNER))
    return y, _unpair_states(s_f), _unpair_states(s_b)

def _hgrn_kernel(q_ref, ff_ref, fb_ref, i_ref, g_ref, lb_ref, s0_ref, nw_ref, o_ref, sfin_ref,
                 sf_ref, sb_ref, ob_ref, *, layer):
    t_len = q_ref.shape[0]
    n = HGRN_CHUNK
    n_chunks = t_len // n
    tn = (((0,), (0,)), ((), ()))
    nt = (((1,), (1,)), ((), ()))

    lbp = lb_ref[...]
    e = jnp.exp(lbp - jnp.max(lbp, axis=0, keepdims=True))
    sm = e / jnp.sum(e, axis=0, keepdims=True)
    lb = sm[0] * 0.0
    for j in range(1, layer + 1):
        lb = lb + sm[j]

    r = lax.broadcasted_iota(I32, (n, n), 0)
    c = lax.broadcasted_iota(I32, (n, n), 1)
    lower = jnp.where(c <= r, 1.0, 0.0).astype(F32)
    srow = lax.broadcasted_iota(I32, (n, HGRN_DK), 0)
    qscale = HGRN_DK ** -0.5

    def chunk(row0, f_ref, lb_d, reverse, s_ref):
        q = _silu(q_ref[pl.ds(row0, n), :]) * qscale
        f = f_ref[pl.ds(row0, n), :]
        v = i_ref[pl.ds(row0, n), :]
        k = (1.0 - lb_d) * jax.nn.sigmoid(-f)
        lf = jnp.log(lb_d + (1.0 - lb_d) * jax.nn.sigmoid(f))
        cum = jnp.dot(lower, lf, precision=HIGHEST, preferred_element_type=F32)
        tot = cum[n - 1:n, :]
        if reverse:
            cum = cum - lf
        rows = []
        for t in range(n):
            tile0 = (t // SUBLANES) * SUBLANES
            lo, hi = (tile0, n) if reverse else (0, tile0 + SUBLANES)
            cum_s = cum[lo:hi]
            if reverse:
                seg = jnp.where(srow[lo:hi] >= t, cum_s - cum[t:t + 1, :], NEG_BIG)
            else:
                seg = jnp.where(srow[lo:hi] <= t, cum[t:t + 1, :] - cum_s, NEG_BIG)
            a = q[t:t + 1, :] * k[lo:hi] * jnp.exp(seg)
            sc = jnp.sum(a, axis=1, keepdims=True)
            rows.append(jnp.sum(sc * v[lo:hi], axis=0, keepdims=True))
        o = jnp.concatenate(rows, axis=0)
        s_old = s_ref[...]
        if reverse:
            q_in = q * jnp.exp(tot - cum)
            k_out = k * jnp.exp(cum)
        else:
            q_in = q * jnp.exp(cum)
            k_out = k * jnp.exp(tot - cum)
        o = o + lax.dot_general(q_in.astype(BF16), s_old.astype(BF16), nt, preferred_element_type=F32)
        s_ref[...] = jnp.exp(tot) * s_old + lax.dot_general(
            v.astype(BF16), k_out.astype(BF16), tn, preferred_element_type=F32)
        return o

    sf_ref[...] = s0_ref[0, 0, 0].T
    sb_ref[...] = s0_ref[0, 1, 0].T

    def body(ci, carry):
        row_f = pl.multiple_of(ci * n, n)
        row_b = pl.multiple_of((n_chunks - 1 - ci) * n, n)
        o_ref[pl.ds(row_f, n), :] = chunk(row_f, ff_ref, lb[0:1, :], False, sf_ref)
        ob_ref[pl.ds(row_b, n), :] = chunk(row_b, fb_ref, lb[1:2, :], True, sb_ref)
        return carry

    lax.fori_loop(0, n_chunks, body, 0)
    sfin_ref[0, 0, 0] = sf_ref[...].T
    sfin_ref[0, 1, 0] = sb_ref[...].T

    nw = nw_ref[...]
    blk = min(t_len, ROW_TILE)

    def finish(bi, carry):
        row0 = pl.multiple_of(bi * blk, blk)
        o = o_ref[pl.ds(row0, blk), :] + ob_ref[pl.ds(row0, blk), :]
        ms = jnp.mean(o * o, axis=-1, keepdims=True)
        o = o * lax.rsqrt(ms + NORM_EPS) * nw
        o_ref[pl.ds(row0, blk), :] = o * _silu(g_ref[pl.ds(row0, blk), :])
        return carry

    lax.fori_loop(0, t_len // blk, finish, 0)


def _hgrn(q, f_fw, f_bw, iv, g, o_lb, state0, g_norm, seq, layer):
    m, width = q.shape
    nb = m // seq
    dv = width // HGRN_HEADS
    col = pl.BlockSpec((seq, dv), lambda b, h: (b, h))
    state_spec = pl.BlockSpec((1, 2, 1, HGRN_DK, dv), lambda b, h: (b, 0, h, 0, 0))
    return pl.pallas_call(
        functools.partial(_hgrn_kernel, layer=layer),
        grid=(nb, HGRN_HEADS),
        in_specs=[col, col, col, col, col,
                  pl.BlockSpec((o_lb.shape[0], 2, HGRN_DK), lambda b, h: (0, 0, h)),
                  state_spec,
                  pl.BlockSpec((1, dv), lambda b, h: (0, 0))],
        out_specs=[col, state_spec],
        out_shape=[jax.ShapeDtypeStruct((m, width), F32),
                   jax.ShapeDtypeStruct((nb, 2, HGRN_HEADS, HGRN_DK, dv), F32)],
        scratch_shapes=[pltpu.VMEM((dv, HGRN_DK), F32), pltpu.VMEM((dv, HGRN_DK), F32),
                        pltpu.VMEM((seq, dv), F32)],
        compiler_params=_cparams("arbitrary", "arbitrary"),
        name="hgrn2",
    )(q, f_fw, f_bw, iv, g, o_lb, state0, g_norm.reshape(1, dv))

EVEN_SPLITS = ((0, 512), (512, 640), (640, 768), (768, 1280), (1280, 2048), (2048, 2176))
HGRN_SPLITS = tuple((i * 1024, (i + 1) * 1024) for i in range(5))


def _even_weight(w):
    main = EVEN_SPLITS[-1][0]
    return jnp.pad(w, ((0, 0), (0, LANES - (w.shape[1] - main)))).astype(BF16)


def _run_trunk(x3, mods, mod_row0, P, cache, routed=None, after=None):
    nb, seq, d = x3.shape
    x = x3.reshape(nb * seq, d)
    if after is not None:
        x, _ = lax.optimization_barrier((x, after))
    depth = P['norm_mix'].shape[0]
    ks, vs, ssd_states, hgrn_states = [], [], [], []
    for l in range(depth):
        j = l // 2
        row0 = (l * SUBLANES + mod_row0, 0 if cache is None else 1)
        if l % 2 == 0:
            q, k, v, z, xbc, dt = _inproj(x, mods, row0, seq, P['norm_mix'][l], P['e_w_in'][j],
                                          EVEN_SPLITS, "even_in_proj")
            q, k = _qkprep(q, k, P['e_q_norm'][j], P['e_k_norm'][j], seq, rope=cache is not None)
            if cache is None:
                s0_f = jnp.zeros((nb, SSD_HEADS, D_STATE, SSD_HEAD_DIM), F32)
                s0_b = s0_f
                o_attn = _ctx_attention(q, k, v, P['e_sink'][j], seq)
            else:
                s0_f, s0_b = cache[2][:, j, 0], cache[2][:, j, 1]
                n_ctx = cache[0].shape[2]
                o_attn = _lat_attention(q, k, v, cache[0][:, j].reshape(nb, n_ctx, -1),
                                        cache[1][:, j].reshape(nb, n_ctx, -1), P['e_sink'][j], seq)
            y, s_f, s_b = _ssd(xbc, dt, z, s0_f, s0_b, P['e_conv_w'][j], P['e_conv_b'][j],
                               P['e_dt_bias'][j], P['e_a_log'][j], P['e_d_skip'][j], P['e_ssd_norm'][j], seq)
            if cache is None:
                ks.append(k.reshape(nb, seq, N_KV_HEADS, HEAD_DIM))
                vs.append(v.reshape(nb, seq, N_KV_HEADS, HEAD_DIM))
                ssd_states.append(jnp.stack([s_f, s_b], axis=1))
            mix = jnp.concatenate([o_attn, y], axis=1)
            x = _outproj(mix, x, mods, row0, seq, P['e_w_out'][j], "even_out_proj")
        else:
            q, f_fw, f_bw, iv, g = _inproj(x, mods, row0, seq, P['norm_mix'][l], P['o_w_in'][j],
                                           HGRN_SPLITS, "odd_in_proj")
            if cache is None:
                s0 = jnp.zeros((nb, 2, HGRN_HEADS, HGRN_DK, d // HGRN_HEADS), F32)
            else:
                s0 = cache[3][:, j]
            o, s_new = _hgrn(q, f_fw, f_bw, iv, g, P['o_lb'], s0, P['o_g_norm'][j], seq, j)
            if cache is None:
                hgrn_states.append(s_new)
            x = _outproj(o, x, mods, row0, seq, P['o_w_out'][j], "odd_out_proj")
        x, idx = _peer(x, mods, row0, seq, l, P['norm_ffn'][l], P['p_w_q'][l], P['p_sub_keys'][l],
                       P['p_u'], P['p_v'], P['p_uv'],
                       SC_SHARE_CONTEXT if cache is None else SC_SHARE_LATENT)
        if routed is not None and l == 0:
            routed.append(idx)
    y = x.reshape(nb, seq, d)
    if cache is not None:
        return y, None
    return y, (jnp.stack(ks, axis=1), jnp.stack(vs, axis=1),
               jnp.stack(ssd_states, axis=1), jnp.stack(hgrn_states, axis=1))


def kernel(x_prompt, x_sample, cache_k, cache_v, state_ssd, state_hgrn, c, c_ctx, w_ada, b_ada, norm_mix, norm_ffn, e_w_in, e_q_norm, e_k_norm, e_sink, e_conv_w, e_conv_b, e_dt_bias, e_a_log, e_d_skip, e_ssd_norm, e_w_out, o_w_in, o_lb, o_g_norm, o_w_out, p_w_q, p_sub_keys, p_u, p_v):
    depth, d, d6 = w_ada.shape
    b_lat = x_sample.shape[0]
    cond_rows = jnp.concatenate([c_ctx[None, :], c, jnp.zeros((SUBLANES - 1 - b_lat, d), F32)], axis=0)
    mods = _modulation(cond_rows, w_ada, b_ada).reshape(depth * SUBLANES, 1, d6)
    P = {
        'norm_mix': norm_mix, 'norm_ffn': norm_ffn,
        'e_w_in': jnp.stack([_even_weight(w) for w in e_w_in]), 'e_q_norm': e_q_norm, 'e_k_norm': e_k_norm,
        'e_sink': e_sink, 'e_conv_w': e_conv_w, 'e_conv_b': e_conv_b, 'e_dt_bias': e_dt_bias,
        'e_a_log': e_a_log, 'e_d_skip': e_d_skip, 'e_ssd_norm': e_ssd_norm,
        'e_w_out': e_w_out.astype(BF16),
        'o_w_in': o_w_in.astype(BF16), 'o_lb': o_lb, 'o_g_norm': o_g_norm, 'o_w_out': o_w_out.astype(BF16),
        'p_w_q': p_w_q.astype(BF16),
        'p_sub_keys': p_sub_keys.astype(BF16).reshape(depth, PEER_HEADS * 2, PEER_NKEYS, PEER_DKEY),
        'p_u': p_u, 'p_v': p_v,
        'p_uv': jnp.concatenate([p_u.reshape(-1, d), p_v.reshape(-1, d)], axis=1),
    }
    routed = []
    y_prompt, new_state = _run_trunk(x_prompt, mods, 0, P, None, routed=routed)
    y_sample, _ = _run_trunk(x_sample, mods, 1, P, (cache_k, cache_v, state_ssd, state_hgrn),
                             after=routed[0])
    return (y_prompt, y_sample) + new_state
```

```python
import functools
import math

import jax
import jax.numpy as jnp
from jax import lax
from jax.experimental import pallas as pl
from jax.experimental.pallas import tpu as pltpu
from jax.experimental.pallas import tpu_sc as plsc

F32 = jnp.float32
BF16 = jnp.bfloat16
I32 = jnp.int32
HIGHEST = lax.Precision.HIGHEST

NORM_EPS = 1e-6
NEG_BIG = -1e30
LANES = 128
SUBLANES = 8
VMEM_LIMIT = 48 * 1024 * 1024

GRID_W = 64
HEAD_DIM = 64
N_Q_HEADS = 8
N_KV_HEADS = 2
GQA_GROUP = 4
WINDOW = 128
ROPE_THETA = 10000.0
SSD_HEADS = 8
SSD_HEAD_DIM = 64
SSD_GROUPS = 2
D_STATE = 64
CONV_K = 5
HGRN_HEADS = 8
HGRN_DK = 128
HGRN_CHUNK = 32
PEER_HEADS = 8
PEER_NKEYS = 128
PEER_TOPK = 16
PEER_DKEY = 128
PEER_PAIRS = PEER_HEADS * PEER_TOPK

ROW_TILE = 256
PEER_BLOCK = 128
SC_LANES = 16
SC_CHUNK = 16
SC_SHARE_CONTEXT = (3, 4)
SC_SHARE_LATENT = (3, 4)
GELU_C = math.sqrt(2.0 / math.pi)


def _cparams(*sem):
    return pltpu.CompilerParams(dimension_semantics=sem, vmem_limit_bytes=VMEM_LIMIT)


def _norm_mod(x, nw, scale, shift):
    ms = jnp.mean(x * x, axis=-1, keepdims=True)
    return (x * lax.rsqrt(ms + NORM_EPS)) * nw * (1.0 + scale) + shift


def _mod_kernel(c_ref, w_ref, b_ref, o_ref):
    c = c_ref[...]
    s = c * jax.nn.sigmoid(c)
    o_ref[0] = jnp.dot(s, w_ref[0], precision=HIGHEST, preferred_element_type=F32) + b_ref[0]


def _modulation(cond_rows, w_ada, b_ada):
    depth, d, n = w_ada.shape
    rows = cond_rows.shape[0]
    return pl.pallas_call(
        _mod_kernel,
        grid=(depth, n // d),
        in_specs=[pl.BlockSpec((rows, d), lambda l, j: (0, 0)),
                  pl.BlockSpec((1, d, d), lambda l, j: (l, 0, j)),
                  pl.BlockSpec((1, 1, d), lambda l, j: (l, 0, j))],
        out_specs=pl.BlockSpec((1, rows, d), lambda l, j: (l, 0, j)),
        out_shape=jax.ShapeDtypeStruct((depth, rows, n), F32),
        compiler_params=_cparams("arbitrary", "arbitrary"),
        name="modulation",
    )(cond_rows, w_ada, b_ada.reshape(depth, 1, n))


def _mod_spec(mod_row0, seq, tile, d6, first_tile=0):
    row0, per_batch = mod_row0
    return pl.BlockSpec((1, 1, d6),
                        lambda i: (row0 + per_batch * (((i + first_tile) * tile) // seq), 0, 0))


def _inproj_kernel(x_ref, m_ref, nw_ref, w_ref, *o_refs, splits, d):
    m = m_ref[0]
    h = _norm_mod(x_ref[...], nw_ref[...], m[:, d:2 * d], m[:, 0:d]).astype(BF16)
    for o_ref, (a, b) in zip(o_refs, splits):
        o_ref[...] = jnp.dot(h, w_ref[:, a:b], preferred_element_type=F32)


def _inproj(x, mods, mod_row0, seq, nw, w_bf16, splits, name):
    m, d = x.shape
    n = w_bf16.shape[1]
    tile = min(ROW_TILE, seq)
    return pl.pallas_call(
        functools.partial(_inproj_kernel, splits=splits, d=d),
        grid=(m // tile,),
        in_specs=[pl.BlockSpec((tile, d), lambda i: (i, 0)),
                  _mod_spec(mod_row0, seq, tile, mods.shape[-1]),
                  pl.BlockSpec((1, d), lambda i: (0, 0)),
                  pl.BlockSpec((d, n), lambda i: (0, 0))],
        out_specs=[pl.BlockSpec((tile, b - a), lambda i: (i, 0)) for a, b in splits],
        out_shape=[jax.ShapeDtypeStruct((m, b - a), F32) for a, b in splits],
        compiler_params=_cparams("arbitrary"),
        name=name,
    )(x, mods, nw.reshape(1, d), w_bf16)


def _outproj_kernel(mix_ref, x_ref, m_ref, w_ref, o_ref, *, d):
    y = jnp.dot(mix_ref[...].astype(BF16), w_ref[...], preferred_element_type=F32)
    o_ref[...] = x_ref[...] + m_ref[0][:, 2 * d:3 * d] * y


def _outproj(mix, x, mods, mod_row0, seq, w_bf16, name):
    m, d = x.shape
    k = mix.shape[1]
    tile = min(ROW_TILE, seq)
    return pl.pallas_call(
        functools.partial(_outproj_kernel, d=d),
        grid=(m // tile,),
        in_specs=[pl.BlockSpec((tile, k), lambda i: (i, 0)),
                  pl.BlockSpec((tile, d), lambda i: (i, 0)),
                  _mod_spec(mod_row0, seq, tile, mods.shape[-1]),
                  pl.BlockSpec((k, d), lambda i: (0, 0))],
        out_specs=pl.BlockSpec((tile, d), lambda i: (i, 0)),
        out_shape=jax.ShapeDtypeStruct((m, d), F32),
        compiler_params=_cparams("arbitrary"),
        name=name,
    )(mix, x, mods, w_bf16)


def _topk_over_rows(s, k, payload=None):
    n = s.shape[0]
    iota = lax.broadcasted_iota(I32, s.shape, 0)
    vals, idxs, pays = [], [], []
    for _ in range(k):
        m = jnp.max(s, axis=0, keepdims=True)
        i = jnp.min(jnp.where(s == m, iota, n), axis=0, keepdims=True)
        hit = iota == i
        vals.append(m)
        idxs.append(i)
        if payload is not None:
            pays.append(jnp.max(jnp.where(hit, payload, -1), axis=0, keepdims=True))
        s = jnp.where(hit, -jnp.inf, s)
    out = (jnp.concatenate(vals, axis=0), jnp.concatenate(idxs, axis=0))
    if payload is not None:
        out += (jnp.concatenate(pays, axis=0),)
    return out


def _peer_route_kernel(x_ref, m_ref, nw_ref, wq_ref, keys_ref, h_ref, e_ref, g_ref, *, d):
    m = m_ref[0]
    h = _norm_mod(x_ref[...], nw_ref[...], m[:, 4 * d:5 * d], m[:, 3 * d:4 * d])
    h_ref[...] = h
    hb = h.astype(BF16)
    nt = (((1,), (1,)), ((), ()))
    for head in range(PEER_HEADS):
        tops = []
        for half in range(2):
            c0 = (head * 2 + half) * PEER_DKEY
            q = jnp.dot(hb, wq_ref[:, c0:c0 + PEER_DKEY], preferred_element_type=F32)
            s = lax.dot_general(keys_ref[head * 2 + half], q.astype(BF16), nt,
                                preferred_element_type=F32)
            tops.append(_topk_over_rows(s, PEER_TOPK))
        (s0, i0), (s1, i1) = tops
        widths = [PEER_TOPK // (a + 1) for a in range(PEER_TOPK)]
        n_pad = -sum(widths) % SUBLANES
        cand_s = jnp.concatenate([s0[a:a + 1] + s1[:w] for a, w in enumerate(widths)]
                                 + [jnp.full((n_pad, s0.shape[1]), -jnp.inf, F32)], axis=0)
        cand_e = jnp.concatenate([i0[a:a + 1] * PEER_NKEYS + i1[:w] for a, w in enumerate(widths)]
                                 + [jnp.zeros((n_pad, s0.shape[1]), I32)], axis=0)
        best_s, _, best_e = _topk_over_rows(cand_s, PEER_TOPK, payload=cand_e)
        p = jnp.exp(best_s - best_s[0:1])
        r0 = head * PEER_TOPK
        e_ref[r0:r0 + PEER_TOPK, :] = best_e
        g_ref[r0:r0 + PEER_TOPK, :] = p / jnp.sum(p, axis=0, keepdims=True)


def _peer_route(x, mods, mod_row0, seq, nw, wq_bf16, keys_bf16):
    m, d = x.shape
    tile = min(ROW_TILE, seq)
    nq = wq_bf16.shape[1]
    return pl.pallas_call(
        functools.partial(_peer_route_kernel, d=d),
        grid=(m // tile,),
        in_specs=[pl.BlockSpec((tile, d), lambda i: (i, 0)),
                  _mod_spec(mod_row0, seq, tile, mods.shape[-1]),
                  pl.BlockSpec((1, d), lambda i: (0, 0)),
                  pl.BlockSpec((d, nq), lambda i: (0, 0)),
                  pl.BlockSpec(keys_bf16.shape, lambda i: (0, 0, 0))],
        out_specs=[pl.BlockSpec((tile, d), lambda i: (i, 0)),
                   pl.BlockSpec((PEER_PAIRS, tile), lambda i: (0, i)),
                   pl.BlockSpec((PEER_PAIRS, tile), lambda i: (0, i))],
        out_shape=[jax.ShapeDtypeStruct((m, d), F32),
                   jax.ShapeDtypeStruct((PEER_PAIRS, m), I32),
                   jax.ShapeDtypeStruct((PEER_PAIRS, m), F32)],
        compiler_params=_cparams("arbitrary"),
        name="peer_route",
    )(x, mods, nw.reshape(1, d), wq_bf16, keys_bf16)


def _peer_gather_kernel(idx_hbm, h_ref, g_ref, x_ref, m_ref, u_hbm, v_hbm, o_ref,
                        idx_smem, ubuf, vbuf, sem_idx, sem_u, sem_v, *, layer, d):
    blk = pl.program_id(0)
    n_groups = PEER_BLOCK // SUBLANES

    cp = pltpu.make_async_copy(idx_hbm.at[blk], idx_smem, sem_idx)
    cp.start()
    cp.wait()

    def issue(tok, slot):
        for pair in range(PEER_PAIRS):
            e = idx_smem[tok, pair]
            pltpu.make_async_copy(u_hbm.at[layer, pl.ds(e, 1)], ubuf.at[slot, pl.ds(pair, 1)],
                                  sem_u.at[slot]).start(priority=0)
            pltpu.make_async_copy(v_hbm.at[layer, pl.ds(e, 1)], vbuf.at[slot, pl.ds(pair, 1)],
                                  sem_v.at[slot]).start(priority=1)

    def wait(slot):
        pltpu.make_async_copy(u_hbm.at[layer, pl.ds(0, PEER_PAIRS)], ubuf.at[slot], sem_u.at[slot]).wait()
        pltpu.make_async_copy(v_hbm.at[layer, pl.ds(0, PEER_PAIRS)], vbuf.at[slot], sem_v.at[slot]).wait()

    gate2 = m_ref[0][:, 5 * d:6 * d]
    lane = lax.broadcasted_iota(I32, (PEER_PAIRS, PEER_BLOCK), 1)

    issue(0, 0)

    def group(grp, carry):
        base = pl.multiple_of(grp * SUBLANES, SUBLANES)
        h8 = h_ref[pl.ds(base, SUBLANES), :]
        rows = []
        for r in range(SUBLANES):
            tok = base + r
            slot = r % 2
            if r < SUBLANES - 1:
                issue(tok + 1, 1 - slot)
            else:
                @pl.when(grp < n_groups - 1)
                def _():
                    issue(tok + 1, 1 - slot)
            wait(slot)
            act = jnp.sum(ubuf[slot] * h8[r:r + 1, :], axis=1, keepdims=True)
            gate = jnp.sum(jnp.where(lane == tok, g_ref[...], 0.0), axis=1, keepdims=True)
            w = jax.nn.gelu(act) * gate
            rows.append(jnp.sum(vbuf[slot] * w, axis=0, keepdims=True))
        out8 = jnp.concatenate(rows, axis=0)
        o_ref[pl.ds(base, SUBLANES), :] = x_ref[pl.ds(base, SUBLANES), :] + gate2 * out8
        return carry

    lax.fori_loop(0, n_groups, group, 0)


def _peer_gather(idx, h, gates, x, mods, mod_row0, seq, p_u, p_v, layer, first_block):
    m, d = x.shape
    nblk = m // PEER_BLOCK - first_block
    idx3 = idx[:, first_block * PEER_BLOCK:].T.reshape(nblk, PEER_BLOCK, PEER_PAIRS)
    rows = pl.BlockSpec((PEER_BLOCK, d), lambda i: (i + first_block, 0))
    return pl.pallas_call(
        functools.partial(_peer_gather_kernel, layer=layer, d=d),
        grid=(nblk,),
        in_specs=[pl.BlockSpec(memory_space=pl.ANY),
                  rows,
                  pl.BlockSpec((PEER_PAIRS, PEER_BLOCK), lambda i: (0, i + first_block)),
                  rows,
                  _mod_spec(mod_row0, seq, PEER_BLOCK, mods.shape[-1], first_block),
                  pl.BlockSpec(memory_space=pl.ANY),
                  pl.BlockSpec(memory_space=pl.ANY)],
        out_specs=pl.BlockSpec((PEER_BLOCK, d), lambda i: (i, 0)),
        out_shape=jax.ShapeDtypeStruct((nblk * PEER_BLOCK, d), F32),
        scratch_shapes=[pltpu.SMEM((PEER_BLOCK, PEER_PAIRS), I32),
                        pltpu.VMEM((2, PEER_PAIRS, d), F32),
                        pltpu.VMEM((2, PEER_PAIRS, d), F32),
                        pltpu.SemaphoreType.DMA,
                        pltpu.SemaphoreType.DMA((2,)),
                        pltpu.SemaphoreType.DMA((2,))],
        compiler_params=_cparams("arbitrary"),
        name="peer_gather",
    )(idx3, h, gates, x, mods, p_u, p_v)


def _peer_experts_sc(idx, h, gates, u_rows, v_rows):
    m = idx.shape[0]
    d = h.shape[1]
    info = plsc.get_sparse_core_info()
    n_workers = info.num_cores * info.num_subcores
    per = m // n_workers
    n_chunks = PEER_PAIRS // SC_CHUNK
    n_vec = d // SC_LANES
    mesh = plsc.VectorSubcoreMesh(core_axis_name="c", subcore_axis_name="s")

    @functools.partial(
        pl.kernel, out_type=jax.ShapeDtypeStruct((m, d), F32), mesh=mesh,
        scratch_types=[pltpu.VMEM((2, PEER_PAIRS), I32), pltpu.VMEM((2, d), F32),
                       pltpu.VMEM((2, PEER_PAIRS), F32), pltpu.VMEM((2, d), F32),
                       pltpu.VMEM((2, SC_CHUNK, d), F32), pltpu.VMEM((2, SC_CHUNK, d), F32),
                       pltpu.SemaphoreType.DMA((2,)), pltpu.SemaphoreType.DMA((2,)),
                       pltpu.SemaphoreType.DMA((2,)), pltpu.SemaphoreType.DMA((2,))],
        compiler_params=pltpu.CompilerParams(needs_layout_passes=False),
        name="peer_experts_sc")
    def body(idx_hbm, h_hbm, g_hbm, u_hbm, v_hbm, o_hbm,
             idx_v, x_v, g_v, out_v, ubuf, vbuf, sem_meta, sem_out, sem_u, sem_v):
        wid = lax.axis_index("c") * info.num_subcores + lax.axis_index("s")
        tok0 = wid * per
        lane = lax.iota(I32, SC_LANES)

        def meta_copies(ti, ms):
            t = tok0 + ti
            return (pltpu.make_async_copy(idx_hbm.at[t], idx_v.at[ms], sem_meta.at[ms]),
                    pltpu.make_async_copy(h_hbm.at[t], x_v.at[ms], sem_meta.at[ms]),
                    pltpu.make_async_copy(g_hbm.at[t], g_v.at[ms], sem_meta.at[ms]))

        def gather_copies(ms, c, slot):
            ids = idx_v.at[ms, pl.ds(c * SC_CHUNK, SC_CHUNK)]
            return (pltpu.make_async_copy(u_hbm.at[ids], ubuf.at[slot], sem_u.at[slot]),
                    pltpu.make_async_copy(v_hbm.at[ids], vbuf.at[slot], sem_v.at[slot]))

        def out_copy(ti, ms):
            return pltpu.make_async_copy(out_v.at[ms], o_hbm.at[tok0 + ti], sem_out.at[ms])

        for cp in meta_copies(0, 0):
            cp.start()
        for cp in meta_copies(0, 0):
            cp.wait()
        for cp in gather_copies(0, 0, 0):
            cp.start()

        def token(ti, carry):
            ms = ti % 2
            nxt = 1 - ms

            @pl.when(ti + 1 < per)
            def _():
                for cp in meta_copies(ti + 1, nxt):
                    cp.start()

            @pl.when(ti >= 2)
            def _():
                out_copy(ti - 2, ms).wait()

            def zero(j, c):
                out_v[ms, pl.ds(j * SC_LANES, SC_LANES)] = jnp.zeros((SC_LANES,), F32)
                return c
            lax.fori_loop(0, n_vec, zero, 0)

            for c in range(n_chunks):
                slot = c % 2
                if c + 1 < n_chunks:
                    for cp in gather_copies(ms, c + 1, 1 - slot):
                        cp.start()
                else:
                    @pl.when(ti + 1 < per)
                    def _():
                        for cp in meta_copies(ti + 1, nxt):
                            cp.wait()
                        for cp in gather_copies(nxt, 0, 1 - slot):
                            cp.start()
                cu, cv = gather_copies(ms, c, slot)
                cu.wait()

                def udot(j, accs):
                    xj = x_v[ms, pl.ds(j * SC_LANES, SC_LANES)]
                    return tuple(accs[r] + ubuf[slot, r, pl.ds(j * SC_LANES, SC_LANES)] * xj
                                 for r in range(SC_CHUNK))
                accs = lax.fori_loop(0, n_vec, udot,
                                     tuple(jnp.zeros((SC_LANES,), F32) for _ in range(SC_CHUNK)))
                act = jnp.zeros((SC_LANES,), F32)
                for r in range(SC_CHUNK):
                    act = jnp.where(lane == r, jnp.sum(accs[r]), act)
                y = GELU_C * (act + 0.044715 * (act * act * act))
                w = act / (1.0 + jnp.exp(-2.0 * y)) * g_v[ms, pl.ds(c * SC_CHUNK, SC_CHUNK)]
                ws = [jnp.sum(jnp.where(lane == r, w, 0.0)) for r in range(SC_CHUNK)]
                cv.wait()

                @plsc.parallel_loop(0, n_vec, unroll=2)
                def _(j):
                    parts = [ws[r] * vbuf[slot, r, pl.ds(j * SC_LANES, SC_LANES)] for r in range(SC_CHUNK)]
                    while len(parts) > 1:
                        parts = [parts[i] + parts[i + 1] for i in range(0, len(parts), 2)]
                    plsc.addupdate(out_v.at[ms, pl.ds(j * SC_LANES, SC_LANES)], parts[0])

            out_copy(ti, ms).start()
            return carry

        lax.fori_loop(0, per, token, 0)
        for back in (2, 1):
            if per >= back:
                out_copy(per - back, (per - back) % 2).wait()

    return body(idx, h, gates, u_rows, v_rows)


def _residual_kernel(x_ref, y_ref, m_ref, o_ref, *, d):
    o_ref[...] = x_ref[...] + m_ref[0][:, 5 * d:6 * d] * y_ref[...]


def _residual(x, y, mods, mod_row0, seq):
    m, d = y.shape
    tile = min(ROW_TILE, seq)
    return pl.pallas_call(
        functools.partial(_residual_kernel, d=d),
        grid=(m // tile,),
        in_specs=[pl.BlockSpec((tile, d), lambda i: (i, 0)),
                  pl.BlockSpec((tile, d), lambda i: (i, 0)),
                  _mod_spec(mod_row0, seq, tile, mods.shape[-1])],
        out_specs=pl.BlockSpec((tile, d), lambda i: (i, 0)),
        out_shape=jax.ShapeDtypeStruct((m, d), F32),
        compiler_params=_cparams("arbitrary"),
        name="peer_residual",
    )(x, y, mods)


def _peer(x, mods, mod_row0, seq, layer, nw, wq_bf16, keys_bf16, p_u, p_v, sc_share):
    m, d = x.shape
    h, idx, gates = _peer_route(x, mods, mod_row0, seq, nw, wq_bf16, keys_bf16)
    n_blocks = m // PEER_BLOCK
    sc_blocks = (n_blocks * sc_share[0]) // sc_share[1]
    m_sc = sc_blocks * PEER_BLOCK
    n_experts = p_u.shape[1]
    idx_sc = idx[:, :m_sc].T + layer * n_experts
    y_sc = _peer_experts_sc(idx_sc, h, gates[:, :m_sc].T,
                            p_u.reshape(-1, d), p_v.reshape(-1, d))
    x_sc = _residual(x, y_sc, mods, mod_row0, seq)
    if sc_blocks == n_blocks:
        return x_sc
    x_tc = _peer_gather(idx, h, gates, x, mods, mod_row0, seq, p_u, p_v, layer, sc_blocks)
    return jnp.concatenate([x_sc, x_tc], axis=0)


def _head_mean_square(x):
    n = x.shape[1]
    r = lax.broadcasted_iota(I32, (n, n), 0) // HEAD_DIM
    c = lax.broadcasted_iota(I32, (n, n), 1) // HEAD_DIM
    seg = jnp.where(r == c, 1.0 / HEAD_DIM, 0.0).astype(F32)
    return jnp.dot(x * x, seg, precision=HIGHEST, preferred_element_type=F32)


def _swap_rot_halves(x):
    n = x.shape[1]
    quarter = HEAD_DIM // 4
    lane = lax.broadcasted_iota(I32, x.shape, 1)
    lo = (lane % (2 * quarter)) < quarter
    return jnp.where(lo, pltpu.roll(x, n - quarter, axis=1), pltpu.roll(x, quarter, axis=1))


def _qkprep_kernel(q_ref, k_ref, qw_ref, kw_ref, *rest, rope):
    if rope:
        cos_ref, sin_ref, qo_ref, ko_ref = rest
    else:
        qo_ref, ko_ref = rest
    q = q_ref[...]
    k = k_ref[...]
    q = q * lax.rsqrt(_head_mean_square(q) + NORM_EPS) * qw_ref[...]
    k = k * lax.rsqrt(_head_mean_square(k) + NORM_EPS) * kw_ref[...]
    if rope:
        cos = cos_ref[...]
        sin = sin_ref[...]
        cq = jnp.concatenate([cos] * (q.shape[1] // LANES), axis=1)
        sq = jnp.concatenate([sin] * (q.shape[1] // LANES), axis=1)
        q = q * cq + _swap_rot_halves(q) * sq
        k = k * cos + _swap_rot_halves(k) * sin
    qo_ref[...] = q
    ko_ref[...] = k


def _rope_tables(seq):
    axis_dim = HEAD_DIM // 2
    inv_freq = ROPE_THETA ** (-jnp.arange(0, axis_dim, 2, dtype=F32) / axis_dim)
    t = jnp.arange(seq)
    pos = jnp.stack([(t // GRID_W).astype(F32), (t % GRID_W).astype(F32)], axis=1)
    lane = jnp.arange(LANES)
    dd = lane % HEAD_DIM
    ang = pos[:, dd // axis_dim] * inv_freq[dd % (axis_dim // 2)][None, :]
    sign = jnp.where((dd % axis_dim) < axis_dim // 2, -1.0, 1.0).astype(F32)
    return jnp.cos(ang), jnp.sin(ang) * sign[None, :]


def _qkprep(q, k, qw, kw, seq, rope):
    m, nq = q.shape
    nk = k.shape[1]
    tile = min(ROW_TILE, seq)
    qw_row = jnp.tile(qw, nq // HEAD_DIM).reshape(1, nq)
    kw_row = jnp.tile(kw, nk // HEAD_DIM).reshape(1, nk)
    in_specs = [pl.BlockSpec((tile, nq), lambda i: (i, 0)),
                pl.BlockSpec((tile, nk), lambda i: (i, 0)),
                pl.BlockSpec((1, nq), lambda i: (0, 0)),
                pl.BlockSpec((1, nk), lambda i: (0, 0))]
    args = [q, k, qw_row, kw_row]
    if rope:
        cos, sin = _rope_tables(seq)
        per_seq = seq // tile
        in_specs += [pl.BlockSpec((tile, LANES), lambda i: (i % per_seq, 0)),
                     pl.BlockSpec((tile, LANES), lambda i: (i % per_seq, 0))]
        args += [cos, sin]
    return pl.pallas_call(
        functools.partial(_qkprep_kernel, rope=rope),
        grid=(m // tile,),
        in_specs=in_specs,
        out_specs=[pl.BlockSpec((tile, nq), lambda i: (i, 0)),
                   pl.BlockSpec((tile, nk), lambda i: (i, 0))],
        out_shape=[jax.ShapeDtypeStruct((m, nq), F32), jax.ShapeDtypeStruct((m, nk), F32)],
        compiler_params=_cparams("arbitrary"),
        name="qk_prep",
    )(*args)


def _dup_halves(x):
    lane = lax.broadcasted_iota(I32, x.shape, 1)
    sw = pltpu.roll(x, HEAD_DIM, axis=1)
    lo = lane < HEAD_DIM
    return jnp.where(lo, x, sw), jnp.where(lo, sw, x)


def _attend(q, k_all, v_all, sink_ref, mask):
    scale = HEAD_DIM ** -0.5
    nt = (((1,), (1,)), ((), ()))
    kk = [a.astype(BF16) for a in _dup_halves(k_all)]
    vv = [a.astype(BF16) for a in _dup_halves(v_all)]
    lane = lax.broadcasted_iota(I32, (q.shape[0], LANES), 1)
    lo = lane < HEAD_DIM
    tiles = []
    for t in range(q.shape[1] // LANES):
        qt = q[:, t * LANES:(t + 1) * LANES]
        g = (2 * t) // GQA_GROUP
        halves = []
        for hh in range(2):
            head = 2 * t + hh
            qm = jnp.where(lo if hh == 0 else ~lo, qt, 0.0).astype(BF16)
            s = lax.dot_general(qm, kk[g], nt, preferred_element_type=F32) * scale
            if mask is not None:
                s = jnp.where(mask, s, NEG_BIG)
            sink = sink_ref[head]
            mx = jnp.maximum(jnp.max(s, axis=1, keepdims=True), sink)
            p = jnp.exp(s - mx)
            den = jnp.sum(p, axis=1, keepdims=True) + jnp.exp(sink - mx)
            p = (p / den).astype(BF16)
            halves.append(jnp.dot(p, vv[g], preferred_element_type=F32))
        tiles.append(jnp.where(lo, halves[0], halves[1]))
    return jnp.concatenate(tiles, axis=1)


def _ctx_attn_kernel(sink_ref, q_ref, k_ref, v_ref, o_ref):
    o_ref[...] = _attend(q_ref[...], k_ref[...], v_ref[...], sink_ref, None)


def _ctx_attention(q, k, v, sink, seq):
    m, nq = q.shape
    nk = k.shape[1]
    return pl.pallas_call(
        _ctx_attn_kernel,
        grid=(m // seq,),
        in_specs=[pl.BlockSpec(memory_space=pltpu.SMEM),
                  pl.BlockSpec((seq, nq), lambda b: (b, 0)),
                  pl.BlockSpec((seq, nk), lambda b: (b, 0)),
                  pl.BlockSpec((seq, nk), lambda b: (b, 0))],
        out_specs=pl.BlockSpec((seq, nq), lambda b: (b, 0)),
        out_shape=jax.ShapeDtypeStruct((m, nq), F32),
        compiler_params=_cparams("arbitrary"),
        name="ctx_attention",
    )(sink, q, k, v)


def _lat_attn_kernel(sink_ref, q_ref, kc_ref, vc_ref, kp_ref, k0_ref, kn_ref, vp_ref, v0_ref, vn_ref,
                     o_ref, *, seq):
    qb = pl.program_id(1)
    blk = q_ref.shape[0]
    n_ctx = kc_ref.shape[1]
    k_all = jnp.concatenate([kc_ref[0], kp_ref[...], k0_ref[...], kn_ref[...]], axis=0)
    v_all = jnp.concatenate([vc_ref[0], vp_ref[...], v0_ref[...], vn_ref[...]], axis=0)
    tk = k_all.shape[0]
    qpos = qb * blk + lax.broadcasted_iota(I32, (blk, tk), 0)
    col = lax.broadcasted_iota(I32, (blk, tk), 1)
    kpos = (qb - 1) * blk + col - n_ctx
    local_ok = (jnp.abs(qpos - kpos) <= WINDOW) & (kpos >= 0) & (kpos < seq)
    mask = (col < n_ctx) | local_ok
    o_ref[...] = _attend(q_ref[...], k_all, v_all, sink_ref, mask)


def _lat_attention(q, k, v, k_ctx, v_ctx, sink, seq):
    m, nq = q.shape
    nk = k.shape[1]
    blk = WINDOW
    nb = seq // blk
    n_ctx = k_ctx.shape[1]
    last = m // blk - 1

    def kv_spec(shift):
        return pl.BlockSpec((blk, nk), lambda b, i: (jnp.clip(b * nb + i + shift, 0, last), 0))

    ctx_spec = pl.BlockSpec((1, n_ctx, nk), lambda b, i: (b, 0, 0))
    return pl.pallas_call(
        functools.partial(_lat_attn_kernel, seq=seq),
        grid=(m // seq, nb),
        in_specs=[pl.BlockSpec(memory_space=pltpu.SMEM),
                  pl.BlockSpec((blk, nq), lambda b, i: (b * nb + i, 0)),
                  ctx_spec, ctx_spec,
                  kv_spec(-1), kv_spec(0), kv_spec(1),
                  kv_spec(-1), kv_spec(0), kv_spec(1)],
        out_specs=pl.BlockSpec((blk, nq), lambda b, i: (b * nb + i, 0)),
        out_shape=jax.ShapeDtypeStruct((m, nq), F32),
        compiler_params=_cparams("arbitrary", "arbitrary"),
        name="lat_attention",
    )(sink, q, k_ctx, v_ctx, k, k, k, v, v, v)

SSD_BLOCK = 256
SSD_PAIRS = SSD_HEADS // 2
SSD_INNER = SSD_HEADS * SSD_HEAD_DIM
HALO = SUBLANES


def _softplus(x):
    return jnp.maximum(x, 0.0) + jnp.log1p(jnp.exp(-jnp.abs(x)))


def _silu(x):
    return x * jax.nn.sigmoid(x)


def _ssd_decays(dt_raw, bias, a_log):
    n = dt_raw.shape[0]
    dt = _softplus(dt_raw + bias)
    log_a = dt * (-jnp.exp(a_log))
    r = lax.broadcasted_iota(I32, (n, n), 0)
    c = lax.broadcasted_iota(I32, (n, n), 1)
    lower = jnp.where(c <= r, 1.0, 0.0).astype(F32)
    upper = jnp.where(r <= c, 1.0, 0.0).astype(F32)
    cum_col = jnp.dot(lower, log_a, precision=HIGHEST, preferred_element_type=F32)
    dt_row = dt.T
    la_row = log_a.T
    cum_row = jnp.dot(la_row, upper, precision=HIGHEST, preferred_element_type=F32)
    return dt, log_a, cum_col, dt_row, la_row, cum_row


def _ssd_scan_chunk(xs, bmat, cmat, w_of, q_scale_of, k_scale_of, carry_of, s_ref):
    nt = (((1,), (1,)), ((), ()))
    n = xs.shape[0]
    lane = lax.broadcasted_iota(I32, (n, LANES), 1)
    lo = lane < SSD_HEAD_DIM
    lane_s = lax.broadcasted_iota(I32, (D_STATE, LANES), 1)
    lo_s = lane_s < SSD_HEAD_DIM
    b_t = bmat.T
    cb16 = cmat.astype(BF16)
    ys = []
    for pair in range(SSD_PAIRS):
        g = (2 * pair) // (SSD_HEADS // SSD_GROUPS)
        in_g = (lane // D_STATE) == g
        cg = jnp.where(in_g, cmat, 0.0)
        cb = lax.dot_general(cg.astype(BF16), bmat.astype(BF16), nt, preferred_element_type=F32)
        x_pair = xs[:, pair * LANES:(pair + 1) * LANES]
        x16 = x_pair.astype(BF16)
        s_old = s_ref[pair]
        s2 = jnp.concatenate([s_old, s_old], axis=0).astype(BF16)
        bg_t = b_t[g * D_STATE:(g + 1) * D_STATE, :]
        y_h, s_h = [], []
        for hh in range(2):
            h = 2 * pair + hh
            w = (cb * w_of(h)).astype(BF16)
            y = jnp.dot(w, x16, preferred_element_type=F32)
            cq = (cg * q_scale_of(h)).astype(BF16)
            y = y + jnp.dot(cq, s2, preferred_element_type=F32)
            y_h.append(y)
            kt = (bg_t * k_scale_of(h)).astype(BF16)
            s_h.append(carry_of(h) * s_old + jnp.dot(kt, x16, preferred_element_type=F32))
        ys.append(jnp.where(lo, y_h[0], y_h[1]))
        s_ref[pair] = jnp.where(lo_s, s_h[0], s_h[1])
    return jnp.concatenate(ys, axis=1)


def _ssd_fwd_kernel(x_ref, xp_ref, xn_ref, dt_ref, s0_ref, cw_ref, cb_ref, bias_ref, alog_ref,
                    y_ref, xc_ref, sfin_ref, s_ref):
    c = pl.program_id(1)
    nc = pl.num_programs(1)
    n = x_ref.shape[0]

    @pl.when(c == 0)
    def _():
        s_ref[...] = s0_ref[0]

    prev = jnp.where(c > 0, xp_ref[...], 0.0)
    nxt = jnp.where(c < nc - 1, xn_ref[...], 0.0)
    xe = jnp.concatenate([prev, x_ref[...], nxt], axis=0)
    pad = (CONV_K - 1) // 2
    acc = cb_ref[...] + cw_ref[0:1, :] * xe[HALO - pad:HALO - pad + n, :]
    for k in range(1, CONV_K):
        acc = acc + cw_ref[k:k + 1, :] * xe[HALO - pad + k:HALO - pad + k + n, :]
    xc = _silu(acc)
    xc_ref[...] = xc
    xs = xc[:, :SSD_INNER]
    bmat = xc[:, SSD_INNER:SSD_INNER + LANES]
    cmat = xc[:, SSD_INNER + LANES:SSD_INNER + 2 * LANES]

    dt, log_a, cum_col, dt_row, la_row, cum_row = _ssd_decays(dt_ref[...], bias_ref[...], alog_ref[...])
    r = lax.broadcasted_iota(I32, (n, n), 0)
    cc = lax.broadcasted_iota(I32, (n, n), 1)
    causal = cc <= r
    last_col = cum_col[n - 1:n, :]

    def w_of(h):
        seg = cum_col[:, h:h + 1] - cum_row[h:h + 1, :]
        return jnp.exp(jnp.where(causal, seg, NEG_BIG)) * dt_row[h:h + 1, :]

    def q_scale_of(h):
        return jnp.exp(cum_col[:, h:h + 1])

    def k_scale_of(h):
        return dt_row[h:h + 1, :] * jnp.exp(cum_row[h:h + 1, n - 1:n] - cum_row[h:h + 1, :])

    def carry_of(h):
        return jnp.exp(last_col[:, h:h + 1])

    y_ref[...] = _ssd_scan_chunk(xs, bmat, cmat, w_of, q_scale_of, k_scale_of, carry_of, s_ref)

    @pl.when(c == nc - 1)
    def _():
        sfin_ref[0] = s_ref[...]


def _ssd_bwd_kernel(xc_ref, dt_ref, yf_ref, z_ref, s0_ref, bias_ref, alog_ref, dskip_ref, nw_ref,
                    y_ref, sfin_ref, s_ref):
    c = pl.program_id(1)
    nc = pl.num_programs(1)
    n = xc_ref.shape[0]

    @pl.when(c == 0)
    def _():
        s_ref[...] = s0_ref[0]

    xc = xc_ref[...]
    xs = xc[:, :SSD_INNER]
    bmat = xc[:, SSD_INNER:SSD_INNER + LANES]
    cmat = xc[:, SSD_INNER + LANES:SSD_INNER + 2 * LANES]
    dt, log_a, cum_col, dt_row, la_row, cum_row = _ssd_decays(dt_ref[...], bias_ref[...], alog_ref[...])
    ex_col = cum_col - log_a
    ex_row = cum_row - la_row
    r = lax.broadcasted_iota(I32, (n, n), 0)
    cc = lax.broadcasted_iota(I32, (n, n), 1)
    anti = cc >= r
    tot_col = cum_col[n - 1:n, :]
    off = SSD_HEADS

    def w_of(h):
        j = off + h
        seg = ex_row[j:j + 1, :] - ex_col[:, j:j + 1]
        return jnp.exp(jnp.where(anti, seg, NEG_BIG)) * dt_row[j:j + 1, :]

    def q_scale_of(h):
        j = off + h
        return jnp.exp(tot_col[:, j:j + 1] - ex_col[:, j:j + 1])

    def k_scale_of(h):
        j = off + h
        return dt_row[j:j + 1, :] * jnp.exp(ex_row[j:j + 1, :])

    def carry_of(h):
        j = off + h
        return jnp.exp(tot_col[:, j:j + 1])

    y_b = _ssd_scan_chunk(xs, bmat, cmat, w_of, q_scale_of, k_scale_of, carry_of, s_ref)
    y = yf_ref[...] + y_b + dskip_ref[...] * xs
    y = y * _silu(z_ref[...])
    ms = jnp.mean(y * y, axis=-1, keepdims=True)
    y_ref[...] = y * lax.rsqrt(ms + NORM_EPS) * nw_ref[...]

    @pl.when(c == nc - 1)
    def _():
        sfin_ref[0] = s_ref[...]


def _pair_states(s):
    b, h, n, p = s.shape
    return s.reshape(b, h // 2, 2, n, p).transpose(0, 1, 3, 2, 4).reshape(b, h // 2, n, 2 * p)


def _unpair_states(s):
    b, hp, n, p2 = s.shape
    return s.reshape(b, hp, n, 2, p2 // 2).transpose(0, 1, 3, 2, 4).reshape(b, hp * 2, n, p2 // 2)


def _ssd(xbc, dt, z, s0_f, s0_b, conv_w, conv_b, dt_bias, a_log, d_skip, ssd_norm, seq):
    m, nx = xbc.shape
    nb = m // seq
    blk = min(SSD_BLOCK, seq)
    nc = seq // blk
    hb = blk // HALO
    n_halo = m // HALO
    pad16 = lambda a: jnp.pad(a.reshape(1, -1), ((0, 0), (0, LANES - a.size)))
    bias = pad16(dt_bias)
    alog = pad16(a_log)
    state_spec = pl.BlockSpec((1, SSD_PAIRS, D_STATE, LANES), lambda b, c: (b, 0, 0, 0))
    state_shape = jax.ShapeDtypeStruct((nb, SSD_PAIRS, D_STATE, LANES), F32)
    row = lambda width: pl.BlockSpec((1, width), lambda b, c: (0, 0))

    def fwd_rows(width):
        return pl.BlockSpec((blk, width), lambda b, c: (b * nc + c, 0))

    def bwd_rows(width):
        return pl.BlockSpec((blk, width), lambda b, c: (b * nc + nc - 1 - c, 0))

    y_f, xc, s_f = pl.pallas_call(
        _ssd_fwd_kernel,
        grid=(nb, nc),
        in_specs=[fwd_rows(nx),
                  pl.BlockSpec((HALO, nx), lambda b, c: (jnp.maximum((b * nc + c) * hb - 1, 0), 0)),
                  pl.BlockSpec((HALO, nx), lambda b, c: (jnp.minimum((b * nc + c + 1) * hb, n_halo - 1), 0)),
                  fwd_rows(LANES), state_spec,
                  pl.BlockSpec((CONV_K, nx), lambda b, c: (0, 0)), row(nx), row(LANES), row(LANES)],
        out_specs=[fwd_rows(SSD_INNER), fwd_rows(nx), state_spec],
        out_shape=[jax.ShapeDtypeStruct((m, SSD_INNER), F32), jax.ShapeDtypeStruct((m, nx), F32), state_shape],
        scratch_shapes=[pltpu.VMEM((SSD_PAIRS, D_STATE, LANES), F32)],
        compiler_params=_cparams("arbitrary", "arbitrary"),
        name="ssd_forward",
    )(xbc, xbc, xbc, dt, _pair_states(s0_f), conv_w, conv_b.reshape(1, nx), bias, alog)

    dskip = jnp.repeat(d_skip, SSD_HEAD_DIM).reshape(1, SSD_INNER)
    y, s_b = pl.pallas_call(
        _ssd_bwd_kernel,
        grid=(nb, nc),
        in_specs=[bwd_rows(nx), bwd_rows(LANES), bwd_rows(SSD_INNER), bwd_rows(SSD_INNER), state_spec,
                  row(LANES), row(LANES), row(SSD_INNER), row(SSD_INNER)],
        out_specs=[bwd_rows(SSD_INNER), state_spec],
        out_shape=[jax.ShapeDtypeStruct((m, SSD_INNER), F32), state_shape],
        scratch_shapes=[pltpu.VMEM((SSD_PAIRS, D_STATE, LANES), F32)],
        compiler_params=_cparams("arbitrary", "arbitrary"),
        name="ssd_backward",
    )(xc, dt, y_f, z, _pair_states(s0_b), bias, alog, dskip, ssd_norm.reshape(1, SSD_INNER))
    return y, _unpair_states(s_f), _unpair_states(s_b)

def _hgrn_kernel(q_ref, ff_ref, fb_ref, i_ref, g_ref, lb_ref, s0_ref, nw_ref, o_ref, sfin_ref,
                 sf_ref, sb_ref, ob_ref, *, layer):
    t_len = q_ref.shape[0]
    n = HGRN_CHUNK
    n_chunks = t_len // n
    tn = (((0,), (0,)), ((), ()))
    nt = (((1,), (1,)), ((), ()))

    lbp = lb_ref[...]
    e = jnp.exp(lbp - jnp.max(lbp, axis=0, keepdims=True))
    sm = e / jnp.sum(e, axis=0, keepdims=True)
    lb = sm[0] * 0.0
    for j in range(1, layer + 1):
        lb = lb + sm[j]

    r = lax.broadcasted_iota(I32, (n, n), 0)
    c = lax.broadcasted_iota(I32, (n, n), 1)
    lower = jnp.where(c <= r, 1.0, 0.0).astype(F32)
    srow = lax.broadcasted_iota(I32, (n, HGRN_DK), 0)
    qscale = HGRN_DK ** -0.5

    def chunk(row0, f_ref, lb_d, reverse, s_ref):
        q = _silu(q_ref[pl.ds(row0, n), :]) * qscale
        f = f_ref[pl.ds(row0, n), :]
        v = i_ref[pl.ds(row0, n), :]
        k = (1.0 - lb_d) * jax.nn.sigmoid(-f)
        lf = jnp.log(lb_d + (1.0 - lb_d) * jax.nn.sigmoid(f))
        cum = jnp.dot(lower, lf, precision=HIGHEST, preferred_element_type=F32)
        tot = cum[n - 1:n, :]
        if reverse:
            cum = cum - lf
        rows = []
        for t in range(n):
            tile0 = (t // SUBLANES) * SUBLANES
            lo, hi = (tile0, n) if reverse else (0, tile0 + SUBLANES)
            cum_s = cum[lo:hi]
            if reverse:
                seg = jnp.where(srow[lo:hi] >= t, cum_s - cum[t:t + 1, :], NEG_BIG)
            else:
                seg = jnp.where(srow[lo:hi] <= t, cum[t:t + 1, :] - cum_s, NEG_BIG)
            a = q[t:t + 1, :] * k[lo:hi] * jnp.exp(seg)
            sc = jnp.sum(a, axis=1, keepdims=True)
            rows.append(jnp.sum(sc * v[lo:hi], axis=0, keepdims=True))
        o = jnp.concatenate(rows, axis=0)
        s_old = s_ref[...]
        if reverse:
            q_in = q * jnp.exp(tot - cum)
            k_out = k * jnp.exp(cum)
        else:
            q_in = q * jnp.exp(cum)
            k_out = k * jnp.exp(tot - cum)
        o = o + lax.dot_general(q_in.astype(BF16), s_old.astype(BF16), nt, preferred_element_type=F32)
        s_ref[...] = jnp.exp(tot) * s_old + lax.dot_general(
            v.astype(BF16), k_out.astype(BF16), tn, preferred_element_type=F32)
        return o

    sf_ref[...] = s0_ref[0, 0, 0].T
    sb_ref[...] = s0_ref[0, 1, 0].T

    def body(ci, carry):
        row_f = pl.multiple_of(ci * n, n)
        row_b = pl.multiple_of((n_chunks - 1 - ci) * n, n)
        o_ref[pl.ds(row_f, n), :] = chunk(row_f, ff_ref, lb[0:1, :], False, sf_ref)
        ob_ref[pl.ds(row_b, n), :] = chunk(row_b, fb_ref, lb[1:2, :], True, sb_ref)
        return carry

    lax.fori_loop(0, n_chunks, body, 0)
    sfin_ref[0, 0, 0] = sf_ref[...].T
    sfin_ref[0, 1, 0] = sb_ref[...].T

    nw = nw_ref[...]
    blk = min(t_len, ROW_TILE)

    def finish(bi, carry):
        row0 = pl.multiple_of(bi * blk, blk)
        o = o_ref[pl.ds(row0, blk), :] + ob_ref[pl.ds(row0, blk), :]
        ms = jnp.mean(o * o, axis=-1, keepdims=True)
        o = o * lax.rsqrt(ms + NORM_EPS) * nw
        o_ref[pl.ds(row0, blk), :] = o * _silu(g_ref[pl.ds(row0, blk), :])
        return carry

    lax.fori_loop(0, t_len // blk, finish, 0)


def _hgrn(q, f_fw, f_bw, iv, g, o_lb, state0, g_norm, seq, layer):
    m, width = q.shape
    nb = m // seq
    dv = width // HGRN_HEADS
    col = pl.BlockSpec((seq, dv), lambda b, h: (b, h))
    state_spec = pl.BlockSpec((1, 2, 1, HGRN_DK, dv), lambda b, h: (b, 0, h, 0, 0))
    return pl.pallas_call(
        functools.partial(_hgrn_kernel, layer=layer),
        grid=(nb, HGRN_HEADS),
        in_specs=[col, col, col, col, col,
                  pl.BlockSpec((o_lb.shape[0], 2, HGRN_DK), lambda b, h: (0, 0, h)),
                  state_spec,
                  pl.BlockSpec((1, dv), lambda b, h: (0, 0))],
        out_specs=[col, state_spec],
        out_shape=[jax.ShapeDtypeStruct((m, width), F32),
                   jax.ShapeDtypeStruct((nb, 2, HGRN_HEADS, HGRN_DK, dv), F32)],
        scratch_shapes=[pltpu.VMEM((dv, HGRN_DK), F32), pltpu.VMEM((dv, HGRN_DK), F32),
                        pltpu.VMEM((seq, dv), F32)],
        compiler_params=_cparams("arbitrary", "arbitrary"),
        name="hgrn2",
    )(q, f_fw, f_bw, iv, g, o_lb, state0, g_norm.reshape(1, dv))

EVEN_SPLITS = ((0, 512), (512, 640), (640, 768), (768, 1280), (1280, 2048), (2048, 2176))
HGRN_SPLITS = tuple((i * 1024, (i + 1) * 1024) for i in range(5))


def _even_weight(w):
    main = EVEN_SPLITS[-1][0]
    return jnp.pad(w, ((0, 0), (0, LANES - (w.shape[1] - main)))).astype(BF16)


def _run_trunk(x3, mods, mod_row0, P, cache):
    nb, seq, d = x3.shape
    x = x3.reshape(nb * seq, d)
    depth = P['norm_mix'].shape[0]
    ks, vs, ssd_states, hgrn_states = [], [], [], []
    for l in range(depth):
        j = l // 2
        row0 = (l * SUBLANES + mod_row0, 0 if cache is None else 1)
        if l % 2 == 0:
            q, k, v, z, xbc, dt = _inproj(x, mods, row0, seq, P['norm_mix'][l], P['e_w_in'][j],
                                          EVEN_SPLITS, "even_in_proj")
            q, k = _qkprep(q, k, P['e_q_norm'][j], P['e_k_norm'][j], seq, rope=cache is not None)
            if cache is None:
                s0_f = jnp.zeros((nb, SSD_HEADS, D_STATE, SSD_HEAD_DIM), F32)
                s0_b = s0_f
                o_attn = _ctx_attention(q, k, v, P['e_sink'][j], seq)
            else:
                s0_f, s0_b = cache[2][:, j, 0], cache[2][:, j, 1]
                n_ctx = cache[0].shape[2]
                o_attn = _lat_attention(q, k, v, cache[0][:, j].reshape(nb, n_ctx, -1),
                                        cache[1][:, j].reshape(nb, n_ctx, -1), P['e_sink'][j], seq)
            y, s_f, s_b = _ssd(xbc, dt, z, s0_f, s0_b, P['e_conv_w'][j], P['e_conv_b'][j],
                               P['e_dt_bias'][j], P['e_a_log'][j], P['e_d_skip'][j], P['e_ssd_norm'][j], seq)
            if cache is None:
                ks.append(k.reshape(nb, seq, N_KV_HEADS, HEAD_DIM))
                vs.append(v.reshape(nb, seq, N_KV_HEADS, HEAD_DIM))
                ssd_states.append(jnp.stack([s_f, s_b], axis=1))
            mix = jnp.concatenate([o_attn, y], axis=1)
            x = _outproj(mix, x, mods, row0, seq, P['e_w_out'][j], "even_out_proj")
        else:
            q, f_fw, f_bw, iv, g = _inproj(x, mods, row0, seq, P['norm_mix'][l], P['o_w_in'][j],
                                           HGRN_SPLITS, "odd_in_proj")
            if cache is None:
                s0 = jnp.zeros((nb, 2, HGRN_HEADS, HGRN_DK, d // HGRN_HEADS), F32)
            else:
                s0 = cache[3][:, j]
            o, s_new = _hgrn(q, f_fw, f_bw, iv, g, P['o_lb'], s0, P['o_g_norm'][j], seq, j)
            if cache is None:
                hgrn_states.append(s_new)
            x = _outproj(o, x, mods, row0, seq, P['o_w_out'][j], "odd_out_proj")
        x = _peer(x, mods, row0, seq, l, P['norm_ffn'][l], P['p_w_q'][l], P['p_sub_keys'][l],
                  P['p_u'], P['p_v'],
                  SC_SHARE_CONTEXT if cache is None else SC_SHARE_LATENT)
    y = x.reshape(nb, seq, d)
    if cache is not None:
        return y, None
    return y, (jnp.stack(ks, axis=1), jnp.stack(vs, axis=1),
               jnp.stack(ssd_states, axis=1), jnp.stack(hgrn_states, axis=1))


def kernel(x_prompt, x_sample, cache_k, cache_v, state_ssd, state_hgrn, c, c_ctx, w_ada, b_ada, norm_mix, norm_ffn, e_w_in, e_q_norm, e_k_norm, e_sink, e_conv_w, e_conv_b, e_dt_bias, e_a_log, e_d_skip, e_ssd_norm, e_w_out, o_w_in, o_lb, o_g_norm, o_w_out, p_w_q, p_sub_keys, p_u, p_v):
    depth, d, d6 = w_ada.shape
    b_lat = x_sample.shape[0]
    cond_rows = jnp.concatenate([c_ctx[None, :], c, jnp.zeros((SUBLANES - 1 - b_lat, d), F32)], axis=0)
    mods = _modulation(cond_rows, w_ada, b_ada).reshape(depth * SUBLANES, 1, d6)
    P = {
        'norm_mix': norm_mix, 'norm_ffn': norm_ffn,
        'e_w_in': jnp.stack([_even_weight(w) for w in e_w_in]), 'e_q_norm': e_q_norm, 'e_k_norm': e_k_norm,
        'e_sink': e_sink, 'e_conv_w': e_conv_w, 'e_conv_b': e_conv_b, 'e_dt_bias': e_dt_bias,
        'e_a_log': e_a_log, 'e_d_skip': e_d_skip, 'e_ssd_norm': e_ssd_norm,
        'e_w_out': e_w_out.astype(BF16),
        'o_w_in': o_w_in.astype(BF16), 'o_lb': o_lb, 'o_g_norm': o_g_norm, 'o_w_out': o_w_out.astype(BF16),
        'p_w_q': p_w_q.astype(BF16),
        'p_sub_keys': p_sub_keys.astype(BF16).reshape(depth, PEER_HEADS * 2, PEER_NKEYS, PEER_DKEY),
        'p_u': p_u, 'p_v': p_v,
    }
    y_prompt, new_state = _run_trunk(x_prompt, mods, 0, P, None)
    y_sample, _ = _run_trunk(x_sample, mods, 1, P, (cache_k, cache_v, state_ssd, state_hgrn))
    return (y_prompt, y_sample) + new_state
```

```python
import functools
import math

import jax
import jax.numpy as jnp
from jax import lax
from jax.experimental import pallas as pl
from jax.experimental.pallas import tpu as pltpu
from jax.experimental.pallas import tpu_sc as plsc

F32 = jnp.float32
BF16 = jnp.bfloat16
I32 = jnp.int32
HIGHEST = lax.Precision.HIGHEST

NORM_EPS = 1e-6
NEG_BIG = -1e30
LANES = 128
SUBLANES = 8
VMEM_LIMIT = 48 * 1024 * 1024

GRID_W = 64
HEAD_DIM = 64
N_Q_HEADS = 8
N_KV_HEADS = 2
GQA_GROUP = 4
WINDOW = 128
ROPE_THETA = 10000.0
SSD_HEADS = 8
SSD_HEAD_DIM = 64
SSD_GROUPS = 2
D_STATE = 64
CONV_K = 5
HGRN_HEADS = 8
HGRN_DK = 128
HGRN_CHUNK = 32
PEER_HEADS = 8
PEER_NKEYS = 128
PEER_TOPK = 16
PEER_DKEY = 128
PEER_PAIRS = PEER_HEADS * PEER_TOPK

ROW_TILE = 256
PEER_BLOCK = 128
SC_LANES = 16
SC_CHUNK = 16
SC_SHARE_CONTEXT = (7, 8)
SC_SHARE_LATENT = (13, 16)
GELU_C = math.sqrt(2.0 / math.pi)


def _cparams(*sem):
    return pltpu.CompilerParams(dimension_semantics=sem, vmem_limit_bytes=VMEM_LIMIT)


def _norm_mod(x, nw, scale, shift):
    ms = jnp.mean(x * x, axis=-1, keepdims=True)
    return (x * lax.rsqrt(ms + NORM_EPS)) * nw * (1.0 + scale) + shift


def _mod_kernel(c_ref, w_ref, b_ref, o_ref):
    c = c_ref[...]
    s = c * jax.nn.sigmoid(c)
    o_ref[0] = jnp.dot(s, w_ref[0], precision=HIGHEST, preferred_element_type=F32) + b_ref[0]


def _modulation(cond_rows, w_ada, b_ada):
    depth, d, n = w_ada.shape
    rows = cond_rows.shape[0]
    return pl.pallas_call(
        _mod_kernel,
        grid=(depth, n // d),
        in_specs=[pl.BlockSpec((rows, d), lambda l, j: (0, 0)),
                  pl.BlockSpec((1, d, d), lambda l, j: (l, 0, j)),
                  pl.BlockSpec((1, 1, d), lambda l, j: (l, 0, j))],
        out_specs=pl.BlockSpec((1, rows, d), lambda l, j: (l, 0, j)),
        out_shape=jax.ShapeDtypeStruct((depth, rows, n), F32),
        compiler_params=_cparams("arbitrary", "arbitrary"),
        name="modulation",
    )(cond_rows, w_ada, b_ada.reshape(depth, 1, n))


def _mod_spec(mod_row0, seq, tile, d6, first_tile=0):
    row0, per_batch = mod_row0
    return pl.BlockSpec((1, 1, d6),
                        lambda i: (row0 + per_batch * (((i + first_tile) * tile) // seq), 0, 0))


def _inproj_kernel(x_ref, m_ref, nw_ref, w_ref, *o_refs, splits, d):
    m = m_ref[0]
    h = _norm_mod(x_ref[...], nw_ref[...], m[:, d:2 * d], m[:, 0:d]).astype(BF16)
    for o_ref, (a, b) in zip(o_refs, splits):
        o_ref[...] = jnp.dot(h, w_ref[:, a:b], preferred_element_type=F32)


def _inproj(x, mods, mod_row0, seq, nw, w_bf16, splits, name):
    m, d = x.shape
    n = w_bf16.shape[1]
    tile = min(ROW_TILE, seq)
    return pl.pallas_call(
        functools.partial(_inproj_kernel, splits=splits, d=d),
        grid=(m // tile,),
        in_specs=[pl.BlockSpec((tile, d), lambda i: (i, 0)),
                  _mod_spec(mod_row0, seq, tile, mods.shape[-1]),
                  pl.BlockSpec((1, d), lambda i: (0, 0)),
                  pl.BlockSpec((d, n), lambda i: (0, 0))],
        out_specs=[pl.BlockSpec((tile, b - a), lambda i: (i, 0)) for a, b in splits],
        out_shape=[jax.ShapeDtypeStruct((m, b - a), F32) for a, b in splits],
        compiler_params=_cparams("arbitrary"),
        name=name,
    )(x, mods, nw.reshape(1, d), w_bf16)


def _outproj_kernel(mix_ref, x_ref, m_ref, w_ref, o_ref, *, d):
    y = jnp.dot(mix_ref[...].astype(BF16), w_ref[...], preferred_element_type=F32)
    o_ref[...] = x_ref[...] + m_ref[0][:, 2 * d:3 * d] * y


def _outproj(mix, x, mods, mod_row0, seq, w_bf16, name):
    m, d = x.shape
    k = mix.shape[1]
    tile = min(ROW_TILE, seq)
    return pl.pallas_call(
        functools.partial(_outproj_kernel, d=d),
        grid=(m // tile,),
        in_specs=[pl.BlockSpec((tile, k), lambda i: (i, 0)),
                  pl.BlockSpec((tile, d), lambda i: (i, 0)),
                  _mod_spec(mod_row0, seq, tile, mods.shape[-1]),
                  pl.BlockSpec((k, d), lambda i: (0, 0))],
        out_specs=pl.BlockSpec((tile, d), lambda i: (i, 0)),
        out_shape=jax.ShapeDtypeStruct((m, d), F32),
        compiler_params=_cparams("arbitrary"),
        name=name,
    )(mix, x, mods, w_bf16)


def _topk_over_rows(s, k, payload=None):
    n = s.shape[0]
    iota = lax.broadcasted_iota(I32, s.shape, 0)
    vals, idxs, pays = [], [], []
    for _ in range(k):
        m = jnp.max(s, axis=0, keepdims=True)
        i = jnp.min(jnp.where(s == m, iota, n), axis=0, keepdims=True)
        hit = iota == i
        vals.append(m)
        idxs.append(i)
        if payload is not None:
            pays.append(jnp.max(jnp.where(hit, payload, -1), axis=0, keepdims=True))
        s = jnp.where(hit, -jnp.inf, s)
    out = (jnp.concatenate(vals, axis=0), jnp.concatenate(idxs, axis=0))
    if payload is not None:
        out += (jnp.concatenate(pays, axis=0),)
    return out


def _peer_route_kernel(x_ref, m_ref, nw_ref, wq_ref, keys_ref, h_ref, e_ref, g_ref, *, d):
    m = m_ref[0]
    h = _norm_mod(x_ref[...], nw_ref[...], m[:, 4 * d:5 * d], m[:, 3 * d:4 * d])
    h_ref[...] = h
    hb = h.astype(BF16)
    nt = (((1,), (1,)), ((), ()))
    for head in range(PEER_HEADS):
        tops = []
        for half in range(2):
            c0 = (head * 2 + half) * PEER_DKEY
            q = jnp.dot(hb, wq_ref[:, c0:c0 + PEER_DKEY], preferred_element_type=F32)
            s = lax.dot_general(keys_ref[head * 2 + half], q.astype(BF16), nt,
                                preferred_element_type=F32)
            tops.append(_topk_over_rows(s, PEER_TOPK))
        (s0, i0), (s1, i1) = tops
        widths = [PEER_TOPK // (a + 1) for a in range(PEER_TOPK)]
        n_pad = -sum(widths) % SUBLANES
        cand_s = jnp.concatenate([s0[a:a + 1] + s1[:w] for a, w in enumerate(widths)]
                                 + [jnp.full((n_pad, s0.shape[1]), -jnp.inf, F32)], axis=0)
        cand_e = jnp.concatenate([i0[a:a + 1] * PEER_NKEYS + i1[:w] for a, w in enumerate(widths)]
                                 + [jnp.zeros((n_pad, s0.shape[1]), I32)], axis=0)
        best_s, _, best_e = _topk_over_rows(cand_s, PEER_TOPK, payload=cand_e)
        p = jnp.exp(best_s - best_s[0:1])
        r0 = head * PEER_TOPK
        e_ref[r0:r0 + PEER_TOPK, :] = best_e
        g_ref[r0:r0 + PEER_TOPK, :] = p / jnp.sum(p, axis=0, keepdims=True)


def _peer_route(x, mods, mod_row0, seq, nw, wq_bf16, keys_bf16):
    m, d = x.shape
    tile = min(ROW_TILE, seq)
    nq = wq_bf16.shape[1]
    return pl.pallas_call(
        functools.partial(_peer_route_kernel, d=d),
        grid=(m // tile,),
        in_specs=[pl.BlockSpec((tile, d), lambda i: (i, 0)),
                  _mod_spec(mod_row0, seq, tile, mods.shape[-1]),
                  pl.BlockSpec((1, d), lambda i: (0, 0)),
                  pl.BlockSpec((d, nq), lambda i: (0, 0)),
                  pl.BlockSpec(keys_bf16.shape, lambda i: (0, 0, 0))],
        out_specs=[pl.BlockSpec((tile, d), lambda i: (i, 0)),
                   pl.BlockSpec((PEER_PAIRS, tile), lambda i: (0, i)),
                   pl.BlockSpec((PEER_PAIRS, tile), lambda i: (0, i))],
        out_shape=[jax.ShapeDtypeStruct((m, d), F32),
                   jax.ShapeDtypeStruct((PEER_PAIRS, m), I32),
                   jax.ShapeDtypeStruct((PEER_PAIRS, m), F32)],
        compiler_params=_cparams("arbitrary"),
        name="peer_route",
    )(x, mods, nw.reshape(1, d), wq_bf16, keys_bf16)


def _peer_gather_kernel(idx_hbm, h_ref, g_ref, x_ref, m_ref, u_hbm, v_hbm, o_ref,
                        idx_smem, ubuf, vbuf, sem_idx, sem_u, sem_v, *, layer, d):
    blk = pl.program_id(0)
    n_groups = PEER_BLOCK // SUBLANES

    cp = pltpu.make_async_copy(idx_hbm.at[blk], idx_smem, sem_idx)
    cp.start()
    cp.wait()

    def issue(tok, slot):
        for pair in range(PEER_PAIRS):
            e = idx_smem[tok, pair]
            pltpu.make_async_copy(u_hbm.at[layer, pl.ds(e, 1)], ubuf.at[slot, pl.ds(pair, 1)],
                                  sem_u.at[slot]).start(priority=0)
            pltpu.make_async_copy(v_hbm.at[layer, pl.ds(e, 1)], vbuf.at[slot, pl.ds(pair, 1)],
                                  sem_v.at[slot]).start(priority=1)

    def wait(slot):
        pltpu.make_async_copy(u_hbm.at[layer, pl.ds(0, PEER_PAIRS)], ubuf.at[slot], sem_u.at[slot]).wait()
        pltpu.make_async_copy(v_hbm.at[layer, pl.ds(0, PEER_PAIRS)], vbuf.at[slot], sem_v.at[slot]).wait()

    gate2 = m_ref[0][:, 5 * d:6 * d]
    lane = lax.broadcasted_iota(I32, (PEER_PAIRS, PEER_BLOCK), 1)

    issue(0, 0)

    def group(grp, carry):
        base = pl.multiple_of(grp * SUBLANES, SUBLANES)
        h8 = h_ref[pl.ds(base, SUBLANES), :]
        rows = []
        for r in range(SUBLANES):
            tok = base + r
            slot = r % 2
            if r < SUBLANES - 1:
                issue(tok + 1, 1 - slot)
            else:
                @pl.when(grp < n_groups - 1)
                def _():
                    issue(tok + 1, 1 - slot)
            wait(slot)
            act = jnp.sum(ubuf[slot] * h8[r:r + 1, :], axis=1, keepdims=True)
            gate = jnp.sum(jnp.where(lane == tok, g_ref[...], 0.0), axis=1, keepdims=True)
            w = jax.nn.gelu(act) * gate
            rows.append(jnp.sum(vbuf[slot] * w, axis=0, keepdims=True))
        out8 = jnp.concatenate(rows, axis=0)
        o_ref[pl.ds(base, SUBLANES), :] = x_ref[pl.ds(base, SUBLANES), :] + gate2 * out8
        return carry

    lax.fori_loop(0, n_groups, group, 0)


def _peer_gather(idx, h, gates, x, mods, mod_row0, seq, p_u, p_v, layer, first_block):
    m, d = x.shape
    nblk = m // PEER_BLOCK - first_block
    idx3 = idx[:, first_block * PEER_BLOCK:].T.reshape(nblk, PEER_BLOCK, PEER_PAIRS)
    rows = pl.BlockSpec((PEER_BLOCK, d), lambda i: (i + first_block, 0))
    return pl.pallas_call(
        functools.partial(_peer_gather_kernel, layer=layer, d=d),
        grid=(nblk,),
        in_specs=[pl.BlockSpec(memory_space=pl.ANY),
                  rows,
                  pl.BlockSpec((PEER_PAIRS, PEER_BLOCK), lambda i: (0, i + first_block)),
                  rows,
                  _mod_spec(mod_row0, seq, PEER_BLOCK, mods.shape[-1], first_block),
                  pl.BlockSpec(memory_space=pl.ANY),
                  pl.BlockSpec(memory_space=pl.ANY)],
        out_specs=pl.BlockSpec((PEER_BLOCK, d), lambda i: (i, 0)),
        out_shape=jax.ShapeDtypeStruct((nblk * PEER_BLOCK, d), F32),
        scratch_shapes=[pltpu.SMEM((PEER_BLOCK, PEER_PAIRS), I32),
                        pltpu.VMEM((2, PEER_PAIRS, d), F32),
                        pltpu.VMEM((2, PEER_PAIRS, d), F32),
                        pltpu.SemaphoreType.DMA,
                        pltpu.SemaphoreType.DMA((2,)),
                        pltpu.SemaphoreType.DMA((2,))],
        compiler_params=_cparams("arbitrary"),
        name="peer_gather",
    )(idx3, h, gates, x, mods, p_u, p_v)


def _peer_experts_sc(idx, h, gates, u_rows, v_rows):
    m = idx.shape[0]
    d = h.shape[1]
    info = plsc.get_sparse_core_info()
    n_workers = info.num_cores * info.num_subcores
    per = m // n_workers
    n_chunks = PEER_PAIRS // SC_CHUNK
    n_vec = d // SC_LANES
    mesh = plsc.VectorSubcoreMesh(core_axis_name="c", subcore_axis_name="s")

    @functools.partial(
        pl.kernel, out_type=jax.ShapeDtypeStruct((m, d), F32), mesh=mesh,
        scratch_types=[pltpu.VMEM((2, PEER_PAIRS), I32), pltpu.VMEM((2, d), F32),
                       pltpu.VMEM((2, PEER_PAIRS), F32), pltpu.VMEM((2, d), F32),
                       pltpu.VMEM((2, SC_CHUNK, d), F32), pltpu.VMEM((2, SC_CHUNK, d), F32),
                       pltpu.SemaphoreType.DMA((2,)), pltpu.SemaphoreType.DMA((2,)),
                       pltpu.SemaphoreType.DMA((2,)), pltpu.SemaphoreType.DMA((2,))],
        compiler_params=pltpu.CompilerParams(needs_layout_passes=False),
        name="peer_experts_sc")
    def body(idx_hbm, h_hbm, g_hbm, u_hbm, v_hbm, o_hbm,
             idx_v, x_v, g_v, out_v, ubuf, vbuf, sem_meta, sem_out, sem_u, sem_v):
        wid = lax.axis_index("c") * info.num_subcores + lax.axis_index("s")
        tok0 = wid * per
        lane = lax.iota(I32, SC_LANES)

        def meta_copies(ti, ms):
            t = tok0 + ti
            return (pltpu.make_async_copy(idx_hbm.at[t], idx_v.at[ms], sem_meta.at[ms]),
                    pltpu.make_async_copy(h_hbm.at[t], x_v.at[ms], sem_meta.at[ms]),
                    pltpu.make_async_copy(g_hbm.at[t], g_v.at[ms], sem_meta.at[ms]))

        def gather_copies(ms, c, slot):
            ids = idx_v.at[ms, pl.ds(c * SC_CHUNK, SC_CHUNK)]
            return (pltpu.make_async_copy(u_hbm.at[ids], ubuf.at[slot], sem_u.at[slot]),
                    pltpu.make_async_copy(v_hbm.at[ids], vbuf.at[slot], sem_v.at[slot]))

        def out_copy(ti, ms):
            return pltpu.make_async_copy(out_v.at[ms], o_hbm.at[tok0 + ti], sem_out.at[ms])

        for cp in meta_copies(0, 0):
            cp.start()
        for cp in meta_copies(0, 0):
            cp.wait()
        for cp in gather_copies(0, 0, 0):
            cp.start()

        def token(ti, carry):
            ms = ti % 2
            nxt = 1 - ms

            @pl.when(ti + 1 < per)
            def _():
                for cp in meta_copies(ti + 1, nxt):
                    cp.start()

            @pl.when(ti >= 2)
            def _():
                out_copy(ti - 2, ms).wait()

            def zero(j, c):
                out_v[ms, pl.ds(j * SC_LANES, SC_LANES)] = jnp.zeros((SC_LANES,), F32)
                return c
            lax.fori_loop(0, n_vec, zero, 0)

            for c in range(n_chunks):
                slot = c % 2
                if c + 1 < n_chunks:
                    for cp in gather_copies(ms, c + 1, 1 - slot):
                        cp.start()
                else:
                    @pl.when(ti + 1 < per)
                    def _():
                        for cp in meta_copies(ti + 1, nxt):
                            cp.wait()
                        for cp in gather_copies(nxt, 0, 1 - slot):
                            cp.start()
                cu, cv = gather_copies(ms, c, slot)
                cu.wait()

                def udot(j, accs):
                    xj = x_v[ms, pl.ds(j * SC_LANES, SC_LANES)]
                    return tuple(accs[r] + ubuf[slot, r, pl.ds(j * SC_LANES, SC_LANES)] * xj
                                 for r in range(SC_CHUNK))
                accs = lax.fori_loop(0, n_vec, udot,
                                     tuple(jnp.zeros((SC_LANES,), F32) for _ in range(SC_CHUNK)))
                act = jnp.zeros((SC_LANES,), F32)
                for r in range(SC_CHUNK):
                    act = jnp.where(lane == r, jnp.sum(accs[r]), act)
                y = GELU_C * (act + 0.044715 * (act * act * act))
                w = act / (1.0 + jnp.exp(-2.0 * y)) * g_v[ms, pl.ds(c * SC_CHUNK, SC_CHUNK)]
                ws = [jnp.sum(jnp.where(lane == r, w, 0.0)) for r in range(SC_CHUNK)]
                cv.wait()

                @plsc.parallel_loop(0, n_vec, unroll=2)
                def _(j):
                    parts = [ws[r] * vbuf[slot, r, pl.ds(j * SC_LANES, SC_LANES)] for r in range(SC_CHUNK)]
                    while len(parts) > 1:
                        parts = [parts[i] + parts[i + 1] for i in range(0, len(parts), 2)]
                    plsc.addupdate(out_v.at[ms, pl.ds(j * SC_LANES, SC_LANES)], parts[0])

            out_copy(ti, ms).start()
            return carry

        lax.fori_loop(0, per, token, 0)
        for back in (2, 1):
            if per >= back:
                out_copy(per - back, (per - back) % 2).wait()

    return body(idx, h, gates, u_rows, v_rows)


def _residual_kernel(x_ref, y_ref, m_ref, o_ref, *, d):
    o_ref[...] = x_ref[...] + m_ref[0][:, 5 * d:6 * d] * y_ref[...]


def _residual(x, y, mods, mod_row0, seq):
    m, d = y.shape
    tile = min(ROW_TILE, seq)
    return pl.pallas_call(
        functools.partial(_residual_kernel, d=d),
        grid=(m // tile,),
        in_specs=[pl.BlockSpec((tile, d), lambda i: (i, 0)),
                  pl.BlockSpec((tile, d), lambda i: (i, 0)),
                  _mod_spec(mod_row0, seq, tile, mods.shape[-1])],
        out_specs=pl.BlockSpec((tile, d), lambda i: (i, 0)),
        out_shape=jax.ShapeDtypeStruct((m, d), F32),
        compiler_params=_cparams("arbitrary"),
        name="peer_residual",
    )(x, y, mods)


def _peer(x, mods, mod_row0, seq, layer, nw, wq_bf16, keys_bf16, p_u, p_v, sc_share):
    m, d = x.shape
    h, idx, gates = _peer_route(x, mods, mod_row0, seq, nw, wq_bf16, keys_bf16)
    n_blocks = m // PEER_BLOCK
    sc_blocks = (n_blocks * sc_share[0]) // sc_share[1]
    m_sc = sc_blocks * PEER_BLOCK
    n_experts = p_u.shape[1]
    idx_sc = idx[:, :m_sc].T + layer * n_experts
    y_sc = _peer_experts_sc(idx_sc, h, gates[:, :m_sc].T,
                            p_u.reshape(-1, d), p_v.reshape(-1, d))
    x_sc = _residual(x, y_sc, mods, mod_row0, seq)
    if sc_blocks == n_blocks:
        return x_sc
    x_tc = _peer_gather(idx, h, gates, x, mods, mod_row0, seq, p_u, p_v, layer, sc_blocks)
    return jnp.concatenate([x_sc, x_tc], axis=0)


def _head_mean_square(x):
    n = x.shape[1]
    r = lax.broadcasted_iota(I32, (n, n), 0) // HEAD_DIM
    c = lax.broadcasted_iota(I32, (n, n), 1) // HEAD_DIM
    seg = jnp.where(r == c, 1.0 / HEAD_DIM, 0.0).astype(F32)
    return jnp.dot(x * x, seg, precision=HIGHEST, preferred_element_type=F32)


def _swap_rot_halves(x):
    n = x.shape[1]
    quarter = HEAD_DIM // 4
    lane = lax.broadcasted_iota(I32, x.shape, 1)
    lo = (lane % (2 * quarter)) < quarter
    return jnp.where(lo, pltpu.roll(x, n - quarter, axis=1), pltpu.roll(x, quarter, axis=1))


def _qkprep_kernel(q_ref, k_ref, qw_ref, kw_ref, *rest, rope):
    if rope:
        cos_ref, sin_ref, qo_ref, ko_ref = rest
    else:
        qo_ref, ko_ref = rest
    q = q_ref[...]
    k = k_ref[...]
    q = q * lax.rsqrt(_head_mean_square(q) + NORM_EPS) * qw_ref[...]
    k = k * lax.rsqrt(_head_mean_square(k) + NORM_EPS) * kw_ref[...]
    if rope:
        cos = cos_ref[...]
        sin = sin_ref[...]
        cq = jnp.concatenate([cos] * (q.shape[1] // LANES), axis=1)
        sq = jnp.concatenate([sin] * (q.shape[1] // LANES), axis=1)
        q = q * cq + _swap_rot_halves(q) * sq
        k = k * cos + _swap_rot_halves(k) * sin
    qo_ref[...] = q
    ko_ref[...] = k


def _rope_tables(seq):
    axis_dim = HEAD_DIM // 2
    inv_freq = ROPE_THETA ** (-jnp.arange(0, axis_dim, 2, dtype=F32) / axis_dim)
    t = jnp.arange(seq)
    pos = jnp.stack([(t // GRID_W).astype(F32), (t % GRID_W).astype(F32)], axis=1)
    lane = jnp.arange(LANES)
    dd = lane % HEAD_DIM
    ang = pos[:, dd // axis_dim] * inv_freq[dd % (axis_dim // 2)][None, :]
    sign = jnp.where((dd % axis_dim) < axis_dim // 2, -1.0, 1.0).astype(F32)
    return jnp.cos(ang), jnp.sin(ang) * sign[None, :]


def _qkprep(q, k, qw, kw, seq, rope):
    m, nq = q.shape
    nk = k.shape[1]
    tile = min(ROW_TILE, seq)
    qw_row = jnp.tile(qw, nq // HEAD_DIM).reshape(1, nq)
    kw_row = jnp.tile(kw, nk // HEAD_DIM).reshape(1, nk)
    in_specs = [pl.BlockSpec((tile, nq), lambda i: (i, 0)),
                pl.BlockSpec((tile, nk), lambda i: (i, 0)),
                pl.BlockSpec((1, nq), lambda i: (0, 0)),
                pl.BlockSpec((1, nk), lambda i: (0, 0))]
    args = [q, k, qw_row, kw_row]
    if rope:
        cos, sin = _rope_tables(seq)
        per_seq = seq // tile
        in_specs += [pl.BlockSpec((tile, LANES), lambda i: (i % per_seq, 0)),
                     pl.BlockSpec((tile, LANES), lambda i: (i % per_seq, 0))]
        args += [cos, sin]
    return pl.pallas_call(
        functools.partial(_qkprep_kernel, rope=rope),
        grid=(m // tile,),
        in_specs=in_specs,
        out_specs=[pl.BlockSpec((tile, nq), lambda i: (i, 0)),
                   pl.BlockSpec((tile, nk), lambda i: (i, 0))],
        out_shape=[jax.ShapeDtypeStruct((m, nq), F32), jax.ShapeDtypeStruct((m, nk), F32)],
        compiler_params=_cparams("arbitrary"),
        name="qk_prep",
    )(*args)


def _dup_halves(x):
    lane = lax.broadcasted_iota(I32, x.shape, 1)
    sw = pltpu.roll(x, HEAD_DIM, axis=1)
    lo = lane < HEAD_DIM
    return jnp.where(lo, x, sw), jnp.where(lo, sw, x)


def _attend(q, k_all, v_all, sink_ref, mask):
    scale = HEAD_DIM ** -0.5
    nt = (((1,), (1,)), ((), ()))
    kk = [a.astype(BF16) for a in _dup_halves(k_all)]
    vv = [a.astype(BF16) for a in _dup_halves(v_all)]
    lane = lax.broadcasted_iota(I32, (q.shape[0], LANES), 1)
    lo = lane < HEAD_DIM
    tiles = []
    for t in range(q.shape[1] // LANES):
        qt = q[:, t * LANES:(t + 1) * LANES]
        g = (2 * t) // GQA_GROUP
        halves = []
        for hh in range(2):
            head = 2 * t + hh
            qm = jnp.where(lo if hh == 0 else ~lo, qt, 0.0).astype(BF16)
            s = lax.dot_general(qm, kk[g], nt, preferred_element_type=F32) * scale
            if mask is not None:
                s = jnp.where(mask, s, NEG_BIG)
            sink = sink_ref[head]
            mx = jnp.maximum(jnp.max(s, axis=1, keepdims=True), sink)
            p = jnp.exp(s - mx)
            den = jnp.sum(p, axis=1, keepdims=True) + jnp.exp(sink - mx)
            p = (p / den).astype(BF16)
            halves.append(jnp.dot(p, vv[g], preferred_element_type=F32))
        tiles.append(jnp.where(lo, halves[0], halves[1]))
    return jnp.concatenate(tiles, axis=1)


def _ctx_attn_kernel(sink_ref, q_ref, k_ref, v_ref, o_ref):
    o_ref[...] = _attend(q_ref[...], k_ref[...], v_ref[...], sink_ref, None)


def _ctx_attention(q, k, v, sink, seq):
    m, nq = q.shape
    nk = k.shape[1]
    return pl.pallas_call(
        _ctx_attn_kernel,
        grid=(m // seq,),
        in_specs=[pl.BlockSpec(memory_space=pltpu.SMEM),
                  pl.BlockSpec((seq, nq), lambda b: (b, 0)),
                  pl.BlockSpec((seq, nk), lambda b: (b, 0)),
                  pl.BlockSpec((seq, nk), lambda b: (b, 0))],
        out_specs=pl.BlockSpec((seq, nq), lambda b: (b, 0)),
        out_shape=jax.ShapeDtypeStruct((m, nq), F32),
        compiler_params=_cparams("arbitrary"),
        name="ctx_attention",
    )(sink, q, k, v)


def _lat_attn_kernel(sink_ref, q_ref, kc_ref, vc_ref, kp_ref, k0_ref, kn_ref, vp_ref, v0_ref, vn_ref,
                     o_ref, *, seq):
    qb = pl.program_id(1)
    blk = q_ref.shape[0]
    n_ctx = kc_ref.shape[1]
    k_all = jnp.concatenate([kc_ref[0], kp_ref[...], k0_ref[...], kn_ref[...]], axis=0)
    v_all = jnp.concatenate([vc_ref[0], vp_ref[...], v0_ref[...], vn_ref[...]], axis=0)
    tk = k_all.shape[0]
    qpos = qb * blk + lax.broadcasted_iota(I32, (blk, tk), 0)
    col = lax.broadcasted_iota(I32, (blk, tk), 1)
    kpos = (qb - 1) * blk + col - n_ctx
    local_ok = (jnp.abs(qpos - kpos) <= WINDOW) & (kpos >= 0) & (kpos < seq)
    mask = (col < n_ctx) | local_ok
    o_ref[...] = _attend(q_ref[...], k_all, v_all, sink_ref, mask)


def _lat_attention(q, k, v, k_ctx, v_ctx, sink, seq):
    m, nq = q.shape
    nk = k.shape[1]
    blk = WINDOW
    nb = seq // blk
    n_ctx = k_ctx.shape[1]
    last = m // blk - 1

    def kv_spec(shift):
        return pl.BlockSpec((blk, nk), lambda b, i: (jnp.clip(b * nb + i + shift, 0, last), 0))

    ctx_spec = pl.BlockSpec((1, n_ctx, nk), lambda b, i: (b, 0, 0))
    return pl.pallas_call(
        functools.partial(_lat_attn_kernel, seq=seq),
        grid=(m // seq, nb),
        in_specs=[pl.BlockSpec(memory_space=pltpu.SMEM),
                  pl.BlockSpec((blk, nq), lambda b, i: (b * nb + i, 0)),
                  ctx_spec, ctx_spec,
                  kv_spec(-1), kv_spec(0), kv_spec(1),
                  kv_spec(-1), kv_spec(0), kv_spec(1)],
        out_specs=pl.BlockSpec((blk, nq), lambda b, i: (b * nb + i, 0)),
        out_shape=jax.ShapeDtypeStruct((m, nq), F32),
        compiler_params=_cparams("arbitrary", "arbitrary"),
        name="lat_attention",
    )(sink, q, k_ctx, v_ctx, k, k, k, v, v, v)

SSD_BLOCK = 256
SSD_PAIRS = SSD_HEADS // 2
SSD_INNER = SSD_HEADS * SSD_HEAD_DIM
HALO = SUBLANES


def _softplus(x):
    return jnp.maximum(x, 0.0) + jnp.log1p(jnp.exp(-jnp.abs(x)))


def _silu(x):
    return x * jax.nn.sigmoid(x)


def _ssd_decays(dt_raw, bias, a_log):
    n = dt_raw.shape[0]
    dt = _softplus(dt_raw + bias)
    log_a = dt * (-jnp.exp(a_log))
    r = lax.broadcasted_iota(I32, (n, n), 0)
    c = lax.broadcasted_iota(I32, (n, n), 1)
    lower = jnp.where(c <= r, 1.0, 0.0).astype(F32)
    upper = jnp.where(r <= c, 1.0, 0.0).astype(F32)
    cum_col = jnp.dot(lower, log_a, precision=HIGHEST, preferred_element_type=F32)
    dt_row = dt.T
    la_row = log_a.T
    cum_row = jnp.dot(la_row, upper, precision=HIGHEST, preferred_element_type=F32)
    return dt, log_a, cum_col, dt_row, la_row, cum_row


def _ssd_scan_chunk(xs, bmat, cmat, w_of, q_scale_of, k_scale_of, carry_of, s_ref):
    nt = (((1,), (1,)), ((), ()))
    n = xs.shape[0]
    lane = lax.broadcasted_iota(I32, (n, LANES), 1)
    lo = lane < SSD_HEAD_DIM
    lane_s = lax.broadcasted_iota(I32, (D_STATE, LANES), 1)
    lo_s = lane_s < SSD_HEAD_DIM
    b_t = bmat.T
    cb16 = cmat.astype(BF16)
    ys = []
    for pair in range(SSD_PAIRS):
        g = (2 * pair) // (SSD_HEADS // SSD_GROUPS)
        in_g = (lane // D_STATE) == g
        cg = jnp.where(in_g, cmat, 0.0)
        cb = lax.dot_general(cg.astype(BF16), bmat.astype(BF16), nt, preferred_element_type=F32)
        x_pair = xs[:, pair * LANES:(pair + 1) * LANES]
        x16 = x_pair.astype(BF16)
        s_old = s_ref[pair]
        s2 = jnp.concatenate([s_old, s_old], axis=0).astype(BF16)
        bg_t = b_t[g * D_STATE:(g + 1) * D_STATE, :]
        y_h, s_h = [], []
        for hh in range(2):
            h = 2 * pair + hh
            w = (cb * w_of(h)).astype(BF16)
            y = jnp.dot(w, x16, preferred_element_type=F32)
            cq = (cg * q_scale_of(h)).astype(BF16)
            y = y + jnp.dot(cq, s2, preferred_element_type=F32)
            y_h.append(y)
            kt = (bg_t * k_scale_of(h)).astype(BF16)
            s_h.append(carry_of(h) * s_old + jnp.dot(kt, x16, preferred_element_type=F32))
        ys.append(jnp.where(lo, y_h[0], y_h[1]))
        s_ref[pair] = jnp.where(lo_s, s_h[0], s_h[1])
    return jnp.concatenate(ys, axis=1)


def _ssd_fwd_kernel(x_ref, xp_ref, xn_ref, dt_ref, s0_ref, cw_ref, cb_ref, bias_ref, alog_ref,
                    y_ref, xc_ref, sfin_ref, s_ref):
    c = pl.program_id(1)
    nc = pl.num_programs(1)
    n = x_ref.shape[0]

    @pl.when(c == 0)
    def _():
        s_ref[...] = s0_ref[0]

    prev = jnp.where(c > 0, xp_ref[...], 0.0)
    nxt = jnp.where(c < nc - 1, xn_ref[...], 0.0)
    xe = jnp.concatenate([prev, x_ref[...], nxt], axis=0)
    pad = (CONV_K - 1) // 2
    acc = cb_ref[...] + cw_ref[0:1, :] * xe[HALO - pad:HALO - pad + n, :]
    for k in range(1, CONV_K):
        acc = acc + cw_ref[k:k + 1, :] * xe[HALO - pad + k:HALO - pad + k + n, :]
    xc = _silu(acc)
    xc_ref[...] = xc
    xs = xc[:, :SSD_INNER]
    bmat = xc[:, SSD_INNER:SSD_INNER + LANES]
    cmat = xc[:, SSD_INNER + LANES:SSD_INNER + 2 * LANES]

    dt, log_a, cum_col, dt_row, la_row, cum_row = _ssd_decays(dt_ref[...], bias_ref[...], alog_ref[...])
    r = lax.broadcasted_iota(I32, (n, n), 0)
    cc = lax.broadcasted_iota(I32, (n, n), 1)
    causal = cc <= r
    last_col = cum_col[n - 1:n, :]

    def w_of(h):
        seg = cum_col[:, h:h + 1] - cum_row[h:h + 1, :]
        return jnp.exp(jnp.where(causal, seg, NEG_BIG)) * dt_row[h:h + 1, :]

    def q_scale_of(h):
        return jnp.exp(cum_col[:, h:h + 1])

    def k_scale_of(h):
        return dt_row[h:h + 1, :] * jnp.exp(cum_row[h:h + 1, n - 1:n] - cum_row[h:h + 1, :])

    def carry_of(h):
        return jnp.exp(last_col[:, h:h + 1])

    y_ref[...] = _ssd_scan_chunk(xs, bmat, cmat, w_of, q_scale_of, k_scale_of, carry_of, s_ref)

    @pl.when(c == nc - 1)
    def _():
        sfin_ref[0] = s_ref[...]


def _ssd_bwd_kernel(xc_ref, dt_ref, yf_ref, z_ref, s0_ref, bias_ref, alog_ref, dskip_ref, nw_ref,
                    y_ref, sfin_ref, s_ref):
    c = pl.program_id(1)
    nc = pl.num_programs(1)
    n = xc_ref.shape[0]

    @pl.when(c == 0)
    def _():
        s_ref[...] = s0_ref[0]

    xc = xc_ref[...]
    xs = xc[:, :SSD_INNER]
    bmat = xc[:, SSD_INNER:SSD_INNER + LANES]
    cmat = xc[:, SSD_INNER + LANES:SSD_INNER + 2 * LANES]
    dt, log_a, cum_col, dt_row, la_row, cum_row = _ssd_decays(dt_ref[...], bias_ref[...], alog_ref[...])
    ex_col = cum_col - log_a
    ex_row = cum_row - la_row
    r = lax.broadcasted_iota(I32, (n, n), 0)
    cc = lax.broadcasted_iota(I32, (n, n), 1)
    anti = cc >= r
    tot_col = cum_col[n - 1:n, :]
    off = SSD_HEADS

    def w_of(h):
        j = off + h
        seg = ex_row[j:j + 1, :] - ex_col[:, j:j + 1]
        return jnp.exp(jnp.where(anti, seg, NEG_BIG)) * dt_row[j:j + 1, :]

    def q_scale_of(h):
        j = off + h
        return jnp.exp(tot_col[:, j:j + 1] - ex_col[:, j:j + 1])

    def k_scale_of(h):
        j = off + h
        return dt_row[j:j + 1, :] * jnp.exp(ex_row[j:j + 1, :])

    def carry_of(h):
        j = off + h
        return jnp.exp(tot_col[:, j:j + 1])

    y_b = _ssd_scan_chunk(xs, bmat, cmat, w_of, q_scale_of, k_scale_of, carry_of, s_ref)
    y = yf_ref[...] + y_b + dskip_ref[...] * xs
    y = y * _silu(z_ref[...])
    ms = jnp.mean(y * y, axis=-1, keepdims=True)
    y_ref[...] = y * lax.rsqrt(ms + NORM_EPS) * nw_ref[...]

    @pl.when(c == nc - 1)
    def _():
        sfin_ref[0] = s_ref[...]


def _pair_states(s):
    b, h, n, p = s.shape
    return s.reshape(b, h // 2, 2, n, p).transpose(0, 1, 3, 2, 4).reshape(b, h // 2, n, 2 * p)


def _unpair_states(s):
    b, hp, n, p2 = s.shape
    return s.reshape(b, hp, n, 2, p2 // 2).transpose(0, 1, 3, 2, 4).reshape(b, hp * 2, n, p2 // 2)


def _ssd(xbc, dt, z, s0_f, s0_b, conv_w, conv_b, dt_bias, a_log, d_skip, ssd_norm, seq):
    m, nx = xbc.shape
    nb = m // seq
    blk = min(SSD_BLOCK, seq)
    nc = seq // blk
    hb = blk // HALO
    n_halo = m // HALO
    pad16 = lambda a: jnp.pad(a.reshape(1, -1), ((0, 0), (0, LANES - a.size)))
    bias = pad16(dt_bias)
    alog = pad16(a_log)
    state_spec = pl.BlockSpec((1, SSD_PAIRS, D_STATE, LANES), lambda b, c: (b, 0, 0, 0))
    state_shape = jax.ShapeDtypeStruct((nb, SSD_PAIRS, D_STATE, LANES), F32)
    row = lambda width: pl.BlockSpec((1, width), lambda b, c: (0, 0))

    def fwd_rows(width):
        return pl.BlockSpec((blk, width), lambda b, c: (b * nc + c, 0))

    def bwd_rows(width):
        return pl.BlockSpec((blk, width), lambda b, c: (b * nc + nc - 1 - c, 0))

    y_f, xc, s_f = pl.pallas_call(
        _ssd_fwd_kernel,
        grid=(nb, nc),
        in_specs=[fwd_rows(nx),
                  pl.BlockSpec((HALO, nx), lambda b, c: (jnp.maximum((b * nc + c) * hb - 1, 0), 0)),
                  pl.BlockSpec((HALO, nx), lambda b, c: (jnp.minimum((b * nc + c + 1) * hb, n_halo - 1), 0)),
                  fwd_rows(LANES), state_spec,
                  pl.BlockSpec((CONV_K, nx), lambda b, c: (0, 0)), row(nx), row(LANES), row(LANES)],
        out_specs=[fwd_rows(SSD_INNER), fwd_rows(nx), state_spec],
        out_shape=[jax.ShapeDtypeStruct((m, SSD_INNER), F32), jax.ShapeDtypeStruct((m, nx), F32), state_shape],
        scratch_shapes=[pltpu.VMEM((SSD_PAIRS, D_STATE, LANES), F32)],
        compiler_params=_cparams("arbitrary", "arbitrary"),
        name="ssd_forward",
    )(xbc, xbc, xbc, dt, _pair_states(s0_f), conv_w, conv_b.reshape(1, nx), bias, alog)

    dskip = jnp.repeat(d_skip, SSD_HEAD_DIM).reshape(1, SSD_INNER)
    y, s_b = pl.pallas_call(
        _ssd_bwd_kernel,
        grid=(nb, nc),
        in_specs=[bwd_rows(nx), bwd_rows(LANES), bwd_rows(SSD_INNER), bwd_rows(SSD_INNER), state_spec,
                  row(LANES), row(LANES), row(SSD_INNER), row(SSD_INNER)],
        out_specs=[bwd_rows(SSD_INNER), state_spec],
        out_shape=[jax.ShapeDtypeStruct((m, SSD_INNER), F32), state_shape],
        scratch_shapes=[pltpu.VMEM((SSD_PAIRS, D_STATE, LANES), F32)],
        compiler_params=_cparams("arbitrary", "arbitrary"),
        name="ssd_backward",
    )(xc, dt, y_f, z, _pair_states(s0_b), bias, alog, dskip, ssd_norm.reshape(1, SSD_INNER))
    return y, _unpair_states(s_f), _unpair_states(s_b)

def _hgrn_kernel(q_ref, ff_ref, fb_ref, i_ref, g_ref, lb_ref, s0_ref, nw_ref, o_ref, sfin_ref,
                 sf_ref, sb_ref, ob_ref, *, layer):
    t_len = q_ref.shape[0]
    n = HGRN_CHUNK
    n_chunks = t_len // n
    tn = (((0,), (0,)), ((), ()))
    nt = (((1,), (1,)), ((), ()))

    lbp = lb_ref[...]
    e = jnp.exp(lbp - jnp.max(lbp, axis=0, keepdims=True))
    sm = e / jnp.sum(e, axis=0, keepdims=True)
    lb = sm[0] * 0.0
    for j in range(1, layer + 1):
        lb = lb + sm[j]

    r = lax.broadcasted_iota(I32, (n, n), 0)
    c = lax.broadcasted_iota(I32, (n, n), 1)
    lower = jnp.where(c <= r, 1.0, 0.0).astype(F32)
    srow = lax.broadcasted_iota(I32, (n, HGRN_DK), 0)
    qscale = HGRN_DK ** -0.5

    def chunk(row0, f_ref, lb_d, reverse, s_ref):
        q = _silu(q_ref[pl.ds(row0, n), :]) * qscale
        f = f_ref[pl.ds(row0, n), :]
        v = i_ref[pl.ds(row0, n), :]
        k = (1.0 - lb_d) * jax.nn.sigmoid(-f)
        lf = jnp.log(lb_d + (1.0 - lb_d) * jax.nn.sigmoid(f))
        cum = jnp.dot(lower, lf, precision=HIGHEST, preferred_element_type=F32)
        tot = cum[n - 1:n, :]
        if reverse:
            cum = cum - lf
        rows = []
        for t in range(n):
            tile0 = (t // SUBLANES) * SUBLANES
            lo, hi = (tile0, n) if reverse else (0, tile0 + SUBLANES)
            cum_s = cum[lo:hi]
            if reverse:
                seg = jnp.where(srow[lo:hi] >= t, cum_s - cum[t:t + 1, :], NEG_BIG)
            else:
                seg = jnp.where(srow[lo:hi] <= t, cum[t:t + 1, :] - cum_s, NEG_BIG)
            a = q[t:t + 1, :] * k[lo:hi] * jnp.exp(seg)
            sc = jnp.sum(a, axis=1, keepdims=True)
            rows.append(jnp.sum(sc * v[lo:hi], axis=0, keepdims=True))
        o = jnp.concatenate(rows, axis=0)
        s_old = s_ref[...]
        if reverse:
            q_in = q * jnp.exp(tot - cum)
            k_out = k * jnp.exp(cum)
        else:
            q_in = q * jnp.exp(cum)
            k_out = k * jnp.exp(tot - cum)
        o = o + lax.dot_general(q_in.astype(BF16), s_old.astype(BF16), nt, preferred_element_type=F32)
        s_ref[...] = jnp.exp(tot) * s_old + lax.dot_general(
            v.astype(BF16), k_out.astype(BF16), tn, preferred_element_type=F32)
        return o

    sf_ref[...] = s0_ref[0, 0, 0].T
    sb_ref[...] = s0_ref[0, 1, 0].T

    def body(ci, carry):
        row_f = pl.multiple_of(ci * n, n)
        row_b = pl.multiple_of((n_chunks - 1 - ci) * n, n)
        o_ref[pl.ds(row_f, n), :] = chunk(row_f, ff_ref, lb[0:1, :], False, sf_ref)
        ob_ref[pl.ds(row_b, n), :] = chunk(row_b, fb_ref, lb[1:2, :], True, sb_ref)
        return carry

    lax.fori_loop(0, n_chunks, body, 0)
    sfin_ref[0, 0, 0] = sf_ref[...].T
    sfin_ref[0, 1, 0] = sb_ref[...].T

    nw = nw_ref[...]
    blk = min(t_len, ROW_TILE)

    def finish(bi, carry):
        row0 = pl.multiple_of(bi * blk, blk)
        o = o_ref[pl.ds(row0, blk), :] + ob_ref[pl.ds(row0, blk), :]
        ms = jnp.mean(o * o, axis=-1, keepdims=True)
        o = o * lax.rsqrt(ms + NORM_EPS) * nw
        o_ref[pl.ds(row0, blk), :] = o * _silu(g_ref[pl.ds(row0, blk), :])
        return carry

    lax.fori_loop(0, t_len // blk, finish, 0)


def _hgrn(q, f_fw, f_bw, iv, g, o_lb, state0, g_norm, seq, layer):
    m, width = q.shape
    nb = m // seq
    dv = width // HGRN_HEADS
    col = pl.BlockSpec((seq, dv), lambda b, h: (b, h))
    state_spec = pl.BlockSpec((1, 2, 1, HGRN_DK, dv), lambda b, h: (b, 0, h, 0, 0))
    return pl.pallas_call(
        functools.partial(_hgrn_kernel, layer=layer),
        grid=(nb, HGRN_HEADS),
        in_specs=[col, col, col, col, col,
                  pl.BlockSpec((o_lb.shape[0], 2, HGRN_DK), lambda b, h: (0, 0, h)),
                  state_spec,
                  pl.BlockSpec((1, dv), lambda b, h: (0, 0))],
        out_specs=[col, state_spec],
        out_shape=[jax.ShapeDtypeStruct((m, width), F32),
                   jax.ShapeDtypeStruct((nb, 2, HGRN_HEADS, HGRN_DK, dv), F32)],
        scratch_shapes=[pltpu.VMEM((dv, HGRN_DK), F32), pltpu.VMEM((dv, HGRN_DK), F32),
                        pltpu.VMEM((seq, dv), F32)],
        compiler_params=_cparams("arbitrary", "arbitrary"),
        name="hgrn2",
    )(q, f_fw, f_bw, iv, g, o_lb, state0, g_norm.reshape(1, dv))

EVEN_SPLITS = ((0, 512), (512, 640), (640, 768), (768, 1280), (1280, 2048), (2048, 2176))
HGRN_SPLITS = tuple((i * 1024, (i + 1) * 1024) for i in range(5))


def _even_weight(w):
    main = EVEN_SPLITS[-1][0]
    return jnp.pad(w, ((0, 0), (0, LANES - (w.shape[1] - main)))).astype(BF16)


def _run_trunk(x3, mods, mod_row0, P, cache):
    nb, seq, d = x3.shape
    x = x3.reshape(nb * seq, d)
    depth = P['norm_mix'].shape[0]
    ks, vs, ssd_states, hgrn_states = [], [], [], []
    for l in range(depth):
        j = l // 2
        row0 = (l * SUBLANES + mod_row0, 0 if cache is None else 1)
        if l % 2 == 0:
            q, k, v, z, xbc, dt = _inproj(x, mods, row0, seq, P['norm_mix'][l], P['e_w_in'][j],
                                          EVEN_SPLITS, "even_in_proj")
            q, k = _qkprep(q, k, P['e_q_norm'][j], P['e_k_norm'][j], seq, rope=cache is not None)
            if cache is None:
                s0_f = jnp.zeros((nb, SSD_HEADS, D_STATE, SSD_HEAD_DIM), F32)
                s0_b = s0_f
                o_attn = _ctx_attention(q, k, v, P['e_sink'][j], seq)
            else:
                s0_f, s0_b = cache[2][:, j, 0], cache[2][:, j, 1]
                n_ctx = cache[0].shape[2]
                o_attn = _lat_attention(q, k, v, cache[0][:, j].reshape(nb, n_ctx, -1),
                                        cache[1][:, j].reshape(nb, n_ctx, -1), P['e_sink'][j], seq)
            y, s_f, s_b = _ssd(xbc, dt, z, s0_f, s0_b, P['e_conv_w'][j], P['e_conv_b'][j],
                               P['e_dt_bias'][j], P['e_a_log'][j], P['e_d_skip'][j], P['e_ssd_norm'][j], seq)
            if cache is None:
                ks.append(k.reshape(nb, seq, N_KV_HEADS, HEAD_DIM))
                vs.append(v.reshape(nb, seq, N_KV_HEADS, HEAD_DIM))
                ssd_states.append(jnp.stack([s_f, s_b], axis=1))
            mix = jnp.concatenate([o_attn, y], axis=1)
            x = _outproj(mix, x, mods, row0, seq, P['e_w_out'][j], "even_out_proj")
        else:
            q, f_fw, f_bw, iv, g = _inproj(x, mods, row0, seq, P['norm_mix'][l], P['o_w_in'][j],
                                           HGRN_SPLITS, "odd_in_proj")
            if cache is None:
                s0 = jnp.zeros((nb, 2, HGRN_HEADS, HGRN_DK, d // HGRN_HEADS), F32)
            else:
                s0 = cache[3][:, j]
            o, s_new = _hgrn(q, f_fw, f_bw, iv, g, P['o_lb'], s0, P['o_g_norm'][j], seq, j)
            if cache is None:
                hgrn_states.append(s_new)
            x = _outproj(o, x, mods, row0, seq, P['o_w_out'][j], "odd_out_proj")
        x = _peer(x, mods, row0, seq, l, P['norm_ffn'][l], P['p_w_q'][l], P['p_sub_keys'][l],
                  P['p_u'], P['p_v'],
                  SC_SHARE_CONTEXT if cache is None else SC_SHARE_LATENT)
    y = x.reshape(nb, seq, d)
    if cache is not None:
        return y, None
    return y, (jnp.stack(ks, axis=1), jnp.stack(vs, axis=1),
               jnp.stack(ssd_states, axis=1), jnp.stack(hgrn_states, axis=1))


def kernel(x_prompt, x_sample, cache_k, cache_v, state_ssd, state_hgrn, c, c_ctx, w_ada, b_ada, norm_mix, norm_ffn, e_w_in, e_q_norm, e_k_norm, e_sink, e_conv_w, e_conv_b, e_dt_bias, e_a_log, e_d_skip, e_ssd_norm, e_w_out, o_w_in, o_lb, o_g_norm, o_w_out, p_w_q, p_sub_keys, p_u, p_v):
    depth, d, d6 = w_ada.shape
    b_lat = x_sample.shape[0]
    cond_rows = jnp.concatenate([c_ctx[None, :], c, jnp.zeros((SUBLANES - 1 - b_lat, d), F32)], axis=0)
    mods = _modulation(cond_rows, w_ada, b_ada).reshape(depth * SUBLANES, 1, d6)
    P = {
        'norm_mix': norm_mix, 'norm_ffn': norm_ffn,
        'e_w_in': jnp.stack([_even_weight(w) for w in e_w_in]), 'e_q_norm': e_q_norm, 'e_k_norm': e_k_norm,
        'e_sink': e_sink, 'e_conv_w': e_conv_w, 'e_conv_b': e_conv_b, 'e_dt_bias': e_dt_bias,
        'e_a_log': e_a_log, 'e_d_skip': e_d_skip, 'e_ssd_norm': e_ssd_norm,
        'e_w_out': e_w_out.astype(BF16),
        'o_w_in': o_w_in.astype(BF16), 'o_lb': o_lb, 'o_g_norm': o_g_norm, 'o_w_out': o_w_out.astype(BF16),
        'p_w_q': p_w_q.astype(BF16),
        'p_sub_keys': p_sub_keys.astype(BF16).reshape(depth, PEER_HEADS * 2, PEER_NKEYS, PEER_DKEY),
        'p_u': p_u, 'p_v': p_v,
    }
    y_prompt, new_state = _run_trunk(x_prompt, mods, 0, P, None)
    y_sample, _ = _run_trunk(x_sample, mods, 1, P, (cache_k, cache_v, state_ssd, state_hgrn))
    return (y_prompt, y_sample) + new_state
```

```python
import functools
import math

import jax
import jax.numpy as jnp
from jax import lax
from jax.experimental import pallas as pl
from jax.experimental.pallas import tpu as pltpu
from jax.experimental.pallas import tpu_sc as plsc

F32 = jnp.float32
BF16 = jnp.bfloat16
I32 = jnp.int32
HIGHEST = lax.Precision.HIGHEST

NORM_EPS = 1e-6
NEG_BIG = -1e30
LANES = 128
SUBLANES = 8
VMEM_LIMIT = 48 * 1024 * 1024

GRID_W = 64
HEAD_DIM = 64
N_Q_HEADS = 8
N_KV_HEADS = 2
GQA_GROUP = 4
WINDOW = 128
ROPE_THETA = 10000.0
SSD_HEADS = 8
SSD_HEAD_DIM = 64
SSD_GROUPS = 2
D_STATE = 64
CONV_K = 5
HGRN_HEADS = 8
HGRN_DK = 128
HGRN_CHUNK = 32
PEER_HEADS = 8
PEER_NKEYS = 128
PEER_TOPK = 16
PEER_DKEY = 128
PEER_PAIRS = PEER_HEADS * PEER_TOPK

ROW_TILE = 256
PEER_BLOCK = 128
SC_LANES = 16
SC_CHUNK = 16
SC_SHARE_CONTEXT = (1, 1)
SC_SHARE_LATENT = (13, 16)
GELU_C = math.sqrt(2.0 / math.pi)


def _cparams(*sem):
    return pltpu.CompilerParams(dimension_semantics=sem, vmem_limit_bytes=VMEM_LIMIT)


def _norm_mod(x, nw, scale, shift):
    ms = jnp.mean(x * x, axis=-1, keepdims=True)
    return (x * lax.rsqrt(ms + NORM_EPS)) * nw * (1.0 + scale) + shift


def _mod_kernel(c_ref, w_ref, b_ref, o_ref):
    c = c_ref[...]
    s = c * jax.nn.sigmoid(c)
    o_ref[0] = jnp.dot(s, w_ref[0], precision=HIGHEST, preferred_element_type=F32) + b_ref[0]


def _modulation(cond_rows, w_ada, b_ada):
    depth, d, n = w_ada.shape
    rows = cond_rows.shape[0]
    return pl.pallas_call(
        _mod_kernel,
        grid=(depth, n // d),
        in_specs=[pl.BlockSpec((rows, d), lambda l, j: (0, 0)),
                  pl.BlockSpec((1, d, d), lambda l, j: (l, 0, j)),
                  pl.BlockSpec((1, 1, d), lambda l, j: (l, 0, j))],
        out_specs=pl.BlockSpec((1, rows, d), lambda l, j: (l, 0, j)),
        out_shape=jax.ShapeDtypeStruct((depth, rows, n), F32),
        compiler_params=_cparams("arbitrary", "arbitrary"),
        name="modulation",
    )(cond_rows, w_ada, b_ada.reshape(depth, 1, n))


def _mod_spec(mod_row0, seq, tile, d6, first_tile=0):
    row0, per_batch = mod_row0
    return pl.BlockSpec((1, 1, d6),
                        lambda i: (row0 + per_batch * (((i + first_tile) * tile) // seq), 0, 0))


def _inproj_kernel(x_ref, m_ref, nw_ref, w_ref, *o_refs, splits, d):
    m = m_ref[0]
    h = _norm_mod(x_ref[...], nw_ref[...], m[:, d:2 * d], m[:, 0:d]).astype(BF16)
    for o_ref, (a, b) in zip(o_refs, splits):
        o_ref[...] = jnp.dot(h, w_ref[:, a:b], preferred_element_type=F32)


def _inproj(x, mods, mod_row0, seq, nw, w_bf16, splits, name):
    m, d = x.shape
    n = w_bf16.shape[1]
    tile = min(ROW_TILE, seq)
    return pl.pallas_call(
        functools.partial(_inproj_kernel, splits=splits, d=d),
        grid=(m // tile,),
        in_specs=[pl.BlockSpec((tile, d), lambda i: (i, 0)),
                  _mod_spec(mod_row0, seq, tile, mods.shape[-1]),
                  pl.BlockSpec((1, d), lambda i: (0, 0)),
                  pl.BlockSpec((d, n), lambda i: (0, 0))],
        out_specs=[pl.BlockSpec((tile, b - a), lambda i: (i, 0)) for a, b in splits],
        out_shape=[jax.ShapeDtypeStruct((m, b - a), F32) for a, b in splits],
        compiler_params=_cparams("arbitrary"),
        name=name,
    )(x, mods, nw.reshape(1, d), w_bf16)


def _outproj_kernel(mix_ref, x_ref, m_ref, w_ref, o_ref, *, d):
    y = jnp.dot(mix_ref[...].astype(BF16), w_ref[...], preferred_element_type=F32)
    o_ref[...] = x_ref[...] + m_ref[0][:, 2 * d:3 * d] * y


def _outproj(mix, x, mods, mod_row0, seq, w_bf16, name):
    m, d = x.shape
    k = mix.shape[1]
    tile = min(ROW_TILE, seq)
    return pl.pallas_call(
        functools.partial(_outproj_kernel, d=d),
        grid=(m // tile,),
        in_specs=[pl.BlockSpec((tile, k), lambda i: (i, 0)),
                  pl.BlockSpec((tile, d), lambda i: (i, 0)),
                  _mod_spec(mod_row0, seq, tile, mods.shape[-1]),
                  pl.BlockSpec((k, d), lambda i: (0, 0))],
        out_specs=pl.BlockSpec((tile, d), lambda i: (i, 0)),
        out_shape=jax.ShapeDtypeStruct((m, d), F32),
        compiler_params=_cparams("arbitrary"),
        name=name,
    )(mix, x, mods, w_bf16)


def _topk_over_rows(s, k, payload=None):
    n = s.shape[0]
    iota = lax.broadcasted_iota(I32, s.shape, 0)
    vals, idxs, pays = [], [], []
    for _ in range(k):
        m = jnp.max(s, axis=0, keepdims=True)
        i = jnp.min(jnp.where(s == m, iota, n), axis=0, keepdims=True)
        hit = iota == i
        vals.append(m)
        idxs.append(i)
        if payload is not None:
            pays.append(jnp.max(jnp.where(hit, payload, -1), axis=0, keepdims=True))
        s = jnp.where(hit, -jnp.inf, s)
    out = (jnp.concatenate(vals, axis=0), jnp.concatenate(idxs, axis=0))
    if payload is not None:
        out += (jnp.concatenate(pays, axis=0),)
    return out


def _peer_route_kernel(x_ref, m_ref, nw_ref, wq_ref, keys_ref, h_ref, e_ref, g_ref, *, d):
    m = m_ref[0]
    h = _norm_mod(x_ref[...], nw_ref[...], m[:, 4 * d:5 * d], m[:, 3 * d:4 * d])
    h_ref[...] = h
    hb = h.astype(BF16)
    nt = (((1,), (1,)), ((), ()))
    for head in range(PEER_HEADS):
        tops = []
        for half in range(2):
            c0 = (head * 2 + half) * PEER_DKEY
            q = jnp.dot(hb, wq_ref[:, c0:c0 + PEER_DKEY], preferred_element_type=F32)
            s = lax.dot_general(keys_ref[head * 2 + half], q.astype(BF16), nt,
                                preferred_element_type=F32)
            tops.append(_topk_over_rows(s, PEER_TOPK))
        (s0, i0), (s1, i1) = tops
        widths = [PEER_TOPK // (a + 1) for a in range(PEER_TOPK)]
        n_pad = -sum(widths) % SUBLANES
        cand_s = jnp.concatenate([s0[a:a + 1] + s1[:w] for a, w in enumerate(widths)]
                                 + [jnp.full((n_pad, s0.shape[1]), -jnp.inf, F32)], axis=0)
        cand_e = jnp.concatenate([i0[a:a + 1] * PEER_NKEYS + i1[:w] for a, w in enumerate(widths)]
                                 + [jnp.zeros((n_pad, s0.shape[1]), I32)], axis=0)
        best_s, _, best_e = _topk_over_rows(cand_s, PEER_TOPK, payload=cand_e)
        p = jnp.exp(best_s - best_s[0:1])
        r0 = head * PEER_TOPK
        e_ref[r0:r0 + PEER_TOPK, :] = best_e
        g_ref[r0:r0 + PEER_TOPK, :] = p / jnp.sum(p, axis=0, keepdims=True)


def _peer_route(x, mods, mod_row0, seq, nw, wq_bf16, keys_bf16):
    m, d = x.shape
    tile = min(ROW_TILE, seq)
    nq = wq_bf16.shape[1]
    return pl.pallas_call(
        functools.partial(_peer_route_kernel, d=d),
        grid=(m // tile,),
        in_specs=[pl.BlockSpec((tile, d), lambda i: (i, 0)),
                  _mod_spec(mod_row0, seq, tile, mods.shape[-1]),
                  pl.BlockSpec((1, d), lambda i: (0, 0)),
                  pl.BlockSpec((d, nq), lambda i: (0, 0)),
                  pl.BlockSpec(keys_bf16.shape, lambda i: (0, 0, 0))],
        out_specs=[pl.BlockSpec((tile, d), lambda i: (i, 0)),
                   pl.BlockSpec((PEER_PAIRS, tile), lambda i: (0, i)),
                   pl.BlockSpec((PEER_PAIRS, tile), lambda i: (0, i))],
        out_shape=[jax.ShapeDtypeStruct((m, d), F32),
                   jax.ShapeDtypeStruct((PEER_PAIRS, m), I32),
                   jax.ShapeDtypeStruct((PEER_PAIRS, m), F32)],
        compiler_params=_cparams("arbitrary"),
        name="peer_route",
    )(x, mods, nw.reshape(1, d), wq_bf16, keys_bf16)


def _peer_gather_kernel(idx_hbm, h_ref, g_ref, x_ref, m_ref, u_hbm, v_hbm, o_ref,
                        idx_smem, ubuf, vbuf, sem_idx, sem_u, sem_v, *, layer, d):
    blk = pl.program_id(0)
    n_groups = PEER_BLOCK // SUBLANES

    cp = pltpu.make_async_copy(idx_hbm.at[blk], idx_smem, sem_idx)
    cp.start()
    cp.wait()

    def issue(tok, slot):
        def body(pair, carry):
            e = idx_smem[tok, pair]
            pltpu.make_async_copy(u_hbm.at[layer, pl.ds(e, 1)], ubuf.at[slot, pl.ds(pair, 1)],
                                  sem_u.at[slot]).start()
            pltpu.make_async_copy(v_hbm.at[layer, pl.ds(e, 1)], vbuf.at[slot, pl.ds(pair, 1)],
                                  sem_v.at[slot]).start()
            return carry
        lax.fori_loop(0, PEER_PAIRS, body, 0, unroll=8)

    def wait(slot):
        pltpu.make_async_copy(u_hbm.at[layer, pl.ds(0, PEER_PAIRS)], ubuf.at[slot], sem_u.at[slot]).wait()
        pltpu.make_async_copy(v_hbm.at[layer, pl.ds(0, PEER_PAIRS)], vbuf.at[slot], sem_v.at[slot]).wait()

    gate2 = m_ref[0][:, 5 * d:6 * d]
    lane = lax.broadcasted_iota(I32, (PEER_PAIRS, PEER_BLOCK), 1)

    issue(0, 0)

    def group(grp, carry):
        base = pl.multiple_of(grp * SUBLANES, SUBLANES)
        h8 = h_ref[pl.ds(base, SUBLANES), :]
        rows = []
        for r in range(SUBLANES):
            tok = base + r
            slot = r % 2
            if r < SUBLANES - 1:
                issue(tok + 1, 1 - slot)
            else:
                @pl.when(grp < n_groups - 1)
                def _():
                    issue(tok + 1, 1 - slot)
            wait(slot)
            act = jnp.sum(ubuf[slot] * h8[r:r + 1, :], axis=1, keepdims=True)
            gate = jnp.sum(jnp.where(lane == tok, g_ref[...], 0.0), axis=1, keepdims=True)
            w = jax.nn.gelu(act) * gate
            rows.append(jnp.sum(vbuf[slot] * w, axis=0, keepdims=True))
        out8 = jnp.concatenate(rows, axis=0)
        o_ref[pl.ds(base, SUBLANES), :] = x_ref[pl.ds(base, SUBLANES), :] + gate2 * out8
        return carry

    lax.fori_loop(0, n_groups, group, 0)


def _peer_gather(idx, h, gates, x, mods, mod_row0, seq, p_u, p_v, layer, first_block):
    m, d = x.shape
    nblk = m // PEER_BLOCK - first_block
    idx3 = idx[:, first_block * PEER_BLOCK:].T.reshape(nblk, PEER_BLOCK, PEER_PAIRS)
    rows = pl.BlockSpec((PEER_BLOCK, d), lambda i: (i + first_block, 0))
    return pl.pallas_call(
        functools.partial(_peer_gather_kernel, layer=layer, d=d),
        grid=(nblk,),
        in_specs=[pl.BlockSpec(memory_space=pl.ANY),
                  rows,
                  pl.BlockSpec((PEER_PAIRS, PEER_BLOCK), lambda i: (0, i + first_block)),
                  rows,
                  _mod_spec(mod_row0, seq, PEER_BLOCK, mods.shape[-1], first_block),
                  pl.BlockSpec(memory_space=pl.ANY),
                  pl.BlockSpec(memory_space=pl.ANY)],
        out_specs=pl.BlockSpec((PEER_BLOCK, d), lambda i: (i, 0)),
        out_shape=jax.ShapeDtypeStruct((nblk * PEER_BLOCK, d), F32),
        scratch_shapes=[pltpu.SMEM((PEER_BLOCK, PEER_PAIRS), I32),
                        pltpu.VMEM((2, PEER_PAIRS, d), F32),
                        pltpu.VMEM((2, PEER_PAIRS, d), F32),
                        pltpu.SemaphoreType.DMA,
                        pltpu.SemaphoreType.DMA((2,)),
                        pltpu.SemaphoreType.DMA((2,))],
        compiler_params=_cparams("arbitrary"),
        name="peer_gather",
    )(idx3, h, gates, x, mods, p_u, p_v)


def _peer_experts_sc(idx, h, gates, u_rows, v_rows):
    m = idx.shape[0]
    d = h.shape[1]
    info = plsc.get_sparse_core_info()
    n_workers = info.num_cores * info.num_subcores
    per = m // n_workers
    n_chunks = PEER_PAIRS // SC_CHUNK
    n_vec = d // SC_LANES
    mesh = plsc.VectorSubcoreMesh(core_axis_name="c", subcore_axis_name="s")

    @functools.partial(
        pl.kernel, out_type=jax.ShapeDtypeStruct((m, d), F32), mesh=mesh,
        scratch_types=[pltpu.VMEM((2, PEER_PAIRS), I32), pltpu.VMEM((2, d), F32),
                       pltpu.VMEM((2, PEER_PAIRS), F32), pltpu.VMEM((2, d), F32),
                       pltpu.VMEM((2, SC_CHUNK, d), F32), pltpu.VMEM((2, SC_CHUNK, d), F32),
                       pltpu.SemaphoreType.DMA((2,)), pltpu.SemaphoreType.DMA((2,)),
                       pltpu.SemaphoreType.DMA((2,)), pltpu.SemaphoreType.DMA((2,))],
        compiler_params=pltpu.CompilerParams(needs_layout_passes=False),
        name="peer_experts_sc")
    def body(idx_hbm, h_hbm, g_hbm, u_hbm, v_hbm, o_hbm,
             idx_v, x_v, g_v, out_v, ubuf, vbuf, sem_meta, sem_out, sem_u, sem_v):
        wid = lax.axis_index("c") * info.num_subcores + lax.axis_index("s")
        tok0 = wid * per
        lane = lax.iota(I32, SC_LANES)

        def meta_copies(ti, ms):
            t = tok0 + ti
            return (pltpu.make_async_copy(idx_hbm.at[t], idx_v.at[ms], sem_meta.at[ms]),
                    pltpu.make_async_copy(h_hbm.at[t], x_v.at[ms], sem_meta.at[ms]),
                    pltpu.make_async_copy(g_hbm.at[t], g_v.at[ms], sem_meta.at[ms]))

        def gather_copies(ms, c, slot):
            ids = idx_v.at[ms, pl.ds(c * SC_CHUNK, SC_CHUNK)]
            return (pltpu.make_async_copy(u_hbm.at[ids], ubuf.at[slot], sem_u.at[slot]),
                    pltpu.make_async_copy(v_hbm.at[ids], vbuf.at[slot], sem_v.at[slot]))

        def out_copy(ti, ms):
            return pltpu.make_async_copy(out_v.at[ms], o_hbm.at[tok0 + ti], sem_out.at[ms])

        for cp in meta_copies(0, 0):
            cp.start()
        for cp in meta_copies(0, 0):
            cp.wait()
        for cp in gather_copies(0, 0, 0):
            cp.start()

        def token(ti, carry):
            ms = ti % 2
            nxt = 1 - ms

            @pl.when(ti + 1 < per)
            def _():
                for cp in meta_copies(ti + 1, nxt):
                    cp.start()

            @pl.when(ti >= 2)
            def _():
                out_copy(ti - 2, ms).wait()

            def zero(j, c):
                out_v[ms, pl.ds(j * SC_LANES, SC_LANES)] = jnp.zeros((SC_LANES,), F32)
                return c
            lax.fori_loop(0, n_vec, zero, 0)

            for c in range(n_chunks):
                slot = c % 2
                if c + 1 < n_chunks:
                    for cp in gather_copies(ms, c + 1, 1 - slot):
                        cp.start()
                else:
                    @pl.when(ti + 1 < per)
                    def _():
                        for cp in meta_copies(ti + 1, nxt):
                            cp.wait()
                        for cp in gather_copies(nxt, 0, 1 - slot):
                            cp.start()
                cu, cv = gather_copies(ms, c, slot)
                cu.wait()

                def udot(j, accs):
                    xj = x_v[ms, pl.ds(j * SC_LANES, SC_LANES)]
                    return tuple(accs[r] + ubuf[slot, r, pl.ds(j * SC_LANES, SC_LANES)] * xj
                                 for r in range(SC_CHUNK))
                accs = lax.fori_loop(0, n_vec, udot,
                                     tuple(jnp.zeros((SC_LANES,), F32) for _ in range(SC_CHUNK)))
                act = jnp.zeros((SC_LANES,), F32)
                for r in range(SC_CHUNK):
                    act = jnp.where(lane == r, jnp.sum(accs[r]), act)
                y = GELU_C * (act + 0.044715 * (act * act * act))
                w = act / (1.0 + jnp.exp(-2.0 * y)) * g_v[ms, pl.ds(c * SC_CHUNK, SC_CHUNK)]
                ws = [jnp.sum(jnp.where(lane == r, w, 0.0)) for r in range(SC_CHUNK)]
                cv.wait()

                @plsc.parallel_loop(0, n_vec, unroll=2)
                def _(j):
                    parts = [ws[r] * vbuf[slot, r, pl.ds(j * SC_LANES, SC_LANES)] for r in range(SC_CHUNK)]
                    while len(parts) > 1:
                        parts = [parts[i] + parts[i + 1] for i in range(0, len(parts), 2)]
                    plsc.addupdate(out_v.at[ms, pl.ds(j * SC_LANES, SC_LANES)], parts[0])

            out_copy(ti, ms).start()
            return carry

        lax.fori_loop(0, per, token, 0)
        for back in (2, 1):
            if per >= back:
                out_copy(per - back, (per - back) % 2).wait()

    return body(idx, h, gates, u_rows, v_rows)


def _residual_kernel(x_ref, y_ref, m_ref, o_ref, *, d):
    o_ref[...] = x_ref[...] + m_ref[0][:, 5 * d:6 * d] * y_ref[...]


def _residual(x, y, mods, mod_row0, seq):
    m, d = y.shape
    tile = min(ROW_TILE, seq)
    return pl.pallas_call(
        functools.partial(_residual_kernel, d=d),
        grid=(m // tile,),
        in_specs=[pl.BlockSpec((tile, d), lambda i: (i, 0)),
                  pl.BlockSpec((tile, d), lambda i: (i, 0)),
                  _mod_spec(mod_row0, seq, tile, mods.shape[-1])],
        out_specs=pl.BlockSpec((tile, d), lambda i: (i, 0)),
        out_shape=jax.ShapeDtypeStruct((m, d), F32),
        compiler_params=_cparams("arbitrary"),
        name="peer_residual",
    )(x, y, mods)


def _peer(x, mods, mod_row0, seq, layer, nw, wq_bf16, keys_bf16, p_u, p_v, sc_share, after=None):
    m, d = x.shape
    h, idx, gates = _peer_route(x, mods, mod_row0, seq, nw, wq_bf16, keys_bf16)
    n_blocks = m // PEER_BLOCK
    sc_blocks = (n_blocks * sc_share[0]) // sc_share[1]
    m_sc = sc_blocks * PEER_BLOCK
    n_experts = p_u.shape[1]
    idx_sc = idx[:, :m_sc].T + layer * n_experts
    if after is not None:
        idx_sc, _ = lax.optimization_barrier((idx_sc, after))
    y_sc = _peer_experts_sc(idx_sc, h, gates[:, :m_sc].T,
                            p_u.reshape(-1, d), p_v.reshape(-1, d))
    x_sc = _residual(x, y_sc, mods, mod_row0, seq)
    if sc_blocks == n_blocks:
        return x_sc, y_sc
    x_tc = _peer_gather(idx, h, gates, x, mods, mod_row0, seq, p_u, p_v, layer, sc_blocks)
    return jnp.concatenate([x_sc, x_tc], axis=0), y_sc


def _head_mean_square(x):
    n = x.shape[1]
    r = lax.broadcasted_iota(I32, (n, n), 0) // HEAD_DIM
    c = lax.broadcasted_iota(I32, (n, n), 1) // HEAD_DIM
    seg = jnp.where(r == c, 1.0 / HEAD_DIM, 0.0).astype(F32)
    return jnp.dot(x * x, seg, precision=HIGHEST, preferred_element_type=F32)


def _swap_rot_halves(x):
    n = x.shape[1]
    quarter = HEAD_DIM // 4
    lane = lax.broadcasted_iota(I32, x.shape, 1)
    lo = (lane % (2 * quarter)) < quarter
    return jnp.where(lo, pltpu.roll(x, n - quarter, axis=1), pltpu.roll(x, quarter, axis=1))


def _qkprep_kernel(q_ref, k_ref, qw_ref, kw_ref, *rest, rope):
    if rope:
        cos_ref, sin_ref, qo_ref, ko_ref = rest
    else:
        qo_ref, ko_ref = rest
    q = q_ref[...]
    k = k_ref[...]
    q = q * lax.rsqrt(_head_mean_square(q) + NORM_EPS) * qw_ref[...]
    k = k * lax.rsqrt(_head_mean_square(k) + NORM_EPS) * kw_ref[...]
    if rope:
        cos = cos_ref[...]
        sin = sin_ref[...]
        cq = jnp.concatenate([cos] * (q.shape[1] // LANES), axis=1)
        sq = jnp.concatenate([sin] * (q.shape[1] // LANES), axis=1)
        q = q * cq + _swap_rot_halves(q) * sq
        k = k * cos + _swap_rot_halves(k) * sin
    qo_ref[...] = q
    ko_ref[...] = k


def _rope_tables(seq):
    axis_dim = HEAD_DIM // 2
    inv_freq = ROPE_THETA ** (-jnp.arange(0, axis_dim, 2, dtype=F32) / axis_dim)
    t = jnp.arange(seq)
    pos = jnp.stack([(t // GRID_W).astype(F32), (t % GRID_W).astype(F32)], axis=1)
    lane = jnp.arange(LANES)
    dd = lane % HEAD_DIM
    ang = pos[:, dd // axis_dim] * inv_freq[dd % (axis_dim // 2)][None, :]
    sign = jnp.where((dd % axis_dim) < axis_dim // 2, -1.0, 1.0).astype(F32)
    return jnp.cos(ang), jnp.sin(ang) * sign[None, :]


def _qkprep(q, k, qw, kw, seq, rope):
    m, nq = q.shape
    nk = k.shape[1]
    tile = min(ROW_TILE, seq)
    qw_row = jnp.tile(qw, nq // HEAD_DIM).reshape(1, nq)
    kw_row = jnp.tile(kw, nk // HEAD_DIM).reshape(1, nk)
    in_specs = [pl.BlockSpec((tile, nq), lambda i: (i, 0)),
                pl.BlockSpec((tile, nk), lambda i: (i, 0)),
                pl.BlockSpec((1, nq), lambda i: (0, 0)),
                pl.BlockSpec((1, nk), lambda i: (0, 0))]
    args = [q, k, qw_row, kw_row]
    if rope:
        cos, sin = _rope_tables(seq)
        per_seq = seq // tile
        in_specs += [pl.BlockSpec((tile, LANES), lambda i: (i % per_seq, 0)),
                     pl.BlockSpec((tile, LANES), lambda i: (i % per_seq, 0))]
        args += [cos, sin]
    return pl.pallas_call(
        functools.partial(_qkprep_kernel, rope=rope),
        grid=(m // tile,),
        in_specs=in_specs,
        out_specs=[pl.BlockSpec((tile, nq), lambda i: (i, 0)),
                   pl.BlockSpec((tile, nk), lambda i: (i, 0))],
        out_shape=[jax.ShapeDtypeStruct((m, nq), F32), jax.ShapeDtypeStruct((m, nk), F32)],
        compiler_params=_cparams("arbitrary"),
        name="qk_prep",
    )(*args)


def _dup_halves(x):
    lane = lax.broadcasted_iota(I32, x.shape, 1)
    sw = pltpu.roll(x, HEAD_DIM, axis=1)
    lo = lane < HEAD_DIM
    return jnp.where(lo, x, sw), jnp.where(lo, sw, x)


def _attend(q, k_all, v_all, sink_ref, mask):
    scale = HEAD_DIM ** -0.5
    nt = (((1,), (1,)), ((), ()))
    kk = [a.astype(BF16) for a in _dup_halves(k_all)]
    vv = [a.astype(BF16) for a in _dup_halves(v_all)]
    lane = lax.broadcasted_iota(I32, (q.shape[0], LANES), 1)
    lo = lane < HEAD_DIM
    tiles = []
    for t in range(q.shape[1] // LANES):
        qt = q[:, t * LANES:(t + 1) * LANES]
        g = (2 * t) // GQA_GROUP
        halves = []
        for hh in range(2):
            head = 2 * t + hh
            qm = jnp.where(lo if hh == 0 else ~lo, qt, 0.0).astype(BF16)
            s = lax.dot_general(qm, kk[g], nt, preferred_element_type=F32) * scale
            if mask is not None:
                s = jnp.where(mask, s, NEG_BIG)
            sink = sink_ref[head]
            mx = jnp.maximum(jnp.max(s, axis=1, keepdims=True), sink)
            p = jnp.exp(s - mx)
            den = jnp.sum(p, axis=1, keepdims=True) + jnp.exp(sink - mx)
            p = (p / den).astype(BF16)
            halves.append(jnp.dot(p, vv[g], preferred_element_type=F32))
        tiles.append(jnp.where(lo, halves[0], halves[1]))
    return jnp.concatenate(tiles, axis=1)


def _ctx_attn_kernel(sink_ref, q_ref, k_ref, v_ref, o_ref):
    o_ref[...] = _attend(q_ref[...], k_ref[...], v_ref[...], sink_ref, None)


def _ctx_attention(q, k, v, sink, seq):
    m, nq = q.shape
    nk = k.shape[1]
    return pl.pallas_call(
        _ctx_attn_kernel,
        grid=(m // seq,),
        in_specs=[pl.BlockSpec(memory_space=pltpu.SMEM),
                  pl.BlockSpec((seq, nq), lambda b: (b, 0)),
                  pl.BlockSpec((seq, nk), lambda b: (b, 0)),
                  pl.BlockSpec((seq, nk), lambda b: (b, 0))],
        out_specs=pl.BlockSpec((seq, nq), lambda b: (b, 0)),
        out_shape=jax.ShapeDtypeStruct((m, nq), F32),
        compiler_params=_cparams("arbitrary"),
        name="ctx_attention",
    )(sink, q, k, v)


def _lat_attn_kernel(sink_ref, q_ref, kc_ref, vc_ref, kp_ref, k0_ref, kn_ref, vp_ref, v0_ref, vn_ref,
                     o_ref, *, seq):
    qb = pl.program_id(1)
    blk = q_ref.shape[0]
    n_ctx = kc_ref.shape[1]
    k_all = jnp.concatenate([kc_ref[0], kp_ref[...], k0_ref[...], kn_ref[...]], axis=0)
    v_all = jnp.concatenate([vc_ref[0], vp_ref[...], v0_ref[...], vn_ref[...]], axis=0)
    tk = k_all.shape[0]
    qpos = qb * blk + lax.broadcasted_iota(I32, (blk, tk), 0)
    col = lax.broadcasted_iota(I32, (blk, tk), 1)
    kpos = (qb - 1) * blk + col - n_ctx
    local_ok = (jnp.abs(qpos - kpos) <= WINDOW) & (kpos >= 0) & (kpos < seq)
    mask = (col < n_ctx) | local_ok
    o_ref[...] = _attend(q_ref[...], k_all, v_all, sink_ref, mask)


def _lat_attention(q, k, v, k_ctx, v_ctx, sink, seq):
    m, nq = q.shape
    nk = k.shape[1]
    blk = WINDOW
    nb = seq // blk
    n_ctx = k_ctx.shape[1]
    last = m // blk - 1

    def kv_spec(shift):
        return pl.BlockSpec((blk, nk), lambda b, i: (jnp.clip(b * nb + i + shift, 0, last), 0))

    ctx_spec = pl.BlockSpec((1, n_ctx, nk), lambda b, i: (b, 0, 0))
    return pl.pallas_call(
        functools.partial(_lat_attn_kernel, seq=seq),
        grid=(m // seq, nb),
        in_specs=[pl.BlockSpec(memory_space=pltpu.SMEM),
                  pl.BlockSpec((blk, nq), lambda b, i: (b * nb + i, 0)),
                  ctx_spec, ctx_spec,
                  kv_spec(-1), kv_spec(0), kv_spec(1),
                  kv_spec(-1), kv_spec(0), kv_spec(1)],
        out_specs=pl.BlockSpec((blk, nq), lambda b, i: (b * nb + i, 0)),
        out_shape=jax.ShapeDtypeStruct((m, nq), F32),
        compiler_params=_cparams("arbitrary", "arbitrary"),
        name="lat_attention",
    )(sink, q, k_ctx, v_ctx, k, k, k, v, v, v)

SSD_BLOCK = 256
SSD_PAIRS = SSD_HEADS // 2
SSD_INNER = SSD_HEADS * SSD_HEAD_DIM
HALO = SUBLANES


def _softplus(x):
    return jnp.maximum(x, 0.0) + jnp.log1p(jnp.exp(-jnp.abs(x)))


def _silu(x):
    return x * jax.nn.sigmoid(x)


def _ssd_decays(dt_raw, bias, a_log):
    n = dt_raw.shape[0]
    dt = _softplus(dt_raw + bias)
    log_a = dt * (-jnp.exp(a_log))
    r = lax.broadcasted_iota(I32, (n, n), 0)
    c = lax.broadcasted_iota(I32, (n, n), 1)
    lower = jnp.where(c <= r, 1.0, 0.0).astype(F32)
    upper = jnp.where(r <= c, 1.0, 0.0).astype(F32)
    cum_col = jnp.dot(lower, log_a, precision=HIGHEST, preferred_element_type=F32)
    dt_row = dt.T
    la_row = log_a.T
    cum_row = jnp.dot(la_row, upper, precision=HIGHEST, preferred_element_type=F32)
    return dt, log_a, cum_col, dt_row, la_row, cum_row


def _ssd_scan_chunk(xs, bmat, cmat, w_of, q_scale_of, k_scale_of, carry_of, s_ref):
    nt = (((1,), (1,)), ((), ()))
    n = xs.shape[0]
    lane = lax.broadcasted_iota(I32, (n, LANES), 1)
    lo = lane < SSD_HEAD_DIM
    lane_s = lax.broadcasted_iota(I32, (D_STATE, LANES), 1)
    lo_s = lane_s < SSD_HEAD_DIM
    b_t = bmat.T
    cb16 = cmat.astype(BF16)
    ys = []
    for pair in range(SSD_PAIRS):
        g = (2 * pair) // (SSD_HEADS // SSD_GROUPS)
        in_g = (lane // D_STATE) == g
        cg = jnp.where(in_g, cmat, 0.0)
        cb = lax.dot_general(cg.astype(BF16), bmat.astype(BF16), nt, preferred_element_type=F32)
        x_pair = xs[:, pair * LANES:(pair + 1) * LANES]
        x16 = x_pair.astype(BF16)
        s_old = s_ref[pair]
        s2 = jnp.concatenate([s_old, s_old], axis=0).astype(BF16)
        bg_t = b_t[g * D_STATE:(g + 1) * D_STATE, :]
        y_h, s_h = [], []
        for hh in range(2):
            h = 2 * pair + hh
            w = (cb * w_of(h)).astype(BF16)
            y = jnp.dot(w, x16, preferred_element_type=F32)
            cq = (cg * q_scale_of(h)).astype(BF16)
            y = y + jnp.dot(cq, s2, preferred_element_type=F32)
            y_h.append(y)
            kt = (bg_t * k_scale_of(h)).astype(BF16)
            s_h.append(carry_of(h) * s_old + jnp.dot(kt, x16, preferred_element_type=F32))
        ys.append(jnp.where(lo, y_h[0], y_h[1]))
        s_ref[pair] = jnp.where(lo_s, s_h[0], s_h[1])
    return jnp.concatenate(ys, axis=1)


def _ssd_fwd_kernel(x_ref, xp_ref, xn_ref, dt_ref, s0_ref, cw_ref, cb_ref, bias_ref, alog_ref,
                    y_ref, xc_ref, sfin_ref, s_ref):
    c = pl.program_id(1)
    nc = pl.num_programs(1)
    n = x_ref.shape[0]

    @pl.when(c == 0)
    def _():
        s_ref[...] = s0_ref[0]

    prev = jnp.where(c > 0, xp_ref[...], 0.0)
    nxt = jnp.where(c < nc - 1, xn_ref[...], 0.0)
    xe = jnp.concatenate([prev, x_ref[...], nxt], axis=0)
    pad = (CONV_K - 1) // 2
    acc = cb_ref[...] + cw_ref[0:1, :] * xe[HALO - pad:HALO - pad + n, :]
    for k in range(1, CONV_K):
        acc = acc + cw_ref[k:k + 1, :] * xe[HALO - pad + k:HALO - pad + k + n, :]
    xc = _silu(acc)
    xc_ref[...] = xc
    xs = xc[:, :SSD_INNER]
    bmat = xc[:, SSD_INNER:SSD_INNER + LANES]
    cmat = xc[:, SSD_INNER + LANES:SSD_INNER + 2 * LANES]

    dt, log_a, cum_col, dt_row, la_row, cum_row = _ssd_decays(dt_ref[...], bias_ref[...], alog_ref[...])
    r = lax.broadcasted_iota(I32, (n, n), 0)
    cc = lax.broadcasted_iota(I32, (n, n), 1)
    causal = cc <= r
    last_col = cum_col[n - 1:n, :]

    def w_of(h):
        seg = cum_col[:, h:h + 1] - cum_row[h:h + 1, :]
        return jnp.exp(jnp.where(causal, seg, NEG_BIG)) * dt_row[h:h + 1, :]

    def q_scale_of(h):
        return jnp.exp(cum_col[:, h:h + 1])

    def k_scale_of(h):
        return dt_row[h:h + 1, :] * jnp.exp(cum_row[h:h + 1, n - 1:n] - cum_row[h:h + 1, :])

    def carry_of(h):
        return jnp.exp(last_col[:, h:h + 1])

    y_ref[...] = _ssd_scan_chunk(xs, bmat, cmat, w_of, q_scale_of, k_scale_of, carry_of, s_ref)

    @pl.when(c == nc - 1)
    def _():
        sfin_ref[0] = s_ref[...]


def _ssd_bwd_kernel(xc_ref, dt_ref, yf_ref, z_ref, s0_ref, bias_ref, alog_ref, dskip_ref, nw_ref,
                    y_ref, sfin_ref, s_ref):
    c = pl.program_id(1)
    nc = pl.num_programs(1)
    n = xc_ref.shape[0]

    @pl.when(c == 0)
    def _():
        s_ref[...] = s0_ref[0]

    xc = xc_ref[...]
    xs = xc[:, :SSD_INNER]
    bmat = xc[:, SSD_INNER:SSD_INNER + LANES]
    cmat = xc[:, SSD_INNER + LANES:SSD_INNER + 2 * LANES]
    dt, log_a, cum_col, dt_row, la_row, cum_row = _ssd_decays(dt_ref[...], bias_ref[...], alog_ref[...])
    ex_col = cum_col - log_a
    ex_row = cum_row - la_row
    r = lax.broadcasted_iota(I32, (n, n), 0)
    cc = lax.broadcasted_iota(I32, (n, n), 1)
    anti = cc >= r
    tot_col = cum_col[n - 1:n, :]
    off = SSD_HEADS

    def w_of(h):
        j = off + h
        seg = ex_row[j:j + 1, :] - ex_col[:, j:j + 1]
        return jnp.exp(jnp.where(anti, seg, NEG_BIG)) * dt_row[j:j + 1, :]

    def q_scale_of(h):
        j = off + h
        return jnp.exp(tot_col[:, j:j + 1] - ex_col[:, j:j + 1])

    def k_scale_of(h):
        j = off + h
        return dt_row[j:j + 1, :] * jnp.exp(ex_row[j:j + 1, :])

    def carry_of(h):
        j = off + h
        return jnp.exp(tot_col[:, j:j + 1])

    y_b = _ssd_scan_chunk(xs, bmat, cmat, w_of, q_scale_of, k_scale_of, carry_of, s_ref)
    y = yf_ref[...] + y_b + dskip_ref[...] * xs
    y = y * _silu(z_ref[...])
    ms = jnp.mean(y * y, axis=-1, keepdims=True)
    y_ref[...] = y * lax.rsqrt(ms + NORM_EPS) * nw_ref[...]

    @pl.when(c == nc - 1)
    def _():
        sfin_ref[0] = s_ref[...]


def _pair_states(s):
    b, h, n, p = s.shape
    return s.reshape(b, h // 2, 2, n, p).transpose(0, 1, 3, 2, 4).reshape(b, h // 2, n, 2 * p)


def _unpair_states(s):
    b, hp, n, p2 = s.shape
    return s.reshape(b, hp, n, 2, p2 // 2).transpose(0, 1, 3, 2, 4).reshape(b, hp * 2, n, p2 // 2)


def _ssd(xbc, dt, z, s0_f, s0_b, conv_w, conv_b, dt_bias, a_log, d_skip, ssd_norm, seq):
    m, nx = xbc.shape
    nb = m // seq
    blk = min(SSD_BLOCK, seq)
    nc = seq // blk
    hb = blk // HALO
    n_halo = m // HALO
    pad16 = lambda a: jnp.pad(a.reshape(1, -1), ((0, 0), (0, LANES - a.size)))
    bias = pad16(dt_bias)
    alog = pad16(a_log)
    state_spec = pl.BlockSpec((1, SSD_PAIRS, D_STATE, LANES), lambda b, c: (b, 0, 0, 0))
    state_shape = jax.ShapeDtypeStruct((nb, SSD_PAIRS, D_STATE, LANES), F32)
    row = lambda width: pl.BlockSpec((1, width), lambda b, c: (0, 0))

    def fwd_rows(width):
        return pl.BlockSpec((blk, width), lambda b, c: (b * nc + c, 0))

    def bwd_rows(width):
        return pl.BlockSpec((blk, width), lambda b, c: (b * nc + nc - 1 - c, 0))

    y_f, xc, s_f = pl.pallas_call(
        _ssd_fwd_kernel,
        grid=(nb, nc),
        in_specs=[fwd_rows(nx),
                  pl.BlockSpec((HALO, nx), lambda b, c: (jnp.maximum((b * nc + c) * hb - 1, 0), 0)),
                  pl.BlockSpec((HALO, nx), lambda b, c: (jnp.minimum((b * nc + c + 1) * hb, n_halo - 1), 0)),
                  fwd_rows(LANES), state_spec,
                  pl.BlockSpec((CONV_K, nx), lambda b, c: (0, 0)), row(nx), row(LANES), row(LANES)],
        out_specs=[fwd_rows(SSD_INNER), fwd_rows(nx), state_spec],
        out_shape=[jax.ShapeDtypeStruct((m, SSD_INNER), F32), jax.ShapeDtypeStruct((m, nx), F32), state_shape],
        scratch_shapes=[pltpu.VMEM((SSD_PAIRS, D_STATE, LANES), F32)],
        compiler_params=_cparams("arbitrary", "arbitrary"),
        name="ssd_forward",
    )(xbc, xbc, xbc, dt, _pair_states(s0_f), conv_w, conv_b.reshape(1, nx), bias, alog)

    dskip = jnp.repeat(d_skip, SSD_HEAD_DIM).reshape(1, SSD_INNER)
    y, s_b = pl.pallas_call(
        _ssd_bwd_kernel,
        grid=(nb, nc),
        in_specs=[bwd_rows(nx), bwd_rows(LANES), bwd_rows(SSD_INNER), bwd_rows(SSD_INNER), state_spec,
                  row(LANES), row(LANES), row(SSD_INNER), row(SSD_INNER)],
        out_specs=[bwd_rows(SSD_INNER), state_spec],
        out_shape=[jax.ShapeDtypeStruct((m, SSD_INNER), F32), state_shape],
        scratch_shapes=[pltpu.VMEM((SSD_PAIRS, D_STATE, LANES), F32)],
        compiler_params=_cparams("arbitrary", "arbitrary"),
        name="ssd_backward",
    )(xc, dt, y_f, z, _pair_states(s0_b), bias, alog, dskip, ssd_norm.reshape(1, SSD_INNER))
    return y, _unpair_states(s_f), _unpair_states(s_b)

def _hgrn_kernel(q_ref, ff_ref, fb_ref, i_ref, g_ref, lb_ref, s0_ref, nw_ref, o_ref, sfin_ref,
                 sf_ref, sb_ref, ob_ref, *, layer):
    t_len = q_ref.shape[0]
    n = HGRN_CHUNK
    n_chunks = t_len // n
    tn = (((0,), (0,)), ((), ()))
    nt = (((1,), (1,)), ((), ()))

    lbp = lb_ref[...]
    e = jnp.exp(lbp - jnp.max(lbp, axis=0, keepdims=True))
    sm = e / jnp.sum(e, axis=0, keepdims=True)
    lb = sm[0] * 0.0
    for j in range(1, layer + 1):
        lb = lb + sm[j]

    r = lax.broadcasted_iota(I32, (n, n), 0)
    c = lax.broadcasted_iota(I32, (n, n), 1)
    lower = jnp.where(c <= r, 1.0, 0.0).astype(F32)
    srow = lax.broadcasted_iota(I32, (n, HGRN_DK), 0)
    qscale = HGRN_DK ** -0.5

    def chunk(row0, f_ref, lb_d, reverse, s_ref):
        q = _silu(q_ref[pl.ds(row0, n), :]) * qscale
        f = f_ref[pl.ds(row0, n), :]
        v = i_ref[pl.ds(row0, n), :]
        k = (1.0 - lb_d) * jax.nn.sigmoid(-f)
        lf = jnp.log(lb_d + (1.0 - lb_d) * jax.nn.sigmoid(f))
        cum = jnp.dot(lower, lf, precision=HIGHEST, preferred_element_type=F32)
        tot = cum[n - 1:n, :]
        if reverse:
            cum = cum - lf
        rows = []
        for t in range(n):
            tile0 = (t // SUBLANES) * SUBLANES
            lo, hi = (tile0, n) if reverse else (0, tile0 + SUBLANES)
            cum_s = cum[lo:hi]
            if reverse:
                seg = jnp.where(srow[lo:hi] >= t, cum_s - cum[t:t + 1, :], NEG_BIG)
            else:
                seg = jnp.where(srow[lo:hi] <= t, cum[t:t + 1, :] - cum_s, NEG_BIG)
            a = q[t:t + 1, :] * k[lo:hi] * jnp.exp(seg)
            sc = jnp.sum(a, axis=1, keepdims=True)
            rows.append(jnp.sum(sc * v[lo:hi], axis=0, keepdims=True))
        o = jnp.concatenate(rows, axis=0)
        s_old = s_ref[...]
        if reverse:
            q_in = q * jnp.exp(tot - cum)
            k_out = k * jnp.exp(cum)
        else:
            q_in = q * jnp.exp(cum)
            k_out = k * jnp.exp(tot - cum)
        o = o + lax.dot_general(q_in.astype(BF16), s_old.astype(BF16), nt, preferred_element_type=F32)
        s_ref[...] = jnp.exp(tot) * s_old + lax.dot_general(
            v.astype(BF16), k_out.astype(BF16), tn, preferred_element_type=F32)
        return o

    sf_ref[...] = s0_ref[0, 0, 0].T
    sb_ref[...] = s0_ref[0, 1, 0].T

    def body(ci, carry):
        row_f = pl.multiple_of(ci * n, n)
        row_b = pl.multiple_of((n_chunks - 1 - ci) * n, n)
        o_ref[pl.ds(row_f, n), :] = chunk(row_f, ff_ref, lb[0:1, :], False, sf_ref)
        ob_ref[pl.ds(row_b, n), :] = chunk(row_b, fb_ref, lb[1:2, :], True, sb_ref)
        return carry

    lax.fori_loop(0, n_chunks, body, 0)
    sfin_ref[0, 0, 0] = sf_ref[...].T
    sfin_ref[0, 1, 0] = sb_ref[...].T

    nw = nw_ref[...]
    blk = min(t_len, ROW_TILE)

    def finish(bi, carry):
        row0 = pl.multiple_of(bi * blk, blk)
        o = o_ref[pl.ds(row0, blk), :] + ob_ref[pl.ds(row0, blk), :]
        ms = jnp.mean(o * o, axis=-1, keepdims=True)
        o = o * lax.rsqrt(ms + NORM_EPS) * nw
        o_ref[pl.ds(row0, blk), :] = o * _silu(g_ref[pl.ds(row0, blk), :])
        return carry

    lax.fori_loop(0, t_len // blk, finish, 0)


def _hgrn(q, f_fw, f_bw, iv, g, o_lb, state0, g_norm, seq, layer):
    m, width = q.shape
    nb = m // seq
    dv = width // HGRN_HEADS
    col = pl.BlockSpec((seq, dv), lambda b, h: (b, h))
    state_spec = pl.BlockSpec((1, 2, 1, HGRN_DK, dv), lambda b, h: (b, 0, h, 0, 0))
    return pl.pallas_call(
        functools.partial(_hgrn_kernel, layer=layer),
        grid=(nb, HGRN_HEADS),
        in_specs=[col, col, col, col, col,
                  pl.BlockSpec((o_lb.shape[0], 2, HGRN_DK), lambda b, h: (0, 0, h)),
                  state_spec,
                  pl.BlockSpec((1, dv), lambda b, h: (0, 0))],
        out_specs=[col, state_spec],
        out_shape=[jax.ShapeDtypeStruct((m, width), F32),
                   jax.ShapeDtypeStruct((nb, 2, HGRN_HEADS, HGRN_DK, dv), F32)],
        scratch_shapes=[pltpu.VMEM((dv, HGRN_DK), F32), pltpu.VMEM((dv, HGRN_DK), F32),
                        pltpu.VMEM((seq, dv), F32)],
        compiler_params=_cparams("arbitrary", "arbitrary"),
        name="hgrn2",
    )(q, f_fw, f_bw, iv, g, o_lb, state0, g_norm.reshape(1, dv))

EVEN_SPLITS = ((0, 512), (512, 640), (640, 768), (768, 1280), (1280, 2048), (2048, 2176))
HGRN_SPLITS = tuple((i * 1024, (i + 1) * 1024) for i in range(5))


def _even_weight(w):
    main = EVEN_SPLITS[-1][0]
    return jnp.pad(w, ((0, 0), (0, LANES - (w.shape[1] - main)))).astype(BF16)


def _run_trunk(x3, mods, mod_row0, P, cache, sc_order=None):
    nb, seq, d = x3.shape
    x = x3.reshape(nb * seq, d)
    depth = P['norm_mix'].shape[0]
    ks, vs, ssd_states, hgrn_states = [], [], [], []
    for l in range(depth):
        j = l // 2
        row0 = (l * SUBLANES + mod_row0, 0 if cache is None else 1)
        if l % 2 == 0:
            q, k, v, z, xbc, dt = _inproj(x, mods, row0, seq, P['norm_mix'][l], P['e_w_in'][j],
                                          EVEN_SPLITS, "even_in_proj")
            q, k = _qkprep(q, k, P['e_q_norm'][j], P['e_k_norm'][j], seq, rope=cache is not None)
            if cache is None:
                s0_f = jnp.zeros((nb, SSD_HEADS, D_STATE, SSD_HEAD_DIM), F32)
                s0_b = s0_f
                o_attn = _ctx_attention(q, k, v, P['e_sink'][j], seq)
            else:
                s0_f, s0_b = cache[2][:, j, 0], cache[2][:, j, 1]
                n_ctx = cache[0].shape[2]
                o_attn = _lat_attention(q, k, v, cache[0][:, j].reshape(nb, n_ctx, -1),
                                        cache[1][:, j].reshape(nb, n_ctx, -1), P['e_sink'][j], seq)
            y, s_f, s_b = _ssd(xbc, dt, z, s0_f, s0_b, P['e_conv_w'][j], P['e_conv_b'][j],
                               P['e_dt_bias'][j], P['e_a_log'][j], P['e_d_skip'][j], P['e_ssd_norm'][j], seq)
            if cache is None:
                ks.append(k.reshape(nb, seq, N_KV_HEADS, HEAD_DIM))
                vs.append(v.reshape(nb, seq, N_KV_HEADS, HEAD_DIM))
                ssd_states.append(jnp.stack([s_f, s_b], axis=1))
            mix = jnp.concatenate([o_attn, y], axis=1)
            x = _outproj(mix, x, mods, row0, seq, P['e_w_out'][j], "even_out_proj")
        else:
            q, f_fw, f_bw, iv, g = _inproj(x, mods, row0, seq, P['norm_mix'][l], P['o_w_in'][j],
                                           HGRN_SPLITS, "odd_in_proj")
            if cache is None:
                s0 = jnp.zeros((nb, 2, HGRN_HEADS, HGRN_DK, d // HGRN_HEADS), F32)
            else:
                s0 = cache[3][:, j]
            o, s_new = _hgrn(q, f_fw, f_bw, iv, g, P['o_lb'], s0, P['o_g_norm'][j], seq, j)
            if cache is None:
                hgrn_states.append(s_new)
            x = _outproj(o, x, mods, row0, seq, P['o_w_out'][j], "odd_out_proj")
        x, y_sc = _peer(x, mods, row0, seq, l, P['norm_ffn'][l], P['p_w_q'][l], P['p_sub_keys'][l],
                        P['p_u'], P['p_v'],
                        SC_SHARE_CONTEXT if cache is None else SC_SHARE_LATENT,
                        after=None if sc_order is None or cache is None else sc_order[l])
        if sc_order is not None and cache is None:
            sc_order.append(y_sc)
    y = x.reshape(nb, seq, d)
    if cache is not None:
        return y, None
    return y, (jnp.stack(ks, axis=1), jnp.stack(vs, axis=1),
               jnp.stack(ssd_states, axis=1), jnp.stack(hgrn_states, axis=1))


def kernel(x_prompt, x_sample, cache_k, cache_v, state_ssd, state_hgrn, c, c_ctx, w_ada, b_ada, norm_mix, norm_ffn, e_w_in, e_q_norm, e_k_norm, e_sink, e_conv_w, e_conv_b, e_dt_bias, e_a_log, e_d_skip, e_ssd_norm, e_w_out, o_w_in, o_lb, o_g_norm, o_w_out, p_w_q, p_sub_keys, p_u, p_v):
    depth, d, d6 = w_ada.shape
    b_lat = x_sample.shape[0]
    cond_rows = jnp.concatenate([c_ctx[None, :], c, jnp.zeros((SUBLANES - 1 - b_lat, d), F32)], axis=0)
    mods = _modulation(cond_rows, w_ada, b_ada).reshape(depth * SUBLANES, 1, d6)
    P = {
        'norm_mix': norm_mix, 'norm_ffn': norm_ffn,
        'e_w_in': jnp.stack([_even_weight(w) for w in e_w_in]), 'e_q_norm': e_q_norm, 'e_k_norm': e_k_norm,
        'e_sink': e_sink, 'e_conv_w': e_conv_w, 'e_conv_b': e_conv_b, 'e_dt_bias': e_dt_bias,
        'e_a_log': e_a_log, 'e_d_skip': e_d_skip, 'e_ssd_norm': e_ssd_norm,
        'e_w_out': e_w_out.astype(BF16),
        'o_w_in': o_w_in.astype(BF16), 'o_lb': o_lb, 'o_g_norm': o_g_norm, 'o_w_out': o_w_out.astype(BF16),
        'p_w_q': p_w_q.astype(BF16),
        'p_sub_keys': p_sub_keys.astype(BF16).reshape(depth, PEER_HEADS * 2, PEER_NKEYS, PEER_DKEY),
        'p_u': p_u, 'p_v': p_v,
    }
    sc_order = []
    y_prompt, new_state = _run_trunk(x_prompt, mods, 0, P, None, sc_order)
    y_sample, _ = _run_trunk(x_sample, mods, 1, P, (cache_k, cache_v, state_ssd, state_hgrn), sc_order)
    return (y_prompt, y_sample) + new_state
```

```python
import functools
import math

import jax
import jax.numpy as jnp
from jax import lax
from jax.experimental import pallas as pl
from jax.experimental.pallas import tpu as pltpu
from jax.experimental.pallas import tpu_sc as plsc

F32 = jnp.float32
BF16 = jnp.bfloat16
I32 = jnp.int32
HIGHEST = lax.Precision.HIGHEST

NORM_EPS = 1e-6
NEG_BIG = -1e30
LANES = 128
SUBLANES = 8
VMEM_LIMIT = 48 * 1024 * 1024

GRID_W = 64
HEAD_DIM = 64
N_Q_HEADS = 8
N_KV_HEADS = 2
GQA_GROUP = 4
WINDOW = 128
ROPE_THETA = 10000.0
SSD_HEADS = 8
SSD_HEAD_DIM = 64
SSD_GROUPS = 2
D_STATE = 64
CONV_K = 5
HGRN_HEADS = 8
HGRN_DK = 128
HGRN_CHUNK = 32
PEER_HEADS = 8
PEER_NKEYS = 128
PEER_TOPK = 16
PEER_DKEY = 128
PEER_PAIRS = PEER_HEADS * PEER_TOPK

ROW_TILE = 256
PEER_BLOCK = 128
SC_LANES = 16
SC_CHUNK = 16
SC_SHARE_CONTEXT = ((7, 8), (15, 16), (15, 16), (15, 16))
SC_SHARE_LATENT = ((25, 32), (25, 32), (25, 32), (3, 4))
GELU_C = math.sqrt(2.0 / math.pi)


def _cparams(*sem):
    return pltpu.CompilerParams(dimension_semantics=sem, vmem_limit_bytes=VMEM_LIMIT)


def _norm_mod(x, nw, scale, shift):
    ms = jnp.mean(x * x, axis=-1, keepdims=True)
    return (x * lax.rsqrt(ms + NORM_EPS)) * nw * (1.0 + scale) + shift


def _mod_kernel(c_ref, w_ref, b_ref, o_ref):
    c = c_ref[...]
    s = c * jax.nn.sigmoid(c)
    o_ref[0] = jnp.dot(s, w_ref[0], precision=HIGHEST, preferred_element_type=F32) + b_ref[0]


def _modulation(cond_rows, w_ada, b_ada):
    depth, d, n = w_ada.shape
    rows = cond_rows.shape[0]
    return pl.pallas_call(
        _mod_kernel,
        grid=(depth, n // d),
        in_specs=[pl.BlockSpec((rows, d), lambda l, j: (0, 0)),
                  pl.BlockSpec((1, d, d), lambda l, j: (l, 0, j)),
                  pl.BlockSpec((1, 1, d), lambda l, j: (l, 0, j))],
        out_specs=pl.BlockSpec((1, rows, d), lambda l, j: (l, 0, j)),
        out_shape=jax.ShapeDtypeStruct((depth, rows, n), F32),
        compiler_params=_cparams("arbitrary", "arbitrary"),
        name="modulation",
    )(cond_rows, w_ada, b_ada.reshape(depth, 1, n))


def _mod_spec(mod_row0, seq, tile, d6, first_tile=0):
    row0, per_batch = mod_row0
    return pl.BlockSpec((1, 1, d6),
                        lambda i: (row0 + per_batch * (((i + first_tile) * tile) // seq), 0, 0))


def _inproj_kernel(x_ref, m_ref, nw_ref, w_ref, *o_refs, splits, d):
    m = m_ref[0]
    h = _norm_mod(x_ref[...], nw_ref[...], m[:, d:2 * d], m[:, 0:d]).astype(BF16)
    for o_ref, (a, b) in zip(o_refs, splits):
        o_ref[...] = jnp.dot(h, w_ref[:, a:b], preferred_element_type=F32)


def _inproj(x, mods, mod_row0, seq, nw, w_bf16, splits, name):
    m, d = x.shape
    n = w_bf16.shape[1]
    tile = min(ROW_TILE, seq)
    return pl.pallas_call(
        functools.partial(_inproj_kernel, splits=splits, d=d),
        grid=(m // tile,),
        in_specs=[pl.BlockSpec((tile, d), lambda i: (i, 0)),
                  _mod_spec(mod_row0, seq, tile, mods.shape[-1]),
                  pl.BlockSpec((1, d), lambda i: (0, 0)),
                  pl.BlockSpec((d, n), lambda i: (0, 0))],
        out_specs=[pl.BlockSpec((tile, b - a), lambda i: (i, 0)) for a, b in splits],
        out_shape=[jax.ShapeDtypeStruct((m, b - a), F32) for a, b in splits],
        compiler_params=_cparams("arbitrary"),
        name=name,
    )(x, mods, nw.reshape(1, d), w_bf16)


def _outproj_kernel(mix_ref, x_ref, m_ref, w_ref, o_ref, *, d):
    y = jnp.dot(mix_ref[...].astype(BF16), w_ref[...], preferred_element_type=F32)
    o_ref[...] = x_ref[...] + m_ref[0][:, 2 * d:3 * d] * y


def _outproj(mix, x, mods, mod_row0, seq, w_bf16, name):
    m, d = x.shape
    k = mix.shape[1]
    tile = min(ROW_TILE, seq)
    return pl.pallas_call(
        functools.partial(_outproj_kernel, d=d),
        grid=(m // tile,),
        in_specs=[pl.BlockSpec((tile, k), lambda i: (i, 0)),
                  pl.BlockSpec((tile, d), lambda i: (i, 0)),
                  _mod_spec(mod_row0, seq, tile, mods.shape[-1]),
                  pl.BlockSpec((k, d), lambda i: (0, 0))],
        out_specs=pl.BlockSpec((tile, d), lambda i: (i, 0)),
        out_shape=jax.ShapeDtypeStruct((m, d), F32),
        compiler_params=_cparams("arbitrary"),
        name=name,
    )(mix, x, mods, w_bf16)


def _topk_over_rows(s, k, payload=None):
    n = s.shape[0]
    iota = lax.broadcasted_iota(I32, s.shape, 0)
    vals, idxs, pays = [], [], []
    for _ in range(k):
        m = jnp.max(s, axis=0, keepdims=True)
        i = jnp.min(jnp.where(s == m, iota, n), axis=0, keepdims=True)
        hit = iota == i
        vals.append(m)
        idxs.append(i)
        if payload is not None:
            pays.append(jnp.max(jnp.where(hit, payload, -1), axis=0, keepdims=True))
        s = jnp.where(hit, -jnp.inf, s)
    out = (jnp.concatenate(vals, axis=0), jnp.concatenate(idxs, axis=0))
    if payload is not None:
        out += (jnp.concatenate(pays, axis=0),)
    return out


def _peer_route_kernel(x_ref, m_ref, nw_ref, wq_ref, keys_ref, h_ref, e_ref, g_ref, *, d):
    m = m_ref[0]
    h = _norm_mod(x_ref[...], nw_ref[...], m[:, 4 * d:5 * d], m[:, 3 * d:4 * d])
    h_ref[...] = h
    hb = h.astype(BF16)
    nt = (((1,), (1,)), ((), ()))
    for head in range(PEER_HEADS):
        tops = []
        for half in range(2):
            c0 = (head * 2 + half) * PEER_DKEY
            q = jnp.dot(hb, wq_ref[:, c0:c0 + PEER_DKEY], preferred_element_type=F32)
            s = lax.dot_general(keys_ref[head * 2 + half], q.astype(BF16), nt,
                                preferred_element_type=F32)
            tops.append(_topk_over_rows(s, PEER_TOPK))
        (s0, i0), (s1, i1) = tops
        widths = [PEER_TOPK // (a + 1) for a in range(PEER_TOPK)]
        n_pad = -sum(widths) % SUBLANES
        cand_s = jnp.concatenate([s0[a:a + 1] + s1[:w] for a, w in enumerate(widths)]
                                 + [jnp.full((n_pad, s0.shape[1]), -jnp.inf, F32)], axis=0)
        cand_e = jnp.concatenate([i0[a:a + 1] * PEER_NKEYS + i1[:w] for a, w in enumerate(widths)]
                                 + [jnp.zeros((n_pad, s0.shape[1]), I32)], axis=0)
        best_s, _, best_e = _topk_over_rows(cand_s, PEER_TOPK, payload=cand_e)
        p = jnp.exp(best_s - best_s[0:1])
        r0 = head * PEER_TOPK
        e_ref[r0:r0 + PEER_TOPK, :] = best_e
        g_ref[r0:r0 + PEER_TOPK, :] = p / jnp.sum(p, axis=0, keepdims=True)


def _peer_route(x, mods, mod_row0, seq, nw, wq_bf16, keys_bf16):
    m, d = x.shape
    tile = min(ROW_TILE, seq)
    nq = wq_bf16.shape[1]
    return pl.pallas_call(
        functools.partial(_peer_route_kernel, d=d),
        grid=(m // tile,),
        in_specs=[pl.BlockSpec((tile, d), lambda i: (i, 0)),
                  _mod_spec(mod_row0, seq, tile, mods.shape[-1]),
                  pl.BlockSpec((1, d), lambda i: (0, 0)),
                  pl.BlockSpec((d, nq), lambda i: (0, 0)),
                  pl.BlockSpec(keys_bf16.shape, lambda i: (0, 0, 0))],
        out_specs=[pl.BlockSpec((tile, d), lambda i: (i, 0)),
                   pl.BlockSpec((PEER_PAIRS, tile), lambda i: (0, i)),
                   pl.BlockSpec((PEER_PAIRS, tile), lambda i: (0, i))],
        out_shape=[jax.ShapeDtypeStruct((m, d), F32),
                   jax.ShapeDtypeStruct((PEER_PAIRS, m), I32),
                   jax.ShapeDtypeStruct((PEER_PAIRS, m), F32)],
        compiler_params=_cparams("arbitrary"),
        name="peer_route",
    )(x, mods, nw.reshape(1, d), wq_bf16, keys_bf16)


def _peer_gather_kernel(idx_hbm, h_ref, g_ref, x_ref, m_ref, u_hbm, v_hbm, o_ref,
                        idx_smem, ubuf, vbuf, sem_idx, sem_u, sem_v, *, layer, d):
    blk = pl.program_id(0)
    n_groups = PEER_BLOCK // SUBLANES

    cp = pltpu.make_async_copy(idx_hbm.at[blk], idx_smem, sem_idx)
    cp.start()
    cp.wait()

    def issue(tok, slot):
        def body(pair, carry):
            e = idx_smem[tok, pair]
            pltpu.make_async_copy(u_hbm.at[layer, pl.ds(e, 1)], ubuf.at[slot, pl.ds(pair, 1)],
                                  sem_u.at[slot]).start()
            pltpu.make_async_copy(v_hbm.at[layer, pl.ds(e, 1)], vbuf.at[slot, pl.ds(pair, 1)],
                                  sem_v.at[slot]).start()
            return carry
        lax.fori_loop(0, PEER_PAIRS, body, 0, unroll=8)

    def wait(slot):
        pltpu.make_async_copy(u_hbm.at[layer, pl.ds(0, PEER_PAIRS)], ubuf.at[slot], sem_u.at[slot]).wait()
        pltpu.make_async_copy(v_hbm.at[layer, pl.ds(0, PEER_PAIRS)], vbuf.at[slot], sem_v.at[slot]).wait()

    gate2 = m_ref[0][:, 5 * d:6 * d]
    lane = lax.broadcasted_iota(I32, (PEER_PAIRS, PEER_BLOCK), 1)

    issue(0, 0)

    def group(grp, carry):
        base = pl.multiple_of(grp * SUBLANES, SUBLANES)
        h8 = h_ref[pl.ds(base, SUBLANES), :]
        rows = []
        for r in range(SUBLANES):
            tok = base + r
            slot = r % 2
            if r < SUBLANES - 1:
                issue(tok + 1, 1 - slot)
            else:
                @pl.when(grp < n_groups - 1)
                def _():
                    issue(tok + 1, 1 - slot)
            wait(slot)
            act = jnp.sum(ubuf[slot] * h8[r:r + 1, :], axis=1, keepdims=True)
            gate = jnp.sum(jnp.where(lane == tok, g_ref[...], 0.0), axis=1, keepdims=True)
            w = jax.nn.gelu(act) * gate
            rows.append(jnp.sum(vbuf[slot] * w, axis=0, keepdims=True))
        out8 = jnp.concatenate(rows, axis=0)
        o_ref[pl.ds(base, SUBLANES), :] = x_ref[pl.ds(base, SUBLANES), :] + gate2 * out8
        return carry

    lax.fori_loop(0, n_groups, group, 0)


def _peer_gather(idx, h, gates, x, mods, mod_row0, seq, p_u, p_v, layer, first_block):
    m, d = x.shape
    nblk = m // PEER_BLOCK - first_block
    idx3 = idx[:, first_block * PEER_BLOCK:].T.reshape(nblk, PEER_BLOCK, PEER_PAIRS)
    rows = pl.BlockSpec((PEER_BLOCK, d), lambda i: (i + first_block, 0))
    return pl.pallas_call(
        functools.partial(_peer_gather_kernel, layer=layer, d=d),
        grid=(nblk,),
        in_specs=[pl.BlockSpec(memory_space=pl.ANY),
                  rows,
                  pl.BlockSpec((PEER_PAIRS, PEER_BLOCK), lambda i: (0, i + first_block)),
                  rows,
                  _mod_spec(mod_row0, seq, PEER_BLOCK, mods.shape[-1], first_block),
                  pl.BlockSpec(memory_space=pl.ANY),
                  pl.BlockSpec(memory_space=pl.ANY)],
        out_specs=pl.BlockSpec((PEER_BLOCK, d), lambda i: (i, 0)),
        out_shape=jax.ShapeDtypeStruct((nblk * PEER_BLOCK, d), F32),
        scratch_shapes=[pltpu.SMEM((PEER_BLOCK, PEER_PAIRS), I32),
                        pltpu.VMEM((2, PEER_PAIRS, d), F32),
                        pltpu.VMEM((2, PEER_PAIRS, d), F32),
                        pltpu.SemaphoreType.DMA,
                        pltpu.SemaphoreType.DMA((2,)),
                        pltpu.SemaphoreType.DMA((2,))],
        compiler_params=_cparams("arbitrary"),
        name="peer_gather",
    )(idx3, h, gates, x, mods, p_u, p_v)


def _peer_experts_sc(idx, h, gates, u_rows, v_rows):
    m = idx.shape[0]
    d = h.shape[1]
    info = plsc.get_sparse_core_info()
    n_workers = info.num_cores * info.num_subcores
    per = m // n_workers
    n_chunks = PEER_PAIRS // SC_CHUNK
    n_vec = d // SC_LANES
    mesh = plsc.VectorSubcoreMesh(core_axis_name="c", subcore_axis_name="s")

    @functools.partial(
        pl.kernel, out_type=jax.ShapeDtypeStruct((m, d), F32), mesh=mesh,
        scratch_types=[pltpu.VMEM((2, PEER_PAIRS), I32), pltpu.VMEM((2, d), F32),
                       pltpu.VMEM((2, PEER_PAIRS), F32), pltpu.VMEM((2, d), F32),
                       pltpu.VMEM((2, SC_CHUNK, d), F32), pltpu.VMEM((2, SC_CHUNK, d), F32),
                       pltpu.SemaphoreType.DMA((2,)), pltpu.SemaphoreType.DMA((2,)),
                       pltpu.SemaphoreType.DMA((2,)), pltpu.SemaphoreType.DMA((2,))],
        compiler_params=pltpu.CompilerParams(needs_layout_passes=False),
        name="peer_experts_sc")
    def body(idx_hbm, h_hbm, g_hbm, u_hbm, v_hbm, o_hbm,
             idx_v, x_v, g_v, out_v, ubuf, vbuf, sem_meta, sem_out, sem_u, sem_v):
        wid = lax.axis_index("c") * info.num_subcores + lax.axis_index("s")
        tok0 = wid * per
        lane = lax.iota(I32, SC_LANES)

        def meta_copies(ti, ms):
            t = tok0 + ti
            return (pltpu.make_async_copy(idx_hbm.at[t], idx_v.at[ms], sem_meta.at[ms]),
                    pltpu.make_async_copy(h_hbm.at[t], x_v.at[ms], sem_meta.at[ms]),
                    pltpu.make_async_copy(g_hbm.at[t], g_v.at[ms], sem_meta.at[ms]))

        def gather_copies(ms, c, slot):
            ids = idx_v.at[ms, pl.ds(c * SC_CHUNK, SC_CHUNK)]
            return (pltpu.make_async_copy(u_hbm.at[ids], ubuf.at[slot], sem_u.at[slot]),
                    pltpu.make_async_copy(v_hbm.at[ids], vbuf.at[slot], sem_v.at[slot]))

        def out_copy(ti, ms):
            return pltpu.make_async_copy(out_v.at[ms], o_hbm.at[tok0 + ti], sem_out.at[ms])

        for cp in meta_copies(0, 0):
            cp.start()
        for cp in meta_copies(0, 0):
            cp.wait()
        for cp in gather_copies(0, 0, 0):
            cp.start()

        def token(ti, carry):
            ms = ti % 2
            nxt = 1 - ms

            @pl.when(ti + 1 < per)
            def _():
                for cp in meta_copies(ti + 1, nxt):
                    cp.start()

            @pl.when(ti >= 2)
            def _():
                out_copy(ti - 2, ms).wait()

            def zero(j, c):
                out_v[ms, pl.ds(j * SC_LANES, SC_LANES)] = jnp.zeros((SC_LANES,), F32)
                return c
            lax.fori_loop(0, n_vec, zero, 0)

            for c in range(n_chunks):
                slot = c % 2
                if c + 1 < n_chunks:
                    for cp in gather_copies(ms, c + 1, 1 - slot):
                        cp.start()
                else:
                    @pl.when(ti + 1 < per)
                    def _():
                        for cp in meta_copies(ti + 1, nxt):
                            cp.wait()
                        for cp in gather_copies(nxt, 0, 1 - slot):
                            cp.start()
                cu, cv = gather_copies(ms, c, slot)
                cu.wait()

                def udot(j, accs):
                    xj = x_v[ms, pl.ds(j * SC_LANES, SC_LANES)]
                    return tuple(accs[r] + ubuf[slot, r, pl.ds(j * SC_LANES, SC_LANES)] * xj
                                 for r in range(SC_CHUNK))
                accs = lax.fori_loop(0, n_vec, udot,
                                     tuple(jnp.zeros((SC_LANES,), F32) for _ in range(SC_CHUNK)))
                act = jnp.zeros((SC_LANES,), F32)
                for r in range(SC_CHUNK):
                    act = jnp.where(lane == r, jnp.sum(accs[r]), act)
                y = GELU_C * (act + 0.044715 * (act * act * act))
                w = act / (1.0 + jnp.exp(-2.0 * y)) * g_v[ms, pl.ds(c * SC_CHUNK, SC_CHUNK)]
                ws = [jnp.sum(jnp.where(lane == r, w, 0.0)) for r in range(SC_CHUNK)]
                cv.wait()

                @plsc.parallel_loop(0, n_vec, unroll=2)
                def _(j):
                    parts = [ws[r] * vbuf[slot, r, pl.ds(j * SC_LANES, SC_LANES)] for r in range(SC_CHUNK)]
                    while len(parts) > 1:
                        parts = [parts[i] + parts[i + 1] for i in range(0, len(parts), 2)]
                    plsc.addupdate(out_v.at[ms, pl.ds(j * SC_LANES, SC_LANES)], parts[0])

            out_copy(ti, ms).start()
            return carry

        lax.fori_loop(0, per, token, 0)
        for back in (2, 1):
            if per >= back:
                out_copy(per - back, (per - back) % 2).wait()

    return body(idx, h, gates, u_rows, v_rows)


def _residual_kernel(x_ref, y_ref, m_ref, o_ref, *, d):
    o_ref[...] = x_ref[...] + m_ref[0][:, 5 * d:6 * d] * y_ref[...]


def _residual(x, y, mods, mod_row0, seq):
    m, d = y.shape
    tile = min(ROW_TILE, seq)
    return pl.pallas_call(
        functools.partial(_residual_kernel, d=d),
        grid=(m // tile,),
        in_specs=[pl.BlockSpec((tile, d), lambda i: (i, 0)),
                  pl.BlockSpec((tile, d), lambda i: (i, 0)),
                  _mod_spec(mod_row0, seq, tile, mods.shape[-1])],
        out_specs=pl.BlockSpec((tile, d), lambda i: (i, 0)),
        out_shape=jax.ShapeDtypeStruct((m, d), F32),
        compiler_params=_cparams("arbitrary"),
        name="peer_residual",
    )(x, y, mods)


def _peer(x, mods, mod_row0, seq, layer, nw, wq_bf16, keys_bf16, p_u, p_v, sc_share, after=None):
    m, d = x.shape
    h, idx, gates = _peer_route(x, mods, mod_row0, seq, nw, wq_bf16, keys_bf16)
    n_blocks = m // PEER_BLOCK
    sc_blocks = (n_blocks * sc_share[0]) // sc_share[1]
    m_sc = sc_blocks * PEER_BLOCK
    n_experts = p_u.shape[1]
    idx_sc = idx[:, :m_sc].T + layer * n_experts
    if after is not None:
        idx_sc, _ = lax.optimization_barrier((idx_sc, after))
    y_sc = _peer_experts_sc(idx_sc, h, gates[:, :m_sc].T,
                            p_u.reshape(-1, d), p_v.reshape(-1, d))
    x_sc = _residual(x, y_sc, mods, mod_row0, seq)
    if sc_blocks == n_blocks:
        return x_sc, y_sc
    x_tc = _peer_gather(idx, h, gates, x, mods, mod_row0, seq, p_u, p_v, layer, sc_blocks)
    return jnp.concatenate([x_sc, x_tc], axis=0), y_sc


def _head_mean_square(x):
    n = x.shape[1]
    r = lax.broadcasted_iota(I32, (n, n), 0) // HEAD_DIM
    c = lax.broadcasted_iota(I32, (n, n), 1) // HEAD_DIM
    seg = jnp.where(r == c, 1.0 / HEAD_DIM, 0.0).astype(F32)
    return jnp.dot(x * x, seg, precision=HIGHEST, preferred_element_type=F32)


def _swap_rot_halves(x):
    n = x.shape[1]
    quarter = HEAD_DIM // 4
    lane = lax.broadcasted_iota(I32, x.shape, 1)
    lo = (lane % (2 * quarter)) < quarter
    return jnp.where(lo, pltpu.roll(x, n - quarter, axis=1), pltpu.roll(x, quarter, axis=1))


def _qkprep_kernel(q_ref, k_ref, qw_ref, kw_ref, *rest, rope):
    if rope:
        cos_ref, sin_ref, qo_ref, ko_ref = rest
    else:
        qo_ref, ko_ref = rest
    q = q_ref[...]
    k = k_ref[...]
    q = q * lax.rsqrt(_head_mean_square(q) + NORM_EPS) * qw_ref[...]
    k = k * lax.rsqrt(_head_mean_square(k) + NORM_EPS) * kw_ref[...]
    if rope:
        cos = cos_ref[...]
        sin = sin_ref[...]
        cq = jnp.concatenate([cos] * (q.shape[1] // LANES), axis=1)
        sq = jnp.concatenate([sin] * (q.shape[1] // LANES), axis=1)
        q = q * cq + _swap_rot_halves(q) * sq
        k = k * cos + _swap_rot_halves(k) * sin
    qo_ref[...] = q
    ko_ref[...] = k


def _rope_tables(seq):
    axis_dim = HEAD_DIM // 2
    inv_freq = ROPE_THETA ** (-jnp.arange(0, axis_dim, 2, dtype=F32) / axis_dim)
    t = jnp.arange(seq)
    pos = jnp.stack([(t // GRID_W).astype(F32), (t % GRID_W).astype(F32)], axis=1)
    lane = jnp.arange(LANES)
    dd = lane % HEAD_DIM
    ang = pos[:, dd // axis_dim] * inv_freq[dd % (axis_dim // 2)][None, :]
    sign = jnp.where((dd % axis_dim) < axis_dim // 2, -1.0, 1.0).astype(F32)
    return jnp.cos(ang), jnp.sin(ang) * sign[None, :]


def _qkprep(q, k, qw, kw, seq, rope):
    m, nq = q.shape
    nk = k.shape[1]
    tile = min(ROW_TILE, seq)
    qw_row = jnp.tile(qw, nq // HEAD_DIM).reshape(1, nq)
    kw_row = jnp.tile(kw, nk // HEAD_DIM).reshape(1, nk)
    in_specs = [pl.BlockSpec((tile, nq), lambda i: (i, 0)),
                pl.BlockSpec((tile, nk), lambda i: (i, 0)),
                pl.BlockSpec((1, nq), lambda i: (0, 0)),
                pl.BlockSpec((1, nk), lambda i: (0, 0))]
    args = [q, k, qw_row, kw_row]
    if rope:
        cos, sin = _rope_tables(seq)
        per_seq = seq // tile
        in_specs += [pl.BlockSpec((tile, LANES), lambda i: (i % per_seq, 0)),
                     pl.BlockSpec((tile, LANES), lambda i: (i % per_seq, 0))]
        args += [cos, sin]
    return pl.pallas_call(
        functools.partial(_qkprep_kernel, rope=rope),
        grid=(m // tile,),
        in_specs=in_specs,
        out_specs=[pl.BlockSpec((tile, nq), lambda i: (i, 0)),
                   pl.BlockSpec((tile, nk), lambda i: (i, 0))],
        out_shape=[jax.ShapeDtypeStruct((m, nq), F32), jax.ShapeDtypeStruct((m, nk), F32)],
        compiler_params=_cparams("arbitrary"),
        name="qk_prep",
    )(*args)


def _dup_halves(x):
    lane = lax.broadcasted_iota(I32, x.shape, 1)
    sw = pltpu.roll(x, HEAD_DIM, axis=1)
    lo = lane < HEAD_DIM
    return jnp.where(lo, x, sw), jnp.where(lo, sw, x)


def _attend(q, k_all, v_all, sink_ref, mask):
    scale = HEAD_DIM ** -0.5
    nt = (((1,), (1,)), ((), ()))
    kk = [a.astype(BF16) for a in _dup_halves(k_all)]
    vv = [a.astype(BF16) for a in _dup_halves(v_all)]
    lane = lax.broadcasted_iota(I32, (q.shape[0], LANES), 1)
    lo = lane < HEAD_DIM
    tiles = []
    for t in range(q.shape[1] // LANES):
        qt = q[:, t * LANES:(t + 1) * LANES]
        g = (2 * t) // GQA_GROUP
        halves = []
        for hh in range(2):
            head = 2 * t + hh
            qm = jnp.where(lo if hh == 0 else ~lo, qt, 0.0).astype(BF16)
            s = lax.dot_general(qm, kk[g], nt, preferred_element_type=F32) * scale
            if mask is not None:
                s = jnp.where(mask, s, NEG_BIG)
            sink = sink_ref[head]
            mx = jnp.maximum(jnp.max(s, axis=1, keepdims=True), sink)
            p = jnp.exp(s - mx)
            den = jnp.sum(p, axis=1, keepdims=True) + jnp.exp(sink - mx)
            p = (p / den).astype(BF16)
            halves.append(jnp.dot(p, vv[g], preferred_element_type=F32))
        tiles.append(jnp.where(lo, halves[0], halves[1]))
    return jnp.concatenate(tiles, axis=1)


def _ctx_attn_kernel(sink_ref, q_ref, k_ref, v_ref, o_ref):
    o_ref[...] = _attend(q_ref[...], k_ref[...], v_ref[...], sink_ref, None)


def _ctx_attention(q, k, v, sink, seq):
    m, nq = q.shape
    nk = k.shape[1]
    return pl.pallas_call(
        _ctx_attn_kernel,
        grid=(m // seq,),
        in_specs=[pl.BlockSpec(memory_space=pltpu.SMEM),
                  pl.BlockSpec((seq, nq), lambda b: (b, 0)),
                  pl.BlockSpec((seq, nk), lambda b: (b, 0)),
                  pl.BlockSpec((seq, nk), lambda b: (b, 0))],
        out_specs=pl.BlockSpec((seq, nq), lambda b: (b, 0)),
        out_shape=jax.ShapeDtypeStruct((m, nq), F32),
        compiler_params=_cparams("arbitrary"),
        name="ctx_attention",
    )(sink, q, k, v)


def _lat_attn_kernel(sink_ref, q_ref, kc_ref, vc_ref, kp_ref, k0_ref, kn_ref, vp_ref, v0_ref, vn_ref,
                     o_ref, *, seq):
    qb = pl.program_id(1)
    blk = q_ref.shape[0]
    n_ctx = kc_ref.shape[1]
    k_all = jnp.concatenate([kc_ref[0], kp_ref[...], k0_ref[...], kn_ref[...]], axis=0)
    v_all = jnp.concatenate([vc_ref[0], vp_ref[...], v0_ref[...], vn_ref[...]], axis=0)
    tk = k_all.shape[0]
    qpos = qb * blk + lax.broadcasted_iota(I32, (blk, tk), 0)
    col = lax.broadcasted_iota(I32, (blk, tk), 1)
    kpos = (qb - 1) * blk + col - n_ctx
    local_ok = (jnp.abs(qpos - kpos) <= WINDOW) & (kpos >= 0) & (kpos < seq)
    mask = (col < n_ctx) | local_ok
    o_ref[...] = _attend(q_ref[...], k_all, v_all, sink_ref, mask)


def _lat_attention(q, k, v, k_ctx, v_ctx, sink, seq):
    m, nq = q.shape
    nk = k.shape[1]
    blk = WINDOW
    nb = seq // blk
    n_ctx = k_ctx.shape[1]
    last = m // blk - 1

    def kv_spec(shift):
        return pl.BlockSpec((blk, nk), lambda b, i: (jnp.clip(b * nb + i + shift, 0, last), 0))

    ctx_spec = pl.BlockSpec((1, n_ctx, nk), lambda b, i: (b, 0, 0))
    return pl.pallas_call(
        functools.partial(_lat_attn_kernel, seq=seq),
        grid=(m // seq, nb),
        in_specs=[pl.BlockSpec(memory_space=pltpu.SMEM),
                  pl.BlockSpec((blk, nq), lambda b, i: (b * nb + i, 0)),
                  ctx_spec, ctx_spec,
                  kv_spec(-1), kv_spec(0), kv_spec(1),
                  kv_spec(-1), kv_spec(0), kv_spec(1)],
        out_specs=pl.BlockSpec((blk, nq), lambda b, i: (b * nb + i, 0)),
        out_shape=jax.ShapeDtypeStruct((m, nq), F32),
        compiler_params=_cparams("arbitrary", "arbitrary"),
        name="lat_attention",
    )(sink, q, k_ctx, v_ctx, k, k, k, v, v, v)

SSD_BLOCK = 256
SSD_PAIRS = SSD_HEADS // 2
SSD_INNER = SSD_HEADS * SSD_HEAD_DIM
HALO = SUBLANES


def _softplus(x):
    return jnp.maximum(x, 0.0) + jnp.log1p(jnp.exp(-jnp.abs(x)))


def _silu(x):
    return x * jax.nn.sigmoid(x)


def _ssd_decays(dt_raw, bias, a_log):
    n = dt_raw.shape[0]
    dt = _softplus(dt_raw + bias)
    log_a = dt * (-jnp.exp(a_log))
    r = lax.broadcasted_iota(I32, (n, n), 0)
    c = lax.broadcasted_iota(I32, (n, n), 1)
    lower = jnp.where(c <= r, 1.0, 0.0).astype(F32)
    upper = jnp.where(r <= c, 1.0, 0.0).astype(F32)
    cum_col = jnp.dot(lower, log_a, precision=HIGHEST, preferred_element_type=F32)
    dt_row = dt.T
    la_row = log_a.T
    cum_row = jnp.dot(la_row, upper, precision=HIGHEST, preferred_element_type=F32)
    return dt, log_a, cum_col, dt_row, la_row, cum_row


def _ssd_scan_chunk(xs, bmat, cmat, w_of, q_scale_of, k_scale_of, carry_of, s_ref):
    nt = (((1,), (1,)), ((), ()))
    n = xs.shape[0]
    lane = lax.broadcasted_iota(I32, (n, LANES), 1)
    lo = lane < SSD_HEAD_DIM
    lane_s = lax.broadcasted_iota(I32, (D_STATE, LANES), 1)
    lo_s = lane_s < SSD_HEAD_DIM
    b_t = bmat.T
    cb16 = cmat.astype(BF16)
    ys = []
    for pair in range(SSD_PAIRS):
        g = (2 * pair) // (SSD_HEADS // SSD_GROUPS)
        in_g = (lane // D_STATE) == g
        cg = jnp.where(in_g, cmat, 0.0)
        cb = lax.dot_general(cg.astype(BF16), bmat.astype(BF16), nt, preferred_element_type=F32)
        x_pair = xs[:, pair * LANES:(pair + 1) * LANES]
        x16 = x_pair.astype(BF16)
        s_old = s_ref[pair]
        s2 = jnp.concatenate([s_old, s_old], axis=0).astype(BF16)
        bg_t = b_t[g * D_STATE:(g + 1) * D_STATE, :]
        y_h, s_h = [], []
        for hh in range(2):
            h = 2 * pair + hh
            w = (cb * w_of(h)).astype(BF16)
            y = jnp.dot(w, x16, preferred_element_type=F32)
            cq = (cg * q_scale_of(h)).astype(BF16)
            y = y + jnp.dot(cq, s2, preferred_element_type=F32)
            y_h.append(y)
            kt = (bg_t * k_scale_of(h)).astype(BF16)
            s_h.append(carry_of(h) * s_old + jnp.dot(kt, x16, preferred_element_type=F32))
        ys.append(jnp.where(lo, y_h[0], y_h[1]))
        s_ref[pair] = jnp.where(lo_s, s_h[0], s_h[1])
    return jnp.concatenate(ys, axis=1)


def _ssd_fwd_kernel(x_ref, xp_ref, xn_ref, dt_ref, s0_ref, cw_ref, cb_ref, bias_ref, alog_ref,
                    y_ref, xc_ref, sfin_ref, s_ref):
    c = pl.program_id(1)
    nc = pl.num_programs(1)
    n = x_ref.shape[0]

    @pl.when(c == 0)
    def _():
        s_ref[...] = s0_ref[0]

    prev = jnp.where(c > 0, xp_ref[...], 0.0)
    nxt = jnp.where(c < nc - 1, xn_ref[...], 0.0)
    xe = jnp.concatenate([prev, x_ref[...], nxt], axis=0)
    pad = (CONV_K - 1) // 2
    acc = cb_ref[...] + cw_ref[0:1, :] * xe[HALO - pad:HALO - pad + n, :]
    for k in range(1, CONV_K):
        acc = acc + cw_ref[k:k + 1, :] * xe[HALO - pad + k:HALO - pad + k + n, :]
    xc = _silu(acc)
    xc_ref[...] = xc
    xs = xc[:, :SSD_INNER]
    bmat = xc[:, SSD_INNER:SSD_INNER + LANES]
    cmat = xc[:, SSD_INNER + LANES:SSD_INNER + 2 * LANES]

    dt, log_a, cum_col, dt_row, la_row, cum_row = _ssd_decays(dt_ref[...], bias_ref[...], alog_ref[...])
    r = lax.broadcasted_iota(I32, (n, n), 0)
    cc = lax.broadcasted_iota(I32, (n, n), 1)
    causal = cc <= r
    last_col = cum_col[n - 1:n, :]

    def w_of(h):
        seg = cum_col[:, h:h + 1] - cum_row[h:h + 1, :]
        return jnp.exp(jnp.where(causal, seg, NEG_BIG)) * dt_row[h:h + 1, :]

    def q_scale_of(h):
        return jnp.exp(cum_col[:, h:h + 1])

    def k_scale_of(h):
        return dt_row[h:h + 1, :] * jnp.exp(cum_row[h:h + 1, n - 1:n] - cum_row[h:h + 1, :])

    def carry_of(h):
        return jnp.exp(last_col[:, h:h + 1])

    y_ref[...] = _ssd_scan_chunk(xs, bmat, cmat, w_of, q_scale_of, k_scale_of, carry_of, s_ref)

    @pl.when(c == nc - 1)
    def _():
        sfin_ref[0] = s_ref[...]


def _ssd_bwd_kernel(xc_ref, dt_ref, yf_ref, z_ref, s0_ref, bias_ref, alog_ref, dskip_ref, nw_ref,
                    y_ref, sfin_ref, s_ref):
    c = pl.program_id(1)
    nc = pl.num_programs(1)
    n = xc_ref.shape[0]

    @pl.when(c == 0)
    def _():
        s_ref[...] = s0_ref[0]

    xc = xc_ref[...]
    xs = xc[:, :SSD_INNER]
    bmat = xc[:, SSD_INNER:SSD_INNER + LANES]
    cmat = xc[:, SSD_INNER + LANES:SSD_INNER + 2 * LANES]
    dt, log_a, cum_col, dt_row, la_row, cum_row = _ssd_decays(dt_ref[...], bias_ref[...], alog_ref[...])
    ex_col = cum_col - log_a
    ex_row = cum_row - la_row
    r = lax.broadcasted_iota(I32, (n, n), 0)
    cc = lax.broadcasted_iota(I32, (n, n), 1)
    anti = cc >= r
    tot_col = cum_col[n - 1:n, :]
    off = SSD_HEADS

    def w_of(h):
        j = off + h
        seg = ex_row[j:j + 1, :] - ex_col[:, j:j + 1]
        return jnp.exp(jnp.where(anti, seg, NEG_BIG)) * dt_row[j:j + 1, :]

    def q_scale_of(h):
        j = off + h
        return jnp.exp(tot_col[:, j:j + 1] - ex_col[:, j:j + 1])

    def k_scale_of(h):
        j = off + h
        return dt_row[j:j + 1, :] * jnp.exp(ex_row[j:j + 1, :])

    def carry_of(h):
        j = off + h
        return jnp.exp(tot_col[:, j:j + 1])

    y_b = _ssd_scan_chunk(xs, bmat, cmat, w_of, q_scale_of, k_scale_of, carry_of, s_ref)
    y = yf_ref[...] + y_b + dskip_ref[...] * xs
    y = y * _silu(z_ref[...])
    ms = jnp.mean(y * y, axis=-1, keepdims=True)
    y_ref[...] = y * lax.rsqrt(ms + NORM_EPS) * nw_ref[...]

    @pl.when(c == nc - 1)
    def _():
        sfin_ref[0] = s_ref[...]


def _pair_states(s):
    b, h, n, p = s.shape
    return s.reshape(b, h // 2, 2, n, p).transpose(0, 1, 3, 2, 4).reshape(b, h // 2, n, 2 * p)


def _unpair_states(s):
    b, hp, n, p2 = s.shape
    return s.reshape(b, hp, n, 2, p2 // 2).transpose(0, 1, 3, 2, 4).reshape(b, hp * 2, n, p2 // 2)


def _ssd(xbc, dt, z, s0_f, s0_b, conv_w, conv_b, dt_bias, a_log, d_skip, ssd_norm, seq):
    m, nx = xbc.shape
    nb = m // seq
    blk = min(SSD_BLOCK, seq)
    nc = seq // blk
    hb = blk // HALO
    n_halo = m // HALO
    pad16 = lambda a: jnp.pad(a.reshape(1, -1), ((0, 0), (0, LANES - a.size)))
    bias = pad16(dt_bias)
    alog = pad16(a_log)
    state_spec = pl.BlockSpec((1, SSD_PAIRS, D_STATE, LANES), lambda b, c: (b, 0, 0, 0))
    state_shape = jax.ShapeDtypeStruct((nb, SSD_PAIRS, D_STATE, LANES), F32)
    row = lambda width: pl.BlockSpec((1, width), lambda b, c: (0, 0))

    def fwd_rows(width):
        return pl.BlockSpec((blk, width), lambda b, c: (b * nc + c, 0))

    def bwd_rows(width):
        return pl.BlockSpec((blk, width), lambda b, c: (b * nc + nc - 1 - c, 0))

    y_f, xc, s_f = pl.pallas_call(
        _ssd_fwd_kernel,
        grid=(nb, nc),
        in_specs=[fwd_rows(nx),
                  pl.BlockSpec((HALO, nx), lambda b, c: (jnp.maximum((b * nc + c) * hb - 1, 0), 0)),
                  pl.BlockSpec((HALO, nx), lambda b, c: (jnp.minimum((b * nc + c + 1) * hb, n_halo - 1), 0)),
                  fwd_rows(LANES), state_spec,
                  pl.BlockSpec((CONV_K, nx), lambda b, c: (0, 0)), row(nx), row(LANES), row(LANES)],
        out_specs=[fwd_rows(SSD_INNER), fwd_rows(nx), state_spec],
        out_shape=[jax.ShapeDtypeStruct((m, SSD_INNER), F32), jax.ShapeDtypeStruct((m, nx), F32), state_shape],
        scratch_shapes=[pltpu.VMEM((SSD_PAIRS, D_STATE, LANES), F32)],
        compiler_params=_cparams("arbitrary", "arbitrary"),
        name="ssd_forward",
    )(xbc, xbc, xbc, dt, _pair_states(s0_f), conv_w, conv_b.reshape(1, nx), bias, alog)

    dskip = jnp.repeat(d_skip, SSD_HEAD_DIM).reshape(1, SSD_INNER)
    y, s_b = pl.pallas_call(
        _ssd_bwd_kernel,
        grid=(nb, nc),
        in_specs=[bwd_rows(nx), bwd_rows(LANES), bwd_rows(SSD_INNER), bwd_rows(SSD_INNER), state_spec,
                  row(LANES), row(LANES), row(SSD_INNER), row(SSD_INNER)],
        out_specs=[bwd_rows(SSD_INNER), state_spec],
        out_shape=[jax.ShapeDtypeStruct((m, SSD_INNER), F32), state_shape],
        scratch_shapes=[pltpu.VMEM((SSD_PAIRS, D_STATE, LANES), F32)],
        compiler_params=_cparams("arbitrary", "arbitrary"),
        name="ssd_backward",
    )(xc, dt, y_f, z, _pair_states(s0_b), bias, alog, dskip, ssd_norm.reshape(1, SSD_INNER))
    return y, _unpair_states(s_f), _unpair_states(s_b)

def _hgrn_kernel(q_ref, ff_ref, fb_ref, i_ref, g_ref, lb_ref, s0_ref, nw_ref, o_ref, sfin_ref,
                 sf_ref, sb_ref, ob_ref, *, layer):
    t_len = q_ref.shape[0]
    n = HGRN_CHUNK
    n_chunks = t_len // n
    tn = (((0,), (0,)), ((), ()))
    nt = (((1,), (1,)), ((), ()))

    lbp = lb_ref[...]
    e = jnp.exp(lbp - jnp.max(lbp, axis=0, keepdims=True))
    sm = e / jnp.sum(e, axis=0, keepdims=True)
    lb = sm[0] * 0.0
    for j in range(1, layer + 1):
        lb = lb + sm[j]

    r = lax.broadcasted_iota(I32, (n, n), 0)
    c = lax.broadcasted_iota(I32, (n, n), 1)
    lower = jnp.where(c <= r, 1.0, 0.0).astype(F32)
    srow = lax.broadcasted_iota(I32, (n, HGRN_DK), 0)
    qscale = HGRN_DK ** -0.5

    def chunk(row0, f_ref, lb_d, reverse, s_ref):
        q = _silu(q_ref[pl.ds(row0, n), :]) * qscale
        f = f_ref[pl.ds(row0, n), :]
        v = i_ref[pl.ds(row0, n), :]
        k = (1.0 - lb_d) * jax.nn.sigmoid(-f)
        lf = jnp.log(lb_d + (1.0 - lb_d) * jax.nn.sigmoid(f))
        cum = jnp.dot(lower, lf, precision=HIGHEST, preferred_element_type=F32)
        tot = cum[n - 1:n, :]
        if reverse:
            cum = cum - lf
        rows = []
        for t in range(n):
            tile0 = (t // SUBLANES) * SUBLANES
            lo, hi = (tile0, n) if reverse else (0, tile0 + SUBLANES)
            cum_s = cum[lo:hi]
            if reverse:
                seg = jnp.where(srow[lo:hi] >= t, cum_s - cum[t:t + 1, :], NEG_BIG)
            else:
                seg = jnp.where(srow[lo:hi] <= t, cum[t:t + 1, :] - cum_s, NEG_BIG)
            a = q[t:t + 1, :] * k[lo:hi] * jnp.exp(seg)
            sc = jnp.sum(a, axis=1, keepdims=True)
            rows.append(jnp.sum(sc * v[lo:hi], axis=0, keepdims=True))
        o = jnp.concatenate(rows, axis=0)
        s_old = s_ref[...]
        if reverse:
            q_in = q * jnp.exp(tot - cum)
            k_out = k * jnp.exp(cum)
        else:
            q_in = q * jnp.exp(cum)
            k_out = k * jnp.exp(tot - cum)
        o = o + lax.dot_general(q_in.astype(BF16), s_old.astype(BF16), nt, preferred_element_type=F32)
        s_ref[...] = jnp.exp(tot) * s_old + lax.dot_general(
            v.astype(BF16), k_out.astype(BF16), tn, preferred_element_type=F32)
        return o

    sf_ref[...] = s0_ref[0, 0, 0].T
    sb_ref[...] = s0_ref[0, 1, 0].T

    def body(ci, carry):
        row_f = pl.multiple_of(ci * n, n)
        row_b = pl.multiple_of((n_chunks - 1 - ci) * n, n)
        o_ref[pl.ds(row_f, n), :] = chunk(row_f, ff_ref, lb[0:1, :], False, sf_ref)
        ob_ref[pl.ds(row_b, n), :] = chunk(row_b, fb_ref, lb[1:2, :], True, sb_ref)
        return carry

    lax.fori_loop(0, n_chunks, body, 0)
    sfin_ref[0, 0, 0] = sf_ref[...].T
    sfin_ref[0, 1, 0] = sb_ref[...].T

    nw = nw_ref[...]
    blk = min(t_len, ROW_TILE)

    def finish(bi, carry):
        row0 = pl.multiple_of(bi * blk, blk)
        o = o_ref[pl.ds(row0, blk), :] + ob_ref[pl.ds(row0, blk), :]
        ms = jnp.mean(o * o, axis=-1, keepdims=True)
        o = o * lax.rsqrt(ms + NORM_EPS) * nw
        o_ref[pl.ds(row0, blk), :] = o * _silu(g_ref[pl.ds(row0, blk), :])
        return carry

    lax.fori_loop(0, t_len // blk, finish, 0)


def _hgrn(q, f_fw, f_bw, iv, g, o_lb, state0, g_norm, seq, layer):
    m, width = q.shape
    nb = m // seq
    dv = width // HGRN_HEADS
    col = pl.BlockSpec((seq, dv), lambda b, h: (b, h))
    state_spec = pl.BlockSpec((1, 2, 1, HGRN_DK, dv), lambda b, h: (b, 0, h, 0, 0))
    return pl.pallas_call(
        functools.partial(_hgrn_kernel, layer=layer),
        grid=(nb, HGRN_HEADS),
        in_specs=[col, col, col, col, col,
                  pl.BlockSpec((o_lb.shape[0], 2, HGRN_DK), lambda b, h: (0, 0, h)),
                  state_spec,
                  pl.BlockSpec((1, dv), lambda b, h: (0, 0))],
        out_specs=[col, state_spec],
        out_shape=[jax.ShapeDtypeStruct((m, width), F32),
                   jax.ShapeDtypeStruct((nb, 2, HGRN_HEADS, HGRN_DK, dv), F32)],
        scratch_shapes=[pltpu.VMEM((dv, HGRN_DK), F32), pltpu.VMEM((dv, HGRN_DK), F32),
                        pltpu.VMEM((seq, dv), F32)],
        compiler_params=_cparams("arbitrary", "arbitrary"),
        name="hgrn2",
    )(q, f_fw, f_bw, iv, g, o_lb, state0, g_norm.reshape(1, dv))

EVEN_SPLITS = ((0, 512), (512, 640), (640, 768), (768, 1280), (1280, 2048), (2048, 2176))
HGRN_SPLITS = tuple((i * 1024, (i + 1) * 1024) for i in range(5))


def _even_weight(w):
    main = EVEN_SPLITS[-1][0]
    return jnp.pad(w, ((0, 0), (0, LANES - (w.shape[1] - main)))).astype(BF16)


def _run_trunk(x3, mods, mod_row0, P, cache, sc_order=None):
    nb, seq, d = x3.shape
    x = x3.reshape(nb * seq, d)
    depth = P['norm_mix'].shape[0]
    ks, vs, ssd_states, hgrn_states = [], [], [], []
    for l in range(depth):
        j = l // 2
        row0 = (l * SUBLANES + mod_row0, 0 if cache is None else 1)
        if l % 2 == 0:
            q, k, v, z, xbc, dt = _inproj(x, mods, row0, seq, P['norm_mix'][l], P['e_w_in'][j],
                                          EVEN_SPLITS, "even_in_proj")
            q, k = _qkprep(q, k, P['e_q_norm'][j], P['e_k_norm'][j], seq, rope=cache is not None)
            if cache is None:
                s0_f = jnp.zeros((nb, SSD_HEADS, D_STATE, SSD_HEAD_DIM), F32)
                s0_b = s0_f
                o_attn = _ctx_attention(q, k, v, P['e_sink'][j], seq)
            else:
                s0_f, s0_b = cache[2][:, j, 0], cache[2][:, j, 1]
                n_ctx = cache[0].shape[2]
                o_attn = _lat_attention(q, k, v, cache[0][:, j].reshape(nb, n_ctx, -1),
                                        cache[1][:, j].reshape(nb, n_ctx, -1), P['e_sink'][j], seq)
            y, s_f, s_b = _ssd(xbc, dt, z, s0_f, s0_b, P['e_conv_w'][j], P['e_conv_b'][j],
                               P['e_dt_bias'][j], P['e_a_log'][j], P['e_d_skip'][j], P['e_ssd_norm'][j], seq)
            if cache is None:
                ks.append(k.reshape(nb, seq, N_KV_HEADS, HEAD_DIM))
                vs.append(v.reshape(nb, seq, N_KV_HEADS, HEAD_DIM))
                ssd_states.append(jnp.stack([s_f, s_b], axis=1))
            mix = jnp.concatenate([o_attn, y], axis=1)
            x = _outproj(mix, x, mods, row0, seq, P['e_w_out'][j], "even_out_proj")
        else:
            q, f_fw, f_bw, iv, g = _inproj(x, mods, row0, seq, P['norm_mix'][l], P['o_w_in'][j],
                                           HGRN_SPLITS, "odd_in_proj")
            if cache is None:
                s0 = jnp.zeros((nb, 2, HGRN_HEADS, HGRN_DK, d // HGRN_HEADS), F32)
            else:
                s0 = cache[3][:, j]
            o, s_new = _hgrn(q, f_fw, f_bw, iv, g, P['o_lb'], s0, P['o_g_norm'][j], seq, j)
            if cache is None:
                hgrn_states.append(s_new)
            x = _outproj(o, x, mods, row0, seq, P['o_w_out'][j], "odd_out_proj")
        x, y_sc = _peer(x, mods, row0, seq, l, P['norm_ffn'][l], P['p_w_q'][l], P['p_sub_keys'][l],
                        P['p_u'], P['p_v'],
                        (SC_SHARE_CONTEXT if cache is None else SC_SHARE_LATENT)[l],
                        after=None if sc_order is None or cache is None else sc_order[l])
        if sc_order is not None and cache is None:
            sc_order.append(y_sc)
    y = x.reshape(nb, seq, d)
    if cache is not None:
        return y, None
    return y, (jnp.stack(ks, axis=1), jnp.stack(vs, axis=1),
               jnp.stack(ssd_states, axis=1), jnp.stack(hgrn_states, axis=1))


def kernel(x_prompt, x_sample, cache_k, cache_v, state_ssd, state_hgrn, c, c_ctx, w_ada, b_ada, norm_mix, norm_ffn, e_w_in, e_q_norm, e_k_norm, e_sink, e_conv_w, e_conv_b, e_dt_bias, e_a_log, e_d_skip, e_ssd_norm, e_w_out, o_w_in, o_lb, o_g_norm, o_w_out, p_w_q, p_sub_keys, p_u, p_v):
    depth, d, d6 = w_ada.shape
    b_lat = x_sample.shape[0]
    cond_rows = jnp.concatenate([c_ctx[None, :], c, jnp.zeros((SUBLANES - 1 - b_lat, d), F32)], axis=0)
    mods = _modulation(cond_rows, w_ada, b_ada).reshape(depth * SUBLANES, 1, d6)
    P = {
        'norm_mix': norm_mix, 'norm_ffn': norm_ffn,
        'e_w_in': jnp.stack([_even_weight(w) for w in e_w_in]), 'e_q_norm': e_q_norm, 'e_k_norm': e_k_norm,
        'e_sink': e_sink, 'e_conv_w': e_conv_w, 'e_conv_b': e_conv_b, 'e_dt_bias': e_dt_bias,
        'e_a_log': e_a_log, 'e_d_skip': e_d_skip, 'e_ssd_norm': e_ssd_norm,
        'e_w_out': e_w_out.astype(BF16),
        'o_w_in': o_w_in.astype(BF16), 'o_lb': o_lb, 'o_g_norm': o_g_norm, 'o_w_out': o_w_out.astype(BF16),
        'p_w_q': p_w_q.astype(BF16),
        'p_sub_keys': p_sub_keys.astype(BF16).reshape(depth, PEER_HEADS * 2, PEER_NKEYS, PEER_DKEY),
        'p_u': p_u, 'p_v': p_v,
    }
    sc_order = []
    y_prompt, new_state = _run_trunk(x_prompt, mods, 0, P, None, sc_order)
    y_sample, _ = _run_trunk(x_sample, mods, 1, P, (cache_k, cache_v, state_ssd, state_hgrn), sc_order)
    return (y_prompt, y_sample) + new_state
```

```python
import functools
import math

import jax
import jax.numpy as jnp
from jax import lax
from jax.experimental import pallas as pl
from jax.experimental.pallas import tpu as pltpu
from jax.experimental.pallas import tpu_sc as plsc

F32 = jnp.float32
BF16 = jnp.bfloat16
I32 = jnp.int32
HIGHEST = lax.Precision.HIGHEST

NORM_EPS = 1e-6
NEG_BIG = -1e30
LANES = 128
SUBLANES = 8
VMEM_LIMIT = 48 * 1024 * 1024

GRID_W = 64
HEAD_DIM = 64
N_Q_HEADS = 8
N_KV_HEADS = 2
GQA_GROUP = 4
WINDOW = 128
ROPE_THETA = 10000.0
SSD_HEADS = 8
SSD_HEAD_DIM = 64
SSD_GROUPS = 2
D_STATE = 64
CONV_K = 5
HGRN_HEADS = 8
HGRN_DK = 128
HGRN_CHUNK = 32
PEER_HEADS = 8
PEER_NKEYS = 128
PEER_TOPK = 16
PEER_DKEY = 128
PEER_PAIRS = PEER_HEADS * PEER_TOPK

ROW_TILE = 256
PEER_BLOCK = 128
SC_LANES = 16
SC_CHUNK = 16
SC_SHARE_CONTEXT = ((7, 8), (1, 1), (1, 1), (1, 1))
SC_SHARE_LATENT = ((13, 16), (13, 16), (13, 16), (3, 4))
GELU_C = math.sqrt(2.0 / math.pi)


def _cparams(*sem):
    return pltpu.CompilerParams(dimension_semantics=sem, vmem_limit_bytes=VMEM_LIMIT)


def _norm_mod(x, nw, scale, shift):
    ms = jnp.mean(x * x, axis=-1, keepdims=True)
    return (x * lax.rsqrt(ms + NORM_EPS)) * nw * (1.0 + scale) + shift


def _mod_kernel(c_ref, w_ref, b_ref, o_ref):
    c = c_ref[...]
    s = c * jax.nn.sigmoid(c)
    o_ref[0] = jnp.dot(s, w_ref[0], precision=HIGHEST, preferred_element_type=F32) + b_ref[0]


def _modulation(cond_rows, w_ada, b_ada):
    depth, d, n = w_ada.shape
    rows = cond_rows.shape[0]
    return pl.pallas_call(
        _mod_kernel,
        grid=(depth, n // d),
        in_specs=[pl.BlockSpec((rows, d), lambda l, j: (0, 0)),
                  pl.BlockSpec((1, d, d), lambda l, j: (l, 0, j)),
                  pl.BlockSpec((1, 1, d), lambda l, j: (l, 0, j))],
        out_specs=pl.BlockSpec((1, rows, d), lambda l, j: (l, 0, j)),
        out_shape=jax.ShapeDtypeStruct((depth, rows, n), F32),
        compiler_params=_cparams("arbitrary", "arbitrary"),
        name="modulation",
    )(cond_rows, w_ada, b_ada.reshape(depth, 1, n))


def _mod_spec(mod_row0, seq, tile, d6, first_tile=0):
    row0, per_batch = mod_row0
    return pl.BlockSpec((1, 1, d6),
                        lambda i: (row0 + per_batch * (((i + first_tile) * tile) // seq), 0, 0))


def _inproj_kernel(x_ref, m_ref, nw_ref, w_ref, *o_refs, splits, d):
    m = m_ref[0]
    h = _norm_mod(x_ref[...], nw_ref[...], m[:, d:2 * d], m[:, 0:d]).astype(BF16)
    for o_ref, (a, b) in zip(o_refs, splits):
        o_ref[...] = jnp.dot(h, w_ref[:, a:b], preferred_element_type=F32)


def _inproj(x, mods, mod_row0, seq, nw, w_bf16, splits, name):
    m, d = x.shape
    n = w_bf16.shape[1]
    tile = min(ROW_TILE, seq)
    return pl.pallas_call(
        functools.partial(_inproj_kernel, splits=splits, d=d),
        grid=(m // tile,),
        in_specs=[pl.BlockSpec((tile, d), lambda i: (i, 0)),
                  _mod_spec(mod_row0, seq, tile, mods.shape[-1]),
                  pl.BlockSpec((1, d), lambda i: (0, 0)),
                  pl.BlockSpec((d, n), lambda i: (0, 0))],
        out_specs=[pl.BlockSpec((tile, b - a), lambda i: (i, 0)) for a, b in splits],
        out_shape=[jax.ShapeDtypeStruct((m, b - a), F32) for a, b in splits],
        compiler_params=_cparams("arbitrary"),
        name=name,
    )(x, mods, nw.reshape(1, d), w_bf16)


def _outproj_kernel(mix_ref, x_ref, m_ref, w_ref, o_ref, *, d):
    y = jnp.dot(mix_ref[...].astype(BF16), w_ref[...], preferred_element_type=F32)
    o_ref[...] = x_ref[...] + m_ref[0][:, 2 * d:3 * d] * y


def _outproj(mix, x, mods, mod_row0, seq, w_bf16, name):
    m, d = x.shape
    k = mix.shape[1]
    tile = min(ROW_TILE, seq)
    return pl.pallas_call(
        functools.partial(_outproj_kernel, d=d),
        grid=(m // tile,),
        in_specs=[pl.BlockSpec((tile, k), lambda i: (i, 0)),
                  pl.BlockSpec((tile, d), lambda i: (i, 0)),
                  _mod_spec(mod_row0, seq, tile, mods.shape[-1]),
                  pl.BlockSpec((k, d), lambda i: (0, 0))],
        out_specs=pl.BlockSpec((tile, d), lambda i: (i, 0)),
        out_shape=jax.ShapeDtypeStruct((m, d), F32),
        compiler_params=_cparams("arbitrary"),
        name=name,
    )(mix, x, mods, w_bf16)


def _topk_over_rows(s, k, payload=None):
    n = s.shape[0]
    iota = lax.broadcasted_iota(I32, s.shape, 0)
    vals, idxs, pays = [], [], []
    for _ in range(k):
        m = jnp.max(s, axis=0, keepdims=True)
        i = jnp.min(jnp.where(s == m, iota, n), axis=0, keepdims=True)
        hit = iota == i
        vals.append(m)
        idxs.append(i)
        if payload is not None:
            pays.append(jnp.max(jnp.where(hit, payload, -1), axis=0, keepdims=True))
        s = jnp.where(hit, -jnp.inf, s)
    out = (jnp.concatenate(vals, axis=0), jnp.concatenate(idxs, axis=0))
    if payload is not None:
        out += (jnp.concatenate(pays, axis=0),)
    return out


def _peer_route_kernel(x_ref, m_ref, nw_ref, wq_ref, keys_ref, h_ref, e_ref, g_ref, *, d):
    m = m_ref[0]
    h = _norm_mod(x_ref[...], nw_ref[...], m[:, 4 * d:5 * d], m[:, 3 * d:4 * d])
    h_ref[...] = h
    hb = h.astype(BF16)
    nt = (((1,), (1,)), ((), ()))
    for head in range(PEER_HEADS):
        tops = []
        for half in range(2):
            c0 = (head * 2 + half) * PEER_DKEY
            q = jnp.dot(hb, wq_ref[:, c0:c0 + PEER_DKEY], preferred_element_type=F32)
            s = lax.dot_general(keys_ref[head * 2 + half], q.astype(BF16), nt,
                                preferred_element_type=F32)
            tops.append(_topk_over_rows(s, PEER_TOPK))
        (s0, i0), (s1, i1) = tops
        widths = [PEER_TOPK // (a + 1) for a in range(PEER_TOPK)]
        n_pad = -sum(widths) % SUBLANES
        cand_s = jnp.concatenate([s0[a:a + 1] + s1[:w] for a, w in enumerate(widths)]
                                 + [jnp.full((n_pad, s0.shape[1]), -jnp.inf, F32)], axis=0)
        cand_e = jnp.concatenate([i0[a:a + 1] * PEER_NKEYS + i1[:w] for a, w in enumerate(widths)]
                                 + [jnp.zeros((n_pad, s0.shape[1]), I32)], axis=0)
        best_s, _, best_e = _topk_over_rows(cand_s, PEER_TOPK, payload=cand_e)
        p = jnp.exp(best_s - best_s[0:1])
        r0 = head * PEER_TOPK
        e_ref[r0:r0 + PEER_TOPK, :] = best_e
        g_ref[r0:r0 + PEER_TOPK, :] = p / jnp.sum(p, axis=0, keepdims=True)


def _peer_route(x, mods, mod_row0, seq, nw, wq_bf16, keys_bf16):
    m, d = x.shape
    tile = min(ROW_TILE, seq)
    nq = wq_bf16.shape[1]
    return pl.pallas_call(
        functools.partial(_peer_route_kernel, d=d),
        grid=(m // tile,),
        in_specs=[pl.BlockSpec((tile, d), lambda i: (i, 0)),
                  _mod_spec(mod_row0, seq, tile, mods.shape[-1]),
                  pl.BlockSpec((1, d), lambda i: (0, 0)),
                  pl.BlockSpec((d, nq), lambda i: (0, 0)),
                  pl.BlockSpec(keys_bf16.shape, lambda i: (0, 0, 0))],
        out_specs=[pl.BlockSpec((tile, d), lambda i: (i, 0)),
                   pl.BlockSpec((PEER_PAIRS, tile), lambda i: (0, i)),
                   pl.BlockSpec((PEER_PAIRS, tile), lambda i: (0, i))],
        out_shape=[jax.ShapeDtypeStruct((m, d), F32),
                   jax.ShapeDtypeStruct((PEER_PAIRS, m), I32),
                   jax.ShapeDtypeStruct((PEER_PAIRS, m), F32)],
        compiler_params=_cparams("arbitrary"),
        name="peer_route",
    )(x, mods, nw.reshape(1, d), wq_bf16, keys_bf16)


def _peer_gather_kernel(idx_hbm, h_ref, g_ref, x_ref, m_ref, u_hbm, v_hbm, o_ref,
                        idx_smem, ubuf, vbuf, sem_idx, sem_u, sem_v, *, layer, d):
    blk = pl.program_id(0)
    n_groups = PEER_BLOCK // SUBLANES

    cp = pltpu.make_async_copy(idx_hbm.at[blk], idx_smem, sem_idx)
    cp.start()
    cp.wait()

    def issue(tok, slot):
        def body(pair, carry):
            e = idx_smem[tok, pair]
            pltpu.make_async_copy(u_hbm.at[layer, pl.ds(e, 1)], ubuf.at[slot, pl.ds(pair, 1)],
                                  sem_u.at[slot]).start()
            pltpu.make_async_copy(v_hbm.at[layer, pl.ds(e, 1)], vbuf.at[slot, pl.ds(pair, 1)],
                                  sem_v.at[slot]).start()
            return carry
        lax.fori_loop(0, PEER_PAIRS, body, 0, unroll=8)

    def wait(slot):
        pltpu.make_async_copy(u_hbm.at[layer, pl.ds(0, PEER_PAIRS)], ubuf.at[slot], sem_u.at[slot]).wait()
        pltpu.make_async_copy(v_hbm.at[layer, pl.ds(0, PEER_PAIRS)], vbuf.at[slot], sem_v.at[slot]).wait()

    gate2 = m_ref[0][:, 5 * d:6 * d]
    lane = lax.broadcasted_iota(I32, (PEER_PAIRS, PEER_BLOCK), 1)

    issue(0, 0)

    def group(grp, carry):
        base = pl.multiple_of(grp * SUBLANES, SUBLANES)
        h8 = h_ref[pl.ds(base, SUBLANES), :]
        rows = []
        for r in range(SUBLANES):
            tok = base + r
            slot = r % 2
            if r < SUBLANES - 1:
                issue(tok + 1, 1 - slot)
            else:
                @pl.when(grp < n_groups - 1)
                def _():
                    issue(tok + 1, 1 - slot)
            wait(slot)
            act = jnp.sum(ubuf[slot] * h8[r:r + 1, :], axis=1, keepdims=True)
            gate = jnp.sum(jnp.where(lane == tok, g_ref[...], 0.0), axis=1, keepdims=True)
            w = jax.nn.gelu(act) * gate
            rows.append(jnp.sum(vbuf[slot] * w, axis=0, keepdims=True))
        out8 = jnp.concatenate(rows, axis=0)
        o_ref[pl.ds(base, SUBLANES), :] = x_ref[pl.ds(base, SUBLANES), :] + gate2 * out8
        return carry

    lax.fori_loop(0, n_groups, group, 0)


def _peer_gather(idx, h, gates, x, mods, mod_row0, seq, p_u, p_v, layer, first_block):
    m, d = x.shape
    nblk = m // PEER_BLOCK - first_block
    idx3 = idx[:, first_block * PEER_BLOCK:].T.reshape(nblk, PEER_BLOCK, PEER_PAIRS)
    rows = pl.BlockSpec((PEER_BLOCK, d), lambda i: (i + first_block, 0))
    return pl.pallas_call(
        functools.partial(_peer_gather_kernel, layer=layer, d=d),
        grid=(nblk,),
        in_specs=[pl.BlockSpec(memory_space=pl.ANY),
                  rows,
                  pl.BlockSpec((PEER_PAIRS, PEER_BLOCK), lambda i: (0, i + first_block)),
                  rows,
                  _mod_spec(mod_row0, seq, PEER_BLOCK, mods.shape[-1], first_block),
                  pl.BlockSpec(memory_space=pl.ANY),
                  pl.BlockSpec(memory_space=pl.ANY)],
        out_specs=pl.BlockSpec((PEER_BLOCK, d), lambda i: (i, 0)),
        out_shape=jax.ShapeDtypeStruct((nblk * PEER_BLOCK, d), F32),
        scratch_shapes=[pltpu.SMEM((PEER_BLOCK, PEER_PAIRS), I32),
                        pltpu.VMEM((2, PEER_PAIRS, d), F32),
                        pltpu.VMEM((2, PEER_PAIRS, d), F32),
                        pltpu.SemaphoreType.DMA,
                        pltpu.SemaphoreType.DMA((2,)),
                        pltpu.SemaphoreType.DMA((2,))],
        compiler_params=_cparams("arbitrary"),
        name="peer_gather",
    )(idx3, h, gates, x, mods, p_u, p_v)


def _peer_experts_sc(idx, h, gates, u_rows, v_rows):
    m = idx.shape[0]
    d = h.shape[1]
    info = plsc.get_sparse_core_info()
    n_workers = info.num_cores * info.num_subcores
    per = m // n_workers
    n_chunks = PEER_PAIRS // SC_CHUNK
    n_vec = d // SC_LANES
    mesh = plsc.VectorSubcoreMesh(core_axis_name="c", subcore_axis_name="s")

    @functools.partial(
        pl.kernel, out_type=jax.ShapeDtypeStruct((m, d), F32), mesh=mesh,
        scratch_types=[pltpu.VMEM((2, PEER_PAIRS), I32), pltpu.VMEM((2, d), F32),
                       pltpu.VMEM((2, PEER_PAIRS), F32), pltpu.VMEM((2, d), F32),
                       pltpu.VMEM((2, SC_CHUNK, d), F32), pltpu.VMEM((2, SC_CHUNK, d), F32),
                       pltpu.SemaphoreType.DMA((2,)), pltpu.SemaphoreType.DMA((2,)),
                       pltpu.SemaphoreType.DMA((2,)), pltpu.SemaphoreType.DMA((2,))],
        compiler_params=pltpu.CompilerParams(needs_layout_passes=False),
        name="peer_experts_sc")
    def body(idx_hbm, h_hbm, g_hbm, u_hbm, v_hbm, o_hbm,
             idx_v, x_v, g_v, out_v, ubuf, vbuf, sem_meta, sem_out, sem_u, sem_v):
        wid = lax.axis_index("c") * info.num_subcores + lax.axis_index("s")
        tok0 = wid * per
        lane = lax.iota(I32, SC_LANES)

        def meta_copies(ti, ms):
            t = tok0 + ti
            return (pltpu.make_async_copy(idx_hbm.at[t], idx_v.at[ms], sem_meta.at[ms]),
                    pltpu.make_async_copy(h_hbm.at[t], x_v.at[ms], sem_meta.at[ms]),
                    pltpu.make_async_copy(g_hbm.at[t], g_v.at[ms], sem_meta.at[ms]))

        def gather_copies(ms, c, slot):
            ids = idx_v.at[ms, pl.ds(c * SC_CHUNK, SC_CHUNK)]
            return (pltpu.make_async_copy(u_hbm.at[ids], ubuf.at[slot], sem_u.at[slot]),
                    pltpu.make_async_copy(v_hbm.at[ids], vbuf.at[slot], sem_v.at[slot]))

        def out_copy(ti, ms):
            return pltpu.make_async_copy(out_v.at[ms], o_hbm.at[tok0 + ti], sem_out.at[ms])

        for cp in meta_copies(0, 0):
            cp.start()
        for cp in meta_copies(0, 0):
            cp.wait()
        for cp in gather_copies(0, 0, 0):
            cp.start()

        def token(ti, carry):
            ms = ti % 2
            nxt = 1 - ms

            @pl.when(ti + 1 < per)
            def _():
                for cp in meta_copies(ti + 1, nxt):
                    cp.start()

            @pl.when(ti >= 2)
            def _():
                out_copy(ti - 2, ms).wait()

            def zero(j, c):
                out_v[ms, pl.ds(j * SC_LANES, SC_LANES)] = jnp.zeros((SC_LANES,), F32)
                return c
            lax.fori_loop(0, n_vec, zero, 0)

            for c in range(n_chunks):
                slot = c % 2
                if c + 1 < n_chunks:
                    for cp in gather_copies(ms, c + 1, 1 - slot):
                        cp.start()
                else:
                    @pl.when(ti + 1 < per)
                    def _():
                        for cp in meta_copies(ti + 1, nxt):
                            cp.wait()
                        for cp in gather_copies(nxt, 0, 1 - slot):
                            cp.start()
                cu, cv = gather_copies(ms, c, slot)
                cu.wait()

                def udot(j, accs):
                    xj = x_v[ms, pl.ds(j * SC_LANES, SC_LANES)]
                    return tuple(accs[r] + ubuf[slot, r, pl.ds(j * SC_LANES, SC_LANES)] * xj
                                 for r in range(SC_CHUNK))
                accs = lax.fori_loop(0, n_vec, udot,
                                     tuple(jnp.zeros((SC_LANES,), F32) for _ in range(SC_CHUNK)))
                act = jnp.zeros((SC_LANES,), F32)
                for r in range(SC_CHUNK):
                    act = jnp.where(lane == r, jnp.sum(accs[r]), act)
                y = GELU_C * (act + 0.044715 * (act * act * act))
                w = act / (1.0 + jnp.exp(-2.0 * y)) * g_v[ms, pl.ds(c * SC_CHUNK, SC_CHUNK)]
                ws = [jnp.sum(jnp.where(lane == r, w, 0.0)) for r in range(SC_CHUNK)]
                cv.wait()

                @plsc.parallel_loop(0, n_vec, unroll=2)
                def _(j):
                    parts = [ws[r] * vbuf[slot, r, pl.ds(j * SC_LANES, SC_LANES)] for r in range(SC_CHUNK)]
                    while len(parts) > 1:
                        parts = [parts[i] + parts[i + 1] for i in range(0, len(parts), 2)]
                    plsc.addupdate(out_v.at[ms, pl.ds(j * SC_LANES, SC_LANES)], parts[0])

            out_copy(ti, ms).start()
            return carry

        lax.fori_loop(0, per, token, 0)
        for back in (2, 1):
            if per >= back:
                out_copy(per - back, (per - back) % 2).wait()

    return body(idx, h, gates, u_rows, v_rows)


def _residual_kernel(x_ref, y_ref, m_ref, o_ref, *, d):
    o_ref[...] = x_ref[...] + m_ref[0][:, 5 * d:6 * d] * y_ref[...]


def _residual(x, y, mods, mod_row0, seq):
    m, d = y.shape
    tile = min(ROW_TILE, seq)
    return pl.pallas_call(
        functools.partial(_residual_kernel, d=d),
        grid=(m // tile,),
        in_specs=[pl.BlockSpec((tile, d), lambda i: (i, 0)),
                  pl.BlockSpec((tile, d), lambda i: (i, 0)),
                  _mod_spec(mod_row0, seq, tile, mods.shape[-1])],
        out_specs=pl.BlockSpec((tile, d), lambda i: (i, 0)),
        out_shape=jax.ShapeDtypeStruct((m, d), F32),
        compiler_params=_cparams("arbitrary"),
        name="peer_residual",
    )(x, y, mods)


def _peer(x, mods, mod_row0, seq, layer, nw, wq_bf16, keys_bf16, p_u, p_v, sc_share, after=None):
    m, d = x.shape
    h, idx, gates = _peer_route(x, mods, mod_row0, seq, nw, wq_bf16, keys_bf16)
    n_blocks = m // PEER_BLOCK
    sc_blocks = (n_blocks * sc_share[0]) // sc_share[1]
    m_sc = sc_blocks * PEER_BLOCK
    n_experts = p_u.shape[1]
    idx_sc = idx[:, :m_sc].T + layer * n_experts
    if after is not None:
        idx_sc, _ = lax.optimization_barrier((idx_sc, after))
    y_sc = _peer_experts_sc(idx_sc, h, gates[:, :m_sc].T,
                            p_u.reshape(-1, d), p_v.reshape(-1, d))
    x_sc = _residual(x, y_sc, mods, mod_row0, seq)
    if sc_blocks == n_blocks:
        return x_sc, y_sc
    x_tc = _peer_gather(idx, h, gates, x, mods, mod_row0, seq, p_u, p_v, layer, sc_blocks)
    return jnp.concatenate([x_sc, x_tc], axis=0), y_sc


def _head_mean_square(x):
    n = x.shape[1]
    r = lax.broadcasted_iota(I32, (n, n), 0) // HEAD_DIM
    c = lax.broadcasted_iota(I32, (n, n), 1) // HEAD_DIM
    seg = jnp.where(r == c, 1.0 / HEAD_DIM, 0.0).astype(F32)
    return jnp.dot(x * x, seg, precision=HIGHEST, preferred_element_type=F32)


def _swap_rot_halves(x):
    n = x.shape[1]
    quarter = HEAD_DIM // 4
    lane = lax.broadcasted_iota(I32, x.shape, 1)
    lo = (lane % (2 * quarter)) < quarter
    return jnp.where(lo, pltpu.roll(x, n - quarter, axis=1), pltpu.roll(x, quarter, axis=1))


def _qkprep_kernel(q_ref, k_ref, qw_ref, kw_ref, *rest, rope):
    if rope:
        cos_ref, sin_ref, qo_ref, ko_ref = rest
    else:
        qo_ref, ko_ref = rest
    q = q_ref[...]
    k = k_ref[...]
    q = q * lax.rsqrt(_head_mean_square(q) + NORM_EPS) * qw_ref[...]
    k = k * lax.rsqrt(_head_mean_square(k) + NORM_EPS) * kw_ref[...]
    if rope:
        cos = cos_ref[...]
        sin = sin_ref[...]
        cq = jnp.concatenate([cos] * (q.shape[1] // LANES), axis=1)
        sq = jnp.concatenate([sin] * (q.shape[1] // LANES), axis=1)
        q = q * cq + _swap_rot_halves(q) * sq
        k = k * cos + _swap_rot_halves(k) * sin
    qo_ref[...] = q
    ko_ref[...] = k


def _rope_tables(seq):
    axis_dim = HEAD_DIM // 2
    inv_freq = ROPE_THETA ** (-jnp.arange(0, axis_dim, 2, dtype=F32) / axis_dim)
    t = jnp.arange(seq)
    pos = jnp.stack([(t // GRID_W).astype(F32), (t % GRID_W).astype(F32)], axis=1)
    lane = jnp.arange(LANES)
    dd = lane % HEAD_DIM
    ang = pos[:, dd // axis_dim] * inv_freq[dd % (axis_dim // 2)][None, :]
    sign = jnp.where((dd % axis_dim) < axis_dim // 2, -1.0, 1.0).astype(F32)
    return jnp.cos(ang), jnp.sin(ang) * sign[None, :]


def _qkprep(q, k, qw, kw, seq, rope):
    m, nq = q.shape
    nk = k.shape[1]
    tile = min(ROW_TILE, seq)
    qw_row = jnp.tile(qw, nq // HEAD_DIM).reshape(1, nq)
    kw_row = jnp.tile(kw, nk // HEAD_DIM).reshape(1, nk)
    in_specs = [pl.BlockSpec((tile, nq), lambda i: (i, 0)),
                pl.BlockSpec((tile, nk), lambda i: (i, 0)),
                pl.BlockSpec((1, nq), lambda i: (0, 0)),
                pl.BlockSpec((1, nk), lambda i: (0, 0))]
    args = [q, k, qw_row, kw_row]
    if rope:
        cos, sin = _rope_tables(seq)
        per_seq = seq // tile
        in_specs += [pl.BlockSpec((tile, LANES), lambda i: (i % per_seq, 0)),
                     pl.BlockSpec((tile, LANES), lambda i: (i % per_seq, 0))]
        args += [cos, sin]
    return pl.pallas_call(
        functools.partial(_qkprep_kernel, rope=rope),
        grid=(m // tile,),
        in_specs=in_specs,
        out_specs=[pl.BlockSpec((tile, nq), lambda i: (i, 0)),
                   pl.BlockSpec((tile, nk), lambda i: (i, 0))],
        out_shape=[jax.ShapeDtypeStruct((m, nq), F32), jax.ShapeDtypeStruct((m, nk), F32)],
        compiler_params=_cparams("arbitrary"),
        name="qk_prep",
    )(*args)


def _dup_halves(x):
    lane = lax.broadcasted_iota(I32, x.shape, 1)
    sw = pltpu.roll(x, HEAD_DIM, axis=1)
    lo = lane < HEAD_DIM
    return jnp.where(lo, x, sw), jnp.where(lo, sw, x)


def _attend(q, k_all, v_all, sink_ref, mask):
    scale = HEAD_DIM ** -0.5
    nt = (((1,), (1,)), ((), ()))
    kk = [a.astype(BF16) for a in _dup_halves(k_all)]
    vv = [a.astype(BF16) for a in _dup_halves(v_all)]
    lane = lax.broadcasted_iota(I32, (q.shape[0], LANES), 1)
    lo = lane < HEAD_DIM
    tiles = []
    for t in range(q.shape[1] // LANES):
        qt = q[:, t * LANES:(t + 1) * LANES]
        g = (2 * t) // GQA_GROUP
        halves = []
        for hh in range(2):
            head = 2 * t + hh
            qm = jnp.where(lo if hh == 0 else ~lo, qt, 0.0).astype(BF16)
            s = lax.dot_general(qm, kk[g], nt, preferred_element_type=F32) * scale
            if mask is not None:
                s = jnp.where(mask, s, NEG_BIG)
            sink = sink_ref[head]
            mx = jnp.maximum(jnp.max(s, axis=1, keepdims=True), sink)
            p = jnp.exp(s - mx)
            den = jnp.sum(p, axis=1, keepdims=True) + jnp.exp(sink - mx)
            p = (p / den).astype(BF16)
            halves.append(jnp.dot(p, vv[g], preferred_element_type=F32))
        tiles.append(jnp.where(lo, halves[0], halves[1]))
    return jnp.concatenate(tiles, axis=1)


def _ctx_attn_kernel(sink_ref, q_ref, k_ref, v_ref, o_ref):
    o_ref[...] = _attend(q_ref[...], k_ref[...], v_ref[...], sink_ref, None)


def _ctx_attention(q, k, v, sink, seq):
    m, nq = q.shape
    nk = k.shape[1]
    return pl.pallas_call(
        _ctx_attn_kernel,
        grid=(m // seq,),
        in_specs=[pl.BlockSpec(memory_space=pltpu.SMEM),
                  pl.BlockSpec((seq, nq), lambda b: (b, 0)),
                  pl.BlockSpec((seq, nk), lambda b: (b, 0)),
                  pl.BlockSpec((seq, nk), lambda b: (b, 0))],
        out_specs=pl.BlockSpec((seq, nq), lambda b: (b, 0)),
        out_shape=jax.ShapeDtypeStruct((m, nq), F32),
        compiler_params=_cparams("arbitrary"),
        name="ctx_attention",
    )(sink, q, k, v)


def _lat_attn_kernel(sink_ref, q_ref, kc_ref, vc_ref, kp_ref, k0_ref, kn_ref, vp_ref, v0_ref, vn_ref,
                     o_ref, *, seq):
    qb = pl.program_id(1)
    blk = q_ref.shape[0]
    n_ctx = kc_ref.shape[1]
    k_all = jnp.concatenate([kc_ref[0], kp_ref[...], k0_ref[...], kn_ref[...]], axis=0)
    v_all = jnp.concatenate([vc_ref[0], vp_ref[...], v0_ref[...], vn_ref[...]], axis=0)
    tk = k_all.shape[0]
    qpos = qb * blk + lax.broadcasted_iota(I32, (blk, tk), 0)
    col = lax.broadcasted_iota(I32, (blk, tk), 1)
    kpos = (qb - 1) * blk + col - n_ctx
    local_ok = (jnp.abs(qpos - kpos) <= WINDOW) & (kpos >= 0) & (kpos < seq)
    mask = (col < n_ctx) | local_ok
    o_ref[...] = _attend(q_ref[...], k_all, v_all, sink_ref, mask)


def _lat_attention(q, k, v, k_ctx, v_ctx, sink, seq):
    m, nq = q.shape
    nk = k.shape[1]
    blk = WINDOW
    nb = seq // blk
    n_ctx = k_ctx.shape[1]
    last = m // blk - 1

    def kv_spec(shift):
        return pl.BlockSpec((blk, nk), lambda b, i: (jnp.clip(b * nb + i + shift, 0, last), 0))

    ctx_spec = pl.BlockSpec((1, n_ctx, nk), lambda b, i: (b, 0, 0))
    return pl.pallas_call(
        functools.partial(_lat_attn_kernel, seq=seq),
        grid=(m // seq, nb),
        in_specs=[pl.BlockSpec(memory_space=pltpu.SMEM),
                  pl.BlockSpec((blk, nq), lambda b, i: (b * nb + i, 0)),
                  ctx_spec, ctx_spec,
                  kv_spec(-1), kv_spec(0), kv_spec(1),
                  kv_spec(-1), kv_spec(0), kv_spec(1)],
        out_specs=pl.BlockSpec((blk, nq), lambda b, i: (b * nb + i, 0)),
        out_shape=jax.ShapeDtypeStruct((m, nq), F32),
        compiler_params=_cparams("arbitrary", "arbitrary"),
        name="lat_attention",
    )(sink, q, k_ctx, v_ctx, k, k, k, v, v, v)

SSD_BLOCK = 256
SSD_PAIRS = SSD_HEADS // 2
SSD_INNER = SSD_HEADS * SSD_HEAD_DIM
HALO = SUBLANES


def _softplus(x):
    return jnp.maximum(x, 0.0) + jnp.log1p(jnp.exp(-jnp.abs(x)))


def _silu(x):
    return x * jax.nn.sigmoid(x)


def _ssd_decays(dt_raw, bias, a_log):
    n = dt_raw.shape[0]
    dt = _softplus(dt_raw + bias)
    log_a = dt * (-jnp.exp(a_log))
    r = lax.broadcasted_iota(I32, (n, n), 0)
    c = lax.broadcasted_iota(I32, (n, n), 1)
    lower = jnp.where(c <= r, 1.0, 0.0).astype(F32)
    upper = jnp.where(r <= c, 1.0, 0.0).astype(F32)
    cum_col = jnp.dot(lower, log_a, precision=HIGHEST, preferred_element_type=F32)
    dt_row = dt.T
    la_row = log_a.T
    cum_row = jnp.dot(la_row, upper, precision=HIGHEST, preferred_element_type=F32)
    return dt, log_a, cum_col, dt_row, la_row, cum_row


def _ssd_scan_chunk(xs, bmat, cmat, w_of, q_scale_of, k_scale_of, carry_of, s_ref):
    nt = (((1,), (1,)), ((), ()))
    n = xs.shape[0]
    lane = lax.broadcasted_iota(I32, (n, LANES), 1)
    lo = lane < SSD_HEAD_DIM
    lane_s = lax.broadcasted_iota(I32, (D_STATE, LANES), 1)
    lo_s = lane_s < SSD_HEAD_DIM
    b_t = bmat.T
    cb16 = cmat.astype(BF16)
    ys = []
    for pair in range(SSD_PAIRS):
        g = (2 * pair) // (SSD_HEADS // SSD_GROUPS)
        in_g = (lane // D_STATE) == g
        cg = jnp.where(in_g, cmat, 0.0)
        cb = lax.dot_general(cg.astype(BF16), bmat.astype(BF16), nt, preferred_element_type=F32)
        x_pair = xs[:, pair * LANES:(pair + 1) * LANES]
        x16 = x_pair.astype(BF16)
        s_old = s_ref[pair]
        s2 = jnp.concatenate([s_old, s_old], axis=0).astype(BF16)
        bg_t = b_t[g * D_STATE:(g + 1) * D_STATE, :]
        y_h, s_h = [], []
        for hh in range(2):
            h = 2 * pair + hh
            w = (cb * w_of(h)).astype(BF16)
            y = jnp.dot(w, x16, preferred_element_type=F32)
            cq = (cg * q_scale_of(h)).astype(BF16)
            y = y + jnp.dot(cq, s2, preferred_element_type=F32)
            y_h.append(y)
            kt = (bg_t * k_scale_of(h)).astype(BF16)
            s_h.append(carry_of(h) * s_old + jnp.dot(kt, x16, preferred_element_type=F32))
        ys.append(jnp.where(lo, y_h[0], y_h[1]))
        s_ref[pair] = jnp.where(lo_s, s_h[0], s_h[1])
    return jnp.concatenate(ys, axis=1)


def _ssd_fwd_kernel(x_ref, xp_ref, xn_ref, dt_ref, s0_ref, cw_ref, cb_ref, bias_ref, alog_ref,
                    y_ref, xc_ref, sfin_ref, s_ref):
    c = pl.program_id(1)
    nc = pl.num_programs(1)
    n = x_ref.shape[0]

    @pl.when(c == 0)
    def _():
        s_ref[...] = s0_ref[0]

    prev = jnp.where(c > 0, xp_ref[...], 0.0)
    nxt = jnp.where(c < nc - 1, xn_ref[...], 0.0)
    xe = jnp.concatenate([prev, x_ref[...], nxt], axis=0)
    pad = (CONV_K - 1) // 2
    acc = cb_ref[...] + cw_ref[0:1, :] * xe[HALO - pad:HALO - pad + n, :]
    for k in range(1, CONV_K):
        acc = acc + cw_ref[k:k + 1, :] * xe[HALO - pad + k:HALO - pad + k + n, :]
    xc = _silu(acc)
    xc_ref[...] = xc
    xs = xc[:, :SSD_INNER]
    bmat = xc[:, SSD_INNER:SSD_INNER + LANES]
    cmat = xc[:, SSD_INNER + LANES:SSD_INNER + 2 * LANES]

    dt, log_a, cum_col, dt_row, la_row, cum_row = _ssd_decays(dt_ref[...], bias_ref[...], alog_ref[...])
    r = lax.broadcasted_iota(I32, (n, n), 0)
    cc = lax.broadcasted_iota(I32, (n, n), 1)
    causal = cc <= r
    last_col = cum_col[n - 1:n, :]

    def w_of(h):
        seg = cum_col[:, h:h + 1] - cum_row[h:h + 1, :]
        return jnp.exp(jnp.where(causal, seg, NEG_BIG)) * dt_row[h:h + 1, :]

    def q_scale_of(h):
        return jnp.exp(cum_col[:, h:h + 1])

    def k_scale_of(h):
        return dt_row[h:h + 1, :] * jnp.exp(cum_row[h:h + 1, n - 1:n] - cum_row[h:h + 1, :])

    def carry_of(h):
        return jnp.exp(last_col[:, h:h + 1])

    y_ref[...] = _ssd_scan_chunk(xs, bmat, cmat, w_of, q_scale_of, k_scale_of, carry_of, s_ref)

    @pl.when(c == nc - 1)
    def _():
        sfin_ref[0] = s_ref[...]


def _ssd_bwd_kernel(xc_ref, dt_ref, yf_ref, z_ref, s0_ref, bias_ref, alog_ref, dskip_ref, nw_ref,
                    y_ref, sfin_ref, s_ref):
    c = pl.program_id(1)
    nc = pl.num_programs(1)
    n = xc_ref.shape[0]

    @pl.when(c == 0)
    def _():
        s_ref[...] = s0_ref[0]

    xc = xc_ref[...]
    xs = xc[:, :SSD_INNER]
    bmat = xc[:, SSD_INNER:SSD_INNER + LANES]
    cmat = xc[:, SSD_INNER + LANES:SSD_INNER + 2 * LANES]
    dt, log_a, cum_col, dt_row, la_row, cum_row = _ssd_decays(dt_ref[...], bias_ref[...], alog_ref[...])
    ex_col = cum_col - log_a
    ex_row = cum_row - la_row
    r = lax.broadcasted_iota(I32, (n, n), 0)
    cc = lax.broadcasted_iota(I32, (n, n), 1)
    anti = cc >= r
    tot_col = cum_col[n - 1:n, :]
    off = SSD_HEADS

    def w_of(h):
        j = off + h
        seg = ex_row[j:j + 1, :] - ex_col[:, j:j + 1]
        return jnp.exp(jnp.where(anti, seg, NEG_BIG)) * dt_row[j:j + 1, :]

    def q_scale_of(h):
        j = off + h
        return jnp.exp(tot_col[:, j:j + 1] - ex_col[:, j:j + 1])

    def k_scale_of(h):
        j = off + h
        return dt_row[j:j + 1, :] * jnp.exp(ex_row[j:j + 1, :])

    def carry_of(h):
        j = off + h
        return jnp.exp(tot_col[:, j:j + 1])

    y_b = _ssd_scan_chunk(xs, bmat, cmat, w_of, q_scale_of, k_scale_of, carry_of, s_ref)
    y = yf_ref[...] + y_b + dskip_ref[...] * xs
    y = y * _silu(z_ref[...])
    ms = jnp.mean(y * y, axis=-1, keepdims=True)
    y_ref[...] = y * lax.rsqrt(ms + NORM_EPS) * nw_ref[...]

    @pl.when(c == nc - 1)
    def _():
        sfin_ref[0] = s_ref[...]


def _pair_states(s):
    b, h, n, p = s.shape
    return s.reshape(b, h // 2, 2, n, p).transpose(0, 1, 3, 2, 4).reshape(b, h // 2, n, 2 * p)


def _unpair_states(s):
    b, hp, n, p2 = s.shape
    return s.reshape(b, hp, n, 2, p2 // 2).transpose(0, 1, 3, 2, 4).reshape(b, hp * 2, n, p2 // 2)


def _ssd(xbc, dt, z, s0_f, s0_b, conv_w, conv_b, dt_bias, a_log, d_skip, ssd_norm, seq):
    m, nx = xbc.shape
    nb = m // seq
    blk = min(SSD_BLOCK, seq)
    nc = seq // blk
    hb = blk // HALO
    n_halo = m // HALO
    pad16 = lambda a: jnp.pad(a.reshape(1, -1), ((0, 0), (0, LANES - a.size)))
    bias = pad16(dt_bias)
    alog = pad16(a_log)
    state_spec = pl.BlockSpec((1, SSD_PAIRS, D_STATE, LANES), lambda b, c: (b, 0, 0, 0))
    state_shape = jax.ShapeDtypeStruct((nb, SSD_PAIRS, D_STATE, LANES), F32)
    row = lambda width: pl.BlockSpec((1, width), lambda b, c: (0, 0))

    def fwd_rows(width):
        return pl.BlockSpec((blk, width), lambda b, c: (b * nc + c, 0))

    def bwd_rows(width):
        return pl.BlockSpec((blk, width), lambda b, c: (b * nc + nc - 1 - c, 0))

    y_f, xc, s_f = pl.pallas_call(
        _ssd_fwd_kernel,
        grid=(nb, nc),
        in_specs=[fwd_rows(nx),
                  pl.BlockSpec((HALO, nx), lambda b, c: (jnp.maximum((b * nc + c) * hb - 1, 0), 0)),
                  pl.BlockSpec((HALO, nx), lambda b, c: (jnp.minimum((b * nc + c + 1) * hb, n_halo - 1), 0)),
                  fwd_rows(LANES), state_spec,
                  pl.BlockSpec((CONV_K, nx), lambda b, c: (0, 0)), row(nx), row(LANES), row(LANES)],
        out_specs=[fwd_rows(SSD_INNER), fwd_rows(nx), state_spec],
        out_shape=[jax.ShapeDtypeStruct((m, SSD_INNER), F32), jax.ShapeDtypeStruct((m, nx), F32), state_shape],
        scratch_shapes=[pltpu.VMEM((SSD_PAIRS, D_STATE, LANES), F32)],
        compiler_params=_cparams("arbitrary", "arbitrary"),
        name="ssd_forward",
    )(xbc, xbc, xbc, dt, _pair_states(s0_f), conv_w, conv_b.reshape(1, nx), bias, alog)

    dskip = jnp.repeat(d_skip, SSD_HEAD_DIM).reshape(1, SSD_INNER)
    y, s_b = pl.pallas_call(
        _ssd_bwd_kernel,
        grid=(nb, nc),
        in_specs=[bwd_rows(nx), bwd_rows(LANES), bwd_rows(SSD_INNER), bwd_rows(SSD_INNER), state_spec,
                  row(LANES), row(LANES), row(SSD_INNER), row(SSD_INNER)],
        out_specs=[bwd_rows(SSD_INNER), state_spec],
        out_shape=[jax.ShapeDtypeStruct((m, SSD_INNER), F32), state_shape],
        scratch_shapes=[pltpu.VMEM((SSD_PAIRS, D_STATE, LANES), F32)],
        compiler_params=_cparams("arbitrary", "arbitrary"),
        name="ssd_backward",
    )(xc, dt, y_f, z, _pair_states(s0_b), bias, alog, dskip, ssd_norm.reshape(1, SSD_INNER))
    return y, _unpair_states(s_f), _unpair_states(s_b)

def _hgrn_kernel(q_ref, ff_ref, fb_ref, i_ref, g_ref, lb_ref, s0_ref, nw_ref, o_ref, sfin_ref,
                 sf_ref, sb_ref, ob_ref, *, layer):
    t_len = q_ref.shape[0]
    n = HGRN_CHUNK
    n_chunks = t_len // n
    tn = (((0,), (0,)), ((), ()))
    nt = (((1,), (1,)), ((), ()))

    lbp = lb_ref[...]
    e = jnp.exp(lbp - jnp.max(lbp, axis=0, keepdims=True))
    sm = e / jnp.sum(e, axis=0, keepdims=True)
    lb = sm[0] * 0.0
    for j in range(1, layer + 1):
        lb = lb + sm[j]

    r = lax.broadcasted_iota(I32, (n, n), 0)
    c = lax.broadcasted_iota(I32, (n, n), 1)
    lower = jnp.where(c <= r, 1.0, 0.0).astype(F32)
    srow = lax.broadcasted_iota(I32, (n, HGRN_DK), 0)
    qscale = HGRN_DK ** -0.5

    def chunk(row0, f_ref, lb_d, reverse, s_ref):
        q = _silu(q_ref[pl.ds(row0, n), :]) * qscale
        f = f_ref[pl.ds(row0, n), :]
        v = i_ref[pl.ds(row0, n), :]
        k = (1.0 - lb_d) * jax.nn.sigmoid(-f)
        lf = jnp.log(lb_d + (1.0 - lb_d) * jax.nn.sigmoid(f))
        cum = jnp.dot(lower, lf, precision=HIGHEST, preferred_element_type=F32)
        tot = cum[n - 1:n, :]
        if reverse:
            cum = cum - lf
        rows = []
        for t in range(n):
            tile0 = (t // SUBLANES) * SUBLANES
            lo, hi = (tile0, n) if reverse else (0, tile0 + SUBLANES)
            cum_s = cum[lo:hi]
            if reverse:
                seg = jnp.where(srow[lo:hi] >= t, cum_s - cum[t:t + 1, :], NEG_BIG)
            else:
                seg = jnp.where(srow[lo:hi] <= t, cum[t:t + 1, :] - cum_s, NEG_BIG)
            a = q[t:t + 1, :] * k[lo:hi] * jnp.exp(seg)
            sc = jnp.sum(a, axis=1, keepdims=True)
            rows.append(jnp.sum(sc * v[lo:hi], axis=0, keepdims=True))
        o = jnp.concatenate(rows, axis=0)
        s_old = s_ref[...]
        if reverse:
            q_in = q * jnp.exp(tot - cum)
            k_out = k * jnp.exp(cum)
        else:
            q_in = q * jnp.exp(cum)
            k_out = k * jnp.exp(tot - cum)
        o = o + lax.dot_general(q_in.astype(BF16), s_old.astype(BF16), nt, preferred_element_type=F32)
        s_ref[...] = jnp.exp(tot) * s_old + lax.dot_general(
            v.astype(BF16), k_out.astype(BF16), tn, preferred_element_type=F32)
        return o

    sf_ref[...] = s0_ref[0, 0, 0].T
    sb_ref[...] = s0_ref[0, 1, 0].T

    def body(ci, carry):
        row_f = pl.multiple_of(ci * n, n)
        row_b = pl.multiple_of((n_chunks - 1 - ci) * n, n)
        o_ref[pl.ds(row_f, n), :] = chunk(row_f, ff_ref, lb[0:1, :], False, sf_ref)
        ob_ref[pl.ds(row_b, n), :] = chunk(row_b, fb_ref, lb[1:2, :], True, sb_ref)
        return carry

    lax.fori_loop(0, n_chunks, body, 0)
    sfin_ref[0, 0, 0] = sf_ref[...].T
    sfin_ref[0, 1, 0] = sb_ref[...].T

    nw = nw_ref[...]
    blk = min(t_len, ROW_TILE)

    def finish(bi, carry):
        row0 = pl.multiple_of(bi * blk, blk)
        o = o_ref[pl.ds(row0, blk), :] + ob_ref[pl.ds(row0, blk), :]
        ms = jnp.mean(o * o, axis=-1, keepdims=True)
        o = o * lax.rsqrt(ms + NORM_EPS) * nw
        o_ref[pl.ds(row0, blk), :] = o * _silu(g_ref[pl.ds(row0, blk), :])
        return carry

    lax.fori_loop(0, t_len // blk, finish, 0)


def _hgrn(q, f_fw, f_bw, iv, g, o_lb, state0, g_norm, seq, layer):
    m, width = q.shape
    nb = m // seq
    dv = width // HGRN_HEADS
    col = pl.BlockSpec((seq, dv), lambda b, h: (b, h))
    state_spec = pl.BlockSpec((1, 2, 1, HGRN_DK, dv), lambda b, h: (b, 0, h, 0, 0))
    return pl.pallas_call(
        functools.partial(_hgrn_kernel, layer=layer),
        grid=(nb, HGRN_HEADS),
        in_specs=[col, col, col, col, col,
                  pl.BlockSpec((o_lb.shape[0], 2, HGRN_DK), lambda b, h: (0, 0, h)),
                  state_spec,
                  pl.BlockSpec((1, dv), lambda b, h: (0, 0))],
        out_specs=[col, state_spec],
        out_shape=[jax.ShapeDtypeStruct((m, width), F32),
                   jax.ShapeDtypeStruct((nb, 2, HGRN_HEADS, HGRN_DK, dv), F32)],
        scratch_shapes=[pltpu.VMEM((dv, HGRN_DK), F32), pltpu.VMEM((dv, HGRN_DK), F32),
                        pltpu.VMEM((seq, dv), F32)],
        compiler_params=_cparams("arbitrary", "arbitrary"),
        name="hgrn2",
    )(q, f_fw, f_bw, iv, g, o_lb, state0, g_norm.reshape(1, dv))

EVEN_SPLITS = ((0, 512), (512, 640), (640, 768), (768, 1280), (1280, 2048), (2048, 2176))
HGRN_SPLITS = tuple((i * 1024, (i + 1) * 1024) for i in range(5))


def _even_weight(w):
    main = EVEN_SPLITS[-1][0]
    return jnp.pad(w, ((0, 0), (0, LANES - (w.shape[1] - main)))).astype(BF16)


def _run_trunk(x3, mods, mod_row0, P, cache, sc_order=None):
    nb, seq, d = x3.shape
    x = x3.reshape(nb * seq, d)
    depth = P['norm_mix'].shape[0]
    ks, vs, ssd_states, hgrn_states = [], [], [], []
    for l in range(depth):
        j = l // 2
        row0 = (l * SUBLANES + mod_row0, 0 if cache is None else 1)
        if l % 2 == 0:
            q, k, v, z, xbc, dt = _inproj(x, mods, row0, seq, P['norm_mix'][l], P['e_w_in'][j],
                                          EVEN_SPLITS, "even_in_proj")
            q, k = _qkprep(q, k, P['e_q_norm'][j], P['e_k_norm'][j], seq, rope=cache is not None)
            if cache is None:
                s0_f = jnp.zeros((nb, SSD_HEADS, D_STATE, SSD_HEAD_DIM), F32)
                s0_b = s0_f
                o_attn = _ctx_attention(q, k, v, P['e_sink'][j], seq)
            else:
                s0_f, s0_b = cache[2][:, j, 0], cache[2][:, j, 1]
                n_ctx = cache[0].shape[2]
                o_attn = _lat_attention(q, k, v, cache[0][:, j].reshape(nb, n_ctx, -1),
                                        cache[1][:, j].reshape(nb, n_ctx, -1), P['e_sink'][j], seq)
            y, s_f, s_b = _ssd(xbc, dt, z, s0_f, s0_b, P['e_conv_w'][j], P['e_conv_b'][j],
                               P['e_dt_bias'][j], P['e_a_log'][j], P['e_d_skip'][j], P['e_ssd_norm'][j], seq)
            if cache is None:
                ks.append(k.reshape(nb, seq, N_KV_HEADS, HEAD_DIM))
                vs.append(v.reshape(nb, seq, N_KV_HEADS, HEAD_DIM))
                ssd_states.append(jnp.stack([s_f, s_b], axis=1))
            mix = jnp.concatenate([o_attn, y], axis=1)
            x = _outproj(mix, x, mods, row0, seq, P['e_w_out'][j], "even_out_proj")
        else:
            q, f_fw, f_bw, iv, g = _inproj(x, mods, row0, seq, P['norm_mix'][l], P['o_w_in'][j],
                                           HGRN_SPLITS, "odd_in_proj")
            if cache is None:
                s0 = jnp.zeros((nb, 2, HGRN_HEADS, HGRN_DK, d // HGRN_HEADS), F32)
            else:
                s0 = cache[3][:, j]
            o, s_new = _hgrn(q, f_fw, f_bw, iv, g, P['o_lb'], s0, P['o_g_norm'][j], seq, j)
            if cache is None:
                hgrn_states.append(s_new)
            x = _outproj(o, x, mods, row0, seq, P['o_w_out'][j], "odd_out_proj")
        x, y_sc = _peer(x, mods, row0, seq, l, P['norm_ffn'][l], P['p_w_q'][l], P['p_sub_keys'][l],
                        P['p_u'], P['p_v'],
                        (SC_SHARE_CONTEXT if cache is None else SC_SHARE_LATENT)[l],
                        after=None if sc_order is None or cache is None else sc_order[l])
        if sc_order is not None and cache is None:
            sc_order.append(y_sc)
    y = x.reshape(nb, seq, d)
    if cache is not None:
        return y, None
    return y, (jnp.stack(ks, axis=1), jnp.stack(vs, axis=1),
               jnp.stack(ssd_states, axis=1), jnp.stack(hgrn_states, axis=1))


def kernel(x_prompt, x_sample, cache_k, cache_v, state_ssd, state_hgrn, c, c_ctx, w_ada, b_ada, norm_mix, norm_ffn, e_w_in, e_q_norm, e_k_norm, e_sink, e_conv_w, e_conv_b, e_dt_bias, e_a_log, e_d_skip, e_ssd_norm, e_w_out, o_w_in, o_lb, o_g_norm, o_w_out, p_w_q, p_sub_keys, p_u, p_v):
    depth, d, d6 = w_ada.shape
    b_lat = x_sample.shape[0]
    cond_rows = jnp.concatenate([c_ctx[None, :], c, jnp.zeros((SUBLANES - 1 - b_lat, d), F32)], axis=0)
    mods = _modulation(cond_rows, w_ada, b_ada).reshape(depth * SUBLANES, 1, d6)
    P = {
        'norm_mix': norm_mix, 'norm_ffn': norm_ffn,
        'e_w_in': jnp.stack([_even_weight(w) for w in e_w_in]), 'e_q_norm': e_q_norm, 'e_k_norm': e_k_norm,
        'e_sink': e_sink, 'e_conv_w': e_conv_w, 'e_conv_b': e_conv_b, 'e_dt_bias': e_dt_bias,
        'e_a_log': e_a_log, 'e_d_skip': e_d_skip, 'e_ssd_norm': e_ssd_norm,
        'e_w_out': e_w_out.astype(BF16),
        'o_w_in': o_w_in.astype(BF16), 'o_lb': o_lb, 'o_g_norm': o_g_norm, 'o_w_out': o_w_out.astype(BF16),
        'p_w_q': p_w_q.astype(BF16),
        'p_sub_keys': p_sub_keys.astype(BF16).reshape(depth, PEER_HEADS * 2, PEER_NKEYS, PEER_DKEY),
        'p_u': p_u, 'p_v': p_v,
    }
    sc_order = []
    y_prompt, new_state = _run_trunk(x_prompt, mods, 0, P, None, sc_order)
    y_sample, _ = _run_trunk(x_sample, mods, 1, P, (cache_k, cache_v, state_ssd, state_hgrn), sc_order)
    return (y_prompt, y_sample) + new_state
```

```python
import functools
import math

import jax
import jax.numpy as jnp
from jax import lax
from jax.experimental import pallas as pl
from jax.experimental.pallas import tpu as pltpu
from jax.experimental.pallas import tpu_sc as plsc

F32 = jnp.float32
BF16 = jnp.bfloat16
I32 = jnp.int32
HIGHEST = lax.Precision.HIGHEST

NORM_EPS = 1e-6
NEG_BIG = -1e30
LANES = 128
SUBLANES = 8
VMEM_LIMIT = 48 * 1024 * 1024

GRID_W = 64
HEAD_DIM = 64
N_Q_HEADS = 8
N_KV_HEADS = 2
GQA_GROUP = 4
WINDOW = 128
ROPE_THETA = 10000.0
SSD_HEADS = 8
SSD_HEAD_DIM = 64
SSD_GROUPS = 2
D_STATE = 64
CONV_K = 5
HGRN_HEADS = 8
HGRN_DK = 128
HGRN_CHUNK = 32
PEER_HEADS = 8
PEER_NKEYS = 128
PEER_TOPK = 16
PEER_DKEY = 128
PEER_PAIRS = PEER_HEADS * PEER_TOPK

ROW_TILE = 256
PEER_BLOCK = 128
SC_LANES = 16
SC_CHUNK = 16
SC_SHARE_CONTEXT = ((1, 1), (1, 1), (1, 1), (1, 1))
SC_SHARE_LATENT = ((13, 16), (13, 16), (13, 16), (3, 4))
GELU_C = math.sqrt(2.0 / math.pi)


def _cparams(*sem):
    return pltpu.CompilerParams(dimension_semantics=sem, vmem_limit_bytes=VMEM_LIMIT)


def _norm_mod(x, nw, scale, shift):
    ms = jnp.mean(x * x, axis=-1, keepdims=True)
    return (x * lax.rsqrt(ms + NORM_EPS)) * nw * (1.0 + scale) + shift


def _mod_kernel(c_ref, w_ref, b_ref, o_ref):
    c = c_ref[...]
    s = c * jax.nn.sigmoid(c)
    o_ref[0] = jnp.dot(s, w_ref[0], precision=HIGHEST, preferred_element_type=F32) + b_ref[0]


def _modulation(cond_rows, w_ada, b_ada):
    depth, d, n = w_ada.shape
    rows = cond_rows.shape[0]
    return pl.pallas_call(
        _mod_kernel,
        grid=(depth, n // d),
        in_specs=[pl.BlockSpec((rows, d), lambda l, j: (0, 0)),
                  pl.BlockSpec((1, d, d), lambda l, j: (l, 0, j)),
                  pl.BlockSpec((1, 1, d), lambda l, j: (l, 0, j))],
        out_specs=pl.BlockSpec((1, rows, d), lambda l, j: (l, 0, j)),
        out_shape=jax.ShapeDtypeStruct((depth, rows, n), F32),
        compiler_params=_cparams("arbitrary", "arbitrary"),
        name="modulation",
    )(cond_rows, w_ada, b_ada.reshape(depth, 1, n))


def _mod_spec(mod_row0, seq, tile, d6, first_tile=0):
    row0, per_batch = mod_row0
    return pl.BlockSpec((1, 1, d6),
                        lambda i: (row0 + per_batch * (((i + first_tile) * tile) // seq), 0, 0))


def _inproj_kernel(x_ref, m_ref, nw_ref, w_ref, *o_refs, splits, d):
    m = m_ref[0]
    h = _norm_mod(x_ref[...], nw_ref[...], m[:, d:2 * d], m[:, 0:d]).astype(BF16)
    for o_ref, (a, b) in zip(o_refs, splits):
        o_ref[...] = jnp.dot(h, w_ref[:, a:b], preferred_element_type=F32)


def _inproj(x, mods, mod_row0, seq, nw, w_bf16, splits, name):
    m, d = x.shape
    n = w_bf16.shape[1]
    tile = min(ROW_TILE, seq)
    return pl.pallas_call(
        functools.partial(_inproj_kernel, splits=splits, d=d),
        grid=(m // tile,),
        in_specs=[pl.BlockSpec((tile, d), lambda i: (i, 0)),
                  _mod_spec(mod_row0, seq, tile, mods.shape[-1]),
                  pl.BlockSpec((1, d), lambda i: (0, 0)),
                  pl.BlockSpec((d, n), lambda i: (0, 0))],
        out_specs=[pl.BlockSpec((tile, b - a), lambda i: (i, 0)) for a, b in splits],
        out_shape=[jax.ShapeDtypeStruct((m, b - a), F32) for a, b in splits],
        compiler_params=_cparams("arbitrary"),
        name=name,
    )(x, mods, nw.reshape(1, d), w_bf16)


def _outproj_kernel(mix_ref, x_ref, m_ref, w_ref, o_ref, *, d):
    y = jnp.dot(mix_ref[...].astype(BF16), w_ref[...], preferred_element_type=F32)
    o_ref[...] = x_ref[...] + m_ref[0][:, 2 * d:3 * d] * y


def _outproj(mix, x, mods, mod_row0, seq, w_bf16, name):
    m, d = x.shape
    k = mix.shape[1]
    tile = min(ROW_TILE, seq)
    return pl.pallas_call(
        functools.partial(_outproj_kernel, d=d),
        grid=(m // tile,),
        in_specs=[pl.BlockSpec((tile, k), lambda i: (i, 0)),
                  pl.BlockSpec((tile, d), lambda i: (i, 0)),
                  _mod_spec(mod_row0, seq, tile, mods.shape[-1]),
                  pl.BlockSpec((k, d), lambda i: (0, 0))],
        out_specs=pl.BlockSpec((tile, d), lambda i: (i, 0)),
        out_shape=jax.ShapeDtypeStruct((m, d), F32),
        compiler_params=_cparams("arbitrary"),
        name=name,
    )(mix, x, mods, w_bf16)


def _topk_over_rows(s, k, payload=None):
    n = s.shape[0]
    iota = lax.broadcasted_iota(I32, s.shape, 0)
    vals, idxs, pays = [], [], []
    for _ in range(k):
        m = jnp.max(s, axis=0, keepdims=True)
        i = jnp.min(jnp.where(s == m, iota, n), axis=0, keepdims=True)
        hit = iota == i
        vals.append(m)
        idxs.append(i)
        if payload is not None:
            pays.append(jnp.max(jnp.where(hit, payload, -1), axis=0, keepdims=True))
        s = jnp.where(hit, -jnp.inf, s)
    out = (jnp.concatenate(vals, axis=0), jnp.concatenate(idxs, axis=0))
    if payload is not None:
        out += (jnp.concatenate(pays, axis=0),)
    return out


def _peer_route_kernel(x_ref, m_ref, nw_ref, wq_ref, keys_ref, h_ref, e_ref, g_ref, *, d):
    m = m_ref[0]
    h = _norm_mod(x_ref[...], nw_ref[...], m[:, 4 * d:5 * d], m[:, 3 * d:4 * d])
    h_ref[...] = h
    hb = h.astype(BF16)
    nt = (((1,), (1,)), ((), ()))
    for head in range(PEER_HEADS):
        tops = []
        for half in range(2):
            c0 = (head * 2 + half) * PEER_DKEY
            q = jnp.dot(hb, wq_ref[:, c0:c0 + PEER_DKEY], preferred_element_type=F32)
            s = lax.dot_general(keys_ref[head * 2 + half], q.astype(BF16), nt,
                                preferred_element_type=F32)
            tops.append(_topk_over_rows(s, PEER_TOPK))
        (s0, i0), (s1, i1) = tops
        widths = [PEER_TOPK // (a + 1) for a in range(PEER_TOPK)]
        n_pad = -sum(widths) % SUBLANES
        cand_s = jnp.concatenate([s0[a:a + 1] + s1[:w] for a, w in enumerate(widths)]
                                 + [jnp.full((n_pad, s0.shape[1]), -jnp.inf, F32)], axis=0)
        cand_e = jnp.concatenate([i0[a:a + 1] * PEER_NKEYS + i1[:w] for a, w in enumerate(widths)]
                                 + [jnp.zeros((n_pad, s0.shape[1]), I32)], axis=0)
        best_s, _, best_e = _topk_over_rows(cand_s, PEER_TOPK, payload=cand_e)
        p = jnp.exp(best_s - best_s[0:1])
        r0 = head * PEER_TOPK
        e_ref[r0:r0 + PEER_TOPK, :] = best_e
        g_ref[r0:r0 + PEER_TOPK, :] = p / jnp.sum(p, axis=0, keepdims=True)


def _peer_route(x, mods, mod_row0, seq, nw, wq_bf16, keys_bf16):
    m, d = x.shape
    tile = min(ROW_TILE, seq)
    nq = wq_bf16.shape[1]
    return pl.pallas_call(
        functools.partial(_peer_route_kernel, d=d),
        grid=(m // tile,),
        in_specs=[pl.BlockSpec((tile, d), lambda i: (i, 0)),
                  _mod_spec(mod_row0, seq, tile, mods.shape[-1]),
                  pl.BlockSpec((1, d), lambda i: (0, 0)),
                  pl.BlockSpec((d, nq), lambda i: (0, 0)),
                  pl.BlockSpec(keys_bf16.shape, lambda i: (0, 0, 0))],
        out_specs=[pl.BlockSpec((tile, d), lambda i: (i, 0)),
                   pl.BlockSpec((PEER_PAIRS, tile), lambda i: (0, i)),
                   pl.BlockSpec((PEER_PAIRS, tile), lambda i: (0, i))],
        out_shape=[jax.ShapeDtypeStruct((m, d), F32),
                   jax.ShapeDtypeStruct((PEER_PAIRS, m), I32),
                   jax.ShapeDtypeStruct((PEER_PAIRS, m), F32)],
        compiler_params=_cparams("arbitrary"),
        name="peer_route",
    )(x, mods, nw.reshape(1, d), wq_bf16, keys_bf16)


def _peer_gather_kernel(idx_hbm, h_ref, g_ref, x_ref, m_ref, u_hbm, v_hbm, o_ref,
                        idx_smem, ubuf, vbuf, sem_idx, sem_u, sem_v, *, layer, d):
    blk = pl.program_id(0)
    n_groups = PEER_BLOCK // SUBLANES

    cp = pltpu.make_async_copy(idx_hbm.at[blk], idx_smem, sem_idx)
    cp.start()
    cp.wait()

    def issue(tok, slot):
        def body(pair, carry):
            e = idx_smem[tok, pair]
            pltpu.make_async_copy(u_hbm.at[layer, pl.ds(e, 1)], ubuf.at[slot, pl.ds(pair, 1)],
                                  sem_u.at[slot]).start()
            pltpu.make_async_copy(v_hbm.at[layer, pl.ds(e, 1)], vbuf.at[slot, pl.ds(pair, 1)],
                                  sem_v.at[slot]).start()
            return carry
        lax.fori_loop(0, PEER_PAIRS, body, 0, unroll=8)

    def wait(slot):
        pltpu.make_async_copy(u_hbm.at[layer, pl.ds(0, PEER_PAIRS)], ubuf.at[slot], sem_u.at[slot]).wait()
        pltpu.make_async_copy(v_hbm.at[layer, pl.ds(0, PEER_PAIRS)], vbuf.at[slot], sem_v.at[slot]).wait()

    gate2 = m_ref[0][:, 5 * d:6 * d]
    lane = lax.broadcasted_iota(I32, (PEER_PAIRS, PEER_BLOCK), 1)

    issue(0, 0)

    def group(grp, carry):
        base = pl.multiple_of(grp * SUBLANES, SUBLANES)
        h8 = h_ref[pl.ds(base, SUBLANES), :]
        rows = []
        for r in range(SUBLANES):
            tok = base + r
            slot = r % 2
            if r < SUBLANES - 1:
                issue(tok + 1, 1 - slot)
            else:
                @pl.when(grp < n_groups - 1)
                def _():
                    issue(tok + 1, 1 - slot)
            wait(slot)
            act = jnp.sum(ubuf[slot] * h8[r:r + 1, :], axis=1, keepdims=True)
            gate = jnp.sum(jnp.where(lane == tok, g_ref[...], 0.0), axis=1, keepdims=True)
            w = jax.nn.gelu(act) * gate
            rows.append(jnp.sum(vbuf[slot] * w, axis=0, keepdims=True))
        out8 = jnp.concatenate(rows, axis=0)
        o_ref[pl.ds(base, SUBLANES), :] = x_ref[pl.ds(base, SUBLANES), :] + gate2 * out8
        return carry

    lax.fori_loop(0, n_groups, group, 0)


def _peer_gather(idx, h, gates, x, mods, mod_row0, seq, p_u, p_v, layer, first_block):
    m, d = x.shape
    nblk = m // PEER_BLOCK - first_block
    idx3 = idx[:, first_block * PEER_BLOCK:].T.reshape(nblk, PEER_BLOCK, PEER_PAIRS)
    rows = pl.BlockSpec((PEER_BLOCK, d), lambda i: (i + first_block, 0))
    return pl.pallas_call(
        functools.partial(_peer_gather_kernel, layer=layer, d=d),
        grid=(nblk,),
        in_specs=[pl.BlockSpec(memory_space=pl.ANY),
                  rows,
                  pl.BlockSpec((PEER_PAIRS, PEER_BLOCK), lambda i: (0, i + first_block)),
                  rows,
                  _mod_spec(mod_row0, seq, PEER_BLOCK, mods.shape[-1], first_block),
                  pl.BlockSpec(memory_space=pl.ANY),
                  pl.BlockSpec(memory_space=pl.ANY)],
        out_specs=pl.BlockSpec((PEER_BLOCK, d), lambda i: (i, 0)),
        out_shape=jax.ShapeDtypeStruct((nblk * PEER_BLOCK, d), F32),
        scratch_shapes=[pltpu.SMEM((PEER_BLOCK, PEER_PAIRS), I32),
                        pltpu.VMEM((2, PEER_PAIRS, d), F32),
                        pltpu.VMEM((2, PEER_PAIRS, d), F32),
                        pltpu.SemaphoreType.DMA,
                        pltpu.SemaphoreType.DMA((2,)),
                        pltpu.SemaphoreType.DMA((2,))],
        compiler_params=_cparams("arbitrary"),
        name="peer_gather",
    )(idx3, h, gates, x, mods, p_u, p_v)


def _peer_experts_sc(idx, h, gates, u_rows, v_rows):
    m = idx.shape[0]
    d = h.shape[1]
    info = plsc.get_sparse_core_info()
    n_workers = info.num_cores * info.num_subcores
    per = m // n_workers
    n_chunks = PEER_PAIRS // SC_CHUNK
    n_vec = d // SC_LANES
    mesh = plsc.VectorSubcoreMesh(core_axis_name="c", subcore_axis_name="s")

    @functools.partial(
        pl.kernel, out_type=jax.ShapeDtypeStruct((m, d), F32), mesh=mesh,
        scratch_types=[pltpu.VMEM((2, PEER_PAIRS), I32), pltpu.VMEM((2, d), F32),
                       pltpu.VMEM((2, PEER_PAIRS), F32), pltpu.VMEM((2, d), F32),
                       pltpu.VMEM((2, SC_CHUNK, d), F32), pltpu.VMEM((2, SC_CHUNK, d), F32),
                       pltpu.SemaphoreType.DMA((2,)), pltpu.SemaphoreType.DMA((2,)),
                       pltpu.SemaphoreType.DMA((2,)), pltpu.SemaphoreType.DMA((2,))],
        compiler_params=pltpu.CompilerParams(needs_layout_passes=False),
        name="peer_experts_sc")
    def body(idx_hbm, h_hbm, g_hbm, u_hbm, v_hbm, o_hbm,
             idx_v, x_v, g_v, out_v, ubuf, vbuf, sem_meta, sem_out, sem_u, sem_v):
        wid = lax.axis_index("c") * info.num_subcores + lax.axis_index("s")
        tok0 = wid * per
        lane = lax.iota(I32, SC_LANES)

        def meta_copies(ti, ms):
            t = tok0 + ti
            return (pltpu.make_async_copy(idx_hbm.at[t], idx_v.at[ms], sem_meta.at[ms]),
                    pltpu.make_async_copy(h_hbm.at[t], x_v.at[ms], sem_meta.at[ms]),
                    pltpu.make_async_copy(g_hbm.at[t], g_v.at[ms], sem_meta.at[ms]))

        def gather_copies(ms, c, slot):
            ids = idx_v.at[ms, pl.ds(c * SC_CHUNK, SC_CHUNK)]
            return (pltpu.make_async_copy(u_hbm.at[ids], ubuf.at[slot], sem_u.at[slot]),
                    pltpu.make_async_copy(v_hbm.at[ids], vbuf.at[slot], sem_v.at[slot]))

        def out_copy(ti, ms):
            return pltpu.make_async_copy(out_v.at[ms], o_hbm.at[tok0 + ti], sem_out.at[ms])

        for cp in meta_copies(0, 0):
            cp.start()
        for cp in meta_copies(0, 0):
            cp.wait()
        for cp in gather_copies(0, 0, 0):
            cp.start()

        def token(ti, carry):
            ms = ti % 2
            nxt = 1 - ms

            @pl.when(ti + 1 < per)
            def _():
                for cp in meta_copies(ti + 1, nxt):
                    cp.start()

            @pl.when(ti >= 2)
            def _():
                out_copy(ti - 2, ms).wait()

            def zero(j, c):
                out_v[ms, pl.ds(j * SC_LANES, SC_LANES)] = jnp.zeros((SC_LANES,), F32)
                return c
            lax.fori_loop(0, n_vec, zero, 0)

            for c in range(n_chunks):
                slot = c % 2
                if c + 1 < n_chunks:
                    for cp in gather_copies(ms, c + 1, 1 - slot):
                        cp.start()
                else:
                    @pl.when(ti + 1 < per)
                    def _():
                        for cp in meta_copies(ti + 1, nxt):
                            cp.wait()
                        for cp in gather_copies(nxt, 0, 1 - slot):
                            cp.start()
                cu, cv = gather_copies(ms, c, slot)
                cu.wait()

                def udot(j, accs):
                    xj = x_v[ms, pl.ds(j * SC_LANES, SC_LANES)]
                    return tuple(accs[r] + ubuf[slot, r, pl.ds(j * SC_LANES, SC_LANES)] * xj
                                 for r in range(SC_CHUNK))
                accs = lax.fori_loop(0, n_vec, udot,
                                     tuple(jnp.zeros((SC_LANES,), F32) for _ in range(SC_CHUNK)))
                act = jnp.zeros((SC_LANES,), F32)
                for r in range(SC_CHUNK):
                    act = jnp.where(lane == r, jnp.sum(accs[r]), act)
                y = GELU_C * (act + 0.044715 * (act * act * act))
                w = act / (1.0 + jnp.exp(-2.0 * y)) * g_v[ms, pl.ds(c * SC_CHUNK, SC_CHUNK)]
                ws = [jnp.sum(jnp.where(lane == r, w, 0.0)) for r in range(SC_CHUNK)]
                cv.wait()

                @plsc.parallel_loop(0, n_vec, unroll=2)
                def _(j):
                    parts = [ws[r] * vbuf[slot, r, pl.ds(j * SC_LANES, SC_LANES)] for r in range(SC_CHUNK)]
                    while len(parts) > 1:
                        parts = [parts[i] + parts[i + 1] for i in range(0, len(parts), 2)]
                    plsc.addupdate(out_v.at[ms, pl.ds(j * SC_LANES, SC_LANES)], parts[0])

            out_copy(ti, ms).start()
            return carry

        lax.fori_loop(0, per, token, 0)
        for back in (2, 1):
            if per >= back:
                out_copy(per - back, (per - back) % 2).wait()

    return body(idx, h, gates, u_rows, v_rows)


def _residual_kernel(x_ref, y_ref, m_ref, o_ref, *, d):
    o_ref[...] = x_ref[...] + m_ref[0][:, 5 * d:6 * d] * y_ref[...]


def _residual(x, y, mods, mod_row0, seq):
    m, d = y.shape
    tile = min(ROW_TILE, seq)
    return pl.pallas_call(
        functools.partial(_residual_kernel, d=d),
        grid=(m // tile,),
        in_specs=[pl.BlockSpec((tile, d), lambda i: (i, 0)),
                  pl.BlockSpec((tile, d), lambda i: (i, 0)),
                  _mod_spec(mod_row0, seq, tile, mods.shape[-1])],
        out_specs=pl.BlockSpec((tile, d), lambda i: (i, 0)),
        out_shape=jax.ShapeDtypeStruct((m, d), F32),
        compiler_params=_cparams("arbitrary"),
        name="peer_residual",
    )(x, y, mods)


def _peer(x, mods, mod_row0, seq, layer, nw, wq_bf16, keys_bf16, p_u, p_v, sc_share, after=None):
    m, d = x.shape
    h, idx, gates = _peer_route(x, mods, mod_row0, seq, nw, wq_bf16, keys_bf16)
    n_blocks = m // PEER_BLOCK
    sc_blocks = (n_blocks * sc_share[0]) // sc_share[1]
    m_sc = sc_blocks * PEER_BLOCK
    n_experts = p_u.shape[1]
    idx_sc = idx[:, :m_sc].T + layer * n_experts
    if after is not None:
        idx_sc, _ = lax.optimization_barrier((idx_sc, after))
    y_sc = _peer_experts_sc(idx_sc, h, gates[:, :m_sc].T,
                            p_u.reshape(-1, d), p_v.reshape(-1, d))
    x_sc = _residual(x, y_sc, mods, mod_row0, seq)
    if sc_blocks == n_blocks:
        return x_sc, y_sc
    x_tc = _peer_gather(idx, h, gates, x, mods, mod_row0, seq, p_u, p_v, layer, sc_blocks)
    return jnp.concatenate([x_sc, x_tc], axis=0), y_sc


def _head_mean_square(x):
    n = x.shape[1]
    r = lax.broadcasted_iota(I32, (n, n), 0) // HEAD_DIM
    c = lax.broadcasted_iota(I32, (n, n), 1) // HEAD_DIM
    seg = jnp.where(r == c, 1.0 / HEAD_DIM, 0.0).astype(F32)
    return jnp.dot(x * x, seg, precision=HIGHEST, preferred_element_type=F32)


def _swap_rot_halves(x):
    n = x.shape[1]
    quarter = HEAD_DIM // 4
    lane = lax.broadcasted_iota(I32, x.shape, 1)
    lo = (lane % (2 * quarter)) < quarter
    return jnp.where(lo, pltpu.roll(x, n - quarter, axis=1), pltpu.roll(x, quarter, axis=1))


def _qkprep_kernel(q_ref, k_ref, qw_ref, kw_ref, *rest, rope):
    if rope:
        cos_ref, sin_ref, qo_ref, ko_ref = rest
    else:
        qo_ref, ko_ref = rest
    q = q_ref[...]
    k = k_ref[...]
    q = q * lax.rsqrt(_head_mean_square(q) + NORM_EPS) * qw_ref[...]
    k = k * lax.rsqrt(_head_mean_square(k) + NORM_EPS) * kw_ref[...]
    if rope:
        cos = cos_ref[...]
        sin = sin_ref[...]
        cq = jnp.concatenate([cos] * (q.shape[1] // LANES), axis=1)
        sq = jnp.concatenate([sin] * (q.shape[1] // LANES), axis=1)
        q = q * cq + _swap_rot_halves(q) * sq
        k = k * cos + _swap_rot_halves(k) * sin
    qo_ref[...] = q
    ko_ref[...] = k


def _rope_tables(seq):
    axis_dim = HEAD_DIM // 2
    inv_freq = ROPE_THETA ** (-jnp.arange(0, axis_dim, 2, dtype=F32) / axis_dim)
    t = jnp.arange(seq)
    pos = jnp.stack([(t // GRID_W).astype(F32), (t % GRID_W).astype(F32)], axis=1)
    lane = jnp.arange(LANES)
    dd = lane % HEAD_DIM
    ang = pos[:, dd // axis_dim] * inv_freq[dd % (axis_dim // 2)][None, :]
    sign = jnp.where((dd % axis_dim) < axis_dim // 2, -1.0, 1.0).astype(F32)
    return jnp.cos(ang), jnp.sin(ang) * sign[None, :]


def _qkprep(q, k, qw, kw, seq, rope):
    m, nq = q.shape
    nk = k.shape[1]
    tile = min(ROW_TILE, seq)
    qw_row = jnp.tile(qw, nq // HEAD_DIM).reshape(1, nq)
    kw_row = jnp.tile(kw, nk // HEAD_DIM).reshape(1, nk)
    in_specs = [pl.BlockSpec((tile, nq), lambda i: (i, 0)),
                pl.BlockSpec((tile, nk), lambda i: (i, 0)),
                pl.BlockSpec((1, nq), lambda i: (0, 0)),
                pl.BlockSpec((1, nk), lambda i: (0, 0))]
    args = [q, k, qw_row, kw_row]
    if rope:
        cos, sin = _rope_tables(seq)
        per_seq = seq // tile
        in_specs += [pl.BlockSpec((tile, LANES), lambda i: (i % per_seq, 0)),
                     pl.BlockSpec((tile, LANES), lambda i: (i % per_seq, 0))]
        args += [cos, sin]
    return pl.pallas_call(
        functools.partial(_qkprep_kernel, rope=rope),
        grid=(m // tile,),
        in_specs=in_specs,
        out_specs=[pl.BlockSpec((tile, nq), lambda i: (i, 0)),
                   pl.BlockSpec((tile, nk), lambda i: (i, 0))],
        out_shape=[jax.ShapeDtypeStruct((m, nq), F32), jax.ShapeDtypeStruct((m, nk), F32)],
        compiler_params=_cparams("arbitrary"),
        name="qk_prep",
    )(*args)


def _dup_halves(x):
    lane = lax.broadcasted_iota(I32, x.shape, 1)
    sw = pltpu.roll(x, HEAD_DIM, axis=1)
    lo = lane < HEAD_DIM
    return jnp.where(lo, x, sw), jnp.where(lo, sw, x)


def _attend(q, k_all, v_all, sink_ref, mask):
    scale = HEAD_DIM ** -0.5
    nt = (((1,), (1,)), ((), ()))
    kk = [a.astype(BF16) for a in _dup_halves(k_all)]
    vv = [a.astype(BF16) for a in _dup_halves(v_all)]
    lane = lax.broadcasted_iota(I32, (q.shape[0], LANES), 1)
    lo = lane < HEAD_DIM
    tiles = []
    for t in range(q.shape[1] // LANES):
        qt = q[:, t * LANES:(t + 1) * LANES]
        g = (2 * t) // GQA_GROUP
        halves = []
        for hh in range(2):
            head = 2 * t + hh
            qm = jnp.where(lo if hh == 0 else ~lo, qt, 0.0).astype(BF16)
            s = lax.dot_general(qm, kk[g], nt, preferred_element_type=F32) * scale
            if mask is not None:
                s = jnp.where(mask, s, NEG_BIG)
            sink = sink_ref[head]
            mx = jnp.maximum(jnp.max(s, axis=1, keepdims=True), sink)
            p = jnp.exp(s - mx)
            den = jnp.sum(p, axis=1, keepdims=True) + jnp.exp(sink - mx)
            p = (p / den).astype(BF16)
            halves.append(jnp.dot(p, vv[g], preferred_element_type=F32))
        tiles.append(jnp.where(lo, halves[0], halves[1]))
    return jnp.concatenate(tiles, axis=1)


def _ctx_attn_kernel(sink_ref, q_ref, k_ref, v_ref, o_ref):
    o_ref[...] = _attend(q_ref[...], k_ref[...], v_ref[...], sink_ref, None)


def _ctx_attention(q, k, v, sink, seq):
    m, nq = q.shape
    nk = k.shape[1]
    return pl.pallas_call(
        _ctx_attn_kernel,
        grid=(m // seq,),
        in_specs=[pl.BlockSpec(memory_space=pltpu.SMEM),
                  pl.BlockSpec((seq, nq), lambda b: (b, 0)),
                  pl.BlockSpec((seq, nk), lambda b: (b, 0)),
                  pl.BlockSpec((seq, nk), lambda b: (b, 0))],
        out_specs=pl.BlockSpec((seq, nq), lambda b: (b, 0)),
        out_shape=jax.ShapeDtypeStruct((m, nq), F32),
        compiler_params=_cparams("arbitrary"),
        name="ctx_attention",
    )(sink, q, k, v)


def _lat_attn_kernel(sink_ref, q_ref, kc_ref, vc_ref, kp_ref, k0_ref, kn_ref, vp_ref, v0_ref, vn_ref,
                     o_ref, *, seq):
    qb = pl.program_id(1)
    blk = q_ref.shape[0]
    n_ctx = kc_ref.shape[1]
    k_all = jnp.concatenate([kc_ref[0], kp_ref[...], k0_ref[...], kn_ref[...]], axis=0)
    v_all = jnp.concatenate([vc_ref[0], vp_ref[...], v0_ref[...], vn_ref[...]], axis=0)
    tk = k_all.shape[0]
    qpos = qb * blk + lax.broadcasted_iota(I32, (blk, tk), 0)
    col = lax.broadcasted_iota(I32, (blk, tk), 1)
    kpos = (qb - 1) * blk + col - n_ctx
    local_ok = (jnp.abs(qpos - kpos) <= WINDOW) & (kpos >= 0) & (kpos < seq)
    mask = (col < n_ctx) | local_ok
    o_ref[...] = _attend(q_ref[...], k_all, v_all, sink_ref, mask)


def _lat_attention(q, k, v, k_ctx, v_ctx, sink, seq):
    m, nq = q.shape
    nk = k.shape[1]
    blk = WINDOW
    nb = seq // blk
    n_ctx = k_ctx.shape[1]
    last = m // blk - 1

    def kv_spec(shift):
        return pl.BlockSpec((blk, nk), lambda b, i: (jnp.clip(b * nb + i + shift, 0, last), 0))

    ctx_spec = pl.BlockSpec((1, n_ctx, nk), lambda b, i: (b, 0, 0))
    return pl.pallas_call(
        functools.partial(_lat_attn_kernel, seq=seq),
        grid=(m // seq, nb),
        in_specs=[pl.BlockSpec(memory_space=pltpu.SMEM),
                  pl.BlockSpec((blk, nq), lambda b, i: (b * nb + i, 0)),
                  ctx_spec, ctx_spec,
                  kv_spec(-1), kv_spec(0), kv_spec(1),
                  kv_spec(-1), kv_spec(0), kv_spec(1)],
        out_specs=pl.BlockSpec((blk, nq), lambda b, i: (b * nb + i, 0)),
        out_shape=jax.ShapeDtypeStruct((m, nq), F32),
        compiler_params=_cparams("arbitrary", "arbitrary"),
        name="lat_attention",
    )(sink, q, k_ctx, v_ctx, k, k, k, v, v, v)

SSD_BLOCK = 256
SSD_PAIRS = SSD_HEADS // 2
SSD_INNER = SSD_HEADS * SSD_HEAD_DIM
HALO = SUBLANES


def _softplus(x):
    return jnp.maximum(x, 0.0) + jnp.log1p(jnp.exp(-jnp.abs(x)))


def _silu(x):
    return x * jax.nn.sigmoid(x)


def _ssd_decays(dt_raw, bias, a_log):
    n = dt_raw.shape[0]
    dt = _softplus(dt_raw + bias)
    log_a = dt * (-jnp.exp(a_log))
    r = lax.broadcasted_iota(I32, (n, n), 0)
    c = lax.broadcasted_iota(I32, (n, n), 1)
    lower = jnp.where(c <= r, 1.0, 0.0).astype(F32)
    upper = jnp.where(r <= c, 1.0, 0.0).astype(F32)
    cum_col = jnp.dot(lower, log_a, precision=HIGHEST, preferred_element_type=F32)
    dt_row = dt.T
    la_row = log_a.T
    cum_row = jnp.dot(la_row, upper, precision=HIGHEST, preferred_element_type=F32)
    return dt, log_a, cum_col, dt_row, la_row, cum_row


def _ssd_scan_chunk(xs, bmat, cmat, w_of, q_scale_of, k_scale_of, carry_of, s_ref):
    nt = (((1,), (1,)), ((), ()))
    n = xs.shape[0]
    lane = lax.broadcasted_iota(I32, (n, LANES), 1)
    lo = lane < SSD_HEAD_DIM
    lane_s = lax.broadcasted_iota(I32, (D_STATE, LANES), 1)
    lo_s = lane_s < SSD_HEAD_DIM
    b_t = bmat.T
    cb16 = cmat.astype(BF16)
    ys = []
    for pair in range(SSD_PAIRS):
        g = (2 * pair) // (SSD_HEADS // SSD_GROUPS)
        in_g = (lane // D_STATE) == g
        cg = jnp.where(in_g, cmat, 0.0)
        cb = lax.dot_general(cg.astype(BF16), bmat.astype(BF16), nt, preferred_element_type=F32)
        x_pair = xs[:, pair * LANES:(pair + 1) * LANES]
        x16 = x_pair.astype(BF16)
        s_old = s_ref[pair]
        s2 = jnp.concatenate([s_old, s_old], axis=0).astype(BF16)
        bg_t = b_t[g * D_STATE:(g + 1) * D_STATE, :]
        y_h, s_h = [], []
        for hh in range(2):
            h = 2 * pair + hh
            w = (cb * w_of(h)).astype(BF16)
            y = jnp.dot(w, x16, preferred_element_type=F32)
            cq = (cg * q_scale_of(h)).astype(BF16)
            y = y + jnp.dot(cq, s2, preferred_element_type=F32)
            y_h.append(y)
            kt = (bg_t * k_scale_of(h)).astype(BF16)
            s_h.append(carry_of(h) * s_old + jnp.dot(kt, x16, preferred_element_type=F32))
        ys.append(jnp.where(lo, y_h[0], y_h[1]))
        s_ref[pair] = jnp.where(lo_s, s_h[0], s_h[1])
    return jnp.concatenate(ys, axis=1)


def _ssd_fwd_kernel(x_ref, xp_ref, xn_ref, dt_ref, s0_ref, cw_ref, cb_ref, bias_ref, alog_ref,
                    y_ref, xc_ref, sfin_ref, s_ref):
    c = pl.program_id(1)
    nc = pl.num_programs(1)
    n = x_ref.shape[0]

    @pl.when(c == 0)
    def _():
        s_ref[...] = s0_ref[0]

    prev = jnp.where(c > 0, xp_ref[...], 0.0)
    nxt = jnp.where(c < nc - 1, xn_ref[...], 0.0)
    xe = jnp.concatenate([prev, x_ref[...], nxt], axis=0)
    pad = (CONV_K - 1) // 2
    acc = cb_ref[...] + cw_ref[0:1, :] * xe[HALO - pad:HALO - pad + n, :]
    for k in range(1, CONV_K):
        acc = acc + cw_ref[k:k + 1, :] * xe[HALO - pad + k:HALO - pad + k + n, :]
    xc = _silu(acc)
    xc_ref[...] = xc
    xs = xc[:, :SSD_INNER]
    bmat = xc[:, SSD_INNER:SSD_INNER + LANES]
    cmat = xc[:, SSD_INNER + LANES:SSD_INNER + 2 * LANES]

    dt, log_a, cum_col, dt_row, la_row, cum_row = _ssd_decays(dt_ref[...], bias_ref[...], alog_ref[...])
    r = lax.broadcasted_iota(I32, (n, n), 0)
    cc = lax.broadcasted_iota(I32, (n, n), 1)
    causal = cc <= r
    last_col = cum_col[n - 1:n, :]

    def w_of(h):
        seg = cum_col[:, h:h + 1] - cum_row[h:h + 1, :]
        return jnp.exp(jnp.where(causal, seg, NEG_BIG)) * dt_row[h:h + 1, :]

    def q_scale_of(h):
        return jnp.exp(cum_col[:, h:h + 1])

    def k_scale_of(h):
        return dt_row[h:h + 1, :] * jnp.exp(cum_row[h:h + 1, n - 1:n] - cum_row[h:h + 1, :])

    def carry_of(h):
        return jnp.exp(last_col[:, h:h + 1])

    y_ref[...] = _ssd_scan_chunk(xs, bmat, cmat, w_of, q_scale_of, k_scale_of, carry_of, s_ref)

    @pl.when(c == nc - 1)
    def _():
        sfin_ref[0] = s_ref[...]


def _ssd_bwd_kernel(xc_ref, dt_ref, yf_ref, z_ref, s0_ref, bias_ref, alog_ref, dskip_ref, nw_ref,
                    y_ref, sfin_ref, s_ref):
    c = pl.program_id(1)
    nc = pl.num_programs(1)
    n = xc_ref.shape[0]

    @pl.when(c == 0)
    def _():
        s_ref[...] = s0_ref[0]

    xc = xc_ref[...]
    xs = xc[:, :SSD_INNER]
    bmat = xc[:, SSD_INNER:SSD_INNER + LANES]
    cmat = xc[:, SSD_INNER + LANES:SSD_INNER + 2 * LANES]
    dt, log_a, cum_col, dt_row, la_row, cum_row = _ssd_decays(dt_ref[...], bias_ref[...], alog_ref[...])
    ex_col = cum_col - log_a
    ex_row = cum_row - la_row
    r = lax.broadcasted_iota(I32, (n, n), 0)
    cc = lax.broadcasted_iota(I32, (n, n), 1)
    anti = cc >= r
    tot_col = cum_col[n - 1:n, :]
    off = SSD_HEADS

    def w_of(h):
        j = off + h
        seg = ex_row[j:j + 1, :] - ex_col[:, j:j + 1]
        return jnp.exp(jnp.where(anti, seg, NEG_BIG)) * dt_row[j:j + 1, :]

    def q_scale_of(h):
        j = off + h
        return jnp.exp(tot_col[:, j:j + 1] - ex_col[:, j:j + 1])

    def k_scale_of(h):
        j = off + h
        return dt_row[j:j + 1, :] * jnp.exp(ex_row[j:j + 1, :])

    def carry_of(h):
        j = off + h
        return jnp.exp(tot_col[:, j:j + 1])

    y_b = _ssd_scan_chunk(xs, bmat, cmat, w_of, q_scale_of, k_scale_of, carry_of, s_ref)
    y = yf_ref[...] + y_b + dskip_ref[...] * xs
    y = y * _silu(z_ref[...])
    ms = jnp.mean(y * y, axis=-1, keepdims=True)
    y_ref[...] = y * lax.rsqrt(ms + NORM_EPS) * nw_ref[...]

    @pl.when(c == nc - 1)
    def _():
        sfin_ref[0] = s_ref[...]


def _pair_states(s):
    b, h, n, p = s.shape
    return s.reshape(b, h // 2, 2, n, p).transpose(0, 1, 3, 2, 4).reshape(b, h // 2, n, 2 * p)


def _unpair_states(s):
    b, hp, n, p2 = s.shape
    return s.reshape(b, hp, n, 2, p2 // 2).transpose(0, 1, 3, 2, 4).reshape(b, hp * 2, n, p2 // 2)


def _ssd(xbc, dt, z, s0_f, s0_b, conv_w, conv_b, dt_bias, a_log, d_skip, ssd_norm, seq):
    m, nx = xbc.shape
    nb = m // seq
    blk = min(SSD_BLOCK, seq)
    nc = seq // blk
    hb = blk // HALO
    n_halo = m // HALO
    pad16 = lambda a: jnp.pad(a.reshape(1, -1), ((0, 0), (0, LANES - a.size)))
    bias = pad16(dt_bias)
    alog = pad16(a_log)
    state_spec = pl.BlockSpec((1, SSD_PAIRS, D_STATE, LANES), lambda b, c: (b, 0, 0, 0))
    state_shape = jax.ShapeDtypeStruct((nb, SSD_PAIRS, D_STATE, LANES), F32)
    row = lambda width: pl.BlockSpec((1, width), lambda b, c: (0, 0))

    def fwd_rows(width):
        return pl.BlockSpec((blk, width), lambda b, c: (b * nc + c, 0))

    def bwd_rows(width):
        return pl.BlockSpec((blk, width), lambda b, c: (b * nc + nc - 1 - c, 0))

    y_f, xc, s_f = pl.pallas_call(
        _ssd_fwd_kernel,
        grid=(nb, nc),
        in_specs=[fwd_rows(nx),
                  pl.BlockSpec((HALO, nx), lambda b, c: (jnp.maximum((b * nc + c) * hb - 1, 0), 0)),
                  pl.BlockSpec((HALO, nx), lambda b, c: (jnp.minimum((b * nc + c + 1) * hb, n_halo - 1), 0)),
                  fwd_rows(LANES), state_spec,
                  pl.BlockSpec((CONV_K, nx), lambda b, c: (0, 0)), row(nx), row(LANES), row(LANES)],
        out_specs=[fwd_rows(SSD_INNER), fwd_rows(nx), state_spec],
        out_shape=[jax.ShapeDtypeStruct((m, SSD_INNER), F32), jax.ShapeDtypeStruct((m, nx), F32), state_shape],
        scratch_shapes=[pltpu.VMEM((SSD_PAIRS, D_STATE, LANES), F32)],
        compiler_params=_cparams("arbitrary", "arbitrary"),
        name="ssd_forward",
    )(xbc, xbc, xbc, dt, _pair_states(s0_f), conv_w, conv_b.reshape(1, nx), bias, alog)

    dskip = jnp.repeat(d_skip, SSD_HEAD_DIM).reshape(1, SSD_INNER)
    y, s_b = pl.pallas_call(
        _ssd_bwd_kernel,
        grid=(nb, nc),
        in_specs=[bwd_rows(nx), bwd_rows(LANES), bwd_rows(SSD_INNER), bwd_rows(SSD_INNER), state_spec,
                  row(LANES), row(LANES), row(SSD_INNER), row(SSD_INNER)],
        out_specs=[bwd_rows(SSD_INNER), state_spec],
        out_shape=[jax.ShapeDtypeStruct((m, SSD_INNER), F32), state_shape],
        scratch_shapes=[pltpu.VMEM((SSD_PAIRS, D_STATE, LANES), F32)],
        compiler_params=_cparams("arbitrary", "arbitrary"),
        name="ssd_backward",
    )(xc, dt, y_f, z, _pair_states(s0_b), bias, alog, dskip, ssd_norm.reshape(1, SSD_INNER))
    return y, _unpair_states(s_f), _unpair_states(s_b)

def _hgrn_kernel(q_ref, ff_ref, fb_ref, i_ref, g_ref, lb_ref, s0_ref, nw_ref, o_ref, sfin_ref,
                 sf_ref, sb_ref, ob_ref, *, layer):
    t_len = q_ref.shape[0]
    n = HGRN_CHUNK
    n_chunks = t_len // n
    tn = (((0,), (0,)), ((), ()))
    nt = (((1,), (1,)), ((), ()))

    lbp = lb_ref[...]
    e = jnp.exp(lbp - jnp.max(lbp, axis=0, keepdims=True))
    sm = e / jnp.sum(e, axis=0, keepdims=True)
    lb = sm[0] * 0.0
    for j in range(1, layer + 1):
        lb = lb + sm[j]

    r = lax.broadcasted_iota(I32, (n, n), 0)
    c = lax.broadcasted_iota(I32, (n, n), 1)
    lower = jnp.where(c <= r, 1.0, 0.0).astype(F32)
    srow = lax.broadcasted_iota(I32, (n, HGRN_DK), 0)
    qscale = HGRN_DK ** -0.5

    def chunk(row0, f_ref, lb_d, reverse, s_ref):
        q = _silu(q_ref[pl.ds(row0, n), :]) * qscale
        f = f_ref[pl.ds(row0, n), :]
        v = i_ref[pl.ds(row0, n), :]
        k = (1.0 - lb_d) * jax.nn.sigmoid(-f)
        lf = jnp.log(lb_d + (1.0 - lb_d) * jax.nn.sigmoid(f))
        cum = jnp.dot(lower, lf, precision=HIGHEST, preferred_element_type=F32)
        tot = cum[n - 1:n, :]
        if reverse:
            cum = cum - lf
        rows = []
        for t in range(n):
            tile0 = (t // SUBLANES) * SUBLANES
            lo, hi = (tile0, n) if reverse else (0, tile0 + SUBLANES)
            cum_s = cum[lo:hi]
            if reverse:
                seg = jnp.where(srow[lo:hi] >= t, cum_s - cum[t:t + 1, :], NEG_BIG)
            else:
                seg = jnp.where(srow[lo:hi] <= t, cum[t:t + 1, :] - cum_s, NEG_BIG)
            a = q[t:t + 1, :] * k[lo:hi] * jnp.exp(seg)
            sc = jnp.sum(a, axis=1, keepdims=True)
            rows.append(jnp.sum(sc * v[lo:hi], axis=0, keepdims=True))
        o = jnp.concatenate(rows, axis=0)
        s_old = s_ref[...]
        if reverse:
            q_in = q * jnp.exp(tot - cum)
            k_out = k * jnp.exp(cum)
        else:
            q_in = q * jnp.exp(cum)
            k_out = k * jnp.exp(tot - cum)
        o = o + lax.dot_general(q_in.astype(BF16), s_old.astype(BF16), nt, preferred_element_type=F32)
        s_ref[...] = jnp.exp(tot) * s_old + lax.dot_general(
            v.astype(BF16), k_out.astype(BF16), tn, preferred_element_type=F32)
        return o

    sf_ref[...] = s0_ref[0, 0, 0].T
    sb_ref[...] = s0_ref[0, 1, 0].T

    def body(ci, carry):
        row_f = pl.multiple_of(ci * n, n)
        row_b = pl.multiple_of((n_chunks - 1 - ci) * n, n)
        o_ref[pl.ds(row_f, n), :] = chunk(row_f, ff_ref, lb[0:1, :], False, sf_ref)
        ob_ref[pl.ds(row_b, n), :] = chunk(row_b, fb_ref, lb[1:2, :], True, sb_ref)
        return carry

    lax.fori_loop(0, n_chunks, body, 0)
    sfin_ref[0, 0, 0] = sf_ref[...].T
    sfin_ref[0, 1, 0] = sb_ref[...].T

    nw = nw_ref[...]
    blk = min(t_len, ROW_TILE)

    def finish(bi, carry):
        row0 = pl.multiple_of(bi * blk, blk)
        o = o_ref[pl.ds(row0, blk), :] + ob_ref[pl.ds(row0, blk), :]
        ms = jnp.mean(o * o, axis=-1, keepdims=True)
        o = o * lax.rsqrt(ms + NORM_EPS) * nw
        o_ref[pl.ds(row0, blk), :] = o * _silu(g_ref[pl.ds(row0, blk), :])
        return carry

    lax.fori_loop(0, t_len // blk, finish, 0)


def _hgrn(q, f_fw, f_bw, iv, g, o_lb, state0, g_norm, seq, layer):
    m, width = q.shape
    nb = m // seq
    dv = width // HGRN_HEADS
    col = pl.BlockSpec((seq, dv), lambda b, h: (b, h))
    state_spec = pl.BlockSpec((1, 2, 1, HGRN_DK, dv), lambda b, h: (b, 0, h, 0, 0))
    return pl.pallas_call(
        functools.partial(_hgrn_kernel, layer=layer),
        grid=(nb, HGRN_HEADS),
        in_specs=[col, col, col, col, col,
                  pl.BlockSpec((o_lb.shape[0], 2, HGRN_DK), lambda b, h: (0, 0, h)),
                  state_spec,
                  pl.BlockSpec((1, dv), lambda b, h: (0, 0))],
        out_specs=[col, state_spec],
        out_shape=[jax.ShapeDtypeStruct((m, width), F32),
                   jax.ShapeDtypeStruct((nb, 2, HGRN_HEADS, HGRN_DK, dv), F32)],
        scratch_shapes=[pltpu.VMEM((dv, HGRN_DK), F32), pltpu.VMEM((dv, HGRN_DK), F32),
                        pltpu.VMEM((seq, dv), F32)],
        compiler_params=_cparams("arbitrary", "arbitrary"),
        name="hgrn2",
    )(q, f_fw, f_bw, iv, g, o_lb, state0, g_norm.reshape(1, dv))

EVEN_SPLITS = ((0, 512), (512, 640), (640, 768), (768, 1280), (1280, 2048), (2048, 2176))
HGRN_SPLITS = tuple((i * 1024, (i + 1) * 1024) for i in range(5))


def _even_weight(w):
    main = EVEN_SPLITS[-1][0]
    return jnp.pad(w, ((0, 0), (0, LANES - (w.shape[1] - main)))).astype(BF16)


def _run_trunk(x3, mods, mod_row0, P, cache, sc_order=None):
    nb, seq, d = x3.shape
    x = x3.reshape(nb * seq, d)
    depth = P['norm_mix'].shape[0]
    ks, vs, ssd_states, hgrn_states = [], [], [], []
    for l in range(depth):
        j = l // 2
        row0 = (l * SUBLANES + mod_row0, 0 if cache is None else 1)
        if l % 2 == 0:
            q, k, v, z, xbc, dt = _inproj(x, mods, row0, seq, P['norm_mix'][l], P['e_w_in'][j],
                                          EVEN_SPLITS, "even_in_proj")
            q, k = _qkprep(q, k, P['e_q_norm'][j], P['e_k_norm'][j], seq, rope=cache is not None)
            if cache is None:
                s0_f = jnp.zeros((nb, SSD_HEADS, D_STATE, SSD_HEAD_DIM), F32)
                s0_b = s0_f
                o_attn = _ctx_attention(q, k, v, P['e_sink'][j], seq)
            else:
                s0_f, s0_b = cache[2][:, j, 0], cache[2][:, j, 1]
                n_ctx = cache[0].shape[2]
                o_attn = _lat_attention(q, k, v, cache[0][:, j].reshape(nb, n_ctx, -1),
                                        cache[1][:, j].reshape(nb, n_ctx, -1), P['e_sink'][j], seq)
            y, s_f, s_b = _ssd(xbc, dt, z, s0_f, s0_b, P['e_conv_w'][j], P['e_conv_b'][j],
                               P['e_dt_bias'][j], P['e_a_log'][j], P['e_d_skip'][j], P['e_ssd_norm'][j], seq)
            if cache is None:
                ks.append(k.reshape(nb, seq, N_KV_HEADS, HEAD_DIM))
                vs.append(v.reshape(nb, seq, N_KV_HEADS, HEAD_DIM))
                ssd_states.append(jnp.stack([s_f, s_b], axis=1))
            mix = jnp.concatenate([o_attn, y], axis=1)
            x = _outproj(mix, x, mods, row0, seq, P['e_w_out'][j], "even_out_proj")
        else:
            q, f_fw, f_bw, iv, g = _inproj(x, mods, row0, seq, P['norm_mix'][l], P['o_w_in'][j],
                                           HGRN_SPLITS, "odd_in_proj")
            if cache is None:
                s0 = jnp.zeros((nb, 2, HGRN_HEADS, HGRN_DK, d // HGRN_HEADS), F32)
            else:
                s0 = cache[3][:, j]
            o, s_new = _hgrn(q, f_fw, f_bw, iv, g, P['o_lb'], s0, P['o_g_norm'][j], seq, j)
            if cache is None:
                hgrn_states.append(s_new)
            x = _outproj(o, x, mods, row0, seq, P['o_w_out'][j], "odd_out_proj")
        x, y_sc = _peer(x, mods, row0, seq, l, P['norm_ffn'][l], P['p_w_q'][l], P['p_sub_keys'][l],
                        P['p_u'], P['p_v'],
                        (SC_SHARE_CONTEXT if cache is None else SC_SHARE_LATENT)[l],
                        after=None if sc_order is None or cache is None else sc_order[l])
        if sc_order is not None and cache is None:
            sc_order.append(y_sc)
    y = x.reshape(nb, seq, d)
    if cache is not None:
        return y, None
    return y, (jnp.stack(ks, axis=1), jnp.stack(vs, axis=1),
               jnp.stack(ssd_states, axis=1), jnp.stack(hgrn_states, axis=1))


def kernel(x_prompt, x_sample, cache_k, cache_v, state_ssd, state_hgrn, c, c_ctx, w_ada, b_ada, norm_mix, norm_ffn, e_w_in, e_q_norm, e_k_norm, e_sink, e_conv_w, e_conv_b, e_dt_bias, e_a_log, e_d_skip, e_ssd_norm, e_w_out, o_w_in, o_lb, o_g_norm, o_w_out, p_w_q, p_sub_keys, p_u, p_v):
    depth, d, d6 = w_ada.shape
    b_lat = x_sample.shape[0]
    cond_rows = jnp.concatenate([c_ctx[None, :], c, jnp.zeros((SUBLANES - 1 - b_lat, d), F32)], axis=0)
    mods = _modulation(cond_rows, w_ada, b_ada).reshape(depth * SUBLANES, 1, d6)
    P = {
        'norm_mix': norm_mix, 'norm_ffn': norm_ffn,
        'e_w_in': jnp.stack([_even_weight(w) for w in e_w_in]), 'e_q_norm': e_q_norm, 'e_k_norm': e_k_norm,
        'e_sink': e_sink, 'e_conv_w': e_conv_w, 'e_conv_b': e_conv_b, 'e_dt_bias': e_dt_bias,
        'e_a_log': e_a_log, 'e_d_skip': e_d_skip, 'e_ssd_norm': e_ssd_norm,
        'e_w_out': e_w_out.astype(BF16),
        'o_w_in': o_w_in.astype(BF16), 'o_lb': o_lb, 'o_g_norm': o_g_norm, 'o_w_out': o_w_out.astype(BF16),
        'p_w_q': p_w_q.astype(BF16),
        'p_sub_keys': p_sub_keys.astype(BF16).reshape(depth, PEER_HEADS * 2, PEER_NKEYS, PEER_DKEY),
        'p_u': p_u, 'p_v': p_v,
    }
    sc_order = []
    y_prompt, new_state = _run_trunk(x_prompt, mods, 0, P, None, sc_order)
    y_sample, _ = _run_trunk(x_sample, mods, 1, P, (cache_k, cache_v, state_ssd, state_hgrn), sc_order)
    return (y_prompt, y_sample) + new_state
```

```python
import functools
import math

import jax
import jax.numpy as jnp
from jax import lax
from jax.experimental import pallas as pl
from jax.experimental.pallas import tpu as pltpu
from jax.experimental.pallas import tpu_sc as plsc

F32 = jnp.float32
BF16 = jnp.bfloat16
I32 = jnp.int32
HIGHEST = lax.Precision.HIGHEST

NORM_EPS = 1e-6
NEG_BIG = -1e30
LANES = 128
SUBLANES = 8
VMEM_LIMIT = 48 * 1024 * 1024

GRID_W = 64
HEAD_DIM = 64
N_Q_HEADS = 8
N_KV_HEADS = 2
GQA_GROUP = 4
WINDOW = 128
ROPE_THETA = 10000.0
SSD_HEADS = 8
SSD_HEAD_DIM = 64
SSD_GROUPS = 2
D_STATE = 64
CONV_K = 5
HGRN_HEADS = 8
HGRN_DK = 128
HGRN_CHUNK = 32
PEER_HEADS = 8
PEER_NKEYS = 128
PEER_TOPK = 16
PEER_DKEY = 128
PEER_PAIRS = PEER_HEADS * PEER_TOPK

ROW_TILE = 256
PEER_BLOCK = 128
SC_LANES = 16
SC_CHUNK = 16
SC_SHARE_CONTEXT = ((1, 1), (1, 1), (1, 1), (1, 1))
SC_SHARE_LATENT = ((25, 32), (25, 32), (25, 32), (3, 4))
GELU_C = math.sqrt(2.0 / math.pi)


def _cparams(*sem):
    return pltpu.CompilerParams(dimension_semantics=sem, vmem_limit_bytes=VMEM_LIMIT)


def _norm_mod(x, nw, scale, shift):
    ms = jnp.mean(x * x, axis=-1, keepdims=True)
    return (x * lax.rsqrt(ms + NORM_EPS)) * nw * (1.0 + scale) + shift


def _mod_kernel(c_ref, w_ref, b_ref, o_ref):
    c = c_ref[...]
    s = c * jax.nn.sigmoid(c)
    o_ref[0] = jnp.dot(s, w_ref[0], precision=HIGHEST, preferred_element_type=F32) + b_ref[0]


def _modulation(cond_rows, w_ada, b_ada):
    depth, d, n = w_ada.shape
    rows = cond_rows.shape[0]
    return pl.pallas_call(
        _mod_kernel,
        grid=(depth, n // d),
        in_specs=[pl.BlockSpec((rows, d), lambda l, j: (0, 0)),
                  pl.BlockSpec((1, d, d), lambda l, j: (l, 0, j)),
                  pl.BlockSpec((1, 1, d), lambda l, j: (l, 0, j))],
        out_specs=pl.BlockSpec((1, rows, d), lambda l, j: (l, 0, j)),
        out_shape=jax.ShapeDtypeStruct((depth, rows, n), F32),
        compiler_params=_cparams("arbitrary", "arbitrary"),
        name="modulation",
    )(cond_rows, w_ada, b_ada.reshape(depth, 1, n))


def _mod_spec(mod_row0, seq, tile, d6, first_tile=0):
    row0, per_batch = mod_row0
    return pl.BlockSpec((1, 1, d6),
                        lambda i: (row0 + per_batch * (((i + first_tile) * tile) // seq), 0, 0))


def _inproj_kernel(x_ref, m_ref, nw_ref, w_ref, *o_refs, splits, d):
    m = m_ref[0]
    h = _norm_mod(x_ref[...], nw_ref[...], m[:, d:2 * d], m[:, 0:d]).astype(BF16)
    for o_ref, (a, b) in zip(o_refs, splits):
        o_ref[...] = jnp.dot(h, w_ref[:, a:b], preferred_element_type=F32)


def _inproj(x, mods, mod_row0, seq, nw, w_bf16, splits, name):
    m, d = x.shape
    n = w_bf16.shape[1]
    tile = min(ROW_TILE, seq)
    return pl.pallas_call(
        functools.partial(_inproj_kernel, splits=splits, d=d),
        grid=(m // tile,),
        in_specs=[pl.BlockSpec((tile, d), lambda i: (i, 0)),
                  _mod_spec(mod_row0, seq, tile, mods.shape[-1]),
                  pl.BlockSpec((1, d), lambda i: (0, 0)),
                  pl.BlockSpec((d, n), lambda i: (0, 0))],
        out_specs=[pl.BlockSpec((tile, b - a), lambda i: (i, 0)) for a, b in splits],
        out_shape=[jax.ShapeDtypeStruct((m, b - a), F32) for a, b in splits],
        compiler_params=_cparams("arbitrary"),
        name=name,
    )(x, mods, nw.reshape(1, d), w_bf16)


def _outproj_kernel(mix_ref, x_ref, m_ref, w_ref, o_ref, *, d):
    y = jnp.dot(mix_ref[...].astype(BF16), w_ref[...], preferred_element_type=F32)
    o_ref[...] = x_ref[...] + m_ref[0][:, 2 * d:3 * d] * y


def _outproj(mix, x, mods, mod_row0, seq, w_bf16, name):
    m, d = x.shape
    k = mix.shape[1]
    tile = min(ROW_TILE, seq)
    return pl.pallas_call(
        functools.partial(_outproj_kernel, d=d),
        grid=(m // tile,),
        in_specs=[pl.BlockSpec((tile, k), lambda i: (i, 0)),
                  pl.BlockSpec((tile, d), lambda i: (i, 0)),
                  _mod_spec(mod_row0, seq, tile, mods.shape[-1]),
                  pl.BlockSpec((k, d), lambda i: (0, 0))],
        out_specs=pl.BlockSpec((tile, d), lambda i: (i, 0)),
        out_shape=jax.ShapeDtypeStruct((m, d), F32),
        compiler_params=_cparams("arbitrary"),
        name=name,
    )(mix, x, mods, w_bf16)


def _topk_over_rows(s, k, payload=None):
    n = s.shape[0]
    iota = lax.broadcasted_iota(I32, s.shape, 0)
    vals, idxs, pays = [], [], []
    for _ in range(k):
        m = jnp.max(s, axis=0, keepdims=True)
        i = jnp.min(jnp.where(s == m, iota, n), axis=0, keepdims=True)
        hit = iota == i
        vals.append(m)
        idxs.append(i)
        if payload is not None:
            pays.append(jnp.max(jnp.where(hit, payload, -1), axis=0, keepdims=True))
        s = jnp.where(hit, -jnp.inf, s)
    out = (jnp.concatenate(vals, axis=0), jnp.concatenate(idxs, axis=0))
    if payload is not None:
        out += (jnp.concatenate(pays, axis=0),)
    return out


def _peer_route_kernel(x_ref, m_ref, nw_ref, wq_ref, keys_ref, h_ref, e_ref, g_ref, *, d):
    m = m_ref[0]
    h = _norm_mod(x_ref[...], nw_ref[...], m[:, 4 * d:5 * d], m[:, 3 * d:4 * d])
    h_ref[...] = h
    hb = h.astype(BF16)
    nt = (((1,), (1,)), ((), ()))
    for head in range(PEER_HEADS):
        tops = []
        for half in range(2):
            c0 = (head * 2 + half) * PEER_DKEY
            q = jnp.dot(hb, wq_ref[:, c0:c0 + PEER_DKEY], preferred_element_type=F32)
            s = lax.dot_general(keys_ref[head * 2 + half], q.astype(BF16), nt,
                                preferred_element_type=F32)
            tops.append(_topk_over_rows(s, PEER_TOPK))
        (s0, i0), (s1, i1) = tops
        widths = [PEER_TOPK // (a + 1) for a in range(PEER_TOPK)]
        n_pad = -sum(widths) % SUBLANES
        cand_s = jnp.concatenate([s0[a:a + 1] + s1[:w] for a, w in enumerate(widths)]
                                 + [jnp.full((n_pad, s0.shape[1]), -jnp.inf, F32)], axis=0)
        cand_e = jnp.concatenate([i0[a:a + 1] * PEER_NKEYS + i1[:w] for a, w in enumerate(widths)]
                                 + [jnp.zeros((n_pad, s0.shape[1]), I32)], axis=0)
        best_s, _, best_e = _topk_over_rows(cand_s, PEER_TOPK, payload=cand_e)
        p = jnp.exp(best_s - best_s[0:1])
        r0 = head * PEER_TOPK
        e_ref[r0:r0 + PEER_TOPK, :] = best_e
        g_ref[r0:r0 + PEER_TOPK, :] = p / jnp.sum(p, axis=0, keepdims=True)


def _peer_route(x, mods, mod_row0, seq, nw, wq_bf16, keys_bf16):
    m, d = x.shape
    tile = min(ROW_TILE, seq)
    nq = wq_bf16.shape[1]
    return pl.pallas_call(
        functools.partial(_peer_route_kernel, d=d),
        grid=(m // tile,),
        in_specs=[pl.BlockSpec((tile, d), lambda i: (i, 0)),
                  _mod_spec(mod_row0, seq, tile, mods.shape[-1]),
                  pl.BlockSpec((1, d), lambda i: (0, 0)),
                  pl.BlockSpec((d, nq), lambda i: (0, 0)),
                  pl.BlockSpec(keys_bf16.shape, lambda i: (0, 0, 0))],
        out_specs=[pl.BlockSpec((tile, d), lambda i: (i, 0)),
                   pl.BlockSpec((PEER_PAIRS, tile), lambda i: (0, i)),
                   pl.BlockSpec((PEER_PAIRS, tile), lambda i: (0, i))],
        out_shape=[jax.ShapeDtypeStruct((m, d), F32),
                   jax.ShapeDtypeStruct((PEER_PAIRS, m), I32),
                   jax.ShapeDtypeStruct((PEER_PAIRS, m), F32)],
        compiler_params=_cparams("arbitrary"),
        name="peer_route",
    )(x, mods, nw.reshape(1, d), wq_bf16, keys_bf16)


def _peer_gather_kernel(idx_hbm, h_ref, g_ref, x_ref, m_ref, u_hbm, v_hbm, o_ref,
                        idx_smem, ubuf, vbuf, sem_idx, sem_u, sem_v, *, layer, d):
    blk = pl.program_id(0)
    n_groups = PEER_BLOCK // SUBLANES

    cp = pltpu.make_async_copy(idx_hbm.at[blk], idx_smem, sem_idx)
    cp.start()
    cp.wait()

    def issue(tok, slot):
        def body(pair, carry):
            e = idx_smem[tok, pair]
            pltpu.make_async_copy(u_hbm.at[layer, pl.ds(e, 1)], ubuf.at[slot, pl.ds(pair, 1)],
                                  sem_u.at[slot]).start()
            pltpu.make_async_copy(v_hbm.at[layer, pl.ds(e, 1)], vbuf.at[slot, pl.ds(pair, 1)],
                                  sem_v.at[slot]).start()
            return carry
        lax.fori_loop(0, PEER_PAIRS, body, 0, unroll=8)

    def wait(slot):
        pltpu.make_async_copy(u_hbm.at[layer, pl.ds(0, PEER_PAIRS)], ubuf.at[slot], sem_u.at[slot]).wait()
        pltpu.make_async_copy(v_hbm.at[layer, pl.ds(0, PEER_PAIRS)], vbuf.at[slot], sem_v.at[slot]).wait()

    gate2 = m_ref[0][:, 5 * d:6 * d]
    lane = lax.broadcasted_iota(I32, (PEER_PAIRS, PEER_BLOCK), 1)

    issue(0, 0)

    def group(grp, carry):
        base = pl.multiple_of(grp * SUBLANES, SUBLANES)
        h8 = h_ref[pl.ds(base, SUBLANES), :]
        rows = []
        for r in range(SUBLANES):
            tok = base + r
            slot = r % 2
            if r < SUBLANES - 1:
                issue(tok + 1, 1 - slot)
            else:
                @pl.when(grp < n_groups - 1)
                def _():
                    issue(tok + 1, 1 - slot)
            wait(slot)
            act = jnp.sum(ubuf[slot] * h8[r:r + 1, :], axis=1, keepdims=True)
            gate = jnp.sum(jnp.where(lane == tok, g_ref[...], 0.0), axis=1, keepdims=True)
            w = jax.nn.gelu(act) * gate
            rows.append(jnp.sum(vbuf[slot] * w, axis=0, keepdims=True))
        out8 = jnp.concatenate(rows, axis=0)
        o_ref[pl.ds(base, SUBLANES), :] = x_ref[pl.ds(base, SUBLANES), :] + gate2 * out8
        return carry

    lax.fori_loop(0, n_groups, group, 0)


def _peer_gather(idx, h, gates, x, mods, mod_row0, seq, p_u, p_v, layer, first_block):
    m, d = x.shape
    nblk = m // PEER_BLOCK - first_block
    idx3 = idx[:, first_block * PEER_BLOCK:].T.reshape(nblk, PEER_BLOCK, PEER_PAIRS)
    rows = pl.BlockSpec((PEER_BLOCK, d), lambda i: (i + first_block, 0))
    return pl.pallas_call(
        functools.partial(_peer_gather_kernel, layer=layer, d=d),
        grid=(nblk,),
        in_specs=[pl.BlockSpec(memory_space=pl.ANY),
                  rows,
                  pl.BlockSpec((PEER_PAIRS, PEER_BLOCK), lambda i: (0, i + first_block)),
                  rows,
                  _mod_spec(mod_row0, seq, PEER_BLOCK, mods.shape[-1], first_block),
                  pl.BlockSpec(memory_space=pl.ANY),
                  pl.BlockSpec(memory_space=pl.ANY)],
        out_specs=pl.BlockSpec((PEER_BLOCK, d), lambda i: (i, 0)),
        out_shape=jax.ShapeDtypeStruct((nblk * PEER_BLOCK, d), F32),
        scratch_shapes=[pltpu.SMEM((PEER_BLOCK, PEER_PAIRS), I32),
                        pltpu.VMEM((2, PEER_PAIRS, d), F32),
                        pltpu.VMEM((2, PEER_PAIRS, d), F32),
                        pltpu.SemaphoreType.DMA,
                        pltpu.SemaphoreType.DMA((2,)),
                        pltpu.SemaphoreType.DMA((2,))],
        compiler_params=_cparams("arbitrary"),
        name="peer_gather",
    )(idx3, h, gates, x, mods, p_u, p_v)


def _peer_experts_sc(idx, h, gates, u_rows, v_rows):
    m = idx.shape[0]
    d = h.shape[1]
    info = plsc.get_sparse_core_info()
    n_workers = info.num_cores * info.num_subcores
    per = m // n_workers
    n_chunks = PEER_PAIRS // SC_CHUNK
    n_vec = d // SC_LANES
    mesh = plsc.VectorSubcoreMesh(core_axis_name="c", subcore_axis_name="s")

    @functools.partial(
        pl.kernel, out_type=jax.ShapeDtypeStruct((m, d), F32), mesh=mesh,
        scratch_types=[pltpu.VMEM((2, PEER_PAIRS), I32), pltpu.VMEM((2, d), F32),
                       pltpu.VMEM((2, PEER_PAIRS), F32), pltpu.VMEM((2, d), F32),
                       pltpu.VMEM((2, SC_CHUNK, d), F32), pltpu.VMEM((2, SC_CHUNK, d), F32),
                       pltpu.SemaphoreType.DMA((2,)), pltpu.SemaphoreType.DMA((2,)),
                       pltpu.SemaphoreType.DMA((2,)), pltpu.SemaphoreType.DMA((2,))],
        compiler_params=pltpu.CompilerParams(needs_layout_passes=False),
        name="peer_experts_sc")
    def body(idx_hbm, h_hbm, g_hbm, u_hbm, v_hbm, o_hbm,
             idx_v, x_v, g_v, out_v, ubuf, vbuf, sem_meta, sem_out, sem_u, sem_v):
        wid = lax.axis_index("c") * info.num_subcores + lax.axis_index("s")
        tok0 = wid * per
        lane = lax.iota(I32, SC_LANES)

        def meta_copies(ti, ms):
            t = tok0 + ti
            return (pltpu.make_async_copy(idx_hbm.at[t], idx_v.at[ms], sem_meta.at[ms]),
                    pltpu.make_async_copy(h_hbm.at[t], x_v.at[ms], sem_meta.at[ms]),
                    pltpu.make_async_copy(g_hbm.at[t], g_v.at[ms], sem_meta.at[ms]))

        def gather_copies(ms, c, slot):
            ids = idx_v.at[ms, pl.ds(c * SC_CHUNK, SC_CHUNK)]
            return (pltpu.make_async_copy(u_hbm.at[ids], ubuf.at[slot], sem_u.at[slot]),
                    pltpu.make_async_copy(v_hbm.at[ids], vbuf.at[slot], sem_v.at[slot]))

        def out_copy(ti, ms):
            return pltpu.make_async_copy(out_v.at[ms], o_hbm.at[tok0 + ti], sem_out.at[ms])

        for cp in meta_copies(0, 0):
            cp.start()
        for cp in meta_copies(0, 0):
            cp.wait()
        for cp in gather_copies(0, 0, 0):
            cp.start()

        def token(ti, carry):
            ms = ti % 2
            nxt = 1 - ms

            @pl.when(ti + 1 < per)
            def _():
                for cp in meta_copies(ti + 1, nxt):
                    cp.start()

            @pl.when(ti >= 2)
            def _():
                out_copy(ti - 2, ms).wait()

            def zero(j, c):
                out_v[ms, pl.ds(j * SC_LANES, SC_LANES)] = jnp.zeros((SC_LANES,), F32)
                return c
            lax.fori_loop(0, n_vec, zero, 0)

            for c in range(n_chunks):
                slot = c % 2
                if c + 1 < n_chunks:
                    for cp in gather_copies(ms, c + 1, 1 - slot):
                        cp.start()
                else:
                    @pl.when(ti + 1 < per)
                    def _():
                        for cp in meta_copies(ti + 1, nxt):
                            cp.wait()
                        for cp in gather_copies(nxt, 0, 1 - slot):
                            cp.start()
                cu, cv = gather_copies(ms, c, slot)
                cu.wait()

                def udot(j, accs):
                    xj = x_v[ms, pl.ds(j * SC_LANES, SC_LANES)]
                    return tuple(accs[r] + ubuf[slot, r, pl.ds(j * SC_LANES, SC_LANES)] * xj
                                 for r in range(SC_CHUNK))
                accs = lax.fori_loop(0, n_vec, udot,
                                     tuple(jnp.zeros((SC_LANES,), F32) for _ in range(SC_CHUNK)))
                act = jnp.zeros((SC_LANES,), F32)
                for r in range(SC_CHUNK):
                    act = jnp.where(lane == r, jnp.sum(accs[r]), act)
                y = GELU_C * (act + 0.044715 * (act * act * act))
                w = act / (1.0 + jnp.exp(-2.0 * y)) * g_v[ms, pl.ds(c * SC_CHUNK, SC_CHUNK)]
                ws = [jnp.sum(jnp.where(lane == r, w, 0.0)) for r in range(SC_CHUNK)]
                cv.wait()

                @plsc.parallel_loop(0, n_vec, unroll=2)
                def _(j):
                    parts = [ws[r] * vbuf[slot, r, pl.ds(j * SC_LANES, SC_LANES)] for r in range(SC_CHUNK)]
                    while len(parts) > 1:
                        parts = [parts[i] + parts[i + 1] for i in range(0, len(parts), 2)]
                    plsc.addupdate(out_v.at[ms, pl.ds(j * SC_LANES, SC_LANES)], parts[0])

            out_copy(ti, ms).start()
            return carry

        lax.fori_loop(0, per, token, 0)
        for back in (2, 1):
            if per >= back:
                out_copy(per - back, (per - back) % 2).wait()

    return body(idx, h, gates, u_rows, v_rows)


def _residual_kernel(x_ref, y_ref, m_ref, o_ref, *, d):
    o_ref[...] = x_ref[...] + m_ref[0][:, 5 * d:6 * d] * y_ref[...]


def _residual(x, y, mods, mod_row0, seq):
    m, d = y.shape
    tile = min(ROW_TILE, seq)
    return pl.pallas_call(
        functools.partial(_residual_kernel, d=d),
        grid=(m // tile,),
        in_specs=[pl.BlockSpec((tile, d), lambda i: (i, 0)),
                  pl.BlockSpec((tile, d), lambda i: (i, 0)),
                  _mod_spec(mod_row0, seq, tile, mods.shape[-1])],
        out_specs=pl.BlockSpec((tile, d), lambda i: (i, 0)),
        out_shape=jax.ShapeDtypeStruct((m, d), F32),
        compiler_params=_cparams("arbitrary"),
        name="peer_residual",
    )(x, y, mods)


def _peer(x, mods, mod_row0, seq, layer, nw, wq_bf16, keys_bf16, p_u, p_v, sc_share, after=None):
    m, d = x.shape
    h, idx, gates = _peer_route(x, mods, mod_row0, seq, nw, wq_bf16, keys_bf16)
    n_blocks = m // PEER_BLOCK
    sc_blocks = (n_blocks * sc_share[0]) // sc_share[1]
    m_sc = sc_blocks * PEER_BLOCK
    n_experts = p_u.shape[1]
    idx_sc = idx[:, :m_sc].T + layer * n_experts
    if after is not None:
        idx_sc, _ = lax.optimization_barrier((idx_sc, after))
    y_sc = _peer_experts_sc(idx_sc, h, gates[:, :m_sc].T,
                            p_u.reshape(-1, d), p_v.reshape(-1, d))
    x_sc = _residual(x, y_sc, mods, mod_row0, seq)
    if sc_blocks == n_blocks:
        return x_sc, y_sc
    x_tc = _peer_gather(idx, h, gates, x, mods, mod_row0, seq, p_u, p_v, layer, sc_blocks)
    return jnp.concatenate([x_sc, x_tc], axis=0), y_sc


def _head_mean_square(x):
    n = x.shape[1]
    r = lax.broadcasted_iota(I32, (n, n), 0) // HEAD_DIM
    c = lax.broadcasted_iota(I32, (n, n), 1) // HEAD_DIM
    seg = jnp.where(r == c, 1.0 / HEAD_DIM, 0.0).astype(F32)
    return jnp.dot(x * x, seg, precision=HIGHEST, preferred_element_type=F32)


def _swap_rot_halves(x):
    n = x.shape[1]
    quarter = HEAD_DIM // 4
    lane = lax.broadcasted_iota(I32, x.shape, 1)
    lo = (lane % (2 * quarter)) < quarter
    return jnp.where(lo, pltpu.roll(x, n - quarter, axis=1), pltpu.roll(x, quarter, axis=1))


def _qkprep_kernel(q_ref, k_ref, qw_ref, kw_ref, *rest, rope):
    if rope:
        cos_ref, sin_ref, qo_ref, ko_ref = rest
    else:
        qo_ref, ko_ref = rest
    q = q_ref[...]
    k = k_ref[...]
    q = q * lax.rsqrt(_head_mean_square(q) + NORM_EPS) * qw_ref[...]
    k = k * lax.rsqrt(_head_mean_square(k) + NORM_EPS) * kw_ref[...]
    if rope:
        cos = cos_ref[...]
        sin = sin_ref[...]
        cq = jnp.concatenate([cos] * (q.shape[1] // LANES), axis=1)
        sq = jnp.concatenate([sin] * (q.shape[1] // LANES), axis=1)
        q = q * cq + _swap_rot_halves(q) * sq
        k = k * cos + _swap_rot_halves(k) * sin
    qo_ref[...] = q
    ko_ref[...] = k


def _rope_tables(seq):
    axis_dim = HEAD_DIM // 2
    inv_freq = ROPE_THETA ** (-jnp.arange(0, axis_dim, 2, dtype=F32) / axis_dim)
    t = jnp.arange(seq)
    pos = jnp.stack([(t // GRID_W).astype(F32), (t % GRID_W).astype(F32)], axis=1)
    lane = jnp.arange(LANES)
    dd = lane % HEAD_DIM
    ang = pos[:, dd // axis_dim] * inv_freq[dd % (axis_dim // 2)][None, :]
    sign = jnp.where((dd % axis_dim) < axis_dim // 2, -1.0, 1.0).astype(F32)
    return jnp.cos(ang), jnp.sin(ang) * sign[None, :]


def _qkprep(q, k, qw, kw, seq, rope):
    m, nq = q.shape
    nk = k.shape[1]
    tile = min(ROW_TILE, seq)
    qw_row = jnp.tile(qw, nq // HEAD_DIM).reshape(1, nq)
    kw_row = jnp.tile(kw, nk // HEAD_DIM).reshape(1, nk)
    in_specs = [pl.BlockSpec((tile, nq), lambda i: (i, 0)),
                pl.BlockSpec((tile, nk), lambda i: (i, 0)),
                pl.BlockSpec((1, nq), lambda i: (0, 0)),
                pl.BlockSpec((1, nk), lambda i: (0, 0))]
    args = [q, k, qw_row, kw_row]
    if rope:
        cos, sin = _rope_tables(seq)
        per_seq = seq // tile
        in_specs += [pl.BlockSpec((tile, LANES), lambda i: (i % per_seq, 0)),
                     pl.BlockSpec((tile, LANES), lambda i: (i % per_seq, 0))]
        args += [cos, sin]
    return pl.pallas_call(
        functools.partial(_qkprep_kernel, rope=rope),
        grid=(m // tile,),
        in_specs=in_specs,
        out_specs=[pl.BlockSpec((tile, nq), lambda i: (i, 0)),
                   pl.BlockSpec((tile, nk), lambda i: (i, 0))],
        out_shape=[jax.ShapeDtypeStruct((m, nq), F32), jax.ShapeDtypeStruct((m, nk), F32)],
        compiler_params=_cparams("arbitrary"),
        name="qk_prep",
    )(*args)


def _dup_halves(x):
    lane = lax.broadcasted_iota(I32, x.shape, 1)
    sw = pltpu.roll(x, HEAD_DIM, axis=1)
    lo = lane < HEAD_DIM
    return jnp.where(lo, x, sw), jnp.where(lo, sw, x)


def _attend(q, k_all, v_all, sink_ref, mask):
    scale = HEAD_DIM ** -0.5
    nt = (((1,), (1,)), ((), ()))
    kk = [a.astype(BF16) for a in _dup_halves(k_all)]
    vv = [a.astype(BF16) for a in _dup_halves(v_all)]
    lane = lax.broadcasted_iota(I32, (q.shape[0], LANES), 1)
    lo = lane < HEAD_DIM
    tiles = []
    for t in range(q.shape[1] // LANES):
        qt = q[:, t * LANES:(t + 1) * LANES]
        g = (2 * t) // GQA_GROUP
        halves = []
        for hh in range(2):
            head = 2 * t + hh
            qm = jnp.where(lo if hh == 0 else ~lo, qt, 0.0).astype(BF16)
            s = lax.dot_general(qm, kk[g], nt, preferred_element_type=F32) * scale
            if mask is not None:
                s = jnp.where(mask, s, NEG_BIG)
            sink = sink_ref[head]
            mx = jnp.maximum(jnp.max(s, axis=1, keepdims=True), sink)
            p = jnp.exp(s - mx)
            den = jnp.sum(p, axis=1, keepdims=True) + jnp.exp(sink - mx)
            p = (p / den).astype(BF16)
            halves.append(jnp.dot(p, vv[g], preferred_element_type=F32))
        tiles.append(jnp.where(lo, halves[0], halves[1]))
    return jnp.concatenate(tiles, axis=1)


def _ctx_attn_kernel(sink_ref, q_ref, k_ref, v_ref, o_ref):
    o_ref[...] = _attend(q_ref[...], k_ref[...], v_ref[...], sink_ref, None)


def _ctx_attention(q, k, v, sink, seq):
    m, nq = q.shape
    nk = k.shape[1]
    return pl.pallas_call(
        _ctx_attn_kernel,
        grid=(m // seq,),
        in_specs=[pl.BlockSpec(memory_space=pltpu.SMEM),
                  pl.BlockSpec((seq, nq), lambda b: (b, 0)),
                  pl.BlockSpec((seq, nk), lambda b: (b, 0)),
                  pl.BlockSpec((seq, nk), lambda b: (b, 0))],
        out_specs=pl.BlockSpec((seq, nq), lambda b: (b, 0)),
        out_shape=jax.ShapeDtypeStruct((m, nq), F32),
        compiler_params=_cparams("arbitrary"),
        name="ctx_attention",
    )(sink, q, k, v)


def _lat_attn_kernel(sink_ref, q_ref, kc_ref, vc_ref, kp_ref, k0_ref, kn_ref, vp_ref, v0_ref, vn_ref,
                     o_ref, *, seq):
    qb = pl.program_id(1)
    blk = q_ref.shape[0]
    n_ctx = kc_ref.shape[1]
    k_all = jnp.concatenate([kc_ref[0], kp_ref[...], k0_ref[...], kn_ref[...]], axis=0)
    v_all = jnp.concatenate([vc_ref[0], vp_ref[...], v0_ref[...], vn_ref[...]], axis=0)
    tk = k_all.shape[0]
    qpos = qb * blk + lax.broadcasted_iota(I32, (blk, tk), 0)
    col = lax.broadcasted_iota(I32, (blk, tk), 1)
    kpos = (qb - 1) * blk + col - n_ctx
    local_ok = (jnp.abs(qpos - kpos) <= WINDOW) & (kpos >= 0) & (kpos < seq)
    mask = (col < n_ctx) | local_ok
    o_ref[...] = _attend(q_ref[...], k_all, v_all, sink_ref, mask)


def _lat_attention(q, k, v, k_ctx, v_ctx, sink, seq):
    m, nq = q.shape
    nk = k.shape[1]
    blk = WINDOW
    nb = seq // blk
    n_ctx = k_ctx.shape[1]
    last = m // blk - 1

    def kv_spec(shift):
        return pl.BlockSpec((blk, nk), lambda b, i: (jnp.clip(b * nb + i + shift, 0, last), 0))

    ctx_spec = pl.BlockSpec((1, n_ctx, nk), lambda b, i: (b, 0, 0))
    return pl.pallas_call(
        functools.partial(_lat_attn_kernel, seq=seq),
        grid=(m // seq, nb),
        in_specs=[pl.BlockSpec(memory_space=pltpu.SMEM),
                  pl.BlockSpec((blk, nq), lambda b, i: (b * nb + i, 0)),
                  ctx_spec, ctx_spec,
                  kv_spec(-1), kv_spec(0), kv_spec(1),
                  kv_spec(-1), kv_spec(0), kv_spec(1)],
        out_specs=pl.BlockSpec((blk, nq), lambda b, i: (b * nb + i, 0)),
        out_shape=jax.ShapeDtypeStruct((m, nq), F32),
        compiler_params=_cparams("arbitrary", "arbitrary"),
        name="lat_attention",
    )(sink, q, k_ctx, v_ctx, k, k, k, v, v, v)

SSD_BLOCK = 256
SSD_PAIRS = SSD_HEADS // 2
SSD_INNER = SSD_HEADS * SSD_HEAD_DIM
HALO = SUBLANES


def _softplus(x):
    return jnp.maximum(x, 0.0) + jnp.log1p(jnp.exp(-jnp.abs(x)))


def _silu(x):
    return x * jax.nn.sigmoid(x)


def _ssd_decays(dt_raw, bias, a_log):
    n = dt_raw.shape[0]
    dt = _softplus(dt_raw + bias)
    log_a = dt * (-jnp.exp(a_log))
    r = lax.broadcasted_iota(I32, (n, n), 0)
    c = lax.broadcasted_iota(I32, (n, n), 1)
    lower = jnp.where(c <= r, 1.0, 0.0).astype(F32)
    upper = jnp.where(r <= c, 1.0, 0.0).astype(F32)
    cum_col = jnp.dot(lower, log_a, precision=HIGHEST, preferred_element_type=F32)
    dt_row = dt.T
    la_row = log_a.T
    cum_row = jnp.dot(la_row, upper, precision=HIGHEST, preferred_element_type=F32)
    return dt, log_a, cum_col, dt_row, la_row, cum_row


def _ssd_scan_chunk(xs, bmat, cmat, w_of, q_scale_of, k_scale_of, carry_of, s_ref):
    nt = (((1,), (1,)), ((), ()))
    n = xs.shape[0]
    lane = lax.broadcasted_iota(I32, (n, LANES), 1)
    lo = lane < SSD_HEAD_DIM
    lane_s = lax.broadcasted_iota(I32, (D_STATE, LANES), 1)
    lo_s = lane_s < SSD_HEAD_DIM
    b_t = bmat.T
    cb16 = cmat.astype(BF16)
    ys = []
    for pair in range(SSD_PAIRS):
        g = (2 * pair) // (SSD_HEADS // SSD_GROUPS)
        in_g = (lane // D_STATE) == g
        cg = jnp.where(in_g, cmat, 0.0)
        cb = lax.dot_general(cg.astype(BF16), bmat.astype(BF16), nt, preferred_element_type=F32)
        x_pair = xs[:, pair * LANES:(pair + 1) * LANES]
        x16 = x_pair.astype(BF16)
        s_old = s_ref[pair]
        s2 = jnp.concatenate([s_old, s_old], axis=0).astype(BF16)
        bg_t = b_t[g * D_STATE:(g + 1) * D_STATE, :]
        y_h, s_h = [], []
        for hh in range(2):
            h = 2 * pair + hh
            w = (cb * w_of(h)).astype(BF16)
            y = jnp.dot(w, x16, preferred_element_type=F32)
            cq = (cg * q_scale_of(h)).astype(BF16)
            y = y + jnp.dot(cq, s2, preferred_element_type=F32)
            y_h.append(y)
            kt = (bg_t * k_scale_of(h)).astype(BF16)
            s_h.append(carry_of(h) * s_old + jnp.dot(kt, x16, preferred_element_type=F32))
        ys.append(jnp.where(lo, y_h[0], y_h[1]))
        s_ref[pair] = jnp.where(lo_s, s_h[0], s_h[1])
    return jnp.concatenate(ys, axis=1)


def _ssd_fwd_kernel(x_ref, xp_ref, xn_ref, dt_ref, s0_ref, cw_ref, cb_ref, bias_ref, alog_ref,
                    y_ref, xc_ref, sfin_ref, s_ref):
    c = pl.program_id(1)
    nc = pl.num_programs(1)
    n = x_ref.shape[0]

    @pl.when(c == 0)
    def _():
        s_ref[...] = s0_ref[0]

    prev = jnp.where(c > 0, xp_ref[...], 0.0)
    nxt = jnp.where(c < nc - 1, xn_ref[...], 0.0)
    xe = jnp.concatenate([prev, x_ref[...], nxt], axis=0)
    pad = (CONV_K - 1) // 2
    acc = cb_ref[...] + cw_ref[0:1, :] * xe[HALO - pad:HALO - pad + n, :]
    for k in range(1, CONV_K):
        acc = acc + cw_ref[k:k + 1, :] * xe[HALO - pad + k:HALO - pad + k + n, :]
    xc = _silu(acc)
    xc_ref[...] = xc
    xs = xc[:, :SSD_INNER]
    bmat = xc[:, SSD_INNER:SSD_INNER + LANES]
    cmat = xc[:, SSD_INNER + LANES:SSD_INNER + 2 * LANES]

    dt, log_a, cum_col, dt_row, la_row, cum_row = _ssd_decays(dt_ref[...], bias_ref[...], alog_ref[...])
    r = lax.broadcasted_iota(I32, (n, n), 0)
    cc = lax.broadcasted_iota(I32, (n, n), 1)
    causal = cc <= r
    last_col = cum_col[n - 1:n, :]

    def w_of(h):
        seg = cum_col[:, h:h + 1] - cum_row[h:h + 1, :]
        return jnp.exp(jnp.where(causal, seg, NEG_BIG)) * dt_row[h:h + 1, :]

    def q_scale_of(h):
        return jnp.exp(cum_col[:, h:h + 1])

    def k_scale_of(h):
        return dt_row[h:h + 1, :] * jnp.exp(cum_row[h:h + 1, n - 1:n] - cum_row[h:h + 1, :])

    def carry_of(h):
        return jnp.exp(last_col[:, h:h + 1])

    y_ref[...] = _ssd_scan_chunk(xs, bmat, cmat, w_of, q_scale_of, k_scale_of, carry_of, s_ref)

    @pl.when(c == nc - 1)
    def _():
        sfin_ref[0] = s_ref[...]


def _ssd_bwd_kernel(xc_ref, dt_ref, yf_ref, z_ref, s0_ref, bias_ref, alog_ref, dskip_ref, nw_ref,
                    y_ref, sfin_ref, s_ref):
    c = pl.program_id(1)
    nc = pl.num_programs(1)
    n = xc_ref.shape[0]

    @pl.when(c == 0)
    def _():
        s_ref[...] = s0_ref[0]

    xc = xc_ref[...]
    xs = xc[:, :SSD_INNER]
    bmat = xc[:, SSD_INNER:SSD_INNER + LANES]
    cmat = xc[:, SSD_INNER + LANES:SSD_INNER + 2 * LANES]
    dt, log_a, cum_col, dt_row, la_row, cum_row = _ssd_decays(dt_ref[...], bias_ref[...], alog_ref[...])
    ex_col = cum_col - log_a
    ex_row = cum_row - la_row
    r = lax.broadcasted_iota(I32, (n, n), 0)
    cc = lax.broadcasted_iota(I32, (n, n), 1)
    anti = cc >= r
    tot_col = cum_col[n - 1:n, :]
    off = SSD_HEADS

    def w_of(h):
        j = off + h
        seg = ex_row[j:j + 1, :] - ex_col[:, j:j + 1]
        return jnp.exp(jnp.where(anti, seg, NEG_BIG)) * dt_row[j:j + 1, :]

    def q_scale_of(h):
        j = off + h
        return jnp.exp(tot_col[:, j:j + 1] - ex_col[:, j:j + 1])

    def k_scale_of(h):
        j = off + h
        return dt_row[j:j + 1, :] * jnp.exp(ex_row[j:j + 1, :])

    def carry_of(h):
        j = off + h
        return jnp.exp(tot_col[:, j:j + 1])

    y_b = _ssd_scan_chunk(xs, bmat, cmat, w_of, q_scale_of, k_scale_of, carry_of, s_ref)
    y = yf_ref[...] + y_b + dskip_ref[...] * xs
    y = y * _silu(z_ref[...])
    ms = jnp.mean(y * y, axis=-1, keepdims=True)
    y_ref[...] = y * lax.rsqrt(ms + NORM_EPS) * nw_ref[...]

    @pl.when(c == nc - 1)
    def _():
        sfin_ref[0] = s_ref[...]


def _pair_states(s):
    b, h, n, p = s.shape
    return s.reshape(b, h // 2, 2, n, p).transpose(0, 1, 3, 2, 4).reshape(b, h // 2, n, 2 * p)


def _unpair_states(s):
    b, hp, n, p2 = s.shape
    return s.reshape(b, hp, n, 2, p2 // 2).transpose(0, 1, 3, 2, 4).reshape(b, hp * 2, n, p2 // 2)


def _ssd(xbc, dt, z, s0_f, s0_b, conv_w, conv_b, dt_bias, a_log, d_skip, ssd_norm, seq):
    m, nx = xbc.shape
    nb = m // seq
    blk = min(SSD_BLOCK, seq)
    nc = seq // blk
    hb = blk // HALO
    n_halo = m // HALO
    pad16 = lambda a: jnp.pad(a.reshape(1, -1), ((0, 0), (0, LANES - a.size)))
    bias = pad16(dt_bias)
    alog = pad16(a_log)
    state_spec = pl.BlockSpec((1, SSD_PAIRS, D_STATE, LANES), lambda b, c: (b, 0, 0, 0))
    state_shape = jax.ShapeDtypeStruct((nb, SSD_PAIRS, D_STATE, LANES), F32)
    row = lambda width: pl.BlockSpec((1, width), lambda b, c: (0, 0))

    def fwd_rows(width):
        return pl.BlockSpec((blk, width), lambda b, c: (b * nc + c, 0))

    def bwd_rows(width):
        return pl.BlockSpec((blk, width), lambda b, c: (b * nc + nc - 1 - c, 0))

    y_f, xc, s_f = pl.pallas_call(
        _ssd_fwd_kernel,
        grid=(nb, nc),
        in_specs=[fwd_rows(nx),
                  pl.BlockSpec((HALO, nx), lambda b, c: (jnp.maximum((b * nc + c) * hb - 1, 0), 0)),
                  pl.BlockSpec((HALO, nx), lambda b, c: (jnp.minimum((b * nc + c + 1) * hb, n_halo - 1), 0)),
                  fwd_rows(LANES), state_spec,
                  pl.BlockSpec((CONV_K, nx), lambda b, c: (0, 0)), row(nx), row(LANES), row(LANES)],
        out_specs=[fwd_rows(SSD_INNER), fwd_rows(nx), state_spec],
        out_shape=[jax.ShapeDtypeStruct((m, SSD_INNER), F32), jax.ShapeDtypeStruct((m, nx), F32), state_shape],
        scratch_shapes=[pltpu.VMEM((SSD_PAIRS, D_STATE, LANES), F32)],
        compiler_params=_cparams("arbitrary", "arbitrary"),
        name="ssd_forward",
    )(xbc, xbc, xbc, dt, _pair_states(s0_f), conv_w, conv_b.reshape(1, nx), bias, alog)

    dskip = jnp.repeat(d_skip, SSD_HEAD_DIM).reshape(1, SSD_INNER)
    y, s_b = pl.pallas_call(
        _ssd_bwd_kernel,
        grid=(nb, nc),
        in_specs=[bwd_rows(nx), bwd_rows(LANES), bwd_rows(SSD_INNER), bwd_rows(SSD_INNER), state_spec,
                  row(LANES), row(LANES), row(SSD_INNER), row(SSD_INNER)],
        out_specs=[bwd_rows(SSD_INNER), state_spec],
        out_shape=[jax.ShapeDtypeStruct((m, SSD_INNER), F32), state_shape],
        scratch_shapes=[pltpu.VMEM((SSD_PAIRS, D_STATE, LANES), F32)],
        compiler_params=_cparams("arbitrary", "arbitrary"),
        name="ssd_backward",
    )(xc, dt, y_f, z, _pair_states(s0_b), bias, alog, dskip, ssd_norm.reshape(1, SSD_INNER))
    return y, _unpair_states(s_f), _unpair_states(s_b)

def _hgrn_kernel(q_ref, ff_ref, fb_ref, i_ref, g_ref, lb_ref, s0_ref, nw_ref, o_ref, sfin_ref,
                 sf_ref, sb_ref, ob_ref, *, layer):
    t_len = q_ref.shape[0]
    n = HGRN_CHUNK
    n_chunks = t_len // n
    tn = (((0,), (0,)), ((), ()))
    nt = (((1,), (1,)), ((), ()))

    lbp = lb_ref[...]
    e = jnp.exp(lbp - jnp.max(lbp, axis=0, keepdims=True))
    sm = e / jnp.sum(e, axis=0, keepdims=True)
    lb = sm[0] * 0.0
    for j in range(1, layer + 1):
        lb = lb + sm[j]

    r = lax.broadcasted_iota(I32, (n, n), 0)
    c = lax.broadcasted_iota(I32, (n, n), 1)
    lower = jnp.where(c <= r, 1.0, 0.0).astype(F32)
    srow = lax.broadcasted_iota(I32, (n, HGRN_DK), 0)
    qscale = HGRN_DK ** -0.5

    def chunk(row0, f_ref, lb_d, reverse, s_ref):
        q = _silu(q_ref[pl.ds(row0, n), :]) * qscale
        f = f_ref[pl.ds(row0, n), :]
        v = i_ref[pl.ds(row0, n), :]
        k = (1.0 - lb_d) * jax.nn.sigmoid(-f)
        lf = jnp.log(lb_d + (1.0 - lb_d) * jax.nn.sigmoid(f))
        cum = jnp.dot(lower, lf, precision=HIGHEST, preferred_element_type=F32)
        tot = cum[n - 1:n, :]
        if reverse:
            cum = cum - lf
        rows = []
        for t in range(n):
            tile0 = (t // SUBLANES) * SUBLANES
            lo, hi = (tile0, n) if reverse else (0, tile0 + SUBLANES)
            cum_s = cum[lo:hi]
            if reverse:
                seg = jnp.where(srow[lo:hi] >= t, cum_s - cum[t:t + 1, :], NEG_BIG)
            else:
                seg = jnp.where(srow[lo:hi] <= t, cum[t:t + 1, :] - cum_s, NEG_BIG)
            a = q[t:t + 1, :] * k[lo:hi] * jnp.exp(seg)
            sc = jnp.sum(a, axis=1, keepdims=True)
            rows.append(jnp.sum(sc * v[lo:hi], axis=0, keepdims=True))
        o = jnp.concatenate(rows, axis=0)
        s_old = s_ref[...]
        if reverse:
            q_in = q * jnp.exp(tot - cum)
            k_out = k * jnp.exp(cum)
        else:
            q_in = q * jnp.exp(cum)
            k_out = k * jnp.exp(tot - cum)
        o = o + lax.dot_general(q_in.astype(BF16), s_old.astype(BF16), nt, preferred_element_type=F32)
        s_ref[...] = jnp.exp(tot) * s_old + lax.dot_general(
            v.astype(BF16), k_out.astype(BF16), tn, preferred_element_type=F32)
        return o

    sf_ref[...] = s0_ref[0, 0, 0].T
    sb_ref[...] = s0_ref[0, 1, 0].T

    def body(ci, carry):
        row_f = pl.multiple_of(ci * n, n)
        row_b = pl.multiple_of((n_chunks - 1 - ci) * n, n)
        o_ref[pl.ds(row_f, n), :] = chunk(row_f, ff_ref, lb[0:1, :], False, sf_ref)
        ob_ref[pl.ds(row_b, n), :] = chunk(row_b, fb_ref, lb[1:2, :], True, sb_ref)
        return carry

    lax.fori_loop(0, n_chunks, body, 0)
    sfin_ref[0, 0, 0] = sf_ref[...].T
    sfin_ref[0, 1, 0] = sb_ref[...].T

    nw = nw_ref[...]
    blk = min(t_len, ROW_TILE)

    def finish(bi, carry):
        row0 = pl.multiple_of(bi * blk, blk)
        o = o_ref[pl.ds(row0, blk), :] + ob_ref[pl.ds(row0, blk), :]
        ms = jnp.mean(o * o, axis=-1, keepdims=True)
        o = o * lax.rsqrt(ms + NORM_EPS) * nw
        o_ref[pl.ds(row0, blk), :] = o * _silu(g_ref[pl.ds(row0, blk), :])
        return carry

    lax.fori_loop(0, t_len // blk, finish, 0)


def _hgrn(q, f_fw, f_bw, iv, g, o_lb, state0, g_norm, seq, layer):
    m, width = q.shape
    nb = m // seq
    dv = width // HGRN_HEADS
    col = pl.BlockSpec((seq, dv), lambda b, h: (b, h))
    state_spec = pl.BlockSpec((1, 2, 1, HGRN_DK, dv), lambda b, h: (b, 0, h, 0, 0))
    return pl.pallas_call(
        functools.partial(_hgrn_kernel, layer=layer),
        grid=(nb, HGRN_HEADS),
        in_specs=[col, col, col, col, col,
                  pl.BlockSpec((o_lb.shape[0], 2, HGRN_DK), lambda b, h: (0, 0, h)),
                  state_spec,
                  pl.BlockSpec((1, dv), lambda b, h: (0, 0))],
        out_specs=[col, state_spec],
        out_shape=[jax.ShapeDtypeStruct((m, width), F32),
                   jax.ShapeDtypeStruct((nb, 2, HGRN_HEADS, HGRN_DK, dv), F32)],
        scratch_shapes=[pltpu.VMEM((dv, HGRN_DK), F32), pltpu.VMEM((dv, HGRN_DK), F32),
                        pltpu.VMEM((seq, dv), F32)],
        compiler_params=_cparams("arbitrary", "arbitrary"),
        name="hgrn2",
    )(q, f_fw, f_bw, iv, g, o_lb, state0, g_norm.reshape(1, dv))

EVEN_SPLITS = ((0, 512), (512, 640), (640, 768), (768, 1280), (1280, 2048), (2048, 2176))
HGRN_SPLITS = tuple((i * 1024, (i + 1) * 1024) for i in range(5))


def _even_weight(w):
    main = EVEN_SPLITS[-1][0]
    return jnp.pad(w, ((0, 0), (0, LANES - (w.shape[1] - main)))).astype(BF16)


def _run_trunk(x3, mods, mod_row0, P, cache, sc_order=None):
    nb, seq, d = x3.shape
    x = x3.reshape(nb * seq, d)
    depth = P['norm_mix'].shape[0]
    ks, vs, ssd_states, hgrn_states = [], [], [], []
    for l in range(depth):
        j = l // 2
        row0 = (l * SUBLANES + mod_row0, 0 if cache is None else 1)
        if l % 2 == 0:
            q, k, v, z, xbc, dt = _inproj(x, mods, row0, seq, P['norm_mix'][l], P['e_w_in'][j],
                                          EVEN_SPLITS, "even_in_proj")
            q, k = _qkprep(q, k, P['e_q_norm'][j], P['e_k_norm'][j], seq, rope=cache is not None)
            if cache is None:
                s0_f = jnp.zeros((nb, SSD_HEADS, D_STATE, SSD_HEAD_DIM), F32)
                s0_b = s0_f
                o_attn = _ctx_attention(q, k, v, P['e_sink'][j], seq)
            else:
                s0_f, s0_b = cache[2][:, j, 0], cache[2][:, j, 1]
                n_ctx = cache[0].shape[2]
                o_attn = _lat_attention(q, k, v, cache[0][:, j].reshape(nb, n_ctx, -1),
                                        cache[1][:, j].reshape(nb, n_ctx, -1), P['e_sink'][j], seq)
            y, s_f, s_b = _ssd(xbc, dt, z, s0_f, s0_b, P['e_conv_w'][j], P['e_conv_b'][j],
                               P['e_dt_bias'][j], P['e_a_log'][j], P['e_d_skip'][j], P['e_ssd_norm'][j], seq)
            if cache is None:
                ks.append(k.reshape(nb, seq, N_KV_HEADS, HEAD_DIM))
                vs.append(v.reshape(nb, seq, N_KV_HEADS, HEAD_DIM))
                ssd_states.append(jnp.stack([s_f, s_b], axis=1))
            mix = jnp.concatenate([o_attn, y], axis=1)
            x = _outproj(mix, x, mods, row0, seq, P['e_w_out'][j], "even_out_proj")
        else:
            q, f_fw, f_bw, iv, g = _inproj(x, mods, row0, seq, P['norm_mix'][l], P['o_w_in'][j],
                                           HGRN_SPLITS, "odd_in_proj")
            if cache is None:
                s0 = jnp.zeros((nb, 2, HGRN_HEADS, HGRN_DK, d // HGRN_HEADS), F32)
            else:
                s0 = cache[3][:, j]
            o, s_new = _hgrn(q, f_fw, f_bw, iv, g, P['o_lb'], s0, P['o_g_norm'][j], seq, j)
            if cache is None:
                hgrn_states.append(s_new)
            x = _outproj(o, x, mods, row0, seq, P['o_w_out'][j], "odd_out_proj")
        x, y_sc = _peer(x, mods, row0, seq, l, P['norm_ffn'][l], P['p_w_q'][l], P['p_sub_keys'][l],
                        P['p_u'], P['p_v'],
                        (SC_SHARE_CONTEXT if cache is None else SC_SHARE_LATENT)[l],
                        after=None if sc_order is None or cache is None else sc_order[l])
        if sc_order is not None and cache is None:
            sc_order.append(y_sc)
    y = x.reshape(nb, seq, d)
    if cache is not None:
        return y, None
    return y, (jnp.stack(ks, axis=1), jnp.stack(vs, axis=1),
               jnp.stack(ssd_states, axis=1), jnp.stack(hgrn_states, axis=1))


def kernel(x_prompt, x_sample, cache_k, cache_v, state_ssd, state_hgrn, c, c_ctx, w_ada, b_ada, norm_mix, norm_ffn, e_w_in, e_q_norm, e_k_norm, e_sink, e_conv_w, e_conv_b, e_dt_bias, e_a_log, e_d_skip, e_ssd_norm, e_w_out, o_w_in, o_lb, o_g_norm, o_w_out, p_w_q, p_sub_keys, p_u, p_v):
    depth, d, d6 = w_ada.shape
    b_lat = x_sample.shape[0]
    cond_rows = jnp.concatenate([c_ctx[None, :], c, jnp.zeros((SUBLANES - 1 - b_lat, d), F32)], axis=0)
    mods = _modulation(cond_rows, w_ada, b_ada).reshape(depth * SUBLANES, 1, d6)
    P = {
        'norm_mix': norm_mix, 'norm_ffn': norm_ffn,
        'e_w_in': jnp.stack([_even_weight(w) for w in e_w_in]), 'e_q_norm': e_q_norm, 'e_k_norm': e_k_norm,
        'e_sink': e_sink, 'e_conv_w': e_conv_w, 'e_conv_b': e_conv_b, 'e_dt_bias': e_dt_bias,
        'e_a_log': e_a_log, 'e_d_skip': e_d_skip, 'e_ssd_norm': e_ssd_norm,
        'e_w_out': e_w_out.astype(BF16),
        'o_w_in': o_w_in.astype(BF16), 'o_lb': o_lb, 'o_g_norm': o_g_norm, 'o_w_out': o_w_out.astype(BF16),
        'p_w_q': p_w_q.astype(BF16),
        'p_sub_keys': p_sub_keys.astype(BF16).reshape(depth, PEER_HEADS * 2, PEER_NKEYS, PEER_DKEY),
        'p_u': p_u, 'p_v': p_v,
    }
    sc_order = []
    y_prompt, new_state = _run_trunk(x_prompt, mods, 0, P, None, sc_order)
    y_sample, _ = _run_trunk(x_sample, mods, 1, P, (cache_k, cache_v, state_ssd, state_hgrn), sc_order)
    return (y_prompt, y_sample) + new_state
```

```python
import functools
import math

import jax
import jax.numpy as jnp
from jax import lax
from jax.experimental import pallas as pl
from jax.experimental.pallas import tpu as pltpu
from jax.experimental.pallas import tpu_sc as plsc

F32 = jnp.float32
BF16 = jnp.bfloat16
I32 = jnp.int32
HIGHEST = lax.Precision.HIGHEST

NORM_EPS = 1e-6
NEG_BIG = -1e30
LANES = 128
SUBLANES = 8
VMEM_LIMIT = 48 * 1024 * 1024

GRID_W = 64
HEAD_DIM = 64
N_Q_HEADS = 8
N_KV_HEADS = 2
GQA_GROUP = 4
WINDOW = 128
ROPE_THETA = 10000.0
SSD_HEADS = 8
SSD_HEAD_DIM = 64
SSD_GROUPS = 2
D_STATE = 64
CONV_K = 5
HGRN_HEADS = 8
HGRN_DK = 128
HGRN_CHUNK = 32
PEER_HEADS = 8
PEER_NKEYS = 128
PEER_TOPK = 16
PEER_DKEY = 128
PEER_PAIRS = PEER_HEADS * PEER_TOPK

ROW_TILE = 256
PEER_BLOCK = 128
SC_LANES = 16
SC_CHUNK = 16
SC_SHARE_CONTEXT = ((1, 1), (1, 1), (1, 1), (1, 1))
SC_SHARE_LATENT = ((13, 16), (13, 16), (13, 16), (3, 4))
GELU_C = math.sqrt(2.0 / math.pi)


def _cparams(*sem):
    return pltpu.CompilerParams(dimension_semantics=sem, vmem_limit_bytes=VMEM_LIMIT)


def _norm_mod(x, nw, scale, shift):
    ms = jnp.mean(x * x, axis=-1, keepdims=True)
    return (x * lax.rsqrt(ms + NORM_EPS)) * nw * (1.0 + scale) + shift


def _mod_kernel(c_ref, w_ref, b_ref, o_ref):
    c = c_ref[...]
    s = c * jax.nn.sigmoid(c)
    o_ref[0] = jnp.dot(s, w_ref[0], precision=HIGHEST, preferred_element_type=F32) + b_ref[0]


def _modulation(cond_rows, w_ada, b_ada):
    depth, d, n = w_ada.shape
    rows = cond_rows.shape[0]
    return pl.pallas_call(
        _mod_kernel,
        grid=(depth, n // d),
        in_specs=[pl.BlockSpec((rows, d), lambda l, j: (0, 0)),
                  pl.BlockSpec((1, d, d), lambda l, j: (l, 0, j)),
                  pl.BlockSpec((1, 1, d), lambda l, j: (l, 0, j))],
        out_specs=pl.BlockSpec((1, rows, d), lambda l, j: (l, 0, j)),
        out_shape=jax.ShapeDtypeStruct((depth, rows, n), F32),
        compiler_params=_cparams("arbitrary", "arbitrary"),
        name="modulation",
    )(cond_rows, w_ada, b_ada.reshape(depth, 1, n))


def _mod_spec(mod_row0, seq, tile, d6, first_tile=0):
    row0, per_batch = mod_row0
    return pl.BlockSpec((1, 1, d6),
                        lambda i: (row0 + per_batch * (((i + first_tile) * tile) // seq), 0, 0))


def _inproj_kernel(x_ref, m_ref, nw_ref, w_ref, *o_refs, splits, d):
    m = m_ref[0]
    h = _norm_mod(x_ref[...], nw_ref[...], m[:, d:2 * d], m[:, 0:d]).astype(BF16)
    for o_ref, (a, b) in zip(o_refs, splits):
        o_ref[...] = jnp.dot(h, w_ref[:, a:b], preferred_element_type=F32)


def _inproj(x, mods, mod_row0, seq, nw, w_bf16, splits, name):
    m, d = x.shape
    n = w_bf16.shape[1]
    tile = min(ROW_TILE, seq)
    return pl.pallas_call(
        functools.partial(_inproj_kernel, splits=splits, d=d),
        grid=(m // tile,),
        in_specs=[pl.BlockSpec((tile, d), lambda i: (i, 0)),
                  _mod_spec(mod_row0, seq, tile, mods.shape[-1]),
                  pl.BlockSpec((1, d), lambda i: (0, 0)),
                  pl.BlockSpec((d, n), lambda i: (0, 0))],
        out_specs=[pl.BlockSpec((tile, b - a), lambda i: (i, 0)) for a, b in splits],
        out_shape=[jax.ShapeDtypeStruct((m, b - a), F32) for a, b in splits],
        compiler_params=_cparams("arbitrary"),
        name=name,
    )(x, mods, nw.reshape(1, d), w_bf16)


def _outproj_kernel(mix_ref, x_ref, m_ref, w_ref, o_ref, *, d):
    y = jnp.dot(mix_ref[...].astype(BF16), w_ref[...], preferred_element_type=F32)
    o_ref[...] = x_ref[...] + m_ref[0][:, 2 * d:3 * d] * y


def _outproj(mix, x, mods, mod_row0, seq, w_bf16, name):
    m, d = x.shape
    k = mix.shape[1]
    tile = min(ROW_TILE, seq)
    return pl.pallas_call(
        functools.partial(_outproj_kernel, d=d),
        grid=(m // tile,),
        in_specs=[pl.BlockSpec((tile, k), lambda i: (i, 0)),
                  pl.BlockSpec((tile, d), lambda i: (i, 0)),
                  _mod_spec(mod_row0, seq, tile, mods.shape[-1]),
                  pl.BlockSpec((k, d), lambda i: (0, 0))],
        out_specs=pl.BlockSpec((tile, d), lambda i: (i, 0)),
        out_shape=jax.ShapeDtypeStruct((m, d), F32),
        compiler_params=_cparams("arbitrary"),
        name=name,
    )(mix, x, mods, w_bf16)


def _topk_over_rows(s, k, payload=None):
    n = s.shape[0]
    iota = lax.broadcasted_iota(I32, s.shape, 0)
    vals, idxs, pays = [], [], []
    for _ in range(k):
        m = jnp.max(s, axis=0, keepdims=True)
        i = jnp.min(jnp.where(s == m, iota, n), axis=0, keepdims=True)
        hit = iota == i
        vals.append(m)
        idxs.append(i)
        if payload is not None:
            pays.append(jnp.max(jnp.where(hit, payload, -1), axis=0, keepdims=True))
        s = jnp.where(hit, -jnp.inf, s)
    out = (jnp.concatenate(vals, axis=0), jnp.concatenate(idxs, axis=0))
    if payload is not None:
        out += (jnp.concatenate(pays, axis=0),)
    return out


def _peer_route_kernel(x_ref, m_ref, nw_ref, wq_ref, keys_ref, h_ref, e_ref, g_ref, *, d):
    m = m_ref[0]
    h = _norm_mod(x_ref[...], nw_ref[...], m[:, 4 * d:5 * d], m[:, 3 * d:4 * d])
    h_ref[...] = h
    hb = h.astype(BF16)
    nt = (((1,), (1,)), ((), ()))
    for head in range(PEER_HEADS):
        tops = []
        for half in range(2):
            c0 = (head * 2 + half) * PEER_DKEY
            q = jnp.dot(hb, wq_ref[:, c0:c0 + PEER_DKEY], preferred_element_type=F32)
            s = lax.dot_general(keys_ref[head * 2 + half], q.astype(BF16), nt,
                                preferred_element_type=F32)
            tops.append(_topk_over_rows(s, PEER_TOPK))
        (s0, i0), (s1, i1) = tops
        widths = [PEER_TOPK // (a + 1) for a in range(PEER_TOPK)]
        n_pad = -sum(widths) % SUBLANES
        cand_s = jnp.concatenate([s0[a:a + 1] + s1[:w] for a, w in enumerate(widths)]
                                 + [jnp.full((n_pad, s0.shape[1]), -jnp.inf, F32)], axis=0)
        cand_e = jnp.concatenate([i0[a:a + 1] * PEER_NKEYS + i1[:w] for a, w in enumerate(widths)]
                                 + [jnp.zeros((n_pad, s0.shape[1]), I32)], axis=0)
        best_s, _, best_e = _topk_over_rows(cand_s, PEER_TOPK, payload=cand_e)
        p = jnp.exp(best_s - best_s[0:1])
        r0 = head * PEER_TOPK
        e_ref[r0:r0 + PEER_TOPK, :] = best_e
        g_ref[r0:r0 + PEER_TOPK, :] = p / jnp.sum(p, axis=0, keepdims=True)


def _peer_route(x, mods, mod_row0, seq, nw, wq_bf16, keys_bf16):
    m, d = x.shape
    tile = min(ROW_TILE, seq)
    nq = wq_bf16.shape[1]
    return pl.pallas_call(
        functools.partial(_peer_route_kernel, d=d),
        grid=(m // tile,),
        in_specs=[pl.BlockSpec((tile, d), lambda i: (i, 0)),
                  _mod_spec(mod_row0, seq, tile, mods.shape[-1]),
                  pl.BlockSpec((1, d), lambda i: (0, 0)),
                  pl.BlockSpec((d, nq), lambda i: (0, 0)),
                  pl.BlockSpec(keys_bf16.shape, lambda i: (0, 0, 0))],
        out_specs=[pl.BlockSpec((tile, d), lambda i: (i, 0)),
                   pl.BlockSpec((PEER_PAIRS, tile), lambda i: (0, i)),
                   pl.BlockSpec((PEER_PAIRS, tile), lambda i: (0, i))],
        out_shape=[jax.ShapeDtypeStruct((m, d), F32),
                   jax.ShapeDtypeStruct((PEER_PAIRS, m), I32),
                   jax.ShapeDtypeStruct((PEER_PAIRS, m), F32)],
        compiler_params=_cparams("arbitrary"),
        name="peer_route",
    )(x, mods, nw.reshape(1, d), wq_bf16, keys_bf16)


def _peer_gather_kernel(idx_hbm, h_ref, g_ref, x_ref, m_ref, u_hbm, v_hbm, o_ref,
                        idx_smem, ubuf, vbuf, sem_idx, sem_u, sem_v, *, layer, d):
    blk = pl.program_id(0)
    n_groups = PEER_BLOCK // SUBLANES

    cp = pltpu.make_async_copy(idx_hbm.at[blk], idx_smem, sem_idx)
    cp.start()
    cp.wait()

    def issue(tok, slot):
        def body(pair, carry):
            e = idx_smem[tok, pair]
            pltpu.make_async_copy(u_hbm.at[layer, pl.ds(e, 1)], ubuf.at[slot, pl.ds(pair, 1)],
                                  sem_u.at[slot]).start()
            pltpu.make_async_copy(v_hbm.at[layer, pl.ds(e, 1)], vbuf.at[slot, pl.ds(pair, 1)],
                                  sem_v.at[slot]).start()
            return carry
        lax.fori_loop(0, PEER_PAIRS, body, 0, unroll=8)

    def wait(slot):
        pltpu.make_async_copy(u_hbm.at[layer, pl.ds(0, PEER_PAIRS)], ubuf.at[slot], sem_u.at[slot]).wait()
        pltpu.make_async_copy(v_hbm.at[layer, pl.ds(0, PEER_PAIRS)], vbuf.at[slot], sem_v.at[slot]).wait()

    gate2 = m_ref[0][:, 5 * d:6 * d]
    lane = lax.broadcasted_iota(I32, (PEER_PAIRS, PEER_BLOCK), 1)

    issue(0, 0)

    def group(grp, carry):
        base = pl.multiple_of(grp * SUBLANES, SUBLANES)
        h8 = h_ref[pl.ds(base, SUBLANES), :]
        rows = []
        for r in range(SUBLANES):
            tok = base + r
            slot = r % 2
            if r < SUBLANES - 1:
                issue(tok + 1, 1 - slot)
            else:
                @pl.when(grp < n_groups - 1)
                def _():
                    issue(tok + 1, 1 - slot)
            wait(slot)
            act = jnp.sum(ubuf[slot] * h8[r:r + 1, :], axis=1, keepdims=True)
            gate = jnp.sum(jnp.where(lane == tok, g_ref[...], 0.0), axis=1, keepdims=True)
            w = jax.nn.gelu(act) * gate
            rows.append(jnp.sum(vbuf[slot] * w, axis=0, keepdims=True))
        out8 = jnp.concatenate(rows, axis=0)
        o_ref[pl.ds(base, SUBLANES), :] = x_ref[pl.ds(base, SUBLANES), :] + gate2 * out8
        return carry

    lax.fori_loop(0, n_groups, group, 0)


def _peer_gather(idx, h, gates, x, mods, mod_row0, seq, p_u, p_v, layer, first_block):
    m, d = x.shape
    nblk = m // PEER_BLOCK - first_block
    idx3 = idx[:, first_block * PEER_BLOCK:].T.reshape(nblk, PEER_BLOCK, PEER_PAIRS)
    rows = pl.BlockSpec((PEER_BLOCK, d), lambda i: (i + first_block, 0))
    return pl.pallas_call(
        functools.partial(_peer_gather_kernel, layer=layer, d=d),
        grid=(nblk,),
        in_specs=[pl.BlockSpec(memory_space=pl.ANY),
                  rows,
                  pl.BlockSpec((PEER_PAIRS, PEER_BLOCK), lambda i: (0, i + first_block)),
                  rows,
                  _mod_spec(mod_row0, seq, PEER_BLOCK, mods.shape[-1], first_block),
                  pl.BlockSpec(memory_space=pl.ANY),
                  pl.BlockSpec(memory_space=pl.ANY)],
        out_specs=pl.BlockSpec((PEER_BLOCK, d), lambda i: (i, 0)),
        out_shape=jax.ShapeDtypeStruct((nblk * PEER_BLOCK, d), F32),
        scratch_shapes=[pltpu.SMEM((PEER_BLOCK, PEER_PAIRS), I32),
                        pltpu.VMEM((2, PEER_PAIRS, d), F32),
                        pltpu.VMEM((2, PEER_PAIRS, d), F32),
                        pltpu.SemaphoreType.DMA,
                        pltpu.SemaphoreType.DMA((2,)),
                        pltpu.SemaphoreType.DMA((2,))],
        compiler_params=_cparams("arbitrary"),
        name="peer_gather",
    )(idx3, h, gates, x, mods, p_u, p_v)


def _peer_experts_sc(idx, h, gates, u_rows, v_rows):
    m = idx.shape[0]
    d = h.shape[1]
    info = plsc.get_sparse_core_info()
    n_workers = info.num_cores * info.num_subcores
    per = m // n_workers
    n_chunks = PEER_PAIRS // SC_CHUNK
    n_vec = d // SC_LANES
    mesh = plsc.VectorSubcoreMesh(core_axis_name="c", subcore_axis_name="s")

    @functools.partial(
        pl.kernel, out_type=jax.ShapeDtypeStruct((m, d), F32), mesh=mesh,
        scratch_types=[pltpu.VMEM((2, PEER_PAIRS), I32), pltpu.VMEM((2, d), F32),
                       pltpu.VMEM((2, PEER_PAIRS), F32), pltpu.VMEM((2, d), F32),
                       pltpu.VMEM((2, SC_CHUNK, d), F32), pltpu.VMEM((2, SC_CHUNK, d), F32),
                       pltpu.VMEM((PEER_PAIRS,), F32),
                       pltpu.SemaphoreType.DMA((2,)), pltpu.SemaphoreType.DMA((2,)),
                       pltpu.SemaphoreType.DMA((2,)), pltpu.SemaphoreType.DMA((2,))],
        compiler_params=pltpu.CompilerParams(needs_layout_passes=False),
        name="peer_experts_sc")
    def body(idx_hbm, h_hbm, g_hbm, u_hbm, v_hbm, o_hbm,
             idx_v, x_v, g_v, out_v, ubuf, vbuf, w_v, sem_meta, sem_out, sem_u, sem_v):
        wid = lax.axis_index("c") * info.num_subcores + lax.axis_index("s")
        tok0 = wid * per
        lane = lax.iota(I32, SC_LANES)

        def meta_copies(ti, ms):
            t = tok0 + ti
            return (pltpu.make_async_copy(idx_hbm.at[t], idx_v.at[ms], sem_meta.at[ms]),
                    pltpu.make_async_copy(h_hbm.at[t], x_v.at[ms], sem_meta.at[ms]),
                    pltpu.make_async_copy(g_hbm.at[t], g_v.at[ms], sem_meta.at[ms]))

        def u_copy(ms, c, slot):
            ids = idx_v.at[ms, pl.ds(c * SC_CHUNK, SC_CHUNK)]
            return pltpu.make_async_copy(u_hbm.at[ids], ubuf.at[slot], sem_u.at[slot])

        def v_copy(ms, c, slot):
            ids = idx_v.at[ms, pl.ds(c * SC_CHUNK, SC_CHUNK)]
            return pltpu.make_async_copy(v_hbm.at[ids], vbuf.at[slot], sem_v.at[slot])

        def out_copy(ti, ms):
            return pltpu.make_async_copy(out_v.at[ms], o_hbm.at[tok0 + ti], sem_out.at[ms])

        for cp in meta_copies(0, 0):
            cp.start()
        for cp in meta_copies(0, 0):
            cp.wait()
        u_copy(0, 0, 0).start()

        def token(ti, carry):
            ms = ti % 2
            nxt = 1 - ms

            @pl.when(ti + 1 < per)
            def _():
                for cp in meta_copies(ti + 1, nxt):
                    cp.start()

            @pl.when(ti >= 2)
            def _():
                out_copy(ti - 2, ms).wait()

            def zero(j, c):
                out_v[ms, pl.ds(j * SC_LANES, SC_LANES)] = jnp.zeros((SC_LANES,), F32)
                return c
            lax.fori_loop(0, n_vec, zero, 0)

            for c in range(n_chunks):
                slot = c % 2
                if c + 1 < n_chunks:
                    u_copy(ms, c + 1, 1 - slot).start()
                else:
                    v_copy(ms, 0, 0).start()
                u_copy(ms, c, slot).wait()

                def udot(j, accs):
                    xj = x_v[ms, pl.ds(j * SC_LANES, SC_LANES)]
                    return tuple(accs[r] + ubuf[slot, r, pl.ds(j * SC_LANES, SC_LANES)] * xj
                                 for r in range(SC_CHUNK))
                accs = lax.fori_loop(0, n_vec, udot,
                                     tuple(jnp.zeros((SC_LANES,), F32) for _ in range(SC_CHUNK)))
                act = jnp.zeros((SC_LANES,), F32)
                for r in range(SC_CHUNK):
                    act = jnp.where(lane == r, jnp.sum(accs[r]), act)
                w_v[pl.ds(c * SC_CHUNK, SC_CHUNK)] = act

            for c in range(n_chunks):
                act = w_v[pl.ds(c * SC_CHUNK, SC_CHUNK)]
                y = GELU_C * (act + 0.044715 * (act * act * act))
                w_v[pl.ds(c * SC_CHUNK, SC_CHUNK)] = (
                    act / (1.0 + jnp.exp(-2.0 * y)) * g_v[ms, pl.ds(c * SC_CHUNK, SC_CHUNK)])

            for c in range(n_chunks):
                slot = c % 2
                if c + 1 < n_chunks:
                    v_copy(ms, c + 1, 1 - slot).start()
                else:
                    @pl.when(ti + 1 < per)
                    def _():
                        for cp in meta_copies(ti + 1, nxt):
                            cp.wait()
                        u_copy(nxt, 0, 0).start()
                w = w_v[pl.ds(c * SC_CHUNK, SC_CHUNK)]
                ws = [jnp.sum(jnp.where(lane == r, w, 0.0)) for r in range(SC_CHUNK)]
                v_copy(ms, c, slot).wait()

                @plsc.parallel_loop(0, n_vec, unroll=2)
                def _(j):
                    parts = [ws[r] * vbuf[slot, r, pl.ds(j * SC_LANES, SC_LANES)] for r in range(SC_CHUNK)]
                    while len(parts) > 1:
                        parts = [parts[i] + parts[i + 1] for i in range(0, len(parts), 2)]
                    plsc.addupdate(out_v.at[ms, pl.ds(j * SC_LANES, SC_LANES)], parts[0])

            out_copy(ti, ms).start()
            return carry

        lax.fori_loop(0, per, token, 0)
        for back in (2, 1):
            if per >= back:
                out_copy(per - back, (per - back) % 2).wait()

    return body(idx, h, gates, u_rows, v_rows)


def _residual_kernel(x_ref, y_ref, m_ref, o_ref, *, d):
    o_ref[...] = x_ref[...] + m_ref[0][:, 5 * d:6 * d] * y_ref[...]


def _residual(x, y, mods, mod_row0, seq):
    m, d = y.shape
    tile = min(ROW_TILE, seq)
    return pl.pallas_call(
        functools.partial(_residual_kernel, d=d),
        grid=(m // tile,),
        in_specs=[pl.BlockSpec((tile, d), lambda i: (i, 0)),
                  pl.BlockSpec((tile, d), lambda i: (i, 0)),
                  _mod_spec(mod_row0, seq, tile, mods.shape[-1])],
        out_specs=pl.BlockSpec((tile, d), lambda i: (i, 0)),
        out_shape=jax.ShapeDtypeStruct((m, d), F32),
        compiler_params=_cparams("arbitrary"),
        name="peer_residual",
    )(x, y, mods)


def _peer(x, mods, mod_row0, seq, layer, nw, wq_bf16, keys_bf16, p_u, p_v, sc_share, after=None):
    m, d = x.shape
    h, idx, gates = _peer_route(x, mods, mod_row0, seq, nw, wq_bf16, keys_bf16)
    n_blocks = m // PEER_BLOCK
    sc_blocks = (n_blocks * sc_share[0]) // sc_share[1]
    m_sc = sc_blocks * PEER_BLOCK
    n_experts = p_u.shape[1]
    idx_sc = idx[:, :m_sc].T + layer * n_experts
    if after is not None:
        idx_sc, _ = lax.optimization_barrier((idx_sc, after))
    y_sc = _peer_experts_sc(idx_sc, h, gates[:, :m_sc].T,
                            p_u.reshape(-1, d), p_v.reshape(-1, d))
    x_sc = _residual(x, y_sc, mods, mod_row0, seq)
    if sc_blocks == n_blocks:
        return x_sc, y_sc
    x_tc = _peer_gather(idx, h, gates, x, mods, mod_row0, seq, p_u, p_v, layer, sc_blocks)
    return jnp.concatenate([x_sc, x_tc], axis=0), y_sc


def _head_mean_square(x):
    n = x.shape[1]
    r = lax.broadcasted_iota(I32, (n, n), 0) // HEAD_DIM
    c = lax.broadcasted_iota(I32, (n, n), 1) // HEAD_DIM
    seg = jnp.where(r == c, 1.0 / HEAD_DIM, 0.0).astype(F32)
    return jnp.dot(x * x, seg, precision=HIGHEST, preferred_element_type=F32)


def _swap_rot_halves(x):
    n = x.shape[1]
    quarter = HEAD_DIM // 4
    lane = lax.broadcasted_iota(I32, x.shape, 1)
    lo = (lane % (2 * quarter)) < quarter
    return jnp.where(lo, pltpu.roll(x, n - quarter, axis=1), pltpu.roll(x, quarter, axis=1))


def _qkprep_kernel(q_ref, k_ref, qw_ref, kw_ref, *rest, rope):
    if rope:
        cos_ref, sin_ref, qo_ref, ko_ref = rest
    else:
        qo_ref, ko_ref = rest
    q = q_ref[...]
    k = k_ref[...]
    q = q * lax.rsqrt(_head_mean_square(q) + NORM_EPS) * qw_ref[...]
    k = k * lax.rsqrt(_head_mean_square(k) + NORM_EPS) * kw_ref[...]
    if rope:
        cos = cos_ref[...]
        sin = sin_ref[...]
        cq = jnp.concatenate([cos] * (q.shape[1] // LANES), axis=1)
        sq = jnp.concatenate([sin] * (q.shape[1] // LANES), axis=1)
        q = q * cq + _swap_rot_halves(q) * sq
        k = k * cos + _swap_rot_halves(k) * sin
    qo_ref[...] = q
    ko_ref[...] = k


def _rope_tables(seq):
    axis_dim = HEAD_DIM // 2
    inv_freq = ROPE_THETA ** (-jnp.arange(0, axis_dim, 2, dtype=F32) / axis_dim)
    t = jnp.arange(seq)
    pos = jnp.stack([(t // GRID_W).astype(F32), (t % GRID_W).astype(F32)], axis=1)
    lane = jnp.arange(LANES)
    dd = lane % HEAD_DIM
    ang = pos[:, dd // axis_dim] * inv_freq[dd % (axis_dim // 2)][None, :]
    sign = jnp.where((dd % axis_dim) < axis_dim // 2, -1.0, 1.0).astype(F32)
    return jnp.cos(ang), jnp.sin(ang) * sign[None, :]


def _qkprep(q, k, qw, kw, seq, rope):
    m, nq = q.shape
    nk = k.shape[1]
    tile = min(ROW_TILE, seq)
    qw_row = jnp.tile(qw, nq // HEAD_DIM).reshape(1, nq)
    kw_row = jnp.tile(kw, nk // HEAD_DIM).reshape(1, nk)
    in_specs = [pl.BlockSpec((tile, nq), lambda i: (i, 0)),
                pl.BlockSpec((tile, nk), lambda i: (i, 0)),
                pl.BlockSpec((1, nq), lambda i: (0, 0)),
                pl.BlockSpec((1, nk), lambda i: (0, 0))]
    args = [q, k, qw_row, kw_row]
    if rope:
        cos, sin = _rope_tables(seq)
        per_seq = seq // tile
        in_specs += [pl.BlockSpec((tile, LANES), lambda i: (i % per_seq, 0)),
                     pl.BlockSpec((tile, LANES), lambda i: (i % per_seq, 0))]
        args += [cos, sin]
    return pl.pallas_call(
        functools.partial(_qkprep_kernel, rope=rope),
        grid=(m // tile,),
        in_specs=in_specs,
        out_specs=[pl.BlockSpec((tile, nq), lambda i: (i, 0)),
                   pl.BlockSpec((tile, nk), lambda i: (i, 0))],
        out_shape=[jax.ShapeDtypeStruct((m, nq), F32), jax.ShapeDtypeStruct((m, nk), F32)],
        compiler_params=_cparams("arbitrary"),
        name="qk_prep",
    )(*args)


def _dup_halves(x):
    lane = lax.broadcasted_iota(I32, x.shape, 1)
    sw = pltpu.roll(x, HEAD_DIM, axis=1)
    lo = lane < HEAD_DIM
    return jnp.where(lo, x, sw), jnp.where(lo, sw, x)


def _attend(q, k_all, v_all, sink_ref, mask):
    scale = HEAD_DIM ** -0.5
    nt = (((1,), (1,)), ((), ()))
    kk = [a.astype(BF16) for a in _dup_halves(k_all)]
    vv = [a.astype(BF16) for a in _dup_halves(v_all)]
    lane = lax.broadcasted_iota(I32, (q.shape[0], LANES), 1)
    lo = lane < HEAD_DIM
    tiles = []
    for t in range(q.shape[1] // LANES):
        qt = q[:, t * LANES:(t + 1) * LANES]
        g = (2 * t) // GQA_GROUP
        halves = []
        for hh in range(2):
            head = 2 * t + hh
            qm = jnp.where(lo if hh == 0 else ~lo, qt, 0.0).astype(BF16)
            s = lax.dot_general(qm, kk[g], nt, preferred_element_type=F32) * scale
            if mask is not None:
                s = jnp.where(mask, s, NEG_BIG)
            sink = sink_ref[head]
            mx = jnp.maximum(jnp.max(s, axis=1, keepdims=True), sink)
            p = jnp.exp(s - mx)
            den = jnp.sum(p, axis=1, keepdims=True) + jnp.exp(sink - mx)
            p = (p / den).astype(BF16)
            halves.append(jnp.dot(p, vv[g], preferred_element_type=F32))
        tiles.append(jnp.where(lo, halves[0], halves[1]))
    return jnp.concatenate(tiles, axis=1)


def _ctx_attn_kernel(sink_ref, q_ref, k_ref, v_ref, o_ref):
    o_ref[...] = _attend(q_ref[...], k_ref[...], v_ref[...], sink_ref, None)


def _ctx_attention(q, k, v, sink, seq):
    m, nq = q.shape
    nk = k.shape[1]
    return pl.pallas_call(
        _ctx_attn_kernel,
        grid=(m // seq,),
        in_specs=[pl.BlockSpec(memory_space=pltpu.SMEM),
                  pl.BlockSpec((seq, nq), lambda b: (b, 0)),
                  pl.BlockSpec((seq, nk), lambda b: (b, 0)),
                  pl.BlockSpec((seq, nk), lambda b: (b, 0))],
        out_specs=pl.BlockSpec((seq, nq), lambda b: (b, 0)),
        out_shape=jax.ShapeDtypeStruct((m, nq), F32),
        compiler_params=_cparams("arbitrary"),
        name="ctx_attention",
    )(sink, q, k, v)


def _lat_attn_kernel(sink_ref, q_ref, kc_ref, vc_ref, kp_ref, k0_ref, kn_ref, vp_ref, v0_ref, vn_ref,
                     o_ref, *, seq):
    qb = pl.program_id(1)
    blk = q_ref.shape[0]
    n_ctx = kc_ref.shape[1]
    k_all = jnp.concatenate([kc_ref[0], kp_ref[...], k0_ref[...], kn_ref[...]], axis=0)
    v_all = jnp.concatenate([vc_ref[0], vp_ref[...], v0_ref[...], vn_ref[...]], axis=0)
    tk = k_all.shape[0]
    qpos = qb * blk + lax.broadcasted_iota(I32, (blk, tk), 0)
    col = lax.broadcasted_iota(I32, (blk, tk), 1)
    kpos = (qb - 1) * blk + col - n_ctx
    local_ok = (jnp.abs(qpos - kpos) <= WINDOW) & (kpos >= 0) & (kpos < seq)
    mask = (col < n_ctx) | local_ok
    o_ref[...] = _attend(q_ref[...], k_all, v_all, sink_ref, mask)


def _lat_attention(q, k, v, k_ctx, v_ctx, sink, seq):
    m, nq = q.shape
    nk = k.shape[1]
    blk = WINDOW
    nb = seq // blk
    n_ctx = k_ctx.shape[1]
    last = m // blk - 1

    def kv_spec(shift):
        return pl.BlockSpec((blk, nk), lambda b, i: (jnp.clip(b * nb + i + shift, 0, last), 0))

    ctx_spec = pl.BlockSpec((1, n_ctx, nk), lambda b, i: (b, 0, 0))
    return pl.pallas_call(
        functools.partial(_lat_attn_kernel, seq=seq),
        grid=(m // seq, nb),
        in_specs=[pl.BlockSpec(memory_space=pltpu.SMEM),
                  pl.BlockSpec((blk, nq), lambda b, i: (b * nb + i, 0)),
                  ctx_spec, ctx_spec,
                  kv_spec(-1), kv_spec(0), kv_spec(1),
                  kv_spec(-1), kv_spec(0), kv_spec(1)],
        out_specs=pl.BlockSpec((blk, nq), lambda b, i: (b * nb + i, 0)),
        out_shape=jax.ShapeDtypeStruct((m, nq), F32),
        compiler_params=_cparams("arbitrary", "arbitrary"),
        name="lat_attention",
    )(sink, q, k_ctx, v_ctx, k, k, k, v, v, v)

SSD_BLOCK = 256
SSD_PAIRS = SSD_HEADS // 2
SSD_INNER = SSD_HEADS * SSD_HEAD_DIM
HALO = SUBLANES


def _softplus(x):
    return jnp.maximum(x, 0.0) + jnp.log1p(jnp.exp(-jnp.abs(x)))


def _silu(x):
    return x * jax.nn.sigmoid(x)


def _ssd_decays(dt_raw, bias, a_log):
    n = dt_raw.shape[0]
    dt = _softplus(dt_raw + bias)
    log_a = dt * (-jnp.exp(a_log))
    r = lax.broadcasted_iota(I32, (n, n), 0)
    c = lax.broadcasted_iota(I32, (n, n), 1)
    lower = jnp.where(c <= r, 1.0, 0.0).astype(F32)
    upper = jnp.where(r <= c, 1.0, 0.0).astype(F32)
    cum_col = jnp.dot(lower, log_a, precision=HIGHEST, preferred_element_type=F32)
    dt_row = dt.T
    la_row = log_a.T
    cum_row = jnp.dot(la_row, upper, precision=HIGHEST, preferred_element_type=F32)
    return dt, log_a, cum_col, dt_row, la_row, cum_row


def _ssd_scan_chunk(xs, bmat, cmat, w_of, q_scale_of, k_scale_of, carry_of, s_ref):
    nt = (((1,), (1,)), ((), ()))
    n = xs.shape[0]
    lane = lax.broadcasted_iota(I32, (n, LANES), 1)
    lo = lane < SSD_HEAD_DIM
    lane_s = lax.broadcasted_iota(I32, (D_STATE, LANES), 1)
    lo_s = lane_s < SSD_HEAD_DIM
    b_t = bmat.T
    cb16 = cmat.astype(BF16)
    ys = []
    for pair in range(SSD_PAIRS):
        g = (2 * pair) // (SSD_HEADS // SSD_GROUPS)
        in_g = (lane // D_STATE) == g
        cg = jnp.where(in_g, cmat, 0.0)
        cb = lax.dot_general(cg.astype(BF16), bmat.astype(BF16), nt, preferred_element_type=F32)
        x_pair = xs[:, pair * LANES:(pair + 1) * LANES]
        x16 = x_pair.astype(BF16)
        s_old = s_ref[pair]
        s2 = jnp.concatenate([s_old, s_old], axis=0).astype(BF16)
        bg_t = b_t[g * D_STATE:(g + 1) * D_STATE, :]
        y_h, s_h = [], []
        for hh in range(2):
            h = 2 * pair + hh
            w = (cb * w_of(h)).astype(BF16)
            y = jnp.dot(w, x16, preferred_element_type=F32)
            cq = (cg * q_scale_of(h)).astype(BF16)
            y = y + jnp.dot(cq, s2, preferred_element_type=F32)
            y_h.append(y)
            kt = (bg_t * k_scale_of(h)).astype(BF16)
            s_h.append(carry_of(h) * s_old + jnp.dot(kt, x16, preferred_element_type=F32))
        ys.append(jnp.where(lo, y_h[0], y_h[1]))
        s_ref[pair] = jnp.where(lo_s, s_h[0], s_h[1])
    return jnp.concatenate(ys, axis=1)


def _ssd_fwd_kernel(x_ref, xp_ref, xn_ref, dt_ref, s0_ref, cw_ref, cb_ref, bias_ref, alog_ref,
                    y_ref, xc_ref, sfin_ref, s_ref):
    c = pl.program_id(1)
    nc = pl.num_programs(1)
    n = x_ref.shape[0]

    @pl.when(c == 0)
    def _():
        s_ref[...] = s0_ref[0]

    prev = jnp.where(c > 0, xp_ref[...], 0.0)
    nxt = jnp.where(c < nc - 1, xn_ref[...], 0.0)
    xe = jnp.concatenate([prev, x_ref[...], nxt], axis=0)
    pad = (CONV_K - 1) // 2
    acc = cb_ref[...] + cw_ref[0:1, :] * xe[HALO - pad:HALO - pad + n, :]
    for k in range(1, CONV_K):
        acc = acc + cw_ref[k:k + 1, :] * xe[HALO - pad + k:HALO - pad + k + n, :]
    xc = _silu(acc)
    xc_ref[...] = xc
    xs = xc[:, :SSD_INNER]
    bmat = xc[:, SSD_INNER:SSD_INNER + LANES]
    cmat = xc[:, SSD_INNER + LANES:SSD_INNER + 2 * LANES]

    dt, log_a, cum_col, dt_row, la_row, cum_row = _ssd_decays(dt_ref[...], bias_ref[...], alog_ref[...])
    r = lax.broadcasted_iota(I32, (n, n), 0)
    cc = lax.broadcasted_iota(I32, (n, n), 1)
    causal = cc <= r
    last_col = cum_col[n - 1:n, :]

    def w_of(h):
        seg = cum_col[:, h:h + 1] - cum_row[h:h + 1, :]
        return jnp.exp(jnp.where(causal, seg, NEG_BIG)) * dt_row[h:h + 1, :]

    def q_scale_of(h):
        return jnp.exp(cum_col[:, h:h + 1])

    def k_scale_of(h):
        return dt_row[h:h + 1, :] * jnp.exp(cum_row[h:h + 1, n - 1:n] - cum_row[h:h + 1, :])

    def carry_of(h):
        return jnp.exp(last_col[:, h:h + 1])

    y_ref[...] = _ssd_scan_chunk(xs, bmat, cmat, w_of, q_scale_of, k_scale_of, carry_of, s_ref)

    @pl.when(c == nc - 1)
    def _():
        sfin_ref[0] = s_ref[...]


def _ssd_bwd_kernel(xc_ref, dt_ref, yf_ref, z_ref, s0_ref, bias_ref, alog_ref, dskip_ref, nw_ref,
                    y_ref, sfin_ref, s_ref):
    c = pl.program_id(1)
    nc = pl.num_programs(1)
    n = xc_ref.shape[0]

    @pl.when(c == 0)
    def _():
        s_ref[...] = s0_ref[0]

    xc = xc_ref[...]
    xs = xc[:, :SSD_INNER]
    bmat = xc[:, SSD_INNER:SSD_INNER + LANES]
    cmat = xc[:, SSD_INNER + LANES:SSD_INNER + 2 * LANES]
    dt, log_a, cum_col, dt_row, la_row, cum_row = _ssd_decays(dt_ref[...], bias_ref[...], alog_ref[...])
    ex_col = cum_col - log_a
    ex_row = cum_row - la_row
    r = lax.broadcasted_iota(I32, (n, n), 0)
    cc = lax.broadcasted_iota(I32, (n, n), 1)
    anti = cc >= r
    tot_col = cum_col[n - 1:n, :]
    off = SSD_HEADS

    def w_of(h):
        j = off + h
        seg = ex_row[j:j + 1, :] - ex_col[:, j:j + 1]
        return jnp.exp(jnp.where(anti, seg, NEG_BIG)) * dt_row[j:j + 1, :]

    def q_scale_of(h):
        j = off + h
        return jnp.exp(tot_col[:, j:j + 1] - ex_col[:, j:j + 1])

    def k_scale_of(h):
        j = off + h
        return dt_row[j:j + 1, :] * jnp.exp(ex_row[j:j + 1, :])

    def carry_of(h):
        j = off + h
        return jnp.exp(tot_col[:, j:j + 1])

    y_b = _ssd_scan_chunk(xs, bmat, cmat, w_of, q_scale_of, k_scale_of, carry_of, s_ref)
    y = yf_ref[...] + y_b + dskip_ref[...] * xs
    y = y * _silu(z_ref[...])
    ms = jnp.mean(y * y, axis=-1, keepdims=True)
    y_ref[...] = y * lax.rsqrt(ms + NORM_EPS) * nw_ref[...]

    @pl.when(c == nc - 1)
    def _():
        sfin_ref[0] = s_ref[...]


def _pair_states(s):
    b, h, n, p = s.shape
    return s.reshape(b, h // 2, 2, n, p).transpose(0, 1, 3, 2, 4).reshape(b, h // 2, n, 2 * p)


def _unpair_states(s):
    b, hp, n, p2 = s.shape
    return s.reshape(b, hp, n, 2, p2 // 2).transpose(0, 1, 3, 2, 4).reshape(b, hp * 2, n, p2 // 2)


def _ssd(xbc, dt, z, s0_f, s0_b, conv_w, conv_b, dt_bias, a_log, d_skip, ssd_norm, seq):
    m, nx = xbc.shape
    nb = m // seq
    blk = min(SSD_BLOCK, seq)
    nc = seq // blk
    hb = blk // HALO
    n_halo = m // HALO
    pad16 = lambda a: jnp.pad(a.reshape(1, -1), ((0, 0), (0, LANES - a.size)))
    bias = pad16(dt_bias)
    alog = pad16(a_log)
    state_spec = pl.BlockSpec((1, SSD_PAIRS, D_STATE, LANES), lambda b, c: (b, 0, 0, 0))
    state_shape = jax.ShapeDtypeStruct((nb, SSD_PAIRS, D_STATE, LANES), F32)
    row = lambda width: pl.BlockSpec((1, width), lambda b, c: (0, 0))

    def fwd_rows(width):
        return pl.BlockSpec((blk, width), lambda b, c: (b * nc + c, 0))

    def bwd_rows(width):
        return pl.BlockSpec((blk, width), lambda b, c: (b * nc + nc - 1 - c, 0))

    y_f, xc, s_f = pl.pallas_call(
        _ssd_fwd_kernel,
        grid=(nb, nc),
        in_specs=[fwd_rows(nx),
                  pl.BlockSpec((HALO, nx), lambda b, c: (jnp.maximum((b * nc + c) * hb - 1, 0), 0)),
                  pl.BlockSpec((HALO, nx), lambda b, c: (jnp.minimum((b * nc + c + 1) * hb, n_halo - 1), 0)),
                  fwd_rows(LANES), state_spec,
                  pl.BlockSpec((CONV_K, nx), lambda b, c: (0, 0)), row(nx), row(LANES), row(LANES)],
        out_specs=[fwd_rows(SSD_INNER), fwd_rows(nx), state_spec],
        out_shape=[jax.ShapeDtypeStruct((m, SSD_INNER), F32), jax.ShapeDtypeStruct((m, nx), F32), state_shape],
        scratch_shapes=[pltpu.VMEM((SSD_PAIRS, D_STATE, LANES), F32)],
        compiler_params=_cparams("arbitrary", "arbitrary"),
        name="ssd_forward",
    )(xbc, xbc, xbc, dt, _pair_states(s0_f), conv_w, conv_b.reshape(1, nx), bias, alog)

    dskip = jnp.repeat(d_skip, SSD_HEAD_DIM).reshape(1, SSD_INNER)
    y, s_b = pl.pallas_call(
        _ssd_bwd_kernel,
        grid=(nb, nc),
        in_specs=[bwd_rows(nx), bwd_rows(LANES), bwd_rows(SSD_INNER), bwd_rows(SSD_INNER), state_spec,
                  row(LANES), row(LANES), row(SSD_INNER), row(SSD_INNER)],
        out_specs=[bwd_rows(SSD_INNER), state_spec],
        out_shape=[jax.ShapeDtypeStruct((m, SSD_INNER), F32), state_shape],
        scratch_shapes=[pltpu.VMEM((SSD_PAIRS, D_STATE, LANES), F32)],
        compiler_params=_cparams("arbitrary", "arbitrary"),
        name="ssd_backward",
    )(xc, dt, y_f, z, _pair_states(s0_b), bias, alog, dskip, ssd_norm.reshape(1, SSD_INNER))
    return y, _unpair_states(s_f), _unpair_states(s_b)

def _hgrn_kernel(q_ref, ff_ref, fb_ref, i_ref, g_ref, lb_ref, s0_ref, nw_ref, o_ref, sfin_ref,
                 sf_ref, sb_ref, ob_ref, *, layer):
    t_len = q_ref.shape[0]
    n = HGRN_CHUNK
    n_chunks = t_len // n
    tn = (((0,), (0,)), ((), ()))
    nt = (((1,), (1,)), ((), ()))

    lbp = lb_ref[...]
    e = jnp.exp(lbp - jnp.max(lbp, axis=0, keepdims=True))
    sm = e / jnp.sum(e, axis=0, keepdims=True)
    lb = sm[0] * 0.0
    for j in range(1, layer + 1):
        lb = lb + sm[j]

    r = lax.broadcasted_iota(I32, (n, n), 0)
    c = lax.broadcasted_iota(I32, (n, n), 1)
    lower = jnp.where(c <= r, 1.0, 0.0).astype(F32)
    srow = lax.broadcasted_iota(I32, (n, HGRN_DK), 0)
    qscale = HGRN_DK ** -0.5

    def chunk(row0, f_ref, lb_d, reverse, s_ref):
        q = _silu(q_ref[pl.ds(row0, n), :]) * qscale
        f = f_ref[pl.ds(row0, n), :]
        v = i_ref[pl.ds(row0, n), :]
        k = (1.0 - lb_d) * jax.nn.sigmoid(-f)
        lf = jnp.log(lb_d + (1.0 - lb_d) * jax.nn.sigmoid(f))
        cum = jnp.dot(lower, lf, precision=HIGHEST, preferred_element_type=F32)
        tot = cum[n - 1:n, :]
        if reverse:
            cum = cum - lf
        rows = []
        for t in range(n):
            tile0 = (t // SUBLANES) * SUBLANES
            lo, hi = (tile0, n) if reverse else (0, tile0 + SUBLANES)
            cum_s = cum[lo:hi]
            if reverse:
                seg = jnp.where(srow[lo:hi] >= t, cum_s - cum[t:t + 1, :], NEG_BIG)
            else:
                seg = jnp.where(srow[lo:hi] <= t, cum[t:t + 1, :] - cum_s, NEG_BIG)
            a = q[t:t + 1, :] * k[lo:hi] * jnp.exp(seg)
            sc = jnp.sum(a, axis=1, keepdims=True)
            rows.append(jnp.sum(sc * v[lo:hi], axis=0, keepdims=True))
        o = jnp.concatenate(rows, axis=0)
        s_old = s_ref[...]
        if reverse:
            q_in = q * jnp.exp(tot - cum)
            k_out = k * jnp.exp(cum)
        else:
            q_in = q * jnp.exp(cum)
            k_out = k * jnp.exp(tot - cum)
        o = o + lax.dot_general(q_in.astype(BF16), s_old.astype(BF16), nt, preferred_element_type=F32)
        s_ref[...] = jnp.exp(tot) * s_old + lax.dot_general(
            v.astype(BF16), k_out.astype(BF16), tn, preferred_element_type=F32)
        return o

    sf_ref[...] = s0_ref[0, 0, 0].T
    sb_ref[...] = s0_ref[0, 1, 0].T

    def body(ci, carry):
        row_f = pl.multiple_of(ci * n, n)
        row_b = pl.multiple_of((n_chunks - 1 - ci) * n, n)
        o_ref[pl.ds(row_f, n), :] = chunk(row_f, ff_ref, lb[0:1, :], False, sf_ref)
        ob_ref[pl.ds(row_b, n), :] = chunk(row_b, fb_ref, lb[1:2, :], True, sb_ref)
        return carry

    lax.fori_loop(0, n_chunks, body, 0)
    sfin_ref[0, 0, 0] = sf_ref[...].T
    sfin_ref[0, 1, 0] = sb_ref[...].T

    nw = nw_ref[...]
    blk = min(t_len, ROW_TILE)

    def finish(bi, carry):
        row0 = pl.multiple_of(bi * blk, blk)
        o = o_ref[pl.ds(row0, blk), :] + ob_ref[pl.ds(row0, blk), :]
        ms = jnp.mean(o * o, axis=-1, keepdims=True)
        o = o * lax.rsqrt(ms + NORM_EPS) * nw
        o_ref[pl.ds(row0, blk), :] = o * _silu(g_ref[pl.ds(row0, blk), :])
        return carry

    lax.fori_loop(0, t_len // blk, finish, 0)


def _hgrn(q, f_fw, f_bw, iv, g, o_lb, state0, g_norm, seq, layer):
    m, width = q.shape
    nb = m // seq
    dv = width // HGRN_HEADS
    col = pl.BlockSpec((seq, dv), lambda b, h: (b, h))
    state_spec = pl.BlockSpec((1, 2, 1, HGRN_DK, dv), lambda b, h: (b, 0, h, 0, 0))
    return pl.pallas_call(
        functools.partial(_hgrn_kernel, layer=layer),
        grid=(nb, HGRN_HEADS),
        in_specs=[col, col, col, col, col,
                  pl.BlockSpec((o_lb.shape[0], 2, HGRN_DK), lambda b, h: (0, 0, h)),
                  state_spec,
                  pl.BlockSpec((1, dv), lambda b, h: (0, 0))],
        out_specs=[col, state_spec],
        out_shape=[jax.ShapeDtypeStruct((m, width), F32),
                   jax.ShapeDtypeStruct((nb, 2, HGRN_HEADS, HGRN_DK, dv), F32)],
        scratch_shapes=[pltpu.VMEM((dv, HGRN_DK), F32), pltpu.VMEM((dv, HGRN_DK), F32),
                        pltpu.VMEM((seq, dv), F32)],
        compiler_params=_cparams("arbitrary", "arbitrary"),
        name="hgrn2",
    )(q, f_fw, f_bw, iv, g, o_lb, state0, g_norm.reshape(1, dv))

EVEN_SPLITS = ((0, 512), (512, 640), (640, 768), (768, 1280), (1280, 2048), (2048, 2176))
HGRN_SPLITS = tuple((i * 1024, (i + 1) * 1024) for i in range(5))


def _even_weight(w):
    main = EVEN_SPLITS[-1][0]
    return jnp.pad(w, ((0, 0), (0, LANES - (w.shape[1] - main)))).astype(BF16)


def _run_trunk(x3, mods, mod_row0, P, cache, sc_order=None):
    nb, seq, d = x3.shape
    x = x3.reshape(nb * seq, d)
    depth = P['norm_mix'].shape[0]
    ks, vs, ssd_states, hgrn_states = [], [], [], []
    for l in range(depth):
        j = l // 2
        row0 = (l * SUBLANES + mod_row0, 0 if cache is None else 1)
        if l % 2 == 0:
            q, k, v, z, xbc, dt = _inproj(x, mods, row0, seq, P['norm_mix'][l], P['e_w_in'][j],
                                          EVEN_SPLITS, "even_in_proj")
            q, k = _qkprep(q, k, P['e_q_norm'][j], P['e_k_norm'][j], seq, rope=cache is not None)
            if cache is None:
                s0_f = jnp.zeros((nb, SSD_HEADS, D_STATE, SSD_HEAD_DIM), F32)
                s0_b = s0_f
                o_attn = _ctx_attention(q, k, v, P['e_sink'][j], seq)
            else:
                s0_f, s0_b = cache[2][:, j, 0], cache[2][:, j, 1]
                n_ctx = cache[0].shape[2]
                o_attn = _lat_attention(q, k, v, cache[0][:, j].reshape(nb, n_ctx, -1),
                                        cache[1][:, j].reshape(nb, n_ctx, -1), P['e_sink'][j], seq)
            y, s_f, s_b = _ssd(xbc, dt, z, s0_f, s0_b, P['e_conv_w'][j], P['e_conv_b'][j],
                               P['e_dt_bias'][j], P['e_a_log'][j], P['e_d_skip'][j], P['e_ssd_norm'][j], seq)
            if cache is None:
                ks.append(k.reshape(nb, seq, N_KV_HEADS, HEAD_DIM))
                vs.append(v.reshape(nb, seq, N_KV_HEADS, HEAD_DIM))
                ssd_states.append(jnp.stack([s_f, s_b], axis=1))
            mix = jnp.concatenate([o_attn, y], axis=1)
            x = _outproj(mix, x, mods, row0, seq, P['e_w_out'][j], "even_out_proj")
        else:
            q, f_fw, f_bw, iv, g = _inproj(x, mods, row0, seq, P['norm_mix'][l], P['o_w_in'][j],
                                           HGRN_SPLITS, "odd_in_proj")
            if cache is None:
                s0 = jnp.zeros((nb, 2, HGRN_HEADS, HGRN_DK, d // HGRN_HEADS), F32)
            else:
                s0 = cache[3][:, j]
            o, s_new = _hgrn(q, f_fw, f_bw, iv, g, P['o_lb'], s0, P['o_g_norm'][j], seq, j)
            if cache is None:
                hgrn_states.append(s_new)
            x = _outproj(o, x, mods, row0, seq, P['o_w_out'][j], "odd_out_proj")
        x, y_sc = _peer(x, mods, row0, seq, l, P['norm_ffn'][l], P['p_w_q'][l], P['p_sub_keys'][l],
                        P['p_u'], P['p_v'],
                        (SC_SHARE_CONTEXT if cache is None else SC_SHARE_LATENT)[l],
                        after=None if sc_order is None or cache is None else sc_order[l])
        if sc_order is not None and cache is None:
            sc_order.append(y_sc)
    y = x.reshape(nb, seq, d)
    if cache is not None:
        return y, None
    return y, (jnp.stack(ks, axis=1), jnp.stack(vs, axis=1),
               jnp.stack(ssd_states, axis=1), jnp.stack(hgrn_states, axis=1))


def kernel(x_prompt, x_sample, cache_k, cache_v, state_ssd, state_hgrn, c, c_ctx, w_ada, b_ada, norm_mix, norm_ffn, e_w_in, e_q_norm, e_k_norm, e_sink, e_conv_w, e_conv_b, e_dt_bias, e_a_log, e_d_skip, e_ssd_norm, e_w_out, o_w_in, o_lb, o_g_norm, o_w_out, p_w_q, p_sub_keys, p_u, p_v):
    depth, d, d6 = w_ada.shape
    b_lat = x_sample.shape[0]
    cond_rows = jnp.concatenate([c_ctx[None, :], c, jnp.zeros((SUBLANES - 1 - b_lat, d), F32)], axis=0)
    mods = _modulation(cond_rows, w_ada, b_ada).reshape(depth * SUBLANES, 1, d6)
    P = {
        'norm_mix': norm_mix, 'norm_ffn': norm_ffn,
        'e_w_in': jnp.stack([_even_weight(w) for w in e_w_in]), 'e_q_norm': e_q_norm, 'e_k_norm': e_k_norm,
        'e_sink': e_sink, 'e_conv_w': e_conv_w, 'e_conv_b': e_conv_b, 'e_dt_bias': e_dt_bias,
        'e_a_log': e_a_log, 'e_d_skip': e_d_skip, 'e_ssd_norm': e_ssd_norm,
        'e_w_out': e_w_out.astype(BF16),
        'o_w_in': o_w_in.astype(BF16), 'o_lb': o_lb, 'o_g_norm': o_g_norm, 'o_w_out': o_w_out.astype(BF16),
        'p_w_q': p_w_q.astype(BF16),
        'p_sub_keys': p_sub_keys.astype(BF16).reshape(depth, PEER_HEADS * 2, PEER_NKEYS, PEER_DKEY),
        'p_u': p_u, 'p_v': p_v,
    }
    sc_order = []
    y_prompt, new_state = _run_trunk(x_prompt, mods, 0, P, None, sc_order)
    y_sample, _ = _run_trunk(x_sample, mods, 1, P, (cache_k, cache_v, state_ssd, state_hgrn), sc_order)
    return (y_prompt, y_sample) + new_state
```

```python
import functools
import math

import jax
import jax.numpy as jnp
from jax import lax
from jax.experimental import pallas as pl
from jax.experimental.pallas import tpu as pltpu
from jax.experimental.pallas import tpu_sc as plsc

F32 = jnp.float32
BF16 = jnp.bfloat16
I32 = jnp.int32
HIGHEST = lax.Precision.HIGHEST

NORM_EPS = 1e-6
NEG_BIG = -1e30
LANES = 128
SUBLANES = 8
VMEM_LIMIT = 48 * 1024 * 1024

GRID_W = 64
HEAD_DIM = 64
N_Q_HEADS = 8
N_KV_HEADS = 2
GQA_GROUP = 4
WINDOW = 128
ROPE_THETA = 10000.0
SSD_HEADS = 8
SSD_HEAD_DIM = 64
SSD_GROUPS = 2
D_STATE = 64
CONV_K = 5
HGRN_HEADS = 8
HGRN_DK = 128
HGRN_CHUNK = 32
PEER_HEADS = 8
PEER_NKEYS = 128
PEER_TOPK = 16
PEER_DKEY = 128
PEER_PAIRS = PEER_HEADS * PEER_TOPK

ROW_TILE = 256
PEER_BLOCK = 128
SC_LANES = 16
SC_CHUNK = 16
SC_SLOTS = 3
SC_SHARE_CONTEXT = ((1, 1), (1, 1), (1, 1), (1, 1))
SC_SHARE_LATENT = ((13, 16), (13, 16), (13, 16), (3, 4))
GELU_C = math.sqrt(2.0 / math.pi)


def _cparams(*sem):
    return pltpu.CompilerParams(dimension_semantics=sem, vmem_limit_bytes=VMEM_LIMIT)


def _norm_mod(x, nw, scale, shift):
    ms = jnp.mean(x * x, axis=-1, keepdims=True)
    return (x * lax.rsqrt(ms + NORM_EPS)) * nw * (1.0 + scale) + shift


def _mod_kernel(c_ref, w_ref, b_ref, o_ref):
    c = c_ref[...]
    s = c * jax.nn.sigmoid(c)
    o_ref[0] = jnp.dot(s, w_ref[0], precision=HIGHEST, preferred_element_type=F32) + b_ref[0]


def _modulation(cond_rows, w_ada, b_ada):
    depth, d, n = w_ada.shape
    rows = cond_rows.shape[0]
    return pl.pallas_call(
        _mod_kernel,
        grid=(depth, n // d),
        in_specs=[pl.BlockSpec((rows, d), lambda l, j: (0, 0)),
                  pl.BlockSpec((1, d, d), lambda l, j: (l, 0, j)),
                  pl.BlockSpec((1, 1, d), lambda l, j: (l, 0, j))],
        out_specs=pl.BlockSpec((1, rows, d), lambda l, j: (l, 0, j)),
        out_shape=jax.ShapeDtypeStruct((depth, rows, n), F32),
        compiler_params=_cparams("arbitrary", "arbitrary"),
        name="modulation",
    )(cond_rows, w_ada, b_ada.reshape(depth, 1, n))


def _mod_spec(mod_row0, seq, tile, d6, first_tile=0):
    row0, per_batch = mod_row0
    return pl.BlockSpec((1, 1, d6),
                        lambda i: (row0 + per_batch * (((i + first_tile) * tile) // seq), 0, 0))


def _inproj_kernel(x_ref, m_ref, nw_ref, w_ref, *o_refs, splits, d):
    m = m_ref[0]
    h = _norm_mod(x_ref[...], nw_ref[...], m[:, d:2 * d], m[:, 0:d]).astype(BF16)
    for o_ref, (a, b) in zip(o_refs, splits):
        o_ref[...] = jnp.dot(h, w_ref[:, a:b], preferred_element_type=F32)


def _inproj(x, mods, mod_row0, seq, nw, w_bf16, splits, name):
    m, d = x.shape
    n = w_bf16.shape[1]
    tile = min(ROW_TILE, seq)
    return pl.pallas_call(
        functools.partial(_inproj_kernel, splits=splits, d=d),
        grid=(m // tile,),
        in_specs=[pl.BlockSpec((tile, d), lambda i: (i, 0)),
                  _mod_spec(mod_row0, seq, tile, mods.shape[-1]),
                  pl.BlockSpec((1, d), lambda i: (0, 0)),
                  pl.BlockSpec((d, n), lambda i: (0, 0))],
        out_specs=[pl.BlockSpec((tile, b - a), lambda i: (i, 0)) for a, b in splits],
        out_shape=[jax.ShapeDtypeStruct((m, b - a), F32) for a, b in splits],
        compiler_params=_cparams("arbitrary"),
        name=name,
    )(x, mods, nw.reshape(1, d), w_bf16)


def _outproj_kernel(mix_ref, x_ref, m_ref, w_ref, o_ref, *, d):
    y = jnp.dot(mix_ref[...].astype(BF16), w_ref[...], preferred_element_type=F32)
    o_ref[...] = x_ref[...] + m_ref[0][:, 2 * d:3 * d] * y


def _outproj(mix, x, mods, mod_row0, seq, w_bf16, name):
    m, d = x.shape
    k = mix.shape[1]
    tile = min(ROW_TILE, seq)
    return pl.pallas_call(
        functools.partial(_outproj_kernel, d=d),
        grid=(m // tile,),
        in_specs=[pl.BlockSpec((tile, k), lambda i: (i, 0)),
                  pl.BlockSpec((tile, d), lambda i: (i, 0)),
                  _mod_spec(mod_row0, seq, tile, mods.shape[-1]),
                  pl.BlockSpec((k, d), lambda i: (0, 0))],
        out_specs=pl.BlockSpec((tile, d), lambda i: (i, 0)),
        out_shape=jax.ShapeDtypeStruct((m, d), F32),
        compiler_params=_cparams("arbitrary"),
        name=name,
    )(mix, x, mods, w_bf16)


def _topk_over_rows(s, k, payload=None):
    n = s.shape[0]
    iota = lax.broadcasted_iota(I32, s.shape, 0)
    vals, idxs, pays = [], [], []
    for _ in range(k):
        m = jnp.max(s, axis=0, keepdims=True)
        i = jnp.min(jnp.where(s == m, iota, n), axis=0, keepdims=True)
        hit = iota == i
        vals.append(m)
        idxs.append(i)
        if payload is not None:
            pays.append(jnp.max(jnp.where(hit, payload, -1), axis=0, keepdims=True))
        s = jnp.where(hit, -jnp.inf, s)
    out = (jnp.concatenate(vals, axis=0), jnp.concatenate(idxs, axis=0))
    if payload is not None:
        out += (jnp.concatenate(pays, axis=0),)
    return out


def _peer_route_kernel(x_ref, m_ref, nw_ref, wq_ref, keys_ref, h_ref, e_ref, g_ref, *, d):
    m = m_ref[0]
    h = _norm_mod(x_ref[...], nw_ref[...], m[:, 4 * d:5 * d], m[:, 3 * d:4 * d])
    h_ref[...] = h
    hb = h.astype(BF16)
    nt = (((1,), (1,)), ((), ()))
    for head in range(PEER_HEADS):
        tops = []
        for half in range(2):
            c0 = (head * 2 + half) * PEER_DKEY
            q = jnp.dot(hb, wq_ref[:, c0:c0 + PEER_DKEY], preferred_element_type=F32)
            s = lax.dot_general(keys_ref[head * 2 + half], q.astype(BF16), nt,
                                preferred_element_type=F32)
            tops.append(_topk_over_rows(s, PEER_TOPK))
        (s0, i0), (s1, i1) = tops
        widths = [PEER_TOPK // (a + 1) for a in range(PEER_TOPK)]
        n_pad = -sum(widths) % SUBLANES
        cand_s = jnp.concatenate([s0[a:a + 1] + s1[:w] for a, w in enumerate(widths)]
                                 + [jnp.full((n_pad, s0.shape[1]), -jnp.inf, F32)], axis=0)
        cand_e = jnp.concatenate([i0[a:a + 1] * PEER_NKEYS + i1[:w] for a, w in enumerate(widths)]
                                 + [jnp.zeros((n_pad, s0.shape[1]), I32)], axis=0)
        best_s, _, best_e = _topk_over_rows(cand_s, PEER_TOPK, payload=cand_e)
        p = jnp.exp(best_s - best_s[0:1])
        r0 = head * PEER_TOPK
        e_ref[r0:r0 + PEER_TOPK, :] = best_e
        g_ref[r0:r0 + PEER_TOPK, :] = p / jnp.sum(p, axis=0, keepdims=True)


def _peer_route(x, mods, mod_row0, seq, nw, wq_bf16, keys_bf16):
    m, d = x.shape
    tile = min(ROW_TILE, seq)
    nq = wq_bf16.shape[1]
    return pl.pallas_call(
        functools.partial(_peer_route_kernel, d=d),
        grid=(m // tile,),
        in_specs=[pl.BlockSpec((tile, d), lambda i: (i, 0)),
                  _mod_spec(mod_row0, seq, tile, mods.shape[-1]),
                  pl.BlockSpec((1, d), lambda i: (0, 0)),
                  pl.BlockSpec((d, nq), lambda i: (0, 0)),
                  pl.BlockSpec(keys_bf16.shape, lambda i: (0, 0, 0))],
        out_specs=[pl.BlockSpec((tile, d), lambda i: (i, 0)),
                   pl.BlockSpec((PEER_PAIRS, tile), lambda i: (0, i)),
                   pl.BlockSpec((PEER_PAIRS, tile), lambda i: (0, i))],
        out_shape=[jax.ShapeDtypeStruct((m, d), F32),
                   jax.ShapeDtypeStruct((PEER_PAIRS, m), I32),
                   jax.ShapeDtypeStruct((PEER_PAIRS, m), F32)],
        compiler_params=_cparams("arbitrary"),
        name="peer_route",
    )(x, mods, nw.reshape(1, d), wq_bf16, keys_bf16)


def _peer_gather_kernel(idx_hbm, h_ref, g_ref, x_ref, m_ref, u_hbm, v_hbm, o_ref,
                        idx_smem, ubuf, vbuf, sem_idx, sem_u, sem_v, *, layer, d):
    blk = pl.program_id(0)
    n_groups = PEER_BLOCK // SUBLANES

    cp = pltpu.make_async_copy(idx_hbm.at[blk], idx_smem, sem_idx)
    cp.start()
    cp.wait()

    def issue(tok, slot):
        def body(pair, carry):
            e = idx_smem[tok, pair]
            pltpu.make_async_copy(u_hbm.at[layer, pl.ds(e, 1)], ubuf.at[slot, pl.ds(pair, 1)],
                                  sem_u.at[slot]).start()
            pltpu.make_async_copy(v_hbm.at[layer, pl.ds(e, 1)], vbuf.at[slot, pl.ds(pair, 1)],
                                  sem_v.at[slot]).start()
            return carry
        lax.fori_loop(0, PEER_PAIRS, body, 0, unroll=8)

    def wait(slot):
        pltpu.make_async_copy(u_hbm.at[layer, pl.ds(0, PEER_PAIRS)], ubuf.at[slot], sem_u.at[slot]).wait()
        pltpu.make_async_copy(v_hbm.at[layer, pl.ds(0, PEER_PAIRS)], vbuf.at[slot], sem_v.at[slot]).wait()

    gate2 = m_ref[0][:, 5 * d:6 * d]
    lane = lax.broadcasted_iota(I32, (PEER_PAIRS, PEER_BLOCK), 1)

    issue(0, 0)

    def group(grp, carry):
        base = pl.multiple_of(grp * SUBLANES, SUBLANES)
        h8 = h_ref[pl.ds(base, SUBLANES), :]
        rows = []
        for r in range(SUBLANES):
            tok = base + r
            slot = r % 2
            if r < SUBLANES - 1:
                issue(tok + 1, 1 - slot)
            else:
                @pl.when(grp < n_groups - 1)
                def _():
                    issue(tok + 1, 1 - slot)
            wait(slot)
            act = jnp.sum(ubuf[slot] * h8[r:r + 1, :], axis=1, keepdims=True)
            gate = jnp.sum(jnp.where(lane == tok, g_ref[...], 0.0), axis=1, keepdims=True)
            w = jax.nn.gelu(act) * gate
            rows.append(jnp.sum(vbuf[slot] * w, axis=0, keepdims=True))
        out8 = jnp.concatenate(rows, axis=0)
        o_ref[pl.ds(base, SUBLANES), :] = x_ref[pl.ds(base, SUBLANES), :] + gate2 * out8
        return carry

    lax.fori_loop(0, n_groups, group, 0)


def _peer_gather(idx, h, gates, x, mods, mod_row0, seq, p_u, p_v, layer, first_block):
    m, d = x.shape
    nblk = m // PEER_BLOCK - first_block
    idx3 = idx[:, first_block * PEER_BLOCK:].T.reshape(nblk, PEER_BLOCK, PEER_PAIRS)
    rows = pl.BlockSpec((PEER_BLOCK, d), lambda i: (i + first_block, 0))
    return pl.pallas_call(
        functools.partial(_peer_gather_kernel, layer=layer, d=d),
        grid=(nblk,),
        in_specs=[pl.BlockSpec(memory_space=pl.ANY),
                  rows,
                  pl.BlockSpec((PEER_PAIRS, PEER_BLOCK), lambda i: (0, i + first_block)),
                  rows,
                  _mod_spec(mod_row0, seq, PEER_BLOCK, mods.shape[-1], first_block),
                  pl.BlockSpec(memory_space=pl.ANY),
                  pl.BlockSpec(memory_space=pl.ANY)],
        out_specs=pl.BlockSpec((PEER_BLOCK, d), lambda i: (i, 0)),
        out_shape=jax.ShapeDtypeStruct((nblk * PEER_BLOCK, d), F32),
        scratch_shapes=[pltpu.SMEM((PEER_BLOCK, PEER_PAIRS), I32),
                        pltpu.VMEM((2, PEER_PAIRS, d), F32),
                        pltpu.VMEM((2, PEER_PAIRS, d), F32),
                        pltpu.SemaphoreType.DMA,
                        pltpu.SemaphoreType.DMA((2,)),
                        pltpu.SemaphoreType.DMA((2,))],
        compiler_params=_cparams("arbitrary"),
        name="peer_gather",
    )(idx3, h, gates, x, mods, p_u, p_v)


def _peer_experts_sc(idx, h, gates, u_rows, v_rows):
    m = idx.shape[0]
    d = h.shape[1]
    info = plsc.get_sparse_core_info()
    n_workers = info.num_cores * info.num_subcores
    per = m // n_workers
    n_chunks = PEER_PAIRS // SC_CHUNK
    n_vec = d // SC_LANES
    mesh = plsc.VectorSubcoreMesh(core_axis_name="c", subcore_axis_name="s")

    @functools.partial(
        pl.kernel, out_type=jax.ShapeDtypeStruct((m, d), F32), mesh=mesh,
        scratch_types=[pltpu.VMEM((2, PEER_PAIRS), I32), pltpu.VMEM((2, d), F32),
                       pltpu.VMEM((2, PEER_PAIRS), F32), pltpu.VMEM((2, d), F32),
                       pltpu.VMEM((SC_SLOTS, SC_CHUNK, d), F32), pltpu.VMEM((SC_SLOTS, SC_CHUNK, d), F32),
                       pltpu.SemaphoreType.DMA((2,)), pltpu.SemaphoreType.DMA((2,)),
                       pltpu.SemaphoreType.DMA((SC_SLOTS,)), pltpu.SemaphoreType.DMA((SC_SLOTS,))],
        compiler_params=pltpu.CompilerParams(needs_layout_passes=False),
        name="peer_experts_sc")
    def body(idx_hbm, h_hbm, g_hbm, u_hbm, v_hbm, o_hbm,
             idx_v, x_v, g_v, out_v, ubuf, vbuf, sem_meta, sem_out, sem_u, sem_v):
        wid = lax.axis_index("c") * info.num_subcores + lax.axis_index("s")
        tok0 = wid * per
        lane = lax.iota(I32, SC_LANES)

        def meta_copies(ti, ms):
            t = tok0 + ti
            return (pltpu.make_async_copy(idx_hbm.at[t], idx_v.at[ms], sem_meta.at[ms]),
                    pltpu.make_async_copy(h_hbm.at[t], x_v.at[ms], sem_meta.at[ms]),
                    pltpu.make_async_copy(g_hbm.at[t], g_v.at[ms], sem_meta.at[ms]))

        def gather_copies(ms, c, slot):
            ids = idx_v.at[ms, pl.ds(c * SC_CHUNK, SC_CHUNK)]
            return (pltpu.make_async_copy(u_hbm.at[ids], ubuf.at[slot], sem_u.at[slot]),
                    pltpu.make_async_copy(v_hbm.at[ids], vbuf.at[slot], sem_v.at[slot]))

        def out_copy(ti, ms):
            return pltpu.make_async_copy(out_v.at[ms], o_hbm.at[tok0 + ti], sem_out.at[ms])

        for cp in meta_copies(0, 0):
            cp.start()
        for cp in meta_copies(0, 0):
            cp.wait()
        for c in range(SC_SLOTS - 1):
            for cp in gather_copies(0, c, c):
                cp.start()

        def token(ti, carry):
            ms = ti % 2
            nxt = 1 - ms

            @pl.when(ti + 1 < per)
            def _():
                for cp in meta_copies(ti + 1, nxt):
                    cp.start()

            @pl.when(ti >= 2)
            def _():
                out_copy(ti - 2, ms).wait()

            def zero(j, c):
                out_v[ms, pl.ds(j * SC_LANES, SC_LANES)] = jnp.zeros((SC_LANES,), F32)
                return c
            lax.fori_loop(0, n_vec, zero, 0)

            for c in range(n_chunks):
                g = ti * n_chunks + c
                slot = lax.rem(g, SC_SLOTS)
                ahead = c + SC_SLOTS - 1
                ahead_slot = lax.rem(g + SC_SLOTS - 1, SC_SLOTS)
                if ahead < n_chunks:
                    for cp in gather_copies(ms, ahead, ahead_slot):
                        cp.start()
                else:
                    @pl.when(ti + 1 < per)
                    def _():
                        if ahead == n_chunks:
                            for cp in meta_copies(ti + 1, nxt):
                                cp.wait()
                        for cp in gather_copies(nxt, ahead - n_chunks, ahead_slot):
                            cp.start()
                cu, cv = gather_copies(ms, c, slot)
                cu.wait()

                def udot(j, accs):
                    xj = x_v[ms, pl.ds(j * SC_LANES, SC_LANES)]
                    return tuple(accs[r] + ubuf[slot, r, pl.ds(j * SC_LANES, SC_LANES)] * xj
                                 for r in range(SC_CHUNK))
                accs = lax.fori_loop(0, n_vec, udot,
                                     tuple(jnp.zeros((SC_LANES,), F32) for _ in range(SC_CHUNK)))
                act = jnp.zeros((SC_LANES,), F32)
                for r in range(SC_CHUNK):
                    act = jnp.where(lane == r, jnp.sum(accs[r]), act)
                y = GELU_C * (act + 0.044715 * (act * act * act))
                w = act / (1.0 + jnp.exp(-2.0 * y)) * g_v[ms, pl.ds(c * SC_CHUNK, SC_CHUNK)]
                ws = [jnp.sum(jnp.where(lane == r, w, 0.0)) for r in range(SC_CHUNK)]
                cv.wait()

                @plsc.parallel_loop(0, n_vec, unroll=2)
                def _(j):
                    parts = [ws[r] * vbuf[slot, r, pl.ds(j * SC_LANES, SC_LANES)] for r in range(SC_CHUNK)]
                    while len(parts) > 1:
                        parts = [parts[i] + parts[i + 1] for i in range(0, len(parts), 2)]
                    plsc.addupdate(out_v.at[ms, pl.ds(j * SC_LANES, SC_LANES)], parts[0])

            out_copy(ti, ms).start()
            return carry

        lax.fori_loop(0, per, token, 0)
        for back in (2, 1):
            if per >= back:
                out_copy(per - back, (per - back) % 2).wait()

    return body(idx, h, gates, u_rows, v_rows)


def _residual_kernel(x_ref, y_ref, m_ref, o_ref, *, d):
    o_ref[...] = x_ref[...] + m_ref[0][:, 5 * d:6 * d] * y_ref[...]


def _residual(x, y, mods, mod_row0, seq):
    m, d = y.shape
    tile = min(ROW_TILE, seq)
    return pl.pallas_call(
        functools.partial(_residual_kernel, d=d),
        grid=(m // tile,),
        in_specs=[pl.BlockSpec((tile, d), lambda i: (i, 0)),
                  pl.BlockSpec((tile, d), lambda i: (i, 0)),
                  _mod_spec(mod_row0, seq, tile, mods.shape[-1])],
        out_specs=pl.BlockSpec((tile, d), lambda i: (i, 0)),
        out_shape=jax.ShapeDtypeStruct((m, d), F32),
        compiler_params=_cparams("arbitrary"),
        name="peer_residual",
    )(x, y, mods)


def _peer(x, mods, mod_row0, seq, layer, nw, wq_bf16, keys_bf16, p_u, p_v, sc_share, after=None):
    m, d = x.shape
    h, idx, gates = _peer_route(x, mods, mod_row0, seq, nw, wq_bf16, keys_bf16)
    n_blocks = m // PEER_BLOCK
    sc_blocks = (n_blocks * sc_share[0]) // sc_share[1]
    m_sc = sc_blocks * PEER_BLOCK
    n_experts = p_u.shape[1]
    idx_sc = idx[:, :m_sc].T + layer * n_experts
    if after is not None:
        idx_sc, _ = lax.optimization_barrier((idx_sc, after))
    y_sc = _peer_experts_sc(idx_sc, h, gates[:, :m_sc].T,
                            p_u.reshape(-1, d), p_v.reshape(-1, d))
    x_sc = _residual(x, y_sc, mods, mod_row0, seq)
    if sc_blocks == n_blocks:
        return x_sc, y_sc
    x_tc = _peer_gather(idx, h, gates, x, mods, mod_row0, seq, p_u, p_v, layer, sc_blocks)
    return jnp.concatenate([x_sc, x_tc], axis=0), y_sc


def _head_mean_square(x):
    n = x.shape[1]
    r = lax.broadcasted_iota(I32, (n, n), 0) // HEAD_DIM
    c = lax.broadcasted_iota(I32, (n, n), 1) // HEAD_DIM
    seg = jnp.where(r == c, 1.0 / HEAD_DIM, 0.0).astype(F32)
    return jnp.dot(x * x, seg, precision=HIGHEST, preferred_element_type=F32)


def _swap_rot_halves(x):
    n = x.shape[1]
    quarter = HEAD_DIM // 4
    lane = lax.broadcasted_iota(I32, x.shape, 1)
    lo = (lane % (2 * quarter)) < quarter
    return jnp.where(lo, pltpu.roll(x, n - quarter, axis=1), pltpu.roll(x, quarter, axis=1))


def _qkprep_kernel(q_ref, k_ref, qw_ref, kw_ref, *rest, rope):
    if rope:
        cos_ref, sin_ref, qo_ref, ko_ref = rest
    else:
        qo_ref, ko_ref = rest
    q = q_ref[...]
    k = k_ref[...]
    q = q * lax.rsqrt(_head_mean_square(q) + NORM_EPS) * qw_ref[...]
    k = k * lax.rsqrt(_head_mean_square(k) + NORM_EPS) * kw_ref[...]
    if rope:
        cos = cos_ref[...]
        sin = sin_ref[...]
        cq = jnp.concatenate([cos] * (q.shape[1] // LANES), axis=1)
        sq = jnp.concatenate([sin] * (q.shape[1] // LANES), axis=1)
        q = q * cq + _swap_rot_halves(q) * sq
        k = k * cos + _swap_rot_halves(k) * sin
    qo_ref[...] = q
    ko_ref[...] = k


def _rope_tables(seq):
    axis_dim = HEAD_DIM // 2
    inv_freq = ROPE_THETA ** (-jnp.arange(0, axis_dim, 2, dtype=F32) / axis_dim)
    t = jnp.arange(seq)
    pos = jnp.stack([(t // GRID_W).astype(F32), (t % GRID_W).astype(F32)], axis=1)
    lane = jnp.arange(LANES)
    dd = lane % HEAD_DIM
    ang = pos[:, dd // axis_dim] * inv_freq[dd % (axis_dim // 2)][None, :]
    sign = jnp.where((dd % axis_dim) < axis_dim // 2, -1.0, 1.0).astype(F32)
    return jnp.cos(ang), jnp.sin(ang) * sign[None, :]


def _qkprep(q, k, qw, kw, seq, rope):
    m, nq = q.shape
    nk = k.shape[1]
    tile = min(ROW_TILE, seq)
    qw_row = jnp.tile(qw, nq // HEAD_DIM).reshape(1, nq)
    kw_row = jnp.tile(kw, nk // HEAD_DIM).reshape(1, nk)
    in_specs = [pl.BlockSpec((tile, nq), lambda i: (i, 0)),
                pl.BlockSpec((tile, nk), lambda i: (i, 0)),
                pl.BlockSpec((1, nq), lambda i: (0, 0)),
                pl.BlockSpec((1, nk), lambda i: (0, 0))]
    args = [q, k, qw_row, kw_row]
    if rope:
        cos, sin = _rope_tables(seq)
        per_seq = seq // tile
        in_specs += [pl.BlockSpec((tile, LANES), lambda i: (i % per_seq, 0)),
                     pl.BlockSpec((tile, LANES), lambda i: (i % per_seq, 0))]
        args += [cos, sin]
    return pl.pallas_call(
        functools.partial(_qkprep_kernel, rope=rope),
        grid=(m // tile,),
        in_specs=in_specs,
        out_specs=[pl.BlockSpec((tile, nq), lambda i: (i, 0)),
                   pl.BlockSpec((tile, nk), lambda i: (i, 0))],
        out_shape=[jax.ShapeDtypeStruct((m, nq), F32), jax.ShapeDtypeStruct((m, nk), F32)],
        compiler_params=_cparams("arbitrary"),
        name="qk_prep",
    )(*args)


def _dup_halves(x):
    lane = lax.broadcasted_iota(I32, x.shape, 1)
    sw = pltpu.roll(x, HEAD_DIM, axis=1)
    lo = lane < HEAD_DIM
    return jnp.where(lo, x, sw), jnp.where(lo, sw, x)


def _attend(q, k_all, v_all, sink_ref, mask):
    scale = HEAD_DIM ** -0.5
    nt = (((1,), (1,)), ((), ()))
    kk = [a.astype(BF16) for a in _dup_halves(k_all)]
    vv = [a.astype(BF16) for a in _dup_halves(v_all)]
    lane = lax.broadcasted_iota(I32, (q.shape[0], LANES), 1)
    lo = lane < HEAD_DIM
    tiles = []
    for t in range(q.shape[1] // LANES):
        qt = q[:, t * LANES:(t + 1) * LANES]
        g = (2 * t) // GQA_GROUP
        halves = []
        for hh in range(2):
            head = 2 * t + hh
            qm = jnp.where(lo if hh == 0 else ~lo, qt, 0.0).astype(BF16)
            s = lax.dot_general(qm, kk[g], nt, preferred_element_type=F32) * scale
            if mask is not None:
                s = jnp.where(mask, s, NEG_BIG)
            sink = sink_ref[head]
            mx = jnp.maximum(jnp.max(s, axis=1, keepdims=True), sink)
            p = jnp.exp(s - mx)
            den = jnp.sum(p, axis=1, keepdims=True) + jnp.exp(sink - mx)
            p = (p / den).astype(BF16)
            halves.append(jnp.dot(p, vv[g], preferred_element_type=F32))
        tiles.append(jnp.where(lo, halves[0], halves[1]))
    return jnp.concatenate(tiles, axis=1)


def _ctx_attn_kernel(sink_ref, q_ref, k_ref, v_ref, o_ref):
    o_ref[...] = _attend(q_ref[...], k_ref[...], v_ref[...], sink_ref, None)


def _ctx_attention(q, k, v, sink, seq):
    m, nq = q.shape
    nk = k.shape[1]
    return pl.pallas_call(
        _ctx_attn_kernel,
        grid=(m // seq,),
        in_specs=[pl.BlockSpec(memory_space=pltpu.SMEM),
                  pl.BlockSpec((seq, nq), lambda b: (b, 0)),
                  pl.BlockSpec((seq, nk), lambda b: (b, 0)),
                  pl.BlockSpec((seq, nk), lambda b: (b, 0))],
        out_specs=pl.BlockSpec((seq, nq), lambda b: (b, 0)),
        out_shape=jax.ShapeDtypeStruct((m, nq), F32),
        compiler_params=_cparams("arbitrary"),
        name="ctx_attention",
    )(sink, q, k, v)


def _lat_attn_kernel(sink_ref, q_ref, kc_ref, vc_ref, kp_ref, k0_ref, kn_ref, vp_ref, v0_ref, vn_ref,
                     o_ref, *, seq):
    qb = pl.program_id(1)
    blk = q_ref.shape[0]
    n_ctx = kc_ref.shape[1]
    k_all = jnp.concatenate([kc_ref[0], kp_ref[...], k0_ref[...], kn_ref[...]], axis=0)
    v_all = jnp.concatenate([vc_ref[0], vp_ref[...], v0_ref[...], vn_ref[...]], axis=0)
    tk = k_all.shape[0]
    qpos = qb * blk + lax.broadcasted_iota(I32, (blk, tk), 0)
    col = lax.broadcasted_iota(I32, (blk, tk), 1)
    kpos = (qb - 1) * blk + col - n_ctx
    local_ok = (jnp.abs(qpos - kpos) <= WINDOW) & (kpos >= 0) & (kpos < seq)
    mask = (col < n_ctx) | local_ok
    o_ref[...] = _attend(q_ref[...], k_all, v_all, sink_ref, mask)


def _lat_attention(q, k, v, k_ctx, v_ctx, sink, seq):
    m, nq = q.shape
    nk = k.shape[1]
    blk = WINDOW
    nb = seq // blk
    n_ctx = k_ctx.shape[1]
    last = m // blk - 1

    def kv_spec(shift):
        return pl.BlockSpec((blk, nk), lambda b, i: (jnp.clip(b * nb + i + shift, 0, last), 0))

    ctx_spec = pl.BlockSpec((1, n_ctx, nk), lambda b, i: (b, 0, 0))
    return pl.pallas_call(
        functools.partial(_lat_attn_kernel, seq=seq),
        grid=(m // seq, nb),
        in_specs=[pl.BlockSpec(memory_space=pltpu.SMEM),
                  pl.BlockSpec((blk, nq), lambda b, i: (b * nb + i, 0)),
                  ctx_spec, ctx_spec,
                  kv_spec(-1), kv_spec(0), kv_spec(1),
                  kv_spec(-1), kv_spec(0), kv_spec(1)],
        out_specs=pl.BlockSpec((blk, nq), lambda b, i: (b * nb + i, 0)),
        out_shape=jax.ShapeDtypeStruct((m, nq), F32),
        compiler_params=_cparams("arbitrary", "arbitrary"),
        name="lat_attention",
    )(sink, q, k_ctx, v_ctx, k, k, k, v, v, v)

SSD_BLOCK = 256
SSD_PAIRS = SSD_HEADS // 2
SSD_INNER = SSD_HEADS * SSD_HEAD_DIM
HALO = SUBLANES


def _softplus(x):
    return jnp.maximum(x, 0.0) + jnp.log1p(jnp.exp(-jnp.abs(x)))


def _silu(x):
    return x * jax.nn.sigmoid(x)


def _ssd_decays(dt_raw, bias, a_log):
    n = dt_raw.shape[0]
    dt = _softplus(dt_raw + bias)
    log_a = dt * (-jnp.exp(a_log))
    r = lax.broadcasted_iota(I32, (n, n), 0)
    c = lax.broadcasted_iota(I32, (n, n), 1)
    lower = jnp.where(c <= r, 1.0, 0.0).astype(F32)
    upper = jnp.where(r <= c, 1.0, 0.0).astype(F32)
    cum_col = jnp.dot(lower, log_a, precision=HIGHEST, preferred_element_type=F32)
    dt_row = dt.T
    la_row = log_a.T
    cum_row = jnp.dot(la_row, upper, precision=HIGHEST, preferred_element_type=F32)
    return dt, log_a, cum_col, dt_row, la_row, cum_row


def _ssd_scan_chunk(xs, bmat, cmat, w_of, q_scale_of, k_scale_of, carry_of, s_ref):
    nt = (((1,), (1,)), ((), ()))
    n = xs.shape[0]
    lane = lax.broadcasted_iota(I32, (n, LANES), 1)
    lo = lane < SSD_HEAD_DIM
    lane_s = lax.broadcasted_iota(I32, (D_STATE, LANES), 1)
    lo_s = lane_s < SSD_HEAD_DIM
    b_t = bmat.T
    cb16 = cmat.astype(BF16)
    ys = []
    for pair in range(SSD_PAIRS):
        g = (2 * pair) // (SSD_HEADS // SSD_GROUPS)
        in_g = (lane // D_STATE) == g
        cg = jnp.where(in_g, cmat, 0.0)
        cb = lax.dot_general(cg.astype(BF16), bmat.astype(BF16), nt, preferred_element_type=F32)
        x_pair = xs[:, pair * LANES:(pair + 1) * LANES]
        x16 = x_pair.astype(BF16)
        s_old = s_ref[pair]
        s2 = jnp.concatenate([s_old, s_old], axis=0).astype(BF16)
        bg_t = b_t[g * D_STATE:(g + 1) * D_STATE, :]
        y_h, s_h = [], []
        for hh in range(2):
            h = 2 * pair + hh
            w = (cb * w_of(h)).astype(BF16)
            y = jnp.dot(w, x16, preferred_element_type=F32)
            cq = (cg * q_scale_of(h)).astype(BF16)
            y = y + jnp.dot(cq, s2, preferred_element_type=F32)
            y_h.append(y)
            kt = (bg_t * k_scale_of(h)).astype(BF16)
            s_h.append(carry_of(h) * s_old + jnp.dot(kt, x16, preferred_element_type=F32))
        ys.append(jnp.where(lo, y_h[0], y_h[1]))
        s_ref[pair] = jnp.where(lo_s, s_h[0], s_h[1])
    return jnp.concatenate(ys, axis=1)


def _ssd_fwd_kernel(x_ref, xp_ref, xn_ref, dt_ref, s0_ref, cw_ref, cb_ref, bias_ref, alog_ref,
                    y_ref, xc_ref, sfin_ref, s_ref):
    c = pl.program_id(1)
    nc = pl.num_programs(1)
    n = x_ref.shape[0]

    @pl.when(c == 0)
    def _():
        s_ref[...] = s0_ref[0]

    prev = jnp.where(c > 0, xp_ref[...], 0.0)
    nxt = jnp.where(c < nc - 1, xn_ref[...], 0.0)
    xe = jnp.concatenate([prev, x_ref[...], nxt], axis=0)
    pad = (CONV_K - 1) // 2
    acc = cb_ref[...] + cw_ref[0:1, :] * xe[HALO - pad:HALO - pad + n, :]
    for k in range(1, CONV_K):
        acc = acc + cw_ref[k:k + 1, :] * xe[HALO - pad + k:HALO - pad + k + n, :]
    xc = _silu(acc)
    xc_ref[...] = xc
    xs = xc[:, :SSD_INNER]
    bmat = xc[:, SSD_INNER:SSD_INNER + LANES]
    cmat = xc[:, SSD_INNER + LANES:SSD_INNER + 2 * LANES]

    dt, log_a, cum_col, dt_row, la_row, cum_row = _ssd_decays(dt_ref[...], bias_ref[...], alog_ref[...])
    r = lax.broadcasted_iota(I32, (n, n), 0)
    cc = lax.broadcasted_iota(I32, (n, n), 1)
    causal = cc <= r
    last_col = cum_col[n - 1:n, :]

    def w_of(h):
        seg = cum_col[:, h:h + 1] - cum_row[h:h + 1, :]
        return jnp.exp(jnp.where(causal, seg, NEG_BIG)) * dt_row[h:h + 1, :]

    def q_scale_of(h):
        return jnp.exp(cum_col[:, h:h + 1])

    def k_scale_of(h):
        return dt_row[h:h + 1, :] * jnp.exp(cum_row[h:h + 1, n - 1:n] - cum_row[h:h + 1, :])

    def carry_of(h):
        return jnp.exp(last_col[:, h:h + 1])

    y_ref[...] = _ssd_scan_chunk(xs, bmat, cmat, w_of, q_scale_of, k_scale_of, carry_of, s_ref)

    @pl.when(c == nc - 1)
    def _():
        sfin_ref[0] = s_ref[...]


def _ssd_bwd_kernel(xc_ref, dt_ref, yf_ref, z_ref, s0_ref, bias_ref, alog_ref, dskip_ref, nw_ref,
                    y_ref, sfin_ref, s_ref):
    c = pl.program_id(1)
    nc = pl.num_programs(1)
    n = xc_ref.shape[0]

    @pl.when(c == 0)
    def _():
        s_ref[...] = s0_ref[0]

    xc = xc_ref[...]
    xs = xc[:, :SSD_INNER]
    bmat = xc[:, SSD_INNER:SSD_INNER + LANES]
    cmat = xc[:, SSD_INNER + LANES:SSD_INNER + 2 * LANES]
    dt, log_a, cum_col, dt_row, la_row, cum_row = _ssd_decays(dt_ref[...], bias_ref[...], alog_ref[...])
    ex_col = cum_col - log_a
    ex_row = cum_row - la_row
    r = lax.broadcasted_iota(I32, (n, n), 0)
    cc = lax.broadcasted_iota(I32, (n, n), 1)
    anti = cc >= r
    tot_col = cum_col[n - 1:n, :]
    off = SSD_HEADS

    def w_of(h):
        j = off + h
        seg = ex_row[j:j + 1, :] - ex_col[:, j:j + 1]
        return jnp.exp(jnp.where(anti, seg, NEG_BIG)) * dt_row[j:j + 1, :]

    def q_scale_of(h):
        j = off + h
        return jnp.exp(tot_col[:, j:j + 1] - ex_col[:, j:j + 1])

    def k_scale_of(h):
        j = off + h
        return dt_row[j:j + 1, :] * jnp.exp(ex_row[j:j + 1, :])

    def carry_of(h):
        j = off + h
        return jnp.exp(tot_col[:, j:j + 1])

    y_b = _ssd_scan_chunk(xs, bmat, cmat, w_of, q_scale_of, k_scale_of, carry_of, s_ref)
    y = yf_ref[...] + y_b + dskip_ref[...] * xs
    y = y * _silu(z_ref[...])
    ms = jnp.mean(y * y, axis=-1, keepdims=True)
    y_ref[...] = y * lax.rsqrt(ms + NORM_EPS) * nw_ref[...]

    @pl.when(c == nc - 1)
    def _():
        sfin_ref[0] = s_ref[...]


def _pair_states(s):
    b, h, n, p = s.shape
    return s.reshape(b, h // 2, 2, n, p).transpose(0, 1, 3, 2, 4).reshape(b, h // 2, n, 2 * p)


def _unpair_states(s):
    b, hp, n, p2 = s.shape
    return s.reshape(b, hp, n, 2, p2 // 2).transpose(0, 1, 3, 2, 4).reshape(b, hp * 2, n, p2 // 2)


def _ssd(xbc, dt, z, s0_f, s0_b, conv_w, conv_b, dt_bias, a_log, d_skip, ssd_norm, seq):
    m, nx = xbc.shape
    nb = m // seq
    blk = min(SSD_BLOCK, seq)
    nc = seq // blk
    hb = blk // HALO
    n_halo = m // HALO
    pad16 = lambda a: jnp.pad(a.reshape(1, -1), ((0, 0), (0, LANES - a.size)))
    bias = pad16(dt_bias)
    alog = pad16(a_log)
    state_spec = pl.BlockSpec((1, SSD_PAIRS, D_STATE, LANES), lambda b, c: (b, 0, 0, 0))
    state_shape = jax.ShapeDtypeStruct((nb, SSD_PAIRS, D_STATE, LANES), F32)
    row = lambda width: pl.BlockSpec((1, width), lambda b, c: (0, 0))

    def fwd_rows(width):
        return pl.BlockSpec((blk, width), lambda b, c: (b * nc + c, 0))

    def bwd_rows(width):
        return pl.BlockSpec((blk, width), lambda b, c: (b * nc + nc - 1 - c, 0))

    y_f, xc, s_f = pl.pallas_call(
        _ssd_fwd_kernel,
        grid=(nb, nc),
        in_specs=[fwd_rows(nx),
                  pl.BlockSpec((HALO, nx), lambda b, c: (jnp.maximum((b * nc + c) * hb - 1, 0), 0)),
                  pl.BlockSpec((HALO, nx), lambda b, c: (jnp.minimum((b * nc + c + 1) * hb, n_halo - 1), 0)),
                  fwd_rows(LANES), state_spec,
                  pl.BlockSpec((CONV_K, nx), lambda b, c: (0, 0)), row(nx), row(LANES), row(LANES)],
        out_specs=[fwd_rows(SSD_INNER), fwd_rows(nx), state_spec],
        out_shape=[jax.ShapeDtypeStruct((m, SSD_INNER), F32), jax.ShapeDtypeStruct((m, nx), F32), state_shape],
        scratch_shapes=[pltpu.VMEM((SSD_PAIRS, D_STATE, LANES), F32)],
        compiler_params=_cparams("arbitrary", "arbitrary"),
        name="ssd_forward",
    )(xbc, xbc, xbc, dt, _pair_states(s0_f), conv_w, conv_b.reshape(1, nx), bias, alog)

    dskip = jnp.repeat(d_skip, SSD_HEAD_DIM).reshape(1, SSD_INNER)
    y, s_b = pl.pallas_call(
        _ssd_bwd_kernel,
        grid=(nb, nc),
        in_specs=[bwd_rows(nx), bwd_rows(LANES), bwd_rows(SSD_INNER), bwd_rows(SSD_INNER), state_spec,
                  row(LANES), row(LANES), row(SSD_INNER), row(SSD_INNER)],
        out_specs=[bwd_rows(SSD_INNER), state_spec],
        out_shape=[jax.ShapeDtypeStruct((m, SSD_INNER), F32), state_shape],
        scratch_shapes=[pltpu.VMEM((SSD_PAIRS, D_STATE, LANES), F32)],
        compiler_params=_cparams("arbitrary", "arbitrary"),
        name="ssd_backward",
    )(xc, dt, y_f, z, _pair_states(s0_b), bias, alog, dskip, ssd_norm.reshape(1, SSD_INNER))
    return y, _unpair_states(s_f), _unpair_states(s_b)

def _hgrn_kernel(q_ref, ff_ref, fb_ref, i_ref, g_ref, lb_ref, s0_ref, nw_ref, o_ref, sfin_ref,
                 sf_ref, sb_ref, ob_ref, *, layer):
    t_len = q_ref.shape[0]
    n = HGRN_CHUNK
    n_chunks = t_len // n
    tn = (((0,), (0,)), ((), ()))
    nt = (((1,), (1,)), ((), ()))

    lbp = lb_ref[...]
    e = jnp.exp(lbp - jnp.max(lbp, axis=0, keepdims=True))
    sm = e / jnp.sum(e, axis=0, keepdims=True)
    lb = sm[0] * 0.0
    for j in range(1, layer + 1):
        lb = lb + sm[j]

    r = lax.broadcasted_iota(I32, (n, n), 0)
    c = lax.broadcasted_iota(I32, (n, n), 1)
    lower = jnp.where(c <= r, 1.0, 0.0).astype(F32)
    srow = lax.broadcasted_iota(I32, (n, HGRN_DK), 0)
    qscale = HGRN_DK ** -0.5

    def chunk(row0, f_ref, lb_d, reverse, s_ref):
        q = _silu(q_ref[pl.ds(row0, n), :]) * qscale
        f = f_ref[pl.ds(row0, n), :]
        v = i_ref[pl.ds(row0, n), :]
        k = (1.0 - lb_d) * jax.nn.sigmoid(-f)
        lf = jnp.log(lb_d + (1.0 - lb_d) * jax.nn.sigmoid(f))
        cum = jnp.dot(lower, lf, precision=HIGHEST, preferred_element_type=F32)
        tot = cum[n - 1:n, :]
        if reverse:
            cum = cum - lf
        rows = []
        for t in range(n):
            tile0 = (t // SUBLANES) * SUBLANES
            lo, hi = (tile0, n) if reverse else (0, tile0 + SUBLANES)
            cum_s = cum[lo:hi]
            if reverse:
                seg = jnp.where(srow[lo:hi] >= t, cum_s - cum[t:t + 1, :], NEG_BIG)
            else:
                seg = jnp.where(srow[lo:hi] <= t, cum[t:t + 1, :] - cum_s, NEG_BIG)
            a = q[t:t + 1, :] * k[lo:hi] * jnp.exp(seg)
            sc = jnp.sum(a, axis=1, keepdims=True)
            rows.append(jnp.sum(sc * v[lo:hi], axis=0, keepdims=True))
        o = jnp.concatenate(rows, axis=0)
        s_old = s_ref[...]
        if reverse:
            q_in = q * jnp.exp(tot - cum)
            k_out = k * jnp.exp(cum)
        else:
            q_in = q * jnp.exp(cum)
            k_out = k * jnp.exp(tot - cum)
        o = o + lax.dot_general(q_in.astype(BF16), s_old.astype(BF16), nt, preferred_element_type=F32)
        s_ref[...] = jnp.exp(tot) * s_old + lax.dot_general(
            v.astype(BF16), k_out.astype(BF16), tn, preferred_element_type=F32)
        return o

    sf_ref[...] = s0_ref[0, 0, 0].T
    sb_ref[...] = s0_ref[0, 1, 0].T

    def body(ci, carry):
        row_f = pl.multiple_of(ci * n, n)
        row_b = pl.multiple_of((n_chunks - 1 - ci) * n, n)
        o_ref[pl.ds(row_f, n), :] = chunk(row_f, ff_ref, lb[0:1, :], False, sf_ref)
        ob_ref[pl.ds(row_b, n), :] = chunk(row_b, fb_ref, lb[1:2, :], True, sb_ref)
        return carry

    lax.fori_loop(0, n_chunks, body, 0)
    sfin_ref[0, 0, 0] = sf_ref[...].T
    sfin_ref[0, 1, 0] = sb_ref[...].T

    nw = nw_ref[...]
    blk = min(t_len, ROW_TILE)

    def finish(bi, carry):
        row0 = pl.multiple_of(bi * blk, blk)
        o = o_ref[pl.ds(row0, blk), :] + ob_ref[pl.ds(row0, blk), :]
        ms = jnp.mean(o * o, axis=-1, keepdims=True)
        o = o * lax.rsqrt(ms + NORM_EPS) * nw
        o_ref[pl.ds(row0, blk), :] = o * _silu(g_ref[pl.ds(row0, blk), :])
        return carry

    lax.fori_loop(0, t_len // blk, finish, 0)


def _hgrn(q, f_fw, f_bw, iv, g, o_lb, state0, g_norm, seq, layer):
    m, width = q.shape
    nb = m // seq
    dv = width // HGRN_HEADS
    col = pl.BlockSpec((seq, dv), lambda b, h: (b, h))
    state_spec = pl.BlockSpec((1, 2, 1, HGRN_DK, dv), lambda b, h: (b, 0, h, 0, 0))
    return pl.pallas_call(
        functools.partial(_hgrn_kernel, layer=layer),
        grid=(nb, HGRN_HEADS),
        in_specs=[col, col, col, col, col,
                  pl.BlockSpec((o_lb.shape[0], 2, HGRN_DK), lambda b, h: (0, 0, h)),
                  state_spec,
                  pl.BlockSpec((1, dv), lambda b, h: (0, 0))],
        out_specs=[col, state_spec],
        out_shape=[jax.ShapeDtypeStruct((m, width), F32),
                   jax.ShapeDtypeStruct((nb, 2, HGRN_HEADS, HGRN_DK, dv), F32)],
        scratch_shapes=[pltpu.VMEM((dv, HGRN_DK), F32), pltpu.VMEM((dv, HGRN_DK), F32),
                        pltpu.VMEM((seq, dv), F32)],
        compiler_params=_cparams("arbitrary", "arbitrary"),
        name="hgrn2",
    )(q, f_fw, f_bw, iv, g, o_lb, state0, g_norm.reshape(1, dv))

EVEN_SPLITS = ((0, 512), (512, 640), (640, 768), (768, 1280), (1280, 2048), (2048, 2176))
HGRN_SPLITS = tuple((i * 1024, (i + 1) * 1024) for i in range(5))


def _even_weight(w):
    main = EVEN_SPLITS[-1][0]
    return jnp.pad(w, ((0, 0), (0, LANES - (w.shape[1] - main)))).astype(BF16)


def _run_trunk(x3, mods, mod_row0, P, cache, sc_order=None):
    nb, seq, d = x3.shape
    x = x3.reshape(nb * seq, d)
    depth = P['norm_mix'].shape[0]
    ks, vs, ssd_states, hgrn_states = [], [], [], []
    for l in range(depth):
        j = l // 2
        row0 = (l * SUBLANES + mod_row0, 0 if cache is None else 1)
        if l % 2 == 0:
            q, k, v, z, xbc, dt = _inproj(x, mods, row0, seq, P['norm_mix'][l], P['e_w_in'][j],
                                          EVEN_SPLITS, "even_in_proj")
            q, k = _qkprep(q, k, P['e_q_norm'][j], P['e_k_norm'][j], seq, rope=cache is not None)
            if cache is None:
                s0_f = jnp.zeros((nb, SSD_HEADS, D_STATE, SSD_HEAD_DIM), F32)
                s0_b = s0_f
                o_attn = _ctx_attention(q, k, v, P['e_sink'][j], seq)
            else:
                s0_f, s0_b = cache[2][:, j, 0], cache[2][:, j, 1]
                n_ctx = cache[0].shape[2]
                o_attn = _lat_attention(q, k, v, cache[0][:, j].reshape(nb, n_ctx, -1),
                                        cache[1][:, j].reshape(nb, n_ctx, -1), P['e_sink'][j], seq)
            y, s_f, s_b = _ssd(xbc, dt, z, s0_f, s0_b, P['e_conv_w'][j], P['e_conv_b'][j],
                               P['e_dt_bias'][j], P['e_a_log'][j], P['e_d_skip'][j], P['e_ssd_norm'][j], seq)
            if cache is None:
                ks.append(k.reshape(nb, seq, N_KV_HEADS, HEAD_DIM))
                vs.append(v.reshape(nb, seq, N_KV_HEADS, HEAD_DIM))
                ssd_states.append(jnp.stack([s_f, s_b], axis=1))
            mix = jnp.concatenate([o_attn, y], axis=1)
            x = _outproj(mix, x, mods, row0, seq, P['e_w_out'][j], "even_out_proj")
        else:
            q, f_fw, f_bw, iv, g = _inproj(x, mods, row0, seq, P['norm_mix'][l], P['o_w_in'][j],
                                           HGRN_SPLITS, "odd_in_proj")
            if cache is None:
                s0 = jnp.zeros((nb, 2, HGRN_HEADS, HGRN_DK, d // HGRN_HEADS), F32)
            else:
                s0 = cache[3][:, j]
            o, s_new = _hgrn(q, f_fw, f_bw, iv, g, P['o_lb'], s0, P['o_g_norm'][j], seq, j)
            if cache is None:
                hgrn_states.append(s_new)
            x = _outproj(o, x, mods, row0, seq, P['o_w_out'][j], "odd_out_proj")
        x, y_sc = _peer(x, mods, row0, seq, l, P['norm_ffn'][l], P['p_w_q'][l], P['p_sub_keys'][l],
                        P['p_u'], P['p_v'],
                        (SC_SHARE_CONTEXT if cache is None else SC_SHARE_LATENT)[l],
                        after=None if sc_order is None or cache is None else sc_order[l])
        if sc_order is not None and cache is None:
            sc_order.append(y_sc)
    y = x.reshape(nb, seq, d)
    if cache is not None:
        return y, None
    return y, (jnp.stack(ks, axis=1), jnp.stack(vs, axis=1),
               jnp.stack(ssd_states, axis=1), jnp.stack(hgrn_states, axis=1))


def kernel(x_prompt, x_sample, cache_k, cache_v, state_ssd, state_hgrn, c, c_ctx, w_ada, b_ada, norm_mix, norm_ffn, e_w_in, e_q_norm, e_k_norm, e_sink, e_conv_w, e_conv_b, e_dt_bias, e_a_log, e_d_skip, e_ssd_norm, e_w_out, o_w_in, o_lb, o_g_norm, o_w_out, p_w_q, p_sub_keys, p_u, p_v):
    depth, d, d6 = w_ada.shape
    b_lat = x_sample.shape[0]
    cond_rows = jnp.concatenate([c_ctx[None, :], c, jnp.zeros((SUBLANES - 1 - b_lat, d), F32)], axis=0)
    mods = _modulation(cond_rows, w_ada, b_ada).reshape(depth * SUBLANES, 1, d6)
    P = {
        'norm_mix': norm_mix, 'norm_ffn': norm_ffn,
        'e_w_in': jnp.stack([_even_weight(w) for w in e_w_in]), 'e_q_norm': e_q_norm, 'e_k_norm': e_k_norm,
        'e_sink': e_sink, 'e_conv_w': e_conv_w, 'e_conv_b': e_conv_b, 'e_dt_bias': e_dt_bias,
        'e_a_log': e_a_log, 'e_d_skip': e_d_skip, 'e_ssd_norm': e_ssd_norm,
        'e_w_out': e_w_out.astype(BF16),
        'o_w_in': o_w_in.astype(BF16), 'o_lb': o_lb, 'o_g_norm': o_g_norm, 'o_w_out': o_w_out.astype(BF16),
        'p_w_q': p_w_q.astype(BF16),
        'p_sub_keys': p_sub_keys.astype(BF16).reshape(depth, PEER_HEADS * 2, PEER_NKEYS, PEER_DKEY),
        'p_u': p_u, 'p_v': p_v,
    }
    sc_order = []
    y_prompt, new_state = _run_trunk(x_prompt, mods, 0, P, None, sc_order)
    y_sample, _ = _run_trunk(x_sample, mods, 1, P, (cache_k, cache_v, state_ssd, state_hgrn), sc_order)
    return (y_prompt, y_sample) + new_state
```

```python
import functools
import math

import jax
import jax.numpy as jnp
from jax import lax
from jax.experimental import pallas as pl
from jax.experimental.pallas import tpu as pltpu
from jax.experimental.pallas import tpu_sc as plsc

F32 = jnp.float32
BF16 = jnp.bfloat16
I32 = jnp.int32
HIGHEST = lax.Precision.HIGHEST

NORM_EPS = 1e-6
NEG_BIG = -1e30
LANES = 128
SUBLANES = 8
VMEM_LIMIT = 48 * 1024 * 1024

GRID_W = 64
HEAD_DIM = 64
N_Q_HEADS = 8
N_KV_HEADS = 2
GQA_GROUP = 4
WINDOW = 128
ROPE_THETA = 10000.0
SSD_HEADS = 8
SSD_HEAD_DIM = 64
SSD_GROUPS = 2
D_STATE = 64
CONV_K = 5
HGRN_HEADS = 8
HGRN_DK = 128
HGRN_CHUNK = 32
PEER_HEADS = 8
PEER_NKEYS = 128
PEER_TOPK = 16
PEER_DKEY = 128
PEER_PAIRS = PEER_HEADS * PEER_TOPK

ROW_TILE = 256
PEER_BLOCK = 128
SC_LANES = 16
SC_CHUNK = 16
SC_SLOTS = 3
SC_SHARE_CONTEXT = ((1, 1), (1, 1), (1, 1), (1, 1))
SC_SHARE_LATENT = ((27, 32), (27, 32), (27, 32), (3, 4))
GELU_C = math.sqrt(2.0 / math.pi)


def _cparams(*sem):
    return pltpu.CompilerParams(dimension_semantics=sem, vmem_limit_bytes=VMEM_LIMIT)


def _norm_mod(x, nw, scale, shift):
    ms = jnp.mean(x * x, axis=-1, keepdims=True)
    return (x * lax.rsqrt(ms + NORM_EPS)) * nw * (1.0 + scale) + shift


def _mod_kernel(c_ref, w_ref, b_ref, o_ref):
    c = c_ref[...]
    s = c * jax.nn.sigmoid(c)
    o_ref[0] = jnp.dot(s, w_ref[0], precision=HIGHEST, preferred_element_type=F32) + b_ref[0]


def _modulation(cond_rows, w_ada, b_ada):
    depth, d, n = w_ada.shape
    rows = cond_rows.shape[0]
    return pl.pallas_call(
        _mod_kernel,
        grid=(depth, n // d),
        in_specs=[pl.BlockSpec((rows, d), lambda l, j: (0, 0)),
                  pl.BlockSpec((1, d, d), lambda l, j: (l, 0, j)),
                  pl.BlockSpec((1, 1, d), lambda l, j: (l, 0, j))],
        out_specs=pl.BlockSpec((1, rows, d), lambda l, j: (l, 0, j)),
        out_shape=jax.ShapeDtypeStruct((depth, rows, n), F32),
        compiler_params=_cparams("arbitrary", "arbitrary"),
        name="modulation",
    )(cond_rows, w_ada, b_ada.reshape(depth, 1, n))


def _mod_spec(mod_row0, seq, tile, d6, first_tile=0):
    row0, per_batch = mod_row0
    return pl.BlockSpec((1, 1, d6),
                        lambda i: (row0 + per_batch * (((i + first_tile) * tile) // seq), 0, 0))


def _inproj_kernel(x_ref, m_ref, nw_ref, w_ref, *o_refs, splits, d):
    m = m_ref[0]
    h = _norm_mod(x_ref[...], nw_ref[...], m[:, d:2 * d], m[:, 0:d]).astype(BF16)
    for o_ref, (a, b) in zip(o_refs, splits):
        o_ref[...] = jnp.dot(h, w_ref[:, a:b], preferred_element_type=F32)


def _inproj(x, mods, mod_row0, seq, nw, w_bf16, splits, name):
    m, d = x.shape
    n = w_bf16.shape[1]
    tile = min(ROW_TILE, seq)
    return pl.pallas_call(
        functools.partial(_inproj_kernel, splits=splits, d=d),
        grid=(m // tile,),
        in_specs=[pl.BlockSpec((tile, d), lambda i: (i, 0)),
                  _mod_spec(mod_row0, seq, tile, mods.shape[-1]),
                  pl.BlockSpec((1, d), lambda i: (0, 0)),
                  pl.BlockSpec((d, n), lambda i: (0, 0))],
        out_specs=[pl.BlockSpec((tile, b - a), lambda i: (i, 0)) for a, b in splits],
        out_shape=[jax.ShapeDtypeStruct((m, b - a), F32) for a, b in splits],
        compiler_params=_cparams("arbitrary"),
        name=name,
    )(x, mods, nw.reshape(1, d), w_bf16)


def _outproj_kernel(mix_ref, x_ref, m_ref, w_ref, o_ref, *, d):
    y = jnp.dot(mix_ref[...].astype(BF16), w_ref[...], preferred_element_type=F32)
    o_ref[...] = x_ref[...] + m_ref[0][:, 2 * d:3 * d] * y


def _outproj(mix, x, mods, mod_row0, seq, w_bf16, name):
    m, d = x.shape
    k = mix.shape[1]
    tile = min(ROW_TILE, seq)
    return pl.pallas_call(
        functools.partial(_outproj_kernel, d=d),
        grid=(m // tile,),
        in_specs=[pl.BlockSpec((tile, k), lambda i: (i, 0)),
                  pl.BlockSpec((tile, d), lambda i: (i, 0)),
                  _mod_spec(mod_row0, seq, tile, mods.shape[-1]),
                  pl.BlockSpec((k, d), lambda i: (0, 0))],
        out_specs=pl.BlockSpec((tile, d), lambda i: (i, 0)),
        out_shape=jax.ShapeDtypeStruct((m, d), F32),
        compiler_params=_cparams("arbitrary"),
        name=name,
    )(mix, x, mods, w_bf16)


def _topk_over_rows(s, k, payload=None):
    n = s.shape[0]
    iota = lax.broadcasted_iota(I32, s.shape, 0)
    vals, idxs, pays = [], [], []
    for _ in range(k):
        m = jnp.max(s, axis=0, keepdims=True)
        i = jnp.min(jnp.where(s == m, iota, n), axis=0, keepdims=True)
        hit = iota == i
        vals.append(m)
        idxs.append(i)
        if payload is not None:
            pays.append(jnp.max(jnp.where(hit, payload, -1), axis=0, keepdims=True))
        s = jnp.where(hit, -jnp.inf, s)
    out = (jnp.concatenate(vals, axis=0), jnp.concatenate(idxs, axis=0))
    if payload is not None:
        out += (jnp.concatenate(pays, axis=0),)
    return out


def _peer_route_kernel(x_ref, m_ref, nw_ref, wq_ref, keys_ref, h_ref, e_ref, g_ref, *, d):
    m = m_ref[0]
    h = _norm_mod(x_ref[...], nw_ref[...], m[:, 4 * d:5 * d], m[:, 3 * d:4 * d])
    h_ref[...] = h
    hb = h.astype(BF16)
    nt = (((1,), (1,)), ((), ()))
    for head in range(PEER_HEADS):
        tops = []
        for half in range(2):
            c0 = (head * 2 + half) * PEER_DKEY
            q = jnp.dot(hb, wq_ref[:, c0:c0 + PEER_DKEY], preferred_element_type=F32)
            s = lax.dot_general(keys_ref[head * 2 + half], q.astype(BF16), nt,
                                preferred_element_type=F32)
            tops.append(_topk_over_rows(s, PEER_TOPK))
        (s0, i0), (s1, i1) = tops
        widths = [PEER_TOPK // (a + 1) for a in range(PEER_TOPK)]
        n_pad = -sum(widths) % SUBLANES
        cand_s = jnp.concatenate([s0[a:a + 1] + s1[:w] for a, w in enumerate(widths)]
                                 + [jnp.full((n_pad, s0.shape[1]), -jnp.inf, F32)], axis=0)
        cand_e = jnp.concatenate([i0[a:a + 1] * PEER_NKEYS + i1[:w] for a, w in enumerate(widths)]
                                 + [jnp.zeros((n_pad, s0.shape[1]), I32)], axis=0)
        best_s, _, best_e = _topk_over_rows(cand_s, PEER_TOPK, payload=cand_e)
        p = jnp.exp(best_s - best_s[0:1])
        r0 = head * PEER_TOPK
        e_ref[r0:r0 + PEER_TOPK, :] = best_e
        g_ref[r0:r0 + PEER_TOPK, :] = p / jnp.sum(p, axis=0, keepdims=True)


def _peer_route(x, mods, mod_row0, seq, nw, wq_bf16, keys_bf16):
    m, d = x.shape
    tile = min(ROW_TILE, seq)
    nq = wq_bf16.shape[1]
    return pl.pallas_call(
        functools.partial(_peer_route_kernel, d=d),
        grid=(m // tile,),
        in_specs=[pl.BlockSpec((tile, d), lambda i: (i, 0)),
                  _mod_spec(mod_row0, seq, tile, mods.shape[-1]),
                  pl.BlockSpec((1, d), lambda i: (0, 0)),
                  pl.BlockSpec((d, nq), lambda i: (0, 0)),
                  pl.BlockSpec(keys_bf16.shape, lambda i: (0, 0, 0))],
        out_specs=[pl.BlockSpec((tile, d), lambda i: (i, 0)),
                   pl.BlockSpec((PEER_PAIRS, tile), lambda i: (0, i)),
                   pl.BlockSpec((PEER_PAIRS, tile), lambda i: (0, i))],
        out_shape=[jax.ShapeDtypeStruct((m, d), F32),
                   jax.ShapeDtypeStruct((PEER_PAIRS, m), I32),
                   jax.ShapeDtypeStruct((PEER_PAIRS, m), F32)],
        compiler_params=_cparams("arbitrary"),
        name="peer_route",
    )(x, mods, nw.reshape(1, d), wq_bf16, keys_bf16)


def _peer_gather_kernel(idx_hbm, h_ref, g_ref, x_ref, m_ref, u_hbm, v_hbm, o_ref,
                        idx_smem, ubuf, vbuf, sem_idx, sem_u, sem_v, *, layer, d):
    blk = pl.program_id(0)
    n_groups = PEER_BLOCK // SUBLANES

    cp = pltpu.make_async_copy(idx_hbm.at[blk], idx_smem, sem_idx)
    cp.start()
    cp.wait()

    def issue(tok, slot):
        def body(pair, carry):
            e = idx_smem[tok, pair]
            pltpu.make_async_copy(u_hbm.at[layer, pl.ds(e, 1)], ubuf.at[slot, pl.ds(pair, 1)],
                                  sem_u.at[slot]).start()
            pltpu.make_async_copy(v_hbm.at[layer, pl.ds(e, 1)], vbuf.at[slot, pl.ds(pair, 1)],
                                  sem_v.at[slot]).start()
            return carry
        lax.fori_loop(0, PEER_PAIRS, body, 0, unroll=8)

    def wait(slot):
        pltpu.make_async_copy(u_hbm.at[layer, pl.ds(0, PEER_PAIRS)], ubuf.at[slot], sem_u.at[slot]).wait()
        pltpu.make_async_copy(v_hbm.at[layer, pl.ds(0, PEER_PAIRS)], vbuf.at[slot], sem_v.at[slot]).wait()

    gate2 = m_ref[0][:, 5 * d:6 * d]
    lane = lax.broadcasted_iota(I32, (PEER_PAIRS, PEER_BLOCK), 1)

    issue(0, 0)

    def group(grp, carry):
        base = pl.multiple_of(grp * SUBLANES, SUBLANES)
        h8 = h_ref[pl.ds(base, SUBLANES), :]
        rows = []
        for r in range(SUBLANES):
            tok = base + r
            slot = r % 2
            if r < SUBLANES - 1:
                issue(tok + 1, 1 - slot)
            else:
                @pl.when(grp < n_groups - 1)
                def _():
                    issue(tok + 1, 1 - slot)
            wait(slot)
            act = jnp.sum(ubuf[slot] * h8[r:r + 1, :], axis=1, keepdims=True)
            gate = jnp.sum(jnp.where(lane == tok, g_ref[...], 0.0), axis=1, keepdims=True)
            w = jax.nn.gelu(act) * gate
            rows.append(jnp.sum(vbuf[slot] * w, axis=0, keepdims=True))
        out8 = jnp.concatenate(rows, axis=0)
        o_ref[pl.ds(base, SUBLANES), :] = x_ref[pl.ds(base, SUBLANES), :] + gate2 * out8
        return carry

    lax.fori_loop(0, n_groups, group, 0)


def _peer_gather(idx, h, gates, x, mods, mod_row0, seq, p_u, p_v, layer, first_block):
    m, d = x.shape
    nblk = m // PEER_BLOCK - first_block
    idx3 = idx[:, first_block * PEER_BLOCK:].T.reshape(nblk, PEER_BLOCK, PEER_PAIRS)
    rows = pl.BlockSpec((PEER_BLOCK, d), lambda i: (i + first_block, 0))
    return pl.pallas_call(
        functools.partial(_peer_gather_kernel, layer=layer, d=d),
        grid=(nblk,),
        in_specs=[pl.BlockSpec(memory_space=pl.ANY),
                  rows,
                  pl.BlockSpec((PEER_PAIRS, PEER_BLOCK), lambda i: (0, i + first_block)),
                  rows,
                  _mod_spec(mod_row0, seq, PEER_BLOCK, mods.shape[-1], first_block),
                  pl.BlockSpec(memory_space=pl.ANY),
                  pl.BlockSpec(memory_space=pl.ANY)],
        out_specs=pl.BlockSpec((PEER_BLOCK, d), lambda i: (i, 0)),
        out_shape=jax.ShapeDtypeStruct((nblk * PEER_BLOCK, d), F32),
        scratch_shapes=[pltpu.SMEM((PEER_BLOCK, PEER_PAIRS), I32),
                        pltpu.VMEM((2, PEER_PAIRS, d), F32),
                        pltpu.VMEM((2, PEER_PAIRS, d), F32),
                        pltpu.SemaphoreType.DMA,
                        pltpu.SemaphoreType.DMA((2,)),
                        pltpu.SemaphoreType.DMA((2,))],
        compiler_params=_cparams("arbitrary"),
        name="peer_gather",
    )(idx3, h, gates, x, mods, p_u, p_v)


def _peer_experts_sc(idx, h, gates, u_rows, v_rows):
    m = idx.shape[0]
    d = h.shape[1]
    info = plsc.get_sparse_core_info()
    n_workers = info.num_cores * info.num_subcores
    per = m // n_workers
    n_chunks = PEER_PAIRS // SC_CHUNK
    n_vec = d // SC_LANES
    mesh = plsc.VectorSubcoreMesh(core_axis_name="c", subcore_axis_name="s")

    @functools.partial(
        pl.kernel, out_type=jax.ShapeDtypeStruct((m, d), F32), mesh=mesh,
        scratch_types=[pltpu.VMEM((2, PEER_PAIRS), I32), pltpu.VMEM((2, d), F32),
                       pltpu.VMEM((2, PEER_PAIRS), F32), pltpu.VMEM((2, d), F32),
                       pltpu.VMEM((SC_SLOTS, SC_CHUNK, d), F32), pltpu.VMEM((SC_SLOTS, SC_CHUNK, d), F32),
                       pltpu.SemaphoreType.DMA((2,)), pltpu.SemaphoreType.DMA((2,)),
                       pltpu.SemaphoreType.DMA((SC_SLOTS,)), pltpu.SemaphoreType.DMA((SC_SLOTS,))],
        compiler_params=pltpu.CompilerParams(needs_layout_passes=False),
        name="peer_experts_sc")
    def body(idx_hbm, h_hbm, g_hbm, u_hbm, v_hbm, o_hbm,
             idx_v, x_v, g_v, out_v, ubuf, vbuf, sem_meta, sem_out, sem_u, sem_v):
        wid = lax.axis_index("c") * info.num_subcores + lax.axis_index("s")
        tok0 = wid * per
        lane = lax.iota(I32, SC_LANES)

        def meta_copies(ti, ms):
            t = tok0 + ti
            return (pltpu.make_async_copy(idx_hbm.at[t], idx_v.at[ms], sem_meta.at[ms]),
                    pltpu.make_async_copy(h_hbm.at[t], x_v.at[ms], sem_meta.at[ms]),
                    pltpu.make_async_copy(g_hbm.at[t], g_v.at[ms], sem_meta.at[ms]))

        def gather_copies(ms, c, slot):
            ids = idx_v.at[ms, pl.ds(c * SC_CHUNK, SC_CHUNK)]
            return (pltpu.make_async_copy(u_hbm.at[ids], ubuf.at[slot], sem_u.at[slot]),
                    pltpu.make_async_copy(v_hbm.at[ids], vbuf.at[slot], sem_v.at[slot]))

        def out_copy(ti, ms):
            return pltpu.make_async_copy(out_v.at[ms], o_hbm.at[tok0 + ti], sem_out.at[ms])

        for cp in meta_copies(0, 0):
            cp.start()
        for cp in meta_copies(0, 0):
            cp.wait()
        for c in range(SC_SLOTS - 1):
            for cp in gather_copies(0, c, c):
                cp.start()

        def token(ti, carry):
            ms = ti % 2
            nxt = 1 - ms

            @pl.when(ti + 1 < per)
            def _():
                for cp in meta_copies(ti + 1, nxt):
                    cp.start()

            @pl.when(ti >= 2)
            def _():
                out_copy(ti - 2, ms).wait()

            def zero(j, c):
                out_v[ms, pl.ds(j * SC_LANES, SC_LANES)] = jnp.zeros((SC_LANES,), F32)
                return c
            lax.fori_loop(0, n_vec, zero, 0)

            for c in range(n_chunks):
                g = ti * n_chunks + c
                slot = lax.rem(g, SC_SLOTS)
                ahead = c + SC_SLOTS - 1
                ahead_slot = lax.rem(g + SC_SLOTS - 1, SC_SLOTS)
                if ahead < n_chunks:
                    for cp in gather_copies(ms, ahead, ahead_slot):
                        cp.start()
                else:
                    @pl.when(ti + 1 < per)
                    def _():
                        if ahead == n_chunks:
                            for cp in meta_copies(ti + 1, nxt):
                                cp.wait()
                        for cp in gather_copies(nxt, ahead - n_chunks, ahead_slot):
                            cp.start()
                cu, cv = gather_copies(ms, c, slot)
                cu.wait()

                def udot(j, accs):
                    xj = x_v[ms, pl.ds(j * SC_LANES, SC_LANES)]
                    return tuple(accs[r] + ubuf[slot, r, pl.ds(j * SC_LANES, SC_LANES)] * xj
                                 for r in range(SC_CHUNK))
                accs = lax.fori_loop(0, n_vec, udot,
                                     tuple(jnp.zeros((SC_LANES,), F32) for _ in range(SC_CHUNK)))
                act = jnp.zeros((SC_LANES,), F32)
                for r in range(SC_CHUNK):
                    act = jnp.where(lane == r, jnp.sum(accs[r]), act)
                y = GELU_C * (act + 0.044715 * (act * act * act))
                w = act / (1.0 + jnp.exp(-2.0 * y)) * g_v[ms, pl.ds(c * SC_CHUNK, SC_CHUNK)]
                ws = [jnp.sum(jnp.where(lane == r, w, 0.0)) for r in range(SC_CHUNK)]
                cv.wait()

                @plsc.parallel_loop(0, n_vec, unroll=2)
                def _(j):
                    parts = [ws[r] * vbuf[slot, r, pl.ds(j * SC_LANES, SC_LANES)] for r in range(SC_CHUNK)]
                    while len(parts) > 1:
                        parts = [parts[i] + parts[i + 1] for i in range(0, len(parts), 2)]
                    plsc.addupdate(out_v.at[ms, pl.ds(j * SC_LANES, SC_LANES)], parts[0])

            out_copy(ti, ms).start()
            return carry

        lax.fori_loop(0, per, token, 0)
        for back in (2, 1):
            if per >= back:
                out_copy(per - back, (per - back) % 2).wait()

    return body(idx, h, gates, u_rows, v_rows)


def _residual_kernel(x_ref, y_ref, m_ref, o_ref, *, d):
    o_ref[...] = x_ref[...] + m_ref[0][:, 5 * d:6 * d] * y_ref[...]


def _residual(x, y, mods, mod_row0, seq):
    m, d = y.shape
    tile = min(ROW_TILE, seq)
    return pl.pallas_call(
        functools.partial(_residual_kernel, d=d),
        grid=(m // tile,),
        in_specs=[pl.BlockSpec((tile, d), lambda i: (i, 0)),
                  pl.BlockSpec((tile, d), lambda i: (i, 0)),
                  _mod_spec(mod_row0, seq, tile, mods.shape[-1])],
        out_specs=pl.BlockSpec((tile, d), lambda i: (i, 0)),
        out_shape=jax.ShapeDtypeStruct((m, d), F32),
        compiler_params=_cparams("arbitrary"),
        name="peer_residual",
    )(x, y, mods)


def _peer(x, mods, mod_row0, seq, layer, nw, wq_bf16, keys_bf16, p_u, p_v, sc_share, after=None):
    m, d = x.shape
    h, idx, gates = _peer_route(x, mods, mod_row0, seq, nw, wq_bf16, keys_bf16)
    n_blocks = m // PEER_BLOCK
    sc_blocks = (n_blocks * sc_share[0]) // sc_share[1]
    m_sc = sc_blocks * PEER_BLOCK
    n_experts = p_u.shape[1]
    idx_sc = idx[:, :m_sc].T + layer * n_experts
    if after is not None:
        idx_sc, _ = lax.optimization_barrier((idx_sc, after))
    y_sc = _peer_experts_sc(idx_sc, h, gates[:, :m_sc].T,
                            p_u.reshape(-1, d), p_v.reshape(-1, d))
    x_sc = _residual(x, y_sc, mods, mod_row0, seq)
    if sc_blocks == n_blocks:
        return x_sc, y_sc
    x_tc = _peer_gather(idx, h, gates, x, mods, mod_row0, seq, p_u, p_v, layer, sc_blocks)
    return jnp.concatenate([x_sc, x_tc], axis=0), y_sc


def _head_mean_square(x):
    n = x.shape[1]
    r = lax.broadcasted_iota(I32, (n, n), 0) // HEAD_DIM
    c = lax.broadcasted_iota(I32, (n, n), 1) // HEAD_DIM
    seg = jnp.where(r == c, 1.0 / HEAD_DIM, 0.0).astype(F32)
    return jnp.dot(x * x, seg, precision=HIGHEST, preferred_element_type=F32)


def _swap_rot_halves(x):
    n = x.shape[1]
    quarter = HEAD_DIM // 4
    lane = lax.broadcasted_iota(I32, x.shape, 1)
    lo = (lane % (2 * quarter)) < quarter
    return jnp.where(lo, pltpu.roll(x, n - quarter, axis=1), pltpu.roll(x, quarter, axis=1))


def _qkprep_kernel(q_ref, k_ref, qw_ref, kw_ref, *rest, rope):
    if rope:
        cos_ref, sin_ref, qo_ref, ko_ref = rest
    else:
        qo_ref, ko_ref = rest
    q = q_ref[...]
    k = k_ref[...]
    q = q * lax.rsqrt(_head_mean_square(q) + NORM_EPS) * qw_ref[...]
    k = k * lax.rsqrt(_head_mean_square(k) + NORM_EPS) * kw_ref[...]
    if rope:
        cos = cos_ref[...]
        sin = sin_ref[...]
        cq = jnp.concatenate([cos] * (q.shape[1] // LANES), axis=1)
        sq = jnp.concatenate([sin] * (q.shape[1] // LANES), axis=1)
        q = q * cq + _swap_rot_halves(q) * sq
        k = k * cos + _swap_rot_halves(k) * sin
    qo_ref[...] = q
    ko_ref[...] = k


def _rope_tables(seq):
    axis_dim = HEAD_DIM // 2
    inv_freq = ROPE_THETA ** (-jnp.arange(0, axis_dim, 2, dtype=F32) / axis_dim)
    t = jnp.arange(seq)
    pos = jnp.stack([(t // GRID_W).astype(F32), (t % GRID_W).astype(F32)], axis=1)
    lane = jnp.arange(LANES)
    dd = lane % HEAD_DIM
    ang = pos[:, dd // axis_dim] * inv_freq[dd % (axis_dim // 2)][None, :]
    sign = jnp.where((dd % axis_dim) < axis_dim // 2, -1.0, 1.0).astype(F32)
    return jnp.cos(ang), jnp.sin(ang) * sign[None, :]


def _qkprep(q, k, qw, kw, seq, rope):
    m, nq = q.shape
    nk = k.shape[1]
    tile = min(ROW_TILE, seq)
    qw_row = jnp.tile(qw, nq // HEAD_DIM).reshape(1, nq)
    kw_row = jnp.tile(kw, nk // HEAD_DIM).reshape(1, nk)
    in_specs = [pl.BlockSpec((tile, nq), lambda i: (i, 0)),
                pl.BlockSpec((tile, nk), lambda i: (i, 0)),
                pl.BlockSpec((1, nq), lambda i: (0, 0)),
                pl.BlockSpec((1, nk), lambda i: (0, 0))]
    args = [q, k, qw_row, kw_row]
    if rope:
        cos, sin = _rope_tables(seq)
        per_seq = seq // tile
        in_specs += [pl.BlockSpec((tile, LANES), lambda i: (i % per_seq, 0)),
                     pl.BlockSpec((tile, LANES), lambda i: (i % per_seq, 0))]
        args += [cos, sin]
    return pl.pallas_call(
        functools.partial(_qkprep_kernel, rope=rope),
        grid=(m // tile,),
        in_specs=in_specs,
        out_specs=[pl.BlockSpec((tile, nq), lambda i: (i, 0)),
                   pl.BlockSpec((tile, nk), lambda i: (i, 0))],
        out_shape=[jax.ShapeDtypeStruct((m, nq), F32), jax.ShapeDtypeStruct((m, nk), F32)],
        compiler_params=_cparams("arbitrary"),
        name="qk_prep",
    )(*args)


def _dup_halves(x):
    lane = lax.broadcasted_iota(I32, x.shape, 1)
    sw = pltpu.roll(x, HEAD_DIM, axis=1)
    lo = lane < HEAD_DIM
    return jnp.where(lo, x, sw), jnp.where(lo, sw, x)


def _attend(q, k_all, v_all, sink_ref, mask):
    scale = HEAD_DIM ** -0.5
    nt = (((1,), (1,)), ((), ()))
    kk = [a.astype(BF16) for a in _dup_halves(k_all)]
    vv = [a.astype(BF16) for a in _dup_halves(v_all)]
    lane = lax.broadcasted_iota(I32, (q.shape[0], LANES), 1)
    lo = lane < HEAD_DIM
    tiles = []
    for t in range(q.shape[1] // LANES):
        qt = q[:, t * LANES:(t + 1) * LANES]
        g = (2 * t) // GQA_GROUP
        halves = []
        for hh in range(2):
            head = 2 * t + hh
            qm = jnp.where(lo if hh == 0 else ~lo, qt, 0.0).astype(BF16)
            s = lax.dot_general(qm, kk[g], nt, preferred_element_type=F32) * scale
            if mask is not None:
                s = jnp.where(mask, s, NEG_BIG)
            sink = sink_ref[head]
            mx = jnp.maximum(jnp.max(s, axis=1, keepdims=True), sink)
            p = jnp.exp(s - mx)
            den = jnp.sum(p, axis=1, keepdims=True) + jnp.exp(sink - mx)
            p = (p / den).astype(BF16)
            halves.append(jnp.dot(p, vv[g], preferred_element_type=F32))
        tiles.append(jnp.where(lo, halves[0], halves[1]))
    return jnp.concatenate(tiles, axis=1)


def _ctx_attn_kernel(sink_ref, q_ref, k_ref, v_ref, o_ref):
    o_ref[...] = _attend(q_ref[...], k_ref[...], v_ref[...], sink_ref, None)


def _ctx_attention(q, k, v, sink, seq):
    m, nq = q.shape
    nk = k.shape[1]
    return pl.pallas_call(
        _ctx_attn_kernel,
        grid=(m // seq,),
        in_specs=[pl.BlockSpec(memory_space=pltpu.SMEM),
                  pl.BlockSpec((seq, nq), lambda b: (b, 0)),
                  pl.BlockSpec((seq, nk), lambda b: (b, 0)),
                  pl.BlockSpec((seq, nk), lambda b: (b, 0))],
        out_specs=pl.BlockSpec((seq, nq), lambda b: (b, 0)),
        out_shape=jax.ShapeDtypeStruct((m, nq), F32),
        compiler_params=_cparams("arbitrary"),
        name="ctx_attention",
    )(sink, q, k, v)


def _lat_attn_kernel(sink_ref, q_ref, kc_ref, vc_ref, kp_ref, k0_ref, kn_ref, vp_ref, v0_ref, vn_ref,
                     o_ref, *, seq):
    qb = pl.program_id(1)
    blk = q_ref.shape[0]
    n_ctx = kc_ref.shape[1]
    k_all = jnp.concatenate([kc_ref[0], kp_ref[...], k0_ref[...], kn_ref[...]], axis=0)
    v_all = jnp.concatenate([vc_ref[0], vp_ref[...], v0_ref[...], vn_ref[...]], axis=0)
    tk = k_all.shape[0]
    qpos = qb * blk + lax.broadcasted_iota(I32, (blk, tk), 0)
    col = lax.broadcasted_iota(I32, (blk, tk), 1)
    kpos = (qb - 1) * blk + col - n_ctx
    local_ok = (jnp.abs(qpos - kpos) <= WINDOW) & (kpos >= 0) & (kpos < seq)
    mask = (col < n_ctx) | local_ok
    o_ref[...] = _attend(q_ref[...], k_all, v_all, sink_ref, mask)


def _lat_attention(q, k, v, k_ctx, v_ctx, sink, seq):
    m, nq = q.shape
    nk = k.shape[1]
    blk = WINDOW
    nb = seq // blk
    n_ctx = k_ctx.shape[1]
    last = m // blk - 1

    def kv_spec(shift):
        return pl.BlockSpec((blk, nk), lambda b, i: (jnp.clip(b * nb + i + shift, 0, last), 0))

    ctx_spec = pl.BlockSpec((1, n_ctx, nk), lambda b, i: (b, 0, 0))
    return pl.pallas_call(
        functools.partial(_lat_attn_kernel, seq=seq),
        grid=(m // seq, nb),
        in_specs=[pl.BlockSpec(memory_space=pltpu.SMEM),
                  pl.BlockSpec((blk, nq), lambda b, i: (b * nb + i, 0)),
                  ctx_spec, ctx_spec,
                  kv_spec(-1), kv_spec(0), kv_spec(1),
                  kv_spec(-1), kv_spec(0), kv_spec(1)],
        out_specs=pl.BlockSpec((blk, nq), lambda b, i: (b * nb + i, 0)),
        out_shape=jax.ShapeDtypeStruct((m, nq), F32),
        compiler_params=_cparams("arbitrary", "arbitrary"),
        name="lat_attention",
    )(sink, q, k_ctx, v_ctx, k, k, k, v, v, v)

SSD_BLOCK = 256
SSD_PAIRS = SSD_HEADS // 2
SSD_INNER = SSD_HEADS * SSD_HEAD_DIM
HALO = SUBLANES


def _softplus(x):
    return jnp.maximum(x, 0.0) + jnp.log1p(jnp.exp(-jnp.abs(x)))


def _silu(x):
    return x * jax.nn.sigmoid(x)


def _ssd_decays(dt_raw, bias, a_log):
    n = dt_raw.shape[0]
    dt = _softplus(dt_raw + bias)
    log_a = dt * (-jnp.exp(a_log))
    r = lax.broadcasted_iota(I32, (n, n), 0)
    c = lax.broadcasted_iota(I32, (n, n), 1)
    lower = jnp.where(c <= r, 1.0, 0.0).astype(F32)
    upper = jnp.where(r <= c, 1.0, 0.0).astype(F32)
    cum_col = jnp.dot(lower, log_a, precision=HIGHEST, preferred_element_type=F32)
    dt_row = dt.T
    la_row = log_a.T
    cum_row = jnp.dot(la_row, upper, precision=HIGHEST, preferred_element_type=F32)
    return dt, log_a, cum_col, dt_row, la_row, cum_row


def _ssd_scan_chunk(xs, bmat, cmat, w_of, q_scale_of, k_scale_of, carry_of, s_ref):
    nt = (((1,), (1,)), ((), ()))
    n = xs.shape[0]
    lane = lax.broadcasted_iota(I32, (n, LANES), 1)
    lo = lane < SSD_HEAD_DIM
    lane_s = lax.broadcasted_iota(I32, (D_STATE, LANES), 1)
    lo_s = lane_s < SSD_HEAD_DIM
    b_t = bmat.T
    cb16 = cmat.astype(BF16)
    ys = []
    for pair in range(SSD_PAIRS):
        g = (2 * pair) // (SSD_HEADS // SSD_GROUPS)
        in_g = (lane // D_STATE) == g
        cg = jnp.where(in_g, cmat, 0.0)
        cb = lax.dot_general(cg.astype(BF16), bmat.astype(BF16), nt, preferred_element_type=F32)
        x_pair = xs[:, pair * LANES:(pair + 1) * LANES]
        x16 = x_pair.astype(BF16)
        s_old = s_ref[pair]
        s2 = jnp.concatenate([s_old, s_old], axis=0).astype(BF16)
        bg_t = b_t[g * D_STATE:(g + 1) * D_STATE, :]
        y_h, s_h = [], []
        for hh in range(2):
            h = 2 * pair + hh
            w = (cb * w_of(h)).astype(BF16)
            y = jnp.dot(w, x16, preferred_element_type=F32)
            cq = (cg * q_scale_of(h)).astype(BF16)
            y = y + jnp.dot(cq, s2, preferred_element_type=F32)
            y_h.append(y)
            kt = (bg_t * k_scale_of(h)).astype(BF16)
            s_h.append(carry_of(h) * s_old + jnp.dot(kt, x16, preferred_element_type=F32))
        ys.append(jnp.where(lo, y_h[0], y_h[1]))
        s_ref[pair] = jnp.where(lo_s, s_h[0], s_h[1])
    return jnp.concatenate(ys, axis=1)


def _ssd_fwd_kernel(x_ref, xp_ref, xn_ref, dt_ref, s0_ref, cw_ref, cb_ref, bias_ref, alog_ref,
                    y_ref, xc_ref, sfin_ref, s_ref):
    c = pl.program_id(1)
    nc = pl.num_programs(1)
    n = x_ref.shape[0]

    @pl.when(c == 0)
    def _():
        s_ref[...] = s0_ref[0]

    prev = jnp.where(c > 0, xp_ref[...], 0.0)
    nxt = jnp.where(c < nc - 1, xn_ref[...], 0.0)
    xe = jnp.concatenate([prev, x_ref[...], nxt], axis=0)
    pad = (CONV_K - 1) // 2
    acc = cb_ref[...] + cw_ref[0:1, :] * xe[HALO - pad:HALO - pad + n, :]
    for k in range(1, CONV_K):
        acc = acc + cw_ref[k:k + 1, :] * xe[HALO - pad + k:HALO - pad + k + n, :]
    xc = _silu(acc)
    xc_ref[...] = xc
    xs = xc[:, :SSD_INNER]
    bmat = xc[:, SSD_INNER:SSD_INNER + LANES]
    cmat = xc[:, SSD_INNER + LANES:SSD_INNER + 2 * LANES]

    dt, log_a, cum_col, dt_row, la_row, cum_row = _ssd_decays(dt_ref[...], bias_ref[...], alog_ref[...])
    r = lax.broadcasted_iota(I32, (n, n), 0)
    cc = lax.broadcasted_iota(I32, (n, n), 1)
    causal = cc <= r
    last_col = cum_col[n - 1:n, :]

    def w_of(h):
        seg = cum_col[:, h:h + 1] - cum_row[h:h + 1, :]
        return jnp.exp(jnp.where(causal, seg, NEG_BIG)) * dt_row[h:h + 1, :]

    def q_scale_of(h):
        return jnp.exp(cum_col[:, h:h + 1])

    def k_scale_of(h):
        return dt_row[h:h + 1, :] * jnp.exp(cum_row[h:h + 1, n - 1:n] - cum_row[h:h + 1, :])

    def carry_of(h):
        return jnp.exp(last_col[:, h:h + 1])

    y_ref[...] = _ssd_scan_chunk(xs, bmat, cmat, w_of, q_scale_of, k_scale_of, carry_of, s_ref)

    @pl.when(c == nc - 1)
    def _():
        sfin_ref[0] = s_ref[...]


def _ssd_bwd_kernel(xc_ref, dt_ref, yf_ref, z_ref, s0_ref, bias_ref, alog_ref, dskip_ref, nw_ref,
                    y_ref, sfin_ref, s_ref):
    c = pl.program_id(1)
    nc = pl.num_programs(1)
    n = xc_ref.shape[0]

    @pl.when(c == 0)
    def _():
        s_ref[...] = s0_ref[0]

    xc = xc_ref[...]
    xs = xc[:, :SSD_INNER]
    bmat = xc[:, SSD_INNER:SSD_INNER + LANES]
    cmat = xc[:, SSD_INNER + LANES:SSD_INNER + 2 * LANES]
    dt, log_a, cum_col, dt_row, la_row, cum_row = _ssd_decays(dt_ref[...], bias_ref[...], alog_ref[...])
    ex_col = cum_col - log_a
    ex_row = cum_row - la_row
    r = lax.broadcasted_iota(I32, (n, n), 0)
    cc = lax.broadcasted_iota(I32, (n, n), 1)
    anti = cc >= r
    tot_col = cum_col[n - 1:n, :]
    off = SSD_HEADS

    def w_of(h):
        j = off + h
        seg = ex_row[j:j + 1, :] - ex_col[:, j:j + 1]
        return jnp.exp(jnp.where(anti, seg, NEG_BIG)) * dt_row[j:j + 1, :]

    def q_scale_of(h):
        j = off + h
        return jnp.exp(tot_col[:, j:j + 1] - ex_col[:, j:j + 1])

    def k_scale_of(h):
        j = off + h
        return dt_row[j:j + 1, :] * jnp.exp(ex_row[j:j + 1, :])

    def carry_of(h):
        j = off + h
        return jnp.exp(tot_col[:, j:j + 1])

    y_b = _ssd_scan_chunk(xs, bmat, cmat, w_of, q_scale_of, k_scale_of, carry_of, s_ref)
    y = yf_ref[...] + y_b + dskip_ref[...] * xs
    y = y * _silu(z_ref[...])
    ms = jnp.mean(y * y, axis=-1, keepdims=True)
    y_ref[...] = y * lax.rsqrt(ms + NORM_EPS) * nw_ref[...]

    @pl.when(c == nc - 1)
    def _():
        sfin_ref[0] = s_ref[...]


def _pair_states(s):
    b, h, n, p = s.shape
    return s.reshape(b, h // 2, 2, n, p).transpose(0, 1, 3, 2, 4).reshape(b, h // 2, n, 2 * p)


def _unpair_states(s):
    b, hp, n, p2 = s.shape
    return s.reshape(b, hp, n, 2, p2 // 2).transpose(0, 1, 3, 2, 4).reshape(b, hp * 2, n, p2 // 2)


def _ssd(xbc, dt, z, s0_f, s0_b, conv_w, conv_b, dt_bias, a_log, d_skip, ssd_norm, seq):
    m, nx = xbc.shape
    nb = m // seq
    blk = min(SSD_BLOCK, seq)
    nc = seq // blk
    hb = blk // HALO
    n_halo = m // HALO
    pad16 = lambda a: jnp.pad(a.reshape(1, -1), ((0, 0), (0, LANES - a.size)))
    bias = pad16(dt_bias)
    alog = pad16(a_log)
    state_spec = pl.BlockSpec((1, SSD_PAIRS, D_STATE, LANES), lambda b, c: (b, 0, 0, 0))
    state_shape = jax.ShapeDtypeStruct((nb, SSD_PAIRS, D_STATE, LANES), F32)
    row = lambda width: pl.BlockSpec((1, width), lambda b, c: (0, 0))

    def fwd_rows(width):
        return pl.BlockSpec((blk, width), lambda b, c: (b * nc + c, 0))

    def bwd_rows(width):
        return pl.BlockSpec((blk, width), lambda b, c: (b * nc + nc - 1 - c, 0))

    y_f, xc, s_f = pl.pallas_call(
        _ssd_fwd_kernel,
        grid=(nb, nc),
        in_specs=[fwd_rows(nx),
                  pl.BlockSpec((HALO, nx), lambda b, c: (jnp.maximum((b * nc + c) * hb - 1, 0), 0)),
                  pl.BlockSpec((HALO, nx), lambda b, c: (jnp.minimum((b * nc + c + 1) * hb, n_halo - 1), 0)),
                  fwd_rows(LANES), state_spec,
                  pl.BlockSpec((CONV_K, nx), lambda b, c: (0, 0)), row(nx), row(LANES), row(LANES)],
        out_specs=[fwd_rows(SSD_INNER), fwd_rows(nx), state_spec],
        out_shape=[jax.ShapeDtypeStruct((m, SSD_INNER), F32), jax.ShapeDtypeStruct((m, nx), F32), state_shape],
        scratch_shapes=[pltpu.VMEM((SSD_PAIRS, D_STATE, LANES), F32)],
        compiler_params=_cparams("arbitrary", "arbitrary"),
        name="ssd_forward",
    )(xbc, xbc, xbc, dt, _pair_states(s0_f), conv_w, conv_b.reshape(1, nx), bias, alog)

    dskip = jnp.repeat(d_skip, SSD_HEAD_DIM).reshape(1, SSD_INNER)
    y, s_b = pl.pallas_call(
        _ssd_bwd_kernel,
        grid=(nb, nc),
        in_specs=[bwd_rows(nx), bwd_rows(LANES), bwd_rows(SSD_INNER), bwd_rows(SSD_INNER), state_spec,
                  row(LANES), row(LANES), row(SSD_INNER), row(SSD_INNER)],
        out_specs=[bwd_rows(SSD_INNER), state_spec],
        out_shape=[jax.ShapeDtypeStruct((m, SSD_INNER), F32), state_shape],
        scratch_shapes=[pltpu.VMEM((SSD_PAIRS, D_STATE, LANES), F32)],
        compiler_params=_cparams("arbitrary", "arbitrary"),
        name="ssd_backward",
    )(xc, dt, y_f, z, _pair_states(s0_b), bias, alog, dskip, ssd_norm.reshape(1, SSD_INNER))
    return y, _unpair_states(s_f), _unpair_states(s_b)

def _hgrn_kernel(q_ref, ff_ref, fb_ref, i_ref, g_ref, lb_ref, s0_ref, nw_ref, o_ref, sfin_ref,
                 sf_ref, sb_ref, ob_ref, *, layer):
    t_len = q_ref.shape[0]
    n = HGRN_CHUNK
    n_chunks = t_len // n
    tn = (((0,), (0,)), ((), ()))
    nt = (((1,), (1,)), ((), ()))

    lbp = lb_ref[...]
    e = jnp.exp(lbp - jnp.max(lbp, axis=0, keepdims=True))
    sm = e / jnp.sum(e, axis=0, keepdims=True)
    lb = sm[0] * 0.0
    for j in range(1, layer + 1):
        lb = lb + sm[j]

    r = lax.broadcasted_iota(I32, (n, n), 0)
    c = lax.broadcasted_iota(I32, (n, n), 1)
    lower = jnp.where(c <= r, 1.0, 0.0).astype(F32)
    srow = lax.broadcasted_iota(I32, (n, HGRN_DK), 0)
    qscale = HGRN_DK ** -0.5

    def chunk(row0, f_ref, lb_d, reverse, s_ref):
        q = _silu(q_ref[pl.ds(row0, n), :]) * qscale
        f = f_ref[pl.ds(row0, n), :]
        v = i_ref[pl.ds(row0, n), :]
        k = (1.0 - lb_d) * jax.nn.sigmoid(-f)
        lf = jnp.log(lb_d + (1.0 - lb_d) * jax.nn.sigmoid(f))
        cum = jnp.dot(lower, lf, precision=HIGHEST, preferred_element_type=F32)
        tot = cum[n - 1:n, :]
        if reverse:
            cum = cum - lf
        rows = []
        for t in range(n):
            tile0 = (t // SUBLANES) * SUBLANES
            lo, hi = (tile0, n) if reverse else (0, tile0 + SUBLANES)
            cum_s = cum[lo:hi]
            if reverse:
                seg = jnp.where(srow[lo:hi] >= t, cum_s - cum[t:t + 1, :], NEG_BIG)
            else:
                seg = jnp.where(srow[lo:hi] <= t, cum[t:t + 1, :] - cum_s, NEG_BIG)
            a = q[t:t + 1, :] * k[lo:hi] * jnp.exp(seg)
            sc = jnp.sum(a, axis=1, keepdims=True)
            rows.append(jnp.sum(sc * v[lo:hi], axis=0, keepdims=True))
        o = jnp.concatenate(rows, axis=0)
        s_old = s_ref[...]
        if reverse:
            q_in = q * jnp.exp(tot - cum)
            k_out = k * jnp.exp(cum)
        else:
            q_in = q * jnp.exp(cum)
            k_out = k * jnp.exp(tot - cum)
        o = o + lax.dot_general(q_in.astype(BF16), s_old.astype(BF16), nt, preferred_element_type=F32)
        s_ref[...] = jnp.exp(tot) * s_old + lax.dot_general(
            v.astype(BF16), k_out.astype(BF16), tn, preferred_element_type=F32)
        return o

    sf_ref[...] = s0_ref[0, 0, 0].T
    sb_ref[...] = s0_ref[0, 1, 0].T

    def body(ci, carry):
        row_f = pl.multiple_of(ci * n, n)
        row_b = pl.multiple_of((n_chunks - 1 - ci) * n, n)
        o_ref[pl.ds(row_f, n), :] = chunk(row_f, ff_ref, lb[0:1, :], False, sf_ref)
        ob_ref[pl.ds(row_b, n), :] = chunk(row_b, fb_ref, lb[1:2, :], True, sb_ref)
        return carry

    lax.fori_loop(0, n_chunks, body, 0)
    sfin_ref[0, 0, 0] = sf_ref[...].T
    sfin_ref[0, 1, 0] = sb_ref[...].T

    nw = nw_ref[...]
    blk = min(t_len, ROW_TILE)

    def finish(bi, carry):
        row0 = pl.multiple_of(bi * blk, blk)
        o = o_ref[pl.ds(row0, blk), :] + ob_ref[pl.ds(row0, blk), :]
        ms = jnp.mean(o * o, axis=-1, keepdims=True)
        o = o * lax.rsqrt(ms + NORM_EPS) * nw
        o_ref[pl.ds(row0, blk), :] = o * _silu(g_ref[pl.ds(row0, blk), :])
        return carry

    lax.fori_loop(0, t_len // blk, finish, 0)


def _hgrn(q, f_fw, f_bw, iv, g, o_lb, state0, g_norm, seq, layer):
    m, width = q.shape
    nb = m // seq
    dv = width // HGRN_HEADS
    col = pl.BlockSpec((seq, dv), lambda b, h: (b, h))
    state_spec = pl.BlockSpec((1, 2, 1, HGRN_DK, dv), lambda b, h: (b, 0, h, 0, 0))
    return pl.pallas_call(
        functools.partial(_hgrn_kernel, layer=layer),
        grid=(nb, HGRN_HEADS),
        in_specs=[col, col, col, col, col,
                  pl.BlockSpec((o_lb.shape[0], 2, HGRN_DK), lambda b, h: (0, 0, h)),
                  state_spec,
                  pl.BlockSpec((1, dv), lambda b, h: (0, 0))],
        out_specs=[col, state_spec],
        out_shape=[jax.ShapeDtypeStruct((m, width), F32),
                   jax.ShapeDtypeStruct((nb, 2, HGRN_HEADS, HGRN_DK, dv), F32)],
        scratch_shapes=[pltpu.VMEM((dv, HGRN_DK), F32), pltpu.VMEM((dv, HGRN_DK), F32),
                        pltpu.VMEM((seq, dv), F32)],
        compiler_params=_cparams("arbitrary", "arbitrary"),
        name="hgrn2",
    )(q, f_fw, f_bw, iv, g, o_lb, state0, g_norm.reshape(1, dv))

EVEN_SPLITS = ((0, 512), (512, 640), (640, 768), (768, 1280), (1280, 2048), (2048, 2176))
HGRN_SPLITS = tuple((i * 1024, (i + 1) * 1024) for i in range(5))


def _even_weight(w):
    main = EVEN_SPLITS[-1][0]
    return jnp.pad(w, ((0, 0), (0, LANES - (w.shape[1] - main)))).astype(BF16)


def _run_trunk(x3, mods, mod_row0, P, cache, sc_order=None):
    nb, seq, d = x3.shape
    x = x3.reshape(nb * seq, d)
    depth = P['norm_mix'].shape[0]
    ks, vs, ssd_states, hgrn_states = [], [], [], []
    for l in range(depth):
        j = l // 2
        row0 = (l * SUBLANES + mod_row0, 0 if cache is None else 1)
        if l % 2 == 0:
            q, k, v, z, xbc, dt = _inproj(x, mods, row0, seq, P['norm_mix'][l], P['e_w_in'][j],
                                          EVEN_SPLITS, "even_in_proj")
            q, k = _qkprep(q, k, P['e_q_norm'][j], P['e_k_norm'][j], seq, rope=cache is not None)
            if cache is None:
                s0_f = jnp.zeros((nb, SSD_HEADS, D_STATE, SSD_HEAD_DIM), F32)
                s0_b = s0_f
                o_attn = _ctx_attention(q, k, v, P['e_sink'][j], seq)
            else:
                s0_f, s0_b = cache[2][:, j, 0], cache[2][:, j, 1]
                n_ctx = cache[0].shape[2]
                o_attn = _lat_attention(q, k, v, cache[0][:, j].reshape(nb, n_ctx, -1),
                                        cache[1][:, j].reshape(nb, n_ctx, -1), P['e_sink'][j], seq)
            y, s_f, s_b = _ssd(xbc, dt, z, s0_f, s0_b, P['e_conv_w'][j], P['e_conv_b'][j],
                               P['e_dt_bias'][j], P['e_a_log'][j], P['e_d_skip'][j], P['e_ssd_norm'][j], seq)
            if cache is None:
                ks.append(k.reshape(nb, seq, N_KV_HEADS, HEAD_DIM))
                vs.append(v.reshape(nb, seq, N_KV_HEADS, HEAD_DIM))
                ssd_states.append(jnp.stack([s_f, s_b], axis=1))
            mix = jnp.concatenate([o_attn, y], axis=1)
            x = _outproj(mix, x, mods, row0, seq, P['e_w_out'][j], "even_out_proj")
        else:
            q, f_fw, f_bw, iv, g = _inproj(x, mods, row0, seq, P['norm_mix'][l], P['o_w_in'][j],
                                           HGRN_SPLITS, "odd_in_proj")
            if cache is None:
                s0 = jnp.zeros((nb, 2, HGRN_HEADS, HGRN_DK, d // HGRN_HEADS), F32)
            else:
                s0 = cache[3][:, j]
            o, s_new = _hgrn(q, f_fw, f_bw, iv, g, P['o_lb'], s0, P['o_g_norm'][j], seq, j)
            if cache is None:
                hgrn_states.append(s_new)
            x = _outproj(o, x, mods, row0, seq, P['o_w_out'][j], "odd_out_proj")
        x, y_sc = _peer(x, mods, row0, seq, l, P['norm_ffn'][l], P['p_w_q'][l], P['p_sub_keys'][l],
                        P['p_u'], P['p_v'],
                        (SC_SHARE_CONTEXT if cache is None else SC_SHARE_LATENT)[l],
                        after=None if sc_order is None or cache is None else sc_order[l])
        if sc_order is not None and cache is None:
            sc_order.append(y_sc)
    y = x.reshape(nb, seq, d)
    if cache is not None:
        return y, None
    return y, (jnp.stack(ks, axis=1), jnp.stack(vs, axis=1),
               jnp.stack(ssd_states, axis=1), jnp.stack(hgrn_states, axis=1))


def kernel(x_prompt, x_sample, cache_k, cache_v, state_ssd, state_hgrn, c, c_ctx, w_ada, b_ada, norm_mix, norm_ffn, e_w_in, e_q_norm, e_k_norm, e_sink, e_conv_w, e_conv_b, e_dt_bias, e_a_log, e_d_skip, e_ssd_norm, e_w_out, o_w_in, o_lb, o_g_norm, o_w_out, p_w_q, p_sub_keys, p_u, p_v):
    depth, d, d6 = w_ada.shape
    b_lat = x_sample.shape[0]
    cond_rows = jnp.concatenate([c_ctx[None, :], c, jnp.zeros((SUBLANES - 1 - b_lat, d), F32)], axis=0)
    mods = _modulation(cond_rows, w_ada, b_ada).reshape(depth * SUBLANES, 1, d6)
    P = {
        'norm_mix': norm_mix, 'norm_ffn': norm_ffn,
        'e_w_in': jnp.stack([_even_weight(w) for w in e_w_in]), 'e_q_norm': e_q_norm, 'e_k_norm': e_k_norm,
        'e_sink': e_sink, 'e_conv_w': e_conv_w, 'e_conv_b': e_conv_b, 'e_dt_bias': e_dt_bias,
        'e_a_log': e_a_log, 'e_d_skip': e_d_skip, 'e_ssd_norm': e_ssd_norm,
        'e_w_out': e_w_out.astype(BF16),
        'o_w_in': o_w_in.astype(BF16), 'o_lb': o_lb, 'o_g_norm': o_g_norm, 'o_w_out': o_w_out.astype(BF16),
        'p_w_q': p_w_q.astype(BF16),
        'p_sub_keys': p_sub_keys.astype(BF16).reshape(depth, PEER_HEADS * 2, PEER_NKEYS, PEER_DKEY),
        'p_u': p_u, 'p_v': p_v,
    }
    sc_order = []
    y_prompt, new_state = _run_trunk(x_prompt, mods, 0, P, None, sc_order)
    y_sample, _ = _run_trunk(x_sample, mods, 1, P, (cache_k, cache_v, state_ssd, state_hgrn), sc_order)
    return (y_prompt, y_sample) + new_state
```

```python
import functools
import math

import jax
import jax.numpy as jnp
from jax import lax
from jax.experimental import pallas as pl
from jax.experimental.pallas import tpu as pltpu
from jax.experimental.pallas import tpu_sc as plsc

F32 = jnp.float32
BF16 = jnp.bfloat16
I32 = jnp.int32
HIGHEST = lax.Precision.HIGHEST

NORM_EPS = 1e-6
NEG_BIG = -1e30
LANES = 128
SUBLANES = 8
VMEM_LIMIT = 48 * 1024 * 1024

GRID_W = 64
HEAD_DIM = 64
N_Q_HEADS = 8
N_KV_HEADS = 2
GQA_GROUP = 4
WINDOW = 128
ROPE_THETA = 10000.0
SSD_HEADS = 8
SSD_HEAD_DIM = 64
SSD_GROUPS = 2
D_STATE = 64
CONV_K = 5
HGRN_HEADS = 8
HGRN_DK = 128
HGRN_CHUNK = 64
PEER_HEADS = 8
PEER_NKEYS = 128
PEER_TOPK = 16
PEER_DKEY = 128
PEER_PAIRS = PEER_HEADS * PEER_TOPK

ROW_TILE = 256
PEER_BLOCK = 128
SC_LANES = 16
SC_CHUNK = 16
SC_SLOTS = 3
SC_SHARE_CONTEXT = ((1, 1), (1, 1), (1, 1), (1, 1))
SC_SHARE_LATENT = ((27, 32), (27, 32), (27, 32), (3, 4))
GELU_C = math.sqrt(2.0 / math.pi)


def _cparams(*sem):
    return pltpu.CompilerParams(dimension_semantics=sem, vmem_limit_bytes=VMEM_LIMIT)


def _norm_mod(x, nw, scale, shift):
    ms = jnp.mean(x * x, axis=-1, keepdims=True)
    return (x * lax.rsqrt(ms + NORM_EPS)) * nw * (1.0 + scale) + shift


def _mod_kernel(c_ref, w_ref, b_ref, o_ref):
    c = c_ref[...]
    s = c * jax.nn.sigmoid(c)
    o_ref[0] = jnp.dot(s, w_ref[0], precision=HIGHEST, preferred_element_type=F32) + b_ref[0]


def _modulation(cond_rows, w_ada, b_ada):
    depth, d, n = w_ada.shape
    rows = cond_rows.shape[0]
    return pl.pallas_call(
        _mod_kernel,
        grid=(depth, n // d),
        in_specs=[pl.BlockSpec((rows, d), lambda l, j: (0, 0)),
                  pl.BlockSpec((1, d, d), lambda l, j: (l, 0, j)),
                  pl.BlockSpec((1, 1, d), lambda l, j: (l, 0, j))],
        out_specs=pl.BlockSpec((1, rows, d), lambda l, j: (l, 0, j)),
        out_shape=jax.ShapeDtypeStruct((depth, rows, n), F32),
        compiler_params=_cparams("arbitrary", "arbitrary"),
        name="modulation",
    )(cond_rows, w_ada, b_ada.reshape(depth, 1, n))


def _mod_spec(mod_row0, seq, tile, d6, first_tile=0):
    row0, per_batch = mod_row0
    return pl.BlockSpec((1, 1, d6),
                        lambda i: (row0 + per_batch * (((i + first_tile) * tile) // seq), 0, 0))


def _inproj_kernel(x_ref, m_ref, nw_ref, w_ref, *o_refs, splits, d):
    m = m_ref[0]
    h = _norm_mod(x_ref[...], nw_ref[...], m[:, d:2 * d], m[:, 0:d]).astype(BF16)
    for o_ref, (a, b) in zip(o_refs, splits):
        o_ref[...] = jnp.dot(h, w_ref[:, a:b], preferred_element_type=F32)


def _inproj(x, mods, mod_row0, seq, nw, w_bf16, splits, name):
    m, d = x.shape
    n = w_bf16.shape[1]
    tile = min(ROW_TILE, seq)
    return pl.pallas_call(
        functools.partial(_inproj_kernel, splits=splits, d=d),
        grid=(m // tile,),
        in_specs=[pl.BlockSpec((tile, d), lambda i: (i, 0)),
                  _mod_spec(mod_row0, seq, tile, mods.shape[-1]),
                  pl.BlockSpec((1, d), lambda i: (0, 0)),
                  pl.BlockSpec((d, n), lambda i: (0, 0))],
        out_specs=[pl.BlockSpec((tile, b - a), lambda i: (i, 0)) for a, b in splits],
        out_shape=[jax.ShapeDtypeStruct((m, b - a), F32) for a, b in splits],
        compiler_params=_cparams("arbitrary"),
        name=name,
    )(x, mods, nw.reshape(1, d), w_bf16)


def _outproj_kernel(mix_ref, x_ref, m_ref, w_ref, o_ref, *, d):
    y = jnp.dot(mix_ref[...].astype(BF16), w_ref[...], preferred_element_type=F32)
    o_ref[...] = x_ref[...] + m_ref[0][:, 2 * d:3 * d] * y


def _outproj(mix, x, mods, mod_row0, seq, w_bf16, name):
    m, d = x.shape
    k = mix.shape[1]
    tile = min(ROW_TILE, seq)
    return pl.pallas_call(
        functools.partial(_outproj_kernel, d=d),
        grid=(m // tile,),
        in_specs=[pl.BlockSpec((tile, k), lambda i: (i, 0)),
                  pl.BlockSpec((tile, d), lambda i: (i, 0)),
                  _mod_spec(mod_row0, seq, tile, mods.shape[-1]),
                  pl.BlockSpec((k, d), lambda i: (0, 0))],
        out_specs=pl.BlockSpec((tile, d), lambda i: (i, 0)),
        out_shape=jax.ShapeDtypeStruct((m, d), F32),
        compiler_params=_cparams("arbitrary"),
        name=name,
    )(mix, x, mods, w_bf16)


def _topk_over_rows(s, k, payload=None):
    n = s.shape[0]
    iota = lax.broadcasted_iota(I32, s.shape, 0)
    vals, idxs, pays = [], [], []
    for _ in range(k):
        m = jnp.max(s, axis=0, keepdims=True)
        i = jnp.min(jnp.where(s == m, iota, n), axis=0, keepdims=True)
        hit = iota == i
        vals.append(m)
        idxs.append(i)
        if payload is not None:
            pays.append(jnp.max(jnp.where(hit, payload, -1), axis=0, keepdims=True))
        s = jnp.where(hit, -jnp.inf, s)
    out = (jnp.concatenate(vals, axis=0), jnp.concatenate(idxs, axis=0))
    if payload is not None:
        out += (jnp.concatenate(pays, axis=0),)
    return out


def _peer_route_kernel(x_ref, m_ref, nw_ref, wq_ref, keys_ref, h_ref, e_ref, g_ref, *, d):
    m = m_ref[0]
    h = _norm_mod(x_ref[...], nw_ref[...], m[:, 4 * d:5 * d], m[:, 3 * d:4 * d])
    h_ref[...] = h
    hb = h.astype(BF16)
    nt = (((1,), (1,)), ((), ()))
    for head in range(PEER_HEADS):
        tops = []
        for half in range(2):
            c0 = (head * 2 + half) * PEER_DKEY
            q = jnp.dot(hb, wq_ref[:, c0:c0 + PEER_DKEY], preferred_element_type=F32)
            s = lax.dot_general(keys_ref[head * 2 + half], q.astype(BF16), nt,
                                preferred_element_type=F32)
            tops.append(_topk_over_rows(s, PEER_TOPK))
        (s0, i0), (s1, i1) = tops
        widths = [PEER_TOPK // (a + 1) for a in range(PEER_TOPK)]
        n_pad = -sum(widths) % SUBLANES
        cand_s = jnp.concatenate([s0[a:a + 1] + s1[:w] for a, w in enumerate(widths)]
                                 + [jnp.full((n_pad, s0.shape[1]), -jnp.inf, F32)], axis=0)
        cand_e = jnp.concatenate([i0[a:a + 1] * PEER_NKEYS + i1[:w] for a, w in enumerate(widths)]
                                 + [jnp.zeros((n_pad, s0.shape[1]), I32)], axis=0)
        best_s, _, best_e = _topk_over_rows(cand_s, PEER_TOPK, payload=cand_e)
        p = jnp.exp(best_s - best_s[0:1])
        r0 = head * PEER_TOPK
        e_ref[r0:r0 + PEER_TOPK, :] = best_e
        g_ref[r0:r0 + PEER_TOPK, :] = p / jnp.sum(p, axis=0, keepdims=True)


def _peer_route(x, mods, mod_row0, seq, nw, wq_bf16, keys_bf16):
    m, d = x.shape
    tile = min(ROW_TILE, seq)
    nq = wq_bf16.shape[1]
    return pl.pallas_call(
        functools.partial(_peer_route_kernel, d=d),
        grid=(m // tile,),
        in_specs=[pl.BlockSpec((tile, d), lambda i: (i, 0)),
                  _mod_spec(mod_row0, seq, tile, mods.shape[-1]),
                  pl.BlockSpec((1, d), lambda i: (0, 0)),
                  pl.BlockSpec((d, nq), lambda i: (0, 0)),
                  pl.BlockSpec(keys_bf16.shape, lambda i: (0, 0, 0))],
        out_specs=[pl.BlockSpec((tile, d), lambda i: (i, 0)),
                   pl.BlockSpec((PEER_PAIRS, tile), lambda i: (0, i)),
                   pl.BlockSpec((PEER_PAIRS, tile), lambda i: (0, i))],
        out_shape=[jax.ShapeDtypeStruct((m, d), F32),
                   jax.ShapeDtypeStruct((PEER_PAIRS, m), I32),
                   jax.ShapeDtypeStruct((PEER_PAIRS, m), F32)],
        compiler_params=_cparams("arbitrary"),
        name="peer_route",
    )(x, mods, nw.reshape(1, d), wq_bf16, keys_bf16)


def _peer_gather_kernel(idx_hbm, h_ref, g_ref, x_ref, m_ref, u_hbm, v_hbm, o_ref,
                        idx_smem, ubuf, vbuf, sem_idx, sem_u, sem_v, *, layer, d):
    blk = pl.program_id(0)
    n_groups = PEER_BLOCK // SUBLANES

    cp = pltpu.make_async_copy(idx_hbm.at[blk], idx_smem, sem_idx)
    cp.start()
    cp.wait()

    def issue(tok, slot):
        def body(pair, carry):
            e = idx_smem[tok, pair]
            pltpu.make_async_copy(u_hbm.at[layer, pl.ds(e, 1)], ubuf.at[slot, pl.ds(pair, 1)],
                                  sem_u.at[slot]).start()
            pltpu.make_async_copy(v_hbm.at[layer, pl.ds(e, 1)], vbuf.at[slot, pl.ds(pair, 1)],
                                  sem_v.at[slot]).start()
            return carry
        lax.fori_loop(0, PEER_PAIRS, body, 0, unroll=8)

    def wait(slot):
        pltpu.make_async_copy(u_hbm.at[layer, pl.ds(0, PEER_PAIRS)], ubuf.at[slot], sem_u.at[slot]).wait()
        pltpu.make_async_copy(v_hbm.at[layer, pl.ds(0, PEER_PAIRS)], vbuf.at[slot], sem_v.at[slot]).wait()

    gate2 = m_ref[0][:, 5 * d:6 * d]
    lane = lax.broadcasted_iota(I32, (PEER_PAIRS, PEER_BLOCK), 1)

    issue(0, 0)

    def group(grp, carry):
        base = pl.multiple_of(grp * SUBLANES, SUBLANES)
        h8 = h_ref[pl.ds(base, SUBLANES), :]
        rows = []
        for r in range(SUBLANES):
            tok = base + r
            slot = r % 2
            if r < SUBLANES - 1:
                issue(tok + 1, 1 - slot)
            else:
                @pl.when(grp < n_groups - 1)
                def _():
                    issue(tok + 1, 1 - slot)
            wait(slot)
            act = jnp.sum(ubuf[slot] * h8[r:r + 1, :], axis=1, keepdims=True)
            gate = jnp.sum(jnp.where(lane == tok, g_ref[...], 0.0), axis=1, keepdims=True)
            w = jax.nn.gelu(act) * gate
            rows.append(jnp.sum(vbuf[slot] * w, axis=0, keepdims=True))
        out8 = jnp.concatenate(rows, axis=0)
        o_ref[pl.ds(base, SUBLANES), :] = x_ref[pl.ds(base, SUBLANES), :] + gate2 * out8
        return carry

    lax.fori_loop(0, n_groups, group, 0)


def _peer_gather(idx, h, gates, x, mods, mod_row0, seq, p_u, p_v, layer, first_block):
    m, d = x.shape
    nblk = m // PEER_BLOCK - first_block
    idx3 = idx[:, first_block * PEER_BLOCK:].T.reshape(nblk, PEER_BLOCK, PEER_PAIRS)
    rows = pl.BlockSpec((PEER_BLOCK, d), lambda i: (i + first_block, 0))
    return pl.pallas_call(
        functools.partial(_peer_gather_kernel, layer=layer, d=d),
        grid=(nblk,),
        in_specs=[pl.BlockSpec(memory_space=pl.ANY),
                  rows,
                  pl.BlockSpec((PEER_PAIRS, PEER_BLOCK), lambda i: (0, i + first_block)),
                  rows,
                  _mod_spec(mod_row0, seq, PEER_BLOCK, mods.shape[-1], first_block),
                  pl.BlockSpec(memory_space=pl.ANY),
                  pl.BlockSpec(memory_space=pl.ANY)],
        out_specs=pl.BlockSpec((PEER_BLOCK, d), lambda i: (i, 0)),
        out_shape=jax.ShapeDtypeStruct((nblk * PEER_BLOCK, d), F32),
        scratch_shapes=[pltpu.SMEM((PEER_BLOCK, PEER_PAIRS), I32),
                        pltpu.VMEM((2, PEER_PAIRS, d), F32),
                        pltpu.VMEM((2, PEER_PAIRS, d), F32),
                        pltpu.SemaphoreType.DMA,
                        pltpu.SemaphoreType.DMA((2,)),
                        pltpu.SemaphoreType.DMA((2,))],
        compiler_params=_cparams("arbitrary"),
        name="peer_gather",
    )(idx3, h, gates, x, mods, p_u, p_v)


def _peer_experts_sc(idx, h, gates, u_rows, v_rows):
    m = idx.shape[0]
    d = h.shape[1]
    info = plsc.get_sparse_core_info()
    n_workers = info.num_cores * info.num_subcores
    per = m // n_workers
    n_chunks = PEER_PAIRS // SC_CHUNK
    n_vec = d // SC_LANES
    mesh = plsc.VectorSubcoreMesh(core_axis_name="c", subcore_axis_name="s")

    @functools.partial(
        pl.kernel, out_type=jax.ShapeDtypeStruct((m, d), F32), mesh=mesh,
        scratch_types=[pltpu.VMEM((2, PEER_PAIRS), I32), pltpu.VMEM((2, d), F32),
                       pltpu.VMEM((2, PEER_PAIRS), F32), pltpu.VMEM((2, d), F32),
                       pltpu.VMEM((SC_SLOTS, SC_CHUNK, d), F32), pltpu.VMEM((SC_SLOTS, SC_CHUNK, d), F32),
                       pltpu.SemaphoreType.DMA((2,)), pltpu.SemaphoreType.DMA((2,)),
                       pltpu.SemaphoreType.DMA((SC_SLOTS,)), pltpu.SemaphoreType.DMA((SC_SLOTS,))],
        compiler_params=pltpu.CompilerParams(needs_layout_passes=False),
        name="peer_experts_sc")
    def body(idx_hbm, h_hbm, g_hbm, u_hbm, v_hbm, o_hbm,
             idx_v, x_v, g_v, out_v, ubuf, vbuf, sem_meta, sem_out, sem_u, sem_v):
        wid = lax.axis_index("c") * info.num_subcores + lax.axis_index("s")
        tok0 = wid * per
        lane = lax.iota(I32, SC_LANES)

        def meta_copies(ti, ms):
            t = tok0 + ti
            return (pltpu.make_async_copy(idx_hbm.at[t], idx_v.at[ms], sem_meta.at[ms]),
                    pltpu.make_async_copy(h_hbm.at[t], x_v.at[ms], sem_meta.at[ms]),
                    pltpu.make_async_copy(g_hbm.at[t], g_v.at[ms], sem_meta.at[ms]))

        def gather_copies(ms, c, slot):
            ids = idx_v.at[ms, pl.ds(c * SC_CHUNK, SC_CHUNK)]
            return (pltpu.make_async_copy(u_hbm.at[ids], ubuf.at[slot], sem_u.at[slot]),
                    pltpu.make_async_copy(v_hbm.at[ids], vbuf.at[slot], sem_v.at[slot]))

        def out_copy(ti, ms):
            return pltpu.make_async_copy(out_v.at[ms], o_hbm.at[tok0 + ti], sem_out.at[ms])

        for cp in meta_copies(0, 0):
            cp.start()
        for cp in meta_copies(0, 0):
            cp.wait()
        for c in range(SC_SLOTS - 1):
            for cp in gather_copies(0, c, c):
                cp.start()

        def token(ti, carry):
            ms = ti % 2
            nxt = 1 - ms

            @pl.when(ti + 1 < per)
            def _():
                for cp in meta_copies(ti + 1, nxt):
                    cp.start()

            @pl.when(ti >= 2)
            def _():
                out_copy(ti - 2, ms).wait()

            def zero(j, c):
                out_v[ms, pl.ds(j * SC_LANES, SC_LANES)] = jnp.zeros((SC_LANES,), F32)
                return c
            lax.fori_loop(0, n_vec, zero, 0)

            for c in range(n_chunks):
                g = ti * n_chunks + c
                slot = lax.rem(g, SC_SLOTS)
                ahead = c + SC_SLOTS - 1
                ahead_slot = lax.rem(g + SC_SLOTS - 1, SC_SLOTS)
                if ahead < n_chunks:
                    for cp in gather_copies(ms, ahead, ahead_slot):
                        cp.start()
                else:
                    @pl.when(ti + 1 < per)
                    def _():
                        if ahead == n_chunks:
                            for cp in meta_copies(ti + 1, nxt):
                                cp.wait()
                        for cp in gather_copies(nxt, ahead - n_chunks, ahead_slot):
                            cp.start()
                cu, cv = gather_copies(ms, c, slot)
                cu.wait()

                def udot(j, accs):
                    xj = x_v[ms, pl.ds(j * SC_LANES, SC_LANES)]
                    return tuple(accs[r] + ubuf[slot, r, pl.ds(j * SC_LANES, SC_LANES)] * xj
                                 for r in range(SC_CHUNK))
                accs = lax.fori_loop(0, n_vec, udot,
                                     tuple(jnp.zeros((SC_LANES,), F32) for _ in range(SC_CHUNK)))
                act = jnp.zeros((SC_LANES,), F32)
                for r in range(SC_CHUNK):
                    act = jnp.where(lane == r, jnp.sum(accs[r]), act)
                y = GELU_C * (act + 0.044715 * (act * act * act))
                w = act / (1.0 + jnp.exp(-2.0 * y)) * g_v[ms, pl.ds(c * SC_CHUNK, SC_CHUNK)]
                ws = [jnp.sum(jnp.where(lane == r, w, 0.0)) for r in range(SC_CHUNK)]
                cv.wait()

                @plsc.parallel_loop(0, n_vec, unroll=2)
                def _(j):
                    parts = [ws[r] * vbuf[slot, r, pl.ds(j * SC_LANES, SC_LANES)] for r in range(SC_CHUNK)]
                    while len(parts) > 1:
                        parts = [parts[i] + parts[i + 1] for i in range(0, len(parts), 2)]
                    plsc.addupdate(out_v.at[ms, pl.ds(j * SC_LANES, SC_LANES)], parts[0])

            out_copy(ti, ms).start()
            return carry

        lax.fori_loop(0, per, token, 0)
        for back in (2, 1):
            if per >= back:
                out_copy(per - back, (per - back) % 2).wait()

    return body(idx, h, gates, u_rows, v_rows)


def _residual_kernel(x_ref, y_ref, m_ref, o_ref, *, d):
    o_ref[...] = x_ref[...] + m_ref[0][:, 5 * d:6 * d] * y_ref[...]


def _residual(x, y, mods, mod_row0, seq):
    m, d = y.shape
    tile = min(ROW_TILE, seq)
    return pl.pallas_call(
        functools.partial(_residual_kernel, d=d),
        grid=(m // tile,),
        in_specs=[pl.BlockSpec((tile, d), lambda i: (i, 0)),
                  pl.BlockSpec((tile, d), lambda i: (i, 0)),
                  _mod_spec(mod_row0, seq, tile, mods.shape[-1])],
        out_specs=pl.BlockSpec((tile, d), lambda i: (i, 0)),
        out_shape=jax.ShapeDtypeStruct((m, d), F32),
        compiler_params=_cparams("arbitrary"),
        name="peer_residual",
    )(x, y, mods)


def _peer(x, mods, mod_row0, seq, layer, nw, wq_bf16, keys_bf16, p_u, p_v, sc_share, after=None):
    m, d = x.shape
    h, idx, gates = _peer_route(x, mods, mod_row0, seq, nw, wq_bf16, keys_bf16)
    n_blocks = m // PEER_BLOCK
    sc_blocks = (n_blocks * sc_share[0]) // sc_share[1]
    m_sc = sc_blocks * PEER_BLOCK
    n_experts = p_u.shape[1]
    idx_sc = idx[:, :m_sc].T + layer * n_experts
    if after is not None:
        idx_sc, _ = lax.optimization_barrier((idx_sc, after))
    y_sc = _peer_experts_sc(idx_sc, h, gates[:, :m_sc].T,
                            p_u.reshape(-1, d), p_v.reshape(-1, d))
    x_sc = _residual(x, y_sc, mods, mod_row0, seq)
    if sc_blocks == n_blocks:
        return x_sc, y_sc
    x_tc = _peer_gather(idx, h, gates, x, mods, mod_row0, seq, p_u, p_v, layer, sc_blocks)
    return jnp.concatenate([x_sc, x_tc], axis=0), y_sc


def _head_mean_square(x):
    n = x.shape[1]
    r = lax.broadcasted_iota(I32, (n, n), 0) // HEAD_DIM
    c = lax.broadcasted_iota(I32, (n, n), 1) // HEAD_DIM
    seg = jnp.where(r == c, 1.0 / HEAD_DIM, 0.0).astype(F32)
    return jnp.dot(x * x, seg, precision=HIGHEST, preferred_element_type=F32)


def _swap_rot_halves(x):
    n = x.shape[1]
    quarter = HEAD_DIM // 4
    lane = lax.broadcasted_iota(I32, x.shape, 1)
    lo = (lane % (2 * quarter)) < quarter
    return jnp.where(lo, pltpu.roll(x, n - quarter, axis=1), pltpu.roll(x, quarter, axis=1))


def _qkprep_kernel(q_ref, k_ref, qw_ref, kw_ref, *rest, rope):
    if rope:
        cos_ref, sin_ref, qo_ref, ko_ref = rest
    else:
        qo_ref, ko_ref = rest
    q = q_ref[...]
    k = k_ref[...]
    q = q * lax.rsqrt(_head_mean_square(q) + NORM_EPS) * qw_ref[...]
    k = k * lax.rsqrt(_head_mean_square(k) + NORM_EPS) * kw_ref[...]
    if rope:
        cos = cos_ref[...]
        sin = sin_ref[...]
        cq = jnp.concatenate([cos] * (q.shape[1] // LANES), axis=1)
        sq = jnp.concatenate([sin] * (q.shape[1] // LANES), axis=1)
        q = q * cq + _swap_rot_halves(q) * sq
        k = k * cos + _swap_rot_halves(k) * sin
    qo_ref[...] = q
    ko_ref[...] = k


def _rope_tables(seq):
    axis_dim = HEAD_DIM // 2
    inv_freq = ROPE_THETA ** (-jnp.arange(0, axis_dim, 2, dtype=F32) / axis_dim)
    t = jnp.arange(seq)
    pos = jnp.stack([(t // GRID_W).astype(F32), (t % GRID_W).astype(F32)], axis=1)
    lane = jnp.arange(LANES)
    dd = lane % HEAD_DIM
    ang = pos[:, dd // axis_dim] * inv_freq[dd % (axis_dim // 2)][None, :]
    sign = jnp.where((dd % axis_dim) < axis_dim // 2, -1.0, 1.0).astype(F32)
    return jnp.cos(ang), jnp.sin(ang) * sign[None, :]


def _qkprep(q, k, qw, kw, seq, rope):
    m, nq = q.shape
    nk = k.shape[1]
    tile = min(ROW_TILE, seq)
    qw_row = jnp.tile(qw, nq // HEAD_DIM).reshape(1, nq)
    kw_row = jnp.tile(kw, nk // HEAD_DIM).reshape(1, nk)
    in_specs = [pl.BlockSpec((tile, nq), lambda i: (i, 0)),
                pl.BlockSpec((tile, nk), lambda i: (i, 0)),
                pl.BlockSpec((1, nq), lambda i: (0, 0)),
                pl.BlockSpec((1, nk), lambda i: (0, 0))]
    args = [q, k, qw_row, kw_row]
    if rope:
        cos, sin = _rope_tables(seq)
        per_seq = seq // tile
        in_specs += [pl.BlockSpec((tile, LANES), lambda i: (i % per_seq, 0)),
                     pl.BlockSpec((tile, LANES), lambda i: (i % per_seq, 0))]
        args += [cos, sin]
    return pl.pallas_call(
        functools.partial(_qkprep_kernel, rope=rope),
        grid=(m // tile,),
        in_specs=in_specs,
        out_specs=[pl.BlockSpec((tile, nq), lambda i: (i, 0)),
                   pl.BlockSpec((tile, nk), lambda i: (i, 0))],
        out_shape=[jax.ShapeDtypeStruct((m, nq), F32), jax.ShapeDtypeStruct((m, nk), F32)],
        compiler_params=_cparams("arbitrary"),
        name="qk_prep",
    )(*args)


def _dup_halves(x):
    lane = lax.broadcasted_iota(I32, x.shape, 1)
    sw = pltpu.roll(x, HEAD_DIM, axis=1)
    lo = lane < HEAD_DIM
    return jnp.where(lo, x, sw), jnp.where(lo, sw, x)


def _attend(q, k_all, v_all, sink_ref, mask):
    scale = HEAD_DIM ** -0.5
    nt = (((1,), (1,)), ((), ()))
    kk = [a.astype(BF16) for a in _dup_halves(k_all)]
    vv = [a.astype(BF16) for a in _dup_halves(v_all)]
    lane = lax.broadcasted_iota(I32, (q.shape[0], LANES), 1)
    lo = lane < HEAD_DIM
    tiles = []
    for t in range(q.shape[1] // LANES):
        qt = q[:, t * LANES:(t + 1) * LANES]
        g = (2 * t) // GQA_GROUP
        halves = []
        for hh in range(2):
            head = 2 * t + hh
            qm = jnp.where(lo if hh == 0 else ~lo, qt, 0.0).astype(BF16)
            s = lax.dot_general(qm, kk[g], nt, preferred_element_type=F32) * scale
            if mask is not None:
                s = jnp.where(mask, s, NEG_BIG)
            sink = sink_ref[head]
            mx = jnp.maximum(jnp.max(s, axis=1, keepdims=True), sink)
            p = jnp.exp(s - mx)
            den = jnp.sum(p, axis=1, keepdims=True) + jnp.exp(sink - mx)
            p = (p / den).astype(BF16)
            halves.append(jnp.dot(p, vv[g], preferred_element_type=F32))
        tiles.append(jnp.where(lo, halves[0], halves[1]))
    return jnp.concatenate(tiles, axis=1)


def _ctx_attn_kernel(sink_ref, q_ref, k_ref, v_ref, o_ref):
    o_ref[...] = _attend(q_ref[...], k_ref[...], v_ref[...], sink_ref, None)


def _ctx_attention(q, k, v, sink, seq):
    m, nq = q.shape
    nk = k.shape[1]
    return pl.pallas_call(
        _ctx_attn_kernel,
        grid=(m // seq,),
        in_specs=[pl.BlockSpec(memory_space=pltpu.SMEM),
                  pl.BlockSpec((seq, nq), lambda b: (b, 0)),
                  pl.BlockSpec((seq, nk), lambda b: (b, 0)),
                  pl.BlockSpec((seq, nk), lambda b: (b, 0))],
        out_specs=pl.BlockSpec((seq, nq), lambda b: (b, 0)),
        out_shape=jax.ShapeDtypeStruct((m, nq), F32),
        compiler_params=_cparams("arbitrary"),
        name="ctx_attention",
    )(sink, q, k, v)


def _lat_attn_kernel(sink_ref, q_ref, kc_ref, vc_ref, kp_ref, k0_ref, kn_ref, vp_ref, v0_ref, vn_ref,
                     o_ref, *, seq):
    qb = pl.program_id(1)
    blk = q_ref.shape[0]
    n_ctx = kc_ref.shape[1]
    k_all = jnp.concatenate([kc_ref[0], kp_ref[...], k0_ref[...], kn_ref[...]], axis=0)
    v_all = jnp.concatenate([vc_ref[0], vp_ref[...], v0_ref[...], vn_ref[...]], axis=0)
    tk = k_all.shape[0]
    qpos = qb * blk + lax.broadcasted_iota(I32, (blk, tk), 0)
    col = lax.broadcasted_iota(I32, (blk, tk), 1)
    kpos = (qb - 1) * blk + col - n_ctx
    local_ok = (jnp.abs(qpos - kpos) <= WINDOW) & (kpos >= 0) & (kpos < seq)
    mask = (col < n_ctx) | local_ok
    o_ref[...] = _attend(q_ref[...], k_all, v_all, sink_ref, mask)


def _lat_attention(q, k, v, k_ctx, v_ctx, sink, seq):
    m, nq = q.shape
    nk = k.shape[1]
    blk = WINDOW
    nb = seq // blk
    n_ctx = k_ctx.shape[1]
    last = m // blk - 1

    def kv_spec(shift):
        return pl.BlockSpec((blk, nk), lambda b, i: (jnp.clip(b * nb + i + shift, 0, last), 0))

    ctx_spec = pl.BlockSpec((1, n_ctx, nk), lambda b, i: (b, 0, 0))
    return pl.pallas_call(
        functools.partial(_lat_attn_kernel, seq=seq),
        grid=(m // seq, nb),
        in_specs=[pl.BlockSpec(memory_space=pltpu.SMEM),
                  pl.BlockSpec((blk, nq), lambda b, i: (b * nb + i, 0)),
                  ctx_spec, ctx_spec,
                  kv_spec(-1), kv_spec(0), kv_spec(1),
                  kv_spec(-1), kv_spec(0), kv_spec(1)],
        out_specs=pl.BlockSpec((blk, nq), lambda b, i: (b * nb + i, 0)),
        out_shape=jax.ShapeDtypeStruct((m, nq), F32),
        compiler_params=_cparams("arbitrary", "arbitrary"),
        name="lat_attention",
    )(sink, q, k_ctx, v_ctx, k, k, k, v, v, v)

SSD_BLOCK = 256
SSD_PAIRS = SSD_HEADS // 2
SSD_INNER = SSD_HEADS * SSD_HEAD_DIM
HALO = SUBLANES


def _softplus(x):
    return jnp.maximum(x, 0.0) + jnp.log1p(jnp.exp(-jnp.abs(x)))


def _silu(x):
    return x * jax.nn.sigmoid(x)


def _ssd_decays(dt_raw, bias, a_log):
    n = dt_raw.shape[0]
    dt = _softplus(dt_raw + bias)
    log_a = dt * (-jnp.exp(a_log))
    r = lax.broadcasted_iota(I32, (n, n), 0)
    c = lax.broadcasted_iota(I32, (n, n), 1)
    lower = jnp.where(c <= r, 1.0, 0.0).astype(F32)
    upper = jnp.where(r <= c, 1.0, 0.0).astype(F32)
    cum_col = jnp.dot(lower, log_a, precision=HIGHEST, preferred_element_type=F32)
    dt_row = dt.T
    la_row = log_a.T
    cum_row = jnp.dot(la_row, upper, precision=HIGHEST, preferred_element_type=F32)
    return dt, log_a, cum_col, dt_row, la_row, cum_row


def _ssd_scan_chunk(xs, bmat, cmat, w_of, q_scale_of, k_scale_of, carry_of, s_ref):
    nt = (((1,), (1,)), ((), ()))
    n = xs.shape[0]
    lane = lax.broadcasted_iota(I32, (n, LANES), 1)
    lo = lane < SSD_HEAD_DIM
    lane_s = lax.broadcasted_iota(I32, (D_STATE, LANES), 1)
    lo_s = lane_s < SSD_HEAD_DIM
    b_t = bmat.T
    cb16 = cmat.astype(BF16)
    ys = []
    for pair in range(SSD_PAIRS):
        g = (2 * pair) // (SSD_HEADS // SSD_GROUPS)
        in_g = (lane // D_STATE) == g
        cg = jnp.where(in_g, cmat, 0.0)
        cb = lax.dot_general(cg.astype(BF16), bmat.astype(BF16), nt, preferred_element_type=F32)
        x_pair = xs[:, pair * LANES:(pair + 1) * LANES]
        x16 = x_pair.astype(BF16)
        s_old = s_ref[pair]
        s2 = jnp.concatenate([s_old, s_old], axis=0).astype(BF16)
        bg_t = b_t[g * D_STATE:(g + 1) * D_STATE, :]
        y_h, s_h = [], []
        for hh in range(2):
            h = 2 * pair + hh
            w = (cb * w_of(h)).astype(BF16)
            y = jnp.dot(w, x16, preferred_element_type=F32)
            cq = (cg * q_scale_of(h)).astype(BF16)
            y = y + jnp.dot(cq, s2, preferred_element_type=F32)
            y_h.append(y)
            kt = (bg_t * k_scale_of(h)).astype(BF16)
            s_h.append(carry_of(h) * s_old + jnp.dot(kt, x16, preferred_element_type=F32))
        ys.append(jnp.where(lo, y_h[0], y_h[1]))
        s_ref[pair] = jnp.where(lo_s, s_h[0], s_h[1])
    return jnp.concatenate(ys, axis=1)


def _ssd_fwd_kernel(x_ref, xp_ref, xn_ref, dt_ref, s0_ref, cw_ref, cb_ref, bias_ref, alog_ref,
                    y_ref, xc_ref, sfin_ref, s_ref):
    c = pl.program_id(1)
    nc = pl.num_programs(1)
    n = x_ref.shape[0]

    @pl.when(c == 0)
    def _():
        s_ref[...] = s0_ref[0]

    prev = jnp.where(c > 0, xp_ref[...], 0.0)
    nxt = jnp.where(c < nc - 1, xn_ref[...], 0.0)
    xe = jnp.concatenate([prev, x_ref[...], nxt], axis=0)
    pad = (CONV_K - 1) // 2
    acc = cb_ref[...] + cw_ref[0:1, :] * xe[HALO - pad:HALO - pad + n, :]
    for k in range(1, CONV_K):
        acc = acc + cw_ref[k:k + 1, :] * xe[HALO - pad + k:HALO - pad + k + n, :]
    xc = _silu(acc)
    xc_ref[...] = xc
    xs = xc[:, :SSD_INNER]
    bmat = xc[:, SSD_INNER:SSD_INNER + LANES]
    cmat = xc[:, SSD_INNER + LANES:SSD_INNER + 2 * LANES]

    dt, log_a, cum_col, dt_row, la_row, cum_row = _ssd_decays(dt_ref[...], bias_ref[...], alog_ref[...])
    r = lax.broadcasted_iota(I32, (n, n), 0)
    cc = lax.broadcasted_iota(I32, (n, n), 1)
    causal = cc <= r
    last_col = cum_col[n - 1:n, :]

    def w_of(h):
        seg = cum_col[:, h:h + 1] - cum_row[h:h + 1, :]
        return jnp.exp(jnp.where(causal, seg, NEG_BIG)) * dt_row[h:h + 1, :]

    def q_scale_of(h):
        return jnp.exp(cum_col[:, h:h + 1])

    def k_scale_of(h):
        return dt_row[h:h + 1, :] * jnp.exp(cum_row[h:h + 1, n - 1:n] - cum_row[h:h + 1, :])

    def carry_of(h):
        return jnp.exp(last_col[:, h:h + 1])

    y_ref[...] = _ssd_scan_chunk(xs, bmat, cmat, w_of, q_scale_of, k_scale_of, carry_of, s_ref)

    @pl.when(c == nc - 1)
    def _():
        sfin_ref[0] = s_ref[...]


def _ssd_bwd_kernel(xc_ref, dt_ref, yf_ref, z_ref, s0_ref, bias_ref, alog_ref, dskip_ref, nw_ref,
                    y_ref, sfin_ref, s_ref):
    c = pl.program_id(1)
    nc = pl.num_programs(1)
    n = xc_ref.shape[0]

    @pl.when(c == 0)
    def _():
        s_ref[...] = s0_ref[0]

    xc = xc_ref[...]
    xs = xc[:, :SSD_INNER]
    bmat = xc[:, SSD_INNER:SSD_INNER + LANES]
    cmat = xc[:, SSD_INNER + LANES:SSD_INNER + 2 * LANES]
    dt, log_a, cum_col, dt_row, la_row, cum_row = _ssd_decays(dt_ref[...], bias_ref[...], alog_ref[...])
    ex_col = cum_col - log_a
    ex_row = cum_row - la_row
    r = lax.broadcasted_iota(I32, (n, n), 0)
    cc = lax.broadcasted_iota(I32, (n, n), 1)
    anti = cc >= r
    tot_col = cum_col[n - 1:n, :]
    off = SSD_HEADS

    def w_of(h):
        j = off + h
        seg = ex_row[j:j + 1, :] - ex_col[:, j:j + 1]
        return jnp.exp(jnp.where(anti, seg, NEG_BIG)) * dt_row[j:j + 1, :]

    def q_scale_of(h):
        j = off + h
        return jnp.exp(tot_col[:, j:j + 1] - ex_col[:, j:j + 1])

    def k_scale_of(h):
        j = off + h
        return dt_row[j:j + 1, :] * jnp.exp(ex_row[j:j + 1, :])

    def carry_of(h):
        j = off + h
        return jnp.exp(tot_col[:, j:j + 1])

    y_b = _ssd_scan_chunk(xs, bmat, cmat, w_of, q_scale_of, k_scale_of, carry_of, s_ref)
    y = yf_ref[...] + y_b + dskip_ref[...] * xs
    y = y * _silu(z_ref[...])
    ms = jnp.mean(y * y, axis=-1, keepdims=True)
    y_ref[...] = y * lax.rsqrt(ms + NORM_EPS) * nw_ref[...]

    @pl.when(c == nc - 1)
    def _():
        sfin_ref[0] = s_ref[...]


def _pair_states(s):
    b, h, n, p = s.shape
    return s.reshape(b, h // 2, 2, n, p).transpose(0, 1, 3, 2, 4).reshape(b, h // 2, n, 2 * p)


def _unpair_states(s):
    b, hp, n, p2 = s.shape
    return s.reshape(b, hp, n, 2, p2 // 2).transpose(0, 1, 3, 2, 4).reshape(b, hp * 2, n, p2 // 2)


def _ssd(xbc, dt, z, s0_f, s0_b, conv_w, conv_b, dt_bias, a_log, d_skip, ssd_norm, seq):
    m, nx = xbc.shape
    nb = m // seq
    blk = min(SSD_BLOCK, seq)
    nc = seq // blk
    hb = blk // HALO
    n_halo = m // HALO
    pad16 = lambda a: jnp.pad(a.reshape(1, -1), ((0, 0), (0, LANES - a.size)))
    bias = pad16(dt_bias)
    alog = pad16(a_log)
    state_spec = pl.BlockSpec((1, SSD_PAIRS, D_STATE, LANES), lambda b, c: (b, 0, 0, 0))
    state_shape = jax.ShapeDtypeStruct((nb, SSD_PAIRS, D_STATE, LANES), F32)
    row = lambda width: pl.BlockSpec((1, width), lambda b, c: (0, 0))

    def fwd_rows(width):
        return pl.BlockSpec((blk, width), lambda b, c: (b * nc + c, 0))

    def bwd_rows(width):
        return pl.BlockSpec((blk, width), lambda b, c: (b * nc + nc - 1 - c, 0))

    y_f, xc, s_f = pl.pallas_call(
        _ssd_fwd_kernel,
        grid=(nb, nc),
        in_specs=[fwd_rows(nx),
                  pl.BlockSpec((HALO, nx), lambda b, c: (jnp.maximum((b * nc + c) * hb - 1, 0), 0)),
                  pl.BlockSpec((HALO, nx), lambda b, c: (jnp.minimum((b * nc + c + 1) * hb, n_halo - 1), 0)),
                  fwd_rows(LANES), state_spec,
                  pl.BlockSpec((CONV_K, nx), lambda b, c: (0, 0)), row(nx), row(LANES), row(LANES)],
        out_specs=[fwd_rows(SSD_INNER), fwd_rows(nx), state_spec],
        out_shape=[jax.ShapeDtypeStruct((m, SSD_INNER), F32), jax.ShapeDtypeStruct((m, nx), F32), state_shape],
        scratch_shapes=[pltpu.VMEM((SSD_PAIRS, D_STATE, LANES), F32)],
        compiler_params=_cparams("arbitrary", "arbitrary"),
        name="ssd_forward",
    )(xbc, xbc, xbc, dt, _pair_states(s0_f), conv_w, conv_b.reshape(1, nx), bias, alog)

    dskip = jnp.repeat(d_skip, SSD_HEAD_DIM).reshape(1, SSD_INNER)
    y, s_b = pl.pallas_call(
        _ssd_bwd_kernel,
        grid=(nb, nc),
        in_specs=[bwd_rows(nx), bwd_rows(LANES), bwd_rows(SSD_INNER), bwd_rows(SSD_INNER), state_spec,
                  row(LANES), row(LANES), row(SSD_INNER), row(SSD_INNER)],
        out_specs=[bwd_rows(SSD_INNER), state_spec],
        out_shape=[jax.ShapeDtypeStruct((m, SSD_INNER), F32), state_shape],
        scratch_shapes=[pltpu.VMEM((SSD_PAIRS, D_STATE, LANES), F32)],
        compiler_params=_cparams("arbitrary", "arbitrary"),
        name="ssd_backward",
    )(xc, dt, y_f, z, _pair_states(s0_b), bias, alog, dskip, ssd_norm.reshape(1, SSD_INNER))
    return y, _unpair_states(s_f), _unpair_states(s_b)

def _hgrn_kernel(q_ref, ff_ref, fb_ref, i_ref, g_ref, lb_ref, s0_ref, nw_ref, o_ref, sfin_ref,
                 sf_ref, sb_ref, ob_ref, *, layer):
    t_len = q_ref.shape[0]
    n = HGRN_CHUNK
    n_chunks = t_len // n
    tn = (((0,), (0,)), ((), ()))
    nt = (((1,), (1,)), ((), ()))

    lbp = lb_ref[...]
    e = jnp.exp(lbp - jnp.max(lbp, axis=0, keepdims=True))
    sm = e / jnp.sum(e, axis=0, keepdims=True)
    lb = sm[0] * 0.0
    for j in range(1, layer + 1):
        lb = lb + sm[j]

    r = lax.broadcasted_iota(I32, (n, n), 0)
    c = lax.broadcasted_iota(I32, (n, n), 1)
    lower = jnp.where(c <= r, 1.0, 0.0).astype(F32)
    srow = lax.broadcasted_iota(I32, (n, HGRN_DK), 0)
    qscale = HGRN_DK ** -0.5

    def chunk(row0, f_ref, lb_d, reverse, s_ref):
        q = _silu(q_ref[pl.ds(row0, n), :]) * qscale
        f = f_ref[pl.ds(row0, n), :]
        v = i_ref[pl.ds(row0, n), :]
        k = (1.0 - lb_d) * jax.nn.sigmoid(-f)
        lf = jnp.log(lb_d + (1.0 - lb_d) * jax.nn.sigmoid(f))
        cum = jnp.dot(lower, lf, precision=HIGHEST, preferred_element_type=F32)
        tot = cum[n - 1:n, :]
        if reverse:
            cum = cum - lf
        rows = []
        for t in range(n):
            tile0 = (t // SUBLANES) * SUBLANES
            lo, hi = (tile0, n) if reverse else (0, tile0 + SUBLANES)
            cum_s = cum[lo:hi]
            if reverse:
                seg = jnp.where(srow[lo:hi] >= t, cum_s - cum[t:t + 1, :], NEG_BIG)
            else:
                seg = jnp.where(srow[lo:hi] <= t, cum[t:t + 1, :] - cum_s, NEG_BIG)
            a = q[t:t + 1, :] * k[lo:hi] * jnp.exp(seg)
            sc = jnp.sum(a, axis=1, keepdims=True)
            rows.append(jnp.sum(sc * v[lo:hi], axis=0, keepdims=True))
        o = jnp.concatenate(rows, axis=0)
        s_old = s_ref[...]
        if reverse:
            q_in = q * jnp.exp(tot - cum)
            k_out = k * jnp.exp(cum)
        else:
            q_in = q * jnp.exp(cum)
            k_out = k * jnp.exp(tot - cum)
        o = o + lax.dot_general(q_in.astype(BF16), s_old.astype(BF16), nt, preferred_element_type=F32)
        s_ref[...] = jnp.exp(tot) * s_old + lax.dot_general(
            v.astype(BF16), k_out.astype(BF16), tn, preferred_element_type=F32)
        return o

    sf_ref[...] = s0_ref[0, 0, 0].T
    sb_ref[...] = s0_ref[0, 1, 0].T

    def body(ci, carry):
        row_f = pl.multiple_of(ci * n, n)
        row_b = pl.multiple_of((n_chunks - 1 - ci) * n, n)
        o_ref[pl.ds(row_f, n), :] = chunk(row_f, ff_ref, lb[0:1, :], False, sf_ref)
        ob_ref[pl.ds(row_b, n), :] = chunk(row_b, fb_ref, lb[1:2, :], True, sb_ref)
        return carry

    lax.fori_loop(0, n_chunks, body, 0)
    sfin_ref[0, 0, 0] = sf_ref[...].T
    sfin_ref[0, 1, 0] = sb_ref[...].T

    nw = nw_ref[...]
    blk = min(t_len, ROW_TILE)

    def finish(bi, carry):
        row0 = pl.multiple_of(bi * blk, blk)
        o = o_ref[pl.ds(row0, blk), :] + ob_ref[pl.ds(row0, blk), :]
        ms = jnp.mean(o * o, axis=-1, keepdims=True)
        o = o * lax.rsqrt(ms + NORM_EPS) * nw
        o_ref[pl.ds(row0, blk), :] = o * _silu(g_ref[pl.ds(row0, blk), :])
        return carry

    lax.fori_loop(0, t_len // blk, finish, 0)


def _hgrn(q, f_fw, f_bw, iv, g, o_lb, state0, g_norm, seq, layer):
    m, width = q.shape
    nb = m // seq
    dv = width // HGRN_HEADS
    col = pl.BlockSpec((seq, dv), lambda b, h: (b, h))
    state_spec = pl.BlockSpec((1, 2, 1, HGRN_DK, dv), lambda b, h: (b, 0, h, 0, 0))
    return pl.pallas_call(
        functools.partial(_hgrn_kernel, layer=layer),
        grid=(nb, HGRN_HEADS),
        in_specs=[col, col, col, col, col,
                  pl.BlockSpec((o_lb.shape[0], 2, HGRN_DK), lambda b, h: (0, 0, h)),
                  state_spec,
                  pl.BlockSpec((1, dv), lambda b, h: (0, 0))],
        out_specs=[col, state_spec],
        out_shape=[jax.ShapeDtypeStruct((m, width), F32),
                   jax.ShapeDtypeStruct((nb, 2, HGRN_HEADS, HGRN_DK, dv), F32)],
        scratch_shapes=[pltpu.VMEM((dv, HGRN_DK), F32), pltpu.VMEM((dv, HGRN_DK), F32),
                        pltpu.VMEM((seq, dv), F32)],
        compiler_params=_cparams("arbitrary", "arbitrary"),
        name="hgrn2",
    )(q, f_fw, f_bw, iv, g, o_lb, state0, g_norm.reshape(1, dv))

EVEN_SPLITS = ((0, 512), (512, 640), (640, 768), (768, 1280), (1280, 2048), (2048, 2176))
HGRN_SPLITS = tuple((i * 1024, (i + 1) * 1024) for i in range(5))


def _even_weight(w):
    main = EVEN_SPLITS[-1][0]
    return jnp.pad(w, ((0, 0), (0, LANES - (w.shape[1] - main)))).astype(BF16)


def _run_trunk(x3, mods, mod_row0, P, cache, sc_order=None):
    nb, seq, d = x3.shape
    x = x3.reshape(nb * seq, d)
    depth = P['norm_mix'].shape[0]
    ks, vs, ssd_states, hgrn_states = [], [], [], []
    for l in range(depth):
        j = l // 2
        row0 = (l * SUBLANES + mod_row0, 0 if cache is None else 1)
        if l % 2 == 0:
            q, k, v, z, xbc, dt = _inproj(x, mods, row0, seq, P['norm_mix'][l], P['e_w_in'][j],
                                          EVEN_SPLITS, "even_in_proj")
            q, k = _qkprep(q, k, P['e_q_norm'][j], P['e_k_norm'][j], seq, rope=cache is not None)
            if cache is None:
                s0_f = jnp.zeros((nb, SSD_HEADS, D_STATE, SSD_HEAD_DIM), F32)
                s0_b = s0_f
                o_attn = _ctx_attention(q, k, v, P['e_sink'][j], seq)
            else:
                s0_f, s0_b = cache[2][:, j, 0], cache[2][:, j, 1]
                n_ctx = cache[0].shape[2]
                o_attn = _lat_attention(q, k, v, cache[0][:, j].reshape(nb, n_ctx, -1),
                                        cache[1][:, j].reshape(nb, n_ctx, -1), P['e_sink'][j], seq)
            y, s_f, s_b = _ssd(xbc, dt, z, s0_f, s0_b, P['e_conv_w'][j], P['e_conv_b'][j],
                               P['e_dt_bias'][j], P['e_a_log'][j], P['e_d_skip'][j], P['e_ssd_norm'][j], seq)
            if cache is None:
                ks.append(k.reshape(nb, seq, N_KV_HEADS, HEAD_DIM))
                vs.append(v.reshape(nb, seq, N_KV_HEADS, HEAD_DIM))
                ssd_states.append(jnp.stack([s_f, s_b], axis=1))
            mix = jnp.concatenate([o_attn, y], axis=1)
            x = _outproj(mix, x, mods, row0, seq, P['e_w_out'][j], "even_out_proj")
        else:
            q, f_fw, f_bw, iv, g = _inproj(x, mods, row0, seq, P['norm_mix'][l], P['o_w_in'][j],
                                           HGRN_SPLITS, "odd_in_proj")
            if cache is None:
                s0 = jnp.zeros((nb, 2, HGRN_HEADS, HGRN_DK, d // HGRN_HEADS), F32)
            else:
                s0 = cache[3][:, j]
            o, s_new = _hgrn(q, f_fw, f_bw, iv, g, P['o_lb'], s0, P['o_g_norm'][j], seq, j)
            if cache is None:
                hgrn_states.append(s_new)
            x = _outproj(o, x, mods, row0, seq, P['o_w_out'][j], "odd_out_proj")
        x, y_sc = _peer(x, mods, row0, seq, l, P['norm_ffn'][l], P['p_w_q'][l], P['p_sub_keys'][l],
                        P['p_u'], P['p_v'],
                        (SC_SHARE_CONTEXT if cache is None else SC_SHARE_LATENT)[l],
                        after=None if sc_order is None or cache is None else sc_order[l])
        if sc_order is not None and cache is None:
            sc_order.append(y_sc)
    y = x.reshape(nb, seq, d)
    if cache is not None:
        return y, None
    return y, (jnp.stack(ks, axis=1), jnp.stack(vs, axis=1),
               jnp.stack(ssd_states, axis=1), jnp.stack(hgrn_states, axis=1))


def kernel(x_prompt, x_sample, cache_k, cache_v, state_ssd, state_hgrn, c, c_ctx, w_ada, b_ada, norm_mix, norm_ffn, e_w_in, e_q_norm, e_k_norm, e_sink, e_conv_w, e_conv_b, e_dt_bias, e_a_log, e_d_skip, e_ssd_norm, e_w_out, o_w_in, o_lb, o_g_norm, o_w_out, p_w_q, p_sub_keys, p_u, p_v):
    depth, d, d6 = w_ada.shape
    b_lat = x_sample.shape[0]
    cond_rows = jnp.concatenate([c_ctx[None, :], c, jnp.zeros((SUBLANES - 1 - b_lat, d), F32)], axis=0)
    mods = _modulation(cond_rows, w_ada, b_ada).reshape(depth * SUBLANES, 1, d6)
    P = {
        'norm_mix': norm_mix, 'norm_ffn': norm_ffn,
        'e_w_in': jnp.stack([_even_weight(w) for w in e_w_in]), 'e_q_norm': e_q_norm, 'e_k_norm': e_k_norm,
        'e_sink': e_sink, 'e_conv_w': e_conv_w, 'e_conv_b': e_conv_b, 'e_dt_bias': e_dt_bias,
        'e_a_log': e_a_log, 'e_d_skip': e_d_skip, 'e_ssd_norm': e_ssd_norm,
        'e_w_out': e_w_out.astype(BF16),
        'o_w_in': o_w_in.astype(BF16), 'o_lb': o_lb, 'o_g_norm': o_g_norm, 'o_w_out': o_w_out.astype(BF16),
        'p_w_q': p_w_q.astype(BF16),
        'p_sub_keys': p_sub_keys.astype(BF16).reshape(depth, PEER_HEADS * 2, PEER_NKEYS, PEER_DKEY),
        'p_u': p_u, 'p_v': p_v,
    }
    sc_order = []
    y_prompt, new_state = _run_trunk(x_prompt, mods, 0, P, None, sc_order)
    y_sample, _ = _run_trunk(x_sample, mods, 1, P, (cache_k, cache_v, state_ssd, state_hgrn), sc_order)
    return (y_prompt, y_sample) + new_state
```
